```python
import jax, jax.numpy as jnp
from jax import lax
import numpy as np

D_MODEL = 2048
BATCH = 4
SEQ = 2048
DEPTH = 4
DEC_BATCH = 8
DEC_SEQ = 4
PAST_LEN = 16384
PAGE_SIZE = 128

N_A_LAYERS = DEPTH // 2
N_B_LAYERS = DEPTH - N_A_LAYERS
EXPAND = 2
C_A = EXPAND * D_MODEL
HEAD_A = 64
H_A = C_A // HEAD_A
LORA_DIM = 64
GN_EPS = 64e-5
C_B = EXPAND * D_MODEL
HEAD_B = 128
H_B = C_B // HEAD_B
G_KV = 4
HG = H_B // G_KV
L_CMP = 32
S_CMP = 16
L_SEL = 64
TOPK_SEL = 16
WINDOW = 512
Q_BLK_SEL = 32
Q_BLK_WIN = 128
D_PLE = 256
RMS_EPS = 1e-6
NEG = -1e30
FORCE_BONUS = 1e4

kernel_name = 'yoco_rwkv7_nsa_decode_step'


def rmsnorm(x, g):
    xf = x.astype(jnp.float32)
    y = xf * lax.rsqrt(jnp.mean(xf * xf, axis=-1, keepdims=True) + RMS_EPS)
    return (y * g.astype(jnp.float32)).astype(x.dtype)


def masked_softmax(s, mask):
    p = jax.nn.softmax(jnp.where(mask, s, NEG), axis=-1)
    return jnp.where(mask, p, 0.0)


def rwkv7_mixer(xn, x_prev, s0, mu, w_in, lw1, lw2, w0, la1, la2, a0, k_k, k_a, r_k, ln_w, ln_b, w_out):
    B, T, _ = xn.shape
    f32 = jnp.float32
    x_shift = jnp.concatenate([x_prev[:, None, :], xn[:, :-1]], axis=1)
    xm = xn[:, :, None, :] + (x_shift - xn)[:, :, None, :] * mu
    rkvg = jnp.einsum('btjd,jdc->btjc', xm[:, :, :4], w_in).astype(f32)
    r, k, v, zg = rkvg[:, :, 0], rkvg[:, :, 1], rkvg[:, :, 2], rkvg[:, :, 3]
    w_log = -jax.nn.softplus(-(w0 + jnp.tanh(xm[:, :, 4] @ lw1) @ lw2).astype(f32)) - 0.5
    decay = jnp.exp(-jnp.exp(w_log))
    a = jax.nn.sigmoid((a0 + (xm[:, :, 5] @ la1) @ la2).astype(f32))
    heads = lambda z: z.reshape(B, T, H_A, HEAD_A)
    kk = heads(k * k_k.astype(f32))
    kk = kk / jnp.maximum(jnp.sqrt(jnp.sum(kk * kk, axis=-1, keepdims=True)), 1e-12)
    k = k * (1.0 + (a - 1.0) * k_a.astype(f32))
    r, decay, k, v, a = heads(r), heads(decay), heads(k), heads(v), heads(a)
    aa, bb = -kk, kk * a

    def step(S, inp):
        r_t, w_t, k_t, v_t, a_t, b_t = inp
        S = (S * w_t[:, :, None, :]
             + jnp.einsum('bhvk,bhk->bhv', S, a_t)[..., None] * b_t[:, :, None, :]
             + v_t[..., None] * k_t[:, :, None, :])
        return S, jnp.einsum('bhvk,bhk->bhv', S, r_t)

    seq = tuple(jnp.moveaxis(z, 1, 0) for z in (r, decay, k, v, aa, bb))
    S_fin, o = lax.scan(step, s0.astype(f32), seq)
    o = jnp.moveaxis(o, 0, 1)
    mean = jnp.mean(o, axis=-1, keepdims=True)
    var = jnp.mean(jnp.square(o - mean), axis=-1, keepdims=True)
    o = ((o - mean) * lax.rsqrt(var + GN_EPS)).reshape(B, T, C_A) * ln_w.astype(f32) + ln_b.astype(f32)
    bonus = (jnp.sum(r * k * r_k.astype(f32), axis=-1, keepdims=True) * v).reshape(B, T, C_A)
    o = (o + bonus) * jax.nn.silu(zg)
    return o.astype(xn.dtype) @ w_out, S_fin.astype(s0.dtype), xn[:, -1]


def compress(x, pe, w1, w2):
    B, T = x.shape[0], x.shape[1]
    nc = (T - L_CMP) // S_CMP + 1
    rr = L_CMP // S_CMP
    chunks = x[:, :(nc + rr - 1) * S_CMP].reshape(B, nc + rr - 1, S_CMP, G_KV, HEAD_B)
    blocks = jnp.concatenate([chunks[:, s:s + nc] for s in range(rr)], axis=2)
    blocks = blocks + pe[None, None, :, None, :]
    flat = jnp.swapaxes(blocks, 2, 3).reshape(B, nc, G_KV, L_CMP * HEAD_B)
    return jax.nn.gelu(flat @ w1) @ w2


def build_sparse_side(rows, pe_cmp, w_cmp1, w_cmp2):
    B, T = rows.shape[0], rows.shape[1]
    kc = compress(rows[:, :, 0], pe_cmp[0], w_cmp1[0], w_cmp2[0])
    vc = compress(rows[:, :, 1], pe_cmp[1], w_cmp1[1], w_cmp2[1])
    nc = kc.shape[1]
    cend = S_CMP * jnp.arange(nc, dtype=jnp.int32) + (L_CMP - 1)
    nsb = max(-(-T // L_SEL), TOPK_SEL)
    sel = jnp.pad(rows[:, :, 2:4], ((0, 0), (0, nsb * L_SEL - T), (0, 0), (0, 0), (0, 0)))
    sel = sel.reshape(B, nsb, L_SEL, 2, G_KV, HEAD_B).transpose(3, 0, 4, 1, 2, 5)
    ci = jnp.arange(nc, dtype=jnp.int32)[:, None]
    sj = jnp.arange(nsb, dtype=jnp.int32)[None, :]
    overlap = ((S_CMP * ci < L_SEL * (sj + 1)) & (S_CMP * ci + L_CMP > L_SEL * sj)).astype(jnp.float32)
    return kc, vc, cend, overlap, sel[0], sel[1]


def sparse_block(q, qpos, kc, vc, cend, overlap, ksb, vsb, slopes):
    B, Q = q.shape[0], q.shape[1]
    f32 = jnp.float32
    dist_c = (qpos[:, None] - cend[None, :]).astype(f32)
    s_c = jnp.einsum('bqghd,bcgd->bqghc', q, kc).astype(f32) - slopes[None, None, :, :, None] * dist_c[None, :, None, None, :]
    mask_c = (cend[None, :] <= qpos[:, None])[None, :, None, None, :]
    p_c = masked_softmax(s_c, mask_c)
    o_cmp = jnp.einsum('bqghc,bcgd->bqghd', p_c.astype(vc.dtype), vc)
    imp = jnp.einsum('bqghc,cj->bqgj', p_c, overlap)
    nsb = overlap.shape[1]
    j = jnp.arange(nsb, dtype=jnp.int32)[None, :]
    cur = (qpos // L_SEL)[:, None]
    valid = j <= cur
    forced = ((j == 0) | (j == cur) | (j == cur - 1)).astype(f32)
    score = jnp.where(valid[None, :, None, :], imp + FORCE_BONUS * forced[None, :, None, :], NEG)
    top_val, top_idx = lax.top_k(score, TOPK_SEL)
    sel_ok = top_val > 0.5 * NEG
    bi = jnp.arange(B)[:, None, None, None]
    gi = jnp.arange(G_KV)[None, None, :, None]
    ks = ksb[bi, gi, top_idx]
    vs = vsb[bi, gi, top_idx]
    kpos = top_idx[..., None] * L_SEL + jnp.arange(L_SEL, dtype=jnp.int32)
    dist_s = qpos[None, :, None, None, None] - kpos
    mask_s = sel_ok[..., None] & (dist_s >= 0)
    s_s = (jnp.einsum('bqghd,bqgkld->bqghkl', q, ks).astype(f32)
           - slopes[None, None, :, :, None, None] * dist_s[:, :, :, None].astype(f32))
    shp = (B, Q, G_KV, HG, TOPK_SEL * L_SEL)
    p_s = masked_softmax(s_s.reshape(shp), jnp.broadcast_to(mask_s[:, :, :, None], s_s.shape).reshape(shp))
    o_sel = jnp.einsum('bqghkl,bqgkld->bqghd', p_s.reshape(s_s.shape).astype(vs.dtype), vs)
    return o_cmp, o_sel


def window_block(q, qpos, kwp, vwp, kpos0, slopes):
    Q = q.shape[1]
    start = qpos[0] - kpos0
    ks = lax.dynamic_slice_in_dim(kwp, start, WINDOW + Q, axis=1)
    vs = lax.dynamic_slice_in_dim(vwp, start, WINDOW + Q, axis=1)
    kpos = qpos[0] - WINDOW + jnp.arange(WINDOW + Q, dtype=jnp.int32)
    dist = qpos[:, None] - kpos[None, :]
    mask = (kpos[None, :] >= kpos0) & (dist >= 0) & (dist < WINDOW)
    s = (jnp.einsum('bqghd,bkgd->bqghk', q, ks).astype(jnp.float32)
         - slopes[None, None, :, :, None] * dist.astype(jnp.float32)[None, :, None, None, :])
    p = masked_softmax(s, mask[None, :, None, None, :])
    return jnp.einsum('bqghk,bkgd->bqghd', p.astype(vs.dtype), vs)


def over_query_blocks(fn, q, qpos, qb):
    B, T = q.shape[0], q.shape[1]
    if T <= qb or T % qb:
        return fn(q, qpos)
    nq = T // qb
    qs = jnp.swapaxes(q.reshape((B, nq, qb) + q.shape[2:]), 0, 1)
    ps = qpos.reshape(nq, qb)
    outs = lax.map(lambda a: fn(a[0], a[1]), (qs, ps))
    return tuple(jnp.swapaxes(o, 0, 1).reshape((B, T) + o.shape[3:]) for o in outs)


def nsa_mixer(hn, pos0, shared, w_in, w_out, slopes):
    B, T, _ = hn.shape
    f32 = jnp.float32
    kc, vc, cend, overlap, ksb, vsb, kwp, vwp, kw_pos0 = shared
    proj = hn @ w_in
    q = proj[..., :C_B].reshape(B, T, G_KV, HG, HEAD_B) * (HEAD_B ** -0.5)
    z = proj[..., C_B:4 * C_B].reshape(B, T, 3, H_B, HEAD_B).astype(f32)
    gate = jax.nn.sigmoid(proj[..., 4 * C_B:].astype(f32)).reshape(B, T, 3, H_B, 1)
    qpos = pos0 + jnp.arange(T, dtype=jnp.int32)
    o_cmp, o_sel = over_query_blocks(
        lambda qb, pb: sparse_block(qb, pb, kc, vc, cend, overlap, ksb, vsb, slopes), q, qpos, Q_BLK_SEL)
    (o_win,) = over_query_blocks(
        lambda qb, pb: (window_block(qb, pb, kwp, vwp, kw_pos0, slopes),), q, qpos, Q_BLK_WIN)
    o = jnp.stack([o_cmp, o_sel, o_win], axis=2).reshape(B, T, 3, H_B, HEAD_B).astype(f32)
    o = jnp.sum(gate * o * jax.nn.silu(z), axis=2)
    return o.reshape(B, T, C_B).astype(hn.dtype) @ w_out


def trunk(x, p, pos0, wkv0, shift0, past_sparse, past_win, slopes, weights):
    (norm_g, mu_a, w_in_a, w_lora_w1, w_lora_w2, w0_a, a_lora1, a_lora2, a0_a, k_k, k_a, r_k,
     ln_x_w, ln_x_b, w_out_a, w_in_b, w_out_b, kv_norm_g, w_kv, pe_cmp, w_cmp1, w_cmp2,
     w_ple, w_ple_gate, final_norm_g) = weights
    B, T, _ = x.shape
    h = x
    wkv_new, shift_new = [], []
    shared, kv_rows, win_state = None, None, None
    for i in range(DEPTH):
        hn = rmsnorm(h, norm_g[i])
        if i < N_A_LAYERS:
            out, s_fin, last = rwkv7_mixer(hn, shift0[i], wkv0[i], mu_a[i], w_in_a[i], w_lora_w1[i], w_lora_w2[i],
                                           w0_a[i], a_lora1[i], a_lora2[i], a0_a[i], k_k[i], k_a[i], r_k[i],
                                           ln_x_w[i], ln_x_b[i], w_out_a[i])
            wkv_new.append(s_fin)
            shift_new.append(last)
        else:
            jb = i - N_A_LAYERS
            out = nsa_mixer(hn, pos0, shared, w_in_b[jb], w_out_b[jb], slopes)
        h = h + out
        h = h + (p[i] @ w_ple[i]) * jax.nn.sigmoid(h @ w_ple_gate[i])
        if i == N_A_LAYERS - 1:
            rows = (rmsnorm(h, kv_norm_g) @ w_kv).reshape(B, T, 6, G_KV, HEAD_B)
            kv_rows = rows[:, :, :4]
            if past_sparse is None:
                sparse_all, win_all, kw_pos0 = kv_rows, rows[:, :, 4:], pos0
            else:
                sparse_all = jnp.concatenate([past_sparse, kv_rows], axis=1)
                win_all = jnp.concatenate([past_win, rows[:, :, 4:]], axis=1)
                kw_pos0 = pos0 - past_win.shape[1]
            win_state = win_all[:, win_all.shape[1] - min(WINDOW, pos0 + T):]
            wpad = jnp.pad(win_all, ((0, 0), (WINDOW, 0), (0, 0), (0, 0), (0, 0)))
            shared = build_sparse_side(sparse_all, pe_cmp, w_cmp1, w_cmp2) + (wpad[:, :, 0], wpad[:, :, 1], kw_pos0)
    y = rmsnorm(h, final_norm_g)
    return y, jnp.stack(wkv_new), jnp.stack(shift_new), kv_rows, win_state


def setup_inputs(seed: int = 0) -> dict:
    key = jax.random.key(seed)
    ks = jax.random.split(key, 40)
    f32 = jnp.float32
    nrm = lambda k, shape, scale: scale * jax.random.normal(k, shape, f32)
    n_pages = PAST_LEN // PAGE_SIZE
    n_pool = (5 * DEC_BATCH * n_pages + 3) // 4
    w_buf = min(WINDOW, PAST_LEN)
    page_table = jax.random.permutation(ks[0], n_pool)[:DEC_BATCH * n_pages].reshape(DEC_BATCH, n_pages).astype(jnp.int32)
    return {
        'x_prompt': nrm(ks[1], (BATCH, SEQ, D_MODEL), 1.0),
        'x_sample': nrm(ks[2], (DEC_BATCH, DEC_SEQ, D_MODEL), 1.0),
        'state_wkv': nrm(ks[3], (N_A_LAYERS, DEC_BATCH, H_A, HEAD_A, HEAD_A), 0.1),
        'state_shift': nrm(ks[4], (N_A_LAYERS, DEC_BATCH, D_MODEL), 1.0),
        'cache_kv': nrm(ks[5], (n_pool, PAGE_SIZE, 4, G_KV, HEAD_B), 1.0),
        'cache_win_kv': nrm(ks[6], (DEC_BATCH, w_buf, 2, G_KV, HEAD_B), 1.0),
        'page_table': page_table,
        'p_prompt': nrm(ks[7], (DEPTH, BATCH, SEQ, D_PLE), 1.0),
        'p_sample': nrm(ks[8], (DEPTH, DEC_BATCH, DEC_SEQ, D_PLE), 1.0),
        'norm_g': 1.0 + nrm(ks[9], (DEPTH, D_MODEL), 0.02),
        'mu_a': jax.random.uniform(ks[10], (N_A_LAYERS, 6, D_MODEL), f32),
        'w_in_a': nrm(ks[11], (N_A_LAYERS, 4, D_MODEL, C_A), D_MODEL ** -0.5),
        'w_lora_w1': nrm(ks[12], (N_A_LAYERS, D_MODEL, LORA_DIM), D_MODEL ** -0.5),
        'w_lora_w2': nrm(ks[13], (N_A_LAYERS, LORA_DIM, C_A), 0.1 * LORA_DIM ** -0.5),
        'w0_a': jax.random.uniform(ks[14], (N_A_LAYERS, C_A), f32, -6.0, 1.0),
        'a_lora1': nrm(ks[15], (N_A_LAYERS, D_MODEL, LORA_DIM), D_MODEL ** -0.5),
        'a_lora2': nrm(ks[16], (N_A_LAYERS, LORA_DIM, C_A), 0.1 * LORA_DIM ** -0.5),
        'a0_a': nrm(ks[17], (N_A_LAYERS, C_A), 0.1),
        'k_k': 0.85 + nrm(ks[18], (N_A_LAYERS, C_A), 0.05),
        'k_a': 1.0 + nrm(ks[19], (N_A_LAYERS, C_A), 0.05),
        'r_k': nrm(ks[20], (N_A_LAYERS, H_A, HEAD_A), 0.1),
        'ln_x_w': 1.0 + nrm(ks[21], (N_A_LAYERS, C_A), 0.02),
        'ln_x_b': nrm(ks[22], (N_A_LAYERS, C_A), 0.01),
        'w_out_a': nrm(ks[23], (N_A_LAYERS, C_A, D_MODEL), C_A ** -0.5),
        'w_in_b': nrm(ks[24], (N_B_LAYERS, D_MODEL, 4 * C_B + 3 * H_B), D_MODEL ** -0.5),
        'w_out_b': nrm(ks[25], (N_B_LAYERS, C_B, D_MODEL), C_B ** -0.5),
        'kv_norm_g': 1.0 + nrm(ks[26], (D_MODEL,), 0.02),
        'w_kv': nrm(ks[27], (D_MODEL, 6 * G_KV * HEAD_B), D_MODEL ** -0.5),
        'pe_cmp': nrm(ks[28], (2, L_CMP, HEAD_B), 0.1),
        'w_cmp1': nrm(ks[29], (2, L_CMP * HEAD_B, HEAD_B), (L_CMP * HEAD_B) ** -0.5),
        'w_cmp2': nrm(ks[30], (2, HEAD_B, HEAD_B), HEAD_B ** -0.5),
        'w_ple': nrm(ks[31], (DEPTH, D_PLE, D_MODEL), D_PLE ** -0.5),
        'w_ple_gate': nrm(ks[32], (DEPTH, D_MODEL, D_MODEL), D_MODEL ** -0.5),
        'final_norm_g': 1.0 + nrm(ks[33], (D_MODEL,), 0.02),
    }


def reference(x_prompt, x_sample, state_wkv, state_shift, cache_kv, cache_win_kv, page_table, p_prompt, p_sample,
              norm_g, mu_a, w_in_a, w_lora_w1, w_lora_w2, w0_a, a_lora1, a_lora2, a0_a, k_k, k_a, r_k,
              ln_x_w, ln_x_b, w_out_a, w_in_b, w_out_b, kv_norm_g, w_kv, pe_cmp, w_cmp1, w_cmp2,
              w_ple, w_ple_gate, final_norm_g):
    weights = (norm_g, mu_a, w_in_a, w_lora_w1, w_lora_w2, w0_a, a_lora1, a_lora2, a0_a, k_k, k_a, r_k,
               ln_x_w, ln_x_b, w_out_a, w_in_b, w_out_b, kv_norm_g, w_kv, pe_cmp, w_cmp1, w_cmp2,
               w_ple, w_ple_gate, final_norm_g)
    slopes = (2.0 ** (-8.0 * jnp.arange(1, H_B + 1, dtype=jnp.float32) / H_B)).reshape(G_KV, HG)
    bp = x_prompt.shape[0]
    wkv0 = jnp.zeros((N_A_LAYERS, bp, H_A, HEAD_A, HEAD_A), x_prompt.dtype)
    shift0 = jnp.zeros((N_A_LAYERS, bp, D_MODEL), x_prompt.dtype)
    y_prompt, wkv_p, shift_p, kv_rows_p, win_p = trunk(
        x_prompt, p_prompt, 0, wkv0, shift0, None, None, slopes, weights)
    db, n_pages = page_table.shape
    past_len = n_pages * cache_kv.shape[1]
    past_kv = cache_kv[page_table].reshape((db, past_len) + cache_kv.shape[2:])
    y_sample, wkv_s, shift_s, kv_rows_s, win_s = trunk(
        x_sample, p_sample, past_len, state_wkv, state_shift, past_kv, cache_win_kv, slopes, weights)
    return (y_prompt, y_sample, wkv_p, shift_p, kv_rows_p, win_p, wkv_s, shift_s, kv_rows_s, win_s)
```

```python
import functools
import math

import jax
import jax.numpy as jnp
from jax import lax
from jax.experimental import pallas as pl
from jax.experimental.pallas import tpu as pltpu

F32 = jnp.float32
BF16 = jnp.bfloat16

HEAD_A = 64
GN_EPS = 64e-5
HEAD_B = 128
G_KV = 4
L_CMP = 32
S_CMP = 16
L_SEL = 64
TOPK_SEL = 16
WINDOW = 512
RMS_EPS = 1e-6
NEG = -1e30
FORCE_BONUS = 1e4

LANES = 128
SUBLANES = 8
VMEM_LIMIT = 56 * 1024 * 1024

SCAN_NH = 2
SCAN_C = 64
SCAN_GP = 2

NT_DIMS = (((1,), (1,)), ((), ()))
TN_DIMS = (((0,), (0,)), ((), ()))


def _cparams(sem):
    return pltpu.CompilerParams(dimension_semantics=sem, vmem_limit_bytes=VMEM_LIMIT)


def _bdot(a, b):
    return jnp.dot(a.astype(BF16), b.astype(BF16), preferred_element_type=F32)


def _bdot_nt(a, b):
    return lax.dot_general(a.astype(BF16), b.astype(BF16), NT_DIMS, preferred_element_type=F32)


def _bdot_tn(a, b):
    return lax.dot_general(a.astype(BF16), b.astype(BF16), TN_DIMS, preferred_element_type=F32)


def _rms_kernel(x_ref, g_ref, o_ref):
    x = x_ref[...]
    ms = jnp.mean(x * x, axis=-1, keepdims=True)
    o_ref[...] = x * lax.rsqrt(ms + RMS_EPS) * g_ref[...]


def rmsnorm(x, g):
    M, D = x.shape
    tm = min(M, 256)
    return pl.pallas_call(
        _rms_kernel,
        grid=(pl.cdiv(M, tm),),
        in_specs=[pl.BlockSpec((tm, D), lambda i: (i, 0)),
                  pl.BlockSpec((1, D), lambda i: (0, 0))],
        out_specs=pl.BlockSpec((tm, D), lambda i: (i, 0)),
        out_shape=jax.ShapeDtypeStruct((M, D), F32),
        compiler_params=_cparams(("parallel",)),
        name="rmsnorm",
    )(x, g.reshape(1, D))


def _mm_kernel(x_ref, w_ref, o_ref):
    o_ref[...] = jnp.dot(x_ref[...].astype(BF16), w_ref[...], preferred_element_type=F32)


def mm(x, w, widx=()):
    M, K = x.shape
    N = w.shape[-1]
    assert w.shape[-2] == K and len(widx) == w.ndim - 2
    tm = min(M, 1024 if K <= 2048 else 512)
    tn = N if N <= 512 else 512
    nlead = len(widx)
    w_spec = pl.BlockSpec((None,) * nlead + (K, tn), lambda i, j: tuple(widx) + (0, j))
    return pl.pallas_call(
        _mm_kernel,
        grid=(pl.cdiv(M, tm), pl.cdiv(N, tn)),
        in_specs=[pl.BlockSpec((tm, K), lambda i, j: (i, 0)), w_spec],
        out_specs=pl.BlockSpec((tm, tn), lambda i, j: (i, j)),
        out_shape=jax.ShapeDtypeStruct((M, N), F32),
        compiler_params=_cparams(("parallel", "parallel")),
        name="mm",
    )(x, w)


def _scan_kernel(r_ref, lw_ref, k_ref, v_ref, a_ref, b_ref, s0_ref, o_ref, sfin_ref, s_scr, *, C, NH, GP):
    L = NH * HEAD_A
    NC = NH * C
    ci = pl.program_id(2)

    @pl.when(ci == 0)
    def _():
        s_scr[...] = s0_ref[0]

    row_c = lax.broadcasted_iota(jnp.int32, (C, NC), 0)
    col_s = lax.broadcasted_iota(jnp.int32, (C, NC), 1) % C
    tri_strict = col_s < row_c
    tri_incl = col_s <= row_c
    st_mask = (lax.broadcasted_iota(jnp.int32, (NC, L), 0) // C
               == lax.broadcasted_iota(jnp.int32, (NC, L), 1) // HEAD_A)
    bd_mask = (lax.broadcasted_iota(jnp.int32, (NC, NC), 0) // C
               == lax.broadcasted_iota(jnp.int32, (NC, NC), 1) // C)
    head_mask = (lax.broadcasted_iota(jnp.int32, (L, L), 0) // HEAD_A
                 == lax.broadcasted_iota(jnp.int32, (L, L), 1) // HEAD_A)
    cum_mat = (lax.broadcasted_iota(jnp.int32, (C, C), 1)
               <= lax.broadcasted_iota(jnp.int32, (C, C), 0)).astype(F32)

    def st(x):
        return jnp.where(st_mask, jnp.concatenate([x] * NH, axis=0), 0.0)

    def bd(w):
        return jnp.where(bd_mask, jnp.concatenate([w] * NH, axis=0), 0.0)

    n_double = int(math.log2(C))
    for gp in range(GP):
        sl = slice(gp * L, (gp + 1) * L)
        lw = lw_ref[0, :, sl]
        cum = jnp.dot(cum_mat, lw, precision=lax.Precision.HIGHEST, preferred_element_type=F32)
        p_incl = jnp.exp(cum)
        p_inv = jnp.exp(-cum)
        at = a_ref[0, :, sl] * jnp.exp(cum - lw)
        rt = r_ref[0, :, sl] * p_incl
        bt = b_ref[0, :, sl] * p_inv
        kt = k_ref[0, :, sl] * p_inv
        v = v_ref[0, :, sl]
        S = s_scr[gp]

        ar = jnp.concatenate([at, rt], axis=0)
        bk_st = jnp.concatenate([st(bt), st(kt)], axis=0)
        Gm = _bdot_nt(ar, bk_st)
        LH = _bdot_nt(ar, S)
        w_ab = jnp.where(tri_strict, Gm[:C, :NC], 0.0)
        w_ak = jnp.where(tri_strict, Gm[:C, NC:], 0.0)
        w_rb = jnp.where(tri_incl, Gm[C:, :NC], 0.0)
        w_rk = jnp.where(tri_incl, Gm[C:, NC:], 0.0)
        v_st = st(v)
        x = LH[:C] + _bdot(w_ak, v_st)
        powers = [w_ab]
        for _ in range(n_double - 1):
            powers.append(_bdot(powers[-1], bd(powers[-1])))
        for wp in reversed(powers):
            x = x + _bdot(wp, st(x))
        u = x
        o = LH[C:] + _bdot(jnp.concatenate([w_rb, w_rk], axis=1),
                           jnp.concatenate([st(u), v_st], axis=0))
        o_ref[0, :, sl] = o
        ds = _bdot_tn(jnp.concatenate([u, v], axis=0), jnp.concatenate([bt, kt], axis=0))
        s_scr[gp] = (S + jnp.where(head_mask, ds, 0.0)) * p_incl[C - 1:C, :]

    @pl.when(ci == pl.num_programs(2) - 1)
    def _():
        sfin_ref[0] = s_scr[...]


def rwkv_scan(r, lw, k, v, a, b, s0):
    B, T, CA = r.shape
    H = CA // HEAD_A
    NH, C, GP = SCAN_NH, SCAN_C, SCAN_GP
    assert NH * C == LANES and H % (NH * GP) == 0
    L = NH * HEAD_A
    NG = H // NH
    Tp = -(-T // C) * C
    if Tp != T:
        pad = lambda z: jnp.pad(z, ((0, 0), (0, Tp - T), (0, 0)))
        r, lw, k, v, a, b = (pad(z) for z in (r, lw, k, v, a, b))
    eye = jnp.eye(NH, dtype=F32)
    s0_bd = (s0.reshape(B, NG, NH, HEAD_A, 1, HEAD_A) * eye[None, None, :, None, :, None]).reshape(B, NG, L, L)
    seq_spec = pl.BlockSpec((1, C, GP * L), lambda bi, gi, ci: (bi, ci, gi))
    st_spec = pl.BlockSpec((1, GP, L, L), lambda bi, gi, ci: (bi, gi, 0, 0))
    o, sfin = pl.pallas_call(
        functools.partial(_scan_kernel, C=C, NH=NH, GP=GP),
        grid=(B, NG // GP, Tp // C),
        in_specs=[seq_spec] * 6 + [st_spec],
        out_specs=[seq_spec, st_spec],
        out_shape=[jax.ShapeDtypeStruct((B, Tp, CA), F32), jax.ShapeDtypeStruct((B, NG, L, L), F32)],
        scratch_shapes=[pltpu.VMEM((GP, L, L), F32)],
        compiler_params=_cparams(("parallel", "parallel", "arbitrary")),
        name="rwkv_scan",
    )(r, lw, k, v, a, b, s0_bd)
    sf = sfin.reshape(B, NG, NH, HEAD_A, NH, HEAD_A)
    s_fin = jnp.stack([sf[:, :, h, :, h, :] for h in range(NH)], axis=2).reshape(B, H, HEAD_A, HEAD_A)
    return o[:, :T], s_fin


def _gelu_tanh(x):
    c = math.sqrt(2.0 / math.pi)
    return 0.5 * x * (1.0 + jnp.tanh(c * (x + 0.044715 * (x * x * x))))


def _compress_kernel(pt_ref, *refs, PGS):
    del pt_ref
    page_refs = refs[:PGS]
    next_ref, pe_ref, w1_ref, w2_ref, out_ref = refs[PGS:]
    CPP = page_refs[0].shape[1]
    NCH = PGS * CPP
    CG = 2 * G_KV
    M = (NCH + 1) * CG

    def rows_of(l, hf):
        pe = pe_ref[hf, l]
        parts = [(page_refs[i][0, :, l] + pe[None]).reshape(CPP * CG, HEAD_B) for i in range(PGS)]
        parts.append(next_ref[0, 0, l] + pe)
        return jnp.concatenate(parts, axis=0)

    top = jnp.zeros((M, 2 * HEAD_B), F32)
    bot = jnp.zeros((M, 2 * HEAD_B), F32)
    for l in range(0, S_CMP, 2):
        wrows = pl.ds(l * HEAD_B, 2 * HEAD_B)
        xt = jnp.concatenate([rows_of(l, 0), rows_of(l + 1, 0)], axis=1).astype(BF16)
        top = top + jnp.dot(xt, w1_ref[0, wrows, :], preferred_element_type=F32)
        xb = jnp.concatenate([rows_of(l, 1), rows_of(l + 1, 1)], axis=1).astype(BF16)
        bot = bot + jnp.dot(xb, w1_ref[1, wrows, :], preferred_element_type=F32)
    is_k = (lax.broadcasted_iota(jnp.int32, (M, 1), 0) % CG) < G_KV
    pick = lambda z, n: jnp.where(is_k[:n], z[:n, :HEAD_B], z[:n, HEAD_B:])
    hcur = pick(top, NCH * CG) + pick(bot, M)[CG:]
    o2 = jnp.dot(_gelu_tanh(hcur).astype(BF16), w2_ref[...], preferred_element_type=F32)
    out_ref[0] = pick(o2, NCH * CG).reshape(NCH, CG, HEAD_B)


def compress_kv(pool, table, pe_cmp, w1, w2):
    NP, PS, W = pool.shape
    B, n_pages = table.shape
    PGS = max(d for d in (8, 4, 2, 1) if n_pages % d == 0)
    CPP = PS // S_CMP
    NCH = PGS * CPP
    CG = 2 * G_KV
    half = S_CMP * HEAD_B
    pool5 = pool.reshape(NP, CPP, S_CMP, W // HEAD_B, HEAD_B)
    pe_r = jnp.repeat(pe_cmp.reshape(2, 2, S_CMP, HEAD_B).transpose(1, 2, 0, 3), G_KV, axis=2)
    w1_r = w1.reshape(2, 2, half, HEAD_B).transpose(1, 2, 0, 3).reshape(2, half, 2 * HEAD_B)
    w2_r = jnp.concatenate([w2[0], w2[1]], axis=1)

    def page_map(i):
        return lambda b, s, pt: (pt[b, s * PGS + i], 0, 0, 0, 0)

    def next_map(b, s, pt):
        return (pt[b, jnp.minimum((s + 1) * PGS, n_pages - 1)], 0, 0, 0, 0)

    const = lambda n: (lambda b, s, pt: (0,) * n)
    grid_spec = pltpu.PrefetchScalarGridSpec(
        num_scalar_prefetch=1,
        grid=(B, n_pages // PGS),
        in_specs=[pl.BlockSpec((1, CPP, S_CMP, CG, HEAD_B), page_map(i)) for i in range(PGS)] + [
            pl.BlockSpec((1, 1, S_CMP, CG, HEAD_B), next_map),
            pl.BlockSpec((2, S_CMP, CG, HEAD_B), const(4)),
            pl.BlockSpec((2, half, 2 * HEAD_B), const(3)),
            pl.BlockSpec((HEAD_B, 2 * HEAD_B), const(2)),
        ],
        out_specs=pl.BlockSpec((1, NCH, CG, HEAD_B), lambda b, s, pt: (b, s, 0, 0)),
    )
    return pl.pallas_call(
        functools.partial(_compress_kernel, PGS=PGS),
        grid_spec=grid_spec,
        out_shape=jax.ShapeDtypeStruct((B, n_pages * CPP, CG, HEAD_B), F32),
        compiler_params=_cparams(("parallel", "arbitrary")),
        name="compress_kv",
    )(table, *([pool5] * PGS), pool5, pe_r, w1_r, w2_r)


def _stack_heads(q, HG):
    return jnp.concatenate([q[:, h * HEAD_B:(h + 1) * HEAD_B] for h in range(HG)], axis=0)


def _masked_softmax_rows(s, mask):
    s = jnp.where(mask, s, NEG)
    m = jnp.max(s, axis=-1, keepdims=True)
    e = jnp.where(mask, jnp.exp(s - m), 0.0)
    l = jnp.sum(e, axis=-1, keepdims=True)
    return e / jnp.where(l > 0.0, l, 1.0)


def _attend_stacked(s, dist, mask, v, slopes_ref, g, HG, tq):
    ps = []
    psum = jnp.zeros(dist.shape, F32)
    for h in range(HG):
        p = _masked_softmax_rows(s[h * tq:(h + 1) * tq] - slopes_ref[g * HG + h] * dist, mask)
        psum = psum + p
        ps.append(p.astype(BF16))
    o = jnp.dot(jnp.concatenate(ps, axis=0), v.astype(BF16), preferred_element_type=F32)
    return o, psum


def _unstack_store(o_ref, o, HG, tq):
    for h in range(HG):
        o_ref[0, :, h * HEAD_B:(h + 1) * HEAD_B] = o[h * tq:(h + 1) * tq]


def _nsa_cmp_kernel(slopes_ref, q_ref, kc_ref, vc_ref, o_ref, selm_ref, *, tq, HG, nc, nsb, pos0):
    g = pl.program_id(1)
    qt = pl.program_id(2)
    NCp = kc_ref.shape[1]
    NSBp = selm_ref.shape[3]
    q_st = _stack_heads(q_ref[0] * (HEAD_B ** -0.5), HG)
    s = _bdot_nt(q_st, kc_ref[0])
    qpos = pos0 + qt * tq + lax.broadcasted_iota(jnp.int32, (tq, 1), 0)
    cidx = lax.broadcasted_iota(jnp.int32, (1, NCp), 1)
    cend = S_CMP * cidx + (L_CMP - 1)
    mask = (cend <= qpos) & (cidx < nc)
    dist = (qpos - cend).astype(F32)
    o, imp_c = _attend_stacked(s, dist, mask, vc_ref[0], slopes_ref, g, HG, tq)
    _unstack_store(o_ref, o, HG, tq)

    crow = lax.broadcasted_iota(jnp.int32, (NCp, NSBp), 0)
    jcol = lax.broadcasted_iota(jnp.int32, (NCp, NSBp), 1)
    overlap = ((S_CMP * crow < L_SEL * (jcol + 1)) & (S_CMP * crow + L_CMP > L_SEL * jcol)
               & (crow < nc)).astype(F32)
    imp = jnp.dot(imp_c, overlap, precision=lax.Precision.HIGHEST, preferred_element_type=F32)
    lane = lax.broadcasted_iota(jnp.int32, (tq, NSBp), 1)
    cur = jnp.right_shift(qpos, int(math.log2(L_SEL)))
    forced = ((lane == 0) | (lane == cur) | (lane == cur - 1)).astype(F32)
    score = jnp.where(lane <= cur, imp + FORCE_BONUS * forced, NEG)
    score = jnp.where(lane < nsb, score, -3e38)

    def body(i, cnt):
        col = jnp.sum(jnp.where(lane == i, score, 0.0), axis=1, keepdims=True)
        beats = (col > score) | ((col == score) & (i < lane))
        return cnt + beats.astype(F32)

    cnt = lax.fori_loop(0, nsb, body, jnp.zeros((tq, NSBp), F32))
    sel = (cnt < TOPK_SEL) & (score > 0.5 * NEG)
    selm_ref[0, 0] = sel.astype(F32)


def nsa_cmp(proj, kvc, slopes, *, tq, nc, nsb, pos0):
    B, T, _ = proj.shape
    HG = slopes.shape[0] // G_KV
    NCp = kvc.shape[1]
    NSBp = -(-nsb // LANES) * LANES
    gw = HG * HEAD_B
    return pl.pallas_call(
        functools.partial(_nsa_cmp_kernel, tq=tq, HG=HG, nc=nc, nsb=nsb, pos0=pos0),
        grid=(B, G_KV, T // tq),
        in_specs=[pl.BlockSpec(memory_space=pltpu.SMEM),
                  pl.BlockSpec((1, tq, gw), lambda b, g, t: (b, t, g)),
                  pl.BlockSpec((1, NCp, HEAD_B), lambda b, g, t: (b, 0, g)),
                  pl.BlockSpec((1, NCp, HEAD_B), lambda b, g, t: (b, 0, G_KV + g))],
        out_specs=[pl.BlockSpec((1, tq, gw), lambda b, g, t: (b, t, g)),
                   pl.BlockSpec((1, 1, tq, NSBp), lambda b, g, t: (b, g, t, 0))],
        out_shape=[jax.ShapeDtypeStruct((B, T, G_KV * gw), F32),
                   jax.ShapeDtypeStruct((B, G_KV, T, NSBp), F32)],
        compiler_params=_cparams(("parallel", "parallel", "parallel")),
        name="nsa_cmp",
    )(slopes, proj, kvc.reshape(B, NCp, -1), kvc.reshape(B, NCp, -1))


def _nsa_selwin_prompt_kernel(slopes_ref, q_ref, selm_ref, ks_ref, vs_ref, kw_ref, vw_ref,
                              osel_ref, owin_ref, *, tq, HG, T, WS):
    g = pl.program_id(1)
    qt = pl.program_id(2)
    NSBp = selm_ref.shape[3]
    q_st = _stack_heads(q_ref[0] * (HEAD_B ** -0.5), HG).astype(BF16)
    qpos = qt * tq + lax.broadcasted_iota(jnp.int32, (tq, 1), 0)

    kpos = lax.broadcasted_iota(jnp.int32, (1, T), 1)
    expand = (jnp.right_shift(lax.broadcasted_iota(jnp.int32, (NSBp, T), 1), int(math.log2(L_SEL)))
              == lax.broadcasted_iota(jnp.int32, (NSBp, T), 0)).astype(BF16)
    in_blk = jnp.dot(selm_ref[0, 0].astype(BF16), expand, preferred_element_type=F32) > 0.5
    dist = qpos - kpos
    s = _bdot_nt(q_st, ks_ref[0])
    o, _ = _attend_stacked(s, dist.astype(F32), in_blk & (dist >= 0), vs_ref[0], slopes_ref, g, HG, tq)
    _unstack_store(osel_ref, o, HG, tq)

    start = pl.multiple_of(jnp.clip(qt * tq - WINDOW, 0, T - WS), SUBLANES)
    kposw = start + lax.broadcasted_iota(jnp.int32, (1, WS), 1)
    distw = qpos - kposw
    sw = _bdot_nt(q_st, kw_ref[0, pl.ds(start, WS), :])
    ow, _ = _attend_stacked(sw, distw.astype(F32), (distw >= 0) & (distw < WINDOW),
                            vw_ref[0, pl.ds(start, WS), :], slopes_ref, g, HG, tq)
    _unstack_store(owin_ref, ow, HG, tq)


def nsa_selwin_prompt(proj, rows, selm, slopes, *, tq):
    B, T, _ = proj.shape
    HG = slopes.shape[0] // G_KV
    NSBp = selm.shape[3]
    gw = HG * HEAD_B
    WS = min(T, WINDOW + tq)
    kv_spec = lambda c: pl.BlockSpec((1, T, HEAD_B), lambda b, g, t: (b, 0, c * G_KV + g))
    o_spec = pl.BlockSpec((1, tq, gw), lambda b, g, t: (b, t, g))
    o_sd = jax.ShapeDtypeStruct((B, T, G_KV * gw), F32)
    return pl.pallas_call(
        functools.partial(_nsa_selwin_prompt_kernel, tq=tq, HG=HG, T=T, WS=WS),
        grid=(B, G_KV, T // tq),
        in_specs=[pl.BlockSpec(memory_space=pltpu.SMEM),
                  pl.BlockSpec((1, tq, gw), lambda b, g, t: (b, t, g)),
                  pl.BlockSpec((1, 1, tq, NSBp), lambda b, g, t: (b, g, t, 0)),
                  kv_spec(2), kv_spec(3), kv_spec(4), kv_spec(5)],
        out_specs=[o_spec, o_spec],
        out_shape=[o_sd, o_sd],
        compiler_params=_cparams(("parallel", "parallel", "parallel")),
        name="nsa_selwin_prompt",
    )(slopes, proj, selm, rows, rows, rows, rows)


def _nsa_selwin_sample_kernel(pt_ref, slopes_ref, q_ref, selm_ref, *refs, PGS, PS, HG, TQ, pos0, n_new, n_win):
    del pt_ref
    page_refs = refs[:PGS]
    new_ref, cwin_ref, osel_ref, owin_ref, m_scr, l_scr, acc_scr = refs[PGS:]
    st = pl.program_id(1)
    NSBp = selm_ref.shape[3]
    R = HG * TQ
    GW = G_KV * HEAD_B
    sel_shift = int(math.log2(L_SEL))

    @pl.when(st == 0)
    def _():
        m_scr[...] = jnp.full(m_scr.shape, NEG, F32)
        l_scr[...] = jnp.zeros(l_scr.shape, F32)
        acc_scr[...] = jnp.zeros(acc_scr.shape, F32)

    qpos = pos0 + lax.broadcasted_iota(jnp.int32, (TQ, 1), 0)
    lane_j = lax.broadcasted_iota(jnp.int32, (TQ, NSBp), 1)
    tile_heads = lambda x: jnp.concatenate([x] * HG, axis=0)

    def sel_col(selm_g, j):
        return jnp.sum(jnp.where(lane_j == j, selm_g, 0.0), axis=1, keepdims=True)

    def online_update(g, s, mask, v):
        m_old = m_scr[g]
        m_new = jnp.maximum(m_old, jnp.max(jnp.where(mask, s, NEG), axis=-1, keepdims=True))
        e = jnp.where(mask, jnp.exp(s - m_new), 0.0)
        alpha = jnp.exp(m_old - m_new)
        l_scr[g] = alpha * l_scr[g] + jnp.sum(e, axis=-1, keepdims=True)
        acc_scr[g] = alpha * acc_scr[g] + _bdot(e, v)
        m_scr[g] = m_new

    for g in range(G_KV):
        q_st = _stack_heads(q_ref[0, :, g * HG * HEAD_B:(g + 1) * HG * HEAD_B] * (HEAD_B ** -0.5), HG).astype(BF16)
        slope_col = jnp.concatenate([jnp.full((TQ, 1), slopes_ref[g * HG + h], F32) for h in range(HG)], axis=0)
        selm_g = selm_ref[0, g]
        for i in range(PGS):
            pg = st * PGS + i
            k = page_refs[i][0, :, g * HEAD_B:(g + 1) * HEAD_B]
            v = page_refs[i][0, :, GW + g * HEAD_B:GW + (g + 1) * HEAD_B]
            kpos = pg * PS + lax.broadcasted_iota(jnp.int32, (1, PS), 1)
            jblk = jnp.right_shift(kpos, sel_shift)
            j0 = jnp.right_shift(pg * PS, sel_shift)
            in_blk = jnp.zeros((TQ, PS), F32)
            for jj in range(PS // L_SEL):
                in_blk = jnp.where(jblk == j0 + jj, sel_col(selm_g, j0 + jj), in_blk)
            dist = qpos - kpos
            mask = tile_heads((in_blk > 0.5) & (dist >= 0))
            s = _bdot_nt(q_st, k) - slope_col * tile_heads(dist.astype(F32))
            online_update(g, s, mask, v)

    @pl.when(st == pl.num_programs(1) - 1)
    def _():
        NN = new_ref.shape[1]
        rnew = lax.broadcasted_iota(jnp.int32, (1, NN), 1)
        kpos_n = pos0 + rnew
        dist_n = qpos - kpos_n
        ok_n = (rnew < n_new) & (dist_n >= 0)
        jn = pos0 >> sel_shift
        kpos_w = pos0 - n_win + lax.broadcasted_iota(jnp.int32, (1, n_win), 1)
        dist_w = qpos - kpos_w
        dist_wall = jnp.concatenate([dist_w, dist_n], axis=1)
        mask_wall = jnp.concatenate([(dist_w >= 0) & (dist_w < WINDOW), ok_n & (dist_n < WINDOW)], axis=1)
        for g in range(G_KV):
            q_st = _stack_heads(q_ref[0, :, g * HG * HEAD_B:(g + 1) * HG * HEAD_B] * (HEAD_B ** -0.5), HG).astype(BF16)
            slope_col = jnp.concatenate([jnp.full((TQ, 1), slopes_ref[g * HG + h], F32) for h in range(HG)], axis=0)
            k = new_ref[0, :, 2 * GW + g * HEAD_B:2 * GW + (g + 1) * HEAD_B]
            v = new_ref[0, :, 3 * GW + g * HEAD_B:3 * GW + (g + 1) * HEAD_B]
            mask = tile_heads((sel_col(selm_ref[0, g], jn) > 0.5) & ok_n)
            s = _bdot_nt(q_st, k) - slope_col * tile_heads(dist_n.astype(F32))
            online_update(g, s, mask, v)
            l = l_scr[g]
            o = acc_scr[g] / jnp.where(l > 0.0, l, 1.0)
            for h in range(HG):
                osel_ref[0, :, (g * HG + h) * HEAD_B:(g * HG + h + 1) * HEAD_B] = o[h * TQ:(h + 1) * TQ]
            kw = jnp.concatenate([cwin_ref[0, :, g * HEAD_B:(g + 1) * HEAD_B],
                                  new_ref[0, :, 4 * GW + g * HEAD_B:4 * GW + (g + 1) * HEAD_B]], axis=0)
            vw = jnp.concatenate([cwin_ref[0, :, GW + g * HEAD_B:GW + (g + 1) * HEAD_B],
                                  new_ref[0, :, 5 * GW + g * HEAD_B:5 * GW + (g + 1) * HEAD_B]], axis=0)
            sw = _bdot_nt(q_st, kw) - slope_col * tile_heads(dist_wall.astype(F32))
            pw = _masked_softmax_rows(sw, tile_heads(mask_wall))
            ow = _bdot(pw, vw)
            for h in range(HG):
                owin_ref[0, :, (g * HG + h) * HEAD_B:(g * HG + h + 1) * HEAD_B] = ow[h * TQ:(h + 1) * TQ]


def nsa_selwin_sample(proj, selm, pool, table, new_rows, cwin, slopes, *, pos0, n_new):
    B, TQ, _ = proj.shape
    HG = slopes.shape[0] // G_KV
    NP, PS, _ = pool.shape
    n_pages = table.shape[1]
    NSBp = selm.shape[3]
    PGS = max(d for d in (8, 4, 2, 1) if n_pages % d == 0)
    CB = G_KV * HG * HEAD_B
    GW = G_KV * HEAD_B
    NN = new_rows.shape[1]
    n_win = cwin.shape[1]
    assert pos0 % L_SEL == 0 and n_new <= L_SEL and pos0 == n_pages * PS

    def page_map(i):
        return lambda b, s, pt: (pt[b, s * PGS + i], 0, 1)

    const = lambda b, s, pt: (b, 0, 0)
    o_sd = jax.ShapeDtypeStruct((B, TQ, CB), F32)
    grid_spec = pltpu.PrefetchScalarGridSpec(
        num_scalar_prefetch=1,
        grid=(B, n_pages // PGS),
        in_specs=[pl.BlockSpec(memory_space=pltpu.SMEM),
                  pl.BlockSpec((1, TQ, CB), const),
                  pl.BlockSpec((1, G_KV, TQ, NSBp), lambda b, s, pt: (b, 0, 0, 0))]
                 + [pl.BlockSpec((1, PS, 2 * GW), page_map(i)) for i in range(PGS)]
                 + [pl.BlockSpec((1, NN, 6 * GW), const),
                    pl.BlockSpec((1, n_win, 2 * GW), const)],
        out_specs=[pl.BlockSpec((1, TQ, CB), const), pl.BlockSpec((1, TQ, CB), const)],
        scratch_shapes=[pltpu.VMEM((G_KV, HG * TQ, 1), F32),
                        pltpu.VMEM((G_KV, HG * TQ, 1), F32),
                        pltpu.VMEM((G_KV, HG * TQ, HEAD_B), F32)],
    )
    return pl.pallas_call(
        functools.partial(_nsa_selwin_sample_kernel, PGS=PGS, PS=PS, HG=HG, TQ=TQ, pos0=pos0,
                          n_new=n_new, n_win=n_win),
        grid_spec=grid_spec,
        out_shape=[o_sd, o_sd],
        compiler_params=_cparams(("parallel", "arbitrary")),
        name="nsa_selwin_sample",
    )(table, slopes, proj, selm, *([pool] * PGS), new_rows, cwin)


def _rwkv_layer(hn, x_prev, s0, i, W):
    B, T, D = hn.shape
    N = B * T
    CA = W["w_out_a"].shape[1]
    H = CA // HEAD_A
    x_shift = jnp.concatenate([x_prev[:, None, :], hn[:, :-1]], axis=1)
    dx = x_shift - hn
    xm = [(hn + dx * W["mu_a"][i, j]).reshape(N, D) for j in range(6)]
    r, k, v, zg = (mm(xm[j], W["w_in_a"], (i, j)) for j in range(4))
    wl = mm(jnp.tanh(mm(xm[4], W["w_lora_w1"], (i,))), W["w_lora_w2"], (i,))
    w_log = -jax.nn.softplus(-(W["w0_a"][i] + wl)) - 0.5
    lw = -jnp.exp(w_log)
    a = jax.nn.sigmoid(W["a0_a"][i] + mm(mm(xm[5], W["a_lora1"], (i,)), W["a_lora2"], (i,)))
    kk = (k * W["k_k"][i]).reshape(N, H, HEAD_A)
    kk = (kk / jnp.maximum(jnp.sqrt(jnp.sum(kk * kk, axis=-1, keepdims=True)), 1e-12)).reshape(N, CA)
    k = k * (1.0 + (a - 1.0) * W["k_a"][i])
    sh = lambda z: z.reshape(B, T, CA)
    o, s_fin = rwkv_scan(sh(r), sh(lw), sh(k), sh(v), sh(-kk), sh(kk * a), s0)
    o = o.reshape(N, H, HEAD_A)
    mean = jnp.mean(o, axis=-1, keepdims=True)
    var = jnp.mean(jnp.square(o - mean), axis=-1, keepdims=True)
    o = ((o - mean) * lax.rsqrt(var + GN_EPS)).reshape(N, CA) * W["ln_x_w"][i] + W["ln_x_b"][i]
    rk = (r * k).reshape(N, H, HEAD_A) * W["r_k"][i]
    bonus = (jnp.sum(rk, axis=-1, keepdims=True) * v.reshape(N, H, HEAD_A)).reshape(N, CA)
    o = (o + bonus) * jax.nn.silu(zg)
    return mm(o, W["w_out_a"], (i,)).reshape(B, T, D), s_fin, hn[:, -1]


def _nsa_layer(hn, jb, shared, W, slopes):
    B, T, D = hn.shape
    N = B * T
    CB = W["w_out_b"].shape[1]
    HB = CB // HEAD_B
    proj = mm(hn.reshape(N, D), W["w_in_b"], (jb,))
    proj3 = proj.reshape(B, T, -1)
    if shared["past"] is None:
        tq = 64
        o_cmp, selm = nsa_cmp(proj3, shared["kvc"], slopes, tq=tq, nc=shared["nc"],
                              nsb=shared["nsb"], pos0=0)
        o_sel, o_win = nsa_selwin_prompt(proj3, shared["rows"], selm, slopes, tq=tq)
    else:
        TQ = SUBLANES
        projp = jnp.pad(proj3, ((0, 0), (0, TQ - T), (0, 0)))
        o_cmp, selm = nsa_cmp(projp, shared["kvc"], slopes, tq=TQ, nc=shared["nc"],
                              nsb=shared["nsb"], pos0=shared["pos0"])
        pool, table, cwin = shared["past"]
        o_sel, o_win = nsa_selwin_sample(projp, selm, pool, table, shared["new_rows"], cwin, slopes,
                                         pos0=shared["pos0"], n_new=T)
        o_cmp, o_sel, o_win = (z[:, :T] for z in (o_cmp, o_sel, o_win))
    z = proj[:, CB:4 * CB].reshape(N, 3, HB, HEAD_B)
    gate = jax.nn.sigmoid(proj[:, 4 * CB:]).reshape(N, 3, HB, 1)
    o = jnp.stack([o_cmp.reshape(N, HB, HEAD_B), o_sel.reshape(N, HB, HEAD_B), o_win.reshape(N, HB, HEAD_B)], axis=1)
    o = jnp.sum(gate * o * jax.nn.silu(z), axis=1).reshape(N, CB)
    return mm(o, W["w_out_b"], (jb,)).reshape(B, T, D)


def _trunk(x, p, pos0, wkv0, shift0, past, W, slopes):
    B, T, D = x.shape
    N = B * T
    depth = p.shape[0]
    n_a = W["w_in_a"].shape[0]
    GW = G_KV * HEAD_B
    h = x
    wkv_new, shift_new = [], []
    shared, kv_rows, win_state = None, None, None
    for i in range(depth):
        hn = rmsnorm(h.reshape(N, D), W["norm_g"][i]).reshape(B, T, D)
        if i < n_a:
            out, s_fin, last = _rwkv_layer(hn, shift0[i], wkv0[i], i, W)
            wkv_new.append(s_fin)
            shift_new.append(last)
        else:
            out = _nsa_layer(hn, i - n_a, shared, W, slopes)
        h = h + out
        ple = mm(p[i].reshape(N, -1), W["w_ple"], (i,))
        gate = jax.nn.sigmoid(mm(h.reshape(N, D), W["w_ple_gate"], (i,)))
        h = h + (ple * gate).reshape(B, T, D)
        if i == n_a - 1:
            rows = mm(rmsnorm(h.reshape(N, D), W["kv_norm_g"]), W["w_kv"]).reshape(B, T, 6 * GW)
            kv_rows = rows[:, :, :4 * GW].reshape(B, T, 4, G_KV, HEAD_B)
            win_new = rows[:, :, 4 * GW:].reshape(B, T, 2, G_KV, HEAD_B)
            if past is None:
                PS = 128
                pool = rows.reshape(B * T // PS, PS, 6 * GW)
                table = jnp.arange(B * T // PS, dtype=jnp.int32).reshape(B, T // PS)
                t_all = T
                win_all = win_new
                shared = {"past": None, "rows": rows}
            else:
                pool, table, cwin = past
                PS = pool.shape[1]
                t_all = pos0 + T
                win_all = jnp.concatenate([cwin.reshape(B, -1, 2, G_KV, HEAD_B), win_new], axis=1)
                NN = LANES
                shared = {"past": past, "new_rows": jnp.pad(rows, ((0, 0), (0, NN - T), (0, 0)))}
            win_state = win_all[:, win_all.shape[1] - min(WINDOW, pos0 + T):]
            nc = (t_all - L_CMP) // S_CMP + 1
            assert nc < table.shape[1] * PS // S_CMP
            kvc = compress_kv(pool, table, W["pe_cmp"], W["w_cmp1"], W["w_cmp2"])
            shared.update(kvc=kvc, nc=nc, nsb=max(-(-t_all // L_SEL), TOPK_SEL), pos0=pos0)
    y = rmsnorm(h.reshape(N, D), W["final_norm_g"]).reshape(B, T, D)
    return y, jnp.stack(wkv_new), jnp.stack(shift_new), kv_rows, win_state


def kernel(x_prompt, x_sample, state_wkv, state_shift, cache_kv, cache_win_kv, page_table, p_prompt, p_sample, norm_g, mu_a, w_in_a, w_lora_w1, w_lora_w2, w0_a, a_lora1, a_lora2, a0_a, k_k, k_a, r_k, ln_x_w, ln_x_b, w_out_a, w_in_b, w_out_b, kv_norm_g, w_kv, pe_cmp, w_cmp1, w_cmp2, w_ple, w_ple_gate, final_norm_g):
    bf = lambda w: w.astype(BF16)
    CA = w_out_a.shape[1]
    W = dict(norm_g=norm_g, mu_a=mu_a, w_in_a=bf(w_in_a), w_lora_w1=bf(w_lora_w1), w_lora_w2=bf(w_lora_w2),
             w0_a=w0_a, a_lora1=bf(a_lora1), a_lora2=bf(a_lora2), a0_a=a0_a, k_k=k_k, k_a=k_a,
             r_k=r_k.reshape(r_k.shape[0], 1, CA // HEAD_A, HEAD_A), ln_x_w=ln_x_w, ln_x_b=ln_x_b,
             w_out_a=bf(w_out_a), w_in_b=bf(w_in_b), w_out_b=bf(w_out_b), kv_norm_g=kv_norm_g, w_kv=bf(w_kv),
             pe_cmp=pe_cmp, w_cmp1=bf(w_cmp1), w_cmp2=bf(w_cmp2), w_ple=bf(w_ple), w_ple_gate=bf(w_ple_gate),
             final_norm_g=final_norm_g)
    HB = w_out_b.shape[1] // HEAD_B
    slopes = 2.0 ** (-8.0 * jnp.arange(1, HB + 1, dtype=F32) / HB)
    bp = x_prompt.shape[0]
    n_a = w_in_a.shape[0]
    D = x_prompt.shape[-1]
    wkv0 = jnp.zeros((n_a, bp, CA // HEAD_A, HEAD_A, HEAD_A), F32)
    shift0 = jnp.zeros((n_a, bp, D), F32)
    y_p, wkv_p, shift_p, kv_p, win_p = _trunk(x_prompt, p_prompt, 0, wkv0, shift0, None, W, slopes)
    db, n_pages = page_table.shape
    NP, PS = cache_kv.shape[:2]
    past = (cache_kv.reshape(NP, PS, -1), page_table, cache_win_kv.reshape(db, cache_win_kv.shape[1], -1))
    y_s, wkv_s, shift_s, kv_s, win_s = _trunk(x_sample, p_sample, n_pages * PS, state_wkv, state_shift, past, W, slopes)
    return (y_p, y_s, wkv_p, shift_p, kv_p, win_p, wkv_s, shift_s, kv_s, win_s)
```

```python
import functools
import math

import jax
import jax.numpy as jnp
from jax import lax
from jax.experimental import pallas as pl
from jax.experimental.pallas import tpu as pltpu

F32 = jnp.float32
BF16 = jnp.bfloat16

HEAD_A = 64
GN_EPS = 64e-5
HEAD_B = 128
G_KV = 4
L_CMP = 32
S_CMP = 16
L_SEL = 64
TOPK_SEL = 16
WINDOW = 512
RMS_EPS = 1e-6
NEG = -1e30
FORCE_BONUS = 1e4

LANES = 128
SUBLANES = 8
VMEM_LIMIT = 56 * 1024 * 1024

SCAN_NH = 2
SCAN_C = 64
SCAN_GP = 8

NT_DIMS = (((1,), (1,)), ((), ()))
TN_DIMS = (((0,), (0,)), ((), ()))


def _cparams(sem):
    return pltpu.CompilerParams(dimension_semantics=sem, vmem_limit_bytes=VMEM_LIMIT)


def _bdot(a, b):
    return jnp.dot(a.astype(BF16), b.astype(BF16), preferred_element_type=F32)


def _bdot_nt(a, b):
    return lax.dot_general(a.astype(BF16), b.astype(BF16), NT_DIMS, preferred_element_type=F32)


def _bdot_tn(a, b):
    return lax.dot_general(a.astype(BF16), b.astype(BF16), TN_DIMS, preferred_element_type=F32)


def _rms_kernel(x_ref, g_ref, o_ref):
    x = x_ref[...]
    ms = jnp.mean(x * x, axis=-1, keepdims=True)
    o_ref[...] = x * lax.rsqrt(ms + RMS_EPS) * g_ref[...]


def rmsnorm(x, g):
    M, D = x.shape
    tm = min(M, 256)
    return pl.pallas_call(
        _rms_kernel,
        grid=(pl.cdiv(M, tm),),
        in_specs=[pl.BlockSpec((tm, D), lambda i: (i, 0)),
                  pl.BlockSpec((1, D), lambda i: (0, 0))],
        out_specs=pl.BlockSpec((tm, D), lambda i: (i, 0)),
        out_shape=jax.ShapeDtypeStruct((M, D), F32),
        compiler_params=_cparams(("parallel",)),
        name="rmsnorm",
    )(x, g.reshape(1, D))


def _mm_kernel(x_ref, w_ref, o_ref):
    o_ref[...] = jnp.dot(x_ref[...].astype(BF16), w_ref[...], preferred_element_type=F32)


def mm(x, w, widx=()):
    M, K = x.shape
    N = w.shape[-1]
    assert w.shape[-2] == K and len(widx) == w.ndim - 2
    tm = min(M, 1024 if K <= 2048 else 512)
    tn = N if N <= 512 else 512
    nlead = len(widx)
    w_spec = pl.BlockSpec((None,) * nlead + (K, tn), lambda i, j: tuple(widx) + (0, j))
    return pl.pallas_call(
        _mm_kernel,
        grid=(pl.cdiv(M, tm), pl.cdiv(N, tn)),
        in_specs=[pl.BlockSpec((tm, K), lambda i, j: (i, 0)), w_spec],
        out_specs=pl.BlockSpec((tm, tn), lambda i, j: (i, j)),
        out_shape=jax.ShapeDtypeStruct((M, N), F32),
        compiler_params=_cparams(("parallel", "parallel")),
        name="mm",
    )(x, w)


def _scan_kernel(r_ref, lw_ref, k_ref, v_ref, a_ref, b_ref, s0_ref, o_ref, sfin_ref, s_scr, *, C, NH, GP):
    L = NH * HEAD_A
    NC = NH * C
    ci = pl.program_id(2)

    @pl.when(ci == 0)
    def _():
        s_scr[...] = s0_ref[0]

    row_c = lax.broadcasted_iota(jnp.int32, (C, NC), 0)
    col_s = lax.broadcasted_iota(jnp.int32, (C, NC), 1) % C
    tri_strict = col_s < row_c
    tri_incl = (lax.broadcasted_iota(jnp.int32, (C, 2 * NC), 1) % C
                <= lax.broadcasted_iota(jnp.int32, (C, 2 * NC), 0))
    st_mask = (lax.broadcasted_iota(jnp.int32, (NC, L), 0) // C
               == lax.broadcasted_iota(jnp.int32, (NC, L), 1) // HEAD_A)
    bd_mask = (lax.broadcasted_iota(jnp.int32, (NC, NC), 0) // C
               == lax.broadcasted_iota(jnp.int32, (NC, NC), 1) // C)
    head_mask = (lax.broadcasted_iota(jnp.int32, (L, L), 0) // HEAD_A
                 == lax.broadcasted_iota(jnp.int32, (L, L), 1) // HEAD_A)
    cum_mat = (lax.broadcasted_iota(jnp.int32, (C, C), 1)
               <= lax.broadcasted_iota(jnp.int32, (C, C), 0)).astype(F32)

    def st(x):
        return jnp.where(st_mask, jnp.concatenate([x] * NH, axis=0), 0.0)

    def bd(w):
        return jnp.where(bd_mask, jnp.concatenate([w] * NH, axis=0), 0.0)

    n_double = int(math.log2(C))
    each = lambda f, *cols: [f(*xs) for xs in zip(*cols)]
    sls = [slice(gp * L, (gp + 1) * L) for gp in range(GP)]
    lw = [lw_ref[0, :, sl] for sl in sls]
    cum = each(lambda z: jnp.dot(cum_mat, z, precision=lax.Precision.HIGHEST, preferred_element_type=F32), lw)
    p_incl = each(jnp.exp, cum)
    p_inv = each(lambda z: jnp.exp(-z), cum)
    at = [a_ref[0, :, sl] * jnp.exp(c - w) for sl, c, w in zip(sls, cum, lw)]
    rt = [r_ref[0, :, sl] * p for sl, p in zip(sls, p_incl)]
    bt = [b_ref[0, :, sl] * p for sl, p in zip(sls, p_inv)]
    kt = [k_ref[0, :, sl] * p for sl, p in zip(sls, p_inv)]
    v = [v_ref[0, :, sl] for sl in sls]
    S = [s_scr[gp] for gp in range(GP)]
    ar = each(lambda x, y: jnp.concatenate([x, y], axis=0), at, rt)
    bk_st = each(lambda x, y: jnp.concatenate([st(x), st(y)], axis=0), bt, kt)
    Gm = each(_bdot_nt, ar, bk_st)
    w_ab = [jnp.where(tri_strict, g[:C, :NC], 0.0) for g in Gm]
    tm = w_ab
    pw = each(lambda w: _bdot(w, bd(w)), w_ab)
    LH = each(_bdot_nt, ar, S)
    v_st = each(st, v)
    x = [lh[:C] + _bdot(jnp.where(tri_strict, g[:C, NC:], 0.0), vs) for lh, g, vs in zip(LH, Gm, v_st)]
    for kk in range(1, n_double):
        tm_next = each(lambda t, p: t + p + _bdot(p, bd(t)), tm, pw)
        if kk < n_double - 1:
            pw = each(lambda p: _bdot(p, bd(p)), pw)
        tm = tm_next
    u = each(lambda xx, t: xx + _bdot(t, st(xx)), x, tm)
    o = [lh[C:] + _bdot(jnp.where(tri_incl, g[C:], 0.0), jnp.concatenate([st(uu), vs], axis=0))
         for lh, g, uu, vs in zip(LH, Gm, u, v_st)]
    for sl, oo in zip(sls, o):
        o_ref[0, :, sl] = oo
    ds = [_bdot_tn(jnp.concatenate([uu, vv], axis=0), jnp.concatenate([b_, k_], axis=0))
          for uu, vv, b_, k_ in zip(u, v, bt, kt)]
    for gp in range(GP):
        s_scr[gp] = (S[gp] + jnp.where(head_mask, ds[gp], 0.0)) * p_incl[gp][C - 1:C, :]

    @pl.when(ci == pl.num_programs(2) - 1)
    def _():
        sfin_ref[0] = s_scr[...]


def rwkv_scan(r, lw, k, v, a, b, s0):
    B, T, CA = r.shape
    H = CA // HEAD_A
    NH, C, GP = SCAN_NH, SCAN_C, SCAN_GP
    assert NH * C == LANES and H % (NH * GP) == 0
    L = NH * HEAD_A
    NG = H // NH
    Tp = -(-T // C) * C
    if Tp != T:
        pad = lambda z: jnp.pad(z, ((0, 0), (0, Tp - T), (0, 0)))
        r, lw, k, v, a, b = (pad(z) for z in (r, lw, k, v, a, b))
    eye = jnp.eye(NH, dtype=F32)
    s0_bd = (s0.reshape(B, NG, NH, HEAD_A, 1, HEAD_A) * eye[None, None, :, None, :, None]).reshape(B, NG, L, L)
    seq_spec = pl.BlockSpec((1, C, GP * L), lambda bi, gi, ci: (bi, ci, gi))
    st_spec = pl.BlockSpec((1, GP, L, L), lambda bi, gi, ci: (bi, gi, 0, 0))
    o, sfin = pl.pallas_call(
        functools.partial(_scan_kernel, C=C, NH=NH, GP=GP),
        grid=(B, NG // GP, Tp // C),
        in_specs=[seq_spec] * 6 + [st_spec],
        out_specs=[seq_spec, st_spec],
        out_shape=[jax.ShapeDtypeStruct((B, Tp, CA), F32), jax.ShapeDtypeStruct((B, NG, L, L), F32)],
        scratch_shapes=[pltpu.VMEM((GP, L, L), F32)],
        compiler_params=_cparams(("parallel", "parallel", "arbitrary")),
        name="rwkv_scan",
    )(r, lw, k, v, a, b, s0_bd)
    sf = sfin.reshape(B, NG, NH, HEAD_A, NH, HEAD_A)
    s_fin = jnp.stack([sf[:, :, h, :, h, :] for h in range(NH)], axis=2).reshape(B, H, HEAD_A, HEAD_A)
    return o[:, :T], s_fin


def _gelu_tanh(x):
    c = math.sqrt(2.0 / math.pi)
    return 0.5 * x * (1.0 + jnp.tanh(c * (x + 0.044715 * (x * x * x))))


def _compress_kernel(pt_ref, *refs, PGS):
    del pt_ref
    page_refs = refs[:PGS]
    next_ref, pe_ref, w1_ref, w2_ref, out_ref = refs[PGS:]
    CPP = page_refs[0].shape[1]
    NCH = PGS * CPP
    CG = 2 * G_KV
    M = (NCH + 1) * CG

    def rows_of(l, hf):
        pe = pe_ref[hf, l]
        parts = [(page_refs[i][0, :, l] + pe[None]).reshape(CPP * CG, HEAD_B) for i in range(PGS)]
        parts.append(next_ref[0, 0, l] + pe)
        return jnp.concatenate(parts, axis=0)

    top = jnp.zeros((M, 2 * HEAD_B), F32)
    bot = jnp.zeros((M, 2 * HEAD_B), F32)
    for l in range(0, S_CMP, 2):
        wrows = pl.ds(l * HEAD_B, 2 * HEAD_B)
        xt = jnp.concatenate([rows_of(l, 0), rows_of(l + 1, 0)], axis=1).astype(BF16)
        top = top + jnp.dot(xt, w1_ref[0, wrows, :], preferred_element_type=F32)
        xb = jnp.concatenate([rows_of(l, 1), rows_of(l + 1, 1)], axis=1).astype(BF16)
        bot = bot + jnp.dot(xb, w1_ref[1, wrows, :], preferred_element_type=F32)
    is_k = (lax.broadcasted_iota(jnp.int32, (M, 1), 0) % CG) < G_KV
    pick = lambda z, n: jnp.where(is_k[:n], z[:n, :HEAD_B], z[:n, HEAD_B:])
    hcur = pick(top, NCH * CG) + pick(bot, M)[CG:]
    o2 = jnp.dot(_gelu_tanh(hcur).astype(BF16), w2_ref[...], preferred_element_type=F32)
    out_ref[0] = pick(o2, NCH * CG).reshape(NCH, CG, HEAD_B)


def compress_kv(pool5, table, pe_cmp, w1, w2):
    NP, CPP = pool5.shape[:2]
    B, n_pages = table.shape
    PGS = max(d for d in (8, 4, 2, 1) if n_pages % d == 0)
    NCH = PGS * CPP
    CG = 2 * G_KV
    half = S_CMP * HEAD_B
    pe_r = jnp.repeat(pe_cmp.reshape(2, 2, S_CMP, HEAD_B).transpose(1, 2, 0, 3), G_KV, axis=2)
    w1_r = w1.reshape(2, 2, half, HEAD_B).transpose(1, 2, 0, 3).reshape(2, half, 2 * HEAD_B)
    w2_r = jnp.concatenate([w2[0], w2[1]], axis=1)

    def page_map(i):
        return lambda b, s, pt: (pt[b, s * PGS + i], 0, 0, 0, 0)

    def next_map(b, s, pt):
        return (pt[b, jnp.minimum((s + 1) * PGS, n_pages - 1)], 0, 0, 0, 0)

    const = lambda n: (lambda b, s, pt: (0,) * n)
    grid_spec = pltpu.PrefetchScalarGridSpec(
        num_scalar_prefetch=1,
        grid=(B, n_pages // PGS),
        in_specs=[pl.BlockSpec((1, CPP, S_CMP, CG, HEAD_B), page_map(i)) for i in range(PGS)] + [
            pl.BlockSpec((1, 1, S_CMP, CG, HEAD_B), next_map),
            pl.BlockSpec((2, S_CMP, CG, HEAD_B), const(4)),
            pl.BlockSpec((2, half, 2 * HEAD_B), const(3)),
            pl.BlockSpec((HEAD_B, 2 * HEAD_B), const(2)),
        ],
        out_specs=pl.BlockSpec((1, NCH, CG, HEAD_B), lambda b, s, pt: (b, s, 0, 0)),
    )
    return pl.pallas_call(
        functools.partial(_compress_kernel, PGS=PGS),
        grid_spec=grid_spec,
        out_shape=jax.ShapeDtypeStruct((B, n_pages * CPP, CG, HEAD_B), F32),
        compiler_params=_cparams(("parallel", "arbitrary")),
        name="compress_kv",
    )(table, *([pool5] * PGS), pool5, pe_r, w1_r, w2_r)


def _stack_heads(q, HG):
    return jnp.concatenate([q[:, h * HEAD_B:(h + 1) * HEAD_B] for h in range(HG)], axis=0)


def _masked_softmax_rows(s, mask):
    s = jnp.where(mask, s, NEG)
    m = jnp.max(s, axis=-1, keepdims=True)
    e = jnp.where(mask, jnp.exp(s - m), 0.0)
    l = jnp.sum(e, axis=-1, keepdims=True)
    return e / jnp.where(l > 0.0, l, 1.0)


def _attend_stacked(s, dist, mask, v, slopes_ref, g, HG, tq):
    ps = []
    psum = jnp.zeros(dist.shape, F32)
    for h in range(HG):
        p = _masked_softmax_rows(s[h * tq:(h + 1) * tq] - slopes_ref[g * HG + h] * dist, mask)
        psum = psum + p
        ps.append(p.astype(BF16))
    o = jnp.dot(jnp.concatenate(ps, axis=0), v.astype(BF16), preferred_element_type=F32)
    return o, psum


def _unstack_store(o_ref, o, HG, tq):
    for h in range(HG):
        o_ref[0, :, h * HEAD_B:(h + 1) * HEAD_B] = o[h * tq:(h + 1) * tq]


def _nsa_cmp_kernel(slopes_ref, q_ref, kc_ref, vc_ref, o_ref, selm_ref, *, tq, HG, nc, nsb, pos0):
    g = pl.program_id(1)
    qt = pl.program_id(2)
    NCp = kc_ref.shape[1]
    NSBp = selm_ref.shape[3]
    q_st = _stack_heads(q_ref[0] * (HEAD_B ** -0.5), HG)
    s = _bdot_nt(q_st, kc_ref[0])
    qpos = pos0 + qt * tq + lax.broadcasted_iota(jnp.int32, (tq, 1), 0)
    cidx = lax.broadcasted_iota(jnp.int32, (1, NCp), 1)
    cend = S_CMP * cidx + (L_CMP - 1)
    mask = (cend <= qpos) & (cidx < nc)
    dist = (qpos - cend).astype(F32)
    o, imp_c = _attend_stacked(s, dist, mask, vc_ref[0], slopes_ref, g, HG, tq)
    _unstack_store(o_ref, o, HG, tq)

    crow = lax.broadcasted_iota(jnp.int32, (NCp, NSBp), 0)
    jcol = lax.broadcasted_iota(jnp.int32, (NCp, NSBp), 1)
    overlap = ((S_CMP * crow < L_SEL * (jcol + 1)) & (S_CMP * crow + L_CMP > L_SEL * jcol)
               & (crow < nc)).astype(F32)
    imp = jnp.dot(imp_c, overlap, precision=lax.Precision.HIGHEST, preferred_element_type=F32)
    lane = lax.broadcasted_iota(jnp.int32, (tq, NSBp), 1)
    cur = jnp.right_shift(qpos, int(math.log2(L_SEL)))
    forced = ((lane == 0) | (lane == cur) | (lane == cur - 1)).astype(F32)
    score = jnp.where(lane <= cur, imp + FORCE_BONUS * forced, NEG)
    score = jnp.where(lane < nsb, score, -3e38)

    cnt = jnp.zeros((tq, NSBp), F32)
    for i in range(nsb):
        col = score[:, i:i + 1]
        beats = (col > score) | ((col == score) & (lane > i))
        cnt = cnt + jnp.where(beats, 1.0, 0.0)
    sel = (cnt < TOPK_SEL) & (score > 0.5 * NEG)
    selm_ref[0, 0] = sel.astype(F32)


def nsa_cmp(proj, kvc, slopes, *, tq, nc, nsb, pos0):
    B, T, _ = proj.shape
    HG = slopes.shape[0] // G_KV
    NCp = kvc.shape[1]
    NSBp = -(-nsb // LANES) * LANES
    gw = HG * HEAD_B
    return pl.pallas_call(
        functools.partial(_nsa_cmp_kernel, tq=tq, HG=HG, nc=nc, nsb=nsb, pos0=pos0),
        grid=(B, G_KV, T // tq),
        in_specs=[pl.BlockSpec(memory_space=pltpu.SMEM),
                  pl.BlockSpec((1, tq, gw), lambda b, g, t: (b, t, g)),
                  pl.BlockSpec((1, NCp, HEAD_B), lambda b, g, t: (b, 0, g)),
                  pl.BlockSpec((1, NCp, HEAD_B), lambda b, g, t: (b, 0, G_KV + g))],
        out_specs=[pl.BlockSpec((1, tq, gw), lambda b, g, t: (b, t, g)),
                   pl.BlockSpec((1, 1, tq, NSBp), lambda b, g, t: (b, g, t, 0))],
        out_shape=[jax.ShapeDtypeStruct((B, T, G_KV * gw), F32),
                   jax.ShapeDtypeStruct((B, G_KV, T, NSBp), F32)],
        compiler_params=_cparams(("parallel", "parallel", "parallel")),
        name="nsa_cmp",
    )(slopes, proj, kvc.reshape(B, NCp, -1), kvc.reshape(B, NCp, -1))


def _nsa_selwin_prompt_kernel(slopes_ref, q_ref, selm_ref, ks_ref, vs_ref, kw_ref, vw_ref,
                              osel_ref, owin_ref, *, tq, HG, T, WS):
    g = pl.program_id(1)
    qt = pl.program_id(2)
    NSBp = selm_ref.shape[3]
    q_st = _stack_heads(q_ref[0] * (HEAD_B ** -0.5), HG).astype(BF16)
    qpos = qt * tq + lax.broadcasted_iota(jnp.int32, (tq, 1), 0)

    kpos = lax.broadcasted_iota(jnp.int32, (1, T), 1)
    expand = (jnp.right_shift(lax.broadcasted_iota(jnp.int32, (NSBp, T), 1), int(math.log2(L_SEL)))
              == lax.broadcasted_iota(jnp.int32, (NSBp, T), 0)).astype(BF16)
    in_blk = jnp.dot(selm_ref[0, 0].astype(BF16), expand, preferred_element_type=F32) > 0.5
    dist = qpos - kpos
    s = _bdot_nt(q_st, ks_ref[0])
    o, _ = _attend_stacked(s, dist.astype(F32), in_blk & (dist >= 0), vs_ref[0], slopes_ref, g, HG, tq)
    _unstack_store(osel_ref, o, HG, tq)

    start = pl.multiple_of(jnp.clip(qt * tq - WINDOW, 0, T - WS), SUBLANES)
    kposw = start + lax.broadcasted_iota(jnp.int32, (1, WS), 1)
    distw = qpos - kposw
    sw = _bdot_nt(q_st, kw_ref[0, pl.ds(start, WS), :])
    ow, _ = _attend_stacked(sw, distw.astype(F32), (distw >= 0) & (distw < WINDOW),
                            vw_ref[0, pl.ds(start, WS), :], slopes_ref, g, HG, tq)
    _unstack_store(owin_ref, ow, HG, tq)


def nsa_selwin_prompt(proj, rows, selm, slopes, *, tq):
    B, T, _ = proj.shape
    HG = slopes.shape[0] // G_KV
    NSBp = selm.shape[3]
    gw = HG * HEAD_B
    WS = min(T, WINDOW + tq)
    kv_spec = lambda c: pl.BlockSpec((1, T, HEAD_B), lambda b, g, t: (b, 0, c * G_KV + g))
    o_spec = pl.BlockSpec((1, tq, gw), lambda b, g, t: (b, t, g))
    o_sd = jax.ShapeDtypeStruct((B, T, G_KV * gw), F32)
    return pl.pallas_call(
        functools.partial(_nsa_selwin_prompt_kernel, tq=tq, HG=HG, T=T, WS=WS),
        grid=(B, G_KV, T // tq),
        in_specs=[pl.BlockSpec(memory_space=pltpu.SMEM),
                  pl.BlockSpec((1, tq, gw), lambda b, g, t: (b, t, g)),
                  pl.BlockSpec((1, 1, tq, NSBp), lambda b, g, t: (b, g, t, 0)),
                  kv_spec(2), kv_spec(3), kv_spec(4), kv_spec(5)],
        out_specs=[o_spec, o_spec],
        out_shape=[o_sd, o_sd],
        compiler_params=_cparams(("parallel", "parallel", "parallel")),
        name="nsa_selwin_prompt",
    )(slopes, proj, selm, rows, rows, rows, rows)


def _nsa_selwin_sample_kernel(pt_ref, slopes_ref, q_ref, selm_ref, *refs, PGS, PS, HG, TQ, pos0, n_new, n_win):
    del pt_ref
    page_refs = refs[:PGS]
    new_ref, cwin_ref, osel_ref, owin_ref, m_scr, l_scr, acc_scr = refs[PGS:]
    st = pl.program_id(1)
    NSBp = selm_ref.shape[3]
    R = HG * TQ
    GW = G_KV * HEAD_B
    sel_shift = int(math.log2(L_SEL))

    @pl.when(st == 0)
    def _():
        m_scr[...] = jnp.full(m_scr.shape, NEG, F32)
        l_scr[...] = jnp.zeros(l_scr.shape, F32)
        acc_scr[...] = jnp.zeros(acc_scr.shape, F32)

    qpos = pos0 + lax.broadcasted_iota(jnp.int32, (TQ, 1), 0)
    lane_j = lax.broadcasted_iota(jnp.int32, (TQ, NSBp), 1)
    tile_heads = lambda x: jnp.concatenate([x] * HG, axis=0)

    def sel_col(selm_g, j):
        return jnp.sum(jnp.where(lane_j == j, selm_g, 0.0), axis=1, keepdims=True)

    def online_update(s, mask, v):
        gs = range(G_KV)
        m_old = [m_scr[g] for g in gs]
        m_new = [jnp.maximum(m_old[g], jnp.max(jnp.where(mask[g], s[g], NEG), axis=-1, keepdims=True)) for g in gs]
        e = [jnp.where(mask[g], jnp.exp(s[g] - m_new[g]), 0.0) for g in gs]
        alpha = [jnp.exp(m_old[g] - m_new[g]) for g in gs]
        pv = [_bdot(e[g], v[g]) for g in gs]
        for g in gs:
            l_scr[g] = alpha[g] * l_scr[g] + jnp.sum(e[g], axis=-1, keepdims=True)
            acc_scr[g] = alpha[g] * acc_scr[g] + pv[g]
            m_scr[g] = m_new[g]

    NK = PGS * PS
    kpos = st * NK + lax.broadcasted_iota(jnp.int32, (1, NK), 1)
    expand = (jnp.right_shift(st * NK + lax.broadcasted_iota(jnp.int32, (NSBp, NK), 1), sel_shift)
              == lax.broadcasted_iota(jnp.int32, (NSBp, NK), 0)).astype(BF16)
    dist = qpos - kpos
    distf = tile_heads(dist.astype(F32))
    gs = range(G_KV)
    q_st = [_stack_heads(q_ref[0, :, g * HG * HEAD_B:(g + 1) * HG * HEAD_B] * (HEAD_B ** -0.5), HG).astype(BF16)
            for g in gs]
    slope_col = [jnp.concatenate([jnp.full((TQ, 1), slopes_ref[g * HG + h], F32) for h in range(HG)], axis=0)
                 for g in gs]
    page_rows = lambda i, cg: page_refs[i][0, :, :, cg, :].reshape(PS, HEAD_B)
    k = [jnp.concatenate([page_rows(i, g) for i in range(PGS)], axis=0) for g in gs]
    v = [jnp.concatenate([page_rows(i, G_KV + g) for i in range(PGS)], axis=0) for g in gs]
    in_blk = [jnp.dot(selm_ref[0, g].astype(BF16), expand, preferred_element_type=F32) for g in gs]
    mask = [tile_heads((in_blk[g] > 0.5) & (dist >= 0)) for g in gs]
    s = [_bdot_nt(q_st[g], k[g]) - slope_col[g] * distf for g in gs]
    online_update(s, mask, v)

    @pl.when(st == pl.num_programs(1) - 1)
    def _():
        NN = new_ref.shape[1]
        rnew = lax.broadcasted_iota(jnp.int32, (1, NN), 1)
        kpos_n = pos0 + rnew
        dist_n = qpos - kpos_n
        ok_n = (rnew < n_new) & (dist_n >= 0)
        jn = pos0 >> sel_shift
        kpos_w = pos0 - n_win + lax.broadcasted_iota(jnp.int32, (1, n_win), 1)
        dist_w = qpos - kpos_w
        dist_wall = jnp.concatenate([dist_w, dist_n], axis=1)
        mask_wall = jnp.concatenate([(dist_w >= 0) & (dist_w < WINDOW), ok_n & (dist_n < WINDOW)], axis=1)
        kn = [new_ref[0, :, 2 * GW + g * HEAD_B:2 * GW + (g + 1) * HEAD_B] for g in gs]
        vn = [new_ref[0, :, 3 * GW + g * HEAD_B:3 * GW + (g + 1) * HEAD_B] for g in gs]
        mask_n = [tile_heads((sel_col(selm_ref[0, g], jn) > 0.5) & ok_n) for g in gs]
        dist_nf = tile_heads(dist_n.astype(F32))
        s_n = [_bdot_nt(q_st[g], kn[g]) - slope_col[g] * dist_nf for g in gs]
        online_update(s_n, mask_n, vn)
        kw = [jnp.concatenate([cwin_ref[0, :, g * HEAD_B:(g + 1) * HEAD_B],
                               new_ref[0, :, 4 * GW + g * HEAD_B:4 * GW + (g + 1) * HEAD_B]], axis=0) for g in gs]
        vw = [jnp.concatenate([cwin_ref[0, :, GW + g * HEAD_B:GW + (g + 1) * HEAD_B],
                               new_ref[0, :, 5 * GW + g * HEAD_B:5 * GW + (g + 1) * HEAD_B]], axis=0) for g in gs]
        dist_wf = tile_heads(dist_wall.astype(F32))
        mask_w = tile_heads(mask_wall)
        sw = [_bdot_nt(q_st[g], kw[g]) - slope_col[g] * dist_wf for g in gs]
        pw = [_masked_softmax_rows(sw[g], mask_w) for g in gs]
        ow = [_bdot(pw[g], vw[g]) for g in gs]
        for g in gs:
            l = l_scr[g]
            o = acc_scr[g] / jnp.where(l > 0.0, l, 1.0)
            for h in range(HG):
                osel_ref[0, :, (g * HG + h) * HEAD_B:(g * HG + h + 1) * HEAD_B] = o[h * TQ:(h + 1) * TQ]
                owin_ref[0, :, (g * HG + h) * HEAD_B:(g * HG + h + 1) * HEAD_B] = ow[g][h * TQ:(h + 1) * TQ]


def nsa_selwin_sample(proj, selm, pool5, table, new_rows, cwin, slopes, *, pos0, n_new):
    B, TQ, _ = proj.shape
    HG = slopes.shape[0] // G_KV
    NP, CPP = pool5.shape[:2]
    PS = CPP * S_CMP
    n_pages = table.shape[1]
    NSBp = selm.shape[3]
    PGS = max(d for d in (8, 4, 2, 1) if n_pages % d == 0)
    CB = G_KV * HG * HEAD_B
    GW = G_KV * HEAD_B
    NN = new_rows.shape[1]
    n_win = cwin.shape[1]
    assert pos0 % L_SEL == 0 and n_new <= L_SEL and pos0 == n_pages * PS

    def page_map(i):
        return lambda b, s, pt: (pt[b, s * PGS + i], 0, 0, 1, 0)

    const = lambda b, s, pt: (b, 0, 0)
    o_sd = jax.ShapeDtypeStruct((B, TQ, CB), F32)
    grid_spec = pltpu.PrefetchScalarGridSpec(
        num_scalar_prefetch=1,
        grid=(B, n_pages // PGS),
        in_specs=[pl.BlockSpec(memory_space=pltpu.SMEM),
                  pl.BlockSpec((1, TQ, CB), const),
                  pl.BlockSpec((1, G_KV, TQ, NSBp), lambda b, s, pt: (b, 0, 0, 0))]
                 + [pl.BlockSpec((1, CPP, S_CMP, 2 * G_KV, HEAD_B), page_map(i)) for i in range(PGS)]
                 + [pl.BlockSpec((1, NN, 6 * GW), const),
                    pl.BlockSpec((1, n_win, 2 * GW), const)],
        out_specs=[pl.BlockSpec((1, TQ, CB), const), pl.BlockSpec((1, TQ, CB), const)],
        scratch_shapes=[pltpu.VMEM((G_KV, HG * TQ, 1), F32),
                        pltpu.VMEM((G_KV, HG * TQ, 1), F32),
                        pltpu.VMEM((G_KV, HG * TQ, HEAD_B), F32)],
    )
    return pl.pallas_call(
        functools.partial(_nsa_selwin_sample_kernel, PGS=PGS, PS=PS, HG=HG, TQ=TQ, pos0=pos0,
                          n_new=n_new, n_win=n_win),
        grid_spec=grid_spec,
        out_shape=[o_sd, o_sd],
        compiler_params=_cparams(("parallel", "arbitrary")),
        name="nsa_selwin_sample",
    )(table, slopes, proj, selm, *([pool5] * PGS), new_rows, cwin)


def _rwkv_layer(hn, x_prev, s0, i, W):
    B, T, D = hn.shape
    N = B * T
    CA = W["w_out_a"].shape[1]
    H = CA // HEAD_A
    x_shift = jnp.concatenate([x_prev[:, None, :], hn[:, :-1]], axis=1)
    dx = x_shift - hn
    xm = [(hn + dx * W["mu_a"][i, j]).reshape(N, D) for j in range(6)]
    r, k, v, zg = (mm(xm[j], W["w_in_a"], (i, j)) for j in range(4))
    wl = mm(jnp.tanh(mm(xm[4], W["w_lora_w1"], (i,))), W["w_lora_w2"], (i,))
    w_log = -jax.nn.softplus(-(W["w0_a"][i] + wl)) - 0.5
    lw = -jnp.exp(w_log)
    a = jax.nn.sigmoid(W["a0_a"][i] + mm(mm(xm[5], W["a_lora1"], (i,)), W["a_lora2"], (i,)))
    kk = (k * W["k_k"][i]).reshape(N, H, HEAD_A)
    kk = (kk / jnp.maximum(jnp.sqrt(jnp.sum(kk * kk, axis=-1, keepdims=True)), 1e-12)).reshape(N, CA)
    k = k * (1.0 + (a - 1.0) * W["k_a"][i])
    sh = lambda z: z.reshape(B, T, CA)
    o, s_fin = rwkv_scan(sh(r), sh(lw), sh(k), sh(v), sh(-kk), sh(kk * a), s0)
    o = o.reshape(N, H, HEAD_A)
    mean = jnp.mean(o, axis=-1, keepdims=True)
    var = jnp.mean(jnp.square(o - mean), axis=-1, keepdims=True)
    o = ((o - mean) * lax.rsqrt(var + GN_EPS)).reshape(N, CA) * W["ln_x_w"][i] + W["ln_x_b"][i]
    rk = (r * k).reshape(N, H, HEAD_A) * W["r_k"][i]
    bonus = (jnp.sum(rk, axis=-1, keepdims=True) * v.reshape(N, H, HEAD_A)).reshape(N, CA)
    o = (o + bonus) * jax.nn.silu(zg)
    return mm(o, W["w_out_a"], (i,)).reshape(B, T, D), s_fin, hn[:, -1]


def _nsa_layer(hn, jb, shared, W, slopes):
    B, T, D = hn.shape
    N = B * T
    CB = W["w_out_b"].shape[1]
    HB = CB // HEAD_B
    proj = mm(hn.reshape(N, D), W["w_in_b"], (jb,))
    proj3 = proj.reshape(B, T, -1)
    if shared["past"] is None:
        tq = 64
        o_cmp, selm = nsa_cmp(proj3, shared["kvc"], slopes, tq=tq, nc=shared["nc"],
                              nsb=shared["nsb"], pos0=0)
        o_sel, o_win = nsa_selwin_prompt(proj3, shared["rows"], selm, slopes, tq=tq)
    else:
        TQ = SUBLANES
        projp = jnp.pad(proj3, ((0, 0), (0, TQ - T), (0, 0)))
        o_cmp, selm = nsa_cmp(projp, shared["kvc"], slopes, tq=TQ, nc=shared["nc"],
                              nsb=shared["nsb"], pos0=shared["pos0"])
        pool, table, cwin = shared["past"]
        o_sel, o_win = nsa_selwin_sample(projp, selm, pool, table, shared["new_rows"], cwin, slopes,
                                         pos0=shared["pos0"], n_new=T)
        o_cmp, o_sel, o_win = (z[:, :T] for z in (o_cmp, o_sel, o_win))
    z = proj[:, CB:4 * CB].reshape(N, 3, HB, HEAD_B)
    gate = jax.nn.sigmoid(proj[:, 4 * CB:]).reshape(N, 3, HB, 1)
    o = jnp.stack([o_cmp.reshape(N, HB, HEAD_B), o_sel.reshape(N, HB, HEAD_B), o_win.reshape(N, HB, HEAD_B)], axis=1)
    o = jnp.sum(gate * o * jax.nn.silu(z), axis=1).reshape(N, CB)
    return mm(o, W["w_out_b"], (jb,)).reshape(B, T, D)


def _trunk(x, p, pos0, wkv0, shift0, past, W, slopes):
    B, T, D = x.shape
    N = B * T
    depth = p.shape[0]
    n_a = W["w_in_a"].shape[0]
    GW = G_KV * HEAD_B
    h = x
    wkv_new, shift_new = [], []
    shared, kv_rows, win_state = None, None, None
    for i in range(depth):
        hn = rmsnorm(h.reshape(N, D), W["norm_g"][i]).reshape(B, T, D)
        if i < n_a:
            out, s_fin, last = _rwkv_layer(hn, shift0[i], wkv0[i], i, W)
            wkv_new.append(s_fin)
            shift_new.append(last)
        else:
            out = _nsa_layer(hn, i - n_a, shared, W, slopes)
        h = h + out
        ple = mm(p[i].reshape(N, -1), W["w_ple"], (i,))
        gate = jax.nn.sigmoid(mm(h.reshape(N, D), W["w_ple_gate"], (i,)))
        h = h + (ple * gate).reshape(B, T, D)
        if i == n_a - 1:
            rows = mm(rmsnorm(h.reshape(N, D), W["kv_norm_g"]), W["w_kv"]).reshape(B, T, 6 * GW)
            kv_rows = rows[:, :, :4 * GW].reshape(B, T, 4, G_KV, HEAD_B)
            win_new = rows[:, :, 4 * GW:].reshape(B, T, 2, G_KV, HEAD_B)
            if past is None:
                PS = 128
                pool = rows.reshape(B * T // PS, PS // S_CMP, S_CMP, 6 * G_KV, HEAD_B)
                table = jnp.arange(B * T // PS, dtype=jnp.int32).reshape(B, T // PS)
                t_all = T
                win_all = win_new
                shared = {"past": None, "rows": rows}
            else:
                pool, table, cwin = past
                PS = pool.shape[1] * S_CMP
                t_all = pos0 + T
                win_all = jnp.concatenate([cwin.reshape(B, -1, 2, G_KV, HEAD_B), win_new], axis=1)
                NN = LANES
                shared = {"past": past, "new_rows": jnp.pad(rows, ((0, 0), (0, NN - T), (0, 0)))}
            win_state = win_all[:, win_all.shape[1] - min(WINDOW, pos0 + T):]
            nc = (t_all - L_CMP) // S_CMP + 1
            assert nc < table.shape[1] * PS // S_CMP
            kvc = compress_kv(pool, table, W["pe_cmp"], W["w_cmp1"], W["w_cmp2"])
            shared.update(kvc=kvc, nc=nc, nsb=max(-(-t_all // L_SEL), TOPK_SEL), pos0=pos0)
    y = rmsnorm(h.reshape(N, D), W["final_norm_g"]).reshape(B, T, D)
    return y, jnp.stack(wkv_new), jnp.stack(shift_new), kv_rows, win_state


def kernel(x_prompt, x_sample, state_wkv, state_shift, cache_kv, cache_win_kv, page_table, p_prompt, p_sample, norm_g, mu_a, w_in_a, w_lora_w1, w_lora_w2, w0_a, a_lora1, a_lora2, a0_a, k_k, k_a, r_k, ln_x_w, ln_x_b, w_out_a, w_in_b, w_out_b, kv_norm_g, w_kv, pe_cmp, w_cmp1, w_cmp2, w_ple, w_ple_gate, final_norm_g):
    bf = lambda w: w.astype(BF16)
    CA = w_out_a.shape[1]
    W = dict(norm_g=norm_g, mu_a=mu_a, w_in_a=bf(w_in_a), w_lora_w1=bf(w_lora_w1), w_lora_w2=bf(w_lora_w2),
             w0_a=w0_a, a_lora1=bf(a_lora1), a_lora2=bf(a_lora2), a0_a=a0_a, k_k=k_k, k_a=k_a,
             r_k=r_k.reshape(r_k.shape[0], 1, CA // HEAD_A, HEAD_A), ln_x_w=ln_x_w, ln_x_b=ln_x_b,
             w_out_a=bf(w_out_a), w_in_b=bf(w_in_b), w_out_b=bf(w_out_b), kv_norm_g=kv_norm_g, w_kv=bf(w_kv),
             pe_cmp=pe_cmp, w_cmp1=bf(w_cmp1), w_cmp2=bf(w_cmp2), w_ple=bf(w_ple), w_ple_gate=bf(w_ple_gate),
             final_norm_g=final_norm_g)
    HB = w_out_b.shape[1] // HEAD_B
    slopes = 2.0 ** (-8.0 * jnp.arange(1, HB + 1, dtype=F32) / HB)
    bp = x_prompt.shape[0]
    n_a = w_in_a.shape[0]
    D = x_prompt.shape[-1]
    wkv0 = jnp.zeros((n_a, bp, CA // HEAD_A, HEAD_A, HEAD_A), F32)
    shift0 = jnp.zeros((n_a, bp, D), F32)
    y_p, wkv_p, shift_p, kv_p, win_p = _trunk(x_prompt, p_prompt, 0, wkv0, shift0, None, W, slopes)
    db, n_pages = page_table.shape
    NP, PS = cache_kv.shape[:2]
    pool5 = cache_kv.reshape(NP, PS // S_CMP, S_CMP, -1, HEAD_B)
    past = (pool5, page_table, cache_win_kv.reshape(db, cache_win_kv.shape[1], -1))
    y_s, wkv_s, shift_s, kv_s, win_s = _trunk(x_sample, p_sample, n_pages * PS, state_wkv, state_shift, past, W, slopes)
    return (y_p, y_s, wkv_p, shift_p, kv_p, win_p, wkv_s, shift_s, kv_s, win_s)
```

```python
import functools
import math

import jax
import jax.numpy as jnp
from jax import lax
from jax.experimental import pallas as pl
from jax.experimental.pallas import tpu as pltpu

F32 = jnp.float32
BF16 = jnp.bfloat16

HEAD_A = 64
GN_EPS = 64e-5
HEAD_B = 128
G_KV = 4
L_CMP = 32
S_CMP = 16
L_SEL = 64
TOPK_SEL = 16
WINDOW = 512
RMS_EPS = 1e-6
NEG = -1e30
FORCE_BONUS = 1e4

LANES = 128
SUBLANES = 8
VMEM_LIMIT = 56 * 1024 * 1024

SCAN_NH = 2
SCAN_C = 64
SCAN_GP = 8

NT_DIMS = (((1,), (1,)), ((), ()))
TN_DIMS = (((0,), (0,)), ((), ()))


def _cparams(sem):
    return pltpu.CompilerParams(dimension_semantics=sem, vmem_limit_bytes=VMEM_LIMIT)


def _bdot(a, b):
    return jnp.dot(a.astype(BF16), b.astype(BF16), preferred_element_type=F32)


def _bdot_nt(a, b):
    return lax.dot_general(a.astype(BF16), b.astype(BF16), NT_DIMS, preferred_element_type=F32)


def _bdot_tn(a, b):
    return lax.dot_general(a.astype(BF16), b.astype(BF16), TN_DIMS, preferred_element_type=F32)


def _rms_kernel(x_ref, g_ref, o_ref):
    x = x_ref[...]
    ms = jnp.mean(x * x, axis=-1, keepdims=True)
    o_ref[...] = x * lax.rsqrt(ms + RMS_EPS) * g_ref[...]


def rmsnorm(x, g):
    M, D = x.shape
    tm = min(M, 256)
    return pl.pallas_call(
        _rms_kernel,
        grid=(pl.cdiv(M, tm),),
        in_specs=[pl.BlockSpec((tm, D), lambda i: (i, 0)),
                  pl.BlockSpec((1, D), lambda i: (0, 0))],
        out_specs=pl.BlockSpec((tm, D), lambda i: (i, 0)),
        out_shape=jax.ShapeDtypeStruct((M, D), F32),
        compiler_params=_cparams(("parallel",)),
        name="rmsnorm",
    )(x, g.reshape(1, D))


def _mm_kernel(x_ref, w_ref, o_ref):
    o_ref[...] = jnp.dot(x_ref[...].astype(BF16), w_ref[...], preferred_element_type=F32)


def mm(x, w, widx=()):
    M, K = x.shape
    N = w.shape[-1]
    assert w.shape[-2] == K and len(widx) == w.ndim - 2
    tm = min(M, 1024 if K <= 2048 else 512)
    tn = N if N <= 512 else 512
    nlead = len(widx)
    w_spec = pl.BlockSpec((None,) * nlead + (K, tn), lambda i, j: tuple(widx) + (0, j))
    return pl.pallas_call(
        _mm_kernel,
        grid=(pl.cdiv(M, tm), pl.cdiv(N, tn)),
        in_specs=[pl.BlockSpec((tm, K), lambda i, j: (i, 0)), w_spec],
        out_specs=pl.BlockSpec((tm, tn), lambda i, j: (i, j)),
        out_shape=jax.ShapeDtypeStruct((M, N), F32),
        compiler_params=_cparams(("parallel", "parallel")),
        name="mm",
    )(x, w)


def _norm_rows(x, g):
    return x * lax.rsqrt(jnp.mean(x * x, axis=-1, keepdims=True) + RMS_EPS) * g


def _sigmoid(x):
    return 1.0 / (1.0 + jnp.exp(-x))


def _norm_and_shift(h_ref, hprev_ref, xprev_ref, g_ref, i, tm, T):
    g = g_ref[...]
    hn = _norm_rows(h_ref[...], g)
    prev_row = _norm_rows(hprev_ref[SUBLANES - 1:SUBLANES, :], g)
    row = lax.broadcasted_iota(jnp.int32, (tm, 1), 0)
    xs = jnp.where(row == 0, prev_row, pltpu.roll(hn, 1, axis=0))
    if T >= tm:
        assert T % tm == 0
        start = (i * tm) % T == 0
        xs = jnp.where((row == 0) & start, xprev_ref[pl.ds((i * tm) // T, 1), :], xs)
    else:
        assert tm % T == 0
        for bb in range(tm // T):
            xs = jnp.where(row == bb * T, xprev_ref[pl.ds(i * (tm // T) + bb, 1), :], xs)
    return hn, xs


def _rwkv_in_kernel(h_ref, hprev_ref, xprev_ref, g_ref, mu_ref, w_ref, o_ref, xs_scr, *, tm, T):
    i, j, n = pl.program_id(0), pl.program_id(1), pl.program_id(2)

    @pl.when(n == 0)
    def _():
        hn, xs = _norm_and_shift(h_ref, hprev_ref, xprev_ref, g_ref, i, tm, T)
        xs_scr[...] = (hn + (xs - hn) * mu_ref[pl.ds(j, 1), :]).astype(BF16)

    o_ref[...] = jnp.dot(xs_scr[...], w_ref[...], preferred_element_type=F32)


def _shift_specs(tm, D, nb, ngrid):
    z = (0,) * (ngrid - 1)
    wrap = lambda f: (lambda i, *_: f(i))
    return [pl.BlockSpec((tm, D), wrap(lambda i: (i, 0))),
            pl.BlockSpec((SUBLANES, D), wrap(lambda i: (jnp.maximum(i * (tm // SUBLANES) - 1, 0), 0))),
            pl.BlockSpec((nb, D), wrap(lambda i: (0, 0))),
            pl.BlockSpec((1, D), wrap(lambda i: (0, 0))),
            pl.BlockSpec((6, D), wrap(lambda i: (0, 0)))]


def rwkv_in(h, xprev, g, mu, w, layer, T):
    N, D = h.shape
    C = w.shape[-1]
    tm = min(N, 1024)
    tn = 512
    return pl.pallas_call(
        functools.partial(_rwkv_in_kernel, tm=tm, T=T),
        grid=(N // tm, 4, C // tn),
        in_specs=_shift_specs(tm, D, xprev.shape[0], 3) + [
            pl.BlockSpec((None, None, D, tn), lambda i, j, n: (layer, j, 0, n))],
        out_specs=pl.BlockSpec((None, tm, tn), lambda i, j, n: (j, i, n)),
        out_shape=jax.ShapeDtypeStruct((4, N, C), F32),
        scratch_shapes=[pltpu.VMEM((tm, D), BF16)],
        compiler_params=_cparams(("parallel", "arbitrary", "arbitrary")),
        name="rwkv_in",
    )(h, h, xprev, g.reshape(1, D), mu, w)


def _rwkv_lora_kernel(h_ref, hprev_ref, xprev_ref, g_ref, mu_ref, lw1_ref, lw2_ref, la1_ref, la2_ref,
                      w0_ref, a0_ref, lw_ref, a_ref, *, tm, T):
    hn, xs = _norm_and_shift(h_ref, hprev_ref, xprev_ref, g_ref, pl.program_id(0), tm, T)
    dx = xs - hn
    x4 = (hn + dx * mu_ref[4:5, :]).astype(BF16)
    x5 = (hn + dx * mu_ref[5:6, :]).astype(BF16)
    t4 = jnp.tanh(jnp.dot(x4, lw1_ref[...], preferred_element_type=F32)).astype(BF16)
    y = -(w0_ref[...] + jnp.dot(t4, lw2_ref[...], preferred_element_type=F32))
    softplus = jnp.maximum(y, 0.0) + jnp.log(1.0 + jnp.exp(-jnp.abs(y)))
    lw_ref[...] = -jnp.exp(-softplus - 0.5)
    t5 = jnp.dot(x5, la1_ref[...], preferred_element_type=F32).astype(BF16)
    a_ref[...] = _sigmoid(a0_ref[...] + jnp.dot(t5, la2_ref[...], preferred_element_type=F32))


def rwkv_lora(h, xprev, g, mu, lw1, lw2, la1, la2, w0, a0, T):
    N, D = h.shape
    R, C = lw2.shape
    tm = min(N, 256)
    full = lambda shape: pl.BlockSpec(shape, lambda i: (0,) * len(shape))
    o_spec = pl.BlockSpec((tm, C), lambda i: (i, 0))
    return pl.pallas_call(
        functools.partial(_rwkv_lora_kernel, tm=tm, T=T),
        grid=(N // tm,),
        in_specs=_shift_specs(tm, D, xprev.shape[0], 1) + [
            full((D, R)), full((R, C)), full((D, R)), full((R, C)), full((1, C)), full((1, C))],
        out_specs=[o_spec, o_spec],
        out_shape=[jax.ShapeDtypeStruct((N, C), F32)] * 2,
        compiler_params=_cparams(("parallel",)),
        name="rwkv_lora",
    )(h, h, xprev, g.reshape(1, D), mu, lw1, lw2, la1, la2, w0.reshape(1, C), a0.reshape(1, C))


def _norm_mm_kernel(h_ref, g_ref, w_ref, o_ref, xs_scr):
    @pl.when(pl.program_id(1) == 0)
    def _():
        xs_scr[...] = _norm_rows(h_ref[...], g_ref[...]).astype(BF16)

    o_ref[...] = jnp.dot(xs_scr[...], w_ref[...], preferred_element_type=F32)


def norm_mm(h, g, w, widx=()):
    N, D = h.shape
    NO = w.shape[-1]
    tm = min(N, 1024)
    tn = 512
    nlead = len(widx)
    return pl.pallas_call(
        _norm_mm_kernel,
        grid=(N // tm, pl.cdiv(NO, tn)),
        in_specs=[pl.BlockSpec((tm, D), lambda i, n: (i, 0)),
                  pl.BlockSpec((1, D), lambda i, n: (0, 0)),
                  pl.BlockSpec((None,) * nlead + (D, tn), lambda i, n: tuple(widx) + (0, n))],
        out_specs=pl.BlockSpec((tm, tn), lambda i, n: (i, n)),
        out_shape=jax.ShapeDtypeStruct((N, NO), F32),
        scratch_shapes=[pltpu.VMEM((tm, D), BF16)],
        compiler_params=_cparams(("parallel", "arbitrary")),
        name="norm_mm",
    )(h, g.reshape(1, D), w)


def _out_ple_kernel(x_ref, h_ref, p_ref, wo_ref, wp_ref, wg_ref, o_ref, h1_scr, h1b_scr, *, NT):
    n = pl.program_id(1)

    @pl.when(n < NT)
    def _():
        h1 = h_ref[...] + jnp.dot(x_ref[...].astype(BF16), wo_ref[...], preferred_element_type=F32)
        h1_scr[n] = h1
        h1b_scr[n] = h1.astype(BF16)

    @pl.when(n >= NT)
    def _():
        h1b = jnp.concatenate([h1b_scr[t] for t in range(NT)], axis=1)
        gate = jnp.dot(h1b, wg_ref[...], preferred_element_type=F32)
        ple = jnp.dot(p_ref[...].astype(BF16), wp_ref[...], preferred_element_type=F32)
        o_ref[...] = h1_scr[n - NT] + ple * _sigmoid(gate)


def out_ple(x, h, p, w_out, oidx, w_ple, w_gate, layer):
    N, C = x.shape
    D = h.shape[1]
    DP = p.shape[1]
    tm = min(N, 512)
    tn = 512
    NT = D // tn
    lo = lambda n: jnp.minimum(n, NT - 1)
    hi = lambda n: jnp.maximum(n - NT, 0)
    return pl.pallas_call(
        functools.partial(_out_ple_kernel, NT=NT),
        grid=(N // tm, 2 * NT),
        in_specs=[pl.BlockSpec((tm, C), lambda i, n: (i, 0)),
                  pl.BlockSpec((tm, tn), lambda i, n: (i, lo(n))),
                  pl.BlockSpec((tm, DP), lambda i, n: (i, 0)),
                  pl.BlockSpec((None, C, tn), lambda i, n: (oidx, 0, lo(n))),
                  pl.BlockSpec((None, DP, tn), lambda i, n: (layer, 0, hi(n))),
                  pl.BlockSpec((None, D, tn), lambda i, n: (layer, 0, hi(n)))],
        out_specs=pl.BlockSpec((tm, tn), lambda i, n: (i, hi(n))),
        out_shape=jax.ShapeDtypeStruct((N, D), F32),
        scratch_shapes=[pltpu.VMEM((NT, tm, tn), F32), pltpu.VMEM((NT, tm, tn), BF16)],
        compiler_params=_cparams(("parallel", "arbitrary")),
        name="out_ple",
    )(x, h, p, w_out, w_ple, w_gate)


def _scan_kernel(r_ref, k_ref, v_ref, zg_ref, lw_ref, a_ref, pv_ref, s0_ref, o_ref, sfin_ref, s_scr, *, C, NH, GP):
    L = NH * HEAD_A
    NC = NH * C
    ci = pl.program_id(2)

    @pl.when(ci == 0)
    def _():
        s_scr[...] = s0_ref[0]

    row_c = lax.broadcasted_iota(jnp.int32, (C, NC), 0)
    col_s = lax.broadcasted_iota(jnp.int32, (C, NC), 1) % C
    tri_strict = col_s < row_c
    tri_incl = (lax.broadcasted_iota(jnp.int32, (C, 2 * NC), 1) % C
                <= lax.broadcasted_iota(jnp.int32, (C, 2 * NC), 0))
    st_mask = (lax.broadcasted_iota(jnp.int32, (NC, L), 0) // C
               == lax.broadcasted_iota(jnp.int32, (NC, L), 1) // HEAD_A)
    bd_mask = (lax.broadcasted_iota(jnp.int32, (NC, NC), 0) // C
               == lax.broadcasted_iota(jnp.int32, (NC, NC), 1) // C)
    head_mask = (lax.broadcasted_iota(jnp.int32, (L, L), 0) // HEAD_A
                 == lax.broadcasted_iota(jnp.int32, (L, L), 1) // HEAD_A)
    cum_mat = (lax.broadcasted_iota(jnp.int32, (C, C), 1)
               <= lax.broadcasted_iota(jnp.int32, (C, C), 0)).astype(F32)

    def st(x):
        return jnp.where(st_mask, jnp.concatenate([x] * NH, axis=0), 0.0)

    def bd(w):
        return jnp.where(bd_mask, jnp.concatenate([w] * NH, axis=0), 0.0)

    n_double = int(math.log2(C))
    each = lambda f, *cols: [f(*xs) for xs in zip(*cols)]
    sls = [slice(gp * L, (gp + 1) * L) for gp in range(GP)]
    head_of_lane = lax.broadcasted_iota(jnp.int32, (1, L), 1) // HEAD_A

    def hsum(x):
        out = None
        for hh in range(NH):
            sh = jnp.sum(jnp.where(head_of_lane == hh, x, 0.0), axis=-1, keepdims=True)
            out = sh if out is None else jnp.where(head_of_lane == hh, sh, out)
        return out

    k_k, k_a, r_k, ln_w, ln_b = ([pv_ref[n:n + 1, sl] for sl in sls] for n in range(5))
    lw = [lw_ref[0, :, sl] for sl in sls]
    a_sig = [a_ref[0, :, sl] for sl in sls]
    r = [r_ref[0, :, sl] for sl in sls]
    k_raw = [k_ref[0, :, sl] for sl in sls]
    v = [v_ref[0, :, sl] for sl in sls]
    kk = each(lambda x, w: x * w, k_raw, k_k)
    kk = each(lambda x: x / jnp.maximum(jnp.sqrt(hsum(x * x)), 1e-12), kk)
    k = each(lambda x, a, w: x * (1.0 + (a - 1.0) * w), k_raw, a_sig, k_a)
    cum = each(lambda z: jnp.dot(cum_mat, z, precision=lax.Precision.HIGHEST, preferred_element_type=F32), lw)
    p_incl = each(jnp.exp, cum)
    p_inv = each(lambda z: jnp.exp(-z), cum)
    at = each(lambda x, c, w: -x * jnp.exp(c - w), kk, cum, lw)
    rt = each(lambda x, p: x * p, r, p_incl)
    bt = each(lambda x, a, p: x * a * p, kk, a_sig, p_inv)
    kt = each(lambda x, p: x * p, k, p_inv)
    S = [s_scr[gp] for gp in range(GP)]
    ar = each(lambda x, y: jnp.concatenate([x, y], axis=0), at, rt)
    bk_st = each(lambda x, y: jnp.concatenate([st(x), st(y)], axis=0), bt, kt)
    Gm = each(_bdot_nt, ar, bk_st)
    w_ab = [jnp.where(tri_strict, g[:C, :NC], 0.0) for g in Gm]
    tm = w_ab
    pw = each(lambda w: _bdot(w, bd(w)), w_ab)
    LH = each(_bdot_nt, ar, S)
    v_st = each(st, v)
    x = [lh[:C] + _bdot(jnp.where(tri_strict, g[:C, NC:], 0.0), vs) for lh, g, vs in zip(LH, Gm, v_st)]
    for it in range(1, n_double):
        tm_next = each(lambda t, p: t + p + _bdot(p, bd(t)), tm, pw)
        if it < n_double - 1:
            pw = each(lambda p: _bdot(p, bd(p)), pw)
        tm = tm_next
    u = each(lambda xx, t: xx + _bdot(t, st(xx)), x, tm)
    o = [lh[C:] + _bdot(jnp.where(tri_incl, g[C:], 0.0), jnp.concatenate([st(uu), vs], axis=0))
         for lh, g, uu, vs in zip(LH, Gm, u, v_st)]
    inv_n = 1.0 / HEAD_A
    dev = each(lambda x: x - hsum(x) * inv_n, o)
    gn = each(lambda d, w, b_: d * lax.rsqrt(hsum(d * d) * inv_n + GN_EPS) * w + b_, dev, ln_w, ln_b)
    bonus = each(lambda rr, kx, w, vv: hsum(rr * kx * w) * vv, r, k, r_k, v)
    for sl, y, bo in zip(sls, gn, bonus):
        zg = zg_ref[0, :, sl]
        o_ref[0, :, sl] = ((y + bo) * (zg * _sigmoid(zg))).astype(o_ref.dtype)
    ds = [_bdot_tn(jnp.concatenate([uu, vv], axis=0), jnp.concatenate([b_, k_], axis=0))
          for uu, vv, b_, k_ in zip(u, v, bt, kt)]
    for gp in range(GP):
        s_scr[gp] = (S[gp] + jnp.where(head_mask, ds[gp], 0.0)) * p_incl[gp][C - 1:C, :]

    @pl.when(ci == pl.num_programs(2) - 1)
    def _():
        sfin_ref[0] = s_scr[...]


def rwkv_scan(rkvg, lw, a, pvec, s0):
    _, B, T, CA = rkvg.shape
    H = CA // HEAD_A
    NH, C, GP = SCAN_NH, SCAN_C, SCAN_GP
    assert NH * C == LANES and H % (NH * GP) == 0
    L = NH * HEAD_A
    NG = H // NH
    Tp = -(-T // C) * C
    if Tp != T:
        rkvg = jnp.pad(rkvg, ((0, 0), (0, 0), (0, Tp - T), (0, 0)))
        lw, a = (jnp.pad(z, ((0, 0), (0, Tp - T), (0, 0))) for z in (lw, a))
    eye = jnp.eye(NH, dtype=F32)
    s0_bd = (s0.reshape(B, NG, NH, HEAD_A, 1, HEAD_A) * eye[None, None, :, None, :, None]).reshape(B, NG, L, L)
    seq_spec = pl.BlockSpec((1, C, GP * L), lambda bi, gi, ci: (bi, ci, gi))
    proj_spec = lambda j: pl.BlockSpec((None, 1, C, GP * L), lambda bi, gi, ci: (j, bi, ci, gi))
    st_spec = pl.BlockSpec((1, GP, L, L), lambda bi, gi, ci: (bi, gi, 0, 0))
    o, sfin = pl.pallas_call(
        functools.partial(_scan_kernel, C=C, NH=NH, GP=GP),
        grid=(B, NG // GP, Tp // C),
        in_specs=[proj_spec(j) for j in range(4)] + [seq_spec, seq_spec,
                  pl.BlockSpec((5, GP * L), lambda bi, gi, ci: (0, gi)), st_spec],
        out_specs=[seq_spec, st_spec],
        out_shape=[jax.ShapeDtypeStruct((B, Tp, CA), BF16), jax.ShapeDtypeStruct((B, NG, L, L), F32)],
        scratch_shapes=[pltpu.VMEM((GP, L, L), F32)],
        compiler_params=_cparams(("parallel", "parallel", "arbitrary")),
        name="rwkv_scan",
    )(rkvg, rkvg, rkvg, rkvg, lw, a, pvec, s0_bd)
    sf = sfin.reshape(B, NG, NH, HEAD_A, NH, HEAD_A)
    s_fin = jnp.stack([sf[:, :, h, :, h, :] for h in range(NH)], axis=2).reshape(B, H, HEAD_A, HEAD_A)
    return o[:, :T], s_fin


def _gelu_tanh(x):
    c = math.sqrt(2.0 / math.pi)
    return 0.5 * x * (1.0 + jnp.tanh(c * (x + 0.044715 * (x * x * x))))


def _compress_kernel(pt_ref, *refs, PGS):
    del pt_ref
    page_refs = refs[:PGS]
    next_ref, pe_ref, w1_ref, w2_ref, out_ref = refs[PGS:]
    CPP = page_refs[0].shape[1]
    NCH = PGS * CPP
    CG = 2 * G_KV
    M = (NCH + 1) * CG

    def rows_of(l, hf):
        pe = pe_ref[hf, l]
        parts = [(page_refs[i][0, :, l] + pe[None]).reshape(CPP * CG, HEAD_B) for i in range(PGS)]
        parts.append(next_ref[0, 0, l] + pe)
        return jnp.concatenate(parts, axis=0)

    top = jnp.zeros((M, 2 * HEAD_B), F32)
    bot = jnp.zeros((M, 2 * HEAD_B), F32)
    for l in range(0, S_CMP, 2):
        wrows = pl.ds(l * HEAD_B, 2 * HEAD_B)
        xt = jnp.concatenate([rows_of(l, 0), rows_of(l + 1, 0)], axis=1).astype(BF16)
        top = top + jnp.dot(xt, w1_ref[0, wrows, :], preferred_element_type=F32)
        xb = jnp.concatenate([rows_of(l, 1), rows_of(l + 1, 1)], axis=1).astype(BF16)
        bot = bot + jnp.dot(xb, w1_ref[1, wrows, :], preferred_element_type=F32)
    is_k = (lax.broadcasted_iota(jnp.int32, (M, 1), 0) % CG) < G_KV
    pick = lambda z, n: jnp.where(is_k[:n], z[:n, :HEAD_B], z[:n, HEAD_B:])
    hcur = pick(top, NCH * CG) + pick(bot, M)[CG:]
    o2 = jnp.dot(_gelu_tanh(hcur).astype(BF16), w2_ref[...], preferred_element_type=F32)
    out_ref[0] = pick(o2, NCH * CG).reshape(NCH, CG, HEAD_B)


def compress_kv(pool5, table, pe_cmp, w1, w2):
    NP, CPP = pool5.shape[:2]
    B, n_pages = table.shape
    PGS = max(d for d in (8, 4, 2, 1) if n_pages % d == 0)
    NCH = PGS * CPP
    CG = 2 * G_KV
    half = S_CMP * HEAD_B
    pe_r = jnp.repeat(pe_cmp.reshape(2, 2, S_CMP, HEAD_B).transpose(1, 2, 0, 3), G_KV, axis=2)
    w1_r = w1.reshape(2, 2, half, HEAD_B).transpose(1, 2, 0, 3).reshape(2, half, 2 * HEAD_B)
    w2_r = jnp.concatenate([w2[0], w2[1]], axis=1)

    def page_map(i):
        return lambda b, s, pt: (pt[b, s * PGS + i], 0, 0, 0, 0)

    def next_map(b, s, pt):
        return (pt[b, jnp.minimum((s + 1) * PGS, n_pages - 1)], 0, 0, 0, 0)

    const = lambda n: (lambda b, s, pt: (0,) * n)
    grid_spec = pltpu.PrefetchScalarGridSpec(
        num_scalar_prefetch=1,
        grid=(B, n_pages // PGS),
        in_specs=[pl.BlockSpec((1, CPP, S_CMP, CG, HEAD_B), page_map(i)) for i in range(PGS)] + [
            pl.BlockSpec((1, 1, S_CMP, CG, HEAD_B), next_map),
            pl.BlockSpec((2, S_CMP, CG, HEAD_B), const(4)),
            pl.BlockSpec((2, half, 2 * HEAD_B), const(3)),
            pl.BlockSpec((HEAD_B, 2 * HEAD_B), const(2)),
        ],
        out_specs=pl.BlockSpec((1, NCH, CG, HEAD_B), lambda b, s, pt: (b, s, 0, 0)),
    )
    return pl.pallas_call(
        functools.partial(_compress_kernel, PGS=PGS),
        grid_spec=grid_spec,
        out_shape=jax.ShapeDtypeStruct((B, n_pages * CPP, CG, HEAD_B), F32),
        compiler_params=_cparams(("parallel", "arbitrary")),
        name="compress_kv",
    )(table, *([pool5] * PGS), pool5, pe_r, w1_r, w2_r)


def _stack_heads(q, HG):
    return jnp.concatenate([q[:, h * HEAD_B:(h + 1) * HEAD_B] for h in range(HG)], axis=0)


def _masked_softmax_rows(s, mask):
    s = jnp.where(mask, s, NEG)
    m = jnp.max(s, axis=-1, keepdims=True)
    e = jnp.where(mask, jnp.exp(s - m), 0.0)
    l = jnp.sum(e, axis=-1, keepdims=True)
    return e / jnp.where(l > 0.0, l, 1.0)


def _attend_stacked(s, dist, mask, v, slopes_ref, g, HG, tq):
    ps = []
    psum = jnp.zeros(dist.shape, F32)
    for h in range(HG):
        p = _masked_softmax_rows(s[h * tq:(h + 1) * tq] - slopes_ref[g * HG + h] * dist, mask)
        psum = psum + p
        ps.append(p.astype(BF16))
    o = jnp.dot(jnp.concatenate(ps, axis=0), v.astype(BF16), preferred_element_type=F32)
    return o, psum


def _unstack_store(o_ref, o, HG, tq):
    for h in range(HG):
        o_ref[0, :, h * HEAD_B:(h + 1) * HEAD_B] = o[h * tq:(h + 1) * tq]


def _nsa_cmp_kernel(slopes_ref, q_ref, kc_ref, vc_ref, o_ref, selm_ref, *, tq, HG, nc, nsb, pos0):
    g = pl.program_id(1)
    qt = pl.program_id(2)
    NCp = kc_ref.shape[1]
    NSBp = selm_ref.shape[3]
    q_st = _stack_heads(q_ref[0] * (HEAD_B ** -0.5), HG)
    s = _bdot_nt(q_st, kc_ref[0])
    qpos = pos0 + qt * tq + lax.broadcasted_iota(jnp.int32, (tq, 1), 0)
    cidx = lax.broadcasted_iota(jnp.int32, (1, NCp), 1)
    cend = S_CMP * cidx + (L_CMP - 1)
    mask = (cend <= qpos) & (cidx < nc)
    dist = (qpos - cend).astype(F32)
    o, imp_c = _attend_stacked(s, dist, mask, vc_ref[0], slopes_ref, g, HG, tq)
    _unstack_store(o_ref, o, HG, tq)

    crow = lax.broadcasted_iota(jnp.int32, (NCp, NSBp), 0)
    jcol = lax.broadcasted_iota(jnp.int32, (NCp, NSBp), 1)
    overlap = ((S_CMP * crow < L_SEL * (jcol + 1)) & (S_CMP * crow + L_CMP > L_SEL * jcol)
               & (crow < nc)).astype(F32)
    imp = jnp.dot(imp_c, overlap, precision=lax.Precision.HIGHEST, preferred_element_type=F32)
    lane = lax.broadcasted_iota(jnp.int32, (tq, NSBp), 1)
    cur = jnp.right_shift(qpos, int(math.log2(L_SEL)))
    forced = ((lane == 0) | (lane == cur) | (lane == cur - 1)).astype(F32)
    score = jnp.where(lane <= cur, imp + FORCE_BONUS * forced, NEG)
    score = jnp.where(lane < nsb, score, -3e38)

    cnt = jnp.zeros((tq, NSBp), F32)
    for i in range(nsb):
        col = score[:, i:i + 1]
        beats = (col > score) | ((col == score) & (lane > i))
        cnt = cnt + jnp.where(beats, 1.0, 0.0)
    sel = (cnt < TOPK_SEL) & (score > 0.5 * NEG)
    selm_ref[0, 0] = sel.astype(F32)


def nsa_cmp(proj, kvc, slopes, *, tq, nc, nsb, pos0):
    B, T, _ = proj.shape
    HG = slopes.shape[0] // G_KV
    NCp = kvc.shape[1]
    NSBp = -(-nsb // LANES) * LANES
    gw = HG * HEAD_B
    return pl.pallas_call(
        functools.partial(_nsa_cmp_kernel, tq=tq, HG=HG, nc=nc, nsb=nsb, pos0=pos0),
        grid=(B, G_KV, T // tq),
        in_specs=[pl.BlockSpec(memory_space=pltpu.SMEM),
                  pl.BlockSpec((1, tq, gw), lambda b, g, t: (b, t, g)),
                  pl.BlockSpec((1, NCp, HEAD_B), lambda b, g, t: (b, 0, g)),
                  pl.BlockSpec((1, NCp, HEAD_B), lambda b, g, t: (b, 0, G_KV + g))],
        out_specs=[pl.BlockSpec((1, tq, gw), lambda b, g, t: (b, t, g)),
                   pl.BlockSpec((1, 1, tq, NSBp), lambda b, g, t: (b, g, t, 0))],
        out_shape=[jax.ShapeDtypeStruct((B, T, G_KV * gw), F32),
                   jax.ShapeDtypeStruct((B, G_KV, T, NSBp), F32)],
        compiler_params=_cparams(("parallel", "parallel", "parallel")),
        name="nsa_cmp",
    )(slopes, proj, kvc.reshape(B, NCp, -1), kvc.reshape(B, NCp, -1))


def _silu(x):
    return x * _sigmoid(x)


def _nsa_selwin_prompt_kernel(slopes_ref, q_ref, selm_ref, ks_ref, vs_ref, kw_ref, vw_ref,
                              ocmp_ref, zc_ref, zs_ref, zw_ref, gate_ref, o_ref, *, tq, HG, T, WS):
    g = pl.program_id(1)
    qt = pl.program_id(2)
    NSBp = selm_ref.shape[3]
    q_st = _stack_heads(q_ref[0] * (HEAD_B ** -0.5), HG).astype(BF16)
    qpos = qt * tq + lax.broadcasted_iota(jnp.int32, (tq, 1), 0)

    kpos = lax.broadcasted_iota(jnp.int32, (1, T), 1)
    expand = (jnp.right_shift(lax.broadcasted_iota(jnp.int32, (NSBp, T), 1), int(math.log2(L_SEL)))
              == lax.broadcasted_iota(jnp.int32, (NSBp, T), 0)).astype(BF16)
    in_blk = jnp.dot(selm_ref[0, 0].astype(BF16), expand, preferred_element_type=F32) > 0.5
    dist = qpos - kpos
    s = _bdot_nt(q_st, ks_ref[0])
    o, _ = _attend_stacked(s, dist.astype(F32), in_blk & (dist >= 0), vs_ref[0], slopes_ref, g, HG, tq)

    start = pl.multiple_of(jnp.clip(qt * tq - WINDOW, 0, T - WS), SUBLANES)
    kposw = start + lax.broadcasted_iota(jnp.int32, (1, WS), 1)
    distw = qpos - kposw
    sw = _bdot_nt(q_st, kw_ref[0, pl.ds(start, WS), :])
    ow, _ = _attend_stacked(sw, distw.astype(F32), (distw >= 0) & (distw < WINDOW),
                            vw_ref[0, pl.ds(start, WS), :], slopes_ref, g, HG, tq)

    HB = G_KV * HG
    gates = _sigmoid(gate_ref[0])
    lane = lax.broadcasted_iota(jnp.int32, gates.shape, 1)
    gate_col = lambda idx: jnp.sum(jnp.where(lane == idx, gates, 0.0), axis=1, keepdims=True)
    for h in range(HG):
        hs = slice(h * HEAD_B, (h + 1) * HEAD_B)
        rs = slice(h * tq, (h + 1) * tq)
        hd = g * HG + h
        y = (gate_col(hd) * ocmp_ref[0, :, hs] * _silu(zc_ref[0, :, hs])
             + gate_col(HB + hd) * o[rs] * _silu(zs_ref[0, :, hs])
             + gate_col(2 * HB + hd) * ow[rs] * _silu(zw_ref[0, :, hs]))
        o_ref[0, :, hs] = y.astype(o_ref.dtype)


def nsa_selwin_prompt(proj, rows, selm, o_cmp, slopes, *, tq):
    B, T, _ = proj.shape
    HG = slopes.shape[0] // G_KV
    NSBp = selm.shape[3]
    gw = HG * HEAD_B
    CB = G_KV * gw
    WS = min(T, WINDOW + tq)
    kv_spec = lambda c: pl.BlockSpec((1, T, HEAD_B), lambda b, g, t: (b, 0, c * G_KV + g))
    head_spec = lambda blk: pl.BlockSpec((1, tq, gw), lambda b, g, t: (b, t, blk * G_KV + g))
    return pl.pallas_call(
        functools.partial(_nsa_selwin_prompt_kernel, tq=tq, HG=HG, T=T, WS=WS),
        grid=(B, G_KV, T // tq),
        in_specs=[pl.BlockSpec(memory_space=pltpu.SMEM),
                  head_spec(0),
                  pl.BlockSpec((1, 1, tq, NSBp), lambda b, g, t: (b, g, t, 0)),
                  kv_spec(2), kv_spec(3), kv_spec(4), kv_spec(5),
                  head_spec(0), head_spec(1), head_spec(2), head_spec(3),
                  pl.BlockSpec((1, tq, LANES), lambda b, g, t: (b, t, 4 * CB // LANES))],
        out_specs=head_spec(0),
        out_shape=jax.ShapeDtypeStruct((B, T, CB), BF16),
        compiler_params=_cparams(("parallel", "parallel", "parallel")),
        name="nsa_selwin_prompt",
    )(slopes, proj, selm, rows, rows, rows, rows, o_cmp, proj, proj, proj, proj)


def _nsa_selwin_sample_kernel(pt_ref, slopes_ref, q_ref, selm_ref, *refs, PGS, PS, HG, TQ, pos0, n_new, n_win):
    del pt_ref
    page_refs = refs[:PGS]
    (new_ref, cwin_ref, ocmp_ref, zc_ref, zs_ref, zw_ref, gate_ref, o_ref,
     m_scr, l_scr, acc_scr) = refs[PGS:]
    st = pl.program_id(1)
    NSBp = selm_ref.shape[3]
    GW = G_KV * HEAD_B
    sel_shift = int(math.log2(L_SEL))

    @pl.when(st == 0)
    def _():
        m_scr[...] = jnp.full(m_scr.shape, NEG, F32)
        l_scr[...] = jnp.zeros(l_scr.shape, F32)
        acc_scr[...] = jnp.zeros(acc_scr.shape, F32)

    qpos = pos0 + lax.broadcasted_iota(jnp.int32, (TQ, 1), 0)
    lane_j = lax.broadcasted_iota(jnp.int32, (TQ, NSBp), 1)
    tile_heads = lambda x: jnp.concatenate([x] * HG, axis=0)

    def sel_col(selm_g, j):
        return jnp.sum(jnp.where(lane_j == j, selm_g, 0.0), axis=1, keepdims=True)

    def online_update(s, mask, v):
        gs = range(G_KV)
        m_old = [m_scr[g] for g in gs]
        m_new = [jnp.maximum(m_old[g], jnp.max(jnp.where(mask[g], s[g], NEG), axis=-1, keepdims=True)) for g in gs]
        e = [jnp.where(mask[g], jnp.exp(s[g] - m_new[g]), 0.0) for g in gs]
        alpha = [jnp.exp(m_old[g] - m_new[g]) for g in gs]
        pv = [_bdot(e[g], v[g]) for g in gs]
        for g in gs:
            l_scr[g] = alpha[g] * l_scr[g] + jnp.sum(e[g], axis=-1, keepdims=True)
            acc_scr[g] = alpha[g] * acc_scr[g] + pv[g]
            m_scr[g] = m_new[g]

    NK = PGS * PS
    kpos = st * NK + lax.broadcasted_iota(jnp.int32, (1, NK), 1)
    expand = (jnp.right_shift(st * NK + lax.broadcasted_iota(jnp.int32, (NSBp, NK), 1), sel_shift)
              == lax.broadcasted_iota(jnp.int32, (NSBp, NK), 0)).astype(BF16)
    dist = qpos - kpos
    distf = tile_heads(dist.astype(F32))
    gs = range(G_KV)
    q_st = [_stack_heads(q_ref[0, :, g * HG * HEAD_B:(g + 1) * HG * HEAD_B] * (HEAD_B ** -0.5), HG).astype(BF16)
            for g in gs]
    slope_col = [jnp.concatenate([jnp.full((TQ, 1), slopes_ref[g * HG + h], F32) for h in range(HG)], axis=0)
                 for g in gs]
    page_rows = lambda i, cg: page_refs[i][0, :, :, cg, :].reshape(PS, HEAD_B)
    k = [jnp.concatenate([page_rows(i, g) for i in range(PGS)], axis=0) for g in gs]
    v = [jnp.concatenate([page_rows(i, G_KV + g) for i in range(PGS)], axis=0) for g in gs]
    in_blk = [jnp.dot(selm_ref[0, g].astype(BF16), expand, preferred_element_type=F32) for g in gs]
    mask = [tile_heads((in_blk[g] > 0.5) & (dist >= 0)) for g in gs]
    s = [_bdot_nt(q_st[g], k[g]) - slope_col[g] * distf for g in gs]
    online_update(s, mask, v)

    @pl.when(st == pl.num_programs(1) - 1)
    def _():
        NN = new_ref.shape[1]
        rnew = lax.broadcasted_iota(jnp.int32, (1, NN), 1)
        kpos_n = pos0 + rnew
        dist_n = qpos - kpos_n
        ok_n = (rnew < n_new) & (dist_n >= 0)
        jn = pos0 >> sel_shift
        kpos_w = pos0 - n_win + lax.broadcasted_iota(jnp.int32, (1, n_win), 1)
        dist_w = qpos - kpos_w
        dist_wall = jnp.concatenate([dist_w, dist_n], axis=1)
        mask_wall = jnp.concatenate([(dist_w >= 0) & (dist_w < WINDOW), ok_n & (dist_n < WINDOW)], axis=1)
        kn = [new_ref[0, :, 2 * GW + g * HEAD_B:2 * GW + (g + 1) * HEAD_B] for g in gs]
        vn = [new_ref[0, :, 3 * GW + g * HEAD_B:3 * GW + (g + 1) * HEAD_B] for g in gs]
        mask_n = [tile_heads((sel_col(selm_ref[0, g], jn) > 0.5) & ok_n) for g in gs]
        dist_nf = tile_heads(dist_n.astype(F32))
        s_n = [_bdot_nt(q_st[g], kn[g]) - slope_col[g] * dist_nf for g in gs]
        online_update(s_n, mask_n, vn)
        kw = [jnp.concatenate([cwin_ref[0, :, g * HEAD_B:(g + 1) * HEAD_B],
                               new_ref[0, :, 4 * GW + g * HEAD_B:4 * GW + (g + 1) * HEAD_B]], axis=0) for g in gs]
        vw = [jnp.concatenate([cwin_ref[0, :, GW + g * HEAD_B:GW + (g + 1) * HEAD_B],
                               new_ref[0, :, 5 * GW + g * HEAD_B:5 * GW + (g + 1) * HEAD_B]], axis=0) for g in gs]
        dist_wf = tile_heads(dist_wall.astype(F32))
        mask_w = tile_heads(mask_wall)
        sw = [_bdot_nt(q_st[g], kw[g]) - slope_col[g] * dist_wf for g in gs]
        pw = [_masked_softmax_rows(sw[g], mask_w) for g in gs]
        ow = [_bdot(pw[g], vw[g]) for g in gs]
        HB = G_KV * HG
        gates = _sigmoid(gate_ref[0])
        for g in gs:
            l = l_scr[g]
            o = acc_scr[g] / jnp.where(l > 0.0, l, 1.0)
            for h in range(HG):
                hd = g * HG + h
                hs = slice(hd * HEAD_B, (hd + 1) * HEAD_B)
                rs = slice(h * TQ, (h + 1) * TQ)
                y = (gates[:, hd:hd + 1] * ocmp_ref[0, :, hs] * _silu(zc_ref[0, :, hs])
                     + gates[:, HB + hd:HB + hd + 1] * o[rs] * _silu(zs_ref[0, :, hs])
                     + gates[:, 2 * HB + hd:2 * HB + hd + 1] * ow[g][rs] * _silu(zw_ref[0, :, hs]))
                o_ref[0, :, hs] = y.astype(o_ref.dtype)


def nsa_selwin_sample(proj, selm, o_cmp, pool5, table, new_rows, cwin, slopes, *, pos0, n_new):
    B, TQ, _ = proj.shape
    HG = slopes.shape[0] // G_KV
    NP, CPP = pool5.shape[:2]
    PS = CPP * S_CMP
    n_pages = table.shape[1]
    NSBp = selm.shape[3]
    PGS = max(d for d in (8, 4, 2, 1) if n_pages % d == 0)
    CB = G_KV * HG * HEAD_B
    GW = G_KV * HEAD_B
    NN = new_rows.shape[1]
    n_win = cwin.shape[1]
    assert pos0 % L_SEL == 0 and n_new <= L_SEL and pos0 == n_pages * PS

    def page_map(i):
        return lambda b, s, pt: (pt[b, s * PGS + i], 0, 0, 1, 0)

    const = lambda b, s, pt: (b, 0, 0)
    wide = lambda blk: pl.BlockSpec((1, TQ, CB), lambda b, s, pt: (b, 0, blk))
    grid_spec = pltpu.PrefetchScalarGridSpec(
        num_scalar_prefetch=1,
        grid=(B, n_pages // PGS),
        in_specs=[pl.BlockSpec(memory_space=pltpu.SMEM),
                  pl.BlockSpec((1, TQ, CB), const),
                  pl.BlockSpec((1, G_KV, TQ, NSBp), lambda b, s, pt: (b, 0, 0, 0))]
                 + [pl.BlockSpec((1, CPP, S_CMP, 2 * G_KV, HEAD_B), page_map(i)) for i in range(PGS)]
                 + [pl.BlockSpec((1, NN, 6 * GW), const),
                    pl.BlockSpec((1, n_win, 2 * GW), const),
                    wide(0), wide(1), wide(2), wide(3),
                    pl.BlockSpec((1, TQ, LANES), lambda b, s, pt: (b, 0, 4 * CB // LANES))],
        out_specs=wide(0),
        scratch_shapes=[pltpu.VMEM((G_KV, HG * TQ, 1), F32),
                        pltpu.VMEM((G_KV, HG * TQ, 1), F32),
                        pltpu.VMEM((G_KV, HG * TQ, HEAD_B), F32)],
    )
    return pl.pallas_call(
        functools.partial(_nsa_selwin_sample_kernel, PGS=PGS, PS=PS, HG=HG, TQ=TQ, pos0=pos0,
                          n_new=n_new, n_win=n_win),
        grid_spec=grid_spec,
        out_shape=jax.ShapeDtypeStruct((B, TQ, CB), F32),
        compiler_params=_cparams(("parallel", "arbitrary")),
        name="nsa_selwin_sample",
    )(table, slopes, proj, selm, *([pool5] * PGS), new_rows, cwin, o_cmp, proj, proj, proj, proj)


def _rwkv_layer(h, x_prev, s0, i, W, B, T):
    N, D = h.shape
    g = W["norm_g"][i]
    rkvg = rwkv_in(h, x_prev, g, W["mu_a"][i], W["w_in_a"], i, T)
    CA = rkvg.shape[-1]
    lw, a = rwkv_lora(h, x_prev, g, W["mu_a"][i], W["w_lora_w1"][i], W["w_lora_w2"][i], W["a_lora1"][i],
                      W["a_lora2"][i], W["w0_a"][i], W["a0_a"][i], T)
    pvec = jnp.stack([W["k_k"][i], W["k_a"][i], W["r_k"][i].reshape(CA), W["ln_x_w"][i], W["ln_x_b"][i]])
    o, s_fin = rwkv_scan(rkvg.reshape(4, B, T, CA), lw.reshape(B, T, CA), a.reshape(B, T, CA), pvec, s0)
    last = rmsnorm(h.reshape(B, T, D)[:, -1], g)
    return o.reshape(N, CA), s_fin, last


def _nsa_layer(h, jb, shared, W, slopes, B, T, norm_g):
    N, D = h.shape
    CB = W["w_out_b"].shape[1]
    proj3 = norm_mm(h, norm_g, W["w_in_b"], (jb,)).reshape(B, T, -1)
    if shared["past"] is None:
        tq = 64
        o_cmp, selm = nsa_cmp(proj3, shared["kvc"], slopes, tq=tq, nc=shared["nc"],
                              nsb=shared["nsb"], pos0=0)
        o = nsa_selwin_prompt(proj3, shared["rows"], selm, o_cmp, slopes, tq=tq)
    else:
        TQ = SUBLANES
        projp = jnp.pad(proj3, ((0, 0), (0, TQ - T), (0, 0)))
        o_cmp, selm = nsa_cmp(projp, shared["kvc"], slopes, tq=TQ, nc=shared["nc"],
                              nsb=shared["nsb"], pos0=shared["pos0"])
        pool, table, cwin = shared["past"]
        o = nsa_selwin_sample(projp, selm, o_cmp, pool, table, shared["new_rows"], cwin, slopes,
                              pos0=shared["pos0"], n_new=T)[:, :T]
    return o.reshape(N, CB)


def _trunk(x, p, pos0, wkv0, shift0, past, W, slopes):
    B, T, D = x.shape
    N = B * T
    depth = p.shape[0]
    n_a = W["w_in_a"].shape[0]
    GW = G_KV * HEAD_B
    h = x.reshape(N, D)
    wkv_new, shift_new = [], []
    shared, kv_rows, win_state = None, None, None
    for i in range(depth):
        if i < n_a:
            o, s_fin, last = _rwkv_layer(h, shift0[i], wkv0[i], i, W, B, T)
            wkv_new.append(s_fin)
            shift_new.append(last)
            h = out_ple(o, h, p[i].reshape(N, -1), W["w_out_a"], i, W["w_ple"], W["w_ple_gate"], i)
        else:
            o = _nsa_layer(h, i - n_a, shared, W, slopes, B, T, W["norm_g"][i])
            h = out_ple(o, h, p[i].reshape(N, -1), W["w_out_b"], i - n_a, W["w_ple"], W["w_ple_gate"], i)
        if i == n_a - 1:
            rows = norm_mm(h, W["kv_norm_g"], W["w_kv"]).reshape(B, T, 6 * GW)
            kv_rows = rows[:, :, :4 * GW].reshape(B, T, 4, G_KV, HEAD_B)
            win_new = rows[:, :, 4 * GW:].reshape(B, T, 2, G_KV, HEAD_B)
            if past is None:
                PS = 128
                pool = rows.reshape(B * T // PS, PS // S_CMP, S_CMP, 6 * G_KV, HEAD_B)
                table = jnp.arange(B * T // PS, dtype=jnp.int32).reshape(B, T // PS)
                t_all = T
                win_all = win_new
                shared = {"past": None, "rows": rows}
            else:
                pool, table, cwin = past
                PS = pool.shape[1] * S_CMP
                t_all = pos0 + T
                win_all = jnp.concatenate([cwin.reshape(B, -1, 2, G_KV, HEAD_B), win_new], axis=1)
                NN = LANES
                shared = {"past": past, "new_rows": jnp.pad(rows, ((0, 0), (0, NN - T), (0, 0)))}
            win_state = win_all[:, win_all.shape[1] - min(WINDOW, pos0 + T):]
            nc = (t_all - L_CMP) // S_CMP + 1
            assert nc < table.shape[1] * PS // S_CMP
            kvc = compress_kv(pool, table, W["pe_cmp"], W["w_cmp1"], W["w_cmp2"])
            shared.update(kvc=kvc, nc=nc, nsb=max(-(-t_all // L_SEL), TOPK_SEL), pos0=pos0)
    y = rmsnorm(h, W["final_norm_g"]).reshape(B, T, D)
    return y, jnp.stack(wkv_new), jnp.stack(shift_new), kv_rows, win_state


def kernel(x_prompt, x_sample, state_wkv, state_shift, cache_kv, cache_win_kv, page_table, p_prompt, p_sample, norm_g, mu_a, w_in_a, w_lora_w1, w_lora_w2, w0_a, a_lora1, a_lora2, a0_a, k_k, k_a, r_k, ln_x_w, ln_x_b, w_out_a, w_in_b, w_out_b, kv_norm_g, w_kv, pe_cmp, w_cmp1, w_cmp2, w_ple, w_ple_gate, final_norm_g):
    bf = lambda w: w.astype(BF16)
    CA = w_out_a.shape[1]
    W = dict(norm_g=norm_g, mu_a=mu_a, w_in_a=bf(w_in_a), w_lora_w1=bf(w_lora_w1), w_lora_w2=bf(w_lora_w2),
             w0_a=w0_a, a_lora1=bf(a_lora1), a_lora2=bf(a_lora2), a0_a=a0_a, k_k=k_k, k_a=k_a,
             r_k=r_k, ln_x_w=ln_x_w, ln_x_b=ln_x_b,
             w_out_a=bf(w_out_a), w_in_b=bf(w_in_b), w_out_b=bf(w_out_b), kv_norm_g=kv_norm_g, w_kv=bf(w_kv),
             pe_cmp=pe_cmp, w_cmp1=bf(w_cmp1), w_cmp2=bf(w_cmp2), w_ple=bf(w_ple), w_ple_gate=bf(w_ple_gate),
             final_norm_g=final_norm_g)
    HB = w_out_b.shape[1] // HEAD_B
    slopes = 2.0 ** (-8.0 * jnp.arange(1, HB + 1, dtype=F32) / HB)
    bp = x_prompt.shape[0]
    n_a = w_in_a.shape[0]
    D = x_prompt.shape[-1]
    wkv0 = jnp.zeros((n_a, bp, CA // HEAD_A, HEAD_A, HEAD_A), F32)
    shift0 = jnp.zeros((n_a, bp, D), F32)
    y_p, wkv_p, shift_p, kv_p, win_p = _trunk(x_prompt, p_prompt, 0, wkv0, shift0, None, W, slopes)
    db, n_pages = page_table.shape
    NP, PS = cache_kv.shape[:2]
    pool5 = cache_kv.reshape(NP, PS // S_CMP, S_CMP, -1, HEAD_B)
    past = (pool5, page_table, cache_win_kv.reshape(db, cache_win_kv.shape[1], -1))
    y_s, wkv_s, shift_s, kv_s, win_s = _trunk(x_sample, p_sample, n_pages * PS, state_wkv, state_shift, past, W, slopes)
    return (y_p, y_s, wkv_p, shift_p, kv_p, win_p, wkv_s, shift_s, kv_s, win_s)
```

```python
import functools
import math

import jax
import jax.numpy as jnp
from jax import lax
from jax.experimental import pallas as pl
from jax.experimental.pallas import tpu as pltpu

F32 = jnp.float32
BF16 = jnp.bfloat16

HEAD_A = 64
GN_EPS = 64e-5
HEAD_B = 128
G_KV = 4
L_CMP = 32
S_CMP = 16
L_SEL = 64
TOPK_SEL = 16
WINDOW = 512
RMS_EPS = 1e-6
NEG = -1e30
FORCE_BONUS = 1e4

LANES = 128
SUBLANES = 8
VMEM_LIMIT = 56 * 1024 * 1024

SCAN_NH = 2
SCAN_C = 64
SCAN_GP = 8

NT_DIMS = (((1,), (1,)), ((), ()))
TN_DIMS = (((0,), (0,)), ((), ()))


def _cparams(sem):
    return pltpu.CompilerParams(dimension_semantics=sem, vmem_limit_bytes=VMEM_LIMIT)


def _bdot(a, b):
    return jnp.dot(a.astype(BF16), b.astype(BF16), preferred_element_type=F32)


def _bdot_nt(a, b):
    return lax.dot_general(a.astype(BF16), b.astype(BF16), NT_DIMS, preferred_element_type=F32)


def _bdot_tn(a, b):
    return lax.dot_general(a.astype(BF16), b.astype(BF16), TN_DIMS, preferred_element_type=F32)


def _rms_kernel(x_ref, g_ref, o_ref):
    x = x_ref[...]
    ms = jnp.mean(x * x, axis=-1, keepdims=True)
    o_ref[...] = x * lax.rsqrt(ms + RMS_EPS) * g_ref[...]


def rmsnorm(x, g):
    M, D = x.shape
    tm = min(M, 256)
    return pl.pallas_call(
        _rms_kernel,
        grid=(pl.cdiv(M, tm),),
        in_specs=[pl.BlockSpec((tm, D), lambda i: (i, 0)),
                  pl.BlockSpec((1, D), lambda i: (0, 0))],
        out_specs=pl.BlockSpec((tm, D), lambda i: (i, 0)),
        out_shape=jax.ShapeDtypeStruct((M, D), F32),
        compiler_params=_cparams(("parallel",)),
        name="rmsnorm",
    )(x, g.reshape(1, D))


def _mm_kernel(x_ref, w_ref, o_ref):
    o_ref[...] = jnp.dot(x_ref[...].astype(BF16), w_ref[...], preferred_element_type=F32)


def mm(x, w, widx=()):
    M, K = x.shape
    N = w.shape[-1]
    assert w.shape[-2] == K and len(widx) == w.ndim - 2
    tm = min(M, 1024 if K <= 2048 else 512)
    tn = N if N <= 512 else 512
    nlead = len(widx)
    w_spec = pl.BlockSpec((None,) * nlead + (K, tn), lambda i, j: tuple(widx) + (0, j))
    return pl.pallas_call(
        _mm_kernel,
        grid=(pl.cdiv(M, tm), pl.cdiv(N, tn)),
        in_specs=[pl.BlockSpec((tm, K), lambda i, j: (i, 0)), w_spec],
        out_specs=pl.BlockSpec((tm, tn), lambda i, j: (i, j)),
        out_shape=jax.ShapeDtypeStruct((M, N), F32),
        compiler_params=_cparams(("parallel", "parallel")),
        name="mm",
    )(x, w)


def _norm_rows(x, g):
    return x * lax.rsqrt(jnp.mean(x * x, axis=-1, keepdims=True) + RMS_EPS) * g


def _sigmoid(x):
    return 1.0 / (1.0 + jnp.exp(-x))


def _norm_and_shift(h_ref, hprev_ref, xprev_ref, g_ref, i, tm, T):
    g = g_ref[...]
    hn = _norm_rows(h_ref[...], g)
    prev_row = _norm_rows(hprev_ref[SUBLANES - 1:SUBLANES, :], g)
    row = lax.broadcasted_iota(jnp.int32, (tm, 1), 0)
    xs = jnp.where(row == 0, prev_row, pltpu.roll(hn, 1, axis=0))
    if T >= tm:
        assert T % tm == 0
        start = (i * tm) % T == 0
        xs = jnp.where((row == 0) & start, xprev_ref[pl.ds((i * tm) // T, 1), :], xs)
    else:
        assert tm % T == 0
        for bb in range(tm // T):
            xs = jnp.where(row == bb * T, xprev_ref[pl.ds(i * (tm // T) + bb, 1), :], xs)
    return hn, xs


def _rwkv_in_kernel(h_ref, hprev_ref, xprev_ref, g_ref, mu_ref, w_ref, o_ref, xs_scr, *, tm, T):
    i, j, n = pl.program_id(0), pl.program_id(1), pl.program_id(2)

    @pl.when(n == 0)
    def _():
        hn, xs = _norm_and_shift(h_ref, hprev_ref, xprev_ref, g_ref, i, tm, T)
        xs_scr[...] = (hn + (xs - hn) * mu_ref[pl.ds(j, 1), :]).astype(BF16)

    o_ref[...] = jnp.dot(xs_scr[...], w_ref[...], preferred_element_type=F32)


def _shift_specs(tm, D, nb, ngrid):
    z = (0,) * (ngrid - 1)
    wrap = lambda f: (lambda i, *_: f(i))
    return [pl.BlockSpec((tm, D), wrap(lambda i: (i, 0))),
            pl.BlockSpec((SUBLANES, D), wrap(lambda i: (jnp.maximum(i * (tm // SUBLANES) - 1, 0), 0))),
            pl.BlockSpec((nb, D), wrap(lambda i: (0, 0))),
            pl.BlockSpec((1, D), wrap(lambda i: (0, 0))),
            pl.BlockSpec((6, D), wrap(lambda i: (0, 0)))]


def rwkv_in(h, xprev, g, mu, w, layer, T):
    N, D = h.shape
    C = w.shape[-1]
    tm = min(N, 1024)
    tn = 512
    return pl.pallas_call(
        functools.partial(_rwkv_in_kernel, tm=tm, T=T),
        grid=(N // tm, 4, C // tn),
        in_specs=_shift_specs(tm, D, xprev.shape[0], 3) + [
            pl.BlockSpec((None, None, D, tn), lambda i, j, n: (layer, j, 0, n))],
        out_specs=pl.BlockSpec((None, tm, tn), lambda i, j, n: (j, i, n)),
        out_shape=jax.ShapeDtypeStruct((4, N, C), F32),
        scratch_shapes=[pltpu.VMEM((tm, D), BF16)],
        compiler_params=_cparams(("parallel", "arbitrary", "arbitrary")),
        name="rwkv_in",
    )(h, h, xprev, g.reshape(1, D), mu, w)


def _rwkv_lora_kernel(h_ref, hprev_ref, xprev_ref, g_ref, mu_ref, lw1_ref, lw2_ref, la1_ref, la2_ref,
                      w0_ref, a0_ref, lw_ref, a_ref, *, tm, T):
    hn, xs = _norm_and_shift(h_ref, hprev_ref, xprev_ref, g_ref, pl.program_id(0), tm, T)
    dx = xs - hn
    x4 = (hn + dx * mu_ref[4:5, :]).astype(BF16)
    x5 = (hn + dx * mu_ref[5:6, :]).astype(BF16)
    t4 = jnp.tanh(jnp.dot(x4, lw1_ref[...], preferred_element_type=F32)).astype(BF16)
    y = -(w0_ref[...] + jnp.dot(t4, lw2_ref[...], preferred_element_type=F32))
    softplus = jnp.maximum(y, 0.0) + jnp.log(1.0 + jnp.exp(-jnp.abs(y)))
    lw_ref[...] = -jnp.exp(-softplus - 0.5)
    t5 = jnp.dot(x5, la1_ref[...], preferred_element_type=F32).astype(BF16)
    a_ref[...] = _sigmoid(a0_ref[...] + jnp.dot(t5, la2_ref[...], preferred_element_type=F32))


def rwkv_lora(h, xprev, g, mu, lw1, lw2, la1, la2, w0, a0, T):
    N, D = h.shape
    R, C = lw2.shape
    tm = min(N, 256)
    full = lambda shape: pl.BlockSpec(shape, lambda i: (0,) * len(shape))
    o_spec = pl.BlockSpec((tm, C), lambda i: (i, 0))
    return pl.pallas_call(
        functools.partial(_rwkv_lora_kernel, tm=tm, T=T),
        grid=(N // tm,),
        in_specs=_shift_specs(tm, D, xprev.shape[0], 1) + [
            full((D, R)), full((R, C)), full((D, R)), full((R, C)), full((1, C)), full((1, C))],
        out_specs=[o_spec, o_spec],
        out_shape=[jax.ShapeDtypeStruct((N, C), F32)] * 2,
        compiler_params=_cparams(("parallel",)),
        name="rwkv_lora",
    )(h, h, xprev, g.reshape(1, D), mu, lw1, lw2, la1, la2, w0.reshape(1, C), a0.reshape(1, C))


def _norm_mm_kernel(h_ref, g_ref, w_ref, o_ref, xs_scr):
    @pl.when(pl.program_id(1) == 0)
    def _():
        xs_scr[...] = _norm_rows(h_ref[...], g_ref[...]).astype(BF16)

    o_ref[...] = jnp.dot(xs_scr[...], w_ref[...], preferred_element_type=F32)


def norm_mm(h, g, w, widx=()):
    N, D = h.shape
    NO = w.shape[-1]
    tm = min(N, 1024)
    tn = 512
    nlead = len(widx)
    return pl.pallas_call(
        _norm_mm_kernel,
        grid=(N // tm, pl.cdiv(NO, tn)),
        in_specs=[pl.BlockSpec((tm, D), lambda i, n: (i, 0)),
                  pl.BlockSpec((1, D), lambda i, n: (0, 0)),
                  pl.BlockSpec((None,) * nlead + (D, tn), lambda i, n: tuple(widx) + (0, n))],
        out_specs=pl.BlockSpec((tm, tn), lambda i, n: (i, n)),
        out_shape=jax.ShapeDtypeStruct((N, NO), F32),
        scratch_shapes=[pltpu.VMEM((tm, D), BF16)],
        compiler_params=_cparams(("parallel", "arbitrary")),
        name="norm_mm",
    )(h, g.reshape(1, D), w)


def _out_ple_kernel(x_ref, h_ref, p_ref, wo_ref, wp_ref, wg_ref, o_ref, h1_scr, h1b_scr, *, NT):
    n = pl.program_id(1)

    @pl.when(n < NT)
    def _():
        h1 = h_ref[...] + jnp.dot(x_ref[...].astype(BF16), wo_ref[...], preferred_element_type=F32)
        h1_scr[n] = h1
        h1b_scr[n] = h1.astype(BF16)

    @pl.when(n >= NT)
    def _():
        h1b = jnp.concatenate([h1b_scr[t] for t in range(NT)], axis=1)
        gate = jnp.dot(h1b, wg_ref[...], preferred_element_type=F32)
        ple = jnp.dot(p_ref[...].astype(BF16), wp_ref[...], preferred_element_type=F32)
        o_ref[...] = h1_scr[n - NT] + ple * _sigmoid(gate)


def out_ple(x, h, p, w_out, oidx, w_ple, w_gate, layer):
    N, C = x.shape
    D = h.shape[1]
    DP = p.shape[1]
    tm = min(N, 512)
    tn = 512
    NT = D // tn
    lo = lambda n: jnp.minimum(n, NT - 1)
    hi = lambda n: jnp.maximum(n - NT, 0)
    return pl.pallas_call(
        functools.partial(_out_ple_kernel, NT=NT),
        grid=(N // tm, 2 * NT),
        in_specs=[pl.BlockSpec((tm, C), lambda i, n: (i, 0)),
                  pl.BlockSpec((tm, tn), lambda i, n: (i, lo(n))),
                  pl.BlockSpec((tm, DP), lambda i, n: (i, 0)),
                  pl.BlockSpec((None, C, tn), lambda i, n: (oidx, 0, lo(n))),
                  pl.BlockSpec((None, DP, tn), lambda i, n: (layer, 0, hi(n))),
                  pl.BlockSpec((None, D, tn), lambda i, n: (layer, 0, hi(n)))],
        out_specs=pl.BlockSpec((tm, tn), lambda i, n: (i, hi(n))),
        out_shape=jax.ShapeDtypeStruct((N, D), F32),
        scratch_shapes=[pltpu.VMEM((NT, tm, tn), F32), pltpu.VMEM((NT, tm, tn), BF16)],
        compiler_params=_cparams(("parallel", "arbitrary")),
        name="out_ple",
    )(x, h, p, w_out, w_ple, w_gate)


def _scan_kernel(r_ref, k_ref, v_ref, zg_ref, lw_ref, a_ref, pv_ref, s0_ref, o_ref, sfin_ref, s_scr, *, C, NH, GP):
    L = NH * HEAD_A
    NC = NH * C
    ci = pl.program_id(2)

    @pl.when(ci == 0)
    def _():
        s_scr[...] = s0_ref[0]

    row_c = lax.broadcasted_iota(jnp.int32, (C, NC), 0)
    col_s = lax.broadcasted_iota(jnp.int32, (C, NC), 1) % C
    tri_strict = col_s < row_c
    tri_incl = (lax.broadcasted_iota(jnp.int32, (C, 2 * NC), 1) % C
                <= lax.broadcasted_iota(jnp.int32, (C, 2 * NC), 0))
    st_mask = (lax.broadcasted_iota(jnp.int32, (NC, L), 0) // C
               == lax.broadcasted_iota(jnp.int32, (NC, L), 1) // HEAD_A)
    bd_mask = (lax.broadcasted_iota(jnp.int32, (NC, NC), 0) // C
               == lax.broadcasted_iota(jnp.int32, (NC, NC), 1) // C)
    head_mask = (lax.broadcasted_iota(jnp.int32, (L, L), 0) // HEAD_A
                 == lax.broadcasted_iota(jnp.int32, (L, L), 1) // HEAD_A)
    cum_mat = (lax.broadcasted_iota(jnp.int32, (C, C), 1)
               <= lax.broadcasted_iota(jnp.int32, (C, C), 0)).astype(F32)

    def st(x):
        return jnp.where(st_mask, jnp.concatenate([x] * NH, axis=0), 0.0)

    def bd(w):
        return jnp.where(bd_mask, jnp.concatenate([w] * NH, axis=0), 0.0)

    n_double = int(math.log2(C))
    each = lambda f, *cols: [f(*xs) for xs in zip(*cols)]
    sls = [slice(gp * L, (gp + 1) * L) for gp in range(GP)]
    head_of_lane = lax.broadcasted_iota(jnp.int32, (1, L), 1) // HEAD_A

    def hsum(x):
        out = None
        for hh in range(NH):
            sh = jnp.sum(jnp.where(head_of_lane == hh, x, 0.0), axis=-1, keepdims=True)
            out = sh if out is None else jnp.where(head_of_lane == hh, sh, out)
        return out

    k_k, k_a, r_k, ln_w, ln_b = ([pv_ref[n:n + 1, sl] for sl in sls] for n in range(5))
    lw = [lw_ref[0, :, sl] for sl in sls]
    a_sig = [a_ref[0, :, sl] for sl in sls]
    r = [r_ref[0, :, sl] for sl in sls]
    k_raw = [k_ref[0, :, sl] for sl in sls]
    v = [v_ref[0, :, sl] for sl in sls]
    kk = each(lambda x, w: x * w, k_raw, k_k)
    kk = each(lambda x: x / jnp.maximum(jnp.sqrt(hsum(x * x)), 1e-12), kk)
    k = each(lambda x, a, w: x * (1.0 + (a - 1.0) * w), k_raw, a_sig, k_a)
    cum = each(lambda z: jnp.dot(cum_mat, z, precision=lax.Precision.HIGHEST, preferred_element_type=F32), lw)
    p_incl = each(jnp.exp, cum)
    p_inv = each(lambda z: jnp.exp(-z), cum)
    at = each(lambda x, c, w: -x * jnp.exp(c - w), kk, cum, lw)
    rt = each(lambda x, p: x * p, r, p_incl)
    bt = each(lambda x, a, p: x * a * p, kk, a_sig, p_inv)
    kt = each(lambda x, p: x * p, k, p_inv)
    S = [s_scr[gp] for gp in range(GP)]
    ar = each(lambda x, y: jnp.concatenate([x, y], axis=0), at, rt)
    bk_st = each(lambda x, y: jnp.concatenate([st(x), st(y)], axis=0), bt, kt)
    Gm = each(_bdot_nt, ar, bk_st)
    w_ab = [jnp.where(tri_strict, g[:C, :NC], 0.0) for g in Gm]
    tm = w_ab
    pw = each(lambda w: _bdot(w, bd(w)), w_ab)
    LH = each(_bdot_nt, ar, S)
    v_st = each(st, v)
    x = [lh[:C] + _bdot(jnp.where(tri_strict, g[:C, NC:], 0.0), vs) for lh, g, vs in zip(LH, Gm, v_st)]
    for it in range(1, n_double):
        tm_next = each(lambda t, p: t + p + _bdot(p, bd(t)), tm, pw)
        if it < n_double - 1:
            pw = each(lambda p: _bdot(p, bd(p)), pw)
        tm = tm_next
    u = each(lambda xx, t: xx + _bdot(t, st(xx)), x, tm)
    o = [lh[C:] + _bdot(jnp.where(tri_incl, g[C:], 0.0), jnp.concatenate([st(uu), vs], axis=0))
         for lh, g, uu, vs in zip(LH, Gm, u, v_st)]
    inv_n = 1.0 / HEAD_A
    dev = each(lambda x: x - hsum(x) * inv_n, o)
    gn = each(lambda d, w, b_: d * lax.rsqrt(hsum(d * d) * inv_n + GN_EPS) * w + b_, dev, ln_w, ln_b)
    bonus = each(lambda rr, kx, w, vv: hsum(rr * kx * w) * vv, r, k, r_k, v)
    for sl, y, bo in zip(sls, gn, bonus):
        zg = zg_ref[0, :, sl]
        o_ref[0, :, sl] = ((y + bo) * (zg * _sigmoid(zg))).astype(o_ref.dtype)
    ds = [_bdot_tn(jnp.concatenate([uu, vv], axis=0), jnp.concatenate([b_, k_], axis=0))
          for uu, vv, b_, k_ in zip(u, v, bt, kt)]
    for gp in range(GP):
        s_scr[gp] = (S[gp] + jnp.where(head_mask, ds[gp], 0.0)) * p_incl[gp][C - 1:C, :]

    @pl.when(ci == pl.num_programs(2) - 1)
    def _():
        sfin_ref[0] = s_scr[...]


def rwkv_scan(rkvg, lw, a, pvec, s0):
    _, B, T, CA = rkvg.shape
    H = CA // HEAD_A
    NH, C, GP = SCAN_NH, SCAN_C, SCAN_GP
    assert NH * C == LANES and H % (NH * GP) == 0
    L = NH * HEAD_A
    NG = H // NH
    Tp = -(-T // C) * C
    if Tp != T:
        rkvg = jnp.pad(rkvg, ((0, 0), (0, 0), (0, Tp - T), (0, 0)))
        lw, a = (jnp.pad(z, ((0, 0), (0, Tp - T), (0, 0))) for z in (lw, a))
    eye = jnp.eye(NH, dtype=F32)
    s0_bd = (s0.reshape(B, NG, NH, HEAD_A, 1, HEAD_A) * eye[None, None, :, None, :, None]).reshape(B, NG, L, L)
    seq_spec = pl.BlockSpec((1, C, GP * L), lambda bi, gi, ci: (bi, ci, gi))
    proj_spec = lambda j: pl.BlockSpec((None, 1, C, GP * L), lambda bi, gi, ci: (j, bi, ci, gi))
    st_spec = pl.BlockSpec((1, GP, L, L), lambda bi, gi, ci: (bi, gi, 0, 0))
    o, sfin = pl.pallas_call(
        functools.partial(_scan_kernel, C=C, NH=NH, GP=GP),
        grid=(B, NG // GP, Tp // C),
        in_specs=[proj_spec(j) for j in range(4)] + [seq_spec, seq_spec,
                  pl.BlockSpec((5, GP * L), lambda bi, gi, ci: (0, gi)), st_spec],
        out_specs=[seq_spec, st_spec],
        out_shape=[jax.ShapeDtypeStruct((B, Tp, CA), BF16), jax.ShapeDtypeStruct((B, NG, L, L), F32)],
        scratch_shapes=[pltpu.VMEM((GP, L, L), F32)],
        compiler_params=_cparams(("parallel", "parallel", "arbitrary")),
        name="rwkv_scan",
    )(rkvg, rkvg, rkvg, rkvg, lw, a, pvec, s0_bd)
    sf = sfin.reshape(B, NG, NH, HEAD_A, NH, HEAD_A)
    s_fin = jnp.stack([sf[:, :, h, :, h, :] for h in range(NH)], axis=2).reshape(B, H, HEAD_A, HEAD_A)
    return o[:, :T], s_fin


def _gelu_tanh(x):
    c = math.sqrt(2.0 / math.pi)
    return 0.5 * x * (1.0 + jnp.tanh(c * (x + 0.044715 * (x * x * x))))


def _compress_kernel(pt_ref, *refs, PGS):
    del pt_ref
    page_refs = refs[:PGS]
    next_ref, pe_ref, w1_ref, w2_ref, out_ref = refs[PGS:]
    CPP = page_refs[0].shape[1]
    NCH = PGS * CPP
    CG = 2 * G_KV
    M = (NCH + 1) * CG

    def rows_of(l, hf):
        pe = pe_ref[hf, l]
        parts = [(page_refs[i][0, :, l] + pe[None]).reshape(CPP * CG, HEAD_B) for i in range(PGS)]
        parts.append(next_ref[0, 0, l] + pe)
        return jnp.concatenate(parts, axis=0)

    top = jnp.zeros((M, 2 * HEAD_B), F32)
    bot = jnp.zeros((M, 2 * HEAD_B), F32)
    for l in range(0, S_CMP, 2):
        wrows = pl.ds(l * HEAD_B, 2 * HEAD_B)
        xt = jnp.concatenate([rows_of(l, 0), rows_of(l + 1, 0)], axis=1).astype(BF16)
        top = top + jnp.dot(xt, w1_ref[0, wrows, :], preferred_element_type=F32)
        xb = jnp.concatenate([rows_of(l, 1), rows_of(l + 1, 1)], axis=1).astype(BF16)
        bot = bot + jnp.dot(xb, w1_ref[1, wrows, :], preferred_element_type=F32)
    is_k = (lax.broadcasted_iota(jnp.int32, (M, 1), 0) % CG) < G_KV
    pick = lambda z, n: jnp.where(is_k[:n], z[:n, :HEAD_B], z[:n, HEAD_B:])
    hcur = pick(top, NCH * CG) + pick(bot, M)[CG:]
    o2 = jnp.dot(_gelu_tanh(hcur).astype(BF16), w2_ref[...], preferred_element_type=F32)
    out_ref[0] = pick(o2, NCH * CG).reshape(NCH, CG, HEAD_B)


def compress_kv(pool5, table, pe_cmp, w1, w2):
    NP, CPP = pool5.shape[:2]
    B, n_pages = table.shape
    PGS = max(d for d in (8, 4, 2, 1) if n_pages % d == 0)
    NCH = PGS * CPP
    CG = 2 * G_KV
    half = S_CMP * HEAD_B
    pe_r = jnp.repeat(pe_cmp.reshape(2, 2, S_CMP, HEAD_B).transpose(1, 2, 0, 3), G_KV, axis=2)
    w1_r = w1.reshape(2, 2, half, HEAD_B).transpose(1, 2, 0, 3).reshape(2, half, 2 * HEAD_B)
    w2_r = jnp.concatenate([w2[0], w2[1]], axis=1)

    def page_map(i):
        return lambda b, s, pt: (pt[b, s * PGS + i], 0, 0, 0, 0)

    def next_map(b, s, pt):
        return (pt[b, jnp.minimum((s + 1) * PGS, n_pages - 1)], 0, 0, 0, 0)

    const = lambda n: (lambda b, s, pt: (0,) * n)
    grid_spec = pltpu.PrefetchScalarGridSpec(
        num_scalar_prefetch=1,
        grid=(B, n_pages // PGS),
        in_specs=[pl.BlockSpec((1, CPP, S_CMP, CG, HEAD_B), page_map(i)) for i in range(PGS)] + [
            pl.BlockSpec((1, 1, S_CMP, CG, HEAD_B), next_map),
            pl.BlockSpec((2, S_CMP, CG, HEAD_B), const(4)),
            pl.BlockSpec((2, half, 2 * HEAD_B), const(3)),
            pl.BlockSpec((HEAD_B, 2 * HEAD_B), const(2)),
        ],
        out_specs=pl.BlockSpec((1, NCH, CG, HEAD_B), lambda b, s, pt: (b, s, 0, 0)),
    )
    return pl.pallas_call(
        functools.partial(_compress_kernel, PGS=PGS),
        grid_spec=grid_spec,
        out_shape=jax.ShapeDtypeStruct((B, n_pages * CPP, CG, HEAD_B), F32),
        compiler_params=_cparams(("parallel", "arbitrary")),
        name="compress_kv",
    )(table, *([pool5] * PGS), pool5, pe_r, w1_r, w2_r)


def _stack_heads(q, HG):
    return jnp.concatenate([q[:, h * HEAD_B:(h + 1) * HEAD_B] for h in range(HG)], axis=0)


def _masked_softmax_rows(s, mask):
    s = jnp.where(mask, s, NEG)
    m = jnp.max(s, axis=-1, keepdims=True)
    e = jnp.where(mask, jnp.exp(s - m), 0.0)
    l = jnp.sum(e, axis=-1, keepdims=True)
    return e / jnp.where(l > 0.0, l, 1.0)


def _attend_stacked(s, dist, mask, v, slopes_ref, g, HG, tq):
    ps = []
    psum = jnp.zeros(dist.shape, F32)
    for h in range(HG):
        p = _masked_softmax_rows(s[h * tq:(h + 1) * tq] - slopes_ref[g * HG + h] * dist, mask)
        psum = psum + p
        ps.append(p.astype(BF16))
    o = jnp.dot(jnp.concatenate(ps, axis=0), v.astype(BF16), preferred_element_type=F32)
    return o, psum


def _unstack_store(o_ref, o, HG, tq):
    for h in range(HG):
        o_ref[0, :, h * HEAD_B:(h + 1) * HEAD_B] = o[h * tq:(h + 1) * tq]


def _nsa_cmp_kernel(slopes_ref, q_ref, kc_ref, vc_ref, o_ref, selm_ref, *, tq, HG, nc, nsb, pos0):
    g = pl.program_id(1)
    qt = pl.program_id(2)
    NCp = kc_ref.shape[1]
    NSBp = selm_ref.shape[3]
    q_st = _stack_heads(q_ref[0] * (HEAD_B ** -0.5), HG)
    s = _bdot_nt(q_st, kc_ref[0])
    qpos = pos0 + qt * tq + lax.broadcasted_iota(jnp.int32, (tq, 1), 0)
    cidx = lax.broadcasted_iota(jnp.int32, (1, NCp), 1)
    cend = S_CMP * cidx + (L_CMP - 1)
    mask = (cend <= qpos) & (cidx < nc)
    dist = (qpos - cend).astype(F32)
    o, imp_c = _attend_stacked(s, dist, mask, vc_ref[0], slopes_ref, g, HG, tq)
    _unstack_store(o_ref, o, HG, tq)

    crow = lax.broadcasted_iota(jnp.int32, (NCp, NSBp), 0)
    jcol = lax.broadcasted_iota(jnp.int32, (NCp, NSBp), 1)
    overlap = ((S_CMP * crow < L_SEL * (jcol + 1)) & (S_CMP * crow + L_CMP > L_SEL * jcol)
               & (crow < nc)).astype(F32)
    imp = jnp.dot(imp_c, overlap, precision=lax.Precision.HIGHEST, preferred_element_type=F32)
    lane = lax.broadcasted_iota(jnp.int32, (tq, NSBp), 1)
    cur = jnp.right_shift(qpos, int(math.log2(L_SEL)))
    forced = ((lane == 0) | (lane == cur) | (lane == cur - 1)).astype(F32)
    score = jnp.where(lane <= cur, imp + FORCE_BONUS * forced, NEG)
    score = jnp.where(lane < nsb, score, -3e38)

    cnt = jnp.zeros((tq, NSBp), F32)
    for i in range(nsb):
        col = score[:, i:i + 1]
        beats = (col > score) | ((col == score) & (lane > i))
        cnt = cnt + jnp.where(beats, 1.0, 0.0)
    sel = (cnt < TOPK_SEL) & (score > 0.5 * NEG)
    selm_ref[0, 0] = sel.astype(F32)


def nsa_cmp(proj, kvc, slopes, *, tq, nc, nsb, pos0):
    B, T, _ = proj.shape
    HG = slopes.shape[0] // G_KV
    NCp = kvc.shape[1]
    NSBp = -(-nsb // LANES) * LANES
    gw = HG * HEAD_B
    return pl.pallas_call(
        functools.partial(_nsa_cmp_kernel, tq=tq, HG=HG, nc=nc, nsb=nsb, pos0=pos0),
        grid=(B, G_KV, T // tq),
        in_specs=[pl.BlockSpec(memory_space=pltpu.SMEM),
                  pl.BlockSpec((1, tq, gw), lambda b, g, t: (b, t, g)),
                  pl.BlockSpec((1, NCp, HEAD_B), lambda b, g, t: (b, 0, g)),
                  pl.BlockSpec((1, NCp, HEAD_B), lambda b, g, t: (b, 0, G_KV + g))],
        out_specs=[pl.BlockSpec((1, tq, gw), lambda b, g, t: (b, t, g)),
                   pl.BlockSpec((1, 1, tq, NSBp), lambda b, g, t: (b, g, t, 0))],
        out_shape=[jax.ShapeDtypeStruct((B, T, G_KV * gw), F32),
                   jax.ShapeDtypeStruct((B, G_KV, T, NSBp), F32)],
        compiler_params=_cparams(("parallel", "parallel", "parallel")),
        name="nsa_cmp",
    )(slopes, proj, kvc.reshape(B, NCp, -1), kvc.reshape(B, NCp, -1))


def _silu(x):
    return x * _sigmoid(x)


def _nsa_selwin_prompt_kernel(slopes_ref, q_ref, selm_ref, ks_ref, vs_ref, kw_ref, vw_ref,
                              ocmp_ref, zc_ref, zs_ref, zw_ref, gate_ref, o_ref, s_scr, *, tq, HG, T, WS, SEG):
    g = pl.program_id(1)
    qt = pl.program_id(2)
    NSBp = selm_ref.shape[3]
    R = HG * tq
    q_st = _stack_heads(q_ref[0] * (HEAD_B ** -0.5), HG).astype(BF16)
    qpos = qt * tq + lax.broadcasted_iota(jnp.int32, (tq, 1), 0)
    slope_col = jnp.concatenate([jnp.full((tq, 1), slopes_ref[g * HG + h], F32) for h in range(HG)], axis=0)
    tile_heads = lambda x: jnp.concatenate([x] * HG, axis=0)

    selm_b = selm_ref[0, 0].astype(BF16)
    nseg = (qt * tq + tq + SEG - 1) // SEG

    def seg_scores(si, m):
        k0 = pl.multiple_of(si * SEG, SEG)
        kcol = k0 + lax.broadcasted_iota(jnp.int32, (NSBp, SEG), 1)
        expand = (jnp.right_shift(kcol, int(math.log2(L_SEL)))
                  == lax.broadcasted_iota(jnp.int32, (NSBp, SEG), 0)).astype(BF16)
        in_blk = jnp.dot(selm_b, expand, preferred_element_type=F32) > 0.5
        dist = qpos - (k0 + lax.broadcasted_iota(jnp.int32, (1, SEG), 1))
        mask = tile_heads(in_blk & (dist >= 0))
        s = _bdot_nt(q_st, ks_ref[0, pl.ds(k0, SEG), :]) - slope_col * tile_heads(dist.astype(F32))
        s = jnp.where(mask, s, NEG)
        s_scr[si] = s
        return jnp.maximum(m, jnp.max(s, axis=-1, keepdims=True))

    m = lax.fori_loop(0, nseg, seg_scores, jnp.full((R, 1), NEG, F32))

    def seg_pv(si, carry):
        l, acc = carry
        k0 = pl.multiple_of(si * SEG, SEG)
        e = jnp.exp(s_scr[si] - m)
        return (l + jnp.sum(e, axis=-1, keepdims=True),
                acc + _bdot(e, vs_ref[0, pl.ds(k0, SEG), :]))

    l, acc = lax.fori_loop(0, nseg, seg_pv, (jnp.zeros((R, 1), F32), jnp.zeros((R, HEAD_B), F32)))
    o = acc / l

    start = pl.multiple_of(jnp.clip(qt * tq - WINDOW, 0, T - WS), SUBLANES)
    distw = qpos - (start + lax.broadcasted_iota(jnp.int32, (1, WS), 1))
    maskw = tile_heads((distw >= 0) & (distw < WINDOW))
    sw = _bdot_nt(q_st, kw_ref[0, pl.ds(start, WS), :]) - slope_col * tile_heads(distw.astype(F32))
    sw = jnp.where(maskw, sw, NEG)
    ew = jnp.exp(sw - jnp.max(sw, axis=-1, keepdims=True))
    ow = _bdot(ew, vw_ref[0, pl.ds(start, WS), :]) / jnp.sum(ew, axis=-1, keepdims=True)

    HB = G_KV * HG
    gates = _sigmoid(gate_ref[0])
    lane = lax.broadcasted_iota(jnp.int32, gates.shape, 1)
    gate_col = lambda idx: jnp.sum(jnp.where(lane == idx, gates, 0.0), axis=1, keepdims=True)
    for h in range(HG):
        hs = slice(h * HEAD_B, (h + 1) * HEAD_B)
        rs = slice(h * tq, (h + 1) * tq)
        hd = g * HG + h
        y = (gate_col(hd) * ocmp_ref[0, :, hs] * _silu(zc_ref[0, :, hs])
             + gate_col(HB + hd) * o[rs] * _silu(zs_ref[0, :, hs])
             + gate_col(2 * HB + hd) * ow[rs] * _silu(zw_ref[0, :, hs]))
        o_ref[0, :, hs] = y.astype(o_ref.dtype)


def nsa_selwin_prompt(proj, rows, selm, o_cmp, slopes, *, tq):
    B, T, _ = proj.shape
    HG = slopes.shape[0] // G_KV
    NSBp = selm.shape[3]
    gw = HG * HEAD_B
    CB = G_KV * gw
    WS = min(T, WINDOW + tq)
    SEG = min(T, 512)
    assert T % SEG == 0
    kv_spec = lambda c: pl.BlockSpec((1, T, HEAD_B), lambda b, g, t: (b, 0, c * G_KV + g))
    head_spec = lambda blk: pl.BlockSpec((1, tq, gw), lambda b, g, t: (b, t, blk * G_KV + g))
    return pl.pallas_call(
        functools.partial(_nsa_selwin_prompt_kernel, tq=tq, HG=HG, T=T, WS=WS, SEG=SEG),
        grid=(B, G_KV, T // tq),
        in_specs=[pl.BlockSpec(memory_space=pltpu.SMEM),
                  head_spec(0),
                  pl.BlockSpec((1, 1, tq, NSBp), lambda b, g, t: (b, g, t, 0)),
                  kv_spec(2), kv_spec(3), kv_spec(4), kv_spec(5),
                  head_spec(0), head_spec(1), head_spec(2), head_spec(3),
                  pl.BlockSpec((1, tq, LANES), lambda b, g, t: (b, t, 4 * CB // LANES))],
        out_specs=head_spec(0),
        out_shape=jax.ShapeDtypeStruct((B, T, CB), BF16),
        scratch_shapes=[pltpu.VMEM((T // SEG, HG * tq, SEG), F32)],
        compiler_params=_cparams(("parallel", "parallel", "parallel")),
        name="nsa_selwin_prompt",
    )(slopes, proj, selm, rows, rows, rows, rows, o_cmp, proj, proj, proj, proj)


def _nsa_selwin_sample_kernel(pt_ref, slopes_ref, q_ref, selm_ref, selst_ref, *refs, PGS, PS, HG, TQ, pos0, n_new, n_win):
    del pt_ref
    page_refs = refs[:PGS]
    (new_ref, cwin_ref, ocmp_ref, zc_ref, zs_ref, zw_ref, gate_ref, o_ref,
     m_scr, l_scr, acc_scr) = refs[PGS:]
    st = pl.program_id(1)
    NSBp = selm_ref.shape[3]
    GW = G_KV * HEAD_B
    sel_shift = int(math.log2(L_SEL))

    @pl.when(st == 0)
    def _():
        m_scr[...] = jnp.full(m_scr.shape, NEG, F32)
        l_scr[...] = jnp.zeros(l_scr.shape, F32)
        acc_scr[...] = jnp.zeros(acc_scr.shape, F32)

    qpos = pos0 + lax.broadcasted_iota(jnp.int32, (TQ, 1), 0)
    lane_j = lax.broadcasted_iota(jnp.int32, (TQ, NSBp), 1)
    tile_heads = lambda x: jnp.concatenate([x] * HG, axis=0)

    def sel_col(selm_g, j):
        return jnp.sum(jnp.where(lane_j == j, selm_g, 0.0), axis=1, keepdims=True)

    def online_update(s, mask, v):
        gs = range(G_KV)
        m_old = [m_scr[g] for g in gs]
        m_new = [jnp.maximum(m_old[g], jnp.max(jnp.where(mask[g], s[g], NEG), axis=-1, keepdims=True)) for g in gs]
        e = [jnp.where(mask[g], jnp.exp(s[g] - m_new[g]), 0.0) for g in gs]
        alpha = [jnp.exp(m_old[g] - m_new[g]) for g in gs]
        pv = [_bdot(e[g], v[g]) for g in gs]
        for g in gs:
            l_scr[g] = alpha[g] * l_scr[g] + jnp.sum(e[g], axis=-1, keepdims=True)
            acc_scr[g] = alpha[g] * acc_scr[g] + pv[g]
            m_scr[g] = m_new[g]

    NK = PGS * PS
    kpos = st * NK + lax.broadcasted_iota(jnp.int32, (1, NK), 1)
    expand = (jnp.right_shift(lax.broadcasted_iota(jnp.int32, (LANES, NK), 1), sel_shift)
              == lax.broadcasted_iota(jnp.int32, (LANES, NK), 0)).astype(BF16)
    in_blk_all = jnp.dot(selst_ref[0, 0].astype(BF16), expand, preferred_element_type=F32)
    dist = qpos - kpos
    distf = tile_heads(dist.astype(F32))
    gs = range(G_KV)
    q_st = [_stack_heads(q_ref[0, :, g * HG * HEAD_B:(g + 1) * HG * HEAD_B] * (HEAD_B ** -0.5), HG).astype(BF16)
            for g in gs]
    slope_col = [jnp.concatenate([jnp.full((TQ, 1), slopes_ref[g * HG + h], F32) for h in range(HG)], axis=0)
                 for g in gs]
    page_rows = lambda i, cg: page_refs[i][0, :, :, cg, :].reshape(PS, HEAD_B)
    k = [jnp.concatenate([page_rows(i, g) for i in range(PGS)], axis=0) for g in gs]
    v = [jnp.concatenate([page_rows(i, G_KV + g) for i in range(PGS)], axis=0) for g in gs]
    mask = [tile_heads((in_blk_all[g * TQ:(g + 1) * TQ] > 0.5) & (dist >= 0)) for g in gs]
    s = [_bdot_nt(q_st[g], k[g]) - slope_col[g] * distf for g in gs]
    online_update(s, mask, v)

    @pl.when(st == pl.num_programs(1) - 1)
    def _():
        NN = new_ref.shape[1]
        rnew = lax.broadcasted_iota(jnp.int32, (1, NN), 1)
        kpos_n = pos0 + rnew
        dist_n = qpos - kpos_n
        ok_n = (rnew < n_new) & (dist_n >= 0)
        jn = pos0 >> sel_shift
        kpos_w = pos0 - n_win + lax.broadcasted_iota(jnp.int32, (1, n_win), 1)
        dist_w = qpos - kpos_w
        dist_wall = jnp.concatenate([dist_w, dist_n], axis=1)
        mask_wall = jnp.concatenate([(dist_w >= 0) & (dist_w < WINDOW), ok_n & (dist_n < WINDOW)], axis=1)
        kn = [new_ref[0, :, 2 * GW + g * HEAD_B:2 * GW + (g + 1) * HEAD_B] for g in gs]
        vn = [new_ref[0, :, 3 * GW + g * HEAD_B:3 * GW + (g + 1) * HEAD_B] for g in gs]
        mask_n = [tile_heads((sel_col(selm_ref[0, g], jn) > 0.5) & ok_n) for g in gs]
        dist_nf = tile_heads(dist_n.astype(F32))
        s_n = [_bdot_nt(q_st[g], kn[g]) - slope_col[g] * dist_nf for g in gs]
        online_update(s_n, mask_n, vn)
        kw = [jnp.concatenate([cwin_ref[0, :, g * HEAD_B:(g + 1) * HEAD_B],
                               new_ref[0, :, 4 * GW + g * HEAD_B:4 * GW + (g + 1) * HEAD_B]], axis=0) for g in gs]
        vw = [jnp.concatenate([cwin_ref[0, :, GW + g * HEAD_B:GW + (g + 1) * HEAD_B],
                               new_ref[0, :, 5 * GW + g * HEAD_B:5 * GW + (g + 1) * HEAD_B]], axis=0) for g in gs]
        dist_wf = tile_heads(dist_wall.astype(F32))
        mask_w = tile_heads(mask_wall)
        sw = [_bdot_nt(q_st[g], kw[g]) - slope_col[g] * dist_wf for g in gs]
        pw = [_masked_softmax_rows(sw[g], mask_w) for g in gs]
        ow = [_bdot(pw[g], vw[g]) for g in gs]
        HB = G_KV * HG
        gates = _sigmoid(gate_ref[0])
        for g in gs:
            l = l_scr[g]
            o = acc_scr[g] / jnp.where(l > 0.0, l, 1.0)
            for h in range(HG):
                hd = g * HG + h
                hs = slice(hd * HEAD_B, (hd + 1) * HEAD_B)
                rs = slice(h * TQ, (h + 1) * TQ)
                y = (gates[:, hd:hd + 1] * ocmp_ref[0, :, hs] * _silu(zc_ref[0, :, hs])
                     + gates[:, HB + hd:HB + hd + 1] * o[rs] * _silu(zs_ref[0, :, hs])
                     + gates[:, 2 * HB + hd:2 * HB + hd + 1] * ow[g][rs] * _silu(zw_ref[0, :, hs]))
                o_ref[0, :, hs] = y.astype(o_ref.dtype)


def nsa_selwin_sample(proj, selm, o_cmp, pool5, table, new_rows, cwin, slopes, *, pos0, n_new):
    B, TQ, _ = proj.shape
    HG = slopes.shape[0] // G_KV
    NP, CPP = pool5.shape[:2]
    PS = CPP * S_CMP
    n_pages = table.shape[1]
    NSBp = selm.shape[3]
    PGS = max(d for d in (8, 4, 2, 1) if n_pages % d == 0)
    CB = G_KV * HG * HEAD_B
    GW = G_KV * HEAD_B
    NN = new_rows.shape[1]
    n_win = cwin.shape[1]
    assert pos0 % L_SEL == 0 and n_new <= L_SEL and pos0 == n_pages * PS

    def page_map(i):
        return lambda b, s, pt: (pt[b, s * PGS + i], 0, 0, 1, 0)

    n_steps = n_pages // PGS
    bps = PGS * PS // L_SEL
    assert bps <= LANES
    selst = selm[:, :, :, :n_steps * bps].reshape(B, G_KV, TQ, n_steps, bps).transpose(0, 3, 1, 2, 4)
    selst = jnp.pad(selst.reshape(B, n_steps, G_KV * TQ, bps), ((0, 0), (0, 0), (0, 0), (0, LANES - bps)))

    const = lambda b, s, pt: (b, 0, 0)
    wide = lambda blk: pl.BlockSpec((1, TQ, CB), lambda b, s, pt: (b, 0, blk))
    grid_spec = pltpu.PrefetchScalarGridSpec(
        num_scalar_prefetch=1,
        grid=(B, n_steps),
        in_specs=[pl.BlockSpec(memory_space=pltpu.SMEM),
                  pl.BlockSpec((1, TQ, CB), const),
                  pl.BlockSpec((1, G_KV, TQ, NSBp), lambda b, s, pt: (b, 0, 0, 0)),
                  pl.BlockSpec((1, 1, G_KV * TQ, LANES), lambda b, s, pt: (b, s, 0, 0))]
                 + [pl.BlockSpec((1, CPP, S_CMP, 2 * G_KV, HEAD_B), page_map(i)) for i in range(PGS)]
                 + [pl.BlockSpec((1, NN, 6 * GW), const),
                    pl.BlockSpec((1, n_win, 2 * GW), const),
                    wide(0), wide(1), wide(2), wide(3),
                    pl.BlockSpec((1, TQ, LANES), lambda b, s, pt: (b, 0, 4 * CB // LANES))],
        out_specs=wide(0),
        scratch_shapes=[pltpu.VMEM((G_KV, HG * TQ, 1), F32),
                        pltpu.VMEM((G_KV, HG * TQ, 1), F32),
                        pltpu.VMEM((G_KV, HG * TQ, HEAD_B), F32)],
    )
    return pl.pallas_call(
        functools.partial(_nsa_selwin_sample_kernel, PGS=PGS, PS=PS, HG=HG, TQ=TQ, pos0=pos0,
                          n_new=n_new, n_win=n_win),
        grid_spec=grid_spec,
        out_shape=jax.ShapeDtypeStruct((B, TQ, CB), F32),
        compiler_params=_cparams(("parallel", "arbitrary")),
        name="nsa_selwin_sample",
    )(table, slopes, proj, selm, selst, *([pool5] * PGS), new_rows, cwin, o_cmp, proj, proj, proj, proj)


def _rwkv_layer(h, x_prev, s0, i, W, B, T):
    N, D = h.shape
    g = W["norm_g"][i]
    rkvg = rwkv_in(h, x_prev, g, W["mu_a"][i], W["w_in_a"], i, T)
    CA = rkvg.shape[-1]
    lw, a = rwkv_lora(h, x_prev, g, W["mu_a"][i], W["w_lora_w1"][i], W["w_lora_w2"][i], W["a_lora1"][i],
                      W["a_lora2"][i], W["w0_a"][i], W["a0_a"][i], T)
    pvec = jnp.stack([W["k_k"][i], W["k_a"][i], W["r_k"][i].reshape(CA), W["ln_x_w"][i], W["ln_x_b"][i]])
    o, s_fin = rwkv_scan(rkvg.reshape(4, B, T, CA), lw.reshape(B, T, CA), a.reshape(B, T, CA), pvec, s0)
    last = rmsnorm(h.reshape(B, T, D)[:, -1], g)
    return o.reshape(N, CA), s_fin, last


def _nsa_layer(h, jb, shared, W, slopes, B, T, norm_g):
    N, D = h.shape
    CB = W["w_out_b"].shape[1]
    proj3 = norm_mm(h, norm_g, W["w_in_b"], (jb,)).reshape(B, T, -1)
    if shared["past"] is None:
        o_cmp, selm = nsa_cmp(proj3, shared["kvc"], slopes, tq=min(T, 128), nc=shared["nc"],
                              nsb=shared["nsb"], pos0=0)
        o = nsa_selwin_prompt(proj3, shared["rows"], selm, o_cmp, slopes, tq=min(T, 64))
    else:
        TQ = SUBLANES
        projp = jnp.pad(proj3, ((0, 0), (0, TQ - T), (0, 0)))
        o_cmp, selm = nsa_cmp(projp, shared["kvc"], slopes, tq=TQ, nc=shared["nc"],
                              nsb=shared["nsb"], pos0=shared["pos0"])
        pool, table, cwin = shared["past"]
        o = nsa_selwin_sample(projp, selm, o_cmp, pool, table, shared["new_rows"], cwin, slopes,
                              pos0=shared["pos0"], n_new=T)[:, :T]
    return o.reshape(N, CB)


def _trunk(x, p, pos0, wkv0, shift0, past, W, slopes):
    B, T, D = x.shape
    N = B * T
    depth = p.shape[0]
    n_a = W["w_in_a"].shape[0]
    GW = G_KV * HEAD_B
    h = x.reshape(N, D)
    wkv_new, shift_new = [], []
    shared, kv_rows, win_state = None, None, None
    for i in range(depth):
        if i < n_a:
            o, s_fin, last = _rwkv_layer(h, shift0[i], wkv0[i], i, W, B, T)
            wkv_new.append(s_fin)
            shift_new.append(last)
            h = out_ple(o, h, p[i].reshape(N, -1), W["w_out_a"], i, W["w_ple"], W["w_ple_gate"], i)
        else:
            o = _nsa_layer(h, i - n_a, shared, W, slopes, B, T, W["norm_g"][i])
            h = out_ple(o, h, p[i].reshape(N, -1), W["w_out_b"], i - n_a, W["w_ple"], W["w_ple_gate"], i)
        if i == n_a - 1:
            rows = norm_mm(h, W["kv_norm_g"], W["w_kv"]).reshape(B, T, 6 * GW)
            kv_rows = rows[:, :, :4 * GW].reshape(B, T, 4, G_KV, HEAD_B)
            win_new = rows[:, :, 4 * GW:].reshape(B, T, 2, G_KV, HEAD_B)
            if past is None:
                PS = 128
                pool = rows.reshape(B * T // PS, PS // S_CMP, S_CMP, 6 * G_KV, HEAD_B)
                table = jnp.arange(B * T // PS, dtype=jnp.int32).reshape(B, T // PS)
                t_all = T
                win_all = win_new
                shared = {"past": None, "rows": rows}
            else:
                pool, table, cwin = past
                PS = pool.shape[1] * S_CMP
                t_all = pos0 + T
                win_all = jnp.concatenate([cwin.reshape(B, -1, 2, G_KV, HEAD_B), win_new], axis=1)
                NN = LANES
                shared = {"past": past, "new_rows": jnp.pad(rows, ((0, 0), (0, NN - T), (0, 0)))}
            win_state = win_all[:, win_all.shape[1] - min(WINDOW, pos0 + T):]
            nc = (t_all - L_CMP) // S_CMP + 1
            assert nc < table.shape[1] * PS // S_CMP
            kvc = compress_kv(pool, table, W["pe_cmp"], W["w_cmp1"], W["w_cmp2"])
            shared.update(kvc=kvc, nc=nc, nsb=max(-(-t_all // L_SEL), TOPK_SEL), pos0=pos0)
    y = rmsnorm(h, W["final_norm_g"]).reshape(B, T, D)
    return y, jnp.stack(wkv_new), jnp.stack(shift_new), kv_rows, win_state


def kernel(x_prompt, x_sample, state_wkv, state_shift, cache_kv, cache_win_kv, page_table, p_prompt, p_sample, norm_g, mu_a, w_in_a, w_lora_w1, w_lora_w2, w0_a, a_lora1, a_lora2, a0_a, k_k, k_a, r_k, ln_x_w, ln_x_b, w_out_a, w_in_b, w_out_b, kv_norm_g, w_kv, pe_cmp, w_cmp1, w_cmp2, w_ple, w_ple_gate, final_norm_g):
    bf = lambda w: w.astype(BF16)
    CA = w_out_a.shape[1]
    W = dict(norm_g=norm_g, mu_a=mu_a, w_in_a=bf(w_in_a), w_lora_w1=bf(w_lora_w1), w_lora_w2=bf(w_lora_w2),
             w0_a=w0_a, a_lora1=bf(a_lora1), a_lora2=bf(a_lora2), a0_a=a0_a, k_k=k_k, k_a=k_a,
             r_k=r_k, ln_x_w=ln_x_w, ln_x_b=ln_x_b,
             w_out_a=bf(w_out_a), w_in_b=bf(w_in_b), w_out_b=bf(w_out_b), kv_norm_g=kv_norm_g, w_kv=bf(w_kv),
             pe_cmp=pe_cmp, w_cmp1=bf(w_cmp1), w_cmp2=bf(w_cmp2), w_ple=bf(w_ple), w_ple_gate=bf(w_ple_gate),
             final_norm_g=final_norm_g)
    HB = w_out_b.shape[1] // HEAD_B
    slopes = 2.0 ** (-8.0 * jnp.arange(1, HB + 1, dtype=F32) / HB)
    bp = x_prompt.shape[0]
    n_a = w_in_a.shape[0]
    D = x_prompt.shape[-1]
    wkv0 = jnp.zeros((n_a, bp, CA // HEAD_A, HEAD_A, HEAD_A), F32)
    shift0 = jnp.zeros((n_a, bp, D), F32)
    y_p, wkv_p, shift_p, kv_p, win_p = _trunk(x_prompt, p_prompt, 0, wkv0, shift0, None, W, slopes)
    db, n_pages = page_table.shape
    NP, PS = cache_kv.shape[:2]
    pool5 = cache_kv.reshape(NP, PS // S_CMP, S_CMP, -1, HEAD_B)
    past = (pool5, page_table, cache_win_kv.reshape(db, cache_win_kv.shape[1], -1))
    y_s, wkv_s, shift_s, kv_s, win_s = _trunk(x_sample, p_sample, n_pages * PS, state_wkv, state_shift, past, W, slopes)
    return (y_p, y_s, wkv_p, shift_p, kv_p, win_p, wkv_s, shift_s, kv_s, win_s)
```

```python
import functools
import math

import jax
import jax.numpy as jnp
from jax import lax
from jax.experimental import pallas as pl
from jax.experimental.pallas import tpu as pltpu

F32 = jnp.float32
BF16 = jnp.bfloat16

HEAD_A = 64
GN_EPS = 64e-5
HEAD_B = 128
G_KV = 4
L_CMP = 32
S_CMP = 16
L_SEL = 64
TOPK_SEL = 16
WINDOW = 512
RMS_EPS = 1e-6
NEG = -1e30
FORCE_BONUS = 1e4

LANES = 128
SUBLANES = 8
VMEM_LIMIT = 56 * 1024 * 1024

SCAN_NH = 2
SCAN_C = 64
SCAN_GP = 16

NT_DIMS = (((1,), (1,)), ((), ()))
TN_DIMS = (((0,), (0,)), ((), ()))


def _cparams(sem):
    return pltpu.CompilerParams(dimension_semantics=sem, vmem_limit_bytes=VMEM_LIMIT)


def _bdot(a, b):
    return jnp.dot(a.astype(BF16), b.astype(BF16), preferred_element_type=F32)


def _bdot_nt(a, b):
    return lax.dot_general(a.astype(BF16), b.astype(BF16), NT_DIMS, preferred_element_type=F32)


def _bdot_tn(a, b):
    return lax.dot_general(a.astype(BF16), b.astype(BF16), TN_DIMS, preferred_element_type=F32)


def _rms_kernel(x_ref, g_ref, o_ref):
    x = x_ref[...]
    ms = jnp.mean(x * x, axis=-1, keepdims=True)
    o_ref[...] = x * lax.rsqrt(ms + RMS_EPS) * g_ref[...]


def rmsnorm(x, g):
    M, D = x.shape
    tm = min(M, 256)
    return pl.pallas_call(
        _rms_kernel,
        grid=(pl.cdiv(M, tm),),
        in_specs=[pl.BlockSpec((tm, D), lambda i: (i, 0)),
                  pl.BlockSpec((1, D), lambda i: (0, 0))],
        out_specs=pl.BlockSpec((tm, D), lambda i: (i, 0)),
        out_shape=jax.ShapeDtypeStruct((M, D), F32),
        compiler_params=_cparams(("parallel",)),
        name="rmsnorm",
    )(x, g.reshape(1, D))


def _mm_kernel(x_ref, w_ref, o_ref):
    o_ref[...] = jnp.dot(x_ref[...].astype(BF16), w_ref[...], preferred_element_type=F32)


def mm(x, w, widx=()):
    M, K = x.shape
    N = w.shape[-1]
    assert w.shape[-2] == K and len(widx) == w.ndim - 2
    tm = min(M, 1024 if K <= 2048 else 512)
    tn = N if N <= 512 else 512
    nlead = len(widx)
    w_spec = pl.BlockSpec((None,) * nlead + (K, tn), lambda i, j: tuple(widx) + (0, j))
    return pl.pallas_call(
        _mm_kernel,
        grid=(pl.cdiv(M, tm), pl.cdiv(N, tn)),
        in_specs=[pl.BlockSpec((tm, K), lambda i, j: (i, 0)), w_spec],
        out_specs=pl.BlockSpec((tm, tn), lambda i, j: (i, j)),
        out_shape=jax.ShapeDtypeStruct((M, N), F32),
        compiler_params=_cparams(("parallel", "parallel")),
        name="mm",
    )(x, w)


def _norm_rows(x, g):
    return x * lax.rsqrt(jnp.mean(x * x, axis=-1, keepdims=True) + RMS_EPS) * g


def _sigmoid(x):
    return 1.0 / (1.0 + jnp.exp(-x))


def _norm_and_shift(h_ref, hprev_ref, xprev_ref, g_ref, i, tm, T):
    g = g_ref[...]
    hn = _norm_rows(h_ref[...], g)
    prev_row = _norm_rows(hprev_ref[SUBLANES - 1:SUBLANES, :], g)
    row = lax.broadcasted_iota(jnp.int32, (tm, 1), 0)
    xs = jnp.where(row == 0, prev_row, pltpu.roll(hn, 1, axis=0))
    if T >= tm:
        assert T % tm == 0
        start = (i * tm) % T == 0
        xs = jnp.where((row == 0) & start, xprev_ref[pl.ds((i * tm) // T, 1), :], xs)
    else:
        assert tm % T == 0
        for bb in range(tm // T):
            xs = jnp.where(row == bb * T, xprev_ref[pl.ds(i * (tm // T) + bb, 1), :], xs)
    return hn, xs


def _rwkv_in_kernel(h_ref, hprev_ref, xprev_ref, g_ref, mu_ref, w_ref, o_ref, xs_scr, *, tm, T):
    i, j, n = pl.program_id(0), pl.program_id(1), pl.program_id(2)

    @pl.when(n == 0)
    def _():
        hn, xs = _norm_and_shift(h_ref, hprev_ref, xprev_ref, g_ref, i, tm, T)
        xs_scr[...] = (hn + (xs - hn) * mu_ref[pl.ds(j, 1), :]).astype(BF16)

    o_ref[...] = jnp.dot(xs_scr[...], w_ref[...], preferred_element_type=F32)


def _shift_specs(tm, D, nb, ngrid):
    z = (0,) * (ngrid - 1)
    wrap = lambda f: (lambda i, *_: f(i))
    return [pl.BlockSpec((tm, D), wrap(lambda i: (i, 0))),
            pl.BlockSpec((SUBLANES, D), wrap(lambda i: (jnp.maximum(i * (tm // SUBLANES) - 1, 0), 0))),
            pl.BlockSpec((nb, D), wrap(lambda i: (0, 0))),
            pl.BlockSpec((1, D), wrap(lambda i: (0, 0))),
            pl.BlockSpec((6, D), wrap(lambda i: (0, 0)))]


def rwkv_in(h, xprev, g, mu, w, layer, T):
    N, D = h.shape
    C = w.shape[-1]
    tm = min(N, 1024)
    tn = 512
    return pl.pallas_call(
        functools.partial(_rwkv_in_kernel, tm=tm, T=T),
        grid=(N // tm, 4, C // tn),
        in_specs=_shift_specs(tm, D, xprev.shape[0], 3) + [
            pl.BlockSpec((None, None, D, tn), lambda i, j, n: (layer, j, 0, n))],
        out_specs=pl.BlockSpec((None, tm, tn), lambda i, j, n: (j, i, n)),
        out_shape=jax.ShapeDtypeStruct((4, N, C), F32),
        scratch_shapes=[pltpu.VMEM((tm, D), BF16)],
        compiler_params=_cparams(("parallel", "arbitrary", "arbitrary")),
        name="rwkv_in",
    )(h, h, xprev, g.reshape(1, D), mu, w)


def _rwkv_lora_kernel(h_ref, hprev_ref, xprev_ref, g_ref, mu_ref, lw1_ref, lw2_ref, la1_ref, la2_ref,
                      w0_ref, a0_ref, lw_ref, a_ref, *, tm, T):
    hn, xs = _norm_and_shift(h_ref, hprev_ref, xprev_ref, g_ref, pl.program_id(0), tm, T)
    dx = xs - hn
    x4 = (hn + dx * mu_ref[4:5, :]).astype(BF16)
    x5 = (hn + dx * mu_ref[5:6, :]).astype(BF16)
    t4 = jnp.tanh(jnp.dot(x4, lw1_ref[...], preferred_element_type=F32)).astype(BF16)
    y = -(w0_ref[...] + jnp.dot(t4, lw2_ref[...], preferred_element_type=F32))
    softplus = jnp.maximum(y, 0.0) + jnp.log(1.0 + jnp.exp(-jnp.abs(y)))
    lw_ref[...] = -jnp.exp(-softplus - 0.5)
    t5 = jnp.dot(x5, la1_ref[...], preferred_element_type=F32).astype(BF16)
    a_ref[...] = _sigmoid(a0_ref[...] + jnp.dot(t5, la2_ref[...], preferred_element_type=F32))


def rwkv_lora(h, xprev, g, mu, lw1, lw2, la1, la2, w0, a0, T):
    N, D = h.shape
    R, C = lw2.shape
    tm = min(N, 256)
    full = lambda shape: pl.BlockSpec(shape, lambda i: (0,) * len(shape))
    o_spec = pl.BlockSpec((tm, C), lambda i: (i, 0))
    return pl.pallas_call(
        functools.partial(_rwkv_lora_kernel, tm=tm, T=T),
        grid=(N // tm,),
        in_specs=_shift_specs(tm, D, xprev.shape[0], 1) + [
            full((D, R)), full((R, C)), full((D, R)), full((R, C)), full((1, C)), full((1, C))],
        out_specs=[o_spec, o_spec],
        out_shape=[jax.ShapeDtypeStruct((N, C), F32)] * 2,
        compiler_params=_cparams(("parallel",)),
        name="rwkv_lora",
    )(h, h, xprev, g.reshape(1, D), mu, lw1, lw2, la1, la2, w0.reshape(1, C), a0.reshape(1, C))


def _norm_mm_kernel(h_ref, g_ref, w_ref, o_ref, xs_scr):
    @pl.when(pl.program_id(1) == 0)
    def _():
        xs_scr[...] = _norm_rows(h_ref[...], g_ref[...]).astype(BF16)

    o_ref[...] = jnp.dot(xs_scr[...], w_ref[...], preferred_element_type=F32)


def norm_mm(h, g, w, widx=()):
    N, D = h.shape
    NO = w.shape[-1]
    tm = min(N, 1024)
    tn = 512
    nlead = len(widx)
    return pl.pallas_call(
        _norm_mm_kernel,
        grid=(N // tm, pl.cdiv(NO, tn)),
        in_specs=[pl.BlockSpec((tm, D), lambda i, n: (i, 0)),
                  pl.BlockSpec((1, D), lambda i, n: (0, 0)),
                  pl.BlockSpec((None,) * nlead + (D, tn), lambda i, n: tuple(widx) + (0, n))],
        out_specs=pl.BlockSpec((tm, tn), lambda i, n: (i, n)),
        out_shape=jax.ShapeDtypeStruct((N, NO), F32),
        scratch_shapes=[pltpu.VMEM((tm, D), BF16)],
        compiler_params=_cparams(("parallel", "arbitrary")),
        name="norm_mm",
    )(h, g.reshape(1, D), w)


def _out_ple_kernel(x_ref, h_ref, p_ref, wo_ref, wp_ref, wg_ref, o_ref, h1_scr, h1b_scr, *, NT):
    n = pl.program_id(1)

    @pl.when(n < NT)
    def _():
        h1 = h_ref[...] + jnp.dot(x_ref[...].astype(BF16), wo_ref[...], preferred_element_type=F32)
        h1_scr[n] = h1
        h1b_scr[n] = h1.astype(BF16)

    @pl.when(n >= NT)
    def _():
        h1b = jnp.concatenate([h1b_scr[t] for t in range(NT)], axis=1)
        gate = jnp.dot(h1b, wg_ref[...], preferred_element_type=F32)
        ple = jnp.dot(p_ref[...].astype(BF16), wp_ref[...], preferred_element_type=F32)
        o_ref[...] = h1_scr[n - NT] + ple * _sigmoid(gate)


def out_ple(x, h, p, w_out, oidx, w_ple, w_gate, layer):
    N, C = x.shape
    D = h.shape[1]
    DP = p.shape[1]
    tm = min(N, 512)
    tn = 512
    NT = D // tn
    lo = lambda n: jnp.minimum(n, NT - 1)
    hi = lambda n: jnp.maximum(n - NT, 0)
    return pl.pallas_call(
        functools.partial(_out_ple_kernel, NT=NT),
        grid=(N // tm, 2 * NT),
        in_specs=[pl.BlockSpec((tm, C), lambda i, n: (i, 0)),
                  pl.BlockSpec((tm, tn), lambda i, n: (i, lo(n))),
                  pl.BlockSpec((tm, DP), lambda i, n: (i, 0)),
                  pl.BlockSpec((None, C, tn), lambda i, n: (oidx, 0, lo(n))),
                  pl.BlockSpec((None, DP, tn), lambda i, n: (layer, 0, hi(n))),
                  pl.BlockSpec((None, D, tn), lambda i, n: (layer, 0, hi(n)))],
        out_specs=pl.BlockSpec((tm, tn), lambda i, n: (i, hi(n))),
        out_shape=jax.ShapeDtypeStruct((N, D), F32),
        scratch_shapes=[pltpu.VMEM((NT, tm, tn), F32), pltpu.VMEM((NT, tm, tn), BF16)],
        compiler_params=_cparams(("parallel", "arbitrary")),
        name="out_ple",
    )(x, h, p, w_out, w_ple, w_gate)


def _scan_kernel(r_ref, k_ref, v_ref, zg_ref, lw_ref, a_ref, pv_ref, s0_ref, o_ref, sfin_ref, s_scr, *, C, NH, GP):
    L = NH * HEAD_A
    NC = NH * C
    ci = pl.program_id(2)

    @pl.when(ci == 0)
    def _():
        s_scr[...] = s0_ref[0]

    row_c = lax.broadcasted_iota(jnp.int32, (C, NC), 0)
    col_s = lax.broadcasted_iota(jnp.int32, (C, NC), 1) % C
    tri_strict = col_s < row_c
    tri_incl = (lax.broadcasted_iota(jnp.int32, (C, 2 * NC), 1) % C
                <= lax.broadcasted_iota(jnp.int32, (C, 2 * NC), 0))
    st_mask = (lax.broadcasted_iota(jnp.int32, (NC, L), 0) // C
               == lax.broadcasted_iota(jnp.int32, (NC, L), 1) // HEAD_A)
    bd_mask = (lax.broadcasted_iota(jnp.int32, (NC, NC), 0) // C
               == lax.broadcasted_iota(jnp.int32, (NC, NC), 1) // C)
    head_mask = (lax.broadcasted_iota(jnp.int32, (L, L), 0) // HEAD_A
                 == lax.broadcasted_iota(jnp.int32, (L, L), 1) // HEAD_A)
    cum_mat = (lax.broadcasted_iota(jnp.int32, (C, C), 1)
               <= lax.broadcasted_iota(jnp.int32, (C, C), 0)).astype(F32)

    def st(x):
        return jnp.where(st_mask, jnp.concatenate([x] * NH, axis=0), 0.0)

    def bd(w):
        return jnp.where(bd_mask, jnp.concatenate([w] * NH, axis=0), 0.0)

    n_double = int(math.log2(C))
    each = lambda f, *cols: [f(*xs) for xs in zip(*cols)]
    sls = [slice(gp * L, (gp + 1) * L) for gp in range(GP)]
    head_of_lane = lax.broadcasted_iota(jnp.int32, (1, L), 1) // HEAD_A

    def hsum(x):
        out = None
        for hh in range(NH):
            sh = jnp.sum(jnp.where(head_of_lane == hh, x, 0.0), axis=-1, keepdims=True)
            out = sh if out is None else jnp.where(head_of_lane == hh, sh, out)
        return out

    k_k, k_a, r_k, ln_w, ln_b = ([pv_ref[n:n + 1, sl] for sl in sls] for n in range(5))
    lw = [lw_ref[0, :, sl] for sl in sls]
    a_sig = [a_ref[0, :, sl] for sl in sls]
    r = [r_ref[0, :, sl] for sl in sls]
    k_raw = [k_ref[0, :, sl] for sl in sls]
    v = [v_ref[0, :, sl] for sl in sls]
    kk = each(lambda x, w: x * w, k_raw, k_k)
    kk = each(lambda x: x / jnp.maximum(jnp.sqrt(hsum(x * x)), 1e-12), kk)
    k = each(lambda x, a, w: x * (1.0 + (a - 1.0) * w), k_raw, a_sig, k_a)
    cum = each(lambda z: jnp.dot(cum_mat, z, precision=lax.Precision.HIGHEST, preferred_element_type=F32), lw)
    p_incl = each(jnp.exp, cum)
    p_inv = each(lambda z: jnp.exp(-z), cum)
    at = each(lambda x, c, w: -x * jnp.exp(c - w), kk, cum, lw)
    rt = each(lambda x, p: x * p, r, p_incl)
    bt = each(lambda x, a, p: x * a * p, kk, a_sig, p_inv)
    kt = each(lambda x, p: x * p, k, p_inv)
    S = [s_scr[gp] for gp in range(GP)]
    ar = each(lambda x, y: jnp.concatenate([x, y], axis=0), at, rt)
    bk_st = each(lambda x, y: jnp.concatenate([st(x), st(y)], axis=0), bt, kt)
    Gm = each(_bdot_nt, ar, bk_st)
    w_ab = [jnp.where(tri_strict, g[:C, :NC], 0.0) for g in Gm]
    tm = w_ab
    pw = each(lambda w: _bdot(w, bd(w)), w_ab)
    LH = each(_bdot_nt, ar, S)
    v_st = each(st, v)
    x = [lh[:C] + _bdot(jnp.where(tri_strict, g[:C, NC:], 0.0), vs) for lh, g, vs in zip(LH, Gm, v_st)]
    for it in range(1, n_double):
        tm_next = each(lambda t, p: t + p + _bdot(p, bd(t)), tm, pw)
        if it < n_double - 1:
            pw = each(lambda p: _bdot(p, bd(p)), pw)
        tm = tm_next
    u = each(lambda xx, t: xx + _bdot(t, st(xx)), x, tm)
    o = [lh[C:] + _bdot(jnp.where(tri_incl, g[C:], 0.0), jnp.concatenate([st(uu), vs], axis=0))
         for lh, g, uu, vs in zip(LH, Gm, u, v_st)]
    inv_n = 1.0 / HEAD_A
    dev = each(lambda x: x - hsum(x) * inv_n, o)
    gn = each(lambda d, w, b_: d * lax.rsqrt(hsum(d * d) * inv_n + GN_EPS) * w + b_, dev, ln_w, ln_b)
    bonus = each(lambda rr, kx, w, vv: hsum(rr * kx * w) * vv, r, k, r_k, v)
    for sl, y, bo in zip(sls, gn, bonus):
        zg = zg_ref[0, :, sl]
        o_ref[0, :, sl] = ((y + bo) * (zg * _sigmoid(zg))).astype(o_ref.dtype)
    ds = [_bdot_tn(jnp.concatenate([uu, vv], axis=0), jnp.concatenate([b_, k_], axis=0))
          for uu, vv, b_, k_ in zip(u, v, bt, kt)]
    for gp in range(GP):
        s_scr[gp] = (S[gp] + jnp.where(head_mask, ds[gp], 0.0)) * p_incl[gp][C - 1:C, :]

    @pl.when(ci == pl.num_programs(2) - 1)
    def _():
        sfin_ref[0] = s_scr[...]


def rwkv_scan(rkvg, lw, a, pvec, s0):
    _, B, T, CA = rkvg.shape
    H = CA // HEAD_A
    NH, C, GP = SCAN_NH, SCAN_C, SCAN_GP
    assert NH * C == LANES and H % (NH * GP) == 0
    L = NH * HEAD_A
    NG = H // NH
    Tp = -(-T // C) * C
    if Tp != T:
        rkvg = jnp.pad(rkvg, ((0, 0), (0, 0), (0, Tp - T), (0, 0)))
        lw, a = (jnp.pad(z, ((0, 0), (0, Tp - T), (0, 0))) for z in (lw, a))
    eye = jnp.eye(NH, dtype=F32)
    s0_bd = (s0.reshape(B, NG, NH, HEAD_A, 1, HEAD_A) * eye[None, None, :, None, :, None]).reshape(B, NG, L, L)
    seq_spec = pl.BlockSpec((1, C, GP * L), lambda bi, gi, ci: (bi, ci, gi))
    proj_spec = lambda j: pl.BlockSpec((None, 1, C, GP * L), lambda bi, gi, ci: (j, bi, ci, gi))
    st_spec = pl.BlockSpec((1, GP, L, L), lambda bi, gi, ci: (bi, gi, 0, 0))
    o, sfin = pl.pallas_call(
        functools.partial(_scan_kernel, C=C, NH=NH, GP=GP),
        grid=(B, NG // GP, Tp // C),
        in_specs=[proj_spec(j) for j in range(4)] + [seq_spec, seq_spec,
                  pl.BlockSpec((5, GP * L), lambda bi, gi, ci: (0, gi)), st_spec],
        out_specs=[seq_spec, st_spec],
        out_shape=[jax.ShapeDtypeStruct((B, Tp, CA), BF16), jax.ShapeDtypeStruct((B, NG, L, L), F32)],
        scratch_shapes=[pltpu.VMEM((GP, L, L), F32)],
        compiler_params=_cparams(("parallel", "parallel", "arbitrary")),
        name="rwkv_scan",
    )(rkvg, rkvg, rkvg, rkvg, lw, a, pvec, s0_bd)
    sf = sfin.reshape(B, NG, NH, HEAD_A, NH, HEAD_A)
    s_fin = jnp.stack([sf[:, :, h, :, h, :] for h in range(NH)], axis=2).reshape(B, H, HEAD_A, HEAD_A)
    return o[:, :T], s_fin


def _gelu_tanh(x):
    c = math.sqrt(2.0 / math.pi)
    return 0.5 * x * (1.0 + jnp.tanh(c * (x + 0.044715 * (x * x * x))))


def _compress_kernel(pt_ref, *refs, PGS):
    del pt_ref
    page_refs = refs[:PGS]
    next_ref, pe_ref, w1_ref, w2_ref, out_ref = refs[PGS:]
    CPP = page_refs[0].shape[1]
    NCH = PGS * CPP
    CG = 2 * G_KV
    M = (NCH + 1) * CG

    def rows_of(l, hf):
        pe = pe_ref[hf, l]
        parts = [(page_refs[i][0, :, l] + pe[None]).reshape(CPP * CG, HEAD_B) for i in range(PGS)]
        parts.append(next_ref[0, 0, l] + pe)
        return jnp.concatenate(parts, axis=0)

    top = jnp.zeros((M, 2 * HEAD_B), F32)
    bot = jnp.zeros((M, 2 * HEAD_B), F32)
    for l in range(0, S_CMP, 2):
        wrows = pl.ds(l * HEAD_B, 2 * HEAD_B)
        xt = jnp.concatenate([rows_of(l, 0), rows_of(l + 1, 0)], axis=1).astype(BF16)
        top = top + jnp.dot(xt, w1_ref[0, wrows, :], preferred_element_type=F32)
        xb = jnp.concatenate([rows_of(l, 1), rows_of(l + 1, 1)], axis=1).astype(BF16)
        bot = bot + jnp.dot(xb, w1_ref[1, wrows, :], preferred_element_type=F32)
    is_k = (lax.broadcasted_iota(jnp.int32, (M, 1), 0) % CG) < G_KV
    pick = lambda z, n: jnp.where(is_k[:n], z[:n, :HEAD_B], z[:n, HEAD_B:])
    hcur = pick(top, NCH * CG) + pick(bot, M)[CG:]
    o2 = jnp.dot(_gelu_tanh(hcur).astype(BF16), w2_ref[...], preferred_element_type=F32)
    out_ref[0] = pick(o2, NCH * CG).reshape(NCH, CG, HEAD_B)


def compress_kv(pool5, table, pe_cmp, w1, w2):
    NP, CPP = pool5.shape[:2]
    B, n_pages = table.shape
    PGS = max(d for d in (8, 4, 2, 1) if n_pages % d == 0)
    NCH = PGS * CPP
    CG = 2 * G_KV
    half = S_CMP * HEAD_B
    pe_r = jnp.repeat(pe_cmp.reshape(2, 2, S_CMP, HEAD_B).transpose(1, 2, 0, 3), G_KV, axis=2)
    w1_r = w1.reshape(2, 2, half, HEAD_B).transpose(1, 2, 0, 3).reshape(2, half, 2 * HEAD_B)
    w2_r = jnp.concatenate([w2[0], w2[1]], axis=1)

    def page_map(i):
        return lambda b, s, pt: (pt[b, s * PGS + i], 0, 0, 0, 0)

    def next_map(b, s, pt):
        return (pt[b, jnp.minimum((s + 1) * PGS, n_pages - 1)], 0, 0, 0, 0)

    const = lambda n: (lambda b, s, pt: (0,) * n)
    grid_spec = pltpu.PrefetchScalarGridSpec(
        num_scalar_prefetch=1,
        grid=(B, n_pages // PGS),
        in_specs=[pl.BlockSpec((1, CPP, S_CMP, CG, HEAD_B), page_map(i)) for i in range(PGS)] + [
            pl.BlockSpec((1, 1, S_CMP, CG, HEAD_B), next_map),
            pl.BlockSpec((2, S_CMP, CG, HEAD_B), const(4)),
            pl.BlockSpec((2, half, 2 * HEAD_B), const(3)),
            pl.BlockSpec((HEAD_B, 2 * HEAD_B), const(2)),
        ],
        out_specs=pl.BlockSpec((1, NCH, CG, HEAD_B), lambda b, s, pt: (b, s, 0, 0)),
    )
    return pl.pallas_call(
        functools.partial(_compress_kernel, PGS=PGS),
        grid_spec=grid_spec,
        out_shape=jax.ShapeDtypeStruct((B, n_pages * CPP, CG, HEAD_B), F32),
        compiler_params=_cparams(("parallel", "arbitrary")),
        name="compress_kv",
    )(table, *([pool5] * PGS), pool5, pe_r, w1_r, w2_r)


def _stack_heads(q, HG):
    return jnp.concatenate([q[:, h * HEAD_B:(h + 1) * HEAD_B] for h in range(HG)], axis=0)


def _masked_softmax_rows(s, mask):
    s = jnp.where(mask, s, NEG)
    m = jnp.max(s, axis=-1, keepdims=True)
    e = jnp.where(mask, jnp.exp(s - m), 0.0)
    l = jnp.sum(e, axis=-1, keepdims=True)
    return e / jnp.where(l > 0.0, l, 1.0)


def _attend_stacked(s, dist, mask, v, slopes_ref, g, HG, tq):
    ps = []
    psum = jnp.zeros(dist.shape, F32)
    for h in range(HG):
        p = _masked_softmax_rows(s[h * tq:(h + 1) * tq] - slopes_ref[g * HG + h] * dist, mask)
        psum = psum + p
        ps.append(p.astype(BF16))
    o = jnp.dot(jnp.concatenate(ps, axis=0), v.astype(BF16), preferred_element_type=F32)
    return o, psum


def _unstack_store(o_ref, o, HG, tq):
    for h in range(HG):
        o_ref[0, :, h * HEAD_B:(h + 1) * HEAD_B] = o[h * tq:(h + 1) * tq]


def _nsa_cmp_kernel(slopes_ref, q_ref, kc_ref, vc_ref, o_ref, selm_ref, *, tq, HG, nc, nsb, pos0):
    g = pl.program_id(1)
    qt = pl.program_id(2)
    NCp = kc_ref.shape[1]
    NSBp = selm_ref.shape[3]
    q_st = _stack_heads(q_ref[0] * (HEAD_B ** -0.5), HG)
    s = _bdot_nt(q_st, kc_ref[0])
    qpos = pos0 + qt * tq + lax.broadcasted_iota(jnp.int32, (tq, 1), 0)
    cidx = lax.broadcasted_iota(jnp.int32, (1, NCp), 1)
    cend = S_CMP * cidx + (L_CMP - 1)
    mask = (cend <= qpos) & (cidx < nc)
    dist = (qpos - cend).astype(F32)
    o, imp_c = _attend_stacked(s, dist, mask, vc_ref[0], slopes_ref, g, HG, tq)
    _unstack_store(o_ref, o, HG, tq)

    crow = lax.broadcasted_iota(jnp.int32, (NCp, NSBp), 0)
    jcol = lax.broadcasted_iota(jnp.int32, (NCp, NSBp), 1)
    overlap = ((S_CMP * crow < L_SEL * (jcol + 1)) & (S_CMP * crow + L_CMP > L_SEL * jcol)
               & (crow < nc)).astype(F32)
    imp = jnp.dot(imp_c, overlap, precision=lax.Precision.HIGHEST, preferred_element_type=F32)
    lane = lax.broadcasted_iota(jnp.int32, (tq, NSBp), 1)
    cur = jnp.right_shift(qpos, int(math.log2(L_SEL)))
    forced = ((lane == 0) | (lane == cur) | (lane == cur - 1)).astype(F32)
    score = jnp.where(lane <= cur, imp + FORCE_BONUS * forced, NEG)
    score = jnp.where(lane < nsb, score, -3e38)

    cnt = jnp.zeros((tq, NSBp), F32)
    for i in range(nsb):
        col = score[:, i:i + 1]
        beats = (col > score) | ((col == score) & (lane > i))
        cnt = cnt + jnp.where(beats, 1.0, 0.0)
    sel = (cnt < TOPK_SEL) & (score > 0.5 * NEG)
    selm_ref[0, 0] = sel.astype(F32)


def nsa_cmp(proj, kvc, slopes, *, tq, nc, nsb, pos0):
    B, T, _ = proj.shape
    HG = slopes.shape[0] // G_KV
    NCp = kvc.shape[1]
    NSBp = -(-nsb // LANES) * LANES
    gw = HG * HEAD_B
    return pl.pallas_call(
        functools.partial(_nsa_cmp_kernel, tq=tq, HG=HG, nc=nc, nsb=nsb, pos0=pos0),
        grid=(B, G_KV, T // tq),
        in_specs=[pl.BlockSpec(memory_space=pltpu.SMEM),
                  pl.BlockSpec((1, tq, gw), lambda b, g, t: (b, t, g)),
                  pl.BlockSpec((1, NCp, HEAD_B), lambda b, g, t: (b, 0, g)),
                  pl.BlockSpec((1, NCp, HEAD_B), lambda b, g, t: (b, 0, G_KV + g))],
        out_specs=[pl.BlockSpec((1, tq, gw), lambda b, g, t: (b, t, g)),
                   pl.BlockSpec((1, 1, tq, NSBp), lambda b, g, t: (b, g, t, 0))],
        out_shape=[jax.ShapeDtypeStruct((B, T, G_KV * gw), F32),
                   jax.ShapeDtypeStruct((B, G_KV, T, NSBp), F32)],
        compiler_params=_cparams(("parallel", "parallel", "parallel")),
        name="nsa_cmp",
    )(slopes, proj, kvc.reshape(B, NCp, -1), kvc.reshape(B, NCp, -1))


LOG2E = 1.4426950408889634


def _bf16_part(x):
    return x.astype(BF16).astype(F32)


def _alibi_lhs(q, slope_col):
    c = slope_col * LOG2E
    c1 = _bf16_part(c)
    c2 = _bf16_part(c - c1)
    c3 = _bf16_part(c - c1 - c2)
    lane = lax.broadcasted_iota(jnp.int32, q.shape, 1)
    extra = jnp.where((lane == 0) | (lane == 3), c1,
                      jnp.where((lane == 1) | (lane == 4), c2, jnp.where((lane == 2) | (lane == 5), c3, 0.0)))
    return jnp.concatenate([q, extra], axis=1).astype(BF16)


def _alibi_rhs(k, k0):
    pos = k0 + lax.broadcasted_iota(jnp.int32, k.shape, 0)
    lane = lax.broadcasted_iota(jnp.int32, k.shape, 1)
    hi = jnp.bitwise_and(pos, -L_SEL)
    extra = jnp.where(lane < 3, hi, jnp.where(lane < 6, pos - hi, 0)).astype(F32)
    return jnp.concatenate([k, extra], axis=1).astype(BF16)


def _with_ones(v):
    lane = lax.broadcasted_iota(jnp.int32, v.shape, 1)
    return jnp.concatenate([v, jnp.where(lane == 0, 1.0, 0.0)], axis=1).astype(BF16)


def _silu(x):
    return x * _sigmoid(x)


def _nsa_selwin_prompt_kernel(slopes_ref, q_ref, selm_ref, ks_ref, vs_ref, kw_ref, vw_ref,
                              ocmp_ref, zc_ref, zs_ref, zw_ref, gate_ref, o_ref, s_scr, *, tq, HG, T, WS, SEG):
    g = pl.program_id(1)
    qt = pl.program_id(2)
    NSBp = selm_ref.shape[3]
    R = HG * tq
    qpos = qt * tq + lax.broadcasted_iota(jnp.int32, (tq, 1), 0)
    slope_col = jnp.concatenate([jnp.full((tq, 1), slopes_ref[g * HG + h], F32) for h in range(HG)], axis=0)
    tile_heads = lambda x: jnp.concatenate([x] * HG, axis=0)
    q2 = _alibi_lhs(_stack_heads(q_ref[0] * (HEAD_B ** -0.5 * LOG2E), HG), slope_col)

    selm_b = selm_ref[0, 0].astype(BF16)
    nseg = (qt * tq + tq + SEG - 1) // SEG

    def seg_scores(si, m):
        k0 = pl.multiple_of(si * SEG, SEG)
        kcol = k0 + lax.broadcasted_iota(jnp.int32, (NSBp, SEG), 1)
        expand = (jnp.right_shift(kcol, int(math.log2(L_SEL)))
                  == lax.broadcasted_iota(jnp.int32, (NSBp, SEG), 0)).astype(BF16)
        in_blk = jnp.dot(selm_b, expand, preferred_element_type=F32) > 0.5
        kpos = k0 + lax.broadcasted_iota(jnp.int32, (1, SEG), 1)
        mask = tile_heads(in_blk & (kpos <= qpos))
        s = lax.dot_general(q2, _alibi_rhs(ks_ref[0, pl.ds(k0, SEG), :], k0), NT_DIMS, preferred_element_type=F32)
        s = jnp.where(mask, s, NEG)
        s_scr[si] = s
        return jnp.maximum(m, jnp.max(s, axis=-1, keepdims=True))

    m = lax.fori_loop(0, nseg, seg_scores, jnp.full((R, 1), NEG, F32))

    def seg_pv(si, acc):
        k0 = pl.multiple_of(si * SEG, SEG)
        e = jnp.exp2(s_scr[si] - m).astype(BF16)
        return acc + jnp.dot(e, _with_ones(vs_ref[0, pl.ds(k0, SEG), :]), preferred_element_type=F32)

    acc = lax.fori_loop(0, nseg, seg_pv, jnp.zeros((R, 2 * HEAD_B), F32))
    o = acc[:, :HEAD_B] / acc[:, HEAD_B:HEAD_B + 1]

    start = pl.multiple_of(jnp.clip(qt * tq - WINDOW, 0, T - WS), SUBLANES)
    distw = qpos - (start + lax.broadcasted_iota(jnp.int32, (1, WS), 1))
    maskw = tile_heads((distw >= 0) & (distw < WINDOW))
    sw = lax.dot_general(q2, _alibi_rhs(kw_ref[0, pl.ds(start, WS), :], start), NT_DIMS, preferred_element_type=F32)
    sw = jnp.where(maskw, sw, NEG)
    ew = jnp.exp2(sw - jnp.max(sw, axis=-1, keepdims=True)).astype(BF16)
    accw = jnp.dot(ew, _with_ones(vw_ref[0, pl.ds(start, WS), :]), preferred_element_type=F32)
    ow = accw[:, :HEAD_B] / accw[:, HEAD_B:HEAD_B + 1]

    HB = G_KV * HG
    gates = _sigmoid(gate_ref[0])
    lane = lax.broadcasted_iota(jnp.int32, gates.shape, 1)
    gate_col = lambda idx: jnp.sum(jnp.where(lane == idx, gates, 0.0), axis=1, keepdims=True)
    for h in range(HG):
        hs = slice(h * HEAD_B, (h + 1) * HEAD_B)
        rs = slice(h * tq, (h + 1) * tq)
        hd = g * HG + h
        y = (gate_col(hd) * ocmp_ref[0, :, hs] * _silu(zc_ref[0, :, hs])
             + gate_col(HB + hd) * o[rs] * _silu(zs_ref[0, :, hs])
             + gate_col(2 * HB + hd) * ow[rs] * _silu(zw_ref[0, :, hs]))
        o_ref[0, :, hs] = y.astype(o_ref.dtype)


def nsa_selwin_prompt(proj, rows, selm, o_cmp, slopes, *, tq):
    B, T, _ = proj.shape
    HG = slopes.shape[0] // G_KV
    NSBp = selm.shape[3]
    gw = HG * HEAD_B
    CB = G_KV * gw
    WS = min(T, WINDOW + tq)
    SEG = min(T, 512)
    assert T % SEG == 0
    kv_spec = lambda c: pl.BlockSpec((1, T, HEAD_B), lambda b, g, t: (b, 0, c * G_KV + g))
    head_spec = lambda blk: pl.BlockSpec((1, tq, gw), lambda b, g, t: (b, t, blk * G_KV + g))
    return pl.pallas_call(
        functools.partial(_nsa_selwin_prompt_kernel, tq=tq, HG=HG, T=T, WS=WS, SEG=SEG),
        grid=(B, G_KV, T // tq),
        in_specs=[pl.BlockSpec(memory_space=pltpu.SMEM),
                  head_spec(0),
                  pl.BlockSpec((1, 1, tq, NSBp), lambda b, g, t: (b, g, t, 0)),
                  kv_spec(2), kv_spec(3), kv_spec(4), kv_spec(5),
                  head_spec(0), head_spec(1), head_spec(2), head_spec(3),
                  pl.BlockSpec((1, tq, LANES), lambda b, g, t: (b, t, 4 * CB // LANES))],
        out_specs=head_spec(0),
        out_shape=jax.ShapeDtypeStruct((B, T, CB), BF16),
        scratch_shapes=[pltpu.VMEM((T // SEG, HG * tq, SEG), F32)],
        compiler_params=_cparams(("parallel", "parallel", "parallel")),
        name="nsa_selwin_prompt",
    )(slopes, proj, selm, rows, rows, rows, rows, o_cmp, proj, proj, proj, proj)


def _nsa_selwin_sample_kernel(pt_ref, slopes_ref, q_ref, selm_ref, selst_ref, *refs, PGS, PS, HG, TQ, pos0, n_new, n_win):
    del pt_ref
    page_refs = refs[:PGS]
    (new_ref, cwin_ref, ocmp_ref, zc_ref, zs_ref, zw_ref, gate_ref, o_ref,
     m_scr, l_scr, acc_scr) = refs[PGS:]
    st = pl.program_id(1)
    NSBp = selm_ref.shape[3]
    GW = G_KV * HEAD_B
    sel_shift = int(math.log2(L_SEL))

    @pl.when(st == 0)
    def _():
        m_scr[...] = jnp.full(m_scr.shape, NEG, F32)
        l_scr[...] = jnp.zeros(l_scr.shape, F32)
        acc_scr[...] = jnp.zeros(acc_scr.shape, F32)

    qpos = pos0 + lax.broadcasted_iota(jnp.int32, (TQ, 1), 0)
    lane_j = lax.broadcasted_iota(jnp.int32, (TQ, NSBp), 1)
    tile_heads = lambda x: jnp.concatenate([x] * HG, axis=0)

    def sel_col(selm_g, j):
        return jnp.sum(jnp.where(lane_j == j, selm_g, 0.0), axis=1, keepdims=True)

    def online_update(s, mask, v):
        gs = range(G_KV)
        m_old = [m_scr[g] for g in gs]
        m_new = [jnp.maximum(m_old[g], jnp.max(jnp.where(mask[g], s[g], NEG), axis=-1, keepdims=True)) for g in gs]
        e = [jnp.where(mask[g], jnp.exp(s[g] - m_new[g]), 0.0) for g in gs]
        alpha = [jnp.exp(m_old[g] - m_new[g]) for g in gs]
        pv = [_bdot(e[g], v[g]) for g in gs]
        for g in gs:
            l_scr[g] = alpha[g] * l_scr[g] + jnp.sum(e[g], axis=-1, keepdims=True)
            acc_scr[g] = alpha[g] * acc_scr[g] + pv[g]
            m_scr[g] = m_new[g]

    NK = PGS * PS
    kpos = st * NK + lax.broadcasted_iota(jnp.int32, (1, NK), 1)
    expand = (jnp.right_shift(lax.broadcasted_iota(jnp.int32, (LANES, NK), 1), sel_shift)
              == lax.broadcasted_iota(jnp.int32, (LANES, NK), 0)).astype(BF16)
    in_blk_all = jnp.dot(selst_ref[0, 0].astype(BF16), expand, preferred_element_type=F32)
    dist = qpos - kpos
    distf = tile_heads(dist.astype(F32))
    gs = range(G_KV)
    q_st = [_stack_heads(q_ref[0, :, g * HG * HEAD_B:(g + 1) * HG * HEAD_B] * (HEAD_B ** -0.5), HG).astype(BF16)
            for g in gs]
    slope_col = [jnp.concatenate([jnp.full((TQ, 1), slopes_ref[g * HG + h], F32) for h in range(HG)], axis=0)
                 for g in gs]
    page_rows = lambda i, cg: page_refs[i][0, :, :, cg, :].reshape(PS, HEAD_B)
    k = [jnp.concatenate([page_rows(i, g) for i in range(PGS)], axis=0) for g in gs]
    v = [jnp.concatenate([page_rows(i, G_KV + g) for i in range(PGS)], axis=0) for g in gs]
    mask = [tile_heads((in_blk_all[g * TQ:(g + 1) * TQ] > 0.5) & (dist >= 0)) for g in gs]
    s = [_bdot_nt(q_st[g], k[g]) - slope_col[g] * distf for g in gs]
    online_update(s, mask, v)

    @pl.when(st == pl.num_programs(1) - 1)
    def _():
        NN = new_ref.shape[1]
        rnew = lax.broadcasted_iota(jnp.int32, (1, NN), 1)
        kpos_n = pos0 + rnew
        dist_n = qpos - kpos_n
        ok_n = (rnew < n_new) & (dist_n >= 0)
        jn = pos0 >> sel_shift
        kpos_w = pos0 - n_win + lax.broadcasted_iota(jnp.int32, (1, n_win), 1)
        dist_w = qpos - kpos_w
        dist_wall = jnp.concatenate([dist_w, dist_n], axis=1)
        mask_wall = jnp.concatenate([(dist_w >= 0) & (dist_w < WINDOW), ok_n & (dist_n < WINDOW)], axis=1)
        kn = [new_ref[0, :, 2 * GW + g * HEAD_B:2 * GW + (g + 1) * HEAD_B] for g in gs]
        vn = [new_ref[0, :, 3 * GW + g * HEAD_B:3 * GW + (g + 1) * HEAD_B] for g in gs]
        mask_n = [tile_heads((sel_col(selm_ref[0, g], jn) > 0.5) & ok_n) for g in gs]
        dist_nf = tile_heads(dist_n.astype(F32))
        s_n = [_bdot_nt(q_st[g], kn[g]) - slope_col[g] * dist_nf for g in gs]
        online_update(s_n, mask_n, vn)
        kw = [jnp.concatenate([cwin_ref[0, :, g * HEAD_B:(g + 1) * HEAD_B],
                               new_ref[0, :, 4 * GW + g * HEAD_B:4 * GW + (g + 1) * HEAD_B]], axis=0) for g in gs]
        vw = [jnp.concatenate([cwin_ref[0, :, GW + g * HEAD_B:GW + (g + 1) * HEAD_B],
                               new_ref[0, :, 5 * GW + g * HEAD_B:5 * GW + (g + 1) * HEAD_B]], axis=0) for g in gs]
        dist_wf = tile_heads(dist_wall.astype(F32))
        mask_w = tile_heads(mask_wall)
        sw = [_bdot_nt(q_st[g], kw[g]) - slope_col[g] * dist_wf for g in gs]
        pw = [_masked_softmax_rows(sw[g], mask_w) for g in gs]
        ow = [_bdot(pw[g], vw[g]) for g in gs]
        HB = G_KV * HG
        gates = _sigmoid(gate_ref[0])
        for g in gs:
            l = l_scr[g]
            o = acc_scr[g] / jnp.where(l > 0.0, l, 1.0)
            for h in range(HG):
                hd = g * HG + h
                hs = slice(hd * HEAD_B, (hd + 1) * HEAD_B)
                rs = slice(h * TQ, (h + 1) * TQ)
                y = (gates[:, hd:hd + 1] * ocmp_ref[0, :, hs] * _silu(zc_ref[0, :, hs])
                     + gates[:, HB + hd:HB + hd + 1] * o[rs] * _silu(zs_ref[0, :, hs])
                     + gates[:, 2 * HB + hd:2 * HB + hd + 1] * ow[g][rs] * _silu(zw_ref[0, :, hs]))
                o_ref[0, :, hs] = y.astype(o_ref.dtype)


def nsa_selwin_sample(proj, selm, o_cmp, pool5, table, new_rows, cwin, slopes, *, pos0, n_new):
    B, TQ, _ = proj.shape
    HG = slopes.shape[0] // G_KV
    NP, CPP = pool5.shape[:2]
    PS = CPP * S_CMP
    n_pages = table.shape[1]
    NSBp = selm.shape[3]
    PGS = max(d for d in (8, 4, 2, 1) if n_pages % d == 0)
    CB = G_KV * HG * HEAD_B
    GW = G_KV * HEAD_B
    NN = new_rows.shape[1]
    n_win = cwin.shape[1]
    assert pos0 % L_SEL == 0 and n_new <= L_SEL and pos0 == n_pages * PS

    def page_map(i):
        return lambda b, s, pt: (pt[b, s * PGS + i], 0, 0, 1, 0)

    n_steps = n_pages // PGS
    bps = PGS * PS // L_SEL
    assert bps <= LANES
    selst = selm[:, :, :, :n_steps * bps].reshape(B, G_KV, TQ, n_steps, bps).transpose(0, 3, 1, 2, 4)
    selst = jnp.pad(selst.reshape(B, n_steps, G_KV * TQ, bps), ((0, 0), (0, 0), (0, 0), (0, LANES - bps)))

    const = lambda b, s, pt: (b, 0, 0)
    wide = lambda blk: pl.BlockSpec((1, TQ, CB), lambda b, s, pt: (b, 0, blk))
    grid_spec = pltpu.PrefetchScalarGridSpec(
        num_scalar_prefetch=1,
        grid=(B, n_steps),
        in_specs=[pl.BlockSpec(memory_space=pltpu.SMEM),
                  pl.BlockSpec((1, TQ, CB), const),
                  pl.BlockSpec((1, G_KV, TQ, NSBp), lambda b, s, pt: (b, 0, 0, 0)),
                  pl.BlockSpec((1, 1, G_KV * TQ, LANES), lambda b, s, pt: (b, s, 0, 0))]
                 + [pl.BlockSpec((1, CPP, S_CMP, 2 * G_KV, HEAD_B), page_map(i)) for i in range(PGS)]
                 + [pl.BlockSpec((1, NN, 6 * GW), const),
                    pl.BlockSpec((1, n_win, 2 * GW), const),
                    wide(0), wide(1), wide(2), wide(3),
                    pl.BlockSpec((1, TQ, LANES), lambda b, s, pt: (b, 0, 4 * CB // LANES))],
        out_specs=wide(0),
        scratch_shapes=[pltpu.VMEM((G_KV, HG * TQ, 1), F32),
                        pltpu.VMEM((G_KV, HG * TQ, 1), F32),
                        pltpu.VMEM((G_KV, HG * TQ, HEAD_B), F32)],
    )
    return pl.pallas_call(
        functools.partial(_nsa_selwin_sample_kernel, PGS=PGS, PS=PS, HG=HG, TQ=TQ, pos0=pos0,
                          n_new=n_new, n_win=n_win),
        grid_spec=grid_spec,
        out_shape=jax.ShapeDtypeStruct((B, TQ, CB), F32),
        compiler_params=_cparams(("parallel", "arbitrary")),
        name="nsa_selwin_sample",
    )(table, slopes, proj, selm, selst, *([pool5] * PGS), new_rows, cwin, o_cmp, proj, proj, proj, proj)


def _rwkv_layer(h, x_prev, s0, i, W, B, T):
    N, D = h.shape
    g = W["norm_g"][i]
    rkvg = rwkv_in(h, x_prev, g, W["mu_a"][i], W["w_in_a"], i, T)
    CA = rkvg.shape[-1]
    lw, a = rwkv_lora(h, x_prev, g, W["mu_a"][i], W["w_lora_w1"][i], W["w_lora_w2"][i], W["a_lora1"][i],
                      W["a_lora2"][i], W["w0_a"][i], W["a0_a"][i], T)
    pvec = jnp.stack([W["k_k"][i], W["k_a"][i], W["r_k"][i].reshape(CA), W["ln_x_w"][i], W["ln_x_b"][i]])
    o, s_fin = rwkv_scan(rkvg.reshape(4, B, T, CA), lw.reshape(B, T, CA), a.reshape(B, T, CA), pvec, s0)
    last = rmsnorm(h.reshape(B, T, D)[:, -1], g)
    return o.reshape(N, CA), s_fin, last


def _nsa_layer(h, jb, shared, W, slopes, B, T, norm_g):
    N, D = h.shape
    CB = W["w_out_b"].shape[1]
    proj3 = norm_mm(h, norm_g, W["w_in_b"], (jb,)).reshape(B, T, -1)
    if shared["past"] is None:
        o_cmp, selm = nsa_cmp(proj3, shared["kvc"], slopes, tq=min(T, 128), nc=shared["nc"],
                              nsb=shared["nsb"], pos0=0)
        o = nsa_selwin_prompt(proj3, shared["rows"], selm, o_cmp, slopes, tq=min(T, 128))
    else:
        TQ = SUBLANES
        projp = jnp.pad(proj3, ((0, 0), (0, TQ - T), (0, 0)))
        o_cmp, selm = nsa_cmp(projp, shared["kvc"], slopes, tq=TQ, nc=shared["nc"],
                              nsb=shared["nsb"], pos0=shared["pos0"])
        pool, table, cwin = shared["past"]
        o = nsa_selwin_sample(projp, selm, o_cmp, pool, table, shared["new_rows"], cwin, slopes,
                              pos0=shared["pos0"], n_new=T)[:, :T]
    return o.reshape(N, CB)


def _trunk(x, p, pos0, wkv0, shift0, past, W, slopes):
    B, T, D = x.shape
    N = B * T
    depth = p.shape[0]
    n_a = W["w_in_a"].shape[0]
    GW = G_KV * HEAD_B
    h = x.reshape(N, D)
    wkv_new, shift_new = [], []
    shared, kv_rows, win_state = None, None, None
    for i in range(depth):
        if i < n_a:
            o, s_fin, last = _rwkv_layer(h, shift0[i], wkv0[i], i, W, B, T)
            wkv_new.append(s_fin)
            shift_new.append(last)
            h = out_ple(o, h, p[i].reshape(N, -1), W["w_out_a"], i, W["w_ple"], W["w_ple_gate"], i)
        else:
            o = _nsa_layer(h, i - n_a, shared, W, slopes, B, T, W["norm_g"][i])
            h = out_ple(o, h, p[i].reshape(N, -1), W["w_out_b"], i - n_a, W["w_ple"], W["w_ple_gate"], i)
        if i == n_a - 1:
            rows = norm_mm(h, W["kv_norm_g"], W["w_kv"]).reshape(B, T, 6 * GW)
            kv_rows = rows[:, :, :4 * GW].reshape(B, T, 4, G_KV, HEAD_B)
            win_new = rows[:, :, 4 * GW:].reshape(B, T, 2, G_KV, HEAD_B)
            if past is None:
                PS = 128
                pool = rows.reshape(B * T // PS, PS // S_CMP, S_CMP, 6 * G_KV, HEAD_B)
                table = jnp.arange(B * T // PS, dtype=jnp.int32).reshape(B, T // PS)
                t_all = T
                win_all = win_new
                shared = {"past": None, "rows": rows}
            else:
                pool, table, cwin = past
                PS = pool.shape[1] * S_CMP
                t_all = pos0 + T
                win_all = jnp.concatenate([cwin.reshape(B, -1, 2, G_KV, HEAD_B), win_new], axis=1)
                NN = LANES
                shared = {"past": past, "new_rows": jnp.pad(rows, ((0, 0), (0, NN - T), (0, 0)))}
            win_state = win_all[:, win_all.shape[1] - min(WINDOW, pos0 + T):]
            nc = (t_all - L_CMP) // S_CMP + 1
            assert nc < table.shape[1] * PS // S_CMP
            kvc = compress_kv(pool, table, W["pe_cmp"], W["w_cmp1"], W["w_cmp2"])
            shared.update(kvc=kvc, nc=nc, nsb=max(-(-t_all // L_SEL), TOPK_SEL), pos0=pos0)
    y = rmsnorm(h, W["final_norm_g"]).reshape(B, T, D)
    return y, jnp.stack(wkv_new), jnp.stack(shift_new), kv_rows, win_state


def kernel(x_prompt, x_sample, state_wkv, state_shift, cache_kv, cache_win_kv, page_table, p_prompt, p_sample, norm_g, mu_a, w_in_a, w_lora_w1, w_lora_w2, w0_a, a_lora1, a_lora2, a0_a, k_k, k_a, r_k, ln_x_w, ln_x_b, w_out_a, w_in_b, w_out_b, kv_norm_g, w_kv, pe_cmp, w_cmp1, w_cmp2, w_ple, w_ple_gate, final_norm_g):
    bf = lambda w: w.astype(BF16)
    CA = w_out_a.shape[1]
    W = dict(norm_g=norm_g, mu_a=mu_a, w_in_a=bf(w_in_a), w_lora_w1=bf(w_lora_w1), w_lora_w2=bf(w_lora_w2),
             w0_a=w0_a, a_lora1=bf(a_lora1), a_lora2=bf(a_lora2), a0_a=a0_a, k_k=k_k, k_a=k_a,
             r_k=r_k, ln_x_w=ln_x_w, ln_x_b=ln_x_b,
             w_out_a=bf(w_out_a), w_in_b=bf(w_in_b), w_out_b=bf(w_out_b), kv_norm_g=kv_norm_g, w_kv=bf(w_kv),
             pe_cmp=pe_cmp, w_cmp1=bf(w_cmp1), w_cmp2=bf(w_cmp2), w_ple=bf(w_ple), w_ple_gate=bf(w_ple_gate),
             final_norm_g=final_norm_g)
    HB = w_out_b.shape[1] // HEAD_B
    slopes = 2.0 ** (-8.0 * jnp.arange(1, HB + 1, dtype=F32) / HB)
    bp = x_prompt.shape[0]
    n_a = w_in_a.shape[0]
    D = x_prompt.shape[-1]
    wkv0 = jnp.zeros((n_a, bp, CA // HEAD_A, HEAD_A, HEAD_A), F32)
    shift0 = jnp.zeros((n_a, bp, D), F32)
    y_p, wkv_p, shift_p, kv_p, win_p = _trunk(x_prompt, p_prompt, 0, wkv0, shift0, None, W, slopes)
    db, n_pages = page_table.shape
    NP, PS = cache_kv.shape[:2]
    pool5 = cache_kv.reshape(NP, PS // S_CMP, S_CMP, -1, HEAD_B)
    past = (pool5, page_table, cache_win_kv.reshape(db, cache_win_kv.shape[1], -1))
    y_s, wkv_s, shift_s, kv_s, win_s = _trunk(x_sample, p_sample, n_pages * PS, state_wkv, state_shift, past, W, slopes)
    return (y_p, y_s, wkv_p, shift_p, kv_p, win_p, wkv_s, shift_s, kv_s, win_s)
```

```python
import functools
import math

import jax
import jax.numpy as jnp
from jax import lax
from jax.experimental import pallas as pl
from jax.experimental.pallas import tpu as pltpu

F32 = jnp.float32
BF16 = jnp.bfloat16

HEAD_A = 64
GN_EPS = 64e-5
HEAD_B = 128
G_KV = 4
L_CMP = 32
S_CMP = 16
L_SEL = 64
TOPK_SEL = 16
WINDOW = 512
RMS_EPS = 1e-6
NEG = -1e30
FORCE_BONUS = 1e4

LANES = 128
SUBLANES = 8
VMEM_LIMIT = 56 * 1024 * 1024

SCAN_NH = 2
SCAN_C = 64
SCAN_GP = 16

NT_DIMS = (((1,), (1,)), ((), ()))
TN_DIMS = (((0,), (0,)), ((), ()))


def _cparams(sem):
    return pltpu.CompilerParams(dimension_semantics=sem, vmem_limit_bytes=VMEM_LIMIT)


def _bdot(a, b):
    return jnp.dot(a.astype(BF16), b.astype(BF16), preferred_element_type=F32)


def _bdot_nt(a, b):
    return lax.dot_general(a.astype(BF16), b.astype(BF16), NT_DIMS, preferred_element_type=F32)


def _bdot_tn(a, b):
    return lax.dot_general(a.astype(BF16), b.astype(BF16), TN_DIMS, preferred_element_type=F32)


def _rms_kernel(x_ref, g_ref, o_ref):
    x = x_ref[...]
    ms = jnp.mean(x * x, axis=-1, keepdims=True)
    o_ref[...] = x * lax.rsqrt(ms + RMS_EPS) * g_ref[...]


def rmsnorm(x, g):
    M, D = x.shape
    tm = min(M, 256)
    return pl.pallas_call(
        _rms_kernel,
        grid=(pl.cdiv(M, tm),),
        in_specs=[pl.BlockSpec((tm, D), lambda i: (i, 0)),
                  pl.BlockSpec((1, D), lambda i: (0, 0))],
        out_specs=pl.BlockSpec((tm, D), lambda i: (i, 0)),
        out_shape=jax.ShapeDtypeStruct((M, D), F32),
        compiler_params=_cparams(("parallel",)),
        name="rmsnorm",
    )(x, g.reshape(1, D))


def _mm_kernel(x_ref, w_ref, o_ref):
    o_ref[...] = jnp.dot(x_ref[...].astype(BF16), w_ref[...], preferred_element_type=F32)


def mm(x, w, widx=()):
    M, K = x.shape
    N = w.shape[-1]
    assert w.shape[-2] == K and len(widx) == w.ndim - 2
    tm = min(M, 1024 if K <= 2048 else 512)
    tn = N if N <= 512 else 512
    nlead = len(widx)
    w_spec = pl.BlockSpec((None,) * nlead + (K, tn), lambda i, j: tuple(widx) + (0, j))
    return pl.pallas_call(
        _mm_kernel,
        grid=(pl.cdiv(M, tm), pl.cdiv(N, tn)),
        in_specs=[pl.BlockSpec((tm, K), lambda i, j: (i, 0)), w_spec],
        out_specs=pl.BlockSpec((tm, tn), lambda i, j: (i, j)),
        out_shape=jax.ShapeDtypeStruct((M, N), F32),
        compiler_params=_cparams(("parallel", "parallel")),
        name="mm",
    )(x, w)


def _norm_rows(x, g):
    return x * lax.rsqrt(jnp.mean(x * x, axis=-1, keepdims=True) + RMS_EPS) * g


def _sigmoid(x):
    return 0.5 + 0.5 * jnp.tanh(0.5 * x)


def _norm_and_shift(h_ref, hprev_ref, xprev_ref, g_ref, i, tm, T):
    g = g_ref[...]
    hn = _norm_rows(h_ref[...], g)
    prev_row = _norm_rows(hprev_ref[SUBLANES - 1:SUBLANES, :], g)
    row = lax.broadcasted_iota(jnp.int32, (tm, 1), 0)
    xs = jnp.where(row == 0, prev_row, pltpu.roll(hn, 1, axis=0))
    if T >= tm:
        assert T % tm == 0
        start = (i * tm) % T == 0
        xs = jnp.where((row == 0) & start, xprev_ref[pl.ds((i * tm) // T, 1), :], xs)
    else:
        assert tm % T == 0
        for bb in range(tm // T):
            xs = jnp.where(row == bb * T, xprev_ref[pl.ds(i * (tm // T) + bb, 1), :], xs)
    return hn, xs


def _rwkv_in_kernel(h_ref, hprev_ref, xprev_ref, g_ref, mu_ref, w_ref, o_ref, xs_scr, *, tm, T):
    i, j, n = pl.program_id(0), pl.program_id(1), pl.program_id(2)

    @pl.when(n == 0)
    def _():
        hn, xs = _norm_and_shift(h_ref, hprev_ref, xprev_ref, g_ref, i, tm, T)
        xs_scr[...] = (hn + (xs - hn) * mu_ref[pl.ds(j, 1), :]).astype(BF16)

    o_ref[...] = jnp.dot(xs_scr[...], w_ref[...], preferred_element_type=F32)


def _shift_specs(tm, D, nb, ngrid):
    z = (0,) * (ngrid - 1)
    wrap = lambda f: (lambda i, *_: f(i))
    return [pl.BlockSpec((tm, D), wrap(lambda i: (i, 0))),
            pl.BlockSpec((SUBLANES, D), wrap(lambda i: (jnp.maximum(i * (tm // SUBLANES) - 1, 0), 0))),
            pl.BlockSpec((nb, D), wrap(lambda i: (0, 0))),
            pl.BlockSpec((1, D), wrap(lambda i: (0, 0))),
            pl.BlockSpec((6, D), wrap(lambda i: (0, 0)))]


def rwkv_in(h, xprev, g, mu, w, layer, T):
    N, D = h.shape
    C = w.shape[-1]
    tm = min(N, 1024)
    tn = 512
    return pl.pallas_call(
        functools.partial(_rwkv_in_kernel, tm=tm, T=T),
        grid=(N // tm, 4, C // tn),
        in_specs=_shift_specs(tm, D, xprev.shape[0], 3) + [
            pl.BlockSpec((None, None, D, tn), lambda i, j, n: (layer, j, 0, n))],
        out_specs=pl.BlockSpec((None, tm, tn), lambda i, j, n: (j, i, n)),
        out_shape=jax.ShapeDtypeStruct((4, N, C), F32),
        scratch_shapes=[pltpu.VMEM((tm, D), BF16)],
        compiler_params=_cparams(("parallel", "arbitrary", "arbitrary")),
        name="rwkv_in",
    )(h, h, xprev, g.reshape(1, D), mu, w)


def _rwkv_lora_kernel(h_ref, hprev_ref, xprev_ref, g_ref, mu_ref, lw1_ref, lw2_ref, la1_ref, la2_ref,
                      w0_ref, a0_ref, lw_ref, a_ref, *, tm, T):
    hn, xs = _norm_and_shift(h_ref, hprev_ref, xprev_ref, g_ref, pl.program_id(0), tm, T)
    dx = xs - hn
    x4 = (hn + dx * mu_ref[4:5, :]).astype(BF16)
    x5 = (hn + dx * mu_ref[5:6, :]).astype(BF16)
    t4 = jnp.tanh(jnp.dot(x4, lw1_ref[...], preferred_element_type=F32)).astype(BF16)
    y = -(w0_ref[...] + jnp.dot(t4, lw2_ref[...], preferred_element_type=F32))
    softplus = jnp.maximum(y, 0.0) + jnp.log(1.0 + jnp.exp(-jnp.abs(y)))
    lw_ref[...] = -jnp.exp(-softplus - 0.5)
    t5 = jnp.dot(x5, la1_ref[...], preferred_element_type=F32).astype(BF16)
    a_ref[...] = _sigmoid(a0_ref[...] + jnp.dot(t5, la2_ref[...], preferred_element_type=F32))


def rwkv_lora(h, xprev, g, mu, lw1, lw2, la1, la2, w0, a0, T):
    N, D = h.shape
    R, C = lw2.shape
    tm = min(N, 256)
    full = lambda shape: pl.BlockSpec(shape, lambda i: (0,) * len(shape))
    o_spec = pl.BlockSpec((tm, C), lambda i: (i, 0))
    return pl.pallas_call(
        functools.partial(_rwkv_lora_kernel, tm=tm, T=T),
        grid=(N // tm,),
        in_specs=_shift_specs(tm, D, xprev.shape[0], 1) + [
            full((D, R)), full((R, C)), full((D, R)), full((R, C)), full((1, C)), full((1, C))],
        out_specs=[o_spec, o_spec],
        out_shape=[jax.ShapeDtypeStruct((N, C), F32)] * 2,
        compiler_params=_cparams(("parallel",)),
        name="rwkv_lora",
    )(h, h, xprev, g.reshape(1, D), mu, lw1, lw2, la1, la2, w0.reshape(1, C), a0.reshape(1, C))


def _norm_mm_kernel(h_ref, g_ref, w_ref, o_ref, xs_scr):
    @pl.when(pl.program_id(1) == 0)
    def _():
        rc = min(h_ref.shape[0], 256)

        def norm_chunk(c, carry):
            rows = pl.ds(pl.multiple_of(c * rc, rc), rc)
            xs_scr[rows, :] = _norm_rows(h_ref[rows, :], g_ref[...]).astype(BF16)
            return carry

        lax.fori_loop(0, h_ref.shape[0] // rc, norm_chunk, 0)

    o_ref[...] = jnp.dot(xs_scr[...], w_ref[...], preferred_element_type=F32)


def norm_mm(h, g, w, widx=()):
    N, D = h.shape
    NO = w.shape[-1]
    tm = min(N, 1024)
    tn = 1024
    nlead = len(widx)
    return pl.pallas_call(
        _norm_mm_kernel,
        grid=(N // tm, pl.cdiv(NO, tn)),
        in_specs=[pl.BlockSpec((tm, D), lambda i, n: (i, 0)),
                  pl.BlockSpec((1, D), lambda i, n: (0, 0)),
                  pl.BlockSpec((None,) * nlead + (D, tn), lambda i, n: tuple(widx) + (0, n))],
        out_specs=pl.BlockSpec((tm, tn), lambda i, n: (i, n)),
        out_shape=jax.ShapeDtypeStruct((N, NO), F32),
        scratch_shapes=[pltpu.VMEM((tm, D), BF16)],
        compiler_params=_cparams(("parallel", "arbitrary")),
        name="norm_mm",
    )(h, g.reshape(1, D), w)


def _out_ple_kernel(x_ref, h_ref, p_ref, wo_ref, wp_ref, wg_ref, o_ref, h1_scr, h1b_scr, *, NT):
    n = pl.program_id(1)

    @pl.when(n < NT)
    def _():
        h1 = h_ref[...] + jnp.dot(x_ref[...].astype(BF16), wo_ref[...], preferred_element_type=F32)
        h1_scr[n] = h1
        h1b_scr[n] = h1.astype(BF16)

    @pl.when(n >= NT)
    def _():
        h1b = jnp.concatenate([h1b_scr[t] for t in range(NT)], axis=1)
        gate = jnp.dot(h1b, wg_ref[...], preferred_element_type=F32)
        ple = jnp.dot(p_ref[...].astype(BF16), wp_ref[...], preferred_element_type=F32)
        o_ref[...] = h1_scr[n - NT] + ple * _sigmoid(gate)


def out_ple(x, h, p, w_out, oidx, w_ple, w_gate, layer):
    N, C = x.shape
    D = h.shape[1]
    DP = p.shape[1]
    tm = min(N, 512)
    tn = 512
    NT = D // tn
    lo = lambda n: jnp.minimum(n, NT - 1)
    hi = lambda n: jnp.maximum(n - NT, 0)
    return pl.pallas_call(
        functools.partial(_out_ple_kernel, NT=NT),
        grid=(N // tm, 2 * NT),
        in_specs=[pl.BlockSpec((tm, C), lambda i, n: (i, 0)),
                  pl.BlockSpec((tm, tn), lambda i, n: (i, lo(n))),
                  pl.BlockSpec((tm, DP), lambda i, n: (i, 0)),
                  pl.BlockSpec((None, C, tn), lambda i, n: (oidx, 0, lo(n))),
                  pl.BlockSpec((None, DP, tn), lambda i, n: (layer, 0, hi(n))),
                  pl.BlockSpec((None, D, tn), lambda i, n: (layer, 0, hi(n)))],
        out_specs=pl.BlockSpec((tm, tn), lambda i, n: (i, hi(n))),
        out_shape=jax.ShapeDtypeStruct((N, D), F32),
        scratch_shapes=[pltpu.VMEM((NT, tm, tn), F32), pltpu.VMEM((NT, tm, tn), BF16)],
        compiler_params=_cparams(("parallel", "arbitrary")),
        name="out_ple",
    )(x, h, p, w_out, w_ple, w_gate)


def _scan_kernel(r_ref, k_ref, v_ref, zg_ref, lw_ref, a_ref, pv_ref, s0_ref, o_ref, sfin_ref, s_scr, *, C, NH, GP):
    L = NH * HEAD_A
    NC = NH * C
    ci = pl.program_id(2)

    @pl.when(ci == 0)
    def _():
        s_scr[...] = s0_ref[0]

    row_c = lax.broadcasted_iota(jnp.int32, (C, NC), 0)
    col_s = lax.broadcasted_iota(jnp.int32, (C, NC), 1) % C
    tri_strict = col_s < row_c
    tri_incl = (lax.broadcasted_iota(jnp.int32, (C, 2 * NC), 1) % C
                <= lax.broadcasted_iota(jnp.int32, (C, 2 * NC), 0))
    st_mask = (lax.broadcasted_iota(jnp.int32, (NC, L), 0) // C
               == lax.broadcasted_iota(jnp.int32, (NC, L), 1) // HEAD_A)
    bd_mask = (lax.broadcasted_iota(jnp.int32, (NC, NC), 0) // C
               == lax.broadcasted_iota(jnp.int32, (NC, NC), 1) // C)
    head_mask = (lax.broadcasted_iota(jnp.int32, (L, L), 0) // HEAD_A
                 == lax.broadcasted_iota(jnp.int32, (L, L), 1) // HEAD_A)

    def st(x):
        return jnp.where(st_mask, jnp.concatenate([x] * NH, axis=0), 0.0)

    def bd(w):
        return jnp.where(bd_mask, jnp.concatenate([w] * NH, axis=0), 0.0)

    n_double = int(math.log2(C))
    each = lambda f, *cols: [f(*xs) for xs in zip(*cols)]
    sls = [slice(gp * L, (gp + 1) * L) for gp in range(GP)]
    head_of_lane = lax.broadcasted_iota(jnp.int32, (1, L), 1) // HEAD_A

    def hsum(x):
        out = None
        for hh in range(NH):
            sh = jnp.sum(jnp.where(head_of_lane == hh, x, 0.0), axis=-1, keepdims=True)
            out = sh if out is None else jnp.where(head_of_lane == hh, sh, out)
        return out

    k_k, k_a, r_k, ln_w, ln_b = ([pv_ref[n:n + 1, sl] for sl in sls] for n in range(5))
    lw = [lw_ref[0, :, sl] for sl in sls]
    a_sig = [a_ref[0, :, sl] for sl in sls]
    r = [r_ref[0, :, sl] for sl in sls]
    k_raw = [k_ref[0, :, sl] for sl in sls]
    v = [v_ref[0, :, sl] for sl in sls]
    kk = each(lambda x, w: x * w, k_raw, k_k)
    kk = each(lambda x: x / jnp.maximum(jnp.sqrt(hsum(x * x)), 1e-12), kk)
    k = each(lambda x, a, w: x * (1.0 + (a - 1.0) * w), k_raw, a_sig, k_a)
    row_t = lax.broadcasted_iota(jnp.int32, (C, 1), 0)
    cum = lw
    for sh in (1 << e for e in range(n_double)):
        cum = each(lambda z: z + jnp.where(row_t >= sh, pltpu.roll(z, sh, axis=0), 0.0), cum)
    p_incl = each(jnp.exp, cum)
    p_inv = each(lambda z: jnp.exp(-z), cum)
    at = each(lambda x, c, w: -x * jnp.exp(c - w), kk, cum, lw)
    rt = each(lambda x, p: x * p, r, p_incl)
    bt = each(lambda x, a, p: x * a * p, kk, a_sig, p_inv)
    kt = each(lambda x, p: x * p, k, p_inv)
    S = [s_scr[gp] for gp in range(GP)]
    ar = each(lambda x, y: jnp.concatenate([x, y], axis=0), at, rt)
    bk_st = each(lambda x, y: jnp.concatenate([st(x), st(y)], axis=0), bt, kt)
    Gm = each(_bdot_nt, ar, bk_st)
    w_ab = [jnp.where(tri_strict, g[:C, :NC], 0.0) for g in Gm]
    tm = w_ab
    pw = each(lambda w: _bdot(w, bd(w)), w_ab)
    LH = each(_bdot_nt, ar, S)
    v_st = each(st, v)
    x = [lh[:C] + _bdot(jnp.where(tri_strict, g[:C, NC:], 0.0), vs) for lh, g, vs in zip(LH, Gm, v_st)]
    for it in range(1, n_double):
        if it < n_double - 1:
            both = each(lambda t, p: _bdot(p, jnp.concatenate([bd(t), bd(p)], axis=1)), tm, pw)
            tm = each(lambda t, p, b2: t + p + b2[:, :NC], tm, pw, both)
            pw = [b2[:, NC:] for b2 in both]
        else:
            tm = each(lambda t, p: t + p + _bdot(p, bd(t)), tm, pw)
    u = each(lambda xx, t: xx + _bdot(t, st(xx)), x, tm)
    o = [lh[C:] + _bdot(jnp.where(tri_incl, g[C:], 0.0), jnp.concatenate([st(uu), vs], axis=0))
         for lh, g, uu, vs in zip(LH, Gm, u, v_st)]
    inv_n = 1.0 / HEAD_A
    dev = each(lambda x: x - hsum(x) * inv_n, o)
    gn = each(lambda d, w, b_: d * lax.rsqrt(hsum(d * d) * inv_n + GN_EPS) * w + b_, dev, ln_w, ln_b)
    bonus = each(lambda rr, kx, w, vv: hsum(rr * kx * w) * vv, r, k, r_k, v)
    for sl, y, bo in zip(sls, gn, bonus):
        zg = zg_ref[0, :, sl]
        o_ref[0, :, sl] = ((y + bo) * (zg * _sigmoid(zg))).astype(o_ref.dtype)
    ds = [_bdot_tn(jnp.concatenate([uu, vv], axis=0), jnp.concatenate([b_, k_], axis=0))
          for uu, vv, b_, k_ in zip(u, v, bt, kt)]
    for gp in range(GP):
        s_scr[gp] = (S[gp] + jnp.where(head_mask, ds[gp], 0.0)) * p_incl[gp][C - 1:C, :]

    @pl.when(ci == pl.num_programs(2) - 1)
    def _():
        sfin_ref[0] = s_scr[...]


def rwkv_scan(rkvg, lw, a, pvec, s0):
    _, B, T, CA = rkvg.shape
    H = CA // HEAD_A
    NH, C, GP = SCAN_NH, SCAN_C, SCAN_GP
    assert NH * C == LANES and H % (NH * GP) == 0
    L = NH * HEAD_A
    NG = H // NH
    Tp = -(-T // C) * C
    if Tp != T:
        rkvg = jnp.pad(rkvg, ((0, 0), (0, 0), (0, Tp - T), (0, 0)))
        lw, a = (jnp.pad(z, ((0, 0), (0, Tp - T), (0, 0))) for z in (lw, a))
    eye = jnp.eye(NH, dtype=F32)
    s0_bd = (s0.reshape(B, NG, NH, HEAD_A, 1, HEAD_A) * eye[None, None, :, None, :, None]).reshape(B, NG, L, L)
    seq_spec = pl.BlockSpec((1, C, GP * L), lambda bi, gi, ci: (bi, ci, gi))
    proj_spec = lambda j: pl.BlockSpec((None, 1, C, GP * L), lambda bi, gi, ci: (j, bi, ci, gi))
    st_spec = pl.BlockSpec((1, GP, L, L), lambda bi, gi, ci: (bi, gi, 0, 0))
    o, sfin = pl.pallas_call(
        functools.partial(_scan_kernel, C=C, NH=NH, GP=GP),
        grid=(B, NG // GP, Tp // C),
        in_specs=[proj_spec(j) for j in range(4)] + [seq_spec, seq_spec,
                  pl.BlockSpec((5, GP * L), lambda bi, gi, ci: (0, gi)), st_spec],
        out_specs=[seq_spec, st_spec],
        out_shape=[jax.ShapeDtypeStruct((B, Tp, CA), BF16), jax.ShapeDtypeStruct((B, NG, L, L), F32)],
        scratch_shapes=[pltpu.VMEM((GP, L, L), F32)],
        compiler_params=_cparams(("parallel", "parallel", "arbitrary")),
        name="rwkv_scan",
    )(rkvg, rkvg, rkvg, rkvg, lw, a, pvec, s0_bd)
    sf = sfin.reshape(B, NG, NH, HEAD_A, NH, HEAD_A)
    s_fin = jnp.stack([sf[:, :, h, :, h, :] for h in range(NH)], axis=2).reshape(B, H, HEAD_A, HEAD_A)
    return o[:, :T], s_fin


def _gelu_tanh(x):
    c = math.sqrt(2.0 / math.pi)
    return 0.5 * x * (1.0 + jnp.tanh(c * (x + 0.044715 * (x * x * x))))


def _compress_kernel(pt_ref, *refs, PGS):
    del pt_ref
    page_refs = refs[:PGS]
    next_ref, pe_ref, w1_ref, w2_ref, out_ref = refs[PGS:]
    CPP = page_refs[0].shape[1]
    NCH = PGS * CPP
    CG = 2 * G_KV
    M = (NCH + 1) * CG

    def rows_of(l, hf):
        pe = pe_ref[hf, l]
        parts = [(page_refs[i][0, :, l] + pe[None]).reshape(CPP * CG, HEAD_B) for i in range(PGS)]
        parts.append(next_ref[0, 0, l] + pe)
        return jnp.concatenate(parts, axis=0)

    top = jnp.zeros((M, 2 * HEAD_B), F32)
    bot = jnp.zeros((M, 2 * HEAD_B), F32)
    for l in range(0, S_CMP, 2):
        wrows = pl.ds(l * HEAD_B, 2 * HEAD_B)
        xt = jnp.concatenate([rows_of(l, 0), rows_of(l + 1, 0)], axis=1).astype(BF16)
        top = top + jnp.dot(xt, w1_ref[0, wrows, :], preferred_element_type=F32)
        xb = jnp.concatenate([rows_of(l, 1), rows_of(l + 1, 1)], axis=1).astype(BF16)
        bot = bot + jnp.dot(xb, w1_ref[1, wrows, :], preferred_element_type=F32)
    is_k = (lax.broadcasted_iota(jnp.int32, (M, 1), 0) % CG) < G_KV
    pick = lambda z, n: jnp.where(is_k[:n], z[:n, :HEAD_B], z[:n, HEAD_B:])
    hcur = pick(top, NCH * CG) + pick(bot, M)[CG:]
    o2 = jnp.dot(_gelu_tanh(hcur).astype(BF16), w2_ref[...], preferred_element_type=F32)
    out_ref[0] = pick(o2, NCH * CG).reshape(NCH, CG, HEAD_B)


def compress_kv(pool5, table, pe_cmp, w1, w2):
    NP, CPP = pool5.shape[:2]
    B, n_pages = table.shape
    PGS = max(d for d in (8, 4, 2, 1) if n_pages % d == 0)
    NCH = PGS * CPP
    CG = 2 * G_KV
    half = S_CMP * HEAD_B
    pe_r = jnp.repeat(pe_cmp.reshape(2, 2, S_CMP, HEAD_B).transpose(1, 2, 0, 3), G_KV, axis=2)
    w1_r = w1.reshape(2, 2, half, HEAD_B).transpose(1, 2, 0, 3).reshape(2, half, 2 * HEAD_B)
    w2_r = jnp.concatenate([w2[0], w2[1]], axis=1)

    def page_map(i):
        return lambda b, s, pt: (pt[b, s * PGS + i], 0, 0, 0, 0)

    def next_map(b, s, pt):
        return (pt[b, jnp.minimum((s + 1) * PGS, n_pages - 1)], 0, 0, 0, 0)

    const = lambda n: (lambda b, s, pt: (0,) * n)
    grid_spec = pltpu.PrefetchScalarGridSpec(
        num_scalar_prefetch=1,
        grid=(B, n_pages // PGS),
        in_specs=[pl.BlockSpec((1, CPP, S_CMP, CG, HEAD_B), page_map(i)) for i in range(PGS)] + [
            pl.BlockSpec((1, 1, S_CMP, CG, HEAD_B), next_map),
            pl.BlockSpec((2, S_CMP, CG, HEAD_B), const(4)),
            pl.BlockSpec((2, half, 2 * HEAD_B), const(3)),
            pl.BlockSpec((HEAD_B, 2 * HEAD_B), const(2)),
        ],
        out_specs=pl.BlockSpec((1, NCH, CG, HEAD_B), lambda b, s, pt: (b, s, 0, 0)),
    )
    return pl.pallas_call(
        functools.partial(_compress_kernel, PGS=PGS),
        grid_spec=grid_spec,
        out_shape=jax.ShapeDtypeStruct((B, n_pages * CPP, CG, HEAD_B), F32),
        compiler_params=_cparams(("parallel", "arbitrary")),
        name="compress_kv",
    )(table, *([pool5] * PGS), pool5, pe_r, w1_r, w2_r)


def _stack_heads(q, HG):
    return jnp.concatenate([q[:, h * HEAD_B:(h + 1) * HEAD_B] for h in range(HG)], axis=0)


def _masked_softmax_rows(s, mask):
    s = jnp.where(mask, s, NEG)
    m = jnp.max(s, axis=-1, keepdims=True)
    e = jnp.where(mask, jnp.exp(s - m), 0.0)
    l = jnp.sum(e, axis=-1, keepdims=True)
    return e / jnp.where(l > 0.0, l, 1.0)


def _attend_stacked(s, dist, mask, v, slopes_ref, g, HG, tq):
    ps = []
    psum = jnp.zeros(dist.shape, F32)
    for h in range(HG):
        p = _masked_softmax_rows(s[h * tq:(h + 1) * tq] - slopes_ref[g * HG + h] * dist, mask)
        psum = psum + p
        ps.append(p.astype(BF16))
    o = jnp.dot(jnp.concatenate(ps, axis=0), v.astype(BF16), preferred_element_type=F32)
    return o, psum


def _unstack_store(o_ref, o, HG, tq):
    for h in range(HG):
        o_ref[0, :, h * HEAD_B:(h + 1) * HEAD_B] = o[h * tq:(h + 1) * tq]


def _nsa_cmp_kernel(slopes_ref, q_ref, kc_ref, vc_ref, o_ref, selm_ref, *, tq, HG, nc, nsb, pos0):
    g = pl.program_id(1)
    qt = pl.program_id(2)
    NCp = kc_ref.shape[1]
    NSBp = selm_ref.shape[3]
    q_st = _stack_heads(q_ref[0] * (HEAD_B ** -0.5), HG)
    s = _bdot_nt(q_st, kc_ref[0])
    qpos = pos0 + qt * tq + lax.broadcasted_iota(jnp.int32, (tq, 1), 0)
    cidx = lax.broadcasted_iota(jnp.int32, (1, NCp), 1)
    cend = S_CMP * cidx + (L_CMP - 1)
    mask = (cend <= qpos) & (cidx < nc)
    dist = (qpos - cend).astype(F32)
    o, imp_c = _attend_stacked(s, dist, mask, vc_ref[0], slopes_ref, g, HG, tq)
    _unstack_store(o_ref, o, HG, tq)

    crow = lax.broadcasted_iota(jnp.int32, (NCp, NSBp), 0)
    jcol = lax.broadcasted_iota(jnp.int32, (NCp, NSBp), 1)
    overlap = ((S_CMP * crow < L_SEL * (jcol + 1)) & (S_CMP * crow + L_CMP > L_SEL * jcol)
               & (crow < nc)).astype(F32)
    imp = jnp.dot(imp_c, overlap, precision=lax.Precision.HIGHEST, preferred_element_type=F32)
    lane = lax.broadcasted_iota(jnp.int32, (tq, NSBp), 1)
    cur = jnp.right_shift(qpos, int(math.log2(L_SEL)))
    forced = ((lane == 0) | (lane == cur) | (lane == cur - 1)).astype(F32)
    score = jnp.where(lane <= cur, imp + FORCE_BONUS * forced, NEG)
    score = jnp.where(lane < nsb, score, -3e38)

    cnt = jnp.zeros((tq, NSBp), F32)
    for i in range(nsb):
        col = score[:, i:i + 1]
        beats = (col > score) | ((col == score) & (lane > i))
        cnt = cnt + jnp.where(beats, 1.0, 0.0)
    sel = (cnt < TOPK_SEL) & (score > 0.5 * NEG)
    selm_ref[0, 0] = sel.astype(F32)


def nsa_cmp(proj, kvc, slopes, *, tq, nc, nsb, pos0):
    B, T, _ = proj.shape
    HG = slopes.shape[0] // G_KV
    NCp = kvc.shape[1]
    NSBp = -(-nsb // LANES) * LANES
    gw = HG * HEAD_B
    return pl.pallas_call(
        functools.partial(_nsa_cmp_kernel, tq=tq, HG=HG, nc=nc, nsb=nsb, pos0=pos0),
        grid=(B, G_KV, T // tq),
        in_specs=[pl.BlockSpec(memory_space=pltpu.SMEM),
                  pl.BlockSpec((1, tq, gw), lambda b, g, t: (b, t, g)),
                  pl.BlockSpec((1, NCp, HEAD_B), lambda b, g, t: (b, 0, g)),
                  pl.BlockSpec((1, NCp, HEAD_B), lambda b, g, t: (b, 0, G_KV + g))],
        out_specs=[pl.BlockSpec((1, tq, gw), lambda b, g, t: (b, t, g)),
                   pl.BlockSpec((1, 1, tq, NSBp), lambda b, g, t: (b, g, t, 0))],
        out_shape=[jax.ShapeDtypeStruct((B, T, G_KV * gw), F32),
                   jax.ShapeDtypeStruct((B, G_KV, T, NSBp), F32)],
        compiler_params=_cparams(("parallel", "parallel", "parallel")),
        name="nsa_cmp",
    )(slopes, proj, kvc.reshape(B, NCp, -1), kvc.reshape(B, NCp, -1))


LOG2E = 1.4426950408889634


def _bf16_part(x):
    return x.astype(BF16).astype(F32)


def _alibi_lhs(q, slope_col):
    c = slope_col * LOG2E
    c1 = _bf16_part(c)
    c2 = _bf16_part(c - c1)
    c3 = _bf16_part(c - c1 - c2)
    lane = lax.broadcasted_iota(jnp.int32, q.shape, 1)
    extra = jnp.where((lane == 0) | (lane == 3), c1,
                      jnp.where((lane == 1) | (lane == 4), c2, jnp.where((lane == 2) | (lane == 5), c3, 0.0)))
    return jnp.concatenate([q, extra], axis=1).astype(BF16)


def _alibi_rhs(k, k0):
    pos = k0 + lax.broadcasted_iota(jnp.int32, k.shape, 0)
    lane = lax.broadcasted_iota(jnp.int32, k.shape, 1)
    hi = jnp.bitwise_and(pos, -L_SEL)
    extra = jnp.where(lane < 3, hi, jnp.where(lane < 6, pos - hi, 0)).astype(F32)
    return jnp.concatenate([k, extra], axis=1).astype(BF16)


def _with_ones(v):
    lane = lax.broadcasted_iota(jnp.int32, v.shape, 1)
    return jnp.concatenate([v, jnp.where(lane == 0, 1.0, 0.0)], axis=1).astype(BF16)


def _silu(x):
    return x * _sigmoid(x)


def _nsa_selwin_prompt_kernel(slopes_ref, q_ref, selm_ref, ks_ref, vs_ref, kw_ref, vw_ref,
                              ocmp_ref, zc_ref, zs_ref, zw_ref, gate_ref, o_ref, s_scr, *, tq, HG, T, WS, SEG):
    g = pl.program_id(1)
    qt = pl.program_id(2)
    NSBp = selm_ref.shape[3]
    R = HG * tq
    qpos = qt * tq + lax.broadcasted_iota(jnp.int32, (tq, 1), 0)
    slope_col = jnp.concatenate([jnp.full((tq, 1), slopes_ref[g * HG + h], F32) for h in range(HG)], axis=0)
    tile_heads = lambda x: jnp.concatenate([x] * HG, axis=0)
    q2 = _alibi_lhs(_stack_heads(q_ref[0] * (HEAD_B ** -0.5 * LOG2E), HG), slope_col)

    selm_b = selm_ref[0, 0].astype(BF16)
    nseg = (qt * tq + tq + SEG - 1) // SEG

    def seg_scores(si, m):
        k0 = pl.multiple_of(si * SEG, SEG)
        kcol = k0 + lax.broadcasted_iota(jnp.int32, (NSBp, SEG), 1)
        expand = (jnp.right_shift(kcol, int(math.log2(L_SEL)))
                  == lax.broadcasted_iota(jnp.int32, (NSBp, SEG), 0)).astype(BF16)
        in_blk = jnp.dot(selm_b, expand, preferred_element_type=F32) > 0.5
        kpos = k0 + lax.broadcasted_iota(jnp.int32, (1, SEG), 1)
        mask = tile_heads(in_blk & (kpos <= qpos))
        s = lax.dot_general(q2, _alibi_rhs(ks_ref[0, pl.ds(k0, SEG), :], k0), NT_DIMS, preferred_element_type=F32)
        s = jnp.where(mask, s, NEG)
        s_scr[si] = s
        return jnp.maximum(m, jnp.max(s, axis=-1, keepdims=True))

    m = lax.fori_loop(0, nseg, seg_scores, jnp.full((R, 1), NEG, F32))

    def seg_pv(si, acc):
        k0 = pl.multiple_of(si * SEG, SEG)
        e = jnp.exp2(s_scr[si] - m).astype(BF16)
        return acc + jnp.dot(e, _with_ones(vs_ref[0, pl.ds(k0, SEG), :]), preferred_element_type=F32)

    acc = lax.fori_loop(0, nseg, seg_pv, jnp.zeros((R, 2 * HEAD_B), F32))
    o = acc[:, :HEAD_B] / acc[:, HEAD_B:HEAD_B + 1]

    start = pl.multiple_of(jnp.clip(qt * tq - WINDOW, 0, T - WS), SUBLANES)
    distw = qpos - (start + lax.broadcasted_iota(jnp.int32, (1, WS), 1))
    maskw = tile_heads((distw >= 0) & (distw < WINDOW))
    sw = lax.dot_general(q2, _alibi_rhs(kw_ref[0, pl.ds(start, WS), :], start), NT_DIMS, preferred_element_type=F32)
    sw = jnp.where(maskw, sw, NEG)
    ew = jnp.exp2(sw - jnp.max(sw, axis=-1, keepdims=True)).astype(BF16)
    accw = jnp.dot(ew, _with_ones(vw_ref[0, pl.ds(start, WS), :]), preferred_element_type=F32)
    ow = accw[:, :HEAD_B] / accw[:, HEAD_B:HEAD_B + 1]

    HB = G_KV * HG
    gates = _sigmoid(gate_ref[0])
    lane = lax.broadcasted_iota(jnp.int32, gates.shape, 1)
    gate_col = lambda idx: jnp.sum(jnp.where(lane == idx, gates, 0.0), axis=1, keepdims=True)
    for h in range(HG):
        hs = slice(h * HEAD_B, (h + 1) * HEAD_B)
        rs = slice(h * tq, (h + 1) * tq)
        hd = g * HG + h
        y = (gate_col(hd) * ocmp_ref[0, :, hs] * _silu(zc_ref[0, :, hs])
             + gate_col(HB + hd) * o[rs] * _silu(zs_ref[0, :, hs])
             + gate_col(2 * HB + hd) * ow[rs] * _silu(zw_ref[0, :, hs]))
        o_ref[0, :, hs] = y.astype(o_ref.dtype)


def nsa_selwin_prompt(proj, rows, selm, o_cmp, slopes, *, tq):
    B, T, _ = proj.shape
    HG = slopes.shape[0] // G_KV
    NSBp = selm.shape[3]
    gw = HG * HEAD_B
    CB = G_KV * gw
    WS = min(T, WINDOW + tq)
    SEG = min(T, 512)
    assert T % SEG == 0
    kv_spec = lambda c: pl.BlockSpec((1, T, HEAD_B), lambda b, g, t: (b, 0, c * G_KV + g))
    head_spec = lambda blk: pl.BlockSpec((1, tq, gw), lambda b, g, t: (b, t, blk * G_KV + g))
    return pl.pallas_call(
        functools.partial(_nsa_selwin_prompt_kernel, tq=tq, HG=HG, T=T, WS=WS, SEG=SEG),
        grid=(B, G_KV, T // tq),
        in_specs=[pl.BlockSpec(memory_space=pltpu.SMEM),
                  head_spec(0),
                  pl.BlockSpec((1, 1, tq, NSBp), lambda b, g, t: (b, g, t, 0)),
                  kv_spec(2), kv_spec(3), kv_spec(4), kv_spec(5),
                  head_spec(0), head_spec(1), head_spec(2), head_spec(3),
                  pl.BlockSpec((1, tq, LANES), lambda b, g, t: (b, t, 4 * CB // LANES))],
        out_specs=head_spec(0),
        out_shape=jax.ShapeDtypeStruct((B, T, CB), BF16),
        scratch_shapes=[pltpu.VMEM((T // SEG, HG * tq, SEG), F32)],
        compiler_params=_cparams(("parallel", "parallel", "parallel")),
        name="nsa_selwin_prompt",
    )(slopes, proj, selm, rows, rows, rows, rows, o_cmp, proj, proj, proj, proj)


def _nsa_selwin_sample_kernel(pt_ref, slopes_ref, q_ref, selm_ref, selst_ref, *refs, PGS, PS, HG, TQ, pos0, n_new, n_win):
    del pt_ref
    page_refs = refs[:PGS]
    (new_ref, cwin_ref, ocmp_ref, zc_ref, zs_ref, zw_ref, gate_ref, o_ref,
     m_scr, l_scr, acc_scr) = refs[PGS:]
    st = pl.program_id(1)
    NSBp = selm_ref.shape[3]
    GW = G_KV * HEAD_B
    sel_shift = int(math.log2(L_SEL))

    @pl.when(st == 0)
    def _():
        m_scr[...] = jnp.full(m_scr.shape, NEG, F32)
        l_scr[...] = jnp.zeros(l_scr.shape, F32)
        acc_scr[...] = jnp.zeros(acc_scr.shape, F32)

    qpos = pos0 + lax.broadcasted_iota(jnp.int32, (TQ, 1), 0)
    lane_j = lax.broadcasted_iota(jnp.int32, (TQ, NSBp), 1)
    tile_heads = lambda x: jnp.concatenate([x] * HG, axis=0)

    def sel_col(selm_g, j):
        return jnp.sum(jnp.where(lane_j == j, selm_g, 0.0), axis=1, keepdims=True)

    def online_update(s, mask, v):
        gs = range(G_KV)
        m_old = [m_scr[g] for g in gs]
        m_new = [jnp.maximum(m_old[g], jnp.max(jnp.where(mask[g], s[g], NEG), axis=-1, keepdims=True)) for g in gs]
        e = [jnp.where(mask[g], jnp.exp(s[g] - m_new[g]), 0.0) for g in gs]
        alpha = [jnp.exp(m_old[g] - m_new[g]) for g in gs]
        pv = [_bdot(e[g], v[g]) for g in gs]
        for g in gs:
            l_scr[g] = alpha[g] * l_scr[g] + jnp.sum(e[g], axis=-1, keepdims=True)
            acc_scr[g] = alpha[g] * acc_scr[g] + pv[g]
            m_scr[g] = m_new[g]

    NK = PGS * PS
    kpos = st * NK + lax.broadcasted_iota(jnp.int32, (1, NK), 1)
    expand = (jnp.right_shift(lax.broadcasted_iota(jnp.int32, (LANES, NK), 1), sel_shift)
              == lax.broadcasted_iota(jnp.int32, (LANES, NK), 0)).astype(BF16)
    in_blk_all = jnp.dot(selst_ref[0, 0].astype(BF16), expand, preferred_element_type=F32)
    dist = qpos - kpos
    distf = tile_heads(dist.astype(F32))
    gs = range(G_KV)
    q_st = [_stack_heads(q_ref[0, :, g * HG * HEAD_B:(g + 1) * HG * HEAD_B] * (HEAD_B ** -0.5), HG).astype(BF16)
            for g in gs]
    slope_col = [jnp.concatenate([jnp.full((TQ, 1), slopes_ref[g * HG + h], F32) for h in range(HG)], axis=0)
                 for g in gs]
    page_rows = lambda i, cg: page_refs[i][0, :, :, cg, :].reshape(PS, HEAD_B)
    k = [jnp.concatenate([page_rows(i, g) for i in range(PGS)], axis=0) for g in gs]
    v = [jnp.concatenate([page_rows(i, G_KV + g) for i in range(PGS)], axis=0) for g in gs]
    mask = [tile_heads((in_blk_all[g * TQ:(g + 1) * TQ] > 0.5) & (dist >= 0)) for g in gs]
    s = [_bdot_nt(q_st[g], k[g]) - slope_col[g] * distf for g in gs]
    online_update(s, mask, v)

    @pl.when(st == pl.num_programs(1) - 1)
    def _():
        NN = new_ref.shape[1]
        rnew = lax.broadcasted_iota(jnp.int32, (1, NN), 1)
        kpos_n = pos0 + rnew
        dist_n = qpos - kpos_n
        ok_n = (rnew < n_new) & (dist_n >= 0)
        jn = pos0 >> sel_shift
        kpos_w = pos0 - n_win + lax.broadcasted_iota(jnp.int32, (1, n_win), 1)
        dist_w = qpos - kpos_w
        dist_wall = jnp.concatenate([dist_w, dist_n], axis=1)
        mask_wall = jnp.concatenate([(dist_w >= 0) & (dist_w < WINDOW), ok_n & (dist_n < WINDOW)], axis=1)
        kn = [new_ref[0, :, 2 * GW + g * HEAD_B:2 * GW + (g + 1) * HEAD_B] for g in gs]
        vn = [new_ref[0, :, 3 * GW + g * HEAD_B:3 * GW + (g + 1) * HEAD_B] for g in gs]
        mask_n = [tile_heads((sel_col(selm_ref[0, g], jn) > 0.5) & ok_n) for g in gs]
        dist_nf = tile_heads(dist_n.astype(F32))
        s_n = [_bdot_nt(q_st[g], kn[g]) - slope_col[g] * dist_nf for g in gs]
        online_update(s_n, mask_n, vn)
        kw = [jnp.concatenate([cwin_ref[0, :, g * HEAD_B:(g + 1) * HEAD_B],
                               new_ref[0, :, 4 * GW + g * HEAD_B:4 * GW + (g + 1) * HEAD_B]], axis=0) for g in gs]
        vw = [jnp.concatenate([cwin_ref[0, :, GW + g * HEAD_B:GW + (g + 1) * HEAD_B],
                               new_ref[0, :, 5 * GW + g * HEAD_B:5 * GW + (g + 1) * HEAD_B]], axis=0) for g in gs]
        dist_wf = tile_heads(dist_wall.astype(F32))
        mask_w = tile_heads(mask_wall)
        sw = [_bdot_nt(q_st[g], kw[g]) - slope_col[g] * dist_wf for g in gs]
        pw = [_masked_softmax_rows(sw[g], mask_w) for g in gs]
        ow = [_bdot(pw[g], vw[g]) for g in gs]
        HB = G_KV * HG
        gates = _sigmoid(gate_ref[0])
        for g in gs:
            l = l_scr[g]
            o = acc_scr[g] / jnp.where(l > 0.0, l, 1.0)
            for h in range(HG):
                hd = g * HG + h
                hs = slice(hd * HEAD_B, (hd + 1) * HEAD_B)
                rs = slice(h * TQ, (h + 1) * TQ)
                y = (gates[:, hd:hd + 1] * ocmp_ref[0, :, hs] * _silu(zc_ref[0, :, hs])
                     + gates[:, HB + hd:HB + hd + 1] * o[rs] * _silu(zs_ref[0, :, hs])
                     + gates[:, 2 * HB + hd:2 * HB + hd + 1] * ow[g][rs] * _silu(zw_ref[0, :, hs]))
                o_ref[0, :, hs] = y.astype(o_ref.dtype)


def nsa_selwin_sample(proj, selm, o_cmp, pool5, table, new_rows, cwin, slopes, *, pos0, n_new):
    B, TQ, _ = proj.shape
    HG = slopes.shape[0] // G_KV
    NP, CPP = pool5.shape[:2]
    PS = CPP * S_CMP
    n_pages = table.shape[1]
    NSBp = selm.shape[3]
    PGS = max(d for d in (8, 4, 2, 1) if n_pages % d == 0)
    CB = G_KV * HG * HEAD_B
    GW = G_KV * HEAD_B
    NN = new_rows.shape[1]
    n_win = cwin.shape[1]
    assert pos0 % L_SEL == 0 and n_new <= L_SEL and pos0 == n_pages * PS

    def page_map(i):
        return lambda b, s, pt: (pt[b, s * PGS + i], 0, 0, 1, 0)

    n_steps = n_pages // PGS
    bps = PGS * PS // L_SEL
    assert bps <= LANES
    selst = selm[:, :, :, :n_steps * bps].reshape(B, G_KV, TQ, n_steps, bps).transpose(0, 3, 1, 2, 4)
    selst = jnp.pad(selst.reshape(B, n_steps, G_KV * TQ, bps), ((0, 0), (0, 0), (0, 0), (0, LANES - bps)))

    const = lambda b, s, pt: (b, 0, 0)
    wide = lambda blk: pl.BlockSpec((1, TQ, CB), lambda b, s, pt: (b, 0, blk))
    grid_spec = pltpu.PrefetchScalarGridSpec(
        num_scalar_prefetch=1,
        grid=(B, n_steps),
        in_specs=[pl.BlockSpec(memory_space=pltpu.SMEM),
                  pl.BlockSpec((1, TQ, CB), const),
                  pl.BlockSpec((1, G_KV, TQ, NSBp), lambda b, s, pt: (b, 0, 0, 0)),
                  pl.BlockSpec((1, 1, G_KV * TQ, LANES), lambda b, s, pt: (b, s, 0, 0))]
                 + [pl.BlockSpec((1, CPP, S_CMP, 2 * G_KV, HEAD_B), page_map(i)) for i in range(PGS)]
                 + [pl.BlockSpec((1, NN, 6 * GW), const),
                    pl.BlockSpec((1, n_win, 2 * GW), const),
                    wide(0), wide(1), wide(2), wide(3),
                    pl.BlockSpec((1, TQ, LANES), lambda b, s, pt: (b, 0, 4 * CB // LANES))],
        out_specs=wide(0),
        scratch_shapes=[pltpu.VMEM((G_KV, HG * TQ, 1), F32),
                        pltpu.VMEM((G_KV, HG * TQ, 1), F32),
                        pltpu.VMEM((G_KV, HG * TQ, HEAD_B), F32)],
    )
    return pl.pallas_call(
        functools.partial(_nsa_selwin_sample_kernel, PGS=PGS, PS=PS, HG=HG, TQ=TQ, pos0=pos0,
                          n_new=n_new, n_win=n_win),
        grid_spec=grid_spec,
        out_shape=jax.ShapeDtypeStruct((B, TQ, CB), F32),
        compiler_params=_cparams(("parallel", "arbitrary")),
        name="nsa_selwin_sample",
    )(table, slopes, proj, selm, selst, *([pool5] * PGS), new_rows, cwin, o_cmp, proj, proj, proj, proj)


def _rwkv_layer(h, x_prev, s0, i, W, B, T):
    N, D = h.shape
    g = W["norm_g"][i]
    rkvg = rwkv_in(h, x_prev, g, W["mu_a"][i], W["w_in_a"], i, T)
    CA = rkvg.shape[-1]
    lw, a = rwkv_lora(h, x_prev, g, W["mu_a"][i], W["w_lora_w1"][i], W["w_lora_w2"][i], W["a_lora1"][i],
                      W["a_lora2"][i], W["w0_a"][i], W["a0_a"][i], T)
    pvec = jnp.stack([W["k_k"][i], W["k_a"][i], W["r_k"][i].reshape(CA), W["ln_x_w"][i], W["ln_x_b"][i]])
    o, s_fin = rwkv_scan(rkvg.reshape(4, B, T, CA), lw.reshape(B, T, CA), a.reshape(B, T, CA), pvec, s0)
    last = rmsnorm(h.reshape(B, T, D)[:, -1], g)
    return o.reshape(N, CA), s_fin, last


def _nsa_layer(h, jb, shared, W, slopes, B, T, norm_g):
    N, D = h.shape
    CB = W["w_out_b"].shape[1]
    proj3 = norm_mm(h, norm_g, W["w_in_b"], (jb,)).reshape(B, T, -1)
    if shared["past"] is None:
        o_cmp, selm = nsa_cmp(proj3, shared["kvc"], slopes, tq=min(T, 128), nc=shared["nc"],
                              nsb=shared["nsb"], pos0=0)
        o = nsa_selwin_prompt(proj3, shared["rows"], selm, o_cmp, slopes, tq=min(T, 128))
    else:
        TQ = SUBLANES
        projp = jnp.pad(proj3, ((0, 0), (0, TQ - T), (0, 0)))
        o_cmp, selm = nsa_cmp(projp, shared["kvc"], slopes, tq=TQ, nc=shared["nc"],
                              nsb=shared["nsb"], pos0=shared["pos0"])
        pool, table, cwin = shared["past"]
        o = nsa_selwin_sample(projp, selm, o_cmp, pool, table, shared["new_rows"], cwin, slopes,
                              pos0=shared["pos0"], n_new=T)[:, :T]
    return o.reshape(N, CB)


def _trunk(x, p, pos0, wkv0, shift0, past, W, slopes):
    B, T, D = x.shape
    N = B * T
    depth = p.shape[0]
    n_a = W["w_in_a"].shape[0]
    GW = G_KV * HEAD_B
    h = x.reshape(N, D)
    wkv_new, shift_new = [], []
    shared, kv_rows, win_state = None, None, None
    for i in range(depth):
        if i < n_a:
            o, s_fin, last = _rwkv_layer(h, shift0[i], wkv0[i], i, W, B, T)
            wkv_new.append(s_fin)
            shift_new.append(last)
            h = out_ple(o, h, p[i].reshape(N, -1), W["w_out_a"], i, W["w_ple"], W["w_ple_gate"], i)
        else:
            o = _nsa_layer(h, i - n_a, shared, W, slopes, B, T, W["norm_g"][i])
            h = out_ple(o, h, p[i].reshape(N, -1), W["w_out_b"], i - n_a, W["w_ple"], W["w_ple_gate"], i)
        if i == n_a - 1:
            rows = norm_mm(h, W["kv_norm_g"], W["w_kv"]).reshape(B, T, 6 * GW)
            kv_rows = rows[:, :, :4 * GW].reshape(B, T, 4, G_KV, HEAD_B)
            win_new = rows[:, :, 4 * GW:].reshape(B, T, 2, G_KV, HEAD_B)
            if past is None:
                PS = 128
                pool = rows.reshape(B * T // PS, PS // S_CMP, S_CMP, 6 * G_KV, HEAD_B)
                table = jnp.arange(B * T // PS, dtype=jnp.int32).reshape(B, T // PS)
                t_all = T
                win_all = win_new
                shared = {"past": None, "rows": rows}
            else:
                pool, table, cwin = past
                PS = pool.shape[1] * S_CMP
                t_all = pos0 + T
                win_all = jnp.concatenate([cwin.reshape(B, -1, 2, G_KV, HEAD_B), win_new], axis=1)
                NN = LANES
                shared = {"past": past, "new_rows": jnp.pad(rows, ((0, 0), (0, NN - T), (0, 0)))}
            win_state = win_all[:, win_all.shape[1] - min(WINDOW, pos0 + T):]
            nc = (t_all - L_CMP) // S_CMP + 1
            assert nc < table.shape[1] * PS // S_CMP
            kvc = compress_kv(pool, table, W["pe_cmp"], W["w_cmp1"], W["w_cmp2"])
            shared.update(kvc=kvc, nc=nc, nsb=max(-(-t_all // L_SEL), TOPK_SEL), pos0=pos0)
    y = rmsnorm(h, W["final_norm_g"]).reshape(B, T, D)
    return y, jnp.stack(wkv_new), jnp.stack(shift_new), kv_rows, win_state


def kernel(x_prompt, x_sample, state_wkv, state_shift, cache_kv, cache_win_kv, page_table, p_prompt, p_sample, norm_g, mu_a, w_in_a, w_lora_w1, w_lora_w2, w0_a, a_lora1, a_lora2, a0_a, k_k, k_a, r_k, ln_x_w, ln_x_b, w_out_a, w_in_b, w_out_b, kv_norm_g, w_kv, pe_cmp, w_cmp1, w_cmp2, w_ple, w_ple_gate, final_norm_g):
    bf = lambda w: w.astype(BF16)
    CA = w_out_a.shape[1]
    W = dict(norm_g=norm_g, mu_a=mu_a, w_in_a=bf(w_in_a), w_lora_w1=bf(w_lora_w1), w_lora_w2=bf(w_lora_w2),
             w0_a=w0_a, a_lora1=bf(a_lora1), a_lora2=bf(a_lora2), a0_a=a0_a, k_k=k_k, k_a=k_a,
             r_k=r_k, ln_x_w=ln_x_w, ln_x_b=ln_x_b,
             w_out_a=bf(w_out_a), w_in_b=bf(w_in_b), w_out_b=bf(w_out_b), kv_norm_g=kv_norm_g, w_kv=bf(w_kv),
             pe_cmp=pe_cmp, w_cmp1=bf(w_cmp1), w_cmp2=bf(w_cmp2), w_ple=bf(w_ple), w_ple_gate=bf(w_ple_gate),
             final_norm_g=final_norm_g)
    HB = w_out_b.shape[1] // HEAD_B
    slopes = 2.0 ** (-8.0 * jnp.arange(1, HB + 1, dtype=F32) / HB)
    bp = x_prompt.shape[0]
    n_a = w_in_a.shape[0]
    D = x_prompt.shape[-1]
    wkv0 = jnp.zeros((n_a, bp, CA // HEAD_A, HEAD_A, HEAD_A), F32)
    shift0 = jnp.zeros((n_a, bp, D), F32)
    y_p, wkv_p, shift_p, kv_p, win_p = _trunk(x_prompt, p_prompt, 0, wkv0, shift0, None, W, slopes)
    db, n_pages = page_table.shape
    NP, PS = cache_kv.shape[:2]
    pool5 = cache_kv.reshape(NP, PS // S_CMP, S_CMP, -1, HEAD_B)
    past = (pool5, page_table, cache_win_kv.reshape(db, cache_win_kv.shape[1], -1))
    y_s, wkv_s, shift_s, kv_s, win_s = _trunk(x_sample, p_sample, n_pages * PS, state_wkv, state_shift, past, W, slopes)
    return (y_p, y_s, wkv_p, shift_p, kv_p, win_p, wkv_s, shift_s, kv_s, win_s)
```

```python
import functools
import math

import jax
import jax.numpy as jnp
from jax import lax
from jax.experimental import pallas as pl
from jax.experimental.pallas import tpu as pltpu

F32 = jnp.float32
BF16 = jnp.bfloat16

HEAD_A = 64
GN_EPS = 64e-5
HEAD_B = 128
G_KV = 4
L_CMP = 32
S_CMP = 16
L_SEL = 64
TOPK_SEL = 16
WINDOW = 512
RMS_EPS = 1e-6
NEG = -1e30
FORCE_BONUS = 1e4

LANES = 128
SUBLANES = 8
VMEM_LIMIT = 56 * 1024 * 1024

SCAN_NH = 2
SCAN_C = 64
SCAN_GP = 16

NT_DIMS = (((1,), (1,)), ((), ()))
TN_DIMS = (((0,), (0,)), ((), ()))


def _cparams(sem):
    return pltpu.CompilerParams(dimension_semantics=sem, vmem_limit_bytes=VMEM_LIMIT)


def _bdot(a, b):
    return jnp.dot(a.astype(BF16), b.astype(BF16), preferred_element_type=F32)


def _bdot_nt(a, b):
    return lax.dot_general(a.astype(BF16), b.astype(BF16), NT_DIMS, preferred_element_type=F32)


def _bdot_tn(a, b):
    return lax.dot_general(a.astype(BF16), b.astype(BF16), TN_DIMS, preferred_element_type=F32)


def _rms_kernel(x_ref, g_ref, o_ref):
    x = x_ref[...]
    ms = jnp.mean(x * x, axis=-1, keepdims=True)
    o_ref[...] = x * lax.rsqrt(ms + RMS_EPS) * g_ref[...]


def rmsnorm(x, g):
    M, D = x.shape
    tm = min(M, 256)
    return pl.pallas_call(
        _rms_kernel,
        grid=(pl.cdiv(M, tm),),
        in_specs=[pl.BlockSpec((tm, D), lambda i: (i, 0)),
                  pl.BlockSpec((1, D), lambda i: (0, 0))],
        out_specs=pl.BlockSpec((tm, D), lambda i: (i, 0)),
        out_shape=jax.ShapeDtypeStruct((M, D), F32),
        compiler_params=_cparams(("parallel",)),
        name="rmsnorm",
    )(x, g.reshape(1, D))


def _mm_kernel(x_ref, w_ref, o_ref):
    o_ref[...] = jnp.dot(x_ref[...].astype(BF16), w_ref[...], preferred_element_type=F32)


def mm(x, w, widx=()):
    M, K = x.shape
    N = w.shape[-1]
    assert w.shape[-2] == K and len(widx) == w.ndim - 2
    tm = min(M, 1024 if K <= 2048 else 512)
    tn = N if N <= 512 else 512
    nlead = len(widx)
    w_spec = pl.BlockSpec((None,) * nlead + (K, tn), lambda i, j: tuple(widx) + (0, j))
    return pl.pallas_call(
        _mm_kernel,
        grid=(pl.cdiv(M, tm), pl.cdiv(N, tn)),
        in_specs=[pl.BlockSpec((tm, K), lambda i, j: (i, 0)), w_spec],
        out_specs=pl.BlockSpec((tm, tn), lambda i, j: (i, j)),
        out_shape=jax.ShapeDtypeStruct((M, N), F32),
        compiler_params=_cparams(("parallel", "parallel")),
        name="mm",
    )(x, w)


def _norm_rows(x, g):
    return x * lax.rsqrt(jnp.mean(x * x, axis=-1, keepdims=True) + RMS_EPS) * g


def _sigmoid(x):
    return 0.5 + 0.5 * jnp.tanh(0.5 * x)


def _norm_and_shift(h_ref, hprev_ref, xprev_ref, g_ref, i, tm, T):
    g = g_ref[...]
    hn = _norm_rows(h_ref[...], g)
    prev_row = _norm_rows(hprev_ref[SUBLANES - 1:SUBLANES, :], g)
    row = lax.broadcasted_iota(jnp.int32, (tm, 1), 0)
    xs = jnp.where(row == 0, prev_row, pltpu.roll(hn, 1, axis=0))
    if T >= tm:
        assert T % tm == 0
        start = (i * tm) % T == 0
        xs = jnp.where((row == 0) & start, xprev_ref[pl.ds((i * tm) // T, 1), :], xs)
    else:
        assert tm % T == 0
        for bb in range(tm // T):
            xs = jnp.where(row == bb * T, xprev_ref[pl.ds(i * (tm // T) + bb, 1), :], xs)
    return hn, xs


def _rwkv_in_kernel(h_ref, hprev_ref, xprev_ref, g_ref, mu_ref, w_ref, o_ref, xs_scr, *, tm, T):
    i, j, n = pl.program_id(0), pl.program_id(1), pl.program_id(2)

    @pl.when(n == 0)
    def _():
        hn, xs = _norm_and_shift(h_ref, hprev_ref, xprev_ref, g_ref, i, tm, T)
        xs_scr[...] = (hn + (xs - hn) * mu_ref[pl.ds(j, 1), :]).astype(BF16)

    o_ref[...] = jnp.dot(xs_scr[...], w_ref[...], preferred_element_type=F32)


def _shift_specs(tm, D, nb, ngrid):
    z = (0,) * (ngrid - 1)
    wrap = lambda f: (lambda i, *_: f(i))
    return [pl.BlockSpec((tm, D), wrap(lambda i: (i, 0))),
            pl.BlockSpec((SUBLANES, D), wrap(lambda i: (jnp.maximum(i * (tm // SUBLANES) - 1, 0), 0))),
            pl.BlockSpec((nb, D), wrap(lambda i: (0, 0))),
            pl.BlockSpec((1, D), wrap(lambda i: (0, 0))),
            pl.BlockSpec((6, D), wrap(lambda i: (0, 0)))]


def rwkv_in(h, xprev, g, mu, w, layer, T):
    N, D = h.shape
    C = w.shape[-1]
    tm = min(N, 1024)
    tn = 512
    return pl.pallas_call(
        functools.partial(_rwkv_in_kernel, tm=tm, T=T),
        grid=(N // tm, 4, C // tn),
        in_specs=_shift_specs(tm, D, xprev.shape[0], 3) + [
            pl.BlockSpec((None, None, D, tn), lambda i, j, n: (layer, j, 0, n))],
        out_specs=pl.BlockSpec((None, tm, tn), lambda i, j, n: (j, i, n)),
        out_shape=jax.ShapeDtypeStruct((4, N, C), F32),
        scratch_shapes=[pltpu.VMEM((tm, D), BF16)],
        compiler_params=_cparams(("parallel", "arbitrary", "arbitrary")),
        name="rwkv_in",
    )(h, h, xprev, g.reshape(1, D), mu, w)


def _rwkv_lora_kernel(h_ref, hprev_ref, xprev_ref, g_ref, mu_ref, lw1_ref, lw2_ref, la1_ref, la2_ref,
                      w0_ref, a0_ref, lw_ref, a_ref, *, tm, T):
    hn, xs = _norm_and_shift(h_ref, hprev_ref, xprev_ref, g_ref, pl.program_id(0), tm, T)
    dx = xs - hn
    x4 = (hn + dx * mu_ref[4:5, :]).astype(BF16)
    x5 = (hn + dx * mu_ref[5:6, :]).astype(BF16)
    t4 = jnp.tanh(jnp.dot(x4, lw1_ref[...], preferred_element_type=F32)).astype(BF16)
    y = -(w0_ref[...] + jnp.dot(t4, lw2_ref[...], preferred_element_type=F32))
    softplus = jnp.maximum(y, 0.0) + jnp.log(1.0 + jnp.exp(-jnp.abs(y)))
    lw_ref[...] = -jnp.exp(-softplus - 0.5)
    t5 = jnp.dot(x5, la1_ref[...], preferred_element_type=F32).astype(BF16)
    a_ref[...] = _sigmoid(a0_ref[...] + jnp.dot(t5, la2_ref[...], preferred_element_type=F32))


def rwkv_lora(h, xprev, g, mu, lw1, lw2, la1, la2, w0, a0, T):
    N, D = h.shape
    R, C = lw2.shape
    tm = min(N, 256)
    full = lambda shape: pl.BlockSpec(shape, lambda i: (0,) * len(shape))
    o_spec = pl.BlockSpec((tm, C), lambda i: (i, 0))
    return pl.pallas_call(
        functools.partial(_rwkv_lora_kernel, tm=tm, T=T),
        grid=(N // tm,),
        in_specs=_shift_specs(tm, D, xprev.shape[0], 1) + [
            full((D, R)), full((R, C)), full((D, R)), full((R, C)), full((1, C)), full((1, C))],
        out_specs=[o_spec, o_spec],
        out_shape=[jax.ShapeDtypeStruct((N, C), F32)] * 2,
        compiler_params=_cparams(("parallel",)),
        name="rwkv_lora",
    )(h, h, xprev, g.reshape(1, D), mu, lw1, lw2, la1, la2, w0.reshape(1, C), a0.reshape(1, C))


def _norm_mm_kernel(h_ref, g_ref, w_ref, o_ref, xs_scr):
    @pl.when(pl.program_id(1) == 0)
    def _():
        rc = min(h_ref.shape[0], 256)

        def norm_chunk(c, carry):
            rows = pl.ds(pl.multiple_of(c * rc, rc), rc)
            xs_scr[rows, :] = _norm_rows(h_ref[rows, :], g_ref[...]).astype(BF16)
            return carry

        lax.fori_loop(0, h_ref.shape[0] // rc, norm_chunk, 0)

    o_ref[...] = jnp.dot(xs_scr[...], w_ref[...], preferred_element_type=F32)


def norm_mm(h, g, w, widx=()):
    N, D = h.shape
    NO = w.shape[-1]
    tm = min(N, 1024)
    tn = 1024
    nlead = len(widx)
    return pl.pallas_call(
        _norm_mm_kernel,
        grid=(N // tm, pl.cdiv(NO, tn)),
        in_specs=[pl.BlockSpec((tm, D), lambda i, n: (i, 0)),
                  pl.BlockSpec((1, D), lambda i, n: (0, 0)),
                  pl.BlockSpec((None,) * nlead + (D, tn), lambda i, n: tuple(widx) + (0, n))],
        out_specs=pl.BlockSpec((tm, tn), lambda i, n: (i, n)),
        out_shape=jax.ShapeDtypeStruct((N, NO), F32),
        scratch_shapes=[pltpu.VMEM((tm, D), BF16)],
        compiler_params=_cparams(("parallel", "arbitrary")),
        name="norm_mm",
    )(h, g.reshape(1, D), w)


def _mm_res_kernel(x_ref, h_ref, w_ref, o_ref):
    o_ref[...] = h_ref[...] + jnp.dot(x_ref[...].astype(BF16), w_ref[...], preferred_element_type=F32)


def _ple_gate_kernel(h1_ref, h1t_ref, p_ref, wp_ref, wg_ref, o_ref, xb_scr):
    @pl.when(pl.program_id(1) == 0)
    def _():
        rc = min(h1_ref.shape[0], 256)

        def cast_chunk(c, carry):
            rows = pl.ds(pl.multiple_of(c * rc, rc), rc)
            xb_scr[rows, :] = h1_ref[rows, :].astype(BF16)
            return carry

        lax.fori_loop(0, h1_ref.shape[0] // rc, cast_chunk, 0)

    gate = jnp.dot(xb_scr[...], wg_ref[...], preferred_element_type=F32)
    ple = jnp.dot(p_ref[...].astype(BF16), wp_ref[...], preferred_element_type=F32)
    o_ref[...] = h1t_ref[...] + ple * _sigmoid(gate)


def out_ple(x, h, p, w_out, oidx, w_ple, w_gate, layer):
    N, C = x.shape
    D = h.shape[1]
    DP = p.shape[1]
    tm = min(N, 1024)
    tn = 512
    grid = (N // tm, D // tn)
    tile = pl.BlockSpec((tm, tn), lambda i, n: (i, n))
    h1 = pl.pallas_call(
        _mm_res_kernel,
        grid=grid,
        in_specs=[pl.BlockSpec((tm, C), lambda i, n: (i, 0)), tile,
                  pl.BlockSpec((None, C, tn), lambda i, n: (oidx, 0, n))],
        out_specs=tile,
        out_shape=jax.ShapeDtypeStruct((N, D), F32),
        compiler_params=_cparams(("parallel", "parallel")),
        name="mm_res",
    )(x, h, w_out)
    return pl.pallas_call(
        _ple_gate_kernel,
        grid=grid,
        in_specs=[pl.BlockSpec((tm, D), lambda i, n: (i, 0)), tile,
                  pl.BlockSpec((tm, DP), lambda i, n: (i, 0)),
                  pl.BlockSpec((None, DP, tn), lambda i, n: (layer, 0, n)),
                  pl.BlockSpec((None, D, tn), lambda i, n: (layer, 0, n))],
        out_specs=tile,
        out_shape=jax.ShapeDtypeStruct((N, D), F32),
        scratch_shapes=[pltpu.VMEM((tm, D), BF16)],
        compiler_params=_cparams(("parallel", "arbitrary")),
        name="ple_gate",
    )(h1, h1, p, w_ple, w_gate)


def _scan_kernel(r_ref, k_ref, v_ref, zg_ref, lw_ref, a_ref, pv_ref, s0_ref, o_ref, sfin_ref, s_scr, *, C, NH, GP):
    L = NH * HEAD_A
    NC = NH * C
    ci = pl.program_id(2)

    @pl.when(ci == 0)
    def _():
        s_scr[...] = s0_ref[0]

    row_c = lax.broadcasted_iota(jnp.int32, (C, NC), 0)
    col_s = lax.broadcasted_iota(jnp.int32, (C, NC), 1) % C
    tri_strict = col_s < row_c
    tri_incl = (lax.broadcasted_iota(jnp.int32, (C, 2 * NC), 1) % C
                <= lax.broadcasted_iota(jnp.int32, (C, 2 * NC), 0))
    st_mask = (lax.broadcasted_iota(jnp.int32, (NC, L), 0) // C
               == lax.broadcasted_iota(jnp.int32, (NC, L), 1) // HEAD_A)
    bd_mask = (lax.broadcasted_iota(jnp.int32, (NC, NC), 0) // C
               == lax.broadcasted_iota(jnp.int32, (NC, NC), 1) // C)
    head_mask = (lax.broadcasted_iota(jnp.int32, (L, L), 0) // HEAD_A
                 == lax.broadcasted_iota(jnp.int32, (L, L), 1) // HEAD_A)

    def st(x):
        return jnp.where(st_mask, jnp.concatenate([x] * NH, axis=0), 0.0)

    def bd(w):
        return jnp.where(bd_mask, jnp.concatenate([w] * NH, axis=0), 0.0)

    n_double = int(math.log2(C))
    each = lambda f, *cols: [f(*xs) for xs in zip(*cols)]
    sls = [slice(gp * L, (gp + 1) * L) for gp in range(GP)]
    head_of_lane = lax.broadcasted_iota(jnp.int32, (1, L), 1) // HEAD_A

    def hsum(x):
        out = None
        for hh in range(NH):
            sh = jnp.sum(jnp.where(head_of_lane == hh, x, 0.0), axis=-1, keepdims=True)
            out = sh if out is None else jnp.where(head_of_lane == hh, sh, out)
        return out

    k_k, k_a, r_k, ln_w, ln_b = ([pv_ref[n:n + 1, sl] for sl in sls] for n in range(5))
    lw = [lw_ref[0, :, sl] for sl in sls]
    a_sig = [a_ref[0, :, sl] for sl in sls]
    r = [r_ref[0, :, sl] for sl in sls]
    k_raw = [k_ref[0, :, sl] for sl in sls]
    v = [v_ref[0, :, sl] for sl in sls]
    kk = each(lambda x, w: x * w, k_raw, k_k)
    kk = each(lambda x: x / jnp.maximum(jnp.sqrt(hsum(x * x)), 1e-12), kk)
    k = each(lambda x, a, w: x * (1.0 + (a - 1.0) * w), k_raw, a_sig, k_a)
    row_t = lax.broadcasted_iota(jnp.int32, (C, 1), 0)
    cum = lw
    for sh in (1 << e for e in range(n_double)):
        cum = each(lambda z: z + jnp.where(row_t >= sh, pltpu.roll(z, sh, axis=0), 0.0), cum)
    p_incl = each(jnp.exp, cum)
    p_inv = each(lambda z: jnp.exp(-z), cum)
    at = each(lambda x, c, w: -x * jnp.exp(c - w), kk, cum, lw)
    rt = each(lambda x, p: x * p, r, p_incl)
    bt = each(lambda x, a, p: x * a * p, kk, a_sig, p_inv)
    kt = each(lambda x, p: x * p, k, p_inv)
    S = [s_scr[gp] for gp in range(GP)]
    ar = each(lambda x, y: jnp.concatenate([x, y], axis=0), at, rt)
    bk_st = each(lambda x, y: jnp.concatenate([st(x), st(y)], axis=0), bt, kt)
    Gm = each(_bdot_nt, ar, bk_st)
    w_ab = [jnp.where(tri_strict, g[:C, :NC], 0.0) for g in Gm]
    tm = w_ab
    pw = each(lambda w: _bdot(w, bd(w)), w_ab)
    LH = each(_bdot_nt, ar, S)
    v_st = each(st, v)
    x = [lh[:C] + _bdot(jnp.where(tri_strict, g[:C, NC:], 0.0), vs) for lh, g, vs in zip(LH, Gm, v_st)]
    for it in range(1, n_double):
        if it < n_double - 1:
            both = each(lambda t, p: _bdot(p, jnp.concatenate([bd(t), bd(p)], axis=1)), tm, pw)
            tm = each(lambda t, p, b2: t + p + b2[:, :NC], tm, pw, both)
            pw = [b2[:, NC:] for b2 in both]
        else:
            tm = each(lambda t, p: t + p + _bdot(p, bd(t)), tm, pw)
    u = each(lambda xx, t: xx + _bdot(t, st(xx)), x, tm)
    o = [lh[C:] + _bdot(jnp.where(tri_incl, g[C:], 0.0), jnp.concatenate([st(uu), vs], axis=0))
         for lh, g, uu, vs in zip(LH, Gm, u, v_st)]
    inv_n = 1.0 / HEAD_A
    dev = each(lambda x: x - hsum(x) * inv_n, o)
    gn = each(lambda d, w, b_: d * lax.rsqrt(hsum(d * d) * inv_n + GN_EPS) * w + b_, dev, ln_w, ln_b)
    bonus = each(lambda rr, kx, w, vv: hsum(rr * kx * w) * vv, r, k, r_k, v)
    for sl, y, bo in zip(sls, gn, bonus):
        zg = zg_ref[0, :, sl]
        o_ref[0, :, sl] = ((y + bo) * (zg * _sigmoid(zg))).astype(o_ref.dtype)
    ds = [_bdot_tn(jnp.concatenate([uu, vv], axis=0), jnp.concatenate([b_, k_], axis=0))
          for uu, vv, b_, k_ in zip(u, v, bt, kt)]
    for gp in range(GP):
        s_scr[gp] = (S[gp] + jnp.where(head_mask, ds[gp], 0.0)) * p_incl[gp][C - 1:C, :]

    @pl.when(ci == pl.num_programs(2) - 1)
    def _():
        sfin_ref[0] = s_scr[...]


def rwkv_scan(rkvg, lw, a, pvec, s0):
    _, B, T, CA = rkvg.shape
    H = CA // HEAD_A
    NH, C, GP = SCAN_NH, SCAN_C, SCAN_GP
    assert NH * C == LANES and H % (NH * GP) == 0
    L = NH * HEAD_A
    NG = H // NH
    Tp = -(-T // C) * C
    if Tp != T:
        rkvg = jnp.pad(rkvg, ((0, 0), (0, 0), (0, Tp - T), (0, 0)))
        lw, a = (jnp.pad(z, ((0, 0), (0, Tp - T), (0, 0))) for z in (lw, a))
    eye = jnp.eye(NH, dtype=F32)
    s0_bd = (s0.reshape(B, NG, NH, HEAD_A, 1, HEAD_A) * eye[None, None, :, None, :, None]).reshape(B, NG, L, L)
    seq_spec = pl.BlockSpec((1, C, GP * L), lambda bi, gi, ci: (bi, ci, gi))
    proj_spec = lambda j: pl.BlockSpec((None, 1, C, GP * L), lambda bi, gi, ci: (j, bi, ci, gi))
    st_spec = pl.BlockSpec((1, GP, L, L), lambda bi, gi, ci: (bi, gi, 0, 0))
    o, sfin = pl.pallas_call(
        functools.partial(_scan_kernel, C=C, NH=NH, GP=GP),
        grid=(B, NG // GP, Tp // C),
        in_specs=[proj_spec(j) for j in range(4)] + [seq_spec, seq_spec,
                  pl.BlockSpec((5, GP * L), lambda bi, gi, ci: (0, gi)), st_spec],
        out_specs=[seq_spec, st_spec],
        out_shape=[jax.ShapeDtypeStruct((B, Tp, CA), BF16), jax.ShapeDtypeStruct((B, NG, L, L), F32)],
        scratch_shapes=[pltpu.VMEM((GP, L, L), F32)],
        compiler_params=_cparams(("parallel", "parallel", "arbitrary")),
        name="rwkv_scan",
    )(rkvg, rkvg, rkvg, rkvg, lw, a, pvec, s0_bd)
    sf = sfin.reshape(B, NG, NH, HEAD_A, NH, HEAD_A)
    s_fin = jnp.stack([sf[:, :, h, :, h, :] for h in range(NH)], axis=2).reshape(B, H, HEAD_A, HEAD_A)
    return o[:, :T], s_fin


def _gelu_tanh(x):
    c = math.sqrt(2.0 / math.pi)
    return 0.5 * x * (1.0 + jnp.tanh(c * (x + 0.044715 * (x * x * x))))


def _compress_kernel(pt_ref, *refs, PGS):
    del pt_ref
    page_refs = refs[:PGS]
    next_ref, pe_ref, w1_ref, w2_ref, out_ref = refs[PGS:]
    CPP = page_refs[0].shape[1]
    NCH = PGS * CPP
    CG = 2 * G_KV
    M = (NCH + 1) * CG

    def rows_of(l, hf):
        pe = pe_ref[hf, l]
        parts = [(page_refs[i][0, :, l] + pe[None]).reshape(CPP * CG, HEAD_B) for i in range(PGS)]
        parts.append(next_ref[0, 0, l] + pe)
        return jnp.concatenate(parts, axis=0)

    top = jnp.zeros((M, 2 * HEAD_B), F32)
    bot = jnp.zeros((M, 2 * HEAD_B), F32)
    for l in range(0, S_CMP, 2):
        wrows = pl.ds(l * HEAD_B, 2 * HEAD_B)
        xt = jnp.concatenate([rows_of(l, 0), rows_of(l + 1, 0)], axis=1).astype(BF16)
        top = top + jnp.dot(xt, w1_ref[0, wrows, :], preferred_element_type=F32)
        xb = jnp.concatenate([rows_of(l, 1), rows_of(l + 1, 1)], axis=1).astype(BF16)
        bot = bot + jnp.dot(xb, w1_ref[1, wrows, :], preferred_element_type=F32)
    is_k = (lax.broadcasted_iota(jnp.int32, (M, 1), 0) % CG) < G_KV
    pick = lambda z, n: jnp.where(is_k[:n], z[:n, :HEAD_B], z[:n, HEAD_B:])
    hcur = pick(top, NCH * CG) + pick(bot, M)[CG:]
    o2 = jnp.dot(_gelu_tanh(hcur).astype(BF16), w2_ref[...], preferred_element_type=F32)
    out_ref[0] = pick(o2, NCH * CG).reshape(NCH, CG, HEAD_B)


def compress_kv(pool5, table, pe_cmp, w1, w2):
    NP, CPP = pool5.shape[:2]
    B, n_pages = table.shape
    PGS = max(d for d in (8, 4, 2, 1) if n_pages % d == 0)
    NCH = PGS * CPP
    CG = 2 * G_KV
    half = S_CMP * HEAD_B
    pe_r = jnp.repeat(pe_cmp.reshape(2, 2, S_CMP, HEAD_B).transpose(1, 2, 0, 3), G_KV, axis=2)
    w1_r = w1.reshape(2, 2, half, HEAD_B).transpose(1, 2, 0, 3).reshape(2, half, 2 * HEAD_B)
    w2_r = jnp.concatenate([w2[0], w2[1]], axis=1)

    def page_map(i):
        return lambda b, s, pt: (pt[b, s * PGS + i], 0, 0, 0, 0)

    def next_map(b, s, pt):
        return (pt[b, jnp.minimum((s + 1) * PGS, n_pages - 1)], 0, 0, 0, 0)

    const = lambda n: (lambda b, s, pt: (0,) * n)
    grid_spec = pltpu.PrefetchScalarGridSpec(
        num_scalar_prefetch=1,
        grid=(B, n_pages // PGS),
        in_specs=[pl.BlockSpec((1, CPP, S_CMP, CG, HEAD_B), page_map(i)) for i in range(PGS)] + [
            pl.BlockSpec((1, 1, S_CMP, CG, HEAD_B), next_map),
            pl.BlockSpec((2, S_CMP, CG, HEAD_B), const(4)),
            pl.BlockSpec((2, half, 2 * HEAD_B), const(3)),
            pl.BlockSpec((HEAD_B, 2 * HEAD_B), const(2)),
        ],
        out_specs=pl.BlockSpec((1, NCH, CG, HEAD_B), lambda b, s, pt: (b, s, 0, 0)),
    )
    return pl.pallas_call(
        functools.partial(_compress_kernel, PGS=PGS),
        grid_spec=grid_spec,
        out_shape=jax.ShapeDtypeStruct((B, n_pages * CPP, CG, HEAD_B), F32),
        compiler_params=_cparams(("parallel", "arbitrary")),
        name="compress_kv",
    )(table, *([pool5] * PGS), pool5, pe_r, w1_r, w2_r)


def _stack_heads(q, HG):
    return jnp.concatenate([q[:, h * HEAD_B:(h + 1) * HEAD_B] for h in range(HG)], axis=0)


def _masked_softmax_rows(s, mask):
    s = jnp.where(mask, s, NEG)
    m = jnp.max(s, axis=-1, keepdims=True)
    e = jnp.where(mask, jnp.exp(s - m), 0.0)
    l = jnp.sum(e, axis=-1, keepdims=True)
    return e / jnp.where(l > 0.0, l, 1.0)


def _attend_stacked(s, dist, mask, v, slopes_ref, g, HG, tq):
    ps = []
    psum = jnp.zeros(dist.shape, F32)
    for h in range(HG):
        p = _masked_softmax_rows(s[h * tq:(h + 1) * tq] - slopes_ref[g * HG + h] * dist, mask)
        psum = psum + p
        ps.append(p.astype(BF16))
    o = jnp.dot(jnp.concatenate(ps, axis=0), v.astype(BF16), preferred_element_type=F32)
    return o, psum


def _unstack_store(o_ref, o, HG, tq):
    for h in range(HG):
        o_ref[0, :, h * HEAD_B:(h + 1) * HEAD_B] = o[h * tq:(h + 1) * tq]


def _nsa_cmp_kernel(slopes_ref, q_ref, kc_ref, vc_ref, o_ref, selm_ref, *, tq, HG, nc, nsb, pos0):
    g = pl.program_id(1)
    qt = pl.program_id(2)
    NCp = kc_ref.shape[1]
    NSBp = selm_ref.shape[3]
    q_st = _stack_heads(q_ref[0] * (HEAD_B ** -0.5), HG)
    s = _bdot_nt(q_st, kc_ref[0])
    qpos = pos0 + qt * tq + lax.broadcasted_iota(jnp.int32, (tq, 1), 0)
    cidx = lax.broadcasted_iota(jnp.int32, (1, NCp), 1)
    cend = S_CMP * cidx + (L_CMP - 1)
    mask = (cend <= qpos) & (cidx < nc)
    dist = (qpos - cend).astype(F32)
    o, imp_c = _attend_stacked(s, dist, mask, vc_ref[0], slopes_ref, g, HG, tq)
    _unstack_store(o_ref, o, HG, tq)

    crow = lax.broadcasted_iota(jnp.int32, (NCp, NSBp), 0)
    jcol = lax.broadcasted_iota(jnp.int32, (NCp, NSBp), 1)
    overlap = ((S_CMP * crow < L_SEL * (jcol + 1)) & (S_CMP * crow + L_CMP > L_SEL * jcol)
               & (crow < nc)).astype(F32)
    imp = jnp.dot(imp_c, overlap, precision=lax.Precision.HIGHEST, preferred_element_type=F32)
    lane = lax.broadcasted_iota(jnp.int32, (tq, NSBp), 1)
    cur = jnp.right_shift(qpos, int(math.log2(L_SEL)))
    forced = ((lane == 0) | (lane == cur) | (lane == cur - 1)).astype(F32)
    score = jnp.where(lane <= cur, imp + FORCE_BONUS * forced, NEG)
    score = jnp.where(lane < nsb, score, -3e38)

    cnt = jnp.zeros((tq, NSBp), F32)
    for i in range(nsb):
        col = score[:, i:i + 1]
        beats = (col > score) | ((col == score) & (lane > i))
        cnt = cnt + jnp.where(beats, 1.0, 0.0)
    sel = (cnt < TOPK_SEL) & (score > 0.5 * NEG)
    selm_ref[0, 0] = sel.astype(F32)


def nsa_cmp(proj, kvc, slopes, *, tq, nc, nsb, pos0):
    B, T, _ = proj.shape
    HG = slopes.shape[0] // G_KV
    NCp = kvc.shape[1]
    NSBp = -(-nsb // LANES) * LANES
    gw = HG * HEAD_B
    return pl.pallas_call(
        functools.partial(_nsa_cmp_kernel, tq=tq, HG=HG, nc=nc, nsb=nsb, pos0=pos0),
        grid=(B, G_KV, T // tq),
        in_specs=[pl.BlockSpec(memory_space=pltpu.SMEM),
                  pl.BlockSpec((1, tq, gw), lambda b, g, t: (b, t, g)),
                  pl.BlockSpec((1, NCp, HEAD_B), lambda b, g, t: (b, 0, g)),
                  pl.BlockSpec((1, NCp, HEAD_B), lambda b, g, t: (b, 0, G_KV + g))],
        out_specs=[pl.BlockSpec((1, tq, gw), lambda b, g, t: (b, t, g)),
                   pl.BlockSpec((1, 1, tq, NSBp), lambda b, g, t: (b, g, t, 0))],
        out_shape=[jax.ShapeDtypeStruct((B, T, G_KV * gw), F32),
                   jax.ShapeDtypeStruct((B, G_KV, T, NSBp), F32)],
        compiler_params=_cparams(("parallel", "parallel", "parallel")),
        name="nsa_cmp",
    )(slopes, proj, kvc.reshape(B, NCp, -1), kvc.reshape(B, NCp, -1))


LOG2E = 1.4426950408889634


def _bf16_part(x):
    return x.astype(BF16).astype(F32)


def _alibi_lhs(q, slope_col):
    c = slope_col * LOG2E
    c1 = _bf16_part(c)
    c2 = _bf16_part(c - c1)
    c3 = _bf16_part(c - c1 - c2)
    lane = lax.broadcasted_iota(jnp.int32, q.shape, 1)
    extra = jnp.where((lane == 0) | (lane == 3), c1,
                      jnp.where((lane == 1) | (lane == 4), c2, jnp.where((lane == 2) | (lane == 5), c3, 0.0)))
    return jnp.concatenate([q, extra], axis=1).astype(BF16)


def _alibi_rhs(k, k0):
    pos = k0 + lax.broadcasted_iota(jnp.int32, k.shape, 0)
    lane = lax.broadcasted_iota(jnp.int32, k.shape, 1)
    hi = jnp.bitwise_and(pos, -L_SEL)
    extra = jnp.where(lane < 3, hi, jnp.where(lane < 6, pos - hi, 0)).astype(F32)
    return jnp.concatenate([k, extra], axis=1).astype(BF16)


def _with_ones(v):
    lane = lax.broadcasted_iota(jnp.int32, v.shape, 1)
    return jnp.concatenate([v, jnp.where(lane == 0, 1.0, 0.0)], axis=1).astype(BF16)


def _silu(x):
    return x * _sigmoid(x)


def _nsa_selwin_prompt_kernel(slopes_ref, q_ref, selm_ref, ks_ref, vs_ref, kw_ref, vw_ref,
                              ocmp_ref, zc_ref, zs_ref, zw_ref, gate_ref, o_ref, s_scr, *, tq, HG, T, WS, SEG):
    g = pl.program_id(1)
    qt = pl.program_id(2)
    NSBp = selm_ref.shape[3]
    R = HG * tq
    qpos = qt * tq + lax.broadcasted_iota(jnp.int32, (tq, 1), 0)
    slope_col = jnp.concatenate([jnp.full((tq, 1), slopes_ref[g * HG + h], F32) for h in range(HG)], axis=0)
    tile_heads = lambda x: jnp.concatenate([x] * HG, axis=0)
    q2 = _alibi_lhs(_stack_heads(q_ref[0] * (HEAD_B ** -0.5 * LOG2E), HG), slope_col)

    selm_b = selm_ref[0, 0].astype(BF16)
    nseg = (qt * tq + tq + SEG - 1) // SEG

    def seg_scores(si, m):
        k0 = pl.multiple_of(si * SEG, SEG)
        kcol = k0 + lax.broadcasted_iota(jnp.int32, (NSBp, SEG), 1)
        expand = (jnp.right_shift(kcol, int(math.log2(L_SEL)))
                  == lax.broadcasted_iota(jnp.int32, (NSBp, SEG), 0)).astype(BF16)
        in_blk = jnp.dot(selm_b, expand, preferred_element_type=F32) > 0.5
        kpos = k0 + lax.broadcasted_iota(jnp.int32, (1, SEG), 1)
        mask = tile_heads(in_blk & (kpos <= qpos))
        s = lax.dot_general(q2, _alibi_rhs(ks_ref[0, pl.ds(k0, SEG), :], k0), NT_DIMS, preferred_element_type=F32)
        s = jnp.where(mask, s, NEG)
        s_scr[si] = s
        return jnp.maximum(m, jnp.max(s, axis=-1, keepdims=True))

    m = lax.fori_loop(0, nseg, seg_scores, jnp.full((R, 1), NEG, F32))

    def seg_pv(si, acc):
        k0 = pl.multiple_of(si * SEG, SEG)
        e = jnp.exp2(s_scr[si] - m).astype(BF16)
        return acc + jnp.dot(e, _with_ones(vs_ref[0, pl.ds(k0, SEG), :]), preferred_element_type=F32)

    acc = lax.fori_loop(0, nseg, seg_pv, jnp.zeros((R, 2 * HEAD_B), F32))
    o = acc[:, :HEAD_B] / acc[:, HEAD_B:HEAD_B + 1]

    start = pl.multiple_of(jnp.clip(qt * tq - WINDOW, 0, T - WS), SUBLANES)
    distw = qpos - (start + lax.broadcasted_iota(jnp.int32, (1, WS), 1))
    maskw = tile_heads((distw >= 0) & (distw < WINDOW))
    sw = lax.dot_general(q2, _alibi_rhs(kw_ref[0, pl.ds(start, WS), :], start), NT_DIMS, preferred_element_type=F32)
    sw = jnp.where(maskw, sw, NEG)
    ew = jnp.exp2(sw - jnp.max(sw, axis=-1, keepdims=True)).astype(BF16)
    accw = jnp.dot(ew, _with_ones(vw_ref[0, pl.ds(start, WS), :]), preferred_element_type=F32)
    ow = accw[:, :HEAD_B] / accw[:, HEAD_B:HEAD_B + 1]

    HB = G_KV * HG
    gates = _sigmoid(gate_ref[0])
    lane = lax.broadcasted_iota(jnp.int32, gates.shape, 1)
    gate_col = lambda idx: jnp.sum(jnp.where(lane == idx, gates, 0.0), axis=1, keepdims=True)
    for h in range(HG):
        hs = slice(h * HEAD_B, (h + 1) * HEAD_B)
        rs = slice(h * tq, (h + 1) * tq)
        hd = g * HG + h
        y = (gate_col(hd) * ocmp_ref[0, :, hs] * _silu(zc_ref[0, :, hs])
             + gate_col(HB + hd) * o[rs] * _silu(zs_ref[0, :, hs])
             + gate_col(2 * HB + hd) * ow[rs] * _silu(zw_ref[0, :, hs]))
        o_ref[0, :, hs] = y.astype(o_ref.dtype)


def nsa_selwin_prompt(proj, rows, selm, o_cmp, slopes, *, tq):
    B, T, _ = proj.shape
    HG = slopes.shape[0] // G_KV
    NSBp = selm.shape[3]
    gw = HG * HEAD_B
    CB = G_KV * gw
    WS = min(T, WINDOW + tq)
    SEG = min(T, 512)
    assert T % SEG == 0
    kv_spec = lambda c: pl.BlockSpec((1, T, HEAD_B), lambda b, g, t: (b, 0, c * G_KV + g))
    head_spec = lambda blk: pl.BlockSpec((1, tq, gw), lambda b, g, t: (b, t, blk * G_KV + g))
    return pl.pallas_call(
        functools.partial(_nsa_selwin_prompt_kernel, tq=tq, HG=HG, T=T, WS=WS, SEG=SEG),
        grid=(B, G_KV, T // tq),
        in_specs=[pl.BlockSpec(memory_space=pltpu.SMEM),
                  head_spec(0),
                  pl.BlockSpec((1, 1, tq, NSBp), lambda b, g, t: (b, g, t, 0)),
                  kv_spec(2), kv_spec(3), kv_spec(4), kv_spec(5),
                  head_spec(0), head_spec(1), head_spec(2), head_spec(3),
                  pl.BlockSpec((1, tq, LANES), lambda b, g, t: (b, t, 4 * CB // LANES))],
        out_specs=head_spec(0),
        out_shape=jax.ShapeDtypeStruct((B, T, CB), BF16),
        scratch_shapes=[pltpu.VMEM((T // SEG, HG * tq, SEG), F32)],
        compiler_params=_cparams(("parallel", "parallel", "parallel")),
        name="nsa_selwin_prompt",
    )(slopes, proj, selm, rows, rows, rows, rows, o_cmp, proj, proj, proj, proj)


def _nsa_selwin_sample_kernel(pt_ref, slopes_ref, q_ref, selm_ref, selst_ref, *refs, PGS, PS, HG, TQ, pos0, n_new, n_win):
    del pt_ref
    page_refs = refs[:PGS]
    (new_ref, cwin_ref, ocmp_ref, zc_ref, zs_ref, zw_ref, gate_ref, o_ref,
     m_scr, l_scr, acc_scr) = refs[PGS:]
    st = pl.program_id(1)
    NSBp = selm_ref.shape[3]
    GW = G_KV * HEAD_B
    sel_shift = int(math.log2(L_SEL))

    @pl.when(st == 0)
    def _():
        m_scr[...] = jnp.full(m_scr.shape, NEG, F32)
        l_scr[...] = jnp.zeros(l_scr.shape, F32)
        acc_scr[...] = jnp.zeros(acc_scr.shape, F32)

    qpos = pos0 + lax.broadcasted_iota(jnp.int32, (TQ, 1), 0)
    lane_j = lax.broadcasted_iota(jnp.int32, (TQ, NSBp), 1)
    tile_heads = lambda x: jnp.concatenate([x] * HG, axis=0)

    def sel_col(selm_g, j):
        return jnp.sum(jnp.where(lane_j == j, selm_g, 0.0), axis=1, keepdims=True)

    def online_update(s, mask, v):
        gs = range(G_KV)
        m_old = [m_scr[g] for g in gs]
        m_new = [jnp.maximum(m_old[g], jnp.max(jnp.where(mask[g], s[g], NEG), axis=-1, keepdims=True)) for g in gs]
        e = [jnp.where(mask[g], jnp.exp(s[g] - m_new[g]), 0.0) for g in gs]
        alpha = [jnp.exp(m_old[g] - m_new[g]) for g in gs]
        pv = [_bdot(e[g], v[g]) for g in gs]
        for g in gs:
            l_scr[g] = alpha[g] * l_scr[g] + jnp.sum(e[g], axis=-1, keepdims=True)
            acc_scr[g] = alpha[g] * acc_scr[g] + pv[g]
            m_scr[g] = m_new[g]

    NK = PGS * PS
    kpos = st * NK + lax.broadcasted_iota(jnp.int32, (1, NK), 1)
    expand = (jnp.right_shift(lax.broadcasted_iota(jnp.int32, (LANES, NK), 1), sel_shift)
              == lax.broadcasted_iota(jnp.int32, (LANES, NK), 0)).astype(BF16)
    in_blk_all = jnp.dot(selst_ref[0, 0].astype(BF16), expand, preferred_element_type=F32)
    dist = qpos - kpos
    distf = tile_heads(dist.astype(F32))
    gs = range(G_KV)
    q_st = [_stack_heads(q_ref[0, :, g * HG * HEAD_B:(g + 1) * HG * HEAD_B] * (HEAD_B ** -0.5), HG).astype(BF16)
            for g in gs]
    slope_col = [jnp.concatenate([jnp.full((TQ, 1), slopes_ref[g * HG + h], F32) for h in range(HG)], axis=0)
                 for g in gs]
    page_rows = lambda i, cg: page_refs[i][0, :, :, cg, :].reshape(PS, HEAD_B)
    k = [jnp.concatenate([page_rows(i, g) for i in range(PGS)], axis=0) for g in gs]
    v = [jnp.concatenate([page_rows(i, G_KV + g) for i in range(PGS)], axis=0) for g in gs]
    mask = [tile_heads((in_blk_all[g * TQ:(g + 1) * TQ] > 0.5) & (dist >= 0)) for g in gs]
    s = [_bdot_nt(q_st[g], k[g]) - slope_col[g] * distf for g in gs]
    online_update(s, mask, v)

    @pl.when(st == pl.num_programs(1) - 1)
    def _():
        NN = new_ref.shape[1]
        rnew = lax.broadcasted_iota(jnp.int32, (1, NN), 1)
        kpos_n = pos0 + rnew
        dist_n = qpos - kpos_n
        ok_n = (rnew < n_new) & (dist_n >= 0)
        jn = pos0 >> sel_shift
        kpos_w = pos0 - n_win + lax.broadcasted_iota(jnp.int32, (1, n_win), 1)
        dist_w = qpos - kpos_w
        dist_wall = jnp.concatenate([dist_w, dist_n], axis=1)
        mask_wall = jnp.concatenate([(dist_w >= 0) & (dist_w < WINDOW), ok_n & (dist_n < WINDOW)], axis=1)
        kn = [new_ref[0, :, 2 * GW + g * HEAD_B:2 * GW + (g + 1) * HEAD_B] for g in gs]
        vn = [new_ref[0, :, 3 * GW + g * HEAD_B:3 * GW + (g + 1) * HEAD_B] for g in gs]
        mask_n = [tile_heads((sel_col(selm_ref[0, g], jn) > 0.5) & ok_n) for g in gs]
        dist_nf = tile_heads(dist_n.astype(F32))
        s_n = [_bdot_nt(q_st[g], kn[g]) - slope_col[g] * dist_nf for g in gs]
        online_update(s_n, mask_n, vn)
        kw = [jnp.concatenate([cwin_ref[0, :, g * HEAD_B:(g + 1) * HEAD_B],
                               new_ref[0, :, 4 * GW + g * HEAD_B:4 * GW + (g + 1) * HEAD_B]], axis=0) for g in gs]
        vw = [jnp.concatenate([cwin_ref[0, :, GW + g * HEAD_B:GW + (g + 1) * HEAD_B],
                               new_ref[0, :, 5 * GW + g * HEAD_B:5 * GW + (g + 1) * HEAD_B]], axis=0) for g in gs]
        dist_wf = tile_heads(dist_wall.astype(F32))
        mask_w = tile_heads(mask_wall)
        sw = [_bdot_nt(q_st[g], kw[g]) - slope_col[g] * dist_wf for g in gs]
        pw = [_masked_softmax_rows(sw[g], mask_w) for g in gs]
        ow = [_bdot(pw[g], vw[g]) for g in gs]
        HB = G_KV * HG
        gates = _sigmoid(gate_ref[0])
        for g in gs:
            l = l_scr[g]
            o = acc_scr[g] / jnp.where(l > 0.0, l, 1.0)
            for h in range(HG):
                hd = g * HG + h
                hs = slice(hd * HEAD_B, (hd + 1) * HEAD_B)
                rs = slice(h * TQ, (h + 1) * TQ)
                y = (gates[:, hd:hd + 1] * ocmp_ref[0, :, hs] * _silu(zc_ref[0, :, hs])
                     + gates[:, HB + hd:HB + hd + 1] * o[rs] * _silu(zs_ref[0, :, hs])
                     + gates[:, 2 * HB + hd:2 * HB + hd + 1] * ow[g][rs] * _silu(zw_ref[0, :, hs]))
                o_ref[0, :, hs] = y.astype(o_ref.dtype)


def nsa_selwin_sample(proj, selm, o_cmp, pool5, table, new_rows, cwin, slopes, *, pos0, n_new):
    B, TQ, _ = proj.shape
    HG = slopes.shape[0] // G_KV
    NP, CPP = pool5.shape[:2]
    PS = CPP * S_CMP
    n_pages = table.shape[1]
    NSBp = selm.shape[3]
    PGS = max(d for d in (8, 4, 2, 1) if n_pages % d == 0)
    CB = G_KV * HG * HEAD_B
    GW = G_KV * HEAD_B
    NN = new_rows.shape[1]
    n_win = cwin.shape[1]
    assert pos0 % L_SEL == 0 and n_new <= L_SEL and pos0 == n_pages * PS

    def page_map(i):
        return lambda b, s, pt: (pt[b, s * PGS + i], 0, 0, 1, 0)

    n_steps = n_pages // PGS
    bps = PGS * PS // L_SEL
    assert bps <= LANES
    selst = selm[:, :, :, :n_steps * bps].reshape(B, G_KV, TQ, n_steps, bps).transpose(0, 3, 1, 2, 4)
    selst = jnp.pad(selst.reshape(B, n_steps, G_KV * TQ, bps), ((0, 0), (0, 0), (0, 0), (0, LANES - bps)))

    const = lambda b, s, pt: (b, 0, 0)
    wide = lambda blk: pl.BlockSpec((1, TQ, CB), lambda b, s, pt: (b, 0, blk))
    grid_spec = pltpu.PrefetchScalarGridSpec(
        num_scalar_prefetch=1,
        grid=(B, n_steps),
        in_specs=[pl.BlockSpec(memory_space=pltpu.SMEM),
                  pl.BlockSpec((1, TQ, CB), const),
                  pl.BlockSpec((1, G_KV, TQ, NSBp), lambda b, s, pt: (b, 0, 0, 0)),
                  pl.BlockSpec((1, 1, G_KV * TQ, LANES), lambda b, s, pt: (b, s, 0, 0))]
                 + [pl.BlockSpec((1, CPP, S_CMP, 2 * G_KV, HEAD_B), page_map(i)) for i in range(PGS)]
                 + [pl.BlockSpec((1, NN, 6 * GW), const),
                    pl.BlockSpec((1, n_win, 2 * GW), const),
                    wide(0), wide(1), wide(2), wide(3),
                    pl.BlockSpec((1, TQ, LANES), lambda b, s, pt: (b, 0, 4 * CB // LANES))],
        out_specs=wide(0),
        scratch_shapes=[pltpu.VMEM((G_KV, HG * TQ, 1), F32),
                        pltpu.VMEM((G_KV, HG * TQ, 1), F32),
                        pltpu.VMEM((G_KV, HG * TQ, HEAD_B), F32)],
    )
    return pl.pallas_call(
        functools.partial(_nsa_selwin_sample_kernel, PGS=PGS, PS=PS, HG=HG, TQ=TQ, pos0=pos0,
                          n_new=n_new, n_win=n_win),
        grid_spec=grid_spec,
        out_shape=jax.ShapeDtypeStruct((B, TQ, CB), F32),
        compiler_params=_cparams(("parallel", "arbitrary")),
        name="nsa_selwin_sample",
    )(table, slopes, proj, selm, selst, *([pool5] * PGS), new_rows, cwin, o_cmp, proj, proj, proj, proj)


def _rwkv_layer(h, x_prev, s0, i, W, B, T):
    N, D = h.shape
    g = W["norm_g"][i]
    rkvg = rwkv_in(h, x_prev, g, W["mu_a"][i], W["w_in_a"], i, T)
    CA = rkvg.shape[-1]
    lw, a = rwkv_lora(h, x_prev, g, W["mu_a"][i], W["w_lora_w1"][i], W["w_lora_w2"][i], W["a_lora1"][i],
                      W["a_lora2"][i], W["w0_a"][i], W["a0_a"][i], T)
    pvec = jnp.stack([W["k_k"][i], W["k_a"][i], W["r_k"][i].reshape(CA), W["ln_x_w"][i], W["ln_x_b"][i]])
    o, s_fin = rwkv_scan(rkvg.reshape(4, B, T, CA), lw.reshape(B, T, CA), a.reshape(B, T, CA), pvec, s0)
    last = rmsnorm(h.reshape(B, T, D)[:, -1], g)
    return o.reshape(N, CA), s_fin, last


def _nsa_layer(h, jb, shared, W, slopes, B, T, norm_g):
    N, D = h.shape
    CB = W["w_out_b"].shape[1]
    proj3 = norm_mm(h, norm_g, W["w_in_b"], (jb,)).reshape(B, T, -1)
    if shared["past"] is None:
        o_cmp, selm = nsa_cmp(proj3, shared["kvc"], slopes, tq=min(T, 256), nc=shared["nc"],
                              nsb=shared["nsb"], pos0=0)
        o = nsa_selwin_prompt(proj3, shared["rows"], selm, o_cmp, slopes, tq=min(T, 256))
    else:
        TQ = SUBLANES
        projp = jnp.pad(proj3, ((0, 0), (0, TQ - T), (0, 0)))
        o_cmp, selm = nsa_cmp(projp, shared["kvc"], slopes, tq=TQ, nc=shared["nc"],
                              nsb=shared["nsb"], pos0=shared["pos0"])
        pool, table, cwin = shared["past"]
        o = nsa_selwin_sample(projp, selm, o_cmp, pool, table, shared["new_rows"], cwin, slopes,
                              pos0=shared["pos0"], n_new=T)[:, :T]
    return o.reshape(N, CB)


def _trunk(x, p, pos0, wkv0, shift0, past, W, slopes):
    B, T, D = x.shape
    N = B * T
    depth = p.shape[0]
    n_a = W["w_in_a"].shape[0]
    GW = G_KV * HEAD_B
    h = x.reshape(N, D)
    wkv_new, shift_new = [], []
    shared, kv_rows, win_state = None, None, None
    for i in range(depth):
        if i < n_a:
            o, s_fin, last = _rwkv_layer(h, shift0[i], wkv0[i], i, W, B, T)
            wkv_new.append(s_fin)
            shift_new.append(last)
            h = out_ple(o, h, p[i].reshape(N, -1), W["w_out_a"], i, W["w_ple"], W["w_ple_gate"], i)
        else:
            o = _nsa_layer(h, i - n_a, shared, W, slopes, B, T, W["norm_g"][i])
            h = out_ple(o, h, p[i].reshape(N, -1), W["w_out_b"], i - n_a, W["w_ple"], W["w_ple_gate"], i)
        if i == n_a - 1:
            rows = norm_mm(h, W["kv_norm_g"], W["w_kv"]).reshape(B, T, 6 * GW)
            kv_rows = rows[:, :, :4 * GW].reshape(B, T, 4, G_KV, HEAD_B)
            win_new = rows[:, :, 4 * GW:].reshape(B, T, 2, G_KV, HEAD_B)
            if past is None:
                PS = 128
                pool = rows.reshape(B * T // PS, PS // S_CMP, S_CMP, 6 * G_KV, HEAD_B)
                table = jnp.arange(B * T // PS, dtype=jnp.int32).reshape(B, T // PS)
                t_all = T
                win_all = win_new
                shared = {"past": None, "rows": rows}
            else:
                pool, table, cwin = past
                PS = pool.shape[1] * S_CMP
                t_all = pos0 + T
                win_all = jnp.concatenate([cwin.reshape(B, -1, 2, G_KV, HEAD_B), win_new], axis=1)
                NN = LANES
                shared = {"past": past, "new_rows": jnp.pad(rows, ((0, 0), (0, NN - T), (0, 0)))}
            win_state = win_all[:, win_all.shape[1] - min(WINDOW, pos0 + T):]
            nc = (t_all - L_CMP) // S_CMP + 1
            assert nc < table.shape[1] * PS // S_CMP
            kvc = compress_kv(pool, table, W["pe_cmp"], W["w_cmp1"], W["w_cmp2"])
            shared.update(kvc=kvc, nc=nc, nsb=max(-(-t_all // L_SEL), TOPK_SEL), pos0=pos0)
    y = rmsnorm(h, W["final_norm_g"]).reshape(B, T, D)
    return y, jnp.stack(wkv_new), jnp.stack(shift_new), kv_rows, win_state


def kernel(x_prompt, x_sample, state_wkv, state_shift, cache_kv, cache_win_kv, page_table, p_prompt, p_sample, norm_g, mu_a, w_in_a, w_lora_w1, w_lora_w2, w0_a, a_lora1, a_lora2, a0_a, k_k, k_a, r_k, ln_x_w, ln_x_b, w_out_a, w_in_b, w_out_b, kv_norm_g, w_kv, pe_cmp, w_cmp1, w_cmp2, w_ple, w_ple_gate, final_norm_g):
    bf = lambda w: w.astype(BF16)
    CA = w_out_a.shape[1]
    W = dict(norm_g=norm_g, mu_a=mu_a, w_in_a=bf(w_in_a), w_lora_w1=bf(w_lora_w1), w_lora_w2=bf(w_lora_w2),
             w0_a=w0_a, a_lora1=bf(a_lora1), a_lora2=bf(a_lora2), a0_a=a0_a, k_k=k_k, k_a=k_a,
             r_k=r_k, ln_x_w=ln_x_w, ln_x_b=ln_x_b,
             w_out_a=bf(w_out_a), w_in_b=bf(w_in_b), w_out_b=bf(w_out_b), kv_norm_g=kv_norm_g, w_kv=bf(w_kv),
             pe_cmp=pe_cmp, w_cmp1=bf(w_cmp1), w_cmp2=bf(w_cmp2), w_ple=bf(w_ple), w_ple_gate=bf(w_ple_gate),
             final_norm_g=final_norm_g)
    HB = w_out_b.shape[1] // HEAD_B
    slopes = 2.0 ** (-8.0 * jnp.arange(1, HB + 1, dtype=F32) / HB)
    bp = x_prompt.shape[0]
    n_a = w_in_a.shape[0]
    D = x_prompt.shape[-1]
    wkv0 = jnp.zeros((n_a, bp, CA // HEAD_A, HEAD_A, HEAD_A), F32)
    shift0 = jnp.zeros((n_a, bp, D), F32)
    y_p, wkv_p, shift_p, kv_p, win_p = _trunk(x_prompt, p_prompt, 0, wkv0, shift0, None, W, slopes)
    db, n_pages = page_table.shape
    NP, PS = cache_kv.shape[:2]
    pool5 = cache_kv.reshape(NP, PS // S_CMP, S_CMP, -1, HEAD_B)
    past = (pool5, page_table, cache_win_kv.reshape(db, cache_win_kv.shape[1], -1))
    y_s, wkv_s, shift_s, kv_s, win_s = _trunk(x_sample, p_sample, n_pages * PS, state_wkv, state_shift, past, W, slopes)
    return (y_p, y_s, wkv_p, shift_p, kv_p, win_p, wkv_s, shift_s, kv_s, win_s)
```

```python
import functools
import math

import jax
import jax.numpy as jnp
from jax import lax
from jax.experimental import pallas as pl
from jax.experimental.pallas import tpu as pltpu

F32 = jnp.float32
BF16 = jnp.bfloat16

HEAD_A = 64
GN_EPS = 64e-5
HEAD_B = 128
G_KV = 4
L_CMP = 32
S_CMP = 16
L_SEL = 64
TOPK_SEL = 16
WINDOW = 512
RMS_EPS = 1e-6
NEG = -1e30
FORCE_BONUS = 1e4

LANES = 128
SUBLANES = 8
VMEM_LIMIT = 56 * 1024 * 1024

SCAN_NH = 2
SCAN_C = 64
SCAN_GP = 16

NT_DIMS = (((1,), (1,)), ((), ()))
TN_DIMS = (((0,), (0,)), ((), ()))


def _cparams(sem):
    return pltpu.CompilerParams(dimension_semantics=sem, vmem_limit_bytes=VMEM_LIMIT)


def _bdot(a, b):
    return jnp.dot(a.astype(BF16), b.astype(BF16), preferred_element_type=F32)


def _bdot_nt(a, b):
    return lax.dot_general(a.astype(BF16), b.astype(BF16), NT_DIMS, preferred_element_type=F32)


def _bdot_tn(a, b):
    return lax.dot_general(a.astype(BF16), b.astype(BF16), TN_DIMS, preferred_element_type=F32)


def _rms_kernel(x_ref, g_ref, o_ref):
    x = x_ref[...]
    ms = jnp.mean(x * x, axis=-1, keepdims=True)
    o_ref[...] = x * lax.rsqrt(ms + RMS_EPS) * g_ref[...]


def rmsnorm(x, g):
    M, D = x.shape
    tm = min(M, 256)
    return pl.pallas_call(
        _rms_kernel,
        grid=(pl.cdiv(M, tm),),
        in_specs=[pl.BlockSpec((tm, D), lambda i: (i, 0)),
                  pl.BlockSpec((1, D), lambda i: (0, 0))],
        out_specs=pl.BlockSpec((tm, D), lambda i: (i, 0)),
        out_shape=jax.ShapeDtypeStruct((M, D), F32),
        compiler_params=_cparams(("parallel",)),
        name="rmsnorm",
    )(x, g.reshape(1, D))


def _mm_kernel(x_ref, w_ref, o_ref):
    o_ref[...] = jnp.dot(x_ref[...].astype(BF16), w_ref[...], preferred_element_type=F32)


def mm(x, w, widx=()):
    M, K = x.shape
    N = w.shape[-1]
    assert w.shape[-2] == K and len(widx) == w.ndim - 2
    tm = min(M, 1024 if K <= 2048 else 512)
    tn = N if N <= 512 else 512
    nlead = len(widx)
    w_spec = pl.BlockSpec((None,) * nlead + (K, tn), lambda i, j: tuple(widx) + (0, j))
    return pl.pallas_call(
        _mm_kernel,
        grid=(pl.cdiv(M, tm), pl.cdiv(N, tn)),
        in_specs=[pl.BlockSpec((tm, K), lambda i, j: (i, 0)), w_spec],
        out_specs=pl.BlockSpec((tm, tn), lambda i, j: (i, j)),
        out_shape=jax.ShapeDtypeStruct((M, N), F32),
        compiler_params=_cparams(("parallel", "parallel")),
        name="mm",
    )(x, w)


def _norm_rows(x, g):
    return x * lax.rsqrt(jnp.mean(x * x, axis=-1, keepdims=True) + RMS_EPS) * g


def _sigmoid(x):
    return 0.5 + 0.5 * jnp.tanh(0.5 * x)


def _norm_and_shift(h_ref, hprev_ref, xprev_ref, g_ref, i, tm, T):
    g = g_ref[...]
    hn = _norm_rows(h_ref[...], g)
    prev_row = _norm_rows(hprev_ref[SUBLANES - 1:SUBLANES, :], g)
    row = lax.broadcasted_iota(jnp.int32, (tm, 1), 0)
    xs = jnp.where(row == 0, prev_row, pltpu.roll(hn, 1, axis=0))
    if T >= tm:
        assert T % tm == 0
        start = (i * tm) % T == 0
        xs = jnp.where((row == 0) & start, xprev_ref[pl.ds((i * tm) // T, 1), :], xs)
    else:
        assert tm % T == 0
        for bb in range(tm // T):
            xs = jnp.where(row == bb * T, xprev_ref[pl.ds(i * (tm // T) + bb, 1), :], xs)
    return hn, xs


def _rwkv_in_kernel(h_ref, hprev_ref, xprev_ref, g_ref, mu_ref, w_ref, o_ref, xs_scr, *, tm, T):
    i, j, n = pl.program_id(0), pl.program_id(1), pl.program_id(2)

    @pl.when(n == 0)
    def _():
        hn, xs = _norm_and_shift(h_ref, hprev_ref, xprev_ref, g_ref, i, tm, T)
        xs_scr[...] = (hn + (xs - hn) * mu_ref[pl.ds(j, 1), :]).astype(BF16)

    o_ref[...] = jnp.dot(xs_scr[...], w_ref[...], preferred_element_type=F32)


def _shift_specs(tm, D, nb, ngrid):
    z = (0,) * (ngrid - 1)
    wrap = lambda f: (lambda i, *_: f(i))
    return [pl.BlockSpec((tm, D), wrap(lambda i: (i, 0))),
            pl.BlockSpec((SUBLANES, D), wrap(lambda i: (jnp.maximum(i * (tm // SUBLANES) - 1, 0), 0))),
            pl.BlockSpec((nb, D), wrap(lambda i: (0, 0))),
            pl.BlockSpec((1, D), wrap(lambda i: (0, 0))),
            pl.BlockSpec((6, D), wrap(lambda i: (0, 0)))]


def rwkv_in(h, xprev, g, mu, w, layer, T):
    N, D = h.shape
    C = w.shape[-1]
    tm = min(N, 1024)
    tn = 512
    return pl.pallas_call(
        functools.partial(_rwkv_in_kernel, tm=tm, T=T),
        grid=(N // tm, 4, C // tn),
        in_specs=_shift_specs(tm, D, xprev.shape[0], 3) + [
            pl.BlockSpec((None, None, D, tn), lambda i, j, n: (layer, j, 0, n))],
        out_specs=pl.BlockSpec((None, tm, tn), lambda i, j, n: (j, i, n)),
        out_shape=jax.ShapeDtypeStruct((4, N, C), F32),
        scratch_shapes=[pltpu.VMEM((tm, D), BF16)],
        compiler_params=_cparams(("parallel", "arbitrary", "arbitrary")),
        name="rwkv_in",
    )(h, h, xprev, g.reshape(1, D), mu, w)


def _rwkv_lora_kernel(h_ref, hprev_ref, xprev_ref, g_ref, mu_ref, lw1_ref, lw2_ref, la1_ref, la2_ref,
                      w0_ref, a0_ref, lw_ref, a_ref, *, tm, T):
    hn, xs = _norm_and_shift(h_ref, hprev_ref, xprev_ref, g_ref, pl.program_id(0), tm, T)
    dx = xs - hn
    x4 = (hn + dx * mu_ref[4:5, :]).astype(BF16)
    x5 = (hn + dx * mu_ref[5:6, :]).astype(BF16)
    t4 = jnp.tanh(jnp.dot(x4, lw1_ref[...], preferred_element_type=F32)).astype(BF16)
    x = w0_ref[...] + jnp.dot(t4, lw2_ref[...], preferred_element_type=F32)
    lw_ref[...] = -math.exp(-0.5) * _sigmoid(x)
    t5 = jnp.dot(x5, la1_ref[...], preferred_element_type=F32).astype(BF16)
    a_ref[...] = _sigmoid(a0_ref[...] + jnp.dot(t5, la2_ref[...], preferred_element_type=F32))


def rwkv_lora(h, xprev, g, mu, lw1, lw2, la1, la2, w0, a0, T):
    N, D = h.shape
    R, C = lw2.shape
    tm = min(N, 256)
    full = lambda shape: pl.BlockSpec(shape, lambda i: (0,) * len(shape))
    o_spec = pl.BlockSpec((tm, C), lambda i: (i, 0))
    return pl.pallas_call(
        functools.partial(_rwkv_lora_kernel, tm=tm, T=T),
        grid=(N // tm,),
        in_specs=_shift_specs(tm, D, xprev.shape[0], 1) + [
            full((D, R)), full((R, C)), full((D, R)), full((R, C)), full((1, C)), full((1, C))],
        out_specs=[o_spec, o_spec],
        out_shape=[jax.ShapeDtypeStruct((N, C), F32)] * 2,
        compiler_params=_cparams(("parallel",)),
        name="rwkv_lora",
    )(h, h, xprev, g.reshape(1, D), mu, lw1, lw2, la1, la2, w0.reshape(1, C), a0.reshape(1, C))


def _norm_mm_kernel(h_ref, g_ref, w_ref, o_ref, xs_scr):
    @pl.when(pl.program_id(1) == 0)
    def _():
        rc = min(h_ref.shape[0], 256)

        def norm_chunk(c, carry):
            rows = pl.ds(pl.multiple_of(c * rc, rc), rc)
            xs_scr[rows, :] = _norm_rows(h_ref[rows, :], g_ref[...]).astype(BF16)
            return carry

        lax.fori_loop(0, h_ref.shape[0] // rc, norm_chunk, 0)

    o_ref[...] = jnp.dot(xs_scr[...], w_ref[...], preferred_element_type=F32)


def norm_mm(h, g, w, widx=()):
    N, D = h.shape
    NO = w.shape[-1]
    tm = min(N, 1024)
    tn = 1024
    nlead = len(widx)
    return pl.pallas_call(
        _norm_mm_kernel,
        grid=(N // tm, pl.cdiv(NO, tn)),
        in_specs=[pl.BlockSpec((tm, D), lambda i, n: (i, 0)),
                  pl.BlockSpec((1, D), lambda i, n: (0, 0)),
                  pl.BlockSpec((None,) * nlead + (D, tn), lambda i, n: tuple(widx) + (0, n))],
        out_specs=pl.BlockSpec((tm, tn), lambda i, n: (i, n)),
        out_shape=jax.ShapeDtypeStruct((N, NO), F32),
        scratch_shapes=[pltpu.VMEM((tm, D), BF16)],
        compiler_params=_cparams(("parallel", "arbitrary")),
        name="norm_mm",
    )(h, g.reshape(1, D), w)


def _mm_res_kernel(x_ref, h_ref, w_ref, o_ref):
    o_ref[...] = h_ref[...] + jnp.dot(x_ref[...].astype(BF16), w_ref[...], preferred_element_type=F32)


def _ple_gate_kernel(h1_ref, h1t_ref, p_ref, wp_ref, wg_ref, o_ref, xb_scr):
    @pl.when(pl.program_id(1) == 0)
    def _():
        rc = min(h1_ref.shape[0], 256)

        def cast_chunk(c, carry):
            rows = pl.ds(pl.multiple_of(c * rc, rc), rc)
            xb_scr[rows, :] = h1_ref[rows, :].astype(BF16)
            return carry

        lax.fori_loop(0, h1_ref.shape[0] // rc, cast_chunk, 0)

    gate = jnp.dot(xb_scr[...], wg_ref[...], preferred_element_type=F32)
    ple = jnp.dot(p_ref[...].astype(BF16), wp_ref[...], preferred_element_type=F32)
    o_ref[...] = h1t_ref[...] + ple * _sigmoid(gate)


def out_ple(x, h, p, w_out, oidx, w_ple, w_gate, layer):
    N, C = x.shape
    D = h.shape[1]
    DP = p.shape[1]
    tm = min(N, 1024)
    tn = 512
    grid = (N // tm, D // tn)
    tile = pl.BlockSpec((tm, tn), lambda i, n: (i, n))
    h1 = pl.pallas_call(
        _mm_res_kernel,
        grid=grid,
        in_specs=[pl.BlockSpec((tm, C), lambda i, n: (i, 0)), tile,
                  pl.BlockSpec((None, C, tn), lambda i, n: (oidx, 0, n))],
        out_specs=tile,
        out_shape=jax.ShapeDtypeStruct((N, D), F32),
        compiler_params=_cparams(("parallel", "parallel")),
        name="mm_res",
    )(x, h, w_out)
    return pl.pallas_call(
        _ple_gate_kernel,
        grid=grid,
        in_specs=[pl.BlockSpec((tm, D), lambda i, n: (i, 0)), tile,
                  pl.BlockSpec((tm, DP), lambda i, n: (i, 0)),
                  pl.BlockSpec((None, DP, tn), lambda i, n: (layer, 0, n)),
                  pl.BlockSpec((None, D, tn), lambda i, n: (layer, 0, n))],
        out_specs=tile,
        out_shape=jax.ShapeDtypeStruct((N, D), F32),
        scratch_shapes=[pltpu.VMEM((tm, D), BF16)],
        compiler_params=_cparams(("parallel", "arbitrary")),
        name="ple_gate",
    )(h1, h1, p, w_ple, w_gate)


def _scan_kernel(r_ref, k_ref, v_ref, zg_ref, lw_ref, a_ref, pv_ref, s0_ref, o_ref, sfin_ref, s_scr, *, C, NH, GP):
    L = NH * HEAD_A
    NC = NH * C
    ci = pl.program_id(2)

    @pl.when(ci == 0)
    def _():
        s_scr[...] = s0_ref[0]

    row_c = lax.broadcasted_iota(jnp.int32, (C, NC), 0)
    col_s = lax.broadcasted_iota(jnp.int32, (C, NC), 1) % C
    tri_strict = col_s < row_c
    tri_incl = (lax.broadcasted_iota(jnp.int32, (C, 2 * NC), 1) % C
                <= lax.broadcasted_iota(jnp.int32, (C, 2 * NC), 0))
    st_mask = (lax.broadcasted_iota(jnp.int32, (NC, L), 0) // C
               == lax.broadcasted_iota(jnp.int32, (NC, L), 1) // HEAD_A)
    bd_mask = (lax.broadcasted_iota(jnp.int32, (NC, NC), 0) // C
               == lax.broadcasted_iota(jnp.int32, (NC, NC), 1) // C)
    head_mask = (lax.broadcasted_iota(jnp.int32, (L, L), 0) // HEAD_A
                 == lax.broadcasted_iota(jnp.int32, (L, L), 1) // HEAD_A)

    def st(x):
        return jnp.where(st_mask, jnp.concatenate([x] * NH, axis=0), 0.0)

    def bd(w):
        return jnp.where(bd_mask, jnp.concatenate([w] * NH, axis=0), 0.0)

    n_double = int(math.log2(C))
    each = lambda f, *cols: [f(*xs) for xs in zip(*cols)]
    sls = [slice(gp * L, (gp + 1) * L) for gp in range(GP)]
    head_of_lane = lax.broadcasted_iota(jnp.int32, (1, L), 1) // HEAD_A

    def hsum(x):
        out = None
        for hh in range(NH):
            sh = jnp.sum(jnp.where(head_of_lane == hh, x, 0.0), axis=-1, keepdims=True)
            out = sh if out is None else jnp.where(head_of_lane == hh, sh, out)
        return out

    k_k, k_a, r_k, ln_w, ln_b = ([pv_ref[n:n + 1, sl] for sl in sls] for n in range(5))
    lw = [lw_ref[0, :, sl] for sl in sls]
    a_sig = [a_ref[0, :, sl] for sl in sls]
    r = [r_ref[0, :, sl] for sl in sls]
    k_raw = [k_ref[0, :, sl] for sl in sls]
    v = [v_ref[0, :, sl] for sl in sls]
    kk = each(lambda x, w: x * w, k_raw, k_k)
    kk = each(lambda x: x / jnp.maximum(jnp.sqrt(hsum(x * x)), 1e-12), kk)
    k = each(lambda x, a, w: x * (1.0 + (a - 1.0) * w), k_raw, a_sig, k_a)
    row_t = lax.broadcasted_iota(jnp.int32, (C, 1), 0)
    cum = lw
    for sh in (1 << e for e in range(n_double)):
        cum = each(lambda z: z + jnp.where(row_t >= sh, pltpu.roll(z, sh, axis=0), 0.0), cum)
    p_incl = each(jnp.exp, cum)
    p_inv = each(lambda z: jnp.exp(-z), cum)
    at = each(lambda x, c, w: -x * jnp.exp(c - w), kk, cum, lw)
    rt = each(lambda x, p: x * p, r, p_incl)
    bt = each(lambda x, a, p: x * a * p, kk, a_sig, p_inv)
    kt = each(lambda x, p: x * p, k, p_inv)
    S = [s_scr[gp] for gp in range(GP)]
    ar = each(lambda x, y: jnp.concatenate([x, y], axis=0), at, rt)
    bk_st = each(lambda x, y: jnp.concatenate([st(x), st(y)], axis=0), bt, kt)
    Gm = each(_bdot_nt, ar, bk_st)
    w_ab = [jnp.where(tri_strict, g[:C, :NC], 0.0) for g in Gm]
    tm = w_ab
    pw = each(lambda w: _bdot(w, bd(w)), w_ab)
    LH = each(_bdot_nt, ar, S)
    v_st = each(st, v)
    x = [lh[:C] + _bdot(jnp.where(tri_strict, g[:C, NC:], 0.0), vs) for lh, g, vs in zip(LH, Gm, v_st)]
    for it in range(1, n_double):
        if it < n_double - 1:
            both = each(lambda t, p: _bdot(p, jnp.concatenate([bd(t), bd(p)], axis=1)), tm, pw)
            tm = each(lambda t, p, b2: t + p + b2[:, :NC], tm, pw, both)
            pw = [b2[:, NC:] for b2 in both]
        else:
            tm = each(lambda t, p: t + p + _bdot(p, bd(t)), tm, pw)
    u = each(lambda xx, t: xx + _bdot(t, st(xx)), x, tm)
    o = [lh[C:] + _bdot(jnp.where(tri_incl, g[C:], 0.0), jnp.concatenate([st(uu), vs], axis=0))
         for lh, g, uu, vs in zip(LH, Gm, u, v_st)]
    inv_n = 1.0 / HEAD_A
    dev = each(lambda x: x - hsum(x) * inv_n, o)
    gn = each(lambda d, w, b_: d * lax.rsqrt(hsum(d * d) * inv_n + GN_EPS) * w + b_, dev, ln_w, ln_b)
    bonus = each(lambda rr, kx, w, vv: hsum(rr * kx * w) * vv, r, k, r_k, v)
    for sl, y, bo in zip(sls, gn, bonus):
        zg = zg_ref[0, :, sl]
        o_ref[0, :, sl] = ((y + bo) * (zg * _sigmoid(zg))).astype(o_ref.dtype)
    ds = [_bdot_tn(jnp.concatenate([uu, vv], axis=0), jnp.concatenate([b_, k_], axis=0))
          for uu, vv, b_, k_ in zip(u, v, bt, kt)]
    for gp in range(GP):
        s_scr[gp] = (S[gp] + jnp.where(head_mask, ds[gp], 0.0)) * p_incl[gp][C - 1:C, :]

    @pl.when(ci == pl.num_programs(2) - 1)
    def _():
        sfin_ref[0] = s_scr[...]


def rwkv_scan(rkvg, lw, a, pvec, s0):
    _, B, T, CA = rkvg.shape
    H = CA // HEAD_A
    NH, C, GP = SCAN_NH, SCAN_C, SCAN_GP
    assert NH * C == LANES and H % (NH * GP) == 0
    L = NH * HEAD_A
    NG = H // NH
    Tp = -(-T // C) * C
    if Tp != T:
        rkvg = jnp.pad(rkvg, ((0, 0), (0, 0), (0, Tp - T), (0, 0)))
        lw, a = (jnp.pad(z, ((0, 0), (0, Tp - T), (0, 0))) for z in (lw, a))
    eye = jnp.eye(NH, dtype=F32)
    s0_bd = (s0.reshape(B, NG, NH, HEAD_A, 1, HEAD_A) * eye[None, None, :, None, :, None]).reshape(B, NG, L, L)
    seq_spec = pl.BlockSpec((1, C, GP * L), lambda bi, gi, ci: (bi, ci, gi))
    proj_spec = lambda j: pl.BlockSpec((None, 1, C, GP * L), lambda bi, gi, ci: (j, bi, ci, gi))
    st_spec = pl.BlockSpec((1, GP, L, L), lambda bi, gi, ci: (bi, gi, 0, 0))
    o, sfin = pl.pallas_call(
        functools.partial(_scan_kernel, C=C, NH=NH, GP=GP),
        grid=(B, NG // GP, Tp // C),
        in_specs=[proj_spec(j) for j in range(4)] + [seq_spec, seq_spec,
                  pl.BlockSpec((5, GP * L), lambda bi, gi, ci: (0, gi)), st_spec],
        out_specs=[seq_spec, st_spec],
        out_shape=[jax.ShapeDtypeStruct((B, Tp, CA), BF16), jax.ShapeDtypeStruct((B, NG, L, L), F32)],
        scratch_shapes=[pltpu.VMEM((GP, L, L), F32)],
        compiler_params=_cparams(("parallel", "parallel", "arbitrary")),
        name="rwkv_scan",
    )(rkvg, rkvg, rkvg, rkvg, lw, a, pvec, s0_bd)
    sf = sfin.reshape(B, NG, NH, HEAD_A, NH, HEAD_A)
    s_fin = jnp.stack([sf[:, :, h, :, h, :] for h in range(NH)], axis=2).reshape(B, H, HEAD_A, HEAD_A)
    return o[:, :T], s_fin


def _gelu_tanh(x):
    c = math.sqrt(2.0 / math.pi)
    return 0.5 * x * (1.0 + jnp.tanh(c * (x + 0.044715 * (x * x * x))))


def _compress_kernel(pt_ref, *refs, PGS):
    del pt_ref
    page_refs = refs[:PGS]
    next_ref, pe_ref, w1_ref, w2_ref, out_ref = refs[PGS:]
    CPP = page_refs[0].shape[1]
    NCH = PGS * CPP
    CG = 2 * G_KV
    M = (NCH + 1) * CG

    def rows_of(l, hf):
        pe = pe_ref[hf, l]
        parts = [(page_refs[i][0, :, l] + pe[None]).reshape(CPP * CG, HEAD_B) for i in range(PGS)]
        parts.append(next_ref[0, 0, l] + pe)
        return jnp.concatenate(parts, axis=0)

    top = jnp.zeros((M, 2 * HEAD_B), F32)
    bot = jnp.zeros((M, 2 * HEAD_B), F32)
    for l in range(0, S_CMP, 2):
        wrows = pl.ds(l * HEAD_B, 2 * HEAD_B)
        xt = jnp.concatenate([rows_of(l, 0), rows_of(l + 1, 0)], axis=1).astype(BF16)
        top = top + jnp.dot(xt, w1_ref[0, wrows, :], preferred_element_type=F32)
        xb = jnp.concatenate([rows_of(l, 1), rows_of(l + 1, 1)], axis=1).astype(BF16)
        bot = bot + jnp.dot(xb, w1_ref[1, wrows, :], preferred_element_type=F32)
    is_k = (lax.broadcasted_iota(jnp.int32, (M, 1), 0) % CG) < G_KV
    pick = lambda z, n: jnp.where(is_k[:n], z[:n, :HEAD_B], z[:n, HEAD_B:])
    hcur = pick(top, NCH * CG) + pick(bot, M)[CG:]
    o2 = jnp.dot(_gelu_tanh(hcur).astype(BF16), w2_ref[...], preferred_element_type=F32)
    out_ref[0] = pick(o2, NCH * CG).reshape(NCH, CG, HEAD_B)


def compress_kv(pool5, table, pe_cmp, w1, w2):
    NP, CPP = pool5.shape[:2]
    B, n_pages = table.shape
    PGS = max(d for d in (8, 4, 2, 1) if n_pages % d == 0)
    NCH = PGS * CPP
    CG = 2 * G_KV
    half = S_CMP * HEAD_B
    pe_r = jnp.repeat(pe_cmp.reshape(2, 2, S_CMP, HEAD_B).transpose(1, 2, 0, 3), G_KV, axis=2)
    w1_r = w1.reshape(2, 2, half, HEAD_B).transpose(1, 2, 0, 3).reshape(2, half, 2 * HEAD_B)
    w2_r = jnp.concatenate([w2[0], w2[1]], axis=1)

    def page_map(i):
        return lambda b, s, pt: (pt[b, s * PGS + i], 0, 0, 0, 0)

    def next_map(b, s, pt):
        return (pt[b, jnp.minimum((s + 1) * PGS, n_pages - 1)], 0, 0, 0, 0)

    const = lambda n: (lambda b, s, pt: (0,) * n)
    grid_spec = pltpu.PrefetchScalarGridSpec(
        num_scalar_prefetch=1,
        grid=(B, n_pages // PGS),
        in_specs=[pl.BlockSpec((1, CPP, S_CMP, CG, HEAD_B), page_map(i)) for i in range(PGS)] + [
            pl.BlockSpec((1, 1, S_CMP, CG, HEAD_B), next_map),
            pl.BlockSpec((2, S_CMP, CG, HEAD_B), const(4)),
            pl.BlockSpec((2, half, 2 * HEAD_B), const(3)),
            pl.BlockSpec((HEAD_B, 2 * HEAD_B), const(2)),
        ],
        out_specs=pl.BlockSpec((1, NCH, CG, HEAD_B), lambda b, s, pt: (b, s, 0, 0)),
    )
    return pl.pallas_call(
        functools.partial(_compress_kernel, PGS=PGS),
        grid_spec=grid_spec,
        out_shape=jax.ShapeDtypeStruct((B, n_pages * CPP, CG, HEAD_B), F32),
        compiler_params=_cparams(("parallel", "arbitrary")),
        name="compress_kv",
    )(table, *([pool5] * PGS), pool5, pe_r, w1_r, w2_r)


def _stack_heads(q, HG):
    return jnp.concatenate([q[:, h * HEAD_B:(h + 1) * HEAD_B] for h in range(HG)], axis=0)


def _masked_softmax_rows(s, mask):
    s = jnp.where(mask, s, NEG)
    m = jnp.max(s, axis=-1, keepdims=True)
    e = jnp.where(mask, jnp.exp(s - m), 0.0)
    l = jnp.sum(e, axis=-1, keepdims=True)
    return e / jnp.where(l > 0.0, l, 1.0)


def _attend_stacked(s, dist, mask, v, slopes_ref, g, HG, tq):
    ps = []
    psum = jnp.zeros(dist.shape, F32)
    for h in range(HG):
        p = _masked_softmax_rows(s[h * tq:(h + 1) * tq] - slopes_ref[g * HG + h] * dist, mask)
        psum = psum + p
        ps.append(p.astype(BF16))
    o = jnp.dot(jnp.concatenate(ps, axis=0), v.astype(BF16), preferred_element_type=F32)
    return o, psum


def _unstack_store(o_ref, o, HG, tq):
    for h in range(HG):
        o_ref[0, :, h * HEAD_B:(h + 1) * HEAD_B] = o[h * tq:(h + 1) * tq]


def _nsa_cmp_kernel(slopes_ref, q_ref, kc_ref, vc_ref, o_ref, selm_ref, *, tq, HG, nc, nsb, pos0):
    g = pl.program_id(1)
    qt = pl.program_id(2)
    NCp = kc_ref.shape[1]
    NSBp = selm_ref.shape[3]
    q_st = _stack_heads(q_ref[0] * (HEAD_B ** -0.5), HG)
    s = _bdot_nt(q_st, kc_ref[0])
    qpos = pos0 + qt * tq + lax.broadcasted_iota(jnp.int32, (tq, 1), 0)
    cidx = lax.broadcasted_iota(jnp.int32, (1, NCp), 1)
    cend = S_CMP * cidx + (L_CMP - 1)
    mask = (cend <= qpos) & (cidx < nc)
    dist = (qpos - cend).astype(F32)
    o, imp_c = _attend_stacked(s, dist, mask, vc_ref[0], slopes_ref, g, HG, tq)
    _unstack_store(o_ref, o, HG, tq)

    crow = lax.broadcasted_iota(jnp.int32, (NCp, NSBp), 0)
    jcol = lax.broadcasted_iota(jnp.int32, (NCp, NSBp), 1)
    overlap = ((S_CMP * crow < L_SEL * (jcol + 1)) & (S_CMP * crow + L_CMP > L_SEL * jcol)
               & (crow < nc)).astype(F32)
    imp = jnp.dot(imp_c, overlap, precision=lax.Precision.HIGHEST, preferred_element_type=F32)
    lane = lax.broadcasted_iota(jnp.int32, (tq, NSBp), 1)
    cur = jnp.right_shift(qpos, int(math.log2(L_SEL)))
    forced = ((lane == 0) | (lane == cur) | (lane == cur - 1)).astype(F32)
    score = jnp.where(lane <= cur, imp + FORCE_BONUS * forced, NEG)
    score = jnp.where(lane < nsb, score, -3e38)

    if tq % LANES == 0 and NSBp == LANES:
        nr = -(-nsb // SUBLANES) * SUBLANES
        st = score.T[:nr]
        sub = lax.broadcasted_iota(jnp.int32, (nr, tq), 0)
        cnt = jnp.zeros((nr, tq), F32)
        for i in range(nsb):
            row = st[i:i + 1, :]
            beats = (row > st) | ((row == st) & (sub > i))
            cnt = cnt + jnp.where(beats, 1.0, 0.0)
        sel_t = jnp.where((cnt < TOPK_SEL) & (st > 0.5 * NEG), 1.0, 0.0)
        selm_ref[0, 0] = jnp.concatenate([sel_t, jnp.zeros((NSBp - nr, tq), F32)], axis=0).T
    else:
        cnt = jnp.zeros((tq, NSBp), F32)
        for i in range(nsb):
            col = score[:, i:i + 1]
            beats = (col > score) | ((col == score) & (lane > i))
            cnt = cnt + jnp.where(beats, 1.0, 0.0)
        sel = (cnt < TOPK_SEL) & (score > 0.5 * NEG)
        selm_ref[0, 0] = sel.astype(F32)


def nsa_cmp(proj, kvc, slopes, *, tq, nc, nsb, pos0):
    B, T, _ = proj.shape
    HG = slopes.shape[0] // G_KV
    NCp = kvc.shape[1]
    NSBp = -(-nsb // LANES) * LANES
    gw = HG * HEAD_B
    return pl.pallas_call(
        functools.partial(_nsa_cmp_kernel, tq=tq, HG=HG, nc=nc, nsb=nsb, pos0=pos0),
        grid=(B, G_KV, T // tq),
        in_specs=[pl.BlockSpec(memory_space=pltpu.SMEM),
                  pl.BlockSpec((1, tq, gw), lambda b, g, t: (b, t, g)),
                  pl.BlockSpec((1, NCp, HEAD_B), lambda b, g, t: (b, 0, g)),
                  pl.BlockSpec((1, NCp, HEAD_B), lambda b, g, t: (b, 0, G_KV + g))],
        out_specs=[pl.BlockSpec((1, tq, gw), lambda b, g, t: (b, t, g)),
                   pl.BlockSpec((1, 1, tq, NSBp), lambda b, g, t: (b, g, t, 0))],
        out_shape=[jax.ShapeDtypeStruct((B, T, G_KV * gw), F32),
                   jax.ShapeDtypeStruct((B, G_KV, T, NSBp), F32)],
        compiler_params=_cparams(("parallel", "parallel", "parallel")),
        name="nsa_cmp",
    )(slopes, proj, kvc.reshape(B, NCp, -1), kvc.reshape(B, NCp, -1))


LOG2E = 1.4426950408889634


def _bf16_part(x):
    return x.astype(BF16).astype(F32)


def _alibi_lhs(q, slope_col):
    c = slope_col * LOG2E
    c1 = _bf16_part(c)
    c2 = _bf16_part(c - c1)
    c3 = _bf16_part(c - c1 - c2)
    lane = lax.broadcasted_iota(jnp.int32, q.shape, 1)
    extra = jnp.where((lane == 0) | (lane == 3), c1,
                      jnp.where((lane == 1) | (lane == 4), c2, jnp.where((lane == 2) | (lane == 5), c3, 0.0)))
    return jnp.concatenate([q, extra], axis=1).astype(BF16)


def _alibi_rhs(k, k0):
    pos = k0 + lax.broadcasted_iota(jnp.int32, k.shape, 0)
    lane = lax.broadcasted_iota(jnp.int32, k.shape, 1)
    hi = jnp.bitwise_and(pos, -L_SEL)
    extra = jnp.where(lane < 3, hi, jnp.where(lane < 6, pos - hi, 0)).astype(F32)
    return jnp.concatenate([k, extra], axis=1).astype(BF16)


def _with_ones(v):
    lane = lax.broadcasted_iota(jnp.int32, v.shape, 1)
    return jnp.concatenate([v, jnp.where(lane == 0, 1.0, 0.0)], axis=1).astype(BF16)


def _silu(x):
    return x * _sigmoid(x)


def _nsa_selwin_prompt_kernel(slopes_ref, q_ref, selm_ref, ks_ref, vs_ref, kw_ref, vw_ref,
                              ocmp_ref, zc_ref, zs_ref, zw_ref, gate_ref, o_ref, s_scr, *, tq, HG, T, WS, SEG):
    g = pl.program_id(1)
    qt = pl.program_id(2)
    NSBp = selm_ref.shape[3]
    R = HG * tq
    qpos = qt * tq + lax.broadcasted_iota(jnp.int32, (tq, 1), 0)
    slope_col = jnp.concatenate([jnp.full((tq, 1), slopes_ref[g * HG + h], F32) for h in range(HG)], axis=0)
    tile_heads = lambda x: jnp.concatenate([x] * HG, axis=0)
    q2 = _alibi_lhs(_stack_heads(q_ref[0] * (HEAD_B ** -0.5 * LOG2E), HG), slope_col)

    selm_b = selm_ref[0, 0].astype(BF16)
    nseg = (qt * tq + tq + SEG - 1) // SEG

    def seg_scores(si, m):
        k0 = pl.multiple_of(si * SEG, SEG)
        kcol = k0 + lax.broadcasted_iota(jnp.int32, (NSBp, SEG), 1)
        expand = (jnp.right_shift(kcol, int(math.log2(L_SEL)))
                  == lax.broadcasted_iota(jnp.int32, (NSBp, SEG), 0)).astype(BF16)
        in_blk = jnp.dot(selm_b, expand, preferred_element_type=F32) > 0.5
        kpos = k0 + lax.broadcasted_iota(jnp.int32, (1, SEG), 1)
        mask = tile_heads(in_blk & (kpos <= qpos))
        s = lax.dot_general(q2, _alibi_rhs(ks_ref[0, pl.ds(k0, SEG), :], k0), NT_DIMS, preferred_element_type=F32)
        s = jnp.where(mask, s, NEG)
        s_scr[si] = s
        return jnp.maximum(m, jnp.max(s, axis=-1, keepdims=True))

    m = lax.fori_loop(0, nseg, seg_scores, jnp.full((R, 1), NEG, F32))

    def seg_pv(si, acc):
        k0 = pl.multiple_of(si * SEG, SEG)
        e = jnp.exp2(s_scr[si] - m).astype(BF16)
        return acc + jnp.dot(e, _with_ones(vs_ref[0, pl.ds(k0, SEG), :]), preferred_element_type=F32)

    acc = lax.fori_loop(0, nseg, seg_pv, jnp.zeros((R, 2 * HEAD_B), F32))
    o = acc[:, :HEAD_B] / acc[:, HEAD_B:HEAD_B + 1]

    start = pl.multiple_of(jnp.clip(qt * tq - WINDOW, 0, T - WS), SUBLANES)
    distw = qpos - (start + lax.broadcasted_iota(jnp.int32, (1, WS), 1))
    maskw = tile_heads((distw >= 0) & (distw < WINDOW))
    sw = lax.dot_general(q2, _alibi_rhs(kw_ref[0, pl.ds(start, WS), :], start), NT_DIMS, preferred_element_type=F32)
    sw = jnp.where(maskw, sw, NEG)
    ew = jnp.exp2(sw - jnp.max(sw, axis=-1, keepdims=True)).astype(BF16)
    accw = jnp.dot(ew, _with_ones(vw_ref[0, pl.ds(start, WS), :]), preferred_element_type=F32)
    ow = accw[:, :HEAD_B] / accw[:, HEAD_B:HEAD_B + 1]

    HB = G_KV * HG
    gates = _sigmoid(gate_ref[0])
    lane = lax.broadcasted_iota(jnp.int32, gates.shape, 1)
    gate_col = lambda idx: jnp.sum(jnp.where(lane == idx, gates, 0.0), axis=1, keepdims=True)
    for h in range(HG):
        hs = slice(h * HEAD_B, (h + 1) * HEAD_B)
        rs = slice(h * tq, (h + 1) * tq)
        hd = g * HG + h
        y = (gate_col(hd) * ocmp_ref[0, :, hs] * _silu(zc_ref[0, :, hs])
             + gate_col(HB + hd) * o[rs] * _silu(zs_ref[0, :, hs])
             + gate_col(2 * HB + hd) * ow[rs] * _silu(zw_ref[0, :, hs]))
        o_ref[0, :, hs] = y.astype(o_ref.dtype)


def nsa_selwin_prompt(proj, rows, selm, o_cmp, slopes, *, tq):
    B, T, _ = proj.shape
    HG = slopes.shape[0] // G_KV
    NSBp = selm.shape[3]
    gw = HG * HEAD_B
    CB = G_KV * gw
    WS = min(T, WINDOW + tq)
    SEG = min(T, 512)
    assert T % SEG == 0
    kv_spec = lambda c: pl.BlockSpec((1, T, HEAD_B), lambda b, g, t: (b, 0, c * G_KV + g))
    head_spec = lambda blk: pl.BlockSpec((1, tq, gw), lambda b, g, t: (b, t, blk * G_KV + g))
    return pl.pallas_call(
        functools.partial(_nsa_selwin_prompt_kernel, tq=tq, HG=HG, T=T, WS=WS, SEG=SEG),
        grid=(B, G_KV, T // tq),
        in_specs=[pl.BlockSpec(memory_space=pltpu.SMEM),
                  head_spec(0),
                  pl.BlockSpec((1, 1, tq, NSBp), lambda b, g, t: (b, g, t, 0)),
                  kv_spec(2), kv_spec(3), kv_spec(4), kv_spec(5),
                  head_spec(0), head_spec(1), head_spec(2), head_spec(3),
                  pl.BlockSpec((1, tq, LANES), lambda b, g, t: (b, t, 4 * CB // LANES))],
        out_specs=head_spec(0),
        out_shape=jax.ShapeDtypeStruct((B, T, CB), BF16),
        scratch_shapes=[pltpu.VMEM((T // SEG, HG * tq, SEG), F32)],
        compiler_params=_cparams(("parallel", "parallel", "parallel")),
        name="nsa_selwin_prompt",
    )(slopes, proj, selm, rows, rows, rows, rows, o_cmp, proj, proj, proj, proj)


def _nsa_selwin_sample_kernel(pt_ref, slopes_ref, q_ref, selm_ref, selst_ref, *refs, PGS, PS, HG, TQ, pos0, n_new, n_win):
    del pt_ref
    page_refs = refs[:PGS]
    (new_ref, cwin_ref, ocmp_ref, zc_ref, zs_ref, zw_ref, gate_ref, o_ref,
     m_scr, l_scr, acc_scr) = refs[PGS:]
    st = pl.program_id(1)
    NSBp = selm_ref.shape[3]
    GW = G_KV * HEAD_B
    sel_shift = int(math.log2(L_SEL))

    @pl.when(st == 0)
    def _():
        m_scr[...] = jnp.full(m_scr.shape, NEG, F32)
        l_scr[...] = jnp.zeros(l_scr.shape, F32)
        acc_scr[...] = jnp.zeros(acc_scr.shape, F32)

    qpos = pos0 + lax.broadcasted_iota(jnp.int32, (TQ, 1), 0)
    lane_j = lax.broadcasted_iota(jnp.int32, (TQ, NSBp), 1)
    tile_heads = lambda x: jnp.concatenate([x] * HG, axis=0)

    def sel_col(selm_g, j):
        return jnp.sum(jnp.where(lane_j == j, selm_g, 0.0), axis=1, keepdims=True)

    def online_update(s, mask, v):
        gs = range(G_KV)
        m_old = [m_scr[g] for g in gs]
        m_new = [jnp.maximum(m_old[g], jnp.max(jnp.where(mask[g], s[g], NEG), axis=-1, keepdims=True)) for g in gs]
        e = [jnp.where(mask[g], jnp.exp(s[g] - m_new[g]), 0.0) for g in gs]
        alpha = [jnp.exp(m_old[g] - m_new[g]) for g in gs]
        pv = [_bdot(e[g], v[g]) for g in gs]
        for g in gs:
            l_scr[g] = alpha[g] * l_scr[g] + jnp.sum(e[g], axis=-1, keepdims=True)
            acc_scr[g] = alpha[g] * acc_scr[g] + pv[g]
            m_scr[g] = m_new[g]

    NK = PGS * PS
    kpos = st * NK + lax.broadcasted_iota(jnp.int32, (1, NK), 1)
    expand = (jnp.right_shift(lax.broadcasted_iota(jnp.int32, (LANES, NK), 1), sel_shift)
              == lax.broadcasted_iota(jnp.int32, (LANES, NK), 0)).astype(BF16)
    in_blk_all = jnp.dot(selst_ref[0, 0].astype(BF16), expand, preferred_element_type=F32)
    dist = qpos - kpos
    distf = tile_heads(dist.astype(F32))
    gs = range(G_KV)
    q_st = [_stack_heads(q_ref[0, :, g * HG * HEAD_B:(g + 1) * HG * HEAD_B] * (HEAD_B ** -0.5), HG).astype(BF16)
            for g in gs]
    slope_col = [jnp.concatenate([jnp.full((TQ, 1), slopes_ref[g * HG + h], F32) for h in range(HG)], axis=0)
                 for g in gs]
    slabs = [pltpu.einshape("rcd->crd", page_refs[i][0].reshape(PS, 2 * G_KV, HEAD_B)) for i in range(PGS)]
    k = [jnp.concatenate([slabs[i][g] for i in range(PGS)], axis=0) for g in gs]
    v = [jnp.concatenate([slabs[i][G_KV + g] for i in range(PGS)], axis=0) for g in gs]
    mask = [tile_heads((in_blk_all[g * TQ:(g + 1) * TQ] > 0.5) & (dist >= 0)) for g in gs]
    s = [_bdot_nt(q_st[g], k[g]) - slope_col[g] * distf for g in gs]
    online_update(s, mask, v)

    @pl.when(st == pl.num_programs(1) - 1)
    def _():
        NN = new_ref.shape[1]
        rnew = lax.broadcasted_iota(jnp.int32, (1, NN), 1)
        kpos_n = pos0 + rnew
        dist_n = qpos - kpos_n
        ok_n = (rnew < n_new) & (dist_n >= 0)
        jn = pos0 >> sel_shift
        kpos_w = pos0 - n_win + lax.broadcasted_iota(jnp.int32, (1, n_win), 1)
        dist_w = qpos - kpos_w
        dist_wall = jnp.concatenate([dist_w, dist_n], axis=1)
        mask_wall = jnp.concatenate([(dist_w >= 0) & (dist_w < WINDOW), ok_n & (dist_n < WINDOW)], axis=1)
        kn = [new_ref[0, :, 2 * GW + g * HEAD_B:2 * GW + (g + 1) * HEAD_B] for g in gs]
        vn = [new_ref[0, :, 3 * GW + g * HEAD_B:3 * GW + (g + 1) * HEAD_B] for g in gs]
        mask_n = [tile_heads((sel_col(selm_ref[0, g], jn) > 0.5) & ok_n) for g in gs]
        dist_nf = tile_heads(dist_n.astype(F32))
        s_n = [_bdot_nt(q_st[g], kn[g]) - slope_col[g] * dist_nf for g in gs]
        online_update(s_n, mask_n, vn)
        kw = [jnp.concatenate([cwin_ref[0, :, g * HEAD_B:(g + 1) * HEAD_B],
                               new_ref[0, :, 4 * GW + g * HEAD_B:4 * GW + (g + 1) * HEAD_B]], axis=0) for g in gs]
        vw = [jnp.concatenate([cwin_ref[0, :, GW + g * HEAD_B:GW + (g + 1) * HEAD_B],
                               new_ref[0, :, 5 * GW + g * HEAD_B:5 * GW + (g + 1) * HEAD_B]], axis=0) for g in gs]
        dist_wf = tile_heads(dist_wall.astype(F32))
        mask_w = tile_heads(mask_wall)
        sw = [_bdot_nt(q_st[g], kw[g]) - slope_col[g] * dist_wf for g in gs]
        pw = [_masked_softmax_rows(sw[g], mask_w) for g in gs]
        ow = [_bdot(pw[g], vw[g]) for g in gs]
        HB = G_KV * HG
        gates = _sigmoid(gate_ref[0])
        for g in gs:
            l = l_scr[g]
            o = acc_scr[g] / jnp.where(l > 0.0, l, 1.0)
            for h in range(HG):
                hd = g * HG + h
                hs = slice(hd * HEAD_B, (hd + 1) * HEAD_B)
                rs = slice(h * TQ, (h + 1) * TQ)
                y = (gates[:, hd:hd + 1] * ocmp_ref[0, :, hs] * _silu(zc_ref[0, :, hs])
                     + gates[:, HB + hd:HB + hd + 1] * o[rs] * _silu(zs_ref[0, :, hs])
                     + gates[:, 2 * HB + hd:2 * HB + hd + 1] * ow[g][rs] * _silu(zw_ref[0, :, hs]))
                o_ref[0, :, hs] = y.astype(o_ref.dtype)


def nsa_selwin_sample(proj, selm, o_cmp, pool5, table, new_rows, cwin, slopes, *, pos0, n_new):
    B, TQ, _ = proj.shape
    HG = slopes.shape[0] // G_KV
    NP, CPP = pool5.shape[:2]
    PS = CPP * S_CMP
    n_pages = table.shape[1]
    NSBp = selm.shape[3]
    PGS = max(d for d in (8, 4, 2, 1) if n_pages % d == 0)
    CB = G_KV * HG * HEAD_B
    GW = G_KV * HEAD_B
    NN = new_rows.shape[1]
    n_win = cwin.shape[1]
    assert pos0 % L_SEL == 0 and n_new <= L_SEL and pos0 == n_pages * PS

    def page_map(i):
        return lambda b, s, pt: (pt[b, s * PGS + i], 0, 0, 1, 0)

    n_steps = n_pages // PGS
    bps = PGS * PS // L_SEL
    assert bps <= LANES
    selst = selm[:, :, :, :n_steps * bps].reshape(B, G_KV, TQ, n_steps, bps).transpose(0, 3, 1, 2, 4)
    selst = jnp.pad(selst.reshape(B, n_steps, G_KV * TQ, bps), ((0, 0), (0, 0), (0, 0), (0, LANES - bps)))

    const = lambda b, s, pt: (b, 0, 0)
    wide = lambda blk: pl.BlockSpec((1, TQ, CB), lambda b, s, pt: (b, 0, blk))
    grid_spec = pltpu.PrefetchScalarGridSpec(
        num_scalar_prefetch=1,
        grid=(B, n_steps),
        in_specs=[pl.BlockSpec(memory_space=pltpu.SMEM),
                  pl.BlockSpec((1, TQ, CB), const),
                  pl.BlockSpec((1, G_KV, TQ, NSBp), lambda b, s, pt: (b, 0, 0, 0)),
                  pl.BlockSpec((1, 1, G_KV * TQ, LANES), lambda b, s, pt: (b, s, 0, 0))]
                 + [pl.BlockSpec((1, CPP, S_CMP, 2 * G_KV, HEAD_B), page_map(i)) for i in range(PGS)]
                 + [pl.BlockSpec((1, NN, 6 * GW), const),
                    pl.BlockSpec((1, n_win, 2 * GW), const),
                    wide(0), wide(1), wide(2), wide(3),
                    pl.BlockSpec((1, TQ, LANES), lambda b, s, pt: (b, 0, 4 * CB // LANES))],
        out_specs=wide(0),
        scratch_shapes=[pltpu.VMEM((G_KV, HG * TQ, 1), F32),
                        pltpu.VMEM((G_KV, HG * TQ, 1), F32),
                        pltpu.VMEM((G_KV, HG * TQ, HEAD_B), F32)],
    )
    return pl.pallas_call(
        functools.partial(_nsa_selwin_sample_kernel, PGS=PGS, PS=PS, HG=HG, TQ=TQ, pos0=pos0,
                          n_new=n_new, n_win=n_win),
        grid_spec=grid_spec,
        out_shape=jax.ShapeDtypeStruct((B, TQ, CB), F32),
        compiler_params=_cparams(("parallel", "arbitrary")),
        name="nsa_selwin_sample",
    )(table, slopes, proj, selm, selst, *([pool5] * PGS), new_rows, cwin, o_cmp, proj, proj, proj, proj)


def _rwkv_layer(h, x_prev, s0, i, W, B, T):
    N, D = h.shape
    g = W["norm_g"][i]
    rkvg = rwkv_in(h, x_prev, g, W["mu_a"][i], W["w_in_a"], i, T)
    CA = rkvg.shape[-1]
    lw, a = rwkv_lora(h, x_prev, g, W["mu_a"][i], W["w_lora_w1"][i], W["w_lora_w2"][i], W["a_lora1"][i],
                      W["a_lora2"][i], W["w0_a"][i], W["a0_a"][i], T)
    pvec = jnp.stack([W["k_k"][i], W["k_a"][i], W["r_k"][i].reshape(CA), W["ln_x_w"][i], W["ln_x_b"][i]])
    o, s_fin = rwkv_scan(rkvg.reshape(4, B, T, CA), lw.reshape(B, T, CA), a.reshape(B, T, CA), pvec, s0)
    last = rmsnorm(h.reshape(B, T, D)[:, -1], g)
    return o.reshape(N, CA), s_fin, last


def _nsa_layer(h, jb, shared, W, slopes, B, T, norm_g):
    N, D = h.shape
    CB = W["w_out_b"].shape[1]
    proj3 = norm_mm(h, norm_g, W["w_in_b"], (jb,)).reshape(B, T, -1)
    if shared["past"] is None:
        o_cmp, selm = nsa_cmp(proj3, shared["kvc"], slopes, tq=min(T, 256), nc=shared["nc"],
                              nsb=shared["nsb"], pos0=0)
        o = nsa_selwin_prompt(proj3, shared["rows"], selm, o_cmp, slopes, tq=min(T, 256))
    else:
        TQ = SUBLANES
        projp = jnp.pad(proj3, ((0, 0), (0, TQ - T), (0, 0)))
        o_cmp, selm = nsa_cmp(projp, shared["kvc"], slopes, tq=TQ, nc=shared["nc"],
                              nsb=shared["nsb"], pos0=shared["pos0"])
        pool, table, cwin = shared["past"]
        o = nsa_selwin_sample(projp, selm, o_cmp, pool, table, shared["new_rows"], cwin, slopes,
                              pos0=shared["pos0"], n_new=T)[:, :T]
    return o.reshape(N, CB)


def _trunk(x, p, pos0, wkv0, shift0, past, W, slopes):
    B, T, D = x.shape
    N = B * T
    depth = p.shape[0]
    n_a = W["w_in_a"].shape[0]
    GW = G_KV * HEAD_B
    h = x.reshape(N, D)
    wkv_new, shift_new = [], []
    shared, kv_rows, win_state = None, None, None
    for i in range(depth):
        if i < n_a:
            o, s_fin, last = _rwkv_layer(h, shift0[i], wkv0[i], i, W, B, T)
            wkv_new.append(s_fin)
            shift_new.append(last)
            h = out_ple(o, h, p[i].reshape(N, -1), W["w_out_a"], i, W["w_ple"], W["w_ple_gate"], i)
        else:
            o = _nsa_layer(h, i - n_a, shared, W, slopes, B, T, W["norm_g"][i])
            h = out_ple(o, h, p[i].reshape(N, -1), W["w_out_b"], i - n_a, W["w_ple"], W["w_ple_gate"], i)
        if i == n_a - 1:
            rows = norm_mm(h, W["kv_norm_g"], W["w_kv"]).reshape(B, T, 6 * GW)
            kv_rows = rows[:, :, :4 * GW].reshape(B, T, 4, G_KV, HEAD_B)
            win_new = rows[:, :, 4 * GW:].reshape(B, T, 2, G_KV, HEAD_B)
            if past is None:
                PS = 128
                pool = rows.reshape(B * T // PS, PS // S_CMP, S_CMP, 6 * G_KV, HEAD_B)
                table = jnp.arange(B * T // PS, dtype=jnp.int32).reshape(B, T // PS)
                t_all = T
                win_all = win_new
                shared = {"past": None, "rows": rows}
            else:
                pool, table, cwin = past
                PS = pool.shape[1] * S_CMP
                t_all = pos0 + T
                win_all = jnp.concatenate([cwin.reshape(B, -1, 2, G_KV, HEAD_B), win_new], axis=1)
                NN = LANES
                shared = {"past": past, "new_rows": jnp.pad(rows, ((0, 0), (0, NN - T), (0, 0)))}
            win_state = win_all[:, win_all.shape[1] - min(WINDOW, pos0 + T):]
            nc = (t_all - L_CMP) // S_CMP + 1
            assert nc < table.shape[1] * PS // S_CMP
            kvc = compress_kv(pool, table, W["pe_cmp"], W["w_cmp1"], W["w_cmp2"])
            shared.update(kvc=kvc, nc=nc, nsb=max(-(-t_all // L_SEL), TOPK_SEL), pos0=pos0)
    y = rmsnorm(h, W["final_norm_g"]).reshape(B, T, D)
    return y, jnp.stack(wkv_new), jnp.stack(shift_new), kv_rows, win_state


def kernel(x_prompt, x_sample, state_wkv, state_shift, cache_kv, cache_win_kv, page_table, p_prompt, p_sample, norm_g, mu_a, w_in_a, w_lora_w1, w_lora_w2, w0_a, a_lora1, a_lora2, a0_a, k_k, k_a, r_k, ln_x_w, ln_x_b, w_out_a, w_in_b, w_out_b, kv_norm_g, w_kv, pe_cmp, w_cmp1, w_cmp2, w_ple, w_ple_gate, final_norm_g):
    bf = lambda w: w.astype(BF16)
    CA = w_out_a.shape[1]
    W = dict(norm_g=norm_g, mu_a=mu_a, w_in_a=bf(w_in_a), w_lora_w1=bf(w_lora_w1), w_lora_w2=bf(w_lora_w2),
             w0_a=w0_a, a_lora1=bf(a_lora1), a_lora2=bf(a_lora2), a0_a=a0_a, k_k=k_k, k_a=k_a,
             r_k=r_k, ln_x_w=ln_x_w, ln_x_b=ln_x_b,
             w_out_a=bf(w_out_a), w_in_b=bf(w_in_b), w_out_b=bf(w_out_b), kv_norm_g=kv_norm_g, w_kv=bf(w_kv),
             pe_cmp=pe_cmp, w_cmp1=bf(w_cmp1), w_cmp2=bf(w_cmp2), w_ple=bf(w_ple), w_ple_gate=bf(w_ple_gate),
             final_norm_g=final_norm_g)
    HB = w_out_b.shape[1] // HEAD_B
    slopes = 2.0 ** (-8.0 * jnp.arange(1, HB + 1, dtype=F32) / HB)
    bp = x_prompt.shape[0]
    n_a = w_in_a.shape[0]
    D = x_prompt.shape[-1]
    wkv0 = jnp.zeros((n_a, bp, CA // HEAD_A, HEAD_A, HEAD_A), F32)
    shift0 = jnp.zeros((n_a, bp, D), F32)
    y_p, wkv_p, shift_p, kv_p, win_p = _trunk(x_prompt, p_prompt, 0, wkv0, shift0, None, W, slopes)
    db, n_pages = page_table.shape
    NP, PS = cache_kv.shape[:2]
    pool5 = cache_kv.reshape(NP, PS // S_CMP, S_CMP, -1, HEAD_B)
    past = (pool5, page_table, cache_win_kv.reshape(db, cache_win_kv.shape[1], -1))
    y_s, wkv_s, shift_s, kv_s, win_s = _trunk(x_sample, p_sample, n_pages * PS, state_wkv, state_shift, past, W, slopes)
    return (y_p, y_s, wkv_p, shift_p, kv_p, win_p, wkv_s, shift_s, kv_s, win_s)
```

```python
import functools
import math

import jax
import jax.numpy as jnp
from jax import lax
from jax.experimental import pallas as pl
from jax.experimental.pallas import tpu as pltpu

F32 = jnp.float32
BF16 = jnp.bfloat16

HEAD_A = 64
GN_EPS = 64e-5
HEAD_B = 128
G_KV = 4
L_CMP = 32
S_CMP = 16
L_SEL = 64
TOPK_SEL = 16
WINDOW = 512
RMS_EPS = 1e-6
NEG = -1e30
FORCE_BONUS = 1e4

LANES = 128
SUBLANES = 8
VMEM_LIMIT = 56 * 1024 * 1024

SCAN_NH = 2
SCAN_C = 64
SCAN_GP = 16

NT_DIMS = (((1,), (1,)), ((), ()))
TN_DIMS = (((0,), (0,)), ((), ()))


def _cparams(sem):
    return pltpu.CompilerParams(dimension_semantics=sem, vmem_limit_bytes=VMEM_LIMIT)


def _bdot(a, b):
    return jnp.dot(a.astype(BF16), b.astype(BF16), preferred_element_type=F32)


def _bdot_nt(a, b):
    return lax.dot_general(a.astype(BF16), b.astype(BF16), NT_DIMS, preferred_element_type=F32)


def _bdot_tn(a, b):
    return lax.dot_general(a.astype(BF16), b.astype(BF16), TN_DIMS, preferred_element_type=F32)


def _rms_kernel(x_ref, g_ref, o_ref):
    x = x_ref[...]
    ms = jnp.mean(x * x, axis=-1, keepdims=True)
    o_ref[...] = x * lax.rsqrt(ms + RMS_EPS) * g_ref[...]


def rmsnorm(x, g):
    M, D = x.shape
    tm = min(M, 256)
    return pl.pallas_call(
        _rms_kernel,
        grid=(pl.cdiv(M, tm),),
        in_specs=[pl.BlockSpec((tm, D), lambda i: (i, 0)),
                  pl.BlockSpec((1, D), lambda i: (0, 0))],
        out_specs=pl.BlockSpec((tm, D), lambda i: (i, 0)),
        out_shape=jax.ShapeDtypeStruct((M, D), F32),
        compiler_params=_cparams(("parallel",)),
        name="rmsnorm",
    )(x, g.reshape(1, D))


def _mm_kernel(x_ref, w_ref, o_ref):
    o_ref[...] = jnp.dot(x_ref[...].astype(BF16), w_ref[...], preferred_element_type=F32)


def mm(x, w, widx=()):
    M, K = x.shape
    N = w.shape[-1]
    assert w.shape[-2] == K and len(widx) == w.ndim - 2
    tm = min(M, 1024 if K <= 2048 else 512)
    tn = N if N <= 512 else 512
    nlead = len(widx)
    w_spec = pl.BlockSpec((None,) * nlead + (K, tn), lambda i, j: tuple(widx) + (0, j))
    return pl.pallas_call(
        _mm_kernel,
        grid=(pl.cdiv(M, tm), pl.cdiv(N, tn)),
        in_specs=[pl.BlockSpec((tm, K), lambda i, j: (i, 0)), w_spec],
        out_specs=pl.BlockSpec((tm, tn), lambda i, j: (i, j)),
        out_shape=jax.ShapeDtypeStruct((M, N), F32),
        compiler_params=_cparams(("parallel", "parallel")),
        name="mm",
    )(x, w)


def _norm_rows(x, g):
    return x * lax.rsqrt(jnp.mean(x * x, axis=-1, keepdims=True) + RMS_EPS) * g


def _sigmoid(x):
    return 0.5 + 0.5 * jnp.tanh(0.5 * x)


def _norm_and_shift(h_ref, hprev_ref, xprev_ref, g_ref, i, tm, T):
    g = g_ref[...]
    hn = _norm_rows(h_ref[...], g)
    prev_row = _norm_rows(hprev_ref[SUBLANES - 1:SUBLANES, :], g)
    row = lax.broadcasted_iota(jnp.int32, (tm, 1), 0)
    xs = jnp.where(row == 0, prev_row, pltpu.roll(hn, 1, axis=0))
    if T >= tm:
        assert T % tm == 0
        start = (i * tm) % T == 0
        xs = jnp.where((row == 0) & start, xprev_ref[pl.ds((i * tm) // T, 1), :], xs)
    else:
        assert tm % T == 0
        for bb in range(tm // T):
            xs = jnp.where(row == bb * T, xprev_ref[pl.ds(i * (tm // T) + bb, 1), :], xs)
    return hn, xs


def _rwkv_in_kernel(h_ref, hprev_ref, xprev_ref, g_ref, mu_ref, w_ref, o_ref, hn_scr, xs_scr, *, tm, T):
    i, j, n = pl.program_id(0), pl.program_id(1), pl.program_id(2)
    rc = min(tm, 256)
    chunks = [slice(c * rc, (c + 1) * rc) for c in range(tm // rc)]
    P = SUBLANES

    @pl.when((j == 0) & (n == 0))
    def _():
        g = g_ref[...]
        prev = _norm_rows(hprev_ref[P - 1:P, :], g)
        if T >= tm:
            assert T % tm == 0
            prev = jnp.where((i * tm) % T == 0, xprev_ref[pl.ds((i * tm) // T, 1), :], prev)
        hn_scr[0:P, :] = jnp.broadcast_to(prev, (P, prev.shape[1]))
        for ch in chunks:
            hn_scr[P + ch.start:P + ch.stop, :] = _norm_rows(h_ref[ch, :], g)

    @pl.when(n == 0)
    def _():
        mu = mu_ref[pl.ds(j, 1), :]
        for ch in chunks:
            hn = hn_scr[P + ch.start:P + ch.stop, :]
            xs = hn_scr[P - 1 + ch.start:P - 1 + ch.stop, :]
            if T < tm:
                assert tm % T == 0 and len(chunks) == 1
                row = lax.broadcasted_iota(jnp.int32, (tm, 1), 0)
                for bb in range(tm // T):
                    xs = jnp.where(row == bb * T, xprev_ref[pl.ds(i * (tm // T) + bb, 1), :], xs)
            xs_scr[ch, :] = (hn + (xs - hn) * mu).astype(BF16)

    o_ref[...] = jnp.dot(xs_scr[...], w_ref[...], preferred_element_type=F32)


def _shift_specs(tm, D, nb, ngrid):
    z = (0,) * (ngrid - 1)
    wrap = lambda f: (lambda i, *_: f(i))
    return [pl.BlockSpec((tm, D), wrap(lambda i: (i, 0))),
            pl.BlockSpec((SUBLANES, D), wrap(lambda i: (jnp.maximum(i * (tm // SUBLANES) - 1, 0), 0))),
            pl.BlockSpec((nb, D), wrap(lambda i: (0, 0))),
            pl.BlockSpec((1, D), wrap(lambda i: (0, 0))),
            pl.BlockSpec((6, D), wrap(lambda i: (0, 0)))]


def rwkv_in(h, xprev, g, mu, w, layer, T):
    N, D = h.shape
    C = w.shape[-1]
    tm = min(N, 1024)
    tn = 512
    return pl.pallas_call(
        functools.partial(_rwkv_in_kernel, tm=tm, T=T),
        grid=(N // tm, 4, C // tn),
        in_specs=_shift_specs(tm, D, xprev.shape[0], 3) + [
            pl.BlockSpec((None, None, D, tn), lambda i, j, n: (layer, j, 0, n))],
        out_specs=pl.BlockSpec((None, tm, tn), lambda i, j, n: (j, i, n)),
        out_shape=jax.ShapeDtypeStruct((4, N, C), F32),
        scratch_shapes=[pltpu.VMEM((tm + SUBLANES, D), F32), pltpu.VMEM((tm, D), BF16)],
        compiler_params=_cparams(("parallel", "arbitrary", "arbitrary")),
        name="rwkv_in",
    )(h, h, xprev, g.reshape(1, D), mu, w)


def _rwkv_lora_kernel(h_ref, hprev_ref, xprev_ref, g_ref, mu_ref, lw1_ref, lw2_ref, la1_ref, la2_ref,
                      w0_ref, a0_ref, lw_ref, a_ref, *, tm, T):
    hn, xs = _norm_and_shift(h_ref, hprev_ref, xprev_ref, g_ref, pl.program_id(0), tm, T)
    dx = xs - hn
    x4 = (hn + dx * mu_ref[4:5, :]).astype(BF16)
    x5 = (hn + dx * mu_ref[5:6, :]).astype(BF16)
    t4 = jnp.tanh(jnp.dot(x4, lw1_ref[...], preferred_element_type=F32)).astype(BF16)
    x = w0_ref[...] + jnp.dot(t4, lw2_ref[...], preferred_element_type=F32)
    lw_ref[...] = -math.exp(-0.5) * _sigmoid(x)
    t5 = jnp.dot(x5, la1_ref[...], preferred_element_type=F32).astype(BF16)
    a_ref[...] = _sigmoid(a0_ref[...] + jnp.dot(t5, la2_ref[...], preferred_element_type=F32))


def rwkv_lora(h, xprev, g, mu, lw1, lw2, la1, la2, w0, a0, T):
    N, D = h.shape
    R, C = lw2.shape
    tm = min(N, 256)
    full = lambda shape: pl.BlockSpec(shape, lambda i: (0,) * len(shape))
    o_spec = pl.BlockSpec((tm, C), lambda i: (i, 0))
    return pl.pallas_call(
        functools.partial(_rwkv_lora_kernel, tm=tm, T=T),
        grid=(N // tm,),
        in_specs=_shift_specs(tm, D, xprev.shape[0], 1) + [
            full((D, R)), full((R, C)), full((D, R)), full((R, C)), full((1, C)), full((1, C))],
        out_specs=[o_spec, o_spec],
        out_shape=[jax.ShapeDtypeStruct((N, C), F32)] * 2,
        compiler_params=_cparams(("parallel",)),
        name="rwkv_lora",
    )(h, h, xprev, g.reshape(1, D), mu, lw1, lw2, la1, la2, w0.reshape(1, C), a0.reshape(1, C))


def _norm_mm_kernel(h_ref, g_ref, w_ref, o_ref, xs_scr):
    @pl.when(pl.program_id(1) == 0)
    def _():
        rc = min(h_ref.shape[0], 256)

        def norm_chunk(c, carry):
            rows = pl.ds(pl.multiple_of(c * rc, rc), rc)
            xs_scr[rows, :] = _norm_rows(h_ref[rows, :], g_ref[...]).astype(BF16)
            return carry

        lax.fori_loop(0, h_ref.shape[0] // rc, norm_chunk, 0)

    o_ref[...] = jnp.dot(xs_scr[...], w_ref[...], preferred_element_type=F32)


def norm_mm(h, g, w, widx=()):
    N, D = h.shape
    NO = w.shape[-1]
    tm = min(N, 1024)
    tn = 1024
    nlead = len(widx)
    return pl.pallas_call(
        _norm_mm_kernel,
        grid=(N // tm, pl.cdiv(NO, tn)),
        in_specs=[pl.BlockSpec((tm, D), lambda i, n: (i, 0)),
                  pl.BlockSpec((1, D), lambda i, n: (0, 0)),
                  pl.BlockSpec((None,) * nlead + (D, tn), lambda i, n: tuple(widx) + (0, n))],
        out_specs=pl.BlockSpec((tm, tn), lambda i, n: (i, n)),
        out_shape=jax.ShapeDtypeStruct((N, NO), F32),
        scratch_shapes=[pltpu.VMEM((tm, D), BF16)],
        compiler_params=_cparams(("parallel", "arbitrary")),
        name="norm_mm",
    )(h, g.reshape(1, D), w)


def _mm_res_kernel(x_ref, h_ref, w_ref, o_ref):
    o_ref[...] = h_ref[...] + jnp.dot(x_ref[...].astype(BF16), w_ref[...], preferred_element_type=F32)


def _ple_gate_kernel(h1_ref, h1t_ref, p_ref, wp_ref, wg_ref, o_ref, xb_scr):
    @pl.when(pl.program_id(1) == 0)
    def _():
        rc = min(h1_ref.shape[0], 256)

        def cast_chunk(c, carry):
            rows = pl.ds(pl.multiple_of(c * rc, rc), rc)
            xb_scr[rows, :] = h1_ref[rows, :].astype(BF16)
            return carry

        lax.fori_loop(0, h1_ref.shape[0] // rc, cast_chunk, 0)

    gate = jnp.dot(xb_scr[...], wg_ref[...], preferred_element_type=F32)
    ple = jnp.dot(p_ref[...].astype(BF16), wp_ref[...], preferred_element_type=F32)
    o_ref[...] = h1t_ref[...] + ple * _sigmoid(gate)


def out_ple(x, h, p, w_out, oidx, w_ple, w_gate, layer):
    N, C = x.shape
    D = h.shape[1]
    DP = p.shape[1]
    tm = min(N, 1024)
    tn = 512
    grid = (N // tm, D // tn)
    tile = pl.BlockSpec((tm, tn), lambda i, n: (i, n))
    h1 = pl.pallas_call(
        _mm_res_kernel,
        grid=grid,
        in_specs=[pl.BlockSpec((tm, C), lambda i, n: (i, 0)), tile,
                  pl.BlockSpec((None, C, tn), lambda i, n: (oidx, 0, n))],
        out_specs=tile,
        out_shape=jax.ShapeDtypeStruct((N, D), F32),
        compiler_params=_cparams(("parallel", "parallel")),
        name="mm_res",
    )(x, h, w_out)
    return pl.pallas_call(
        _ple_gate_kernel,
        grid=grid,
        in_specs=[pl.BlockSpec((tm, D), lambda i, n: (i, 0)), tile,
                  pl.BlockSpec((tm, DP), lambda i, n: (i, 0)),
                  pl.BlockSpec((None, DP, tn), lambda i, n: (layer, 0, n)),
                  pl.BlockSpec((None, D, tn), lambda i, n: (layer, 0, n))],
        out_specs=tile,
        out_shape=jax.ShapeDtypeStruct((N, D), F32),
        scratch_shapes=[pltpu.VMEM((tm, D), BF16)],
        compiler_params=_cparams(("parallel", "arbitrary")),
        name="ple_gate",
    )(h1, h1, p, w_ple, w_gate)


def _scan_kernel(r_ref, k_ref, v_ref, zg_ref, lw_ref, a_ref, pv_ref, s0_ref, o_ref, sfin_ref, s_scr, *, C, NH, GP):
    L = NH * HEAD_A
    NC = NH * C
    ci = pl.program_id(2)

    @pl.when(ci == 0)
    def _():
        s_scr[...] = s0_ref[0]

    row_c = lax.broadcasted_iota(jnp.int32, (C, NC), 0)
    col_s = lax.broadcasted_iota(jnp.int32, (C, NC), 1) % C
    tri_strict = col_s < row_c
    tri_incl = (lax.broadcasted_iota(jnp.int32, (C, 2 * NC), 1) % C
                <= lax.broadcasted_iota(jnp.int32, (C, 2 * NC), 0))
    st_mask = (lax.broadcasted_iota(jnp.int32, (NC, L), 0) // C
               == lax.broadcasted_iota(jnp.int32, (NC, L), 1) // HEAD_A)
    bd_mask = (lax.broadcasted_iota(jnp.int32, (NC, NC), 0) // C
               == lax.broadcasted_iota(jnp.int32, (NC, NC), 1) // C)
    head_mask = (lax.broadcasted_iota(jnp.int32, (L, L), 0) // HEAD_A
                 == lax.broadcasted_iota(jnp.int32, (L, L), 1) // HEAD_A)

    def st(x):
        return jnp.where(st_mask, jnp.concatenate([x] * NH, axis=0), 0.0)

    def bd(w):
        return jnp.where(bd_mask, jnp.concatenate([w] * NH, axis=0), 0.0)

    n_double = int(math.log2(C))
    each = lambda f, *cols: [f(*xs) for xs in zip(*cols)]
    sls = [slice(gp * L, (gp + 1) * L) for gp in range(GP)]
    head_of_lane = lax.broadcasted_iota(jnp.int32, (1, L), 1) // HEAD_A

    def hsum(x):
        out = None
        for hh in range(NH):
            sh = jnp.sum(jnp.where(head_of_lane == hh, x, 0.0), axis=-1, keepdims=True)
            out = sh if out is None else jnp.where(head_of_lane == hh, sh, out)
        return out

    k_k, k_a, r_k, ln_w, ln_b = ([pv_ref[n:n + 1, sl] for sl in sls] for n in range(5))
    lw = [lw_ref[0, :, sl] for sl in sls]
    a_sig = [a_ref[0, :, sl] for sl in sls]
    r = [r_ref[0, :, sl] for sl in sls]
    k_raw = [k_ref[0, :, sl] for sl in sls]
    v = [v_ref[0, :, sl] for sl in sls]
    kk = each(lambda x, w: x * w, k_raw, k_k)
    kk = each(lambda x: x * lax.rsqrt(jnp.maximum(hsum(x * x), 1e-24)), kk)
    k = each(lambda x, a, w: x * (1.0 + (a - 1.0) * w), k_raw, a_sig, k_a)
    lw_hi = each(lambda z: z.astype(BF16), lw)
    lw_lo = each(lambda z, hi: (z - hi.astype(F32)).astype(BF16), lw, lw_hi)
    cum_mat = (lax.broadcasted_iota(jnp.int32, (C, C), 1)
               <= lax.broadcasted_iota(jnp.int32, (C, C), 0)).astype(BF16)
    cum2 = each(lambda hi, lo: jnp.dot(cum_mat, jnp.concatenate([hi, lo], axis=1), preferred_element_type=F32),
                lw_hi, lw_lo)
    cum = [c2[:, :L] + c2[:, L:] for c2 in cum2]
    p_incl = each(jnp.exp, cum)
    p_inv = each(lambda z: jnp.exp(-z), cum)
    at = each(lambda x, c, w: -x * jnp.exp(c - w), kk, cum, lw)
    rt = each(lambda x, p: x * p, r, p_incl)
    bt = each(lambda x, a, p: x * a * p, kk, a_sig, p_inv)
    kt = each(lambda x, p: x * p, k, p_inv)
    S = [s_scr[gp] for gp in range(GP)]
    ar = each(lambda x, y: jnp.concatenate([x, y], axis=0), at, rt)
    bk_st = each(lambda x, y: jnp.concatenate([st(x), st(y)], axis=0), bt, kt)
    Gm = each(_bdot_nt, ar, bk_st)
    w_ab = [jnp.where(tri_strict, g[:C, :NC], 0.0) for g in Gm]
    tm = w_ab
    pw = each(lambda w: _bdot(w, bd(w)), w_ab)
    LH = each(_bdot_nt, ar, S)
    v_st = each(st, v)
    x = [lh[:C] + _bdot(jnp.where(tri_strict, g[:C, NC:], 0.0), vs) for lh, g, vs in zip(LH, Gm, v_st)]
    for it in range(1, n_double):
        if it < n_double - 1:
            both = each(lambda t, p: _bdot(p, jnp.concatenate([bd(t), bd(p)], axis=1)), tm, pw)
            tm = each(lambda t, p, b2: t + p + b2[:, :NC], tm, pw, both)
            pw = [b2[:, NC:] for b2 in both]
        else:
            tm = each(lambda t, p: t + p + _bdot(p, bd(t)), tm, pw)
    u = each(lambda xx, t: xx + _bdot(t, st(xx)), x, tm)
    o = [lh[C:] + _bdot(jnp.where(tri_incl, g[C:], 0.0), jnp.concatenate([st(uu), vs], axis=0))
         for lh, g, uu, vs in zip(LH, Gm, u, v_st)]
    inv_n = 1.0 / HEAD_A
    dev = each(lambda x: x - hsum(x) * inv_n, o)
    gn = each(lambda d, w, b_: d * lax.rsqrt(hsum(d * d) * inv_n + GN_EPS) * w + b_, dev, ln_w, ln_b)
    bonus = each(lambda rr, kx, w, vv: hsum(rr * kx * w) * vv, r, k, r_k, v)
    for sl, y, bo in zip(sls, gn, bonus):
        zg = zg_ref[0, :, sl]
        o_ref[0, :, sl] = ((y + bo) * (zg * _sigmoid(zg))).astype(o_ref.dtype)
    ds = [_bdot_tn(jnp.concatenate([uu, vv], axis=0), jnp.concatenate([b_, k_], axis=0))
          for uu, vv, b_, k_ in zip(u, v, bt, kt)]
    for gp in range(GP):
        s_scr[gp] = (S[gp] + jnp.where(head_mask, ds[gp], 0.0)) * p_incl[gp][C - 1:C, :]

    @pl.when(ci == pl.num_programs(2) - 1)
    def _():
        sfin_ref[0] = s_scr[...]


def rwkv_scan(rkvg, lw, a, pvec, s0):
    _, B, T, CA = rkvg.shape
    H = CA // HEAD_A
    NH, C, GP = SCAN_NH, SCAN_C, SCAN_GP
    assert NH * C == LANES and H % (NH * GP) == 0
    L = NH * HEAD_A
    NG = H // NH
    Tp = -(-T // C) * C
    if Tp != T:
        rkvg = jnp.pad(rkvg, ((0, 0), (0, 0), (0, Tp - T), (0, 0)))
        lw, a = (jnp.pad(z, ((0, 0), (0, Tp - T), (0, 0))) for z in (lw, a))
    eye = jnp.eye(NH, dtype=F32)
    s0_bd = (s0.reshape(B, NG, NH, HEAD_A, 1, HEAD_A) * eye[None, None, :, None, :, None]).reshape(B, NG, L, L)
    seq_spec = pl.BlockSpec((1, C, GP * L), lambda bi, gi, ci: (bi, ci, gi))
    proj_spec = lambda j: pl.BlockSpec((None, 1, C, GP * L), lambda bi, gi, ci: (j, bi, ci, gi))
    st_spec = pl.BlockSpec((1, GP, L, L), lambda bi, gi, ci: (bi, gi, 0, 0))
    o, sfin = pl.pallas_call(
        functools.partial(_scan_kernel, C=C, NH=NH, GP=GP),
        grid=(B, NG // GP, Tp // C),
        in_specs=[proj_spec(j) for j in range(4)] + [seq_spec, seq_spec,
                  pl.BlockSpec((5, GP * L), lambda bi, gi, ci: (0, gi)), st_spec],
        out_specs=[seq_spec, st_spec],
        out_shape=[jax.ShapeDtypeStruct((B, Tp, CA), BF16), jax.ShapeDtypeStruct((B, NG, L, L), F32)],
        scratch_shapes=[pltpu.VMEM((GP, L, L), F32)],
        compiler_params=_cparams(("parallel", "parallel", "arbitrary")),
        name="rwkv_scan",
    )(rkvg, rkvg, rkvg, rkvg, lw, a, pvec, s0_bd)
    sf = sfin.reshape(B, NG, NH, HEAD_A, NH, HEAD_A)
    s_fin = jnp.stack([sf[:, :, h, :, h, :] for h in range(NH)], axis=2).reshape(B, H, HEAD_A, HEAD_A)
    return o[:, :T], s_fin


def _gelu_tanh(x):
    c = math.sqrt(2.0 / math.pi)
    return 0.5 * x * (1.0 + jnp.tanh(c * (x + 0.044715 * (x * x * x))))


def _compress_kernel(pt_ref, *refs, PGS):
    del pt_ref
    page_refs = refs[:PGS]
    next_ref, pe_ref, w1_ref, w2_ref, out_ref = refs[PGS:]
    CPP = page_refs[0].shape[1]
    NCH = PGS * CPP
    CG = 2 * G_KV
    M = (NCH + 1) * CG

    def rows_of(l, hf):
        pe = pe_ref[hf, l]
        parts = [(page_refs[i][0, :, l] + pe[None]).reshape(CPP * CG, HEAD_B) for i in range(PGS)]
        parts.append(next_ref[0, 0, l] + pe)
        return jnp.concatenate(parts, axis=0)

    top = jnp.zeros((M, 2 * HEAD_B), F32)
    bot = jnp.zeros((M, 2 * HEAD_B), F32)
    for l in range(0, S_CMP, 2):
        wrows = pl.ds(l * HEAD_B, 2 * HEAD_B)
        xt = jnp.concatenate([rows_of(l, 0), rows_of(l + 1, 0)], axis=1).astype(BF16)
        top = top + jnp.dot(xt, w1_ref[0, wrows, :], preferred_element_type=F32)
        xb = jnp.concatenate([rows_of(l, 1), rows_of(l + 1, 1)], axis=1).astype(BF16)
        bot = bot + jnp.dot(xb, w1_ref[1, wrows, :], preferred_element_type=F32)
    is_k = (lax.broadcasted_iota(jnp.int32, (M, 1), 0) % CG) < G_KV
    pick = lambda z, n: jnp.where(is_k[:n], z[:n, :HEAD_B], z[:n, HEAD_B:])
    hcur = pick(top, NCH * CG) + pick(bot, M)[CG:]
    o2 = jnp.dot(_gelu_tanh(hcur).astype(BF16), w2_ref[...], preferred_element_type=F32)
    out_ref[0] = pick(o2, NCH * CG).reshape(NCH, CG, HEAD_B)


def compress_kv(pool5, table, pe_cmp, w1, w2):
    NP, CPP = pool5.shape[:2]
    B, n_pages = table.shape
    PGS = max(d for d in (8, 4, 2, 1) if n_pages % d == 0)
    NCH = PGS * CPP
    CG = 2 * G_KV
    half = S_CMP * HEAD_B
    pe_r = jnp.repeat(pe_cmp.reshape(2, 2, S_CMP, HEAD_B).transpose(1, 2, 0, 3), G_KV, axis=2)
    w1_r = w1.reshape(2, 2, half, HEAD_B).transpose(1, 2, 0, 3).reshape(2, half, 2 * HEAD_B)
    w2_r = jnp.concatenate([w2[0], w2[1]], axis=1)

    def page_map(i):
        return lambda b, s, pt: (pt[b, s * PGS + i], 0, 0, 0, 0)

    def next_map(b, s, pt):
        return (pt[b, jnp.minimum((s + 1) * PGS, n_pages - 1)], 0, 0, 0, 0)

    const = lambda n: (lambda b, s, pt: (0,) * n)
    grid_spec = pltpu.PrefetchScalarGridSpec(
        num_scalar_prefetch=1,
        grid=(B, n_pages // PGS),
        in_specs=[pl.BlockSpec((1, CPP, S_CMP, CG, HEAD_B), page_map(i)) for i in range(PGS)] + [
            pl.BlockSpec((1, 1, S_CMP, CG, HEAD_B), next_map),
            pl.BlockSpec((2, S_CMP, CG, HEAD_B), const(4)),
            pl.BlockSpec((2, half, 2 * HEAD_B), const(3)),
            pl.BlockSpec((HEAD_B, 2 * HEAD_B), const(2)),
        ],
        out_specs=pl.BlockSpec((1, NCH, CG, HEAD_B), lambda b, s, pt: (b, s, 0, 0)),
    )
    return pl.pallas_call(
        functools.partial(_compress_kernel, PGS=PGS),
        grid_spec=grid_spec,
        out_shape=jax.ShapeDtypeStruct((B, n_pages * CPP, CG, HEAD_B), F32),
        compiler_params=_cparams(("parallel", "arbitrary")),
        name="compress_kv",
    )(table, *([pool5] * PGS), pool5, pe_r, w1_r, w2_r)


def _stack_heads(q, HG):
    return jnp.concatenate([q[:, h * HEAD_B:(h + 1) * HEAD_B] for h in range(HG)], axis=0)


def _masked_softmax_rows(s, mask):
    s = jnp.where(mask, s, NEG)
    m = jnp.max(s, axis=-1, keepdims=True)
    e = jnp.where(mask, jnp.exp(s - m), 0.0)
    l = jnp.sum(e, axis=-1, keepdims=True)
    return e / jnp.where(l > 0.0, l, 1.0)


def _attend_stacked(s, dist, mask, v, slopes_ref, g, HG, tq):
    ps = []
    psum = jnp.zeros(dist.shape, F32)
    for h in range(HG):
        p = _masked_softmax_rows(s[h * tq:(h + 1) * tq] - slopes_ref[g * HG + h] * dist, mask)
        psum = psum + p
        ps.append(p.astype(BF16))
    o = jnp.dot(jnp.concatenate(ps, axis=0), v.astype(BF16), preferred_element_type=F32)
    return o, psum


def _unstack_store(o_ref, o, HG, tq):
    for h in range(HG):
        o_ref[0, :, h * HEAD_B:(h + 1) * HEAD_B] = o[h * tq:(h + 1) * tq]


def _nsa_cmp_kernel(slopes_ref, q_ref, kc_ref, vc_ref, o_ref, selm_ref, *, tq, HG, nc, nsb, pos0):
    g = pl.program_id(1)
    qt = pl.program_id(2)
    NCp = kc_ref.shape[1]
    NSBp = selm_ref.shape[3]
    q_st = _stack_heads(q_ref[0] * (HEAD_B ** -0.5), HG)
    s = _bdot_nt(q_st, kc_ref[0])
    qpos = pos0 + qt * tq + lax.broadcasted_iota(jnp.int32, (tq, 1), 0)
    cidx = lax.broadcasted_iota(jnp.int32, (1, NCp), 1)
    cend = S_CMP * cidx + (L_CMP - 1)
    mask = (cend <= qpos) & (cidx < nc)
    dist = (qpos - cend).astype(F32)
    o, imp_c = _attend_stacked(s, dist, mask, vc_ref[0], slopes_ref, g, HG, tq)
    _unstack_store(o_ref, o, HG, tq)

    crow = lax.broadcasted_iota(jnp.int32, (NCp, NSBp), 0)
    jcol = lax.broadcasted_iota(jnp.int32, (NCp, NSBp), 1)
    overlap = ((S_CMP * crow < L_SEL * (jcol + 1)) & (S_CMP * crow + L_CMP > L_SEL * jcol)
               & (crow < nc)).astype(F32)
    imp = jnp.dot(imp_c, overlap, precision=lax.Precision.HIGHEST, preferred_element_type=F32)
    lane = lax.broadcasted_iota(jnp.int32, (tq, NSBp), 1)
    cur = jnp.right_shift(qpos, int(math.log2(L_SEL)))
    forced = ((lane == 0) | (lane == cur) | (lane == cur - 1)).astype(F32)
    score = jnp.where(lane <= cur, imp + FORCE_BONUS * forced, NEG)
    score = jnp.where(lane < nsb, score, -3e38)

    if tq % LANES == 0 and NSBp == LANES:
        nr = -(-nsb // SUBLANES) * SUBLANES
        st = score.T[:nr]
        sub = lax.broadcasted_iota(jnp.int32, (nr, tq), 0)
        cnt = jnp.zeros((nr, tq), F32)
        for i in range(nsb):
            row = st[i:i + 1, :]
            beats = (row > st) | ((row == st) & (sub > i))
            cnt = cnt + jnp.where(beats, 1.0, 0.0)
        sel_t = jnp.where((cnt < TOPK_SEL) & (st > 0.5 * NEG), 1.0, 0.0)
        selm_ref[0, 0] = jnp.concatenate([sel_t, jnp.zeros((NSBp - nr, tq), F32)], axis=0).T
    else:
        cnt = jnp.zeros((tq, NSBp), F32)
        for i in range(nsb):
            col = score[:, i:i + 1]
            beats = (col > score) | ((col == score) & (lane > i))
            cnt = cnt + jnp.where(beats, 1.0, 0.0)
        sel = (cnt < TOPK_SEL) & (score > 0.5 * NEG)
        selm_ref[0, 0] = sel.astype(F32)


def nsa_cmp(proj, kvc, slopes, *, tq, nc, nsb, pos0):
    B, T, _ = proj.shape
    HG = slopes.shape[0] // G_KV
    NCp = kvc.shape[1]
    NSBp = -(-nsb // LANES) * LANES
    gw = HG * HEAD_B
    return pl.pallas_call(
        functools.partial(_nsa_cmp_kernel, tq=tq, HG=HG, nc=nc, nsb=nsb, pos0=pos0),
        grid=(B, G_KV, T // tq),
        in_specs=[pl.BlockSpec(memory_space=pltpu.SMEM),
                  pl.BlockSpec((1, tq, gw), lambda b, g, t: (b, t, g)),
                  pl.BlockSpec((1, NCp, HEAD_B), lambda b, g, t: (b, 0, g)),
                  pl.BlockSpec((1, NCp, HEAD_B), lambda b, g, t: (b, 0, G_KV + g))],
        out_specs=[pl.BlockSpec((1, tq, gw), lambda b, g, t: (b, t, g)),
                   pl.BlockSpec((1, 1, tq, NSBp), lambda b, g, t: (b, g, t, 0))],
        out_shape=[jax.ShapeDtypeStruct((B, T, G_KV * gw), F32),
                   jax.ShapeDtypeStruct((B, G_KV, T, NSBp), F32)],
        compiler_params=_cparams(("parallel", "parallel", "parallel")),
        name="nsa_cmp",
    )(slopes, proj, kvc.reshape(B, NCp, -1), kvc.reshape(B, NCp, -1))


LOG2E = 1.4426950408889634


def _bf16_part(x):
    return x.astype(BF16).astype(F32)


def _alibi_lhs(q, slope_col):
    c = slope_col * LOG2E
    c1 = _bf16_part(c)
    c2 = _bf16_part(c - c1)
    c3 = _bf16_part(c - c1 - c2)
    lane = lax.broadcasted_iota(jnp.int32, q.shape, 1)
    extra = jnp.where((lane == 0) | (lane == 3), c1,
                      jnp.where((lane == 1) | (lane == 4), c2, jnp.where((lane == 2) | (lane == 5), c3, 0.0)))
    return jnp.concatenate([q, extra], axis=1).astype(BF16)


def _alibi_rhs(k, k0):
    pos = k0 + lax.broadcasted_iota(jnp.int32, k.shape, 0)
    lane = lax.broadcasted_iota(jnp.int32, k.shape, 1)
    hi = jnp.bitwise_and(pos, -L_SEL)
    extra = jnp.where(lane < 3, hi, jnp.where(lane < 6, pos - hi, 0)).astype(F32)
    return jnp.concatenate([k, extra], axis=1).astype(BF16)


def _with_ones(v):
    lane = lax.broadcasted_iota(jnp.int32, v.shape, 1)
    return jnp.concatenate([v, jnp.where(lane == 0, 1.0, 0.0)], axis=1).astype(BF16)


def _silu(x):
    return x * _sigmoid(x)


def _nsa_selwin_prompt_kernel(slopes_ref, q_ref, selm_ref, ks_ref, vs_ref, kw_ref, vw_ref,
                              ocmp_ref, zc_ref, zs_ref, zw_ref, gate_ref, o_ref, s_scr, *, tq, HG, T, WS, SEG):
    g = pl.program_id(1)
    qt = pl.program_id(2)
    NSBp = selm_ref.shape[3]
    R = HG * tq
    qpos = qt * tq + lax.broadcasted_iota(jnp.int32, (tq, 1), 0)
    slope_col = jnp.concatenate([jnp.full((tq, 1), slopes_ref[g * HG + h], F32) for h in range(HG)], axis=0)
    tile_heads = lambda x: jnp.concatenate([x] * HG, axis=0)
    q2 = _alibi_lhs(_stack_heads(q_ref[0] * (HEAD_B ** -0.5 * LOG2E), HG), slope_col)

    selm_b = selm_ref[0, 0].astype(BF16)
    nseg = (qt * tq + tq + SEG - 1) // SEG

    def seg_scores(si, m):
        k0 = pl.multiple_of(si * SEG, SEG)
        kcol = k0 + lax.broadcasted_iota(jnp.int32, (NSBp, SEG), 1)
        expand = (jnp.right_shift(kcol, int(math.log2(L_SEL)))
                  == lax.broadcasted_iota(jnp.int32, (NSBp, SEG), 0)).astype(BF16)
        in_blk = jnp.dot(selm_b, expand, preferred_element_type=F32) > 0.5
        kpos = k0 + lax.broadcasted_iota(jnp.int32, (1, SEG), 1)
        mask = tile_heads(in_blk & (kpos <= qpos))
        s = lax.dot_general(q2, _alibi_rhs(ks_ref[0, pl.ds(k0, SEG), :], k0), NT_DIMS, preferred_element_type=F32)
        s = jnp.where(mask, s, NEG)
        s_scr[si] = s
        return jnp.maximum(m, jnp.max(s, axis=-1, keepdims=True))

    m = lax.fori_loop(0, nseg, seg_scores, jnp.full((R, 1), NEG, F32))

    def seg_pv(si, acc):
        k0 = pl.multiple_of(si * SEG, SEG)
        e = jnp.exp2(s_scr[si] - m).astype(BF16)
        return acc + jnp.dot(e, _with_ones(vs_ref[0, pl.ds(k0, SEG), :]), preferred_element_type=F32)

    acc = lax.fori_loop(0, nseg, seg_pv, jnp.zeros((R, 2 * HEAD_B), F32))
    o = acc[:, :HEAD_B] / acc[:, HEAD_B:HEAD_B + 1]

    start = pl.multiple_of(jnp.clip(qt * tq - WINDOW, 0, T - WS), SUBLANES)
    distw = qpos - (start + lax.broadcasted_iota(jnp.int32, (1, WS), 1))
    maskw = tile_heads((distw >= 0) & (distw < WINDOW))
    sw = lax.dot_general(q2, _alibi_rhs(kw_ref[0, pl.ds(start, WS), :], start), NT_DIMS, preferred_element_type=F32)
    sw = jnp.where(maskw, sw, NEG)
    ew = jnp.exp2(sw - jnp.max(sw, axis=-1, keepdims=True)).astype(BF16)
    accw = jnp.dot(ew, _with_ones(vw_ref[0, pl.ds(start, WS), :]), preferred_element_type=F32)
    ow = accw[:, :HEAD_B] / accw[:, HEAD_B:HEAD_B + 1]

    HB = G_KV * HG
    gates = _sigmoid(gate_ref[0])
    lane = lax.broadcasted_iota(jnp.int32, gates.shape, 1)
    gate_col = lambda idx: jnp.sum(jnp.where(lane == idx, gates, 0.0), axis=1, keepdims=True)
    for h in range(HG):
        hs = slice(h * HEAD_B, (h + 1) * HEAD_B)
        rs = slice(h * tq, (h + 1) * tq)
        hd = g * HG + h
        y = (gate_col(hd) * ocmp_ref[0, :, hs] * _silu(zc_ref[0, :, hs])
             + gate_col(HB + hd) * o[rs] * _silu(zs_ref[0, :, hs])
             + gate_col(2 * HB + hd) * ow[rs] * _silu(zw_ref[0, :, hs]))
        o_ref[0, :, hs] = y.astype(o_ref.dtype)


def nsa_selwin_prompt(proj, rows, selm, o_cmp, slopes, *, tq):
    B, T, _ = proj.shape
    HG = slopes.shape[0] // G_KV
    NSBp = selm.shape[3]
    gw = HG * HEAD_B
    CB = G_KV * gw
    WS = min(T, WINDOW + tq)
    SEG = min(T, 512)
    assert T % SEG == 0
    kv_spec = lambda c: pl.BlockSpec((1, T, HEAD_B), lambda b, g, t: (b, 0, c * G_KV + g))
    head_spec = lambda blk: pl.BlockSpec((1, tq, gw), lambda b, g, t: (b, t, blk * G_KV + g))
    return pl.pallas_call(
        functools.partial(_nsa_selwin_prompt_kernel, tq=tq, HG=HG, T=T, WS=WS, SEG=SEG),
        grid=(B, G_KV, T // tq),
        in_specs=[pl.BlockSpec(memory_space=pltpu.SMEM),
                  head_spec(0),
                  pl.BlockSpec((1, 1, tq, NSBp), lambda b, g, t: (b, g, t, 0)),
                  kv_spec(2), kv_spec(3), kv_spec(4), kv_spec(5),
                  head_spec(0), head_spec(1), head_spec(2), head_spec(3),
                  pl.BlockSpec((1, tq, LANES), lambda b, g, t: (b, t, 4 * CB // LANES))],
        out_specs=head_spec(0),
        out_shape=jax.ShapeDtypeStruct((B, T, CB), BF16),
        scratch_shapes=[pltpu.VMEM((T // SEG, HG * tq, SEG), F32)],
        compiler_params=_cparams(("parallel", "parallel", "parallel")),
        name="nsa_selwin_prompt",
    )(slopes, proj, selm, rows, rows, rows, rows, o_cmp, proj, proj, proj, proj)


def _nsa_selwin_sample_kernel(pt_ref, slopes_ref, q_ref, selm_ref, selst_ref, *refs, PGS, PS, HG, TQ, pos0, n_new, n_win):
    del pt_ref
    page_refs = refs[:PGS]
    (new_ref, cwin_ref, ocmp_ref, zc_ref, zs_ref, zw_ref, gate_ref, o_ref,
     m_scr, l_scr, acc_scr) = refs[PGS:]
    st = pl.program_id(1)
    NSBp = selm_ref.shape[3]
    GW = G_KV * HEAD_B
    sel_shift = int(math.log2(L_SEL))

    @pl.when(st == 0)
    def _():
        m_scr[...] = jnp.full(m_scr.shape, NEG, F32)
        l_scr[...] = jnp.zeros(l_scr.shape, F32)
        acc_scr[...] = jnp.zeros(acc_scr.shape, F32)

    qpos = pos0 + lax.broadcasted_iota(jnp.int32, (TQ, 1), 0)
    lane_j = lax.broadcasted_iota(jnp.int32, (TQ, NSBp), 1)
    tile_heads = lambda x: jnp.concatenate([x] * HG, axis=0)

    def sel_col(selm_g, j):
        return jnp.sum(jnp.where(lane_j == j, selm_g, 0.0), axis=1, keepdims=True)

    def online_update(s, mask, v):
        gs = range(G_KV)
        m_old = [m_scr[g] for g in gs]
        m_new = [jnp.maximum(m_old[g], jnp.max(jnp.where(mask[g], s[g], NEG), axis=-1, keepdims=True)) for g in gs]
        e = [jnp.where(mask[g], jnp.exp(s[g] - m_new[g]), 0.0) for g in gs]
        alpha = [jnp.exp(m_old[g] - m_new[g]) for g in gs]
        pv = [_bdot(e[g], v[g]) for g in gs]
        for g in gs:
            l_scr[g] = alpha[g] * l_scr[g] + jnp.sum(e[g], axis=-1, keepdims=True)
            acc_scr[g] = alpha[g] * acc_scr[g] + pv[g]
            m_scr[g] = m_new[g]

    NK = PGS * PS
    kpos = st * NK + lax.broadcasted_iota(jnp.int32, (1, NK), 1)
    expand = (jnp.right_shift(lax.broadcasted_iota(jnp.int32, (LANES, NK), 1), sel_shift)
              == lax.broadcasted_iota(jnp.int32, (LANES, NK), 0)).astype(BF16)
    in_blk_all = jnp.dot(selst_ref[0, 0].astype(BF16), expand, preferred_element_type=F32)
    dist = qpos - kpos
    distf = tile_heads(dist.astype(F32))
    gs = range(G_KV)
    q_st = [_stack_heads(q_ref[0, :, g * HG * HEAD_B:(g + 1) * HG * HEAD_B] * (HEAD_B ** -0.5), HG).astype(BF16)
            for g in gs]
    slope_col = [jnp.concatenate([jnp.full((TQ, 1), slopes_ref[g * HG + h], F32) for h in range(HG)], axis=0)
                 for g in gs]
    slabs = [pltpu.einshape("rcd->crd", page_refs[i][0].reshape(PS, 2 * G_KV, HEAD_B)) for i in range(PGS)]
    k = [jnp.concatenate([slabs[i][g] for i in range(PGS)], axis=0) for g in gs]
    v = [jnp.concatenate([slabs[i][G_KV + g] for i in range(PGS)], axis=0) for g in gs]
    mask = [tile_heads((in_blk_all[g * TQ:(g + 1) * TQ] > 0.5) & (dist >= 0)) for g in gs]
    s = [_bdot_nt(q_st[g], k[g]) - slope_col[g] * distf for g in gs]
    online_update(s, mask, v)

    @pl.when(st == pl.num_programs(1) - 1)
    def _():
        NN = new_ref.shape[1]
        rnew = lax.broadcasted_iota(jnp.int32, (1, NN), 1)
        kpos_n = pos0 + rnew
        dist_n = qpos - kpos_n
        ok_n = (rnew < n_new) & (dist_n >= 0)
        jn = pos0 >> sel_shift
        kpos_w = pos0 - n_win + lax.broadcasted_iota(jnp.int32, (1, n_win), 1)
        dist_w = qpos - kpos_w
        dist_wall = jnp.concatenate([dist_w, dist_n], axis=1)
        mask_wall = jnp.concatenate([(dist_w >= 0) & (dist_w < WINDOW), ok_n & (dist_n < WINDOW)], axis=1)
        kn = [new_ref[0, :, 2 * GW + g * HEAD_B:2 * GW + (g + 1) * HEAD_B] for g in gs]
        vn = [new_ref[0, :, 3 * GW + g * HEAD_B:3 * GW + (g + 1) * HEAD_B] for g in gs]
        mask_n = [tile_heads((sel_col(selm_ref[0, g], jn) > 0.5) & ok_n) for g in gs]
        dist_nf = tile_heads(dist_n.astype(F32))
        s_n = [_bdot_nt(q_st[g], kn[g]) - slope_col[g] * dist_nf for g in gs]
        online_update(s_n, mask_n, vn)
        kw = [jnp.concatenate([cwin_ref[0, :, g * HEAD_B:(g + 1) * HEAD_B],
                               new_ref[0, :, 4 * GW + g * HEAD_B:4 * GW + (g + 1) * HEAD_B]], axis=0) for g in gs]
        vw = [jnp.concatenate([cwin_ref[0, :, GW + g * HEAD_B:GW + (g + 1) * HEAD_B],
                               new_ref[0, :, 5 * GW + g * HEAD_B:5 * GW + (g + 1) * HEAD_B]], axis=0) for g in gs]
        dist_wf = tile_heads(dist_wall.astype(F32))
        mask_w = tile_heads(mask_wall)
        sw = [_bdot_nt(q_st[g], kw[g]) - slope_col[g] * dist_wf for g in gs]
        pw = [_masked_softmax_rows(sw[g], mask_w) for g in gs]
        ow = [_bdot(pw[g], vw[g]) for g in gs]
        HB = G_KV * HG
        gates = _sigmoid(gate_ref[0])
        for g in gs:
            l = l_scr[g]
            o = acc_scr[g] / jnp.where(l > 0.0, l, 1.0)
            for h in range(HG):
                hd = g * HG + h
                hs = slice(hd * HEAD_B, (hd + 1) * HEAD_B)
                rs = slice(h * TQ, (h + 1) * TQ)
                y = (gates[:, hd:hd + 1] * ocmp_ref[0, :, hs] * _silu(zc_ref[0, :, hs])
                     + gates[:, HB + hd:HB + hd + 1] * o[rs] * _silu(zs_ref[0, :, hs])
                     + gates[:, 2 * HB + hd:2 * HB + hd + 1] * ow[g][rs] * _silu(zw_ref[0, :, hs]))
                o_ref[0, :, hs] = y.astype(o_ref.dtype)


def nsa_selwin_sample(proj, selm, o_cmp, pool5, table, new_rows, cwin, slopes, *, pos0, n_new):
    B, TQ, _ = proj.shape
    HG = slopes.shape[0] // G_KV
    NP, CPP = pool5.shape[:2]
    PS = CPP * S_CMP
    n_pages = table.shape[1]
    NSBp = selm.shape[3]
    PGS = max(d for d in (8, 4, 2, 1) if n_pages % d == 0)
    CB = G_KV * HG * HEAD_B
    GW = G_KV * HEAD_B
    NN = new_rows.shape[1]
    n_win = cwin.shape[1]
    assert pos0 % L_SEL == 0 and n_new <= L_SEL and pos0 == n_pages * PS

    def page_map(i):
        return lambda b, s, pt: (pt[b, s * PGS + i], 0, 0, 1, 0)

    n_steps = n_pages // PGS
    bps = PGS * PS // L_SEL
    assert bps <= LANES
    selst = selm[:, :, :, :n_steps * bps].reshape(B, G_KV, TQ, n_steps, bps).transpose(0, 3, 1, 2, 4)
    selst = jnp.pad(selst.reshape(B, n_steps, G_KV * TQ, bps), ((0, 0), (0, 0), (0, 0), (0, LANES - bps)))

    const = lambda b, s, pt: (b, 0, 0)
    wide = lambda blk: pl.BlockSpec((1, TQ, CB), lambda b, s, pt: (b, 0, blk))
    grid_spec = pltpu.PrefetchScalarGridSpec(
        num_scalar_prefetch=1,
        grid=(B, n_steps),
        in_specs=[pl.BlockSpec(memory_space=pltpu.SMEM),
                  pl.BlockSpec((1, TQ, CB), const),
                  pl.BlockSpec((1, G_KV, TQ, NSBp), lambda b, s, pt: (b, 0, 0, 0)),
                  pl.BlockSpec((1, 1, G_KV * TQ, LANES), lambda b, s, pt: (b, s, 0, 0))]
                 + [pl.BlockSpec((1, CPP, S_CMP, 2 * G_KV, HEAD_B), page_map(i)) for i in range(PGS)]
                 + [pl.BlockSpec((1, NN, 6 * GW), const),
                    pl.BlockSpec((1, n_win, 2 * GW), const),
                    wide(0), wide(1), wide(2), wide(3),
                    pl.BlockSpec((1, TQ, LANES), lambda b, s, pt: (b, 0, 4 * CB // LANES))],
        out_specs=wide(0),
        scratch_shapes=[pltpu.VMEM((G_KV, HG * TQ, 1), F32),
                        pltpu.VMEM((G_KV, HG * TQ, 1), F32),
                        pltpu.VMEM((G_KV, HG * TQ, HEAD_B), F32)],
    )
    return pl.pallas_call(
        functools.partial(_nsa_selwin_sample_kernel, PGS=PGS, PS=PS, HG=HG, TQ=TQ, pos0=pos0,
                          n_new=n_new, n_win=n_win),
        grid_spec=grid_spec,
        out_shape=jax.ShapeDtypeStruct((B, TQ, CB), F32),
        compiler_params=_cparams(("parallel", "arbitrary")),
        name="nsa_selwin_sample",
    )(table, slopes, proj, selm, selst, *([pool5] * PGS), new_rows, cwin, o_cmp, proj, proj, proj, proj)


def _rwkv_layer(h, x_prev, s0, i, W, B, T):
    N, D = h.shape
    g = W["norm_g"][i]
    rkvg = rwkv_in(h, x_prev, g, W["mu_a"][i], W["w_in_a"], i, T)
    CA = rkvg.shape[-1]
    lw, a = rwkv_lora(h, x_prev, g, W["mu_a"][i], W["w_lora_w1"][i], W["w_lora_w2"][i], W["a_lora1"][i],
                      W["a_lora2"][i], W["w0_a"][i], W["a0_a"][i], T)
    pvec = jnp.stack([W["k_k"][i], W["k_a"][i], W["r_k"][i].reshape(CA), W["ln_x_w"][i], W["ln_x_b"][i]])
    o, s_fin = rwkv_scan(rkvg.reshape(4, B, T, CA), lw.reshape(B, T, CA), a.reshape(B, T, CA), pvec, s0)
    last = rmsnorm(h.reshape(B, T, D)[:, -1], g)
    return o.reshape(N, CA), s_fin, last


def _nsa_layer(h, jb, shared, W, slopes, B, T, norm_g):
    N, D = h.shape
    CB = W["w_out_b"].shape[1]
    proj3 = norm_mm(h, norm_g, W["w_in_b"], (jb,)).reshape(B, T, -1)
    if shared["past"] is None:
        o_cmp, selm = nsa_cmp(proj3, shared["kvc"], slopes, tq=min(T, 256), nc=shared["nc"],
                              nsb=shared["nsb"], pos0=0)
        o = nsa_selwin_prompt(proj3, shared["rows"], selm, o_cmp, slopes, tq=min(T, 256))
    else:
        TQ = SUBLANES
        projp = jnp.pad(proj3, ((0, 0), (0, TQ - T), (0, 0)))
        o_cmp, selm = nsa_cmp(projp, shared["kvc"], slopes, tq=TQ, nc=shared["nc"],
                              nsb=shared["nsb"], pos0=shared["pos0"])
        pool, table, cwin = shared["past"]
        o = nsa_selwin_sample(projp, selm, o_cmp, pool, table, shared["new_rows"], cwin, slopes,
                              pos0=shared["pos0"], n_new=T)[:, :T]
    return o.reshape(N, CB)


def _trunk(x, p, pos0, wkv0, shift0, past, W, slopes):
    B, T, D = x.shape
    N = B * T
    depth = p.shape[0]
    n_a = W["w_in_a"].shape[0]
    GW = G_KV * HEAD_B
    h = x.reshape(N, D)
    wkv_new, shift_new = [], []
    shared, kv_rows, win_state = None, None, None
    for i in range(depth):
        if i < n_a:
            o, s_fin, last = _rwkv_layer(h, shift0[i], wkv0[i], i, W, B, T)
            wkv_new.append(s_fin)
            shift_new.append(last)
            h = out_ple(o, h, p[i].reshape(N, -1), W["w_out_a"], i, W["w_ple"], W["w_ple_gate"], i)
        else:
            o = _nsa_layer(h, i - n_a, shared, W, slopes, B, T, W["norm_g"][i])
            h = out_ple(o, h, p[i].reshape(N, -1), W["w_out_b"], i - n_a, W["w_ple"], W["w_ple_gate"], i)
        if i == n_a - 1:
            rows = norm_mm(h, W["kv_norm_g"], W["w_kv"]).reshape(B, T, 6 * GW)
            kv_rows = rows[:, :, :4 * GW].reshape(B, T, 4, G_KV, HEAD_B)
            win_new = rows[:, :, 4 * GW:].reshape(B, T, 2, G_KV, HEAD_B)
            if past is None:
                PS = 128
                pool = rows.reshape(B * T // PS, PS // S_CMP, S_CMP, 6 * G_KV, HEAD_B)
                table = jnp.arange(B * T // PS, dtype=jnp.int32).reshape(B, T // PS)
                t_all = T
                win_all = win_new
                shared = {"past": None, "rows": rows}
            else:
                pool, table, cwin = past
                PS = pool.shape[1] * S_CMP
                t_all = pos0 + T
                win_all = jnp.concatenate([cwin.reshape(B, -1, 2, G_KV, HEAD_B), win_new], axis=1)
                NN = LANES
                shared = {"past": past, "new_rows": jnp.pad(rows, ((0, 0), (0, NN - T), (0, 0)))}
            win_state = win_all[:, win_all.shape[1] - min(WINDOW, pos0 + T):]
            nc = (t_all - L_CMP) // S_CMP + 1
            assert nc < table.shape[1] * PS // S_CMP
            kvc = compress_kv(pool, table, W["pe_cmp"], W["w_cmp1"], W["w_cmp2"])
            shared.update(kvc=kvc, nc=nc, nsb=max(-(-t_all // L_SEL), TOPK_SEL), pos0=pos0)
    y = rmsnorm(h, W["final_norm_g"]).reshape(B, T, D)
    return y, jnp.stack(wkv_new), jnp.stack(shift_new), kv_rows, win_state


def kernel(x_prompt, x_sample, state_wkv, state_shift, cache_kv, cache_win_kv, page_table, p_prompt, p_sample, norm_g, mu_a, w_in_a, w_lora_w1, w_lora_w2, w0_a, a_lora1, a_lora2, a0_a, k_k, k_a, r_k, ln_x_w, ln_x_b, w_out_a, w_in_b, w_out_b, kv_norm_g, w_kv, pe_cmp, w_cmp1, w_cmp2, w_ple, w_ple_gate, final_norm_g):
    bf = lambda w: w.astype(BF16)
    CA = w_out_a.shape[1]
    W = dict(norm_g=norm_g, mu_a=mu_a, w_in_a=bf(w_in_a), w_lora_w1=bf(w_lora_w1), w_lora_w2=bf(w_lora_w2),
             w0_a=w0_a, a_lora1=bf(a_lora1), a_lora2=bf(a_lora2), a0_a=a0_a, k_k=k_k, k_a=k_a,
             r_k=r_k, ln_x_w=ln_x_w, ln_x_b=ln_x_b,
             w_out_a=bf(w_out_a), w_in_b=bf(w_in_b), w_out_b=bf(w_out_b), kv_norm_g=kv_norm_g, w_kv=bf(w_kv),
             pe_cmp=pe_cmp, w_cmp1=bf(w_cmp1), w_cmp2=bf(w_cmp2), w_ple=bf(w_ple), w_ple_gate=bf(w_ple_gate),
             final_norm_g=final_norm_g)
    HB = w_out_b.shape[1] // HEAD_B
    slopes = 2.0 ** (-8.0 * jnp.arange(1, HB + 1, dtype=F32) / HB)
    bp = x_prompt.shape[0]
    n_a = w_in_a.shape[0]
    D = x_prompt.shape[-1]
    wkv0 = jnp.zeros((n_a, bp, CA // HEAD_A, HEAD_A, HEAD_A), F32)
    shift0 = jnp.zeros((n_a, bp, D), F32)
    y_p, wkv_p, shift_p, kv_p, win_p = _trunk(x_prompt, p_prompt, 0, wkv0, shift0, None, W, slopes)
    db, n_pages = page_table.shape
    NP, PS = cache_kv.shape[:2]
    pool5 = cache_kv.reshape(NP, PS // S_CMP, S_CMP, -1, HEAD_B)
    past = (pool5, page_table, cache_win_kv.reshape(db, cache_win_kv.shape[1], -1))
    y_s, wkv_s, shift_s, kv_s, win_s = _trunk(x_sample, p_sample, n_pages * PS, state_wkv, state_shift, past, W, slopes)
    return (y_p, y_s, wkv_p, shift_p, kv_p, win_p, wkv_s, shift_s, kv_s, win_s)
```

```python
import functools
import math

import jax
import jax.numpy as jnp
from jax import lax
from jax.experimental import pallas as pl
from jax.experimental.pallas import tpu as pltpu

F32 = jnp.float32
BF16 = jnp.bfloat16

HEAD_A = 64
GN_EPS = 64e-5
HEAD_B = 128
G_KV = 4
L_CMP = 32
S_CMP = 16
L_SEL = 64
TOPK_SEL = 16
WINDOW = 512
RMS_EPS = 1e-6
NEG = -1e30
FORCE_BONUS = 1e4

LANES = 128
SUBLANES = 8
VMEM_LIMIT = 56 * 1024 * 1024

SCAN_NH = 2
SCAN_C = 64
SCAN_GP = 16

NT_DIMS = (((1,), (1,)), ((), ()))
TN_DIMS = (((0,), (0,)), ((), ()))


def _cparams(sem):
    return pltpu.CompilerParams(dimension_semantics=sem, vmem_limit_bytes=VMEM_LIMIT)


def _bdot(a, b):
    return jnp.dot(a.astype(BF16), b.astype(BF16), preferred_element_type=F32)


def _bdot_nt(a, b):
    return lax.dot_general(a.astype(BF16), b.astype(BF16), NT_DIMS, preferred_element_type=F32)


def _bdot_tn(a, b):
    return lax.dot_general(a.astype(BF16), b.astype(BF16), TN_DIMS, preferred_element_type=F32)


def _rms_kernel(x_ref, g_ref, o_ref):
    x = x_ref[...]
    ms = jnp.mean(x * x, axis=-1, keepdims=True)
    o_ref[...] = x * lax.rsqrt(ms + RMS_EPS) * g_ref[...]


def rmsnorm(x, g):
    M, D = x.shape
    tm = min(M, 256)
    return pl.pallas_call(
        _rms_kernel,
        grid=(pl.cdiv(M, tm),),
        in_specs=[pl.BlockSpec((tm, D), lambda i: (i, 0)),
                  pl.BlockSpec((1, D), lambda i: (0, 0))],
        out_specs=pl.BlockSpec((tm, D), lambda i: (i, 0)),
        out_shape=jax.ShapeDtypeStruct((M, D), F32),
        compiler_params=_cparams(("parallel",)),
        name="rmsnorm",
    )(x, g.reshape(1, D))


def _mm_kernel(x_ref, w_ref, o_ref):
    o_ref[...] = jnp.dot(x_ref[...].astype(BF16), w_ref[...], preferred_element_type=F32)


def mm(x, w, widx=()):
    M, K = x.shape
    N = w.shape[-1]
    assert w.shape[-2] == K and len(widx) == w.ndim - 2
    tm = min(M, 1024 if K <= 2048 else 512)
    tn = N if N <= 512 else 512
    nlead = len(widx)
    w_spec = pl.BlockSpec((None,) * nlead + (K, tn), lambda i, j: tuple(widx) + (0, j))
    return pl.pallas_call(
        _mm_kernel,
        grid=(pl.cdiv(M, tm), pl.cdiv(N, tn)),
        in_specs=[pl.BlockSpec((tm, K), lambda i, j: (i, 0)), w_spec],
        out_specs=pl.BlockSpec((tm, tn), lambda i, j: (i, j)),
        out_shape=jax.ShapeDtypeStruct((M, N), F32),
        compiler_params=_cparams(("parallel", "parallel")),
        name="mm",
    )(x, w)


def _norm_rows(x, g):
    return x * lax.rsqrt(jnp.mean(x * x, axis=-1, keepdims=True) + RMS_EPS) * g


def _sigmoid(x):
    return 0.5 + 0.5 * jnp.tanh(0.5 * x)


def _norm_and_shift(h_ref, hprev_ref, xprev_ref, g_ref, i, tm, T):
    g = g_ref[...]
    hn = _norm_rows(h_ref[...], g)
    prev_row = _norm_rows(hprev_ref[SUBLANES - 1:SUBLANES, :], g)
    row = lax.broadcasted_iota(jnp.int32, (tm, 1), 0)
    xs = jnp.where(row == 0, prev_row, pltpu.roll(hn, 1, axis=0))
    if T >= tm:
        assert T % tm == 0
        start = (i * tm) % T == 0
        xs = jnp.where((row == 0) & start, xprev_ref[pl.ds((i * tm) // T, 1), :], xs)
    else:
        assert tm % T == 0
        for bb in range(tm // T):
            xs = jnp.where(row == bb * T, xprev_ref[pl.ds(i * (tm // T) + bb, 1), :], xs)
    return hn, xs


def _rwkv_in_kernel(h_ref, hprev_ref, xprev_ref, g_ref, mu_ref, w_ref, o_ref, hn_scr, xs_scr, *, tm, T):
    i, j, n = pl.program_id(0), pl.program_id(1), pl.program_id(2)
    rc = min(tm, 256)
    chunks = [slice(c * rc, (c + 1) * rc) for c in range(tm // rc)]
    P = SUBLANES

    @pl.when((j == 0) & (n == 0))
    def _():
        g = g_ref[...]
        prev = _norm_rows(hprev_ref[P - 1:P, :], g)
        if T >= tm:
            assert T % tm == 0
            prev = jnp.where((i * tm) % T == 0, xprev_ref[pl.ds((i * tm) // T, 1), :], prev)
        hn_scr[0:P, :] = jnp.broadcast_to(prev, (P, prev.shape[1]))
        for ch in chunks:
            hn_scr[P + ch.start:P + ch.stop, :] = _norm_rows(h_ref[ch, :], g)

    @pl.when(n == 0)
    def _():
        mu = mu_ref[pl.ds(j, 1), :]
        for ch in chunks:
            hn = hn_scr[P + ch.start:P + ch.stop, :]
            xs = hn_scr[P - 1 + ch.start:P - 1 + ch.stop, :]
            if T < tm:
                assert tm % T == 0 and len(chunks) == 1
                row = lax.broadcasted_iota(jnp.int32, (tm, 1), 0)
                for bb in range(tm // T):
                    xs = jnp.where(row == bb * T, xprev_ref[pl.ds(i * (tm // T) + bb, 1), :], xs)
            xs_scr[ch, :] = (hn + (xs - hn) * mu).astype(BF16)

    o_ref[...] = jnp.dot(xs_scr[...], w_ref[...], preferred_element_type=F32)


def _shift_specs(tm, D, nb, ngrid):
    z = (0,) * (ngrid - 1)
    wrap = lambda f: (lambda i, *_: f(i))
    return [pl.BlockSpec((tm, D), wrap(lambda i: (i, 0))),
            pl.BlockSpec((SUBLANES, D), wrap(lambda i: (jnp.maximum(i * (tm // SUBLANES) - 1, 0), 0))),
            pl.BlockSpec((nb, D), wrap(lambda i: (0, 0))),
            pl.BlockSpec((1, D), wrap(lambda i: (0, 0))),
            pl.BlockSpec((6, D), wrap(lambda i: (0, 0)))]


def rwkv_in(h, xprev, g, mu, w, layer, T):
    N, D = h.shape
    C = w.shape[-1]
    tm = min(N, 1024)
    tn = 512
    return pl.pallas_call(
        functools.partial(_rwkv_in_kernel, tm=tm, T=T),
        grid=(N // tm, 4, C // tn),
        in_specs=_shift_specs(tm, D, xprev.shape[0], 3) + [
            pl.BlockSpec((None, None, D, tn), lambda i, j, n: (layer, j, 0, n))],
        out_specs=pl.BlockSpec((None, tm, tn), lambda i, j, n: (j, i, n)),
        out_shape=jax.ShapeDtypeStruct((4, N, C), F32),
        scratch_shapes=[pltpu.VMEM((tm + SUBLANES, D), F32), pltpu.VMEM((tm, D), BF16)],
        compiler_params=_cparams(("parallel", "arbitrary", "arbitrary")),
        name="rwkv_in",
    )(h, h, xprev, g.reshape(1, D), mu, w)


def _rwkv_lora_kernel(h_ref, hprev_ref, xprev_ref, g_ref, mu_ref, lw1_ref, lw2_ref, la1_ref, la2_ref,
                      w0_ref, a0_ref, lw_ref, a_ref, *, tm, T):
    hn, xs = _norm_and_shift(h_ref, hprev_ref, xprev_ref, g_ref, pl.program_id(0), tm, T)
    dx = xs - hn
    x4 = (hn + dx * mu_ref[4:5, :]).astype(BF16)
    x5 = (hn + dx * mu_ref[5:6, :]).astype(BF16)
    t4 = jnp.tanh(jnp.dot(x4, lw1_ref[...], preferred_element_type=F32)).astype(BF16)
    x = w0_ref[...] + jnp.dot(t4, lw2_ref[...], preferred_element_type=F32)
    lw_ref[...] = -math.exp(-0.5) * _sigmoid(x)
    t5 = jnp.dot(x5, la1_ref[...], preferred_element_type=F32).astype(BF16)
    a_ref[...] = _sigmoid(a0_ref[...] + jnp.dot(t5, la2_ref[...], preferred_element_type=F32))


def rwkv_lora(h, xprev, g, mu, lw1, lw2, la1, la2, w0, a0, T):
    N, D = h.shape
    R, C = lw2.shape
    tm = min(N, 256)
    full = lambda shape: pl.BlockSpec(shape, lambda i: (0,) * len(shape))
    o_spec = pl.BlockSpec((tm, C), lambda i: (i, 0))
    return pl.pallas_call(
        functools.partial(_rwkv_lora_kernel, tm=tm, T=T),
        grid=(N // tm,),
        in_specs=_shift_specs(tm, D, xprev.shape[0], 1) + [
            full((D, R)), full((R, C)), full((D, R)), full((R, C)), full((1, C)), full((1, C))],
        out_specs=[o_spec, o_spec],
        out_shape=[jax.ShapeDtypeStruct((N, C), F32)] * 2,
        compiler_params=_cparams(("parallel",)),
        name="rwkv_lora",
    )(h, h, xprev, g.reshape(1, D), mu, lw1, lw2, la1, la2, w0.reshape(1, C), a0.reshape(1, C))


def _norm_mm_kernel(h_ref, g_ref, w_ref, o_ref, xs_scr):
    @pl.when(pl.program_id(1) == 0)
    def _():
        rc = min(h_ref.shape[0], 256)

        def norm_chunk(c, carry):
            rows = pl.ds(pl.multiple_of(c * rc, rc), rc)
            xs_scr[rows, :] = _norm_rows(h_ref[rows, :], g_ref[...]).astype(BF16)
            return carry

        lax.fori_loop(0, h_ref.shape[0] // rc, norm_chunk, 0)

    o_ref[...] = jnp.dot(xs_scr[...], w_ref[...], preferred_element_type=F32)


def norm_mm(h, g, w, widx=()):
    N, D = h.shape
    NO = w.shape[-1]
    tm = min(N, 1024)
    tn = 1024
    nlead = len(widx)
    return pl.pallas_call(
        _norm_mm_kernel,
        grid=(N // tm, pl.cdiv(NO, tn)),
        in_specs=[pl.BlockSpec((tm, D), lambda i, n: (i, 0)),
                  pl.BlockSpec((1, D), lambda i, n: (0, 0)),
                  pl.BlockSpec((None,) * nlead + (D, tn), lambda i, n: tuple(widx) + (0, n))],
        out_specs=pl.BlockSpec((tm, tn), lambda i, n: (i, n)),
        out_shape=jax.ShapeDtypeStruct((N, NO), F32),
        scratch_shapes=[pltpu.VMEM((tm, D), BF16)],
        compiler_params=_cparams(("parallel", "arbitrary")),
        name="norm_mm",
    )(h, g.reshape(1, D), w)


def _mm_res_kernel(x_ref, h_ref, w_ref, o_ref):
    o_ref[...] = h_ref[...] + jnp.dot(x_ref[...].astype(BF16), w_ref[...], preferred_element_type=F32)


def _ple_gate_kernel(h1_ref, h1t_ref, p_ref, wp_ref, wg_ref, o_ref, xb_scr):
    @pl.when(pl.program_id(1) == 0)
    def _():
        rc = min(h1_ref.shape[0], 256)

        def cast_chunk(c, carry):
            rows = pl.ds(pl.multiple_of(c * rc, rc), rc)
            xb_scr[rows, :] = h1_ref[rows, :].astype(BF16)
            return carry

        lax.fori_loop(0, h1_ref.shape[0] // rc, cast_chunk, 0)

    gate = jnp.dot(xb_scr[...], wg_ref[...], preferred_element_type=F32)
    ple = jnp.dot(p_ref[...].astype(BF16), wp_ref[...], preferred_element_type=F32)
    o_ref[...] = h1t_ref[...] + ple * _sigmoid(gate)


def out_ple(x, h, p, w_out, oidx, w_ple, w_gate, layer):
    N, C = x.shape
    D = h.shape[1]
    DP = p.shape[1]
    tm = min(N, 1024)
    tn = 512
    grid = (N // tm, D // tn)
    tile = pl.BlockSpec((tm, tn), lambda i, n: (i, n))
    h1 = pl.pallas_call(
        _mm_res_kernel,
        grid=grid,
        in_specs=[pl.BlockSpec((tm, C), lambda i, n: (i, 0)), tile,
                  pl.BlockSpec((None, C, tn), lambda i, n: (oidx, 0, n))],
        out_specs=tile,
        out_shape=jax.ShapeDtypeStruct((N, D), F32),
        compiler_params=_cparams(("parallel", "parallel")),
        name="mm_res",
    )(x, h, w_out)
    return pl.pallas_call(
        _ple_gate_kernel,
        grid=grid,
        in_specs=[pl.BlockSpec((tm, D), lambda i, n: (i, 0)), tile,
                  pl.BlockSpec((tm, DP), lambda i, n: (i, 0)),
                  pl.BlockSpec((None, DP, tn), lambda i, n: (layer, 0, n)),
                  pl.BlockSpec((None, D, tn), lambda i, n: (layer, 0, n))],
        out_specs=tile,
        out_shape=jax.ShapeDtypeStruct((N, D), F32),
        scratch_shapes=[pltpu.VMEM((tm, D), BF16)],
        compiler_params=_cparams(("parallel", "arbitrary")),
        name="ple_gate",
    )(h1, h1, p, w_ple, w_gate)


def _scan_kernel(r_ref, k_ref, v_ref, zg_ref, lw_ref, a_ref, pv_ref, s0_ref, o_ref, sfin_ref, s_scr, *, C, NH, GP):
    L = NH * HEAD_A
    NC = NH * C
    ci = pl.program_id(2)

    @pl.when(ci == 0)
    def _():
        for gp in range(GP):
            rows = []
            for hh in range(NH):
                pieces = [s0_ref[0, gp * NH + hh] if h2 == hh else jnp.zeros((HEAD_A, HEAD_A), F32)
                          for h2 in range(NH)]
                rows.append(jnp.concatenate(pieces, axis=1))
            s_scr[gp] = jnp.concatenate(rows, axis=0)

    row_c = lax.broadcasted_iota(jnp.int32, (C, NC), 0)
    col_s = lax.broadcasted_iota(jnp.int32, (C, NC), 1) % C
    tri_strict = col_s < row_c
    tri_incl = (lax.broadcasted_iota(jnp.int32, (C, 2 * NC), 1) % C
                <= lax.broadcasted_iota(jnp.int32, (C, 2 * NC), 0))
    st_mask = (lax.broadcasted_iota(jnp.int32, (NC, L), 0) // C
               == lax.broadcasted_iota(jnp.int32, (NC, L), 1) // HEAD_A)
    bd_mask = (lax.broadcasted_iota(jnp.int32, (NC, NC), 0) // C
               == lax.broadcasted_iota(jnp.int32, (NC, NC), 1) // C)
    head_mask = (lax.broadcasted_iota(jnp.int32, (L, L), 0) // HEAD_A
                 == lax.broadcasted_iota(jnp.int32, (L, L), 1) // HEAD_A)

    def st(x):
        return jnp.where(st_mask, jnp.concatenate([x] * NH, axis=0), 0.0)

    def bd(w):
        return jnp.where(bd_mask, jnp.concatenate([w] * NH, axis=0), 0.0)

    n_double = int(math.log2(C))
    each = lambda f, *cols: [f(*xs) for xs in zip(*cols)]
    sls = [slice(gp * L, (gp + 1) * L) for gp in range(GP)]
    head_of_lane = lax.broadcasted_iota(jnp.int32, (1, L), 1) // HEAD_A

    def hsum(x):
        out = None
        for hh in range(NH):
            sh = jnp.sum(jnp.where(head_of_lane == hh, x, 0.0), axis=-1, keepdims=True)
            out = sh if out is None else jnp.where(head_of_lane == hh, sh, out)
        return out

    k_k, k_a, r_k, ln_w, ln_b = ([pv_ref[n:n + 1, sl] for sl in sls] for n in range(5))
    lw = [lw_ref[0, :, sl] for sl in sls]
    a_sig = [a_ref[0, :, sl] for sl in sls]
    r = [r_ref[0, :, sl] for sl in sls]
    k_raw = [k_ref[0, :, sl] for sl in sls]
    v = [v_ref[0, :, sl] for sl in sls]
    kk = each(lambda x, w: x * w, k_raw, k_k)
    kk = each(lambda x: x * lax.rsqrt(jnp.maximum(hsum(x * x), 1e-24)), kk)
    k = each(lambda x, a, w: x * (1.0 + (a - 1.0) * w), k_raw, a_sig, k_a)
    lw_hi = each(lambda z: z.astype(BF16), lw)
    lw_lo = each(lambda z, hi: (z - hi.astype(F32)).astype(BF16), lw, lw_hi)
    cum_mat = (lax.broadcasted_iota(jnp.int32, (C, C), 1)
               <= lax.broadcasted_iota(jnp.int32, (C, C), 0)).astype(BF16)
    cum2 = each(lambda hi, lo: jnp.dot(cum_mat, jnp.concatenate([hi, lo], axis=1), preferred_element_type=F32),
                lw_hi, lw_lo)
    cum = [c2[:, :L] + c2[:, L:] for c2 in cum2]
    p_incl = each(jnp.exp, cum)
    p_inv = each(lambda z: jnp.exp(-z), cum)
    at = each(lambda x, c, w: -x * jnp.exp(c - w), kk, cum, lw)
    rt = each(lambda x, p: x * p, r, p_incl)
    bt = each(lambda x, a, p: x * a * p, kk, a_sig, p_inv)
    kt = each(lambda x, p: x * p, k, p_inv)
    S = [s_scr[gp] for gp in range(GP)]
    ar = each(lambda x, y: jnp.concatenate([x, y], axis=0), at, rt)
    bk_st = each(lambda x, y: jnp.concatenate([st(x), st(y)], axis=0), bt, kt)
    Gm = each(_bdot_nt, ar, bk_st)
    w_ab = [jnp.where(tri_strict, g[:C, :NC], 0.0) for g in Gm]
    tm = w_ab
    pw = each(lambda w: _bdot(w, bd(w)), w_ab)
    LH = each(_bdot_nt, ar, S)
    v_st = each(st, v)
    x = [lh[:C] + _bdot(jnp.where(tri_strict, g[:C, NC:], 0.0), vs) for lh, g, vs in zip(LH, Gm, v_st)]
    for it in range(1, n_double):
        if it < n_double - 1:
            both = each(lambda t, p: _bdot(p, jnp.concatenate([bd(t), bd(p)], axis=1)), tm, pw)
            tm = each(lambda t, p, b2: t + p + b2[:, :NC], tm, pw, both)
            pw = [b2[:, NC:] for b2 in both]
        else:
            tm = each(lambda t, p: t + p + _bdot(p, bd(t)), tm, pw)
    u = each(lambda xx, t: xx + _bdot(t, st(xx)), x, tm)
    o = [lh[C:] + _bdot(jnp.where(tri_incl, g[C:], 0.0), jnp.concatenate([st(uu), vs], axis=0))
         for lh, g, uu, vs in zip(LH, Gm, u, v_st)]
    inv_n = 1.0 / HEAD_A
    dev = each(lambda x: x - hsum(x) * inv_n, o)
    gn = each(lambda d, w, b_: d * lax.rsqrt(hsum(d * d) * inv_n + GN_EPS) * w + b_, dev, ln_w, ln_b)
    bonus = each(lambda rr, kx, w, vv: hsum(rr * kx * w) * vv, r, k, r_k, v)
    for sl, y, bo in zip(sls, gn, bonus):
        zg = zg_ref[0, :, sl]
        o_ref[0, :, sl] = ((y + bo) * (zg * _sigmoid(zg))).astype(o_ref.dtype)
    ds = [_bdot_tn(jnp.concatenate([uu, vv], axis=0), jnp.concatenate([b_, k_], axis=0))
          for uu, vv, b_, k_ in zip(u, v, bt, kt)]
    for gp in range(GP):
        s_scr[gp] = (S[gp] + jnp.where(head_mask, ds[gp], 0.0)) * p_incl[gp][C - 1:C, :]

    @pl.when(ci == pl.num_programs(2) - 1)
    def _():
        for gp in range(GP):
            s_all = s_scr[gp]
            for hh in range(NH):
                blk = slice(hh * HEAD_A, (hh + 1) * HEAD_A)
                sfin_ref[0, gp * NH + hh] = s_all[blk, blk]


def rwkv_scan(rkvg, lw, a, pvec, s0):
    _, B, T, CA = rkvg.shape
    H = CA // HEAD_A
    NH, C, GP = SCAN_NH, SCAN_C, SCAN_GP
    assert NH * C == LANES and H % (NH * GP) == 0
    L = NH * HEAD_A
    NG = H // NH
    Tp = -(-T // C) * C
    if Tp != T:
        rkvg = jnp.pad(rkvg, ((0, 0), (0, 0), (0, Tp - T), (0, 0)))
        lw, a = (jnp.pad(z, ((0, 0), (0, Tp - T), (0, 0))) for z in (lw, a))
    seq_spec = pl.BlockSpec((1, C, GP * L), lambda bi, gi, ci: (bi, ci, gi))
    proj_spec = lambda j: pl.BlockSpec((None, 1, C, GP * L), lambda bi, gi, ci: (j, bi, ci, gi))
    st_spec = pl.BlockSpec((1, GP * NH, HEAD_A, HEAD_A), lambda bi, gi, ci: (bi, gi, 0, 0))
    o, s_fin = pl.pallas_call(
        functools.partial(_scan_kernel, C=C, NH=NH, GP=GP),
        grid=(B, NG // GP, Tp // C),
        in_specs=[proj_spec(j) for j in range(4)] + [seq_spec, seq_spec,
                  pl.BlockSpec((5, GP * L), lambda bi, gi, ci: (0, gi)), st_spec],
        out_specs=[seq_spec, st_spec],
        out_shape=[jax.ShapeDtypeStruct((B, Tp, CA), BF16), jax.ShapeDtypeStruct((B, H, HEAD_A, HEAD_A), F32)],
        scratch_shapes=[pltpu.VMEM((GP, L, L), F32)],
        compiler_params=_cparams(("parallel", "parallel", "arbitrary")),
        name="rwkv_scan",
    )(rkvg, rkvg, rkvg, rkvg, lw, a, pvec, s0)
    return o[:, :T], s_fin


def _gelu_tanh(x):
    c = math.sqrt(2.0 / math.pi)
    return 0.5 * x * (1.0 + jnp.tanh(c * (x + 0.044715 * (x * x * x))))


def _compress_kernel(pt_ref, *refs, PGS):
    del pt_ref
    page_refs = refs[:PGS]
    next_ref, pe_ref, w1_ref, w2_ref, out_ref = refs[PGS:]
    CPP = page_refs[0].shape[1]
    NCH = PGS * CPP
    CG = 2 * G_KV
    M = (NCH + 1) * CG

    def rows_of(l, hf):
        pe = pe_ref[hf, l]
        parts = [(page_refs[i][0, :, l] + pe[None]).reshape(CPP * CG, HEAD_B) for i in range(PGS)]
        parts.append(next_ref[0, 0, l] + pe)
        return jnp.concatenate(parts, axis=0)

    top = jnp.zeros((M, 2 * HEAD_B), F32)
    bot = jnp.zeros((M, 2 * HEAD_B), F32)
    for l in range(0, S_CMP, 2):
        wrows = pl.ds(l * HEAD_B, 2 * HEAD_B)
        xt = jnp.concatenate([rows_of(l, 0), rows_of(l + 1, 0)], axis=1).astype(BF16)
        top = top + jnp.dot(xt, w1_ref[0, wrows, :], preferred_element_type=F32)
        xb = jnp.concatenate([rows_of(l, 1), rows_of(l + 1, 1)], axis=1).astype(BF16)
        bot = bot + jnp.dot(xb, w1_ref[1, wrows, :], preferred_element_type=F32)
    is_k = (lax.broadcasted_iota(jnp.int32, (M, 1), 0) % CG) < G_KV
    pick = lambda z, n: jnp.where(is_k[:n], z[:n, :HEAD_B], z[:n, HEAD_B:])
    hcur = pick(top, NCH * CG) + pick(bot, M)[CG:]
    o2 = jnp.dot(_gelu_tanh(hcur).astype(BF16), w2_ref[...], preferred_element_type=F32)
    out_ref[0] = pick(o2, NCH * CG).reshape(NCH, CG, HEAD_B)


def compress_kv(pool5, table, pe_cmp, w1, w2):
    NP, CPP = pool5.shape[:2]
    B, n_pages = table.shape
    PGS = max(d for d in (8, 4, 2, 1) if n_pages % d == 0)
    NCH = PGS * CPP
    CG = 2 * G_KV
    half = S_CMP * HEAD_B
    pe_r = jnp.repeat(pe_cmp.reshape(2, 2, S_CMP, HEAD_B).transpose(1, 2, 0, 3), G_KV, axis=2)
    w1_r = w1.reshape(2, 2, half, HEAD_B).transpose(1, 2, 0, 3).reshape(2, half, 2 * HEAD_B)
    w2_r = jnp.concatenate([w2[0], w2[1]], axis=1)

    def page_map(i):
        return lambda b, s, pt: (pt[b, s * PGS + i], 0, 0, 0, 0)

    def next_map(b, s, pt):
        return (pt[b, jnp.minimum((s + 1) * PGS, n_pages - 1)], 0, 0, 0, 0)

    const = lambda n: (lambda b, s, pt: (0,) * n)
    grid_spec = pltpu.PrefetchScalarGridSpec(
        num_scalar_prefetch=1,
        grid=(B, n_pages // PGS),
        in_specs=[pl.BlockSpec((1, CPP, S_CMP, CG, HEAD_B), page_map(i)) for i in range(PGS)] + [
            pl.BlockSpec((1, 1, S_CMP, CG, HEAD_B), next_map),
            pl.BlockSpec((2, S_CMP, CG, HEAD_B), const(4)),
            pl.BlockSpec((2, half, 2 * HEAD_B), const(3)),
            pl.BlockSpec((HEAD_B, 2 * HEAD_B), const(2)),
        ],
        out_specs=pl.BlockSpec((1, NCH, CG, HEAD_B), lambda b, s, pt: (b, s, 0, 0)),
    )
    return pl.pallas_call(
        functools.partial(_compress_kernel, PGS=PGS),
        grid_spec=grid_spec,
        out_shape=jax.ShapeDtypeStruct((B, n_pages * CPP, CG, HEAD_B), F32),
        compiler_params=_cparams(("parallel", "arbitrary")),
        name="compress_kv",
    )(table, *([pool5] * PGS), pool5, pe_r, w1_r, w2_r)


def _stack_heads(q, HG):
    return jnp.concatenate([q[:, h * HEAD_B:(h + 1) * HEAD_B] for h in range(HG)], axis=0)


def _masked_softmax_rows(s, mask):
    s = jnp.where(mask, s, NEG)
    m = jnp.max(s, axis=-1, keepdims=True)
    e = jnp.where(mask, jnp.exp(s - m), 0.0)
    l = jnp.sum(e, axis=-1, keepdims=True)
    return e / jnp.where(l > 0.0, l, 1.0)


def _attend_stacked(s, dist, mask, v, slopes_ref, g, HG, tq):
    ps = []
    psum = jnp.zeros(dist.shape, F32)
    for h in range(HG):
        p = _masked_softmax_rows(s[h * tq:(h + 1) * tq] - slopes_ref[g * HG + h] * dist, mask)
        psum = psum + p
        ps.append(p.astype(BF16))
    o = jnp.dot(jnp.concatenate(ps, axis=0), v.astype(BF16), preferred_element_type=F32)
    return o, psum


def _unstack_store(o_ref, o, HG, tq):
    for h in range(HG):
        o_ref[0, :, h * HEAD_B:(h + 1) * HEAD_B] = o[h * tq:(h + 1) * tq]


def _nsa_cmp_kernel(slopes_ref, q_ref, kc_ref, vc_ref, o_ref, selm_ref, *, tq, HG, nc, nsb, pos0):
    g = pl.program_id(1)
    qt = pl.program_id(2)
    NCp = kc_ref.shape[1]
    NSBp = selm_ref.shape[3]
    q_st = _stack_heads(q_ref[0] * (HEAD_B ** -0.5), HG)
    s = _bdot_nt(q_st, kc_ref[0])
    qpos = pos0 + qt * tq + lax.broadcasted_iota(jnp.int32, (tq, 1), 0)
    cidx = lax.broadcasted_iota(jnp.int32, (1, NCp), 1)
    cend = S_CMP * cidx + (L_CMP - 1)
    mask = (cend <= qpos) & (cidx < nc)
    dist = (qpos - cend).astype(F32)
    o, imp_c = _attend_stacked(s, dist, mask, vc_ref[0], slopes_ref, g, HG, tq)
    _unstack_store(o_ref, o, HG, tq)

    crow = lax.broadcasted_iota(jnp.int32, (NCp, NSBp), 0)
    jcol = lax.broadcasted_iota(jnp.int32, (NCp, NSBp), 1)
    overlap = ((S_CMP * crow < L_SEL * (jcol + 1)) & (S_CMP * crow + L_CMP > L_SEL * jcol)
               & (crow < nc)).astype(F32)
    imp = jnp.dot(imp_c, overlap, precision=lax.Precision.HIGHEST, preferred_element_type=F32)
    lane = lax.broadcasted_iota(jnp.int32, (tq, NSBp), 1)
    cur = jnp.right_shift(qpos, int(math.log2(L_SEL)))
    forced = ((lane == 0) | (lane == cur) | (lane == cur - 1)).astype(F32)
    score = jnp.where(lane <= cur, imp + FORCE_BONUS * forced, NEG)
    score = jnp.where(lane < nsb, score, -3e38)

    if tq % LANES == 0 and NSBp == LANES:
        nr = -(-nsb // SUBLANES) * SUBLANES
        st = score.T[:nr]
        sub = lax.broadcasted_iota(jnp.int32, (nr, tq), 0)
        cnt = jnp.zeros((nr, tq), F32)
        for i in range(nsb):
            row = st[i:i + 1, :]
            beats = (row > st) | ((row == st) & (sub > i))
            cnt = cnt + jnp.where(beats, 1.0, 0.0)
        sel_t = jnp.where((cnt < TOPK_SEL) & (st > 0.5 * NEG), 1.0, 0.0)
        selm_ref[0, 0] = jnp.concatenate([sel_t, jnp.zeros((NSBp - nr, tq), F32)], axis=0).T
    else:
        cnt = jnp.zeros((tq, NSBp), F32)
        for i in range(nsb):
            col = score[:, i:i + 1]
            beats = (col > score) | ((col == score) & (lane > i))
            cnt = cnt + jnp.where(beats, 1.0, 0.0)
        sel = (cnt < TOPK_SEL) & (score > 0.5 * NEG)
        selm_ref[0, 0] = sel.astype(F32)


def nsa_cmp(proj, kvc, slopes, *, tq, nc, nsb, pos0):
    B, T, _ = proj.shape
    HG = slopes.shape[0] // G_KV
    NCp = kvc.shape[1]
    NSBp = -(-nsb // LANES) * LANES
    gw = HG * HEAD_B
    return pl.pallas_call(
        functools.partial(_nsa_cmp_kernel, tq=tq, HG=HG, nc=nc, nsb=nsb, pos0=pos0),
        grid=(B, G_KV, T // tq),
        in_specs=[pl.BlockSpec(memory_space=pltpu.SMEM),
                  pl.BlockSpec((1, tq, gw), lambda b, g, t: (b, t, g)),
                  pl.BlockSpec((1, NCp, HEAD_B), lambda b, g, t: (b, 0, g)),
                  pl.BlockSpec((1, NCp, HEAD_B), lambda b, g, t: (b, 0, G_KV + g))],
        out_specs=[pl.BlockSpec((1, tq, gw), lambda b, g, t: (b, t, g)),
                   pl.BlockSpec((1, 1, tq, NSBp), lambda b, g, t: (b, g, t, 0))],
        out_shape=[jax.ShapeDtypeStruct((B, T, G_KV * gw), F32),
                   jax.ShapeDtypeStruct((B, G_KV, T, NSBp), F32)],
        compiler_params=_cparams(("parallel", "parallel", "parallel")),
        name="nsa_cmp",
    )(slopes, proj, kvc.reshape(B, NCp, -1), kvc.reshape(B, NCp, -1))


LOG2E = 1.4426950408889634


def _bf16_part(x):
    return x.astype(BF16).astype(F32)


def _alibi_lhs(q, slope_col):
    c = slope_col * LOG2E
    c1 = _bf16_part(c)
    c2 = _bf16_part(c - c1)
    c3 = _bf16_part(c - c1 - c2)
    lane = lax.broadcasted_iota(jnp.int32, q.shape, 1)
    extra = jnp.where((lane == 0) | (lane == 3), c1,
                      jnp.where((lane == 1) | (lane == 4), c2, jnp.where((lane == 2) | (lane == 5), c3, 0.0)))
    return jnp.concatenate([q, extra], axis=1).astype(BF16)


def _alibi_rhs(k, k0):
    pos = k0 + lax.broadcasted_iota(jnp.int32, k.shape, 0)
    lane = lax.broadcasted_iota(jnp.int32, k.shape, 1)
    hi = jnp.bitwise_and(pos, -L_SEL)
    extra = jnp.where(lane < 3, hi, jnp.where(lane < 6, pos - hi, 0)).astype(F32)
    return jnp.concatenate([k, extra], axis=1).astype(BF16)


def _with_ones(v):
    lane = lax.broadcasted_iota(jnp.int32, v.shape, 1)
    return jnp.concatenate([v, jnp.where(lane == 0, 1.0, 0.0)], axis=1).astype(BF16)


def _silu(x):
    return x * _sigmoid(x)


def _nsa_selwin_prompt_kernel(slopes_ref, q_ref, selm_ref, ks_ref, vs_ref, kw_ref, vw_ref,
                              ocmp_ref, zc_ref, zs_ref, zw_ref, gate_ref, o_ref, s_scr, *, tq, HG, T, WS, SEG):
    g = pl.program_id(1)
    qt = pl.program_id(2)
    NSBp = selm_ref.shape[3]
    R = HG * tq
    qpos = qt * tq + lax.broadcasted_iota(jnp.int32, (tq, 1), 0)
    slope_col = jnp.concatenate([jnp.full((tq, 1), slopes_ref[g * HG + h], F32) for h in range(HG)], axis=0)
    tile_heads = lambda x: jnp.concatenate([x] * HG, axis=0)
    q2 = _alibi_lhs(_stack_heads(q_ref[0] * (HEAD_B ** -0.5 * LOG2E), HG), slope_col)

    selm_b = selm_ref[0, 0].astype(BF16)
    nseg = (qt * tq + tq + SEG - 1) // SEG

    def seg_scores(si, m):
        k0 = pl.multiple_of(si * SEG, SEG)
        kcol = k0 + lax.broadcasted_iota(jnp.int32, (NSBp, SEG), 1)
        expand = (jnp.right_shift(kcol, int(math.log2(L_SEL)))
                  == lax.broadcasted_iota(jnp.int32, (NSBp, SEG), 0)).astype(BF16)
        in_blk = jnp.dot(selm_b, expand, preferred_element_type=F32) > 0.5
        kpos = k0 + lax.broadcasted_iota(jnp.int32, (1, SEG), 1)
        mask = tile_heads(in_blk & (kpos <= qpos))
        s = lax.dot_general(q2, _alibi_rhs(ks_ref[0, pl.ds(k0, SEG), :], k0), NT_DIMS, preferred_element_type=F32)
        s = jnp.where(mask, s, NEG)
        s_scr[si] = s
        return jnp.maximum(m, jnp.max(s, axis=-1, keepdims=True))

    m = lax.fori_loop(0, nseg, seg_scores, jnp.full((R, 1), NEG, F32))

    def seg_pv(si, acc):
        k0 = pl.multiple_of(si * SEG, SEG)
        e = jnp.exp2(s_scr[si] - m).astype(BF16)
        return acc + jnp.dot(e, _with_ones(vs_ref[0, pl.ds(k0, SEG), :]), preferred_element_type=F32)

    acc = lax.fori_loop(0, nseg, seg_pv, jnp.zeros((R, 2 * HEAD_B), F32))
    o = acc[:, :HEAD_B] / acc[:, HEAD_B:HEAD_B + 1]

    start = pl.multiple_of(jnp.clip(qt * tq - WINDOW, 0, T - WS), SUBLANES)
    distw = qpos - (start + lax.broadcasted_iota(jnp.int32, (1, WS), 1))
    maskw = tile_heads((distw >= 0) & (distw < WINDOW))
    sw = lax.dot_general(q2, _alibi_rhs(kw_ref[0, pl.ds(start, WS), :], start), NT_DIMS, preferred_element_type=F32)
    sw = jnp.where(maskw, sw, NEG)
    ew = jnp.exp2(sw - jnp.max(sw, axis=-1, keepdims=True)).astype(BF16)
    accw = jnp.dot(ew, _with_ones(vw_ref[0, pl.ds(start, WS), :]), preferred_element_type=F32)
    ow = accw[:, :HEAD_B] / accw[:, HEAD_B:HEAD_B + 1]

    HB = G_KV * HG
    gates = _sigmoid(gate_ref[0])
    lane = lax.broadcasted_iota(jnp.int32, gates.shape, 1)
    gate_col = lambda idx: jnp.sum(jnp.where(lane == idx, gates, 0.0), axis=1, keepdims=True)
    for h in range(HG):
        hs = slice(h * HEAD_B, (h + 1) * HEAD_B)
        rs = slice(h * tq, (h + 1) * tq)
        hd = g * HG + h
        y = (gate_col(hd) * ocmp_ref[0, :, hs] * _silu(zc_ref[0, :, hs])
             + gate_col(HB + hd) * o[rs] * _silu(zs_ref[0, :, hs])
             + gate_col(2 * HB + hd) * ow[rs] * _silu(zw_ref[0, :, hs]))
        o_ref[0, :, hs] = y.astype(o_ref.dtype)


def nsa_selwin_prompt(proj, rows, selm, o_cmp, slopes, *, tq):
    B, T, _ = proj.shape
    HG = slopes.shape[0] // G_KV
    NSBp = selm.shape[3]
    gw = HG * HEAD_B
    CB = G_KV * gw
    WS = min(T, WINDOW + tq)
    SEG = min(T, 512)
    assert T % SEG == 0
    kv_spec = lambda c: pl.BlockSpec((1, T, HEAD_B), lambda b, g, t: (b, 0, c * G_KV + g))
    head_spec = lambda blk: pl.BlockSpec((1, tq, gw), lambda b, g, t: (b, t, blk * G_KV + g))
    return pl.pallas_call(
        functools.partial(_nsa_selwin_prompt_kernel, tq=tq, HG=HG, T=T, WS=WS, SEG=SEG),
        grid=(B, G_KV, T // tq),
        in_specs=[pl.BlockSpec(memory_space=pltpu.SMEM),
                  head_spec(0),
                  pl.BlockSpec((1, 1, tq, NSBp), lambda b, g, t: (b, g, t, 0)),
                  kv_spec(2), kv_spec(3), kv_spec(4), kv_spec(5),
                  head_spec(0), head_spec(1), head_spec(2), head_spec(3),
                  pl.BlockSpec((1, tq, LANES), lambda b, g, t: (b, t, 4 * CB // LANES))],
        out_specs=head_spec(0),
        out_shape=jax.ShapeDtypeStruct((B, T, CB), BF16),
        scratch_shapes=[pltpu.VMEM((T // SEG, HG * tq, SEG), F32)],
        compiler_params=_cparams(("parallel", "parallel", "parallel")),
        name="nsa_selwin_prompt",
    )(slopes, proj, selm, rows, rows, rows, rows, o_cmp, proj, proj, proj, proj)


def _nsa_selwin_sample_kernel(pt_ref, slopes_ref, q_ref, selm_ref, selst_ref, *refs, PGS, PS, HG, TQ, pos0, n_new, n_win):
    del pt_ref
    page_refs = refs[:PGS]
    (new_ref, cwin_ref, ocmp_ref, zc_ref, zs_ref, zw_ref, gate_ref, o_ref,
     m_scr, l_scr, acc_scr) = refs[PGS:]
    st = pl.program_id(1)
    NSBp = selm_ref.shape[3]
    GW = G_KV * HEAD_B
    sel_shift = int(math.log2(L_SEL))

    @pl.when(st == 0)
    def _():
        m_scr[...] = jnp.full(m_scr.shape, NEG, F32)
        l_scr[...] = jnp.zeros(l_scr.shape, F32)
        acc_scr[...] = jnp.zeros(acc_scr.shape, F32)

    qpos = pos0 + lax.broadcasted_iota(jnp.int32, (TQ, 1), 0)
    lane_j = lax.broadcasted_iota(jnp.int32, (TQ, NSBp), 1)
    tile_heads = lambda x: jnp.concatenate([x] * HG, axis=0)

    def sel_col(selm_g, j):
        return jnp.sum(jnp.where(lane_j == j, selm_g, 0.0), axis=1, keepdims=True)

    def online_update(s, mask, v):
        gs = range(G_KV)
        m_old = [m_scr[g] for g in gs]
        m_new = [jnp.maximum(m_old[g], jnp.max(jnp.where(mask[g], s[g], NEG), axis=-1, keepdims=True)) for g in gs]
        e = [jnp.where(mask[g], jnp.exp(s[g] - m_new[g]), 0.0) for g in gs]
        alpha = [jnp.exp(m_old[g] - m_new[g]) for g in gs]
        pv = [_bdot(e[g], v[g]) for g in gs]
        for g in gs:
            l_scr[g] = alpha[g] * l_scr[g] + jnp.sum(e[g], axis=-1, keepdims=True)
            acc_scr[g] = alpha[g] * acc_scr[g] + pv[g]
            m_scr[g] = m_new[g]

    NK = PGS * PS
    kpos = st * NK + lax.broadcasted_iota(jnp.int32, (1, NK), 1)
    expand = (jnp.right_shift(lax.broadcasted_iota(jnp.int32, (LANES, NK), 1), sel_shift)
              == lax.broadcasted_iota(jnp.int32, (LANES, NK), 0)).astype(BF16)
    in_blk_all = jnp.dot(selst_ref[0, 0].astype(BF16), expand, preferred_element_type=F32)
    dist = qpos - kpos
    distf = tile_heads(dist.astype(F32))
    gs = range(G_KV)
    q_st = [_stack_heads(q_ref[0, :, g * HG * HEAD_B:(g + 1) * HG * HEAD_B] * (HEAD_B ** -0.5), HG).astype(BF16)
            for g in gs]
    slope_col = [jnp.concatenate([jnp.full((TQ, 1), slopes_ref[g * HG + h], F32) for h in range(HG)], axis=0)
                 for g in gs]
    slabs = [pltpu.einshape("rcd->crd", page_refs[i][0].reshape(PS, 2 * G_KV, HEAD_B)) for i in range(PGS)]
    k = [jnp.concatenate([slabs[i][g] for i in range(PGS)], axis=0) for g in gs]
    v = [jnp.concatenate([slabs[i][G_KV + g] for i in range(PGS)], axis=0) for g in gs]
    mask = [tile_heads((in_blk_all[g * TQ:(g + 1) * TQ] > 0.5) & (dist >= 0)) for g in gs]
    s = [_bdot_nt(q_st[g], k[g]) - slope_col[g] * distf for g in gs]
    online_update(s, mask, v)

    @pl.when(st == pl.num_programs(1) - 1)
    def _():
        NN = new_ref.shape[1]
        rnew = lax.broadcasted_iota(jnp.int32, (1, NN), 1)
        kpos_n = pos0 + rnew
        dist_n = qpos - kpos_n
        ok_n = (rnew < n_new) & (dist_n >= 0)
        jn = pos0 >> sel_shift
        kpos_w = pos0 - n_win + lax.broadcasted_iota(jnp.int32, (1, n_win), 1)
        dist_w = qpos - kpos_w
        dist_wall = jnp.concatenate([dist_w, dist_n], axis=1)
        mask_wall = jnp.concatenate([(dist_w >= 0) & (dist_w < WINDOW), ok_n & (dist_n < WINDOW)], axis=1)
        kn = [new_ref[0, :, 2 * GW + g * HEAD_B:2 * GW + (g + 1) * HEAD_B] for g in gs]
        vn = [new_ref[0, :, 3 * GW + g * HEAD_B:3 * GW + (g + 1) * HEAD_B] for g in gs]
        mask_n = [tile_heads((sel_col(selm_ref[0, g], jn) > 0.5) & ok_n) for g in gs]
        dist_nf = tile_heads(dist_n.astype(F32))
        s_n = [_bdot_nt(q_st[g], kn[g]) - slope_col[g] * dist_nf for g in gs]
        online_update(s_n, mask_n, vn)
        kw = [jnp.concatenate([cwin_ref[0, :, g * HEAD_B:(g + 1) * HEAD_B],
                               new_ref[0, :, 4 * GW + g * HEAD_B:4 * GW + (g + 1) * HEAD_B]], axis=0) for g in gs]
        vw = [jnp.concatenate([cwin_ref[0, :, GW + g * HEAD_B:GW + (g + 1) * HEAD_B],
                               new_ref[0, :, 5 * GW + g * HEAD_B:5 * GW + (g + 1) * HEAD_B]], axis=0) for g in gs]
        dist_wf = tile_heads(dist_wall.astype(F32))
        mask_w = tile_heads(mask_wall)
        sw = [_bdot_nt(q_st[g], kw[g]) - slope_col[g] * dist_wf for g in gs]
        pw = [_masked_softmax_rows(sw[g], mask_w) for g in gs]
        ow = [_bdot(pw[g], vw[g]) for g in gs]
        HB = G_KV * HG
        gates = _sigmoid(gate_ref[0])
        for g in gs:
            l = l_scr[g]
            o = acc_scr[g] / jnp.where(l > 0.0, l, 1.0)
            for h in range(HG):
                hd = g * HG + h
                hs = slice(hd * HEAD_B, (hd + 1) * HEAD_B)
                rs = slice(h * TQ, (h + 1) * TQ)
                y = (gates[:, hd:hd + 1] * ocmp_ref[0, :, hs] * _silu(zc_ref[0, :, hs])
                     + gates[:, HB + hd:HB + hd + 1] * o[rs] * _silu(zs_ref[0, :, hs])
                     + gates[:, 2 * HB + hd:2 * HB + hd + 1] * ow[g][rs] * _silu(zw_ref[0, :, hs]))
                o_ref[0, :, hs] = y.astype(o_ref.dtype)


def nsa_selwin_sample(proj, selm, o_cmp, pool5, table, new_rows, cwin, slopes, *, pos0, n_new):
    B, TQ, _ = proj.shape
    HG = slopes.shape[0] // G_KV
    NP, CPP = pool5.shape[:2]
    PS = CPP * S_CMP
    n_pages = table.shape[1]
    NSBp = selm.shape[3]
    PGS = max(d for d in (8, 4, 2, 1) if n_pages % d == 0)
    CB = G_KV * HG * HEAD_B
    GW = G_KV * HEAD_B
    NN = new_rows.shape[1]
    n_win = cwin.shape[1]
    assert pos0 % L_SEL == 0 and n_new <= L_SEL and pos0 == n_pages * PS

    def page_map(i):
        return lambda b, s, pt: (pt[b, s * PGS + i], 0, 0, 1, 0)

    n_steps = n_pages // PGS
    bps = PGS * PS // L_SEL
    assert bps <= LANES
    selst = selm[:, :, :, :n_steps * bps].reshape(B, G_KV, TQ, n_steps, bps).transpose(0, 3, 1, 2, 4)
    selst = jnp.pad(selst.reshape(B, n_steps, G_KV * TQ, bps), ((0, 0), (0, 0), (0, 0), (0, LANES - bps)))

    const = lambda b, s, pt: (b, 0, 0)
    wide = lambda blk: pl.BlockSpec((1, TQ, CB), lambda b, s, pt: (b, 0, blk))
    grid_spec = pltpu.PrefetchScalarGridSpec(
        num_scalar_prefetch=1,
        grid=(B, n_steps),
        in_specs=[pl.BlockSpec(memory_space=pltpu.SMEM),
                  pl.BlockSpec((1, TQ, CB), const),
                  pl.BlockSpec((1, G_KV, TQ, NSBp), lambda b, s, pt: (b, 0, 0, 0)),
                  pl.BlockSpec((1, 1, G_KV * TQ, LANES), lambda b, s, pt: (b, s, 0, 0))]
                 + [pl.BlockSpec((1, CPP, S_CMP, 2 * G_KV, HEAD_B), page_map(i)) for i in range(PGS)]
                 + [pl.BlockSpec((1, NN, 6 * GW), const),
                    pl.BlockSpec((1, n_win, 2 * GW), const),
                    wide(0), wide(1), wide(2), wide(3),
                    pl.BlockSpec((1, TQ, LANES), lambda b, s, pt: (b, 0, 4 * CB // LANES))],
        out_specs=wide(0),
        scratch_shapes=[pltpu.VMEM((G_KV, HG * TQ, 1), F32),
                        pltpu.VMEM((G_KV, HG * TQ, 1), F32),
                        pltpu.VMEM((G_KV, HG * TQ, HEAD_B), F32)],
    )
    return pl.pallas_call(
        functools.partial(_nsa_selwin_sample_kernel, PGS=PGS, PS=PS, HG=HG, TQ=TQ, pos0=pos0,
                          n_new=n_new, n_win=n_win),
        grid_spec=grid_spec,
        out_shape=jax.ShapeDtypeStruct((B, TQ, CB), F32),
        compiler_params=_cparams(("parallel", "arbitrary")),
        name="nsa_selwin_sample",
    )(table, slopes, proj, selm, selst, *([pool5] * PGS), new_rows, cwin, o_cmp, proj, proj, proj, proj)


def _rwkv_layer(h, x_prev, s0, i, W, B, T):
    N, D = h.shape
    g = W["norm_g"][i]
    rkvg = rwkv_in(h, x_prev, g, W["mu_a"][i], W["w_in_a"], i, T)
    CA = rkvg.shape[-1]
    lw, a = rwkv_lora(h, x_prev, g, W["mu_a"][i], W["w_lora_w1"][i], W["w_lora_w2"][i], W["a_lora1"][i],
                      W["a_lora2"][i], W["w0_a"][i], W["a0_a"][i], T)
    pvec = jnp.stack([W["k_k"][i], W["k_a"][i], W["r_k"][i].reshape(CA), W["ln_x_w"][i], W["ln_x_b"][i]])
    o, s_fin = rwkv_scan(rkvg.reshape(4, B, T, CA), lw.reshape(B, T, CA), a.reshape(B, T, CA), pvec, s0)
    last = rmsnorm(h.reshape(B, T, D)[:, -1], g)
    return o.reshape(N, CA), s_fin, last


def _nsa_layer(h, jb, shared, W, slopes, B, T, norm_g):
    N, D = h.shape
    CB = W["w_out_b"].shape[1]
    proj3 = norm_mm(h, norm_g, W["w_in_b"], (jb,)).reshape(B, T, -1)
    if shared["past"] is None:
        o_cmp, selm = nsa_cmp(proj3, shared["kvc"], slopes, tq=min(T, 256), nc=shared["nc"],
                              nsb=shared["nsb"], pos0=0)
        o = nsa_selwin_prompt(proj3, shared["rows"], selm, o_cmp, slopes, tq=min(T, 256))
    else:
        TQ = SUBLANES
        projp = jnp.pad(proj3, ((0, 0), (0, TQ - T), (0, 0)))
        o_cmp, selm = nsa_cmp(projp, shared["kvc"], slopes, tq=TQ, nc=shared["nc"],
                              nsb=shared["nsb"], pos0=shared["pos0"])
        pool, table, cwin = shared["past"]
        o = nsa_selwin_sample(projp, selm, o_cmp, pool, table, shared["new_rows"], cwin, slopes,
                              pos0=shared["pos0"], n_new=T)[:, :T]
    return o.reshape(N, CB)


def _trunk(x, p, pos0, wkv0, shift0, past, W, slopes):
    B, T, D = x.shape
    N = B * T
    depth = p.shape[0]
    n_a = W["w_in_a"].shape[0]
    GW = G_KV * HEAD_B
    h = x.reshape(N, D)
    wkv_new, shift_new = [], []
    shared, kv_rows, win_state = None, None, None
    for i in range(depth):
        if i < n_a:
            o, s_fin, last = _rwkv_layer(h, shift0[i], wkv0[i], i, W, B, T)
            wkv_new.append(s_fin)
            shift_new.append(last)
            h = out_ple(o, h, p[i].reshape(N, -1), W["w_out_a"], i, W["w_ple"], W["w_ple_gate"], i)
        else:
            o = _nsa_layer(h, i - n_a, shared, W, slopes, B, T, W["norm_g"][i])
            h = out_ple(o, h, p[i].reshape(N, -1), W["w_out_b"], i - n_a, W["w_ple"], W["w_ple_gate"], i)
        if i == n_a - 1:
            rows = norm_mm(h, W["kv_norm_g"], W["w_kv"]).reshape(B, T, 6 * GW)
            kv_rows = rows[:, :, :4 * GW].reshape(B, T, 4, G_KV, HEAD_B)
            win_new = rows[:, :, 4 * GW:].reshape(B, T, 2, G_KV, HEAD_B)
            if past is None:
                PS = 128
                pool = rows.reshape(B * T // PS, PS // S_CMP, S_CMP, 6 * G_KV, HEAD_B)
                table = jnp.arange(B * T // PS, dtype=jnp.int32).reshape(B, T // PS)
                t_all = T
                win_all = win_new
                shared = {"past": None, "rows": rows}
            else:
                pool, table, cwin = past
                PS = pool.shape[1] * S_CMP
                t_all = pos0 + T
                win_all = jnp.concatenate([cwin.reshape(B, -1, 2, G_KV, HEAD_B), win_new], axis=1)
                NN = LANES
                shared = {"past": past, "new_rows": jnp.pad(rows, ((0, 0), (0, NN - T), (0, 0)))}
            win_state = win_all[:, win_all.shape[1] - min(WINDOW, pos0 + T):]
            nc = (t_all - L_CMP) // S_CMP + 1
            assert nc < table.shape[1] * PS // S_CMP
            kvc = compress_kv(pool, table, W["pe_cmp"], W["w_cmp1"], W["w_cmp2"])
            shared.update(kvc=kvc, nc=nc, nsb=max(-(-t_all // L_SEL), TOPK_SEL), pos0=pos0)
    y = rmsnorm(h, W["final_norm_g"]).reshape(B, T, D)
    return y, jnp.stack(wkv_new), jnp.stack(shift_new), kv_rows, win_state


def kernel(x_prompt, x_sample, state_wkv, state_shift, cache_kv, cache_win_kv, page_table, p_prompt, p_sample, norm_g, mu_a, w_in_a, w_lora_w1, w_lora_w2, w0_a, a_lora1, a_lora2, a0_a, k_k, k_a, r_k, ln_x_w, ln_x_b, w_out_a, w_in_b, w_out_b, kv_norm_g, w_kv, pe_cmp, w_cmp1, w_cmp2, w_ple, w_ple_gate, final_norm_g):
    bf = lambda w: w.astype(BF16)
    CA = w_out_a.shape[1]
    W = dict(norm_g=norm_g, mu_a=mu_a, w_in_a=bf(w_in_a), w_lora_w1=bf(w_lora_w1), w_lora_w2=bf(w_lora_w2),
             w0_a=w0_a, a_lora1=bf(a_lora1), a_lora2=bf(a_lora2), a0_a=a0_a, k_k=k_k, k_a=k_a,
             r_k=r_k, ln_x_w=ln_x_w, ln_x_b=ln_x_b,
             w_out_a=bf(w_out_a), w_in_b=bf(w_in_b), w_out_b=bf(w_out_b), kv_norm_g=kv_norm_g, w_kv=bf(w_kv),
             pe_cmp=pe_cmp, w_cmp1=bf(w_cmp1), w_cmp2=bf(w_cmp2), w_ple=bf(w_ple), w_ple_gate=bf(w_ple_gate),
             final_norm_g=final_norm_g)
    HB = w_out_b.shape[1] // HEAD_B
    slopes = 2.0 ** (-8.0 * jnp.arange(1, HB + 1, dtype=F32) / HB)
    bp = x_prompt.shape[0]
    n_a = w_in_a.shape[0]
    D = x_prompt.shape[-1]
    wkv0 = jnp.zeros((n_a, bp, CA // HEAD_A, HEAD_A, HEAD_A), F32)
    shift0 = jnp.zeros((n_a, bp, D), F32)
    y_p, wkv_p, shift_p, kv_p, win_p = _trunk(x_prompt, p_prompt, 0, wkv0, shift0, None, W, slopes)
    db, n_pages = page_table.shape
    NP, PS = cache_kv.shape[:2]
    pool5 = cache_kv.reshape(NP, PS // S_CMP, S_CMP, -1, HEAD_B)
    past = (pool5, page_table, cache_win_kv.reshape(db, cache_win_kv.shape[1], -1))
    y_s, wkv_s, shift_s, kv_s, win_s = _trunk(x_sample, p_sample, n_pages * PS, state_wkv, state_shift, past, W, slopes)
    return (y_p, y_s, wkv_p, shift_p, kv_p, win_p, wkv_s, shift_s, kv_s, win_s)
```

```python
import functools
import math

import jax
import jax.numpy as jnp
from jax import lax
from jax.experimental import pallas as pl
from jax.experimental.pallas import tpu as pltpu

F32 = jnp.float32
BF16 = jnp.bfloat16

HEAD_A = 64
GN_EPS = 64e-5
HEAD_B = 128
G_KV = 4
L_CMP = 32
S_CMP = 16
L_SEL = 64
TOPK_SEL = 16
WINDOW = 512
RMS_EPS = 1e-6
NEG = -1e30
FORCE_BONUS = 1e4

LANES = 128
SUBLANES = 8
VMEM_LIMIT = 56 * 1024 * 1024

SCAN_NH = 2
SCAN_C = 64
SCAN_GP = 16

NT_DIMS = (((1,), (1,)), ((), ()))
TN_DIMS = (((0,), (0,)), ((), ()))


def _cparams(sem):
    return pltpu.CompilerParams(dimension_semantics=sem, vmem_limit_bytes=VMEM_LIMIT)


def _bdot(a, b):
    return jnp.dot(a.astype(BF16), b.astype(BF16), preferred_element_type=F32)


def _bdot_nt(a, b):
    return lax.dot_general(a.astype(BF16), b.astype(BF16), NT_DIMS, preferred_element_type=F32)


def _bdot_tn(a, b):
    return lax.dot_general(a.astype(BF16), b.astype(BF16), TN_DIMS, preferred_element_type=F32)


def _rms_kernel(x_ref, g_ref, o_ref):
    x = x_ref[...]
    ms = jnp.mean(x * x, axis=-1, keepdims=True)
    o_ref[...] = x * lax.rsqrt(ms + RMS_EPS) * g_ref[...]


def rmsnorm(x, g):
    M, D = x.shape
    tm = min(M, 256)
    return pl.pallas_call(
        _rms_kernel,
        grid=(pl.cdiv(M, tm),),
        in_specs=[pl.BlockSpec((tm, D), lambda i: (i, 0)),
                  pl.BlockSpec((1, D), lambda i: (0, 0))],
        out_specs=pl.BlockSpec((tm, D), lambda i: (i, 0)),
        out_shape=jax.ShapeDtypeStruct((M, D), F32),
        compiler_params=_cparams(("parallel",)),
        name="rmsnorm",
    )(x, g.reshape(1, D))


def _mm_kernel(x_ref, w_ref, o_ref):
    o_ref[...] = jnp.dot(x_ref[...].astype(BF16), w_ref[...], preferred_element_type=F32)


def mm(x, w, widx=()):
    M, K = x.shape
    N = w.shape[-1]
    assert w.shape[-2] == K and len(widx) == w.ndim - 2
    tm = min(M, 1024 if K <= 2048 else 512)
    tn = N if N <= 512 else 512
    nlead = len(widx)
    w_spec = pl.BlockSpec((None,) * nlead + (K, tn), lambda i, j: tuple(widx) + (0, j))
    return pl.pallas_call(
        _mm_kernel,
        grid=(pl.cdiv(M, tm), pl.cdiv(N, tn)),
        in_specs=[pl.BlockSpec((tm, K), lambda i, j: (i, 0)), w_spec],
        out_specs=pl.BlockSpec((tm, tn), lambda i, j: (i, j)),
        out_shape=jax.ShapeDtypeStruct((M, N), F32),
        compiler_params=_cparams(("parallel", "parallel")),
        name="mm",
    )(x, w)


def _col_tile(n_rows, n_cols, tn):
    return tn if n_rows > 64 else min(n_cols, 4096)


def _norm_rows(x, g):
    return x * lax.rsqrt(jnp.mean(x * x, axis=-1, keepdims=True) + RMS_EPS) * g


def _sigmoid(x):
    return 0.5 + 0.5 * jnp.tanh(0.5 * x)


def _norm_and_shift(h_ref, hprev_ref, xprev_ref, g_ref, i, tm, T):
    g = g_ref[...]
    hn = _norm_rows(h_ref[...], g)
    prev_row = _norm_rows(hprev_ref[SUBLANES - 1:SUBLANES, :], g)
    row = lax.broadcasted_iota(jnp.int32, (tm, 1), 0)
    xs = jnp.where(row == 0, prev_row, pltpu.roll(hn, 1, axis=0))
    if T >= tm:
        assert T % tm == 0
        start = (i * tm) % T == 0
        xs = jnp.where((row == 0) & start, xprev_ref[pl.ds((i * tm) // T, 1), :], xs)
    else:
        assert tm % T == 0
        for bb in range(tm // T):
            xs = jnp.where(row == bb * T, xprev_ref[pl.ds(i * (tm // T) + bb, 1), :], xs)
    return hn, xs


def _rwkv_in_kernel(h_ref, hprev_ref, xprev_ref, g_ref, mu_ref, w_ref, o_ref, hn_scr, xs_scr, *, tm, T):
    i, j, n = pl.program_id(0), pl.program_id(1), pl.program_id(2)
    rc = min(tm, 256)
    chunks = [slice(c * rc, (c + 1) * rc) for c in range(tm // rc)]
    P = SUBLANES

    @pl.when((j == 0) & (n == 0))
    def _():
        g = g_ref[...]
        prev = _norm_rows(hprev_ref[P - 1:P, :], g)
        if T >= tm:
            assert T % tm == 0
            prev = jnp.where((i * tm) % T == 0, xprev_ref[pl.ds((i * tm) // T, 1), :], prev)
        hn_scr[0:P, :] = jnp.broadcast_to(prev, (P, prev.shape[1]))
        for ch in chunks:
            hn_scr[P + ch.start:P + ch.stop, :] = _norm_rows(h_ref[ch, :], g)

    @pl.when(n == 0)
    def _():
        mu = mu_ref[pl.ds(j, 1), :]
        for ch in chunks:
            hn = hn_scr[P + ch.start:P + ch.stop, :]
            xs = hn_scr[P - 1 + ch.start:P - 1 + ch.stop, :]
            if T < tm:
                assert tm % T == 0 and len(chunks) == 1
                row = lax.broadcasted_iota(jnp.int32, (tm, 1), 0)
                for bb in range(tm // T):
                    xs = jnp.where(row == bb * T, xprev_ref[pl.ds(i * (tm // T) + bb, 1), :], xs)
            xs_scr[ch, :] = (hn + (xs - hn) * mu).astype(BF16)

    o_ref[...] = jnp.dot(xs_scr[...], w_ref[...], preferred_element_type=F32)


def _shift_specs(tm, D, nb, ngrid):
    z = (0,) * (ngrid - 1)
    wrap = lambda f: (lambda i, *_: f(i))
    return [pl.BlockSpec((tm, D), wrap(lambda i: (i, 0))),
            pl.BlockSpec((SUBLANES, D), wrap(lambda i: (jnp.maximum(i * (tm // SUBLANES) - 1, 0), 0))),
            pl.BlockSpec((nb, D), wrap(lambda i: (0, 0))),
            pl.BlockSpec((1, D), wrap(lambda i: (0, 0))),
            pl.BlockSpec((6, D), wrap(lambda i: (0, 0)))]


def rwkv_in(h, xprev, g, mu, w, layer, T):
    N, D = h.shape
    C = w.shape[-1]
    tm = min(N, 1024)
    tn = _col_tile(N, C, 512)
    return pl.pallas_call(
        functools.partial(_rwkv_in_kernel, tm=tm, T=T),
        grid=(N // tm, 4, C // tn),
        in_specs=_shift_specs(tm, D, xprev.shape[0], 3) + [
            pl.BlockSpec((None, None, D, tn), lambda i, j, n: (layer, j, 0, n))],
        out_specs=pl.BlockSpec((None, tm, tn), lambda i, j, n: (j, i, n)),
        out_shape=jax.ShapeDtypeStruct((4, N, C), F32),
        scratch_shapes=[pltpu.VMEM((tm + SUBLANES, D), F32), pltpu.VMEM((tm, D), BF16)],
        compiler_params=_cparams(("parallel", "arbitrary", "arbitrary")),
        name="rwkv_in",
    )(h, h, xprev, g.reshape(1, D), mu, w)


def _rwkv_lora_kernel(h_ref, hprev_ref, xprev_ref, g_ref, mu_ref, lw1_ref, lw2_ref, la1_ref, la2_ref,
                      w0_ref, a0_ref, lw_ref, a_ref, *, tm, T):
    hn, xs = _norm_and_shift(h_ref, hprev_ref, xprev_ref, g_ref, pl.program_id(0), tm, T)
    dx = xs - hn
    x4 = (hn + dx * mu_ref[4:5, :]).astype(BF16)
    x5 = (hn + dx * mu_ref[5:6, :]).astype(BF16)
    t4 = jnp.tanh(jnp.dot(x4, lw1_ref[...], preferred_element_type=F32)).astype(BF16)
    x = w0_ref[...] + jnp.dot(t4, lw2_ref[...], preferred_element_type=F32)
    lw_ref[...] = -math.exp(-0.5) * _sigmoid(x)
    t5 = jnp.dot(x5, la1_ref[...], preferred_element_type=F32).astype(BF16)
    a_ref[...] = _sigmoid(a0_ref[...] + jnp.dot(t5, la2_ref[...], preferred_element_type=F32))


def rwkv_lora(h, xprev, g, mu, lw1, lw2, la1, la2, w0, a0, T):
    N, D = h.shape
    R, C = lw2.shape
    tm = min(N, 256)
    full = lambda shape: pl.BlockSpec(shape, lambda i: (0,) * len(shape))
    o_spec = pl.BlockSpec((tm, C), lambda i: (i, 0))
    return pl.pallas_call(
        functools.partial(_rwkv_lora_kernel, tm=tm, T=T),
        grid=(N // tm,),
        in_specs=_shift_specs(tm, D, xprev.shape[0], 1) + [
            full((D, R)), full((R, C)), full((D, R)), full((R, C)), full((1, C)), full((1, C))],
        out_specs=[o_spec, o_spec],
        out_shape=[jax.ShapeDtypeStruct((N, C), F32)] * 2,
        compiler_params=_cparams(("parallel",)),
        name="rwkv_lora",
    )(h, h, xprev, g.reshape(1, D), mu, lw1, lw2, la1, la2, w0.reshape(1, C), a0.reshape(1, C))


def _norm_mm_kernel(h_ref, g_ref, w_ref, o_ref, xs_scr):
    @pl.when(pl.program_id(1) == 0)
    def _():
        rc = min(h_ref.shape[0], 256)

        def norm_chunk(c, carry):
            rows = pl.ds(pl.multiple_of(c * rc, rc), rc)
            xs_scr[rows, :] = _norm_rows(h_ref[rows, :], g_ref[...]).astype(BF16)
            return carry

        lax.fori_loop(0, h_ref.shape[0] // rc, norm_chunk, 0)

    o_ref[...] = jnp.dot(xs_scr[...], w_ref[...], preferred_element_type=F32)


def norm_mm(h, g, w, widx=()):
    N, D = h.shape
    NO = w.shape[-1]
    tm = min(N, 1024)
    tn = _col_tile(N, NO, 1024)
    nlead = len(widx)
    return pl.pallas_call(
        _norm_mm_kernel,
        grid=(N // tm, pl.cdiv(NO, tn)),
        in_specs=[pl.BlockSpec((tm, D), lambda i, n: (i, 0)),
                  pl.BlockSpec((1, D), lambda i, n: (0, 0)),
                  pl.BlockSpec((None,) * nlead + (D, tn), lambda i, n: tuple(widx) + (0, n))],
        out_specs=pl.BlockSpec((tm, tn), lambda i, n: (i, n)),
        out_shape=jax.ShapeDtypeStruct((N, NO), F32),
        scratch_shapes=[pltpu.VMEM((tm, D), BF16)],
        compiler_params=_cparams(("parallel", "arbitrary")),
        name="norm_mm",
    )(h, g.reshape(1, D), w)


def _mm_res_kernel(x_ref, h_ref, w_ref, o_ref):
    o_ref[...] = h_ref[...] + jnp.dot(x_ref[...].astype(BF16), w_ref[...], preferred_element_type=F32)


def _ple_gate_kernel(h1_ref, h1t_ref, p_ref, wp_ref, wg_ref, o_ref, xb_scr):
    @pl.when(pl.program_id(1) == 0)
    def _():
        rc = min(h1_ref.shape[0], 256)

        def cast_chunk(c, carry):
            rows = pl.ds(pl.multiple_of(c * rc, rc), rc)
            xb_scr[rows, :] = h1_ref[rows, :].astype(BF16)
            return carry

        lax.fori_loop(0, h1_ref.shape[0] // rc, cast_chunk, 0)

    gate = jnp.dot(xb_scr[...], wg_ref[...], preferred_element_type=F32)
    ple = jnp.dot(p_ref[...].astype(BF16), wp_ref[...], preferred_element_type=F32)
    o_ref[...] = h1t_ref[...] + ple * _sigmoid(gate)


def out_ple(x, h, p, w_out, oidx, w_ple, w_gate, layer):
    N, C = x.shape
    D = h.shape[1]
    DP = p.shape[1]
    tm = min(N, 1024)
    tn = _col_tile(N, D, 512)
    grid = (N // tm, D // tn)
    tile = pl.BlockSpec((tm, tn), lambda i, n: (i, n))
    h1 = pl.pallas_call(
        _mm_res_kernel,
        grid=grid,
        in_specs=[pl.BlockSpec((tm, C), lambda i, n: (i, 0)), tile,
                  pl.BlockSpec((None, C, tn), lambda i, n: (oidx, 0, n))],
        out_specs=tile,
        out_shape=jax.ShapeDtypeStruct((N, D), F32),
        compiler_params=_cparams(("parallel", "parallel")),
        name="mm_res",
    )(x, h, w_out)
    return pl.pallas_call(
        _ple_gate_kernel,
        grid=grid,
        in_specs=[pl.BlockSpec((tm, D), lambda i, n: (i, 0)), tile,
                  pl.BlockSpec((tm, DP), lambda i, n: (i, 0)),
                  pl.BlockSpec((None, DP, tn), lambda i, n: (layer, 0, n)),
                  pl.BlockSpec((None, D, tn), lambda i, n: (layer, 0, n))],
        out_specs=tile,
        out_shape=jax.ShapeDtypeStruct((N, D), F32),
        scratch_shapes=[pltpu.VMEM((tm, D), BF16)],
        compiler_params=_cparams(("parallel", "arbitrary")),
        name="ple_gate",
    )(h1, h1, p, w_ple, w_gate)


def _scan_kernel(r_ref, k_ref, v_ref, zg_ref, lw_ref, a_ref, pv_ref, s0_ref, o_ref, sfin_ref, s_scr, *, C, NH, GP):
    L = NH * HEAD_A
    NC = NH * C
    ci = pl.program_id(2)

    @pl.when(ci == 0)
    def _():
        for gp in range(GP):
            rows = []
            for hh in range(NH):
                pieces = [s0_ref[0, gp * NH + hh] if h2 == hh else jnp.zeros((HEAD_A, HEAD_A), F32)
                          for h2 in range(NH)]
                rows.append(jnp.concatenate(pieces, axis=1))
            s_scr[gp] = jnp.concatenate(rows, axis=0)

    row_c = lax.broadcasted_iota(jnp.int32, (C, NC), 0)
    col_s = lax.broadcasted_iota(jnp.int32, (C, NC), 1) % C
    tri_strict = col_s < row_c
    tri_incl = (lax.broadcasted_iota(jnp.int32, (C, 2 * NC), 1) % C
                <= lax.broadcasted_iota(jnp.int32, (C, 2 * NC), 0))
    st_mask = (lax.broadcasted_iota(jnp.int32, (NC, L), 0) // C
               == lax.broadcasted_iota(jnp.int32, (NC, L), 1) // HEAD_A)
    bd_mask = (lax.broadcasted_iota(jnp.int32, (NC, NC), 0) // C
               == lax.broadcasted_iota(jnp.int32, (NC, NC), 1) // C)
    head_mask = (lax.broadcasted_iota(jnp.int32, (L, L), 0) // HEAD_A
                 == lax.broadcasted_iota(jnp.int32, (L, L), 1) // HEAD_A)

    def st(x):
        return jnp.where(st_mask, jnp.concatenate([x] * NH, axis=0), 0.0)

    def bd(w):
        return jnp.where(bd_mask, jnp.concatenate([w] * NH, axis=0), 0.0)

    n_double = int(math.log2(C))
    each = lambda f, *cols: [f(*xs) for xs in zip(*cols)]
    sls = [slice(gp * L, (gp + 1) * L) for gp in range(GP)]
    head_of_lane = lax.broadcasted_iota(jnp.int32, (1, L), 1) // HEAD_A

    def hsum(x):
        out = None
        for hh in range(NH):
            sh = jnp.sum(jnp.where(head_of_lane == hh, x, 0.0), axis=-1, keepdims=True)
            out = sh if out is None else jnp.where(head_of_lane == hh, sh, out)
        return out

    k_k, k_a, r_k, ln_w, ln_b = ([pv_ref[n:n + 1, sl] for sl in sls] for n in range(5))
    lw = [lw_ref[0, :, sl] for sl in sls]
    a_sig = [a_ref[0, :, sl] for sl in sls]
    r = [r_ref[0, :, sl] for sl in sls]
    k_raw = [k_ref[0, :, sl] for sl in sls]
    v = [v_ref[0, :, sl] for sl in sls]
    kk = each(lambda x, w: x * w, k_raw, k_k)
    kk = each(lambda x: x * lax.rsqrt(jnp.maximum(hsum(x * x), 1e-24)), kk)
    k = each(lambda x, a, w: x * (1.0 + (a - 1.0) * w), k_raw, a_sig, k_a)
    lw_hi = each(lambda z: z.astype(BF16), lw)
    lw_lo = each(lambda z, hi: (z - hi.astype(F32)).astype(BF16), lw, lw_hi)
    cum_mat = (lax.broadcasted_iota(jnp.int32, (C, C), 1)
               <= lax.broadcasted_iota(jnp.int32, (C, C), 0)).astype(BF16)
    cum2 = each(lambda hi, lo: jnp.dot(cum_mat, jnp.concatenate([hi, lo], axis=1), preferred_element_type=F32),
                lw_hi, lw_lo)
    cum = [c2[:, :L] + c2[:, L:] for c2 in cum2]
    p_incl = each(jnp.exp, cum)
    p_inv = each(lambda z: jnp.exp(-z), cum)
    at = each(lambda x, c, w: -x * jnp.exp(c - w), kk, cum, lw)
    rt = each(lambda x, p: x * p, r, p_incl)
    bt = each(lambda x, a, p: x * a * p, kk, a_sig, p_inv)
    kt = each(lambda x, p: x * p, k, p_inv)
    S = [s_scr[gp] for gp in range(GP)]
    ar = each(lambda x, y: jnp.concatenate([x, y], axis=0), at, rt)
    bk_st = each(lambda x, y: jnp.concatenate([st(x), st(y)], axis=0), bt, kt)
    Gm = each(_bdot_nt, ar, bk_st)
    w_ab = [jnp.where(tri_strict, g[:C, :NC], 0.0) for g in Gm]
    tm = w_ab
    pw = each(lambda w: _bdot(w, bd(w)), w_ab)
    LH = each(_bdot_nt, ar, S)
    v_st = each(st, v)
    x = [lh[:C] + _bdot(jnp.where(tri_strict, g[:C, NC:], 0.0), vs) for lh, g, vs in zip(LH, Gm, v_st)]
    for it in range(1, n_double):
        if it < n_double - 1:
            both = each(lambda t, p: _bdot(p, jnp.concatenate([bd(t), bd(p)], axis=1)), tm, pw)
            tm = each(lambda t, p, b2: t + p + b2[:, :NC], tm, pw, both)
            pw = [b2[:, NC:] for b2 in both]
        else:
            tm = each(lambda t, p: t + p + _bdot(p, bd(t)), tm, pw)
    u = each(lambda xx, t: xx + _bdot(t, st(xx)), x, tm)
    o = [lh[C:] + _bdot(jnp.where(tri_incl, g[C:], 0.0), jnp.concatenate([st(uu), vs], axis=0))
         for lh, g, uu, vs in zip(LH, Gm, u, v_st)]
    inv_n = 1.0 / HEAD_A
    dev = each(lambda x: x - hsum(x) * inv_n, o)
    gn = each(lambda d, w, b_: d * lax.rsqrt(hsum(d * d) * inv_n + GN_EPS) * w + b_, dev, ln_w, ln_b)
    bonus = each(lambda rr, kx, w, vv: hsum(rr * kx * w) * vv, r, k, r_k, v)
    for sl, y, bo in zip(sls, gn, bonus):
        zg = zg_ref[0, :, sl]
        o_ref[0, :, sl] = ((y + bo) * (zg * _sigmoid(zg))).astype(o_ref.dtype)
    ds = [_bdot_tn(jnp.concatenate([uu, vv], axis=0), jnp.concatenate([b_, k_], axis=0))
          for uu, vv, b_, k_ in zip(u, v, bt, kt)]
    for gp in range(GP):
        s_scr[gp] = (S[gp] + jnp.where(head_mask, ds[gp], 0.0)) * p_incl[gp][C - 1:C, :]

    @pl.when(ci == pl.num_programs(2) - 1)
    def _():
        for gp in range(GP):
            s_all = s_scr[gp]
            for hh in range(NH):
                blk = slice(hh * HEAD_A, (hh + 1) * HEAD_A)
                sfin_ref[0, gp * NH + hh] = s_all[blk, blk]


def rwkv_scan(rkvg, lw, a, pvec, s0):
    _, B, T, CA = rkvg.shape
    H = CA // HEAD_A
    NH, C, GP = SCAN_NH, SCAN_C, SCAN_GP
    assert NH * C == LANES and H % (NH * GP) == 0
    L = NH * HEAD_A
    NG = H // NH
    Tp = -(-T // C) * C
    if Tp != T:
        rkvg = jnp.pad(rkvg, ((0, 0), (0, 0), (0, Tp - T), (0, 0)))
        lw, a = (jnp.pad(z, ((0, 0), (0, Tp - T), (0, 0))) for z in (lw, a))
    seq_spec = pl.BlockSpec((1, C, GP * L), lambda bi, gi, ci: (bi, ci, gi))
    proj_spec = lambda j: pl.BlockSpec((None, 1, C, GP * L), lambda bi, gi, ci: (j, bi, ci, gi))
    st_spec = pl.BlockSpec((1, GP * NH, HEAD_A, HEAD_A), lambda bi, gi, ci: (bi, gi, 0, 0))
    o, s_fin = pl.pallas_call(
        functools.partial(_scan_kernel, C=C, NH=NH, GP=GP),
        grid=(B, NG // GP, Tp // C),
        in_specs=[proj_spec(j) for j in range(4)] + [seq_spec, seq_spec,
                  pl.BlockSpec((5, GP * L), lambda bi, gi, ci: (0, gi)), st_spec],
        out_specs=[seq_spec, st_spec],
        out_shape=[jax.ShapeDtypeStruct((B, Tp, CA), BF16), jax.ShapeDtypeStruct((B, H, HEAD_A, HEAD_A), F32)],
        scratch_shapes=[pltpu.VMEM((GP, L, L), F32)],
        compiler_params=_cparams(("parallel", "parallel", "arbitrary")),
        name="rwkv_scan",
    )(rkvg, rkvg, rkvg, rkvg, lw, a, pvec, s0)
    return o[:, :T], s_fin


def _gelu_tanh(x):
    c = math.sqrt(2.0 / math.pi)
    return 0.5 * x * (1.0 + jnp.tanh(c * (x + 0.044715 * (x * x * x))))


def _compress_kernel(pt_ref, *refs, PGS):
    del pt_ref
    page_refs = refs[:PGS]
    next_ref, pe_ref, w1_ref, w2_ref, out_ref = refs[PGS:]
    CPP = page_refs[0].shape[1]
    NCH = PGS * CPP
    CG = 2 * G_KV
    M = (NCH + 1) * CG

    def rows_of(l, hf):
        pe = pe_ref[hf, l]
        parts = [(page_refs[i][0, :, l] + pe[None]).reshape(CPP * CG, HEAD_B) for i in range(PGS)]
        parts.append(next_ref[0, 0, l] + pe)
        return jnp.concatenate(parts, axis=0)

    top = jnp.zeros((M, 2 * HEAD_B), F32)
    bot = jnp.zeros((M, 2 * HEAD_B), F32)
    for l in range(0, S_CMP, 2):
        wrows = pl.ds(l * HEAD_B, 2 * HEAD_B)
        xt = jnp.concatenate([rows_of(l, 0), rows_of(l + 1, 0)], axis=1).astype(BF16)
        top = top + jnp.dot(xt, w1_ref[0, wrows, :], preferred_element_type=F32)
        xb = jnp.concatenate([rows_of(l, 1), rows_of(l + 1, 1)], axis=1).astype(BF16)
        bot = bot + jnp.dot(xb, w1_ref[1, wrows, :], preferred_element_type=F32)
    is_k = (lax.broadcasted_iota(jnp.int32, (M, 1), 0) % CG) < G_KV
    pick = lambda z, n: jnp.where(is_k[:n], z[:n, :HEAD_B], z[:n, HEAD_B:])
    hcur = pick(top, NCH * CG) + pick(bot, M)[CG:]
    o2 = jnp.dot(_gelu_tanh(hcur).astype(BF16), w2_ref[...], preferred_element_type=F32)
    out_ref[0] = pick(o2, NCH * CG).reshape(NCH, CG, HEAD_B)


def compress_kv(pool5, table, pe_cmp, w1, w2):
    NP, CPP = pool5.shape[:2]
    B, n_pages = table.shape
    PGS = max(d for d in (16, 8, 4, 2, 1) if n_pages % d == 0)
    NCH = PGS * CPP
    CG = 2 * G_KV
    half = S_CMP * HEAD_B
    pe_r = jnp.repeat(pe_cmp.reshape(2, 2, S_CMP, HEAD_B).transpose(1, 2, 0, 3), G_KV, axis=2)
    w1_r = w1.reshape(2, 2, half, HEAD_B).transpose(1, 2, 0, 3).reshape(2, half, 2 * HEAD_B)
    w2_r = jnp.concatenate([w2[0], w2[1]], axis=1)

    def page_map(i):
        return lambda b, s, pt: (pt[b, s * PGS + i], 0, 0, 0, 0)

    def next_map(b, s, pt):
        return (pt[b, jnp.minimum((s + 1) * PGS, n_pages - 1)], 0, 0, 0, 0)

    const = lambda n: (lambda b, s, pt: (0,) * n)
    grid_spec = pltpu.PrefetchScalarGridSpec(
        num_scalar_prefetch=1,
        grid=(B, n_pages // PGS),
        in_specs=[pl.BlockSpec((1, CPP, S_CMP, CG, HEAD_B), page_map(i)) for i in range(PGS)] + [
            pl.BlockSpec((1, 1, S_CMP, CG, HEAD_B), next_map),
            pl.BlockSpec((2, S_CMP, CG, HEAD_B), const(4)),
            pl.BlockSpec((2, half, 2 * HEAD_B), const(3)),
            pl.BlockSpec((HEAD_B, 2 * HEAD_B), const(2)),
        ],
        out_specs=pl.BlockSpec((1, NCH, CG, HEAD_B), lambda b, s, pt: (b, s, 0, 0)),
    )
    return pl.pallas_call(
        functools.partial(_compress_kernel, PGS=PGS),
        grid_spec=grid_spec,
        out_shape=jax.ShapeDtypeStruct((B, n_pages * CPP, CG, HEAD_B), F32),
        compiler_params=_cparams(("parallel", "arbitrary")),
        name="compress_kv",
    )(table, *([pool5] * PGS), pool5, pe_r, w1_r, w2_r)


def _stack_heads(q, HG):
    return jnp.concatenate([q[:, h * HEAD_B:(h + 1) * HEAD_B] for h in range(HG)], axis=0)


def _masked_softmax_rows(s, mask):
    s = jnp.where(mask, s, NEG)
    m = jnp.max(s, axis=-1, keepdims=True)
    e = jnp.where(mask, jnp.exp(s - m), 0.0)
    l = jnp.sum(e, axis=-1, keepdims=True)
    return e / jnp.where(l > 0.0, l, 1.0)


def _attend_stacked(s, dist, mask, v, slopes_ref, g, HG, tq):
    ps = []
    psum = jnp.zeros(dist.shape, F32)
    for h in range(HG):
        p = _masked_softmax_rows(s[h * tq:(h + 1) * tq] - slopes_ref[g * HG + h] * dist, mask)
        psum = psum + p
        ps.append(p.astype(BF16))
    o = jnp.dot(jnp.concatenate(ps, axis=0), v.astype(BF16), preferred_element_type=F32)
    return o, psum


def _unstack_store(o_ref, o, HG, tq):
    for h in range(HG):
        o_ref[0, :, h * HEAD_B:(h + 1) * HEAD_B] = o[h * tq:(h + 1) * tq]


def _nsa_cmp_kernel(slopes_ref, q_ref, kc_ref, vc_ref, o_ref, selm_ref, *, tq, HG, nc, nsb, pos0):
    g = pl.program_id(1)
    qt = pl.program_id(2)
    NCp = kc_ref.shape[1]
    NSBp = selm_ref.shape[3]
    q_st = _stack_heads(q_ref[0] * (HEAD_B ** -0.5), HG)
    s = _bdot_nt(q_st, kc_ref[0])
    qpos = pos0 + qt * tq + lax.broadcasted_iota(jnp.int32, (tq, 1), 0)
    cidx = lax.broadcasted_iota(jnp.int32, (1, NCp), 1)
    cend = S_CMP * cidx + (L_CMP - 1)
    mask = (cend <= qpos) & (cidx < nc)
    dist = (qpos - cend).astype(F32)
    o, imp_c = _attend_stacked(s, dist, mask, vc_ref[0], slopes_ref, g, HG, tq)
    _unstack_store(o_ref, o, HG, tq)

    crow = lax.broadcasted_iota(jnp.int32, (NCp, NSBp), 0)
    jcol = lax.broadcasted_iota(jnp.int32, (NCp, NSBp), 1)
    overlap = ((S_CMP * crow < L_SEL * (jcol + 1)) & (S_CMP * crow + L_CMP > L_SEL * jcol)
               & (crow < nc)).astype(F32)
    imp = jnp.dot(imp_c, overlap, precision=lax.Precision.HIGHEST, preferred_element_type=F32)
    lane = lax.broadcasted_iota(jnp.int32, (tq, NSBp), 1)
    cur = jnp.right_shift(qpos, int(math.log2(L_SEL)))
    forced = ((lane == 0) | (lane == cur) | (lane == cur - 1)).astype(F32)
    score = jnp.where(lane <= cur, imp + FORCE_BONUS * forced, NEG)
    score = jnp.where(lane < nsb, score, -3e38)

    if tq % LANES == 0 and NSBp == LANES:
        nr = -(-nsb // SUBLANES) * SUBLANES
        st = score.T[:nr]
        sub = lax.broadcasted_iota(jnp.int32, (nr, tq), 0)
        cnt = jnp.zeros((nr, tq), F32)
        for i in range(nsb):
            row = st[i:i + 1, :]
            beats = (row > st) | ((row == st) & (sub > i))
            cnt = cnt + jnp.where(beats, 1.0, 0.0)
        sel_t = jnp.where((cnt < TOPK_SEL) & (st > 0.5 * NEG), 1.0, 0.0)
        selm_ref[0, 0] = jnp.concatenate([sel_t, jnp.zeros((NSBp - nr, tq), F32)], axis=0).T
    else:
        cnt = jnp.zeros((tq, NSBp), F32)
        for i in range(nsb):
            col = score[:, i:i + 1]
            beats = (col > score) | ((col == score) & (lane > i))
            cnt = cnt + jnp.where(beats, 1.0, 0.0)
        sel = (cnt < TOPK_SEL) & (score > 0.5 * NEG)
        selm_ref[0, 0] = sel.astype(F32)


def nsa_cmp(proj, kvc, slopes, *, tq, nc, nsb, pos0):
    B, T, _ = proj.shape
    HG = slopes.shape[0] // G_KV
    NCp = kvc.shape[1]
    NSBp = -(-nsb // LANES) * LANES
    gw = HG * HEAD_B
    return pl.pallas_call(
        functools.partial(_nsa_cmp_kernel, tq=tq, HG=HG, nc=nc, nsb=nsb, pos0=pos0),
        grid=(B, G_KV, T // tq),
        in_specs=[pl.BlockSpec(memory_space=pltpu.SMEM),
                  pl.BlockSpec((1, tq, gw), lambda b, g, t: (b, t, g)),
                  pl.BlockSpec((1, NCp, HEAD_B), lambda b, g, t: (b, 0, g)),
                  pl.BlockSpec((1, NCp, HEAD_B), lambda b, g, t: (b, 0, G_KV + g))],
        out_specs=[pl.BlockSpec((1, tq, gw), lambda b, g, t: (b, t, g)),
                   pl.BlockSpec((1, 1, tq, NSBp), lambda b, g, t: (b, g, t, 0))],
        out_shape=[jax.ShapeDtypeStruct((B, T, G_KV * gw), F32),
                   jax.ShapeDtypeStruct((B, G_KV, T, NSBp), F32)],
        compiler_params=_cparams(("parallel", "parallel", "parallel")),
        name="nsa_cmp",
    )(slopes, proj, kvc.reshape(B, NCp, -1), kvc.reshape(B, NCp, -1))


LOG2E = 1.4426950408889634


def _bf16_part(x):
    return x.astype(BF16).astype(F32)


def _alibi_lhs(q, slope_col):
    c = slope_col * LOG2E
    c1 = _bf16_part(c)
    c2 = _bf16_part(c - c1)
    c3 = _bf16_part(c - c1 - c2)
    lane = lax.broadcasted_iota(jnp.int32, q.shape, 1)
    extra = jnp.where((lane == 0) | (lane == 3), c1,
                      jnp.where((lane == 1) | (lane == 4), c2, jnp.where((lane == 2) | (lane == 5), c3, 0.0)))
    return jnp.concatenate([q, extra], axis=1).astype(BF16)


def _alibi_rhs(k, k0):
    pos = k0 + lax.broadcasted_iota(jnp.int32, k.shape, 0)
    lane = lax.broadcasted_iota(jnp.int32, k.shape, 1)
    hi = jnp.bitwise_and(pos, -L_SEL)
    extra = jnp.where(lane < 3, hi, jnp.where(lane < 6, pos - hi, 0)).astype(F32)
    return jnp.concatenate([k, extra], axis=1).astype(BF16)


def _with_ones(v):
    lane = lax.broadcasted_iota(jnp.int32, v.shape, 1)
    return jnp.concatenate([v, jnp.where(lane == 0, 1.0, 0.0)], axis=1).astype(BF16)


def _silu(x):
    return x * _sigmoid(x)


def _nsa_selwin_prompt_kernel(slopes_ref, q_ref, selm_ref, ks_ref, vs_ref, kw_ref, vw_ref,
                              ocmp_ref, zc_ref, zs_ref, zw_ref, gate_ref, o_ref, s_scr, *, tq, HG, T, WS, SEG):
    g = pl.program_id(1)
    qt = pl.program_id(2)
    NSBp = selm_ref.shape[3]
    R = HG * tq
    qpos = qt * tq + lax.broadcasted_iota(jnp.int32, (tq, 1), 0)
    slope_col = jnp.concatenate([jnp.full((tq, 1), slopes_ref[g * HG + h], F32) for h in range(HG)], axis=0)
    tile_heads = lambda x: jnp.concatenate([x] * HG, axis=0)
    q2 = _alibi_lhs(_stack_heads(q_ref[0] * (HEAD_B ** -0.5 * LOG2E), HG), slope_col)

    selm_b = selm_ref[0, 0].astype(BF16)
    nseg = (qt * tq + tq + SEG - 1) // SEG

    def seg_scores(si, m):
        k0 = pl.multiple_of(si * SEG, SEG)
        kcol = k0 + lax.broadcasted_iota(jnp.int32, (NSBp, SEG), 1)
        expand = (jnp.right_shift(kcol, int(math.log2(L_SEL)))
                  == lax.broadcasted_iota(jnp.int32, (NSBp, SEG), 0)).astype(BF16)
        in_blk = jnp.dot(selm_b, expand, preferred_element_type=F32) > 0.5
        kpos = k0 + lax.broadcasted_iota(jnp.int32, (1, SEG), 1)
        mask = tile_heads(in_blk & (kpos <= qpos))
        s = lax.dot_general(q2, _alibi_rhs(ks_ref[0, pl.ds(k0, SEG), :], k0), NT_DIMS, preferred_element_type=F32)
        s = jnp.where(mask, s, NEG)
        s_scr[si] = s
        return jnp.maximum(m, jnp.max(s, axis=-1, keepdims=True))

    m = lax.fori_loop(0, nseg, seg_scores, jnp.full((R, 1), NEG, F32))

    def seg_pv(si, acc):
        k0 = pl.multiple_of(si * SEG, SEG)
        e = jnp.exp2(s_scr[si] - m).astype(BF16)
        return acc + jnp.dot(e, _with_ones(vs_ref[0, pl.ds(k0, SEG), :]), preferred_element_type=F32)

    acc = lax.fori_loop(0, nseg, seg_pv, jnp.zeros((R, 2 * HEAD_B), F32))
    o = acc[:, :HEAD_B] / acc[:, HEAD_B:HEAD_B + 1]

    start = pl.multiple_of(jnp.clip(qt * tq - WINDOW, 0, T - WS), SUBLANES)
    distw = qpos - (start + lax.broadcasted_iota(jnp.int32, (1, WS), 1))
    maskw = tile_heads((distw >= 0) & (distw < WINDOW))
    sw = lax.dot_general(q2, _alibi_rhs(kw_ref[0, pl.ds(start, WS), :], start), NT_DIMS, preferred_element_type=F32)
    sw = jnp.where(maskw, sw, NEG)
    ew = jnp.exp2(sw - jnp.max(sw, axis=-1, keepdims=True)).astype(BF16)
    accw = jnp.dot(ew, _with_ones(vw_ref[0, pl.ds(start, WS), :]), preferred_element_type=F32)
    ow = accw[:, :HEAD_B] / accw[:, HEAD_B:HEAD_B + 1]

    HB = G_KV * HG
    gates = _sigmoid(gate_ref[0])
    lane = lax.broadcasted_iota(jnp.int32, gates.shape, 1)
    gate_col = lambda idx: jnp.sum(jnp.where(lane == idx, gates, 0.0), axis=1, keepdims=True)
    for h in range(HG):
        hs = slice(h * HEAD_B, (h + 1) * HEAD_B)
        rs = slice(h * tq, (h + 1) * tq)
        hd = g * HG + h
        y = (gate_col(hd) * ocmp_ref[0, :, hs] * _silu(zc_ref[0, :, hs])
             + gate_col(HB + hd) * o[rs] * _silu(zs_ref[0, :, hs])
             + gate_col(2 * HB + hd) * ow[rs] * _silu(zw_ref[0, :, hs]))
        o_ref[0, :, hs] = y.astype(o_ref.dtype)


def nsa_selwin_prompt(proj, rows, selm, o_cmp, slopes, *, tq):
    B, T, _ = proj.shape
    HG = slopes.shape[0] // G_KV
    NSBp = selm.shape[3]
    gw = HG * HEAD_B
    CB = G_KV * gw
    WS = min(T, WINDOW + tq)
    SEG = min(T, 512)
    assert T % SEG == 0
    kv_spec = lambda c: pl.BlockSpec((1, T, HEAD_B), lambda b, g, t: (b, 0, c * G_KV + g))
    head_spec = lambda blk: pl.BlockSpec((1, tq, gw), lambda b, g, t: (b, t, blk * G_KV + g))
    return pl.pallas_call(
        functools.partial(_nsa_selwin_prompt_kernel, tq=tq, HG=HG, T=T, WS=WS, SEG=SEG),
        grid=(B, G_KV, T // tq),
        in_specs=[pl.BlockSpec(memory_space=pltpu.SMEM),
                  head_spec(0),
                  pl.BlockSpec((1, 1, tq, NSBp), lambda b, g, t: (b, g, t, 0)),
                  kv_spec(2), kv_spec(3), kv_spec(4), kv_spec(5),
                  head_spec(0), head_spec(1), head_spec(2), head_spec(3),
                  pl.BlockSpec((1, tq, LANES), lambda b, g, t: (b, t, 4 * CB // LANES))],
        out_specs=head_spec(0),
        out_shape=jax.ShapeDtypeStruct((B, T, CB), BF16),
        scratch_shapes=[pltpu.VMEM((T // SEG, HG * tq, SEG), F32)],
        compiler_params=_cparams(("parallel", "parallel", "parallel")),
        name="nsa_selwin_prompt",
    )(slopes, proj, selm, rows, rows, rows, rows, o_cmp, proj, proj, proj, proj)


def _nsa_selwin_sample_kernel(pt_ref, slopes_ref, q_ref, selm_ref, selst_ref, *refs, PGS, PS, HG, TQ, pos0, n_new, n_win):
    del pt_ref
    page_refs = refs[:PGS]
    (new_ref, cwin_ref, ocmp_ref, zc_ref, zs_ref, zw_ref, gate_ref, o_ref,
     m_scr, l_scr, acc_scr) = refs[PGS:]
    st = pl.program_id(1)
    NSBp = selm_ref.shape[3]
    GW = G_KV * HEAD_B
    sel_shift = int(math.log2(L_SEL))

    @pl.when(st == 0)
    def _():
        m_scr[...] = jnp.full(m_scr.shape, NEG, F32)
        l_scr[...] = jnp.zeros(l_scr.shape, F32)
        acc_scr[...] = jnp.zeros(acc_scr.shape, F32)

    qpos = pos0 + lax.broadcasted_iota(jnp.int32, (TQ, 1), 0)
    lane_j = lax.broadcasted_iota(jnp.int32, (TQ, NSBp), 1)
    tile_heads = lambda x: jnp.concatenate([x] * HG, axis=0)

    def sel_col(selm_g, j):
        return jnp.sum(jnp.where(lane_j == j, selm_g, 0.0), axis=1, keepdims=True)

    def online_update(s, mask, v):
        gs = range(G_KV)
        m_old = [m_scr[g] for g in gs]
        m_new = [jnp.maximum(m_old[g], jnp.max(jnp.where(mask[g], s[g], NEG), axis=-1, keepdims=True)) for g in gs]
        e = [jnp.where(mask[g], jnp.exp(s[g] - m_new[g]), 0.0) for g in gs]
        alpha = [jnp.exp(m_old[g] - m_new[g]) for g in gs]
        pv = [_bdot(e[g], v[g]) for g in gs]
        for g in gs:
            l_scr[g] = alpha[g] * l_scr[g] + jnp.sum(e[g], axis=-1, keepdims=True)
            acc_scr[g] = alpha[g] * acc_scr[g] + pv[g]
            m_scr[g] = m_new[g]

    NK = PGS * PS
    kpos = st * NK + lax.broadcasted_iota(jnp.int32, (1, NK), 1)
    expand = (jnp.right_shift(lax.broadcasted_iota(jnp.int32, (LANES, NK), 1), sel_shift)
              == lax.broadcasted_iota(jnp.int32, (LANES, NK), 0)).astype(BF16)
    in_blk_all = jnp.dot(selst_ref[0, 0].astype(BF16), expand, preferred_element_type=F32)
    dist = qpos - kpos
    distf = tile_heads(dist.astype(F32))
    gs = range(G_KV)
    q_st = [_stack_heads(q_ref[0, :, g * HG * HEAD_B:(g + 1) * HG * HEAD_B] * (HEAD_B ** -0.5), HG).astype(BF16)
            for g in gs]
    slope_col = [jnp.concatenate([jnp.full((TQ, 1), slopes_ref[g * HG + h], F32) for h in range(HG)], axis=0)
                 for g in gs]
    slabs = [pltpu.einshape("rcd->crd", page_refs[i][0].reshape(PS, 2 * G_KV, HEAD_B)) for i in range(PGS)]
    k = [jnp.concatenate([slabs[i][g] for i in range(PGS)], axis=0) for g in gs]
    v = [jnp.concatenate([slabs[i][G_KV + g] for i in range(PGS)], axis=0) for g in gs]
    mask = [tile_heads((in_blk_all[g * TQ:(g + 1) * TQ] > 0.5) & (dist >= 0)) for g in gs]
    s = [_bdot_nt(q_st[g], k[g]) - slope_col[g] * distf for g in gs]
    online_update(s, mask, v)

    @pl.when(st == pl.num_programs(1) - 1)
    def _():
        NN = new_ref.shape[1]
        rnew = lax.broadcasted_iota(jnp.int32, (1, NN), 1)
        kpos_n = pos0 + rnew
        dist_n = qpos - kpos_n
        ok_n = (rnew < n_new) & (dist_n >= 0)
        jn = pos0 >> sel_shift
        kpos_w = pos0 - n_win + lax.broadcasted_iota(jnp.int32, (1, n_win), 1)
        dist_w = qpos - kpos_w
        dist_wall = jnp.concatenate([dist_w, dist_n], axis=1)
        mask_wall = jnp.concatenate([(dist_w >= 0) & (dist_w < WINDOW), ok_n & (dist_n < WINDOW)], axis=1)
        kn = [new_ref[0, :, 2 * GW + g * HEAD_B:2 * GW + (g + 1) * HEAD_B] for g in gs]
        vn = [new_ref[0, :, 3 * GW + g * HEAD_B:3 * GW + (g + 1) * HEAD_B] for g in gs]
        mask_n = [tile_heads((sel_col(selm_ref[0, g], jn) > 0.5) & ok_n) for g in gs]
        dist_nf = tile_heads(dist_n.astype(F32))
        s_n = [_bdot_nt(q_st[g], kn[g]) - slope_col[g] * dist_nf for g in gs]
        online_update(s_n, mask_n, vn)
        kw = [jnp.concatenate([cwin_ref[0, :, g * HEAD_B:(g + 1) * HEAD_B],
                               new_ref[0, :, 4 * GW + g * HEAD_B:4 * GW + (g + 1) * HEAD_B]], axis=0) for g in gs]
        vw = [jnp.concatenate([cwin_ref[0, :, GW + g * HEAD_B:GW + (g + 1) * HEAD_B],
                               new_ref[0, :, 5 * GW + g * HEAD_B:5 * GW + (g + 1) * HEAD_B]], axis=0) for g in gs]
        dist_wf = tile_heads(dist_wall.astype(F32))
        mask_w = tile_heads(mask_wall)
        sw = [_bdot_nt(q_st[g], kw[g]) - slope_col[g] * dist_wf for g in gs]
        pw = [_masked_softmax_rows(sw[g], mask_w) for g in gs]
        ow = [_bdot(pw[g], vw[g]) for g in gs]
        HB = G_KV * HG
        gates = _sigmoid(gate_ref[0])
        for g in gs:
            l = l_scr[g]
            o = acc_scr[g] / jnp.where(l > 0.0, l, 1.0)
            for h in range(HG):
                hd = g * HG + h
                hs = slice(hd * HEAD_B, (hd + 1) * HEAD_B)
                rs = slice(h * TQ, (h + 1) * TQ)
                y = (gates[:, hd:hd + 1] * ocmp_ref[0, :, hs] * _silu(zc_ref[0, :, hs])
                     + gates[:, HB + hd:HB + hd + 1] * o[rs] * _silu(zs_ref[0, :, hs])
                     + gates[:, 2 * HB + hd:2 * HB + hd + 1] * ow[g][rs] * _silu(zw_ref[0, :, hs]))
                o_ref[0, :, hs] = y.astype(o_ref.dtype)


def nsa_selwin_sample(proj, selm, o_cmp, pool5, table, new_rows, cwin, slopes, *, pos0, n_new):
    B, TQ, _ = proj.shape
    HG = slopes.shape[0] // G_KV
    NP, CPP = pool5.shape[:2]
    PS = CPP * S_CMP
    n_pages = table.shape[1]
    NSBp = selm.shape[3]
    PGS = max(d for d in (8, 4, 2, 1) if n_pages % d == 0)
    CB = G_KV * HG * HEAD_B
    GW = G_KV * HEAD_B
    NN = new_rows.shape[1]
    n_win = cwin.shape[1]
    assert pos0 % L_SEL == 0 and n_new <= L_SEL and pos0 == n_pages * PS

    def page_map(i):
        return lambda b, s, pt: (pt[b, s * PGS + i], 0, 0, 1, 0)

    n_steps = n_pages // PGS
    bps = PGS * PS // L_SEL
    assert bps <= LANES
    selst = selm[:, :, :, :n_steps * bps].reshape(B, G_KV, TQ, n_steps, bps).transpose(0, 3, 1, 2, 4)
    selst = jnp.pad(selst.reshape(B, n_steps, G_KV * TQ, bps), ((0, 0), (0, 0), (0, 0), (0, LANES - bps)))

    const = lambda b, s, pt: (b, 0, 0)
    wide = lambda blk: pl.BlockSpec((1, TQ, CB), lambda b, s, pt: (b, 0, blk))
    grid_spec = pltpu.PrefetchScalarGridSpec(
        num_scalar_prefetch=1,
        grid=(B, n_steps),
        in_specs=[pl.BlockSpec(memory_space=pltpu.SMEM),
                  pl.BlockSpec((1, TQ, CB), const),
                  pl.BlockSpec((1, G_KV, TQ, NSBp), lambda b, s, pt: (b, 0, 0, 0)),
                  pl.BlockSpec((1, 1, G_KV * TQ, LANES), lambda b, s, pt: (b, s, 0, 0))]
                 + [pl.BlockSpec((1, CPP, S_CMP, 2 * G_KV, HEAD_B), page_map(i)) for i in range(PGS)]
                 + [pl.BlockSpec((1, NN, 6 * GW), const),
                    pl.BlockSpec((1, n_win, 2 * GW), const),
                    wide(0), wide(1), wide(2), wide(3),
                    pl.BlockSpec((1, TQ, LANES), lambda b, s, pt: (b, 0, 4 * CB // LANES))],
        out_specs=wide(0),
        scratch_shapes=[pltpu.VMEM((G_KV, HG * TQ, 1), F32),
                        pltpu.VMEM((G_KV, HG * TQ, 1), F32),
                        pltpu.VMEM((G_KV, HG * TQ, HEAD_B), F32)],
    )
    return pl.pallas_call(
        functools.partial(_nsa_selwin_sample_kernel, PGS=PGS, PS=PS, HG=HG, TQ=TQ, pos0=pos0,
                          n_new=n_new, n_win=n_win),
        grid_spec=grid_spec,
        out_shape=jax.ShapeDtypeStruct((B, TQ, CB), F32),
        compiler_params=_cparams(("parallel", "arbitrary")),
        name="nsa_selwin_sample",
    )(table, slopes, proj, selm, selst, *([pool5] * PGS), new_rows, cwin, o_cmp, proj, proj, proj, proj)


def _rwkv_layer(h, x_prev, s0, i, W, B, T):
    N, D = h.shape
    g = W["norm_g"][i]
    rkvg = rwkv_in(h, x_prev, g, W["mu_a"][i], W["w_in_a"], i, T)
    CA = rkvg.shape[-1]
    lw, a = rwkv_lora(h, x_prev, g, W["mu_a"][i], W["w_lora_w1"][i], W["w_lora_w2"][i], W["a_lora1"][i],
                      W["a_lora2"][i], W["w0_a"][i], W["a0_a"][i], T)
    pvec = jnp.stack([W["k_k"][i], W["k_a"][i], W["r_k"][i].reshape(CA), W["ln_x_w"][i], W["ln_x_b"][i]])
    o, s_fin = rwkv_scan(rkvg.reshape(4, B, T, CA), lw.reshape(B, T, CA), a.reshape(B, T, CA), pvec, s0)
    last = rmsnorm(h.reshape(B, T, D)[:, -1], g)
    return o.reshape(N, CA), s_fin, last


def _nsa_layer(h, jb, shared, W, slopes, B, T, norm_g):
    N, D = h.shape
    CB = W["w_out_b"].shape[1]
    proj3 = norm_mm(h, norm_g, W["w_in_b"], (jb,)).reshape(B, T, -1)
    if shared["past"] is None:
        o_cmp, selm = nsa_cmp(proj3, shared["kvc"], slopes, tq=min(T, 256), nc=shared["nc"],
                              nsb=shared["nsb"], pos0=0)
        o = nsa_selwin_prompt(proj3, shared["rows"], selm, o_cmp, slopes, tq=min(T, 256))
    else:
        TQ = SUBLANES
        projp = jnp.pad(proj3, ((0, 0), (0, TQ - T), (0, 0)))
        o_cmp, selm = nsa_cmp(projp, shared["kvc"], slopes, tq=TQ, nc=shared["nc"],
                              nsb=shared["nsb"], pos0=shared["pos0"])
        pool, table, cwin = shared["past"]
        o = nsa_selwin_sample(projp, selm, o_cmp, pool, table, shared["new_rows"], cwin, slopes,
                              pos0=shared["pos0"], n_new=T)[:, :T]
    return o.reshape(N, CB)


def _trunk(x, p, pos0, wkv0, shift0, past, W, slopes):
    B, T, D = x.shape
    N = B * T
    depth = p.shape[0]
    n_a = W["w_in_a"].shape[0]
    GW = G_KV * HEAD_B
    h = x.reshape(N, D)
    wkv_new, shift_new = [], []
    shared, kv_rows, win_state = None, None, None
    for i in range(depth):
        if i < n_a:
            o, s_fin, last = _rwkv_layer(h, shift0[i], wkv0[i], i, W, B, T)
            wkv_new.append(s_fin)
            shift_new.append(last)
            h = out_ple(o, h, p[i].reshape(N, -1), W["w_out_a"], i, W["w_ple"], W["w_ple_gate"], i)
        else:
            o = _nsa_layer(h, i - n_a, shared, W, slopes, B, T, W["norm_g"][i])
            h = out_ple(o, h, p[i].reshape(N, -1), W["w_out_b"], i - n_a, W["w_ple"], W["w_ple_gate"], i)
        if i == n_a - 1:
            rows = norm_mm(h, W["kv_norm_g"], W["w_kv"]).reshape(B, T, 6 * GW)
            kv_rows = rows[:, :, :4 * GW].reshape(B, T, 4, G_KV, HEAD_B)
            win_new = rows[:, :, 4 * GW:].reshape(B, T, 2, G_KV, HEAD_B)
            if past is None:
                PS = 128
                pool = rows.reshape(B * T // PS, PS // S_CMP, S_CMP, 6 * G_KV, HEAD_B)
                table = jnp.arange(B * T // PS, dtype=jnp.int32).reshape(B, T // PS)
                t_all = T
                win_all = win_new
                shared = {"past": None, "rows": rows}
            else:
                pool, table, cwin = past
                PS = pool.shape[1] * S_CMP
                t_all = pos0 + T
                win_all = jnp.concatenate([cwin.reshape(B, -1, 2, G_KV, HEAD_B), win_new], axis=1)
                NN = LANES
                shared = {"past": past, "new_rows": jnp.pad(rows, ((0, 0), (0, NN - T), (0, 0)))}
            win_state = win_all[:, win_all.shape[1] - min(WINDOW, pos0 + T):]
            nc = (t_all - L_CMP) // S_CMP + 1
            assert nc < table.shape[1] * PS // S_CMP
            kvc = compress_kv(pool, table, W["pe_cmp"], W["w_cmp1"], W["w_cmp2"])
            shared.update(kvc=kvc, nc=nc, nsb=max(-(-t_all // L_SEL), TOPK_SEL), pos0=pos0)
    y = rmsnorm(h, W["final_norm_g"]).reshape(B, T, D)
    return y, jnp.stack(wkv_new), jnp.stack(shift_new), kv_rows, win_state


def kernel(x_prompt, x_sample, state_wkv, state_shift, cache_kv, cache_win_kv, page_table, p_prompt, p_sample, norm_g, mu_a, w_in_a, w_lora_w1, w_lora_w2, w0_a, a_lora1, a_lora2, a0_a, k_k, k_a, r_k, ln_x_w, ln_x_b, w_out_a, w_in_b, w_out_b, kv_norm_g, w_kv, pe_cmp, w_cmp1, w_cmp2, w_ple, w_ple_gate, final_norm_g):
    bf = lambda w: w.astype(BF16)
    CA = w_out_a.shape[1]
    W = dict(norm_g=norm_g, mu_a=mu_a, w_in_a=bf(w_in_a), w_lora_w1=bf(w_lora_w1), w_lora_w2=bf(w_lora_w2),
             w0_a=w0_a, a_lora1=bf(a_lora1), a_lora2=bf(a_lora2), a0_a=a0_a, k_k=k_k, k_a=k_a,
             r_k=r_k, ln_x_w=ln_x_w, ln_x_b=ln_x_b,
             w_out_a=bf(w_out_a), w_in_b=bf(w_in_b), w_out_b=bf(w_out_b), kv_norm_g=kv_norm_g, w_kv=bf(w_kv),
             pe_cmp=pe_cmp, w_cmp1=bf(w_cmp1), w_cmp2=bf(w_cmp2), w_ple=bf(w_ple), w_ple_gate=bf(w_ple_gate),
             final_norm_g=final_norm_g)
    HB = w_out_b.shape[1] // HEAD_B
    slopes = 2.0 ** (-8.0 * jnp.arange(1, HB + 1, dtype=F32) / HB)
    bp = x_prompt.shape[0]
    n_a = w_in_a.shape[0]
    D = x_prompt.shape[-1]
    wkv0 = jnp.zeros((n_a, bp, CA // HEAD_A, HEAD_A, HEAD_A), F32)
    shift0 = jnp.zeros((n_a, bp, D), F32)
    y_p, wkv_p, shift_p, kv_p, win_p = _trunk(x_prompt, p_prompt, 0, wkv0, shift0, None, W, slopes)
    db, n_pages = page_table.shape
    NP, PS = cache_kv.shape[:2]
    pool5 = cache_kv.reshape(NP, PS // S_CMP, S_CMP, -1, HEAD_B)
    past = (pool5, page_table, cache_win_kv.reshape(db, cache_win_kv.shape[1], -1))
    y_s, wkv_s, shift_s, kv_s, win_s = _trunk(x_sample, p_sample, n_pages * PS, state_wkv, state_shift, past, W, slopes)
    return (y_p, y_s, wkv_p, shift_p, kv_p, win_p, wkv_s, shift_s, kv_s, win_s)
```

```python
import functools
import math

import jax
import jax.numpy as jnp
from jax import lax
from jax.experimental import pallas as pl
from jax.experimental.pallas import tpu as pltpu

F32 = jnp.float32
BF16 = jnp.bfloat16

HEAD_A = 64
GN_EPS = 64e-5
HEAD_B = 128
G_KV = 4
L_CMP = 32
S_CMP = 16
L_SEL = 64
TOPK_SEL = 16
WINDOW = 512
RMS_EPS = 1e-6
NEG = -1e30
FORCE_BONUS = 1e4

LANES = 128
SUBLANES = 8
VMEM_LIMIT = 56 * 1024 * 1024

SCAN_NH = 2
SCAN_C = 64
SCAN_GP = 16

NT_DIMS = (((1,), (1,)), ((), ()))
TN_DIMS = (((0,), (0,)), ((), ()))


def _cparams(sem):
    return pltpu.CompilerParams(dimension_semantics=sem, vmem_limit_bytes=VMEM_LIMIT)


def _bdot(a, b):
    return jnp.dot(a.astype(BF16), b.astype(BF16), preferred_element_type=F32)


def _bdot_nt(a, b):
    return lax.dot_general(a.astype(BF16), b.astype(BF16), NT_DIMS, preferred_element_type=F32)


def _bdot_tn(a, b):
    return lax.dot_general(a.astype(BF16), b.astype(BF16), TN_DIMS, preferred_element_type=F32)


def _rms_kernel(x_ref, g_ref, o_ref):
    x = x_ref[...]
    ms = jnp.mean(x * x, axis=-1, keepdims=True)
    o_ref[...] = x * lax.rsqrt(ms + RMS_EPS) * g_ref[...]


def rmsnorm(x, g):
    M, D = x.shape
    tm = min(M, 256)
    return pl.pallas_call(
        _rms_kernel,
        grid=(pl.cdiv(M, tm),),
        in_specs=[pl.BlockSpec((tm, D), lambda i: (i, 0)),
                  pl.BlockSpec((1, D), lambda i: (0, 0))],
        out_specs=pl.BlockSpec((tm, D), lambda i: (i, 0)),
        out_shape=jax.ShapeDtypeStruct((M, D), F32),
        compiler_params=_cparams(("parallel",)),
        name="rmsnorm",
    )(x, g.reshape(1, D))


def _mm_kernel(x_ref, w_ref, o_ref):
    o_ref[...] = jnp.dot(x_ref[...].astype(BF16), w_ref[...], preferred_element_type=F32)


def mm(x, w, widx=()):
    M, K = x.shape
    N = w.shape[-1]
    assert w.shape[-2] == K and len(widx) == w.ndim - 2
    tm = min(M, 1024 if K <= 2048 else 512)
    tn = N if N <= 512 else 512
    nlead = len(widx)
    w_spec = pl.BlockSpec((None,) * nlead + (K, tn), lambda i, j: tuple(widx) + (0, j))
    return pl.pallas_call(
        _mm_kernel,
        grid=(pl.cdiv(M, tm), pl.cdiv(N, tn)),
        in_specs=[pl.BlockSpec((tm, K), lambda i, j: (i, 0)), w_spec],
        out_specs=pl.BlockSpec((tm, tn), lambda i, j: (i, j)),
        out_shape=jax.ShapeDtypeStruct((M, N), F32),
        compiler_params=_cparams(("parallel", "parallel")),
        name="mm",
    )(x, w)


def _col_tile(n_rows, n_cols, tn):
    return tn if n_rows > 64 else min(n_cols, 4096)


def _norm_rows(x, g):
    return x * lax.rsqrt(jnp.mean(x * x, axis=-1, keepdims=True) + RMS_EPS) * g


def _sigmoid(x):
    return 0.5 + 0.5 * jnp.tanh(0.5 * x)


def _norm_and_shift(h_ref, hprev_ref, xprev_ref, g_ref, i, tm, T):
    g = g_ref[...]
    hn = _norm_rows(h_ref[...], g)
    prev_row = _norm_rows(hprev_ref[SUBLANES - 1:SUBLANES, :], g)
    row = lax.broadcasted_iota(jnp.int32, (tm, 1), 0)
    xs = jnp.where(row == 0, prev_row, pltpu.roll(hn, 1, axis=0))
    if T >= tm:
        assert T % tm == 0
        start = (i * tm) % T == 0
        xs = jnp.where((row == 0) & start, xprev_ref[pl.ds((i * tm) // T, 1), :], xs)
    else:
        assert tm % T == 0
        for bb in range(tm // T):
            xs = jnp.where(row == bb * T, xprev_ref[pl.ds(i * (tm // T) + bb, 1), :], xs)
    return hn, xs


def _rwkv_in_kernel(h_ref, hprev_ref, xprev_ref, g_ref, mu_ref, w_ref, o_ref, hn_scr, xs_scr, *, tm, T):
    i, j, n = pl.program_id(0), pl.program_id(1), pl.program_id(2)
    rc = min(tm, 256)
    chunks = [slice(c * rc, (c + 1) * rc) for c in range(tm // rc)]
    P = SUBLANES

    @pl.when((j == 0) & (n == 0))
    def _():
        g = g_ref[...]
        prev = _norm_rows(hprev_ref[P - 1:P, :], g)
        if T >= tm:
            assert T % tm == 0
            prev = jnp.where((i * tm) % T == 0, xprev_ref[pl.ds((i * tm) // T, 1), :], prev)
        hn_scr[0:P, :] = jnp.broadcast_to(prev, (P, prev.shape[1]))
        for ch in chunks:
            hn_scr[P + ch.start:P + ch.stop, :] = _norm_rows(h_ref[ch, :], g)

    @pl.when(n == 0)
    def _():
        mu = mu_ref[pl.ds(j, 1), :]
        for ch in chunks:
            hn = hn_scr[P + ch.start:P + ch.stop, :]
            xs = hn_scr[P - 1 + ch.start:P - 1 + ch.stop, :]
            if T < tm:
                assert tm % T == 0 and len(chunks) == 1
                row = lax.broadcasted_iota(jnp.int32, (tm, 1), 0)
                for bb in range(tm // T):
                    xs = jnp.where(row == bb * T, xprev_ref[pl.ds(i * (tm // T) + bb, 1), :], xs)
            xs_scr[ch, :] = (hn + (xs - hn) * mu).astype(BF16)

    o_ref[...] = jnp.dot(xs_scr[...], w_ref[...], preferred_element_type=F32)


def _shift_specs(tm, D, nb, ngrid):
    z = (0,) * (ngrid - 1)
    wrap = lambda f: (lambda i, *_: f(i))
    return [pl.BlockSpec((tm, D), wrap(lambda i: (i, 0))),
            pl.BlockSpec((SUBLANES, D), wrap(lambda i: (jnp.maximum(i * (tm // SUBLANES) - 1, 0), 0))),
            pl.BlockSpec((nb, D), wrap(lambda i: (0, 0))),
            pl.BlockSpec((1, D), wrap(lambda i: (0, 0))),
            pl.BlockSpec((6, D), wrap(lambda i: (0, 0)))]


def rwkv_in(h, xprev, g, mu, w, layer, T):
    N, D = h.shape
    C = w.shape[-1]
    tm = min(N, 1024)
    tn = _col_tile(N, C, 1024)
    return pl.pallas_call(
        functools.partial(_rwkv_in_kernel, tm=tm, T=T),
        grid=(N // tm, 4, C // tn),
        in_specs=_shift_specs(tm, D, xprev.shape[0], 3) + [
            pl.BlockSpec((None, None, D, tn), lambda i, j, n: (layer, j, 0, n))],
        out_specs=pl.BlockSpec((None, tm, tn), lambda i, j, n: (j, i, n)),
        out_shape=jax.ShapeDtypeStruct((4, N, C), F32),
        scratch_shapes=[pltpu.VMEM((tm + SUBLANES, D), F32), pltpu.VMEM((tm, D), BF16)],
        compiler_params=_cparams(("parallel", "arbitrary", "arbitrary")),
        name="rwkv_in",
    )(h, h, xprev, g.reshape(1, D), mu, w)


def _rwkv_lora_kernel(h_ref, hprev_ref, xprev_ref, g_ref, mu_ref, lw1_ref, lw2_ref, la1_ref, la2_ref,
                      w0_ref, a0_ref, lw_ref, a_ref, *, tm, T):
    hn, xs = _norm_and_shift(h_ref, hprev_ref, xprev_ref, g_ref, pl.program_id(0), tm, T)
    dx = xs - hn
    x4 = (hn + dx * mu_ref[4:5, :]).astype(BF16)
    x5 = (hn + dx * mu_ref[5:6, :]).astype(BF16)
    t4 = jnp.tanh(jnp.dot(x4, lw1_ref[...], preferred_element_type=F32)).astype(BF16)
    x = w0_ref[...] + jnp.dot(t4, lw2_ref[...], preferred_element_type=F32)
    lw_ref[...] = -math.exp(-0.5) * _sigmoid(x)
    t5 = jnp.dot(x5, la1_ref[...], preferred_element_type=F32).astype(BF16)
    a_ref[...] = _sigmoid(a0_ref[...] + jnp.dot(t5, la2_ref[...], preferred_element_type=F32))


def rwkv_lora(h, xprev, g, mu, lw1, lw2, la1, la2, w0, a0, T):
    N, D = h.shape
    R, C = lw2.shape
    tm = min(N, 256)
    full = lambda shape: pl.BlockSpec(shape, lambda i: (0,) * len(shape))
    o_spec = pl.BlockSpec((tm, C), lambda i: (i, 0))
    return pl.pallas_call(
        functools.partial(_rwkv_lora_kernel, tm=tm, T=T),
        grid=(N // tm,),
        in_specs=_shift_specs(tm, D, xprev.shape[0], 1) + [
            full((D, R)), full((R, C)), full((D, R)), full((R, C)), full((1, C)), full((1, C))],
        out_specs=[o_spec, o_spec],
        out_shape=[jax.ShapeDtypeStruct((N, C), F32)] * 2,
        compiler_params=_cparams(("parallel",)),
        name="rwkv_lora",
    )(h, h, xprev, g.reshape(1, D), mu, lw1, lw2, la1, la2, w0.reshape(1, C), a0.reshape(1, C))


def _norm_mm_kernel(h_ref, g_ref, w_ref, o_ref, xs_scr):
    @pl.when(pl.program_id(1) == 0)
    def _():
        rc = min(h_ref.shape[0], 256)

        def norm_chunk(c, carry):
            rows = pl.ds(pl.multiple_of(c * rc, rc), rc)
            xs_scr[rows, :] = _norm_rows(h_ref[rows, :], g_ref[...]).astype(BF16)
            return carry

        lax.fori_loop(0, h_ref.shape[0] // rc, norm_chunk, 0)

    o_ref[...] = jnp.dot(xs_scr[...], w_ref[...], preferred_element_type=F32)


def norm_mm(h, g, w, widx=()):
    N, D = h.shape
    NO = w.shape[-1]
    tm = min(N, 1024)
    tn = _col_tile(N, NO, 1024)
    nlead = len(widx)
    return pl.pallas_call(
        _norm_mm_kernel,
        grid=(N // tm, pl.cdiv(NO, tn)),
        in_specs=[pl.BlockSpec((tm, D), lambda i, n: (i, 0)),
                  pl.BlockSpec((1, D), lambda i, n: (0, 0)),
                  pl.BlockSpec((None,) * nlead + (D, tn), lambda i, n: tuple(widx) + (0, n))],
        out_specs=pl.BlockSpec((tm, tn), lambda i, n: (i, n)),
        out_shape=jax.ShapeDtypeStruct((N, NO), F32),
        scratch_shapes=[pltpu.VMEM((tm, D), BF16)],
        compiler_params=_cparams(("parallel", "arbitrary")),
        name="norm_mm",
    )(h, g.reshape(1, D), w)


def _mm_res_kernel(x_ref, h_ref, w_ref, o_ref):
    o_ref[...] = h_ref[...] + jnp.dot(x_ref[...].astype(BF16), w_ref[...], preferred_element_type=F32)


def _ple_gate_kernel(h1_ref, h1t_ref, p_ref, wp_ref, wg_ref, o_ref, xb_scr):
    @pl.when(pl.program_id(1) == 0)
    def _():
        rc = min(h1_ref.shape[0], 256)

        def cast_chunk(c, carry):
            rows = pl.ds(pl.multiple_of(c * rc, rc), rc)
            xb_scr[rows, :] = h1_ref[rows, :].astype(BF16)
            return carry

        lax.fori_loop(0, h1_ref.shape[0] // rc, cast_chunk, 0)

    gate = jnp.dot(xb_scr[...], wg_ref[...], preferred_element_type=F32)
    ple = jnp.dot(p_ref[...].astype(BF16), wp_ref[...], preferred_element_type=F32)
    o_ref[...] = h1t_ref[...] + ple * _sigmoid(gate)


def out_ple(x, h, p, w_out, oidx, w_ple, w_gate, layer):
    N, C = x.shape
    D = h.shape[1]
    DP = p.shape[1]
    tm = min(N, 1024)
    tn = _col_tile(N, D, 512)
    grid = (N // tm, D // tn)
    tile = pl.BlockSpec((tm, tn), lambda i, n: (i, n))
    h1 = pl.pallas_call(
        _mm_res_kernel,
        grid=grid,
        in_specs=[pl.BlockSpec((tm, C), lambda i, n: (i, 0)), tile,
                  pl.BlockSpec((None, C, tn), lambda i, n: (oidx, 0, n))],
        out_specs=tile,
        out_shape=jax.ShapeDtypeStruct((N, D), F32),
        compiler_params=_cparams(("parallel", "parallel")),
        name="mm_res",
    )(x, h, w_out)
    return pl.pallas_call(
        _ple_gate_kernel,
        grid=grid,
        in_specs=[pl.BlockSpec((tm, D), lambda i, n: (i, 0)), tile,
                  pl.BlockSpec((tm, DP), lambda i, n: (i, 0)),
                  pl.BlockSpec((None, DP, tn), lambda i, n: (layer, 0, n)),
                  pl.BlockSpec((None, D, tn), lambda i, n: (layer, 0, n))],
        out_specs=tile,
        out_shape=jax.ShapeDtypeStruct((N, D), F32),
        scratch_shapes=[pltpu.VMEM((tm, D), BF16)],
        compiler_params=_cparams(("parallel", "arbitrary")),
        name="ple_gate",
    )(h1, h1, p, w_ple, w_gate)


def _scan_kernel(r_ref, k_ref, v_ref, zg_ref, lw_ref, a_ref, pv_ref, s0_ref, o_ref, sfin_ref, s_scr, *, C, NH, GP):
    L = NH * HEAD_A
    NC = NH * C
    ci = pl.program_id(2)

    @pl.when(ci == 0)
    def _():
        for gp in range(GP):
            rows = []
            for hh in range(NH):
                pieces = [s0_ref[0, gp * NH + hh] if h2 == hh else jnp.zeros((HEAD_A, HEAD_A), F32)
                          for h2 in range(NH)]
                rows.append(jnp.concatenate(pieces, axis=1))
            s_scr[gp] = jnp.concatenate(rows, axis=0)

    row_c = lax.broadcasted_iota(jnp.int32, (C, NC), 0)
    col_s = lax.broadcasted_iota(jnp.int32, (C, NC), 1) % C
    tri_strict = col_s < row_c
    tri_incl = (lax.broadcasted_iota(jnp.int32, (C, 2 * NC), 1) % C
                <= lax.broadcasted_iota(jnp.int32, (C, 2 * NC), 0))
    st_mask = (lax.broadcasted_iota(jnp.int32, (NC, L), 0) // C
               == lax.broadcasted_iota(jnp.int32, (NC, L), 1) // HEAD_A)
    bd_mask = (lax.broadcasted_iota(jnp.int32, (NC, NC), 0) // C
               == lax.broadcasted_iota(jnp.int32, (NC, NC), 1) // C)
    head_mask = (lax.broadcasted_iota(jnp.int32, (L, L), 0) // HEAD_A
                 == lax.broadcasted_iota(jnp.int32, (L, L), 1) // HEAD_A)

    def st(x):
        return jnp.where(st_mask, jnp.concatenate([x] * NH, axis=0), 0.0)

    def bd(w):
        return jnp.where(bd_mask, jnp.concatenate([w] * NH, axis=0), 0.0)

    n_double = int(math.log2(C))
    each = lambda f, *cols: [f(*xs) for xs in zip(*cols)]
    sls = [slice(gp * L, (gp + 1) * L) for gp in range(GP)]
    head_of_lane = lax.broadcasted_iota(jnp.int32, (1, L), 1) // HEAD_A

    def hsum(x):
        out = None
        for hh in range(NH):
            sh = jnp.sum(jnp.where(head_of_lane == hh, x, 0.0), axis=-1, keepdims=True)
            out = sh if out is None else jnp.where(head_of_lane == hh, sh, out)
        return out

    k_k, k_a, r_k, ln_w, ln_b = ([pv_ref[n:n + 1, sl] for sl in sls] for n in range(5))
    lw = [lw_ref[0, :, sl] for sl in sls]
    a_sig = [a_ref[0, :, sl] for sl in sls]
    r = [r_ref[0, :, sl] for sl in sls]
    k_raw = [k_ref[0, :, sl] for sl in sls]
    v = [v_ref[0, :, sl] for sl in sls]
    kk = each(lambda x, w: x * w, k_raw, k_k)
    kk = each(lambda x: x * lax.rsqrt(jnp.maximum(hsum(x * x), 1e-24)), kk)
    k = each(lambda x, a, w: x * (1.0 + (a - 1.0) * w), k_raw, a_sig, k_a)
    lw_hi = each(lambda z: z.astype(BF16), lw)
    lw_lo = each(lambda z, hi: (z - hi.astype(F32)).astype(BF16), lw, lw_hi)
    cum_mat = (lax.broadcasted_iota(jnp.int32, (C, C), 1)
               <= lax.broadcasted_iota(jnp.int32, (C, C), 0)).astype(BF16)
    cum2 = each(lambda hi, lo: jnp.dot(cum_mat, jnp.concatenate([hi, lo], axis=1), preferred_element_type=F32),
                lw_hi, lw_lo)
    cum = [c2[:, :L] + c2[:, L:] for c2 in cum2]
    p_incl = each(jnp.exp, cum)
    p_inv = each(lambda z: jnp.exp(-z), cum)
    at = each(lambda x, c, w: -x * jnp.exp(c - w), kk, cum, lw)
    rt = each(lambda x, p: x * p, r, p_incl)
    bt = each(lambda x, a, p: x * a * p, kk, a_sig, p_inv)
    kt = each(lambda x, p: x * p, k, p_inv)
    S = [s_scr[gp] for gp in range(GP)]
    ar = each(lambda x, y: jnp.concatenate([x, y], axis=0), at, rt)
    bk_st = each(lambda x, y: jnp.concatenate([st(x), st(y)], axis=0), bt, kt)
    Gm = each(_bdot_nt, ar, bk_st)
    w_ab = [jnp.where(tri_strict, g[:C, :NC], 0.0) for g in Gm]
    tm = w_ab
    pw = each(lambda w: _bdot(w, bd(w)), w_ab)
    LH = each(_bdot_nt, ar, S)
    v_st = each(st, v)
    x = [lh[:C] + _bdot(jnp.where(tri_strict, g[:C, NC:], 0.0), vs) for lh, g, vs in zip(LH, Gm, v_st)]
    for it in range(1, n_double):
        if it < n_double - 1:
            both = each(lambda t, p: _bdot(p, jnp.concatenate([bd(t), bd(p)], axis=1)), tm, pw)
            tm = each(lambda t, p, b2: t + p + b2[:, :NC], tm, pw, both)
            pw = [b2[:, NC:] for b2 in both]
        else:
            tm = each(lambda t, p: t + p + _bdot(p, bd(t)), tm, pw)
    u = each(lambda xx, t: xx + _bdot(t, st(xx)), x, tm)
    o = [lh[C:] + _bdot(jnp.where(tri_incl, g[C:], 0.0), jnp.concatenate([st(uu), vs], axis=0))
         for lh, g, uu, vs in zip(LH, Gm, u, v_st)]
    inv_n = 1.0 / HEAD_A
    dev = each(lambda x: x - hsum(x) * inv_n, o)
    gn = each(lambda d, w, b_: d * lax.rsqrt(hsum(d * d) * inv_n + GN_EPS) * w + b_, dev, ln_w, ln_b)
    bonus = each(lambda rr, kx, w, vv: hsum(rr * kx * w) * vv, r, k, r_k, v)
    for sl, y, bo in zip(sls, gn, bonus):
        zg = zg_ref[0, :, sl]
        o_ref[0, :, sl] = ((y + bo) * (zg * _sigmoid(zg))).astype(o_ref.dtype)
    ds = [_bdot_tn(jnp.concatenate([uu, vv], axis=0), jnp.concatenate([b_, k_], axis=0))
          for uu, vv, b_, k_ in zip(u, v, bt, kt)]
    for gp in range(GP):
        s_scr[gp] = (S[gp] + jnp.where(head_mask, ds[gp], 0.0)) * p_incl[gp][C - 1:C, :]

    @pl.when(ci == pl.num_programs(2) - 1)
    def _():
        for gp in range(GP):
            s_all = s_scr[gp]
            for hh in range(NH):
                blk = slice(hh * HEAD_A, (hh + 1) * HEAD_A)
                sfin_ref[0, gp * NH + hh] = s_all[blk, blk]


def rwkv_scan(rkvg, lw, a, pvec, s0):
    _, B, T, CA = rkvg.shape
    H = CA // HEAD_A
    NH, C, GP = SCAN_NH, SCAN_C, SCAN_GP
    assert NH * C == LANES and H % (NH * GP) == 0
    L = NH * HEAD_A
    NG = H // NH
    Tp = -(-T // C) * C
    if Tp != T:
        rkvg = jnp.pad(rkvg, ((0, 0), (0, 0), (0, Tp - T), (0, 0)))
        lw, a = (jnp.pad(z, ((0, 0), (0, Tp - T), (0, 0))) for z in (lw, a))
    seq_spec = pl.BlockSpec((1, C, GP * L), lambda bi, gi, ci: (bi, ci, gi))
    proj_spec = lambda j: pl.BlockSpec((None, 1, C, GP * L), lambda bi, gi, ci: (j, bi, ci, gi))
    st_spec = pl.BlockSpec((1, GP * NH, HEAD_A, HEAD_A), lambda bi, gi, ci: (bi, gi, 0, 0))
    o, s_fin = pl.pallas_call(
        functools.partial(_scan_kernel, C=C, NH=NH, GP=GP),
        grid=(B, NG // GP, Tp // C),
        in_specs=[proj_spec(j) for j in range(4)] + [seq_spec, seq_spec,
                  pl.BlockSpec((5, GP * L), lambda bi, gi, ci: (0, gi)), st_spec],
        out_specs=[seq_spec, st_spec],
        out_shape=[jax.ShapeDtypeStruct((B, Tp, CA), BF16), jax.ShapeDtypeStruct((B, H, HEAD_A, HEAD_A), F32)],
        scratch_shapes=[pltpu.VMEM((GP, L, L), F32)],
        compiler_params=_cparams(("parallel", "parallel", "arbitrary")),
        name="rwkv_scan",
    )(rkvg, rkvg, rkvg, rkvg, lw, a, pvec, s0)
    return o[:, :T], s_fin


def _gelu_tanh(x):
    c = math.sqrt(2.0 / math.pi)
    return 0.5 * x * (1.0 + jnp.tanh(c * (x + 0.044715 * (x * x * x))))


def _compress_kernel(pt_ref, *refs, PGS):
    del pt_ref
    page_refs = refs[:PGS]
    next_ref, pe_ref, w1_ref, w2_ref, out_ref = refs[PGS:]
    CPP = page_refs[0].shape[1]
    NCH = PGS * CPP
    CG = 2 * G_KV
    M = (NCH + 1) * CG

    def rows_of(l, hf):
        pe = pe_ref[hf, l]
        parts = [(page_refs[i][0, :, l] + pe[None]).reshape(CPP * CG, HEAD_B) for i in range(PGS)]
        parts.append(next_ref[0, 0, l] + pe)
        return jnp.concatenate(parts, axis=0)

    top = jnp.zeros((M, 2 * HEAD_B), F32)
    bot = jnp.zeros((M, 2 * HEAD_B), F32)
    for l in range(0, S_CMP, 2):
        wrows = pl.ds(l * HEAD_B, 2 * HEAD_B)
        xt = jnp.concatenate([rows_of(l, 0), rows_of(l + 1, 0)], axis=1).astype(BF16)
        top = top + jnp.dot(xt, w1_ref[0, wrows, :], preferred_element_type=F32)
        xb = jnp.concatenate([rows_of(l, 1), rows_of(l + 1, 1)], axis=1).astype(BF16)
        bot = bot + jnp.dot(xb, w1_ref[1, wrows, :], preferred_element_type=F32)
    is_k = (lax.broadcasted_iota(jnp.int32, (M, 1), 0) % CG) < G_KV
    pick = lambda z, n: jnp.where(is_k[:n], z[:n, :HEAD_B], z[:n, HEAD_B:])
    hcur = pick(top, NCH * CG) + pick(bot, M)[CG:]
    o2 = jnp.dot(_gelu_tanh(hcur).astype(BF16), w2_ref[...], preferred_element_type=F32)
    out_ref[0] = pick(o2, NCH * CG).reshape(NCH, CG, HEAD_B)


def compress_kv(pool5, table, pe_cmp, w1, w2):
    NP, CPP = pool5.shape[:2]
    B, n_pages = table.shape
    PGS = max(d for d in (16, 8, 4, 2, 1) if n_pages % d == 0)
    NCH = PGS * CPP
    CG = 2 * G_KV
    half = S_CMP * HEAD_B
    pe_r = jnp.repeat(pe_cmp.reshape(2, 2, S_CMP, HEAD_B).transpose(1, 2, 0, 3), G_KV, axis=2)
    w1_r = w1.reshape(2, 2, half, HEAD_B).transpose(1, 2, 0, 3).reshape(2, half, 2 * HEAD_B)
    w2_r = jnp.concatenate([w2[0], w2[1]], axis=1)

    def page_map(i):
        return lambda b, s, pt: (pt[b, s * PGS + i], 0, 0, 0, 0)

    def next_map(b, s, pt):
        return (pt[b, jnp.minimum((s + 1) * PGS, n_pages - 1)], 0, 0, 0, 0)

    const = lambda n: (lambda b, s, pt: (0,) * n)
    grid_spec = pltpu.PrefetchScalarGridSpec(
        num_scalar_prefetch=1,
        grid=(B, n_pages // PGS),
        in_specs=[pl.BlockSpec((1, CPP, S_CMP, CG, HEAD_B), page_map(i)) for i in range(PGS)] + [
            pl.BlockSpec((1, 1, S_CMP, CG, HEAD_B), next_map),
            pl.BlockSpec((2, S_CMP, CG, HEAD_B), const(4)),
            pl.BlockSpec((2, half, 2 * HEAD_B), const(3)),
            pl.BlockSpec((HEAD_B, 2 * HEAD_B), const(2)),
        ],
        out_specs=pl.BlockSpec((1, NCH, CG, HEAD_B), lambda b, s, pt: (b, s, 0, 0)),
    )
    return pl.pallas_call(
        functools.partial(_compress_kernel, PGS=PGS),
        grid_spec=grid_spec,
        out_shape=jax.ShapeDtypeStruct((B, n_pages * CPP, CG, HEAD_B), F32),
        compiler_params=_cparams(("parallel", "arbitrary")),
        name="compress_kv",
    )(table, *([pool5] * PGS), pool5, pe_r, w1_r, w2_r)


def _stack_heads(q, HG):
    return jnp.concatenate([q[:, h * HEAD_B:(h + 1) * HEAD_B] for h in range(HG)], axis=0)


def _masked_softmax_rows(s, mask):
    s = jnp.where(mask, s, NEG)
    m = jnp.max(s, axis=-1, keepdims=True)
    e = jnp.where(mask, jnp.exp(s - m), 0.0)
    l = jnp.sum(e, axis=-1, keepdims=True)
    return e / jnp.where(l > 0.0, l, 1.0)


def _attend_stacked(s, dist, mask, v, slopes_ref, g, HG, tq):
    ps = []
    psum = jnp.zeros(dist.shape, F32)
    for h in range(HG):
        p = _masked_softmax_rows(s[h * tq:(h + 1) * tq] - slopes_ref[g * HG + h] * dist, mask)
        psum = psum + p
        ps.append(p.astype(BF16))
    o = jnp.dot(jnp.concatenate(ps, axis=0), v.astype(BF16), preferred_element_type=F32)
    return o, psum


def _unstack_store(o_ref, o, HG, tq):
    for h in range(HG):
        o_ref[0, :, h * HEAD_B:(h + 1) * HEAD_B] = o[h * tq:(h + 1) * tq]


def _nsa_cmp_kernel(slopes_ref, q_ref, kc_ref, vc_ref, o_ref, selm_ref, *, tq, HG, nc, nsb, pos0):
    g = pl.program_id(1)
    qt = pl.program_id(2)
    NCp = kc_ref.shape[1]
    NSBp = selm_ref.shape[3]
    q_st = _stack_heads(q_ref[0] * (HEAD_B ** -0.5), HG)
    s = _bdot_nt(q_st, kc_ref[0])
    qpos = pos0 + qt * tq + lax.broadcasted_iota(jnp.int32, (tq, 1), 0)
    cidx = lax.broadcasted_iota(jnp.int32, (1, NCp), 1)
    cend = S_CMP * cidx + (L_CMP - 1)
    mask = (cend <= qpos) & (cidx < nc)
    dist = (qpos - cend).astype(F32)
    o, imp_c = _attend_stacked(s, dist, mask, vc_ref[0], slopes_ref, g, HG, tq)
    _unstack_store(o_ref, o, HG, tq)

    crow = lax.broadcasted_iota(jnp.int32, (NCp, NSBp), 0)
    jcol = lax.broadcasted_iota(jnp.int32, (NCp, NSBp), 1)
    overlap = ((S_CMP * crow < L_SEL * (jcol + 1)) & (S_CMP * crow + L_CMP > L_SEL * jcol)
               & (crow < nc)).astype(F32)
    imp = jnp.dot(imp_c, overlap, precision=lax.Precision.HIGHEST, preferred_element_type=F32)
    lane = lax.broadcasted_iota(jnp.int32, (tq, NSBp), 1)
    cur = jnp.right_shift(qpos, int(math.log2(L_SEL)))
    forced = ((lane == 0) | (lane == cur) | (lane == cur - 1)).astype(F32)
    score = jnp.where(lane <= cur, imp + FORCE_BONUS * forced, NEG)
    score = jnp.where(lane < nsb, score, -3e38)

    if tq % LANES == 0 and NSBp == LANES:
        nr = -(-nsb // SUBLANES) * SUBLANES
        st = score.T[:nr]
        sub = lax.broadcasted_iota(jnp.int32, (nr, tq), 0)
        cnt = jnp.zeros((nr, tq), F32)
        for i in range(nsb):
            row = st[i:i + 1, :]
            beats = (row > st) | ((row == st) & (sub > i))
            cnt = cnt + jnp.where(beats, 1.0, 0.0)
        sel_t = jnp.where((cnt < TOPK_SEL) & (st > 0.5 * NEG), 1.0, 0.0)
        selm_ref[0, 0] = jnp.concatenate([sel_t, jnp.zeros((NSBp - nr, tq), F32)], axis=0).T
    else:
        cnt = jnp.zeros((tq, NSBp), F32)
        for i in range(nsb):
            col = score[:, i:i + 1]
            beats = (col > score) | ((col == score) & (lane > i))
            cnt = cnt + jnp.where(beats, 1.0, 0.0)
        sel = (cnt < TOPK_SEL) & (score > 0.5 * NEG)
        selm_ref[0, 0] = sel.astype(F32)


def nsa_cmp(proj, kvc, slopes, *, tq, nc, nsb, pos0):
    B, T, _ = proj.shape
    HG = slopes.shape[0] // G_KV
    NCp = kvc.shape[1]
    NSBp = -(-nsb // LANES) * LANES
    gw = HG * HEAD_B
    return pl.pallas_call(
        functools.partial(_nsa_cmp_kernel, tq=tq, HG=HG, nc=nc, nsb=nsb, pos0=pos0),
        grid=(B, G_KV, T // tq),
        in_specs=[pl.BlockSpec(memory_space=pltpu.SMEM),
                  pl.BlockSpec((1, tq, gw), lambda b, g, t: (b, t, g)),
                  pl.BlockSpec((1, NCp, HEAD_B), lambda b, g, t: (b, 0, g)),
                  pl.BlockSpec((1, NCp, HEAD_B), lambda b, g, t: (b, 0, G_KV + g))],
        out_specs=[pl.BlockSpec((1, tq, gw), lambda b, g, t: (b, t, g)),
                   pl.BlockSpec((1, 1, tq, NSBp), lambda b, g, t: (b, g, t, 0))],
        out_shape=[jax.ShapeDtypeStruct((B, T, G_KV * gw), F32),
                   jax.ShapeDtypeStruct((B, G_KV, T, NSBp), F32)],
        compiler_params=_cparams(("parallel", "parallel", "parallel")),
        name="nsa_cmp",
    )(slopes, proj, kvc.reshape(B, NCp, -1), kvc.reshape(B, NCp, -1))


LOG2E = 1.4426950408889634


def _bf16_part(x):
    return x.astype(BF16).astype(F32)


def _alibi_lhs(q, slope_col):
    c = slope_col * LOG2E
    c1 = _bf16_part(c)
    c2 = _bf16_part(c - c1)
    c3 = _bf16_part(c - c1 - c2)
    lane = lax.broadcasted_iota(jnp.int32, q.shape, 1)
    extra = jnp.where((lane == 0) | (lane == 3), c1,
                      jnp.where((lane == 1) | (lane == 4), c2, jnp.where((lane == 2) | (lane == 5), c3, 0.0)))
    return jnp.concatenate([q, extra], axis=1).astype(BF16)


def _alibi_rhs(k, k0):
    pos = k0 + lax.broadcasted_iota(jnp.int32, k.shape, 0)
    lane = lax.broadcasted_iota(jnp.int32, k.shape, 1)
    hi = jnp.bitwise_and(pos, -L_SEL)
    extra = jnp.where(lane < 3, hi, jnp.where(lane < 6, pos - hi, 0)).astype(F32)
    return jnp.concatenate([k, extra], axis=1).astype(BF16)


def _with_ones(v):
    lane = lax.broadcasted_iota(jnp.int32, v.shape, 1)
    return jnp.concatenate([v, jnp.where(lane == 0, 1.0, 0.0)], axis=1).astype(BF16)


def _silu(x):
    return x * _sigmoid(x)


def _nsa_selwin_prompt_kernel(slopes_ref, q_ref, selm_ref, ks_ref, vs_ref, kw_ref, vw_ref,
                              ocmp_ref, zc_ref, zs_ref, zw_ref, gate_ref, o_ref, s_scr, *, tq, HG, T, WS, SEG):
    g = pl.program_id(1)
    qt = pl.program_id(2)
    NSBp = selm_ref.shape[3]
    R = HG * tq
    qpos = qt * tq + lax.broadcasted_iota(jnp.int32, (tq, 1), 0)
    slope_col = jnp.concatenate([jnp.full((tq, 1), slopes_ref[g * HG + h], F32) for h in range(HG)], axis=0)
    tile_heads = lambda x: jnp.concatenate([x] * HG, axis=0)
    q2 = _alibi_lhs(_stack_heads(q_ref[0] * (HEAD_B ** -0.5 * LOG2E), HG), slope_col)

    selm_b = selm_ref[0, 0].astype(BF16)
    nseg = (qt * tq + tq + SEG - 1) // SEG

    def seg_scores(si, m):
        k0 = pl.multiple_of(si * SEG, SEG)
        kcol = k0 + lax.broadcasted_iota(jnp.int32, (NSBp, SEG), 1)
        expand = (jnp.right_shift(kcol, int(math.log2(L_SEL)))
                  == lax.broadcasted_iota(jnp.int32, (NSBp, SEG), 0)).astype(BF16)
        in_blk = jnp.dot(selm_b, expand, preferred_element_type=F32) > 0.5
        kpos = k0 + lax.broadcasted_iota(jnp.int32, (1, SEG), 1)
        mask = tile_heads(in_blk & (kpos <= qpos))
        s = lax.dot_general(q2, _alibi_rhs(ks_ref[0, pl.ds(k0, SEG), :], k0), NT_DIMS, preferred_element_type=F32)
        s = jnp.where(mask, s, NEG)
        s_scr[si] = s
        return jnp.maximum(m, jnp.max(s, axis=-1, keepdims=True))

    m = lax.fori_loop(0, nseg, seg_scores, jnp.full((R, 1), NEG, F32))

    def seg_pv(si, acc):
        k0 = pl.multiple_of(si * SEG, SEG)
        e = jnp.exp2(s_scr[si] - m).astype(BF16)
        return acc + jnp.dot(e, _with_ones(vs_ref[0, pl.ds(k0, SEG), :]), preferred_element_type=F32)

    acc = lax.fori_loop(0, nseg, seg_pv, jnp.zeros((R, 2 * HEAD_B), F32))
    o = acc[:, :HEAD_B] / acc[:, HEAD_B:HEAD_B + 1]

    start = pl.multiple_of(jnp.clip(qt * tq - WINDOW, 0, T - WS), SUBLANES)
    distw = qpos - (start + lax.broadcasted_iota(jnp.int32, (1, WS), 1))
    maskw = tile_heads((distw >= 0) & (distw < WINDOW))
    sw = lax.dot_general(q2, _alibi_rhs(kw_ref[0, pl.ds(start, WS), :], start), NT_DIMS, preferred_element_type=F32)
    sw = jnp.where(maskw, sw, NEG)
    ew = jnp.exp2(sw - jnp.max(sw, axis=-1, keepdims=True)).astype(BF16)
    accw = jnp.dot(ew, _with_ones(vw_ref[0, pl.ds(start, WS), :]), preferred_element_type=F32)
    ow = accw[:, :HEAD_B] / accw[:, HEAD_B:HEAD_B + 1]

    HB = G_KV * HG
    gates = _sigmoid(gate_ref[0])
    lane = lax.broadcasted_iota(jnp.int32, gates.shape, 1)
    gate_col = lambda idx: jnp.sum(jnp.where(lane == idx, gates, 0.0), axis=1, keepdims=True)
    for h in range(HG):
        hs = slice(h * HEAD_B, (h + 1) * HEAD_B)
        rs = slice(h * tq, (h + 1) * tq)
        hd = g * HG + h
        y = (gate_col(hd) * ocmp_ref[0, :, hs] * _silu(zc_ref[0, :, hs])
             + gate_col(HB + hd) * o[rs] * _silu(zs_ref[0, :, hs])
             + gate_col(2 * HB + hd) * ow[rs] * _silu(zw_ref[0, :, hs]))
        o_ref[0, :, hs] = y.astype(o_ref.dtype)


def nsa_selwin_prompt(proj, rows, selm, o_cmp, slopes, *, tq):
    B, T, _ = proj.shape
    HG = slopes.shape[0] // G_KV
    NSBp = selm.shape[3]
    gw = HG * HEAD_B
    CB = G_KV * gw
    WS = min(T, WINDOW + tq)
    SEG = min(T, 512)
    assert T % SEG == 0
    kv_spec = lambda c: pl.BlockSpec((1, T, HEAD_B), lambda b, g, t: (b, 0, c * G_KV + g))
    head_spec = lambda blk: pl.BlockSpec((1, tq, gw), lambda b, g, t: (b, t, blk * G_KV + g))
    return pl.pallas_call(
        functools.partial(_nsa_selwin_prompt_kernel, tq=tq, HG=HG, T=T, WS=WS, SEG=SEG),
        grid=(B, G_KV, T // tq),
        in_specs=[pl.BlockSpec(memory_space=pltpu.SMEM),
                  head_spec(0),
                  pl.BlockSpec((1, 1, tq, NSBp), lambda b, g, t: (b, g, t, 0)),
                  kv_spec(2), kv_spec(3), kv_spec(4), kv_spec(5),
                  head_spec(0), head_spec(1), head_spec(2), head_spec(3),
                  pl.BlockSpec((1, tq, LANES), lambda b, g, t: (b, t, 4 * CB // LANES))],
        out_specs=head_spec(0),
        out_shape=jax.ShapeDtypeStruct((B, T, CB), BF16),
        scratch_shapes=[pltpu.VMEM((T // SEG, HG * tq, SEG), F32)],
        compiler_params=_cparams(("parallel", "parallel", "parallel")),
        name="nsa_selwin_prompt",
    )(slopes, proj, selm, rows, rows, rows, rows, o_cmp, proj, proj, proj, proj)


def _nsa_selwin_sample_kernel(pt_ref, slopes_ref, q_ref, selm_ref, selst_ref, *refs, PGS, PS, HG, TQ, pos0, n_new, n_win):
    del pt_ref
    page_refs = refs[:PGS]
    (new_ref, cwin_ref, ocmp_ref, zc_ref, zs_ref, zw_ref, gate_ref, o_ref,
     m_scr, l_scr, acc_scr) = refs[PGS:]
    st = pl.program_id(1)
    NSBp = selm_ref.shape[3]
    GW = G_KV * HEAD_B
    sel_shift = int(math.log2(L_SEL))

    @pl.when(st == 0)
    def _():
        m_scr[...] = jnp.full(m_scr.shape, NEG, F32)
        l_scr[...] = jnp.zeros(l_scr.shape, F32)
        acc_scr[...] = jnp.zeros(acc_scr.shape, F32)

    qpos = pos0 + lax.broadcasted_iota(jnp.int32, (TQ, 1), 0)
    lane_j = lax.broadcasted_iota(jnp.int32, (TQ, NSBp), 1)
    tile_heads = lambda x: jnp.concatenate([x] * HG, axis=0)

    def sel_col(selm_g, j):
        return jnp.sum(jnp.where(lane_j == j, selm_g, 0.0), axis=1, keepdims=True)

    def online_update(s, mask, v):
        gs = range(G_KV)
        m_old = [m_scr[g] for g in gs]
        m_new = [jnp.maximum(m_old[g], jnp.max(jnp.where(mask[g], s[g], NEG), axis=-1, keepdims=True)) for g in gs]
        e = [jnp.where(mask[g], jnp.exp(s[g] - m_new[g]), 0.0) for g in gs]
        alpha = [jnp.exp(m_old[g] - m_new[g]) for g in gs]
        pv = [_bdot(e[g], v[g]) for g in gs]
        for g in gs:
            l_scr[g] = alpha[g] * l_scr[g] + jnp.sum(e[g], axis=-1, keepdims=True)
            acc_scr[g] = alpha[g] * acc_scr[g] + pv[g]
            m_scr[g] = m_new[g]

    NK = PGS * PS
    kpos = st * NK + lax.broadcasted_iota(jnp.int32, (1, NK), 1)
    expand = (jnp.right_shift(lax.broadcasted_iota(jnp.int32, (LANES, NK), 1), sel_shift)
              == lax.broadcasted_iota(jnp.int32, (LANES, NK), 0)).astype(BF16)
    in_blk_all = jnp.dot(selst_ref[0, 0].astype(BF16), expand, preferred_element_type=F32)
    dist = qpos - kpos
    distf = tile_heads(dist.astype(F32))
    gs = range(G_KV)
    q_st = [_stack_heads(q_ref[0, :, g * HG * HEAD_B:(g + 1) * HG * HEAD_B] * (HEAD_B ** -0.5), HG).astype(BF16)
            for g in gs]
    slope_col = [jnp.concatenate([jnp.full((TQ, 1), slopes_ref[g * HG + h], F32) for h in range(HG)], axis=0)
                 for g in gs]
    slabs = [pltpu.einshape("rcd->crd", page_refs[i][0].reshape(PS, 2 * G_KV, HEAD_B)) for i in range(PGS)]
    k = [jnp.concatenate([slabs[i][g] for i in range(PGS)], axis=0) for g in gs]
    v = [jnp.concatenate([slabs[i][G_KV + g] for i in range(PGS)], axis=0) for g in gs]
    mask = [tile_heads((in_blk_all[g * TQ:(g + 1) * TQ] > 0.5) & (dist >= 0)) for g in gs]
    s = [_bdot_nt(q_st[g], k[g]) - slope_col[g] * distf for g in gs]
    online_update(s, mask, v)

    @pl.when(st == pl.num_programs(1) - 1)
    def _():
        NN = new_ref.shape[1]
        rnew = lax.broadcasted_iota(jnp.int32, (1, NN), 1)
        kpos_n = pos0 + rnew
        dist_n = qpos - kpos_n
        ok_n = (rnew < n_new) & (dist_n >= 0)
        jn = pos0 >> sel_shift
        kpos_w = pos0 - n_win + lax.broadcasted_iota(jnp.int32, (1, n_win), 1)
        dist_w = qpos - kpos_w
        dist_wall = jnp.concatenate([dist_w, dist_n], axis=1)
        mask_wall = jnp.concatenate([(dist_w >= 0) & (dist_w < WINDOW), ok_n & (dist_n < WINDOW)], axis=1)
        kn = [new_ref[0, :, 2 * GW + g * HEAD_B:2 * GW + (g + 1) * HEAD_B] for g in gs]
        vn = [new_ref[0, :, 3 * GW + g * HEAD_B:3 * GW + (g + 1) * HEAD_B] for g in gs]
        mask_n = [tile_heads((sel_col(selm_ref[0, g], jn) > 0.5) & ok_n) for g in gs]
        dist_nf = tile_heads(dist_n.astype(F32))
        s_n = [_bdot_nt(q_st[g], kn[g]) - slope_col[g] * dist_nf for g in gs]
        online_update(s_n, mask_n, vn)
        kw = [jnp.concatenate([cwin_ref[0, :, g * HEAD_B:(g + 1) * HEAD_B],
                               new_ref[0, :, 4 * GW + g * HEAD_B:4 * GW + (g + 1) * HEAD_B]], axis=0) for g in gs]
        vw = [jnp.concatenate([cwin_ref[0, :, GW + g * HEAD_B:GW + (g + 1) * HEAD_B],
                               new_ref[0, :, 5 * GW + g * HEAD_B:5 * GW + (g + 1) * HEAD_B]], axis=0) for g in gs]
        dist_wf = tile_heads(dist_wall.astype(F32))
        mask_w = tile_heads(mask_wall)
        sw = [_bdot_nt(q_st[g], kw[g]) - slope_col[g] * dist_wf for g in gs]
        pw = [_masked_softmax_rows(sw[g], mask_w) for g in gs]
        ow = [_bdot(pw[g], vw[g]) for g in gs]
        HB = G_KV * HG
        gates = _sigmoid(gate_ref[0])
        for g in gs:
            l = l_scr[g]
            o = acc_scr[g] / jnp.where(l > 0.0, l, 1.0)
            for h in range(HG):
                hd = g * HG + h
                hs = slice(hd * HEAD_B, (hd + 1) * HEAD_B)
                rs = slice(h * TQ, (h + 1) * TQ)
                y = (gates[:, hd:hd + 1] * ocmp_ref[0, :, hs] * _silu(zc_ref[0, :, hs])
                     + gates[:, HB + hd:HB + hd + 1] * o[rs] * _silu(zs_ref[0, :, hs])
                     + gates[:, 2 * HB + hd:2 * HB + hd + 1] * ow[g][rs] * _silu(zw_ref[0, :, hs]))
                o_ref[0, :, hs] = y.astype(o_ref.dtype)


def nsa_selwin_sample(proj, selm, o_cmp, pool5, table, new_rows, cwin, slopes, *, pos0, n_new):
    B, TQ, _ = proj.shape
    HG = slopes.shape[0] // G_KV
    NP, CPP = pool5.shape[:2]
    PS = CPP * S_CMP
    n_pages = table.shape[1]
    NSBp = selm.shape[3]
    PGS = max(d for d in (8, 4, 2, 1) if n_pages % d == 0)
    CB = G_KV * HG * HEAD_B
    GW = G_KV * HEAD_B
    NN = new_rows.shape[1]
    n_win = cwin.shape[1]
    assert pos0 % L_SEL == 0 and n_new <= L_SEL and pos0 == n_pages * PS

    def page_map(i):
        return lambda b, s, pt: (pt[b, s * PGS + i], 0, 0, 1, 0)

    n_steps = n_pages // PGS
    bps = PGS * PS // L_SEL
    assert bps <= LANES
    selst = selm[:, :, :, :n_steps * bps].reshape(B, G_KV, TQ, n_steps, bps).transpose(0, 3, 1, 2, 4)
    selst = jnp.pad(selst.reshape(B, n_steps, G_KV * TQ, bps), ((0, 0), (0, 0), (0, 0), (0, LANES - bps)))

    const = lambda b, s, pt: (b, 0, 0)
    wide = lambda blk: pl.BlockSpec((1, TQ, CB), lambda b, s, pt: (b, 0, blk))
    grid_spec = pltpu.PrefetchScalarGridSpec(
        num_scalar_prefetch=1,
        grid=(B, n_steps),
        in_specs=[pl.BlockSpec(memory_space=pltpu.SMEM),
                  pl.BlockSpec((1, TQ, CB), const),
                  pl.BlockSpec((1, G_KV, TQ, NSBp), lambda b, s, pt: (b, 0, 0, 0)),
                  pl.BlockSpec((1, 1, G_KV * TQ, LANES), lambda b, s, pt: (b, s, 0, 0))]
                 + [pl.BlockSpec((1, CPP, S_CMP, 2 * G_KV, HEAD_B), page_map(i)) for i in range(PGS)]
                 + [pl.BlockSpec((1, NN, 6 * GW), const),
                    pl.BlockSpec((1, n_win, 2 * GW), const),
                    wide(0), wide(1), wide(2), wide(3),
                    pl.BlockSpec((1, TQ, LANES), lambda b, s, pt: (b, 0, 4 * CB // LANES))],
        out_specs=wide(0),
        scratch_shapes=[pltpu.VMEM((G_KV, HG * TQ, 1), F32),
                        pltpu.VMEM((G_KV, HG * TQ, 1), F32),
                        pltpu.VMEM((G_KV, HG * TQ, HEAD_B), F32)],
    )
    return pl.pallas_call(
        functools.partial(_nsa_selwin_sample_kernel, PGS=PGS, PS=PS, HG=HG, TQ=TQ, pos0=pos0,
                          n_new=n_new, n_win=n_win),
        grid_spec=grid_spec,
        out_shape=jax.ShapeDtypeStruct((B, TQ, CB), F32),
        compiler_params=_cparams(("parallel", "arbitrary")),
        name="nsa_selwin_sample",
    )(table, slopes, proj, selm, selst, *([pool5] * PGS), new_rows, cwin, o_cmp, proj, proj, proj, proj)


def _rwkv_layer(h, x_prev, s0, i, W, B, T):
    N, D = h.shape
    g = W["norm_g"][i]
    rkvg = rwkv_in(h, x_prev, g, W["mu_a"][i], W["w_in_a"], i, T)
    CA = rkvg.shape[-1]
    lw, a = rwkv_lora(h, x_prev, g, W["mu_a"][i], W["w_lora_w1"][i], W["w_lora_w2"][i], W["a_lora1"][i],
                      W["a_lora2"][i], W["w0_a"][i], W["a0_a"][i], T)
    pvec = jnp.stack([W["k_k"][i], W["k_a"][i], W["r_k"][i].reshape(CA), W["ln_x_w"][i], W["ln_x_b"][i]])
    o, s_fin = rwkv_scan(rkvg.reshape(4, B, T, CA), lw.reshape(B, T, CA), a.reshape(B, T, CA), pvec, s0)
    last = rmsnorm(h.reshape(B, T, D)[:, -1], g)
    return o.reshape(N, CA), s_fin, last


def _nsa_layer(h, jb, shared, W, slopes, B, T, norm_g):
    N, D = h.shape
    CB = W["w_out_b"].shape[1]
    proj3 = norm_mm(h, norm_g, W["w_in_b"], (jb,)).reshape(B, T, -1)
    if shared["past"] is None:
        o_cmp, selm = nsa_cmp(proj3, shared["kvc"], slopes, tq=min(T, 256), nc=shared["nc"],
                              nsb=shared["nsb"], pos0=0)
        o = nsa_selwin_prompt(proj3, shared["rows"], selm, o_cmp, slopes, tq=min(T, 256))
    else:
        TQ = SUBLANES
        projp = jnp.pad(proj3, ((0, 0), (0, TQ - T), (0, 0)))
        o_cmp, selm = nsa_cmp(projp, shared["kvc"], slopes, tq=TQ, nc=shared["nc"],
                              nsb=shared["nsb"], pos0=shared["pos0"])
        pool, table, cwin = shared["past"]
        o = nsa_selwin_sample(projp, selm, o_cmp, pool, table, shared["new_rows"], cwin, slopes,
                              pos0=shared["pos0"], n_new=T)[:, :T]
    return o.reshape(N, CB)


def _trunk(x, p, pos0, wkv0, shift0, past, W, slopes):
    B, T, D = x.shape
    N = B * T
    depth = p.shape[0]
    n_a = W["w_in_a"].shape[0]
    GW = G_KV * HEAD_B
    h = x.reshape(N, D)
    wkv_new, shift_new = [], []
    shared, kv_rows, win_state = None, None, None
    for i in range(depth):
        if i < n_a:
            o, s_fin, last = _rwkv_layer(h, shift0[i], wkv0[i], i, W, B, T)
            wkv_new.append(s_fin)
            shift_new.append(last)
            h = out_ple(o, h, p[i].reshape(N, -1), W["w_out_a"], i, W["w_ple"], W["w_ple_gate"], i)
        else:
            o = _nsa_layer(h, i - n_a, shared, W, slopes, B, T, W["norm_g"][i])
            h = out_ple(o, h, p[i].reshape(N, -1), W["w_out_b"], i - n_a, W["w_ple"], W["w_ple_gate"], i)
        if i == n_a - 1:
            rows = norm_mm(h, W["kv_norm_g"], W["w_kv"]).reshape(B, T, 6 * GW)
            kv_rows = rows[:, :, :4 * GW].reshape(B, T, 4, G_KV, HEAD_B)
            win_new = rows[:, :, 4 * GW:].reshape(B, T, 2, G_KV, HEAD_B)
            if past is None:
                PS = 128
                pool = rows.reshape(B * T // PS, PS // S_CMP, S_CMP, 6 * G_KV, HEAD_B)
                table = jnp.arange(B * T // PS, dtype=jnp.int32).reshape(B, T // PS)
                t_all = T
                win_all = win_new
                shared = {"past": None, "rows": rows}
            else:
                pool, table, cwin = past
                PS = pool.shape[1] * S_CMP
                t_all = pos0 + T
                win_all = jnp.concatenate([cwin.reshape(B, -1, 2, G_KV, HEAD_B), win_new], axis=1)
                NN = LANES
                shared = {"past": past, "new_rows": jnp.pad(rows, ((0, 0), (0, NN - T), (0, 0)))}
            win_state = win_all[:, win_all.shape[1] - min(WINDOW, pos0 + T):]
            nc = (t_all - L_CMP) // S_CMP + 1
            assert nc < table.shape[1] * PS // S_CMP
            kvc = compress_kv(pool, table, W["pe_cmp"], W["w_cmp1"], W["w_cmp2"])
            shared.update(kvc=kvc, nc=nc, nsb=max(-(-t_all // L_SEL), TOPK_SEL), pos0=pos0)
    y = rmsnorm(h, W["final_norm_g"]).reshape(B, T, D)
    return y, jnp.stack(wkv_new), jnp.stack(shift_new), kv_rows, win_state


def kernel(x_prompt, x_sample, state_wkv, state_shift, cache_kv, cache_win_kv, page_table, p_prompt, p_sample, norm_g, mu_a, w_in_a, w_lora_w1, w_lora_w2, w0_a, a_lora1, a_lora2, a0_a, k_k, k_a, r_k, ln_x_w, ln_x_b, w_out_a, w_in_b, w_out_b, kv_norm_g, w_kv, pe_cmp, w_cmp1, w_cmp2, w_ple, w_ple_gate, final_norm_g):
    bf = lambda w: w.astype(BF16)
    CA = w_out_a.shape[1]
    W = dict(norm_g=norm_g, mu_a=mu_a, w_in_a=bf(w_in_a), w_lora_w1=bf(w_lora_w1), w_lora_w2=bf(w_lora_w2),
             w0_a=w0_a, a_lora1=bf(a_lora1), a_lora2=bf(a_lora2), a0_a=a0_a, k_k=k_k, k_a=k_a,
             r_k=r_k, ln_x_w=ln_x_w, ln_x_b=ln_x_b,
             w_out_a=bf(w_out_a), w_in_b=bf(w_in_b), w_out_b=bf(w_out_b), kv_norm_g=kv_norm_g, w_kv=bf(w_kv),
             pe_cmp=pe_cmp, w_cmp1=bf(w_cmp1), w_cmp2=bf(w_cmp2), w_ple=bf(w_ple), w_ple_gate=bf(w_ple_gate),
             final_norm_g=final_norm_g)
    HB = w_out_b.shape[1] // HEAD_B
    slopes = 2.0 ** (-8.0 * jnp.arange(1, HB + 1, dtype=F32) / HB)
    bp = x_prompt.shape[0]
    n_a = w_in_a.shape[0]
    D = x_prompt.shape[-1]
    wkv0 = jnp.zeros((n_a, bp, CA // HEAD_A, HEAD_A, HEAD_A), F32)
    shift0 = jnp.zeros((n_a, bp, D), F32)
    y_p, wkv_p, shift_p, kv_p, win_p = _trunk(x_prompt, p_prompt, 0, wkv0, shift0, None, W, slopes)
    db, n_pages = page_table.shape
    NP, PS = cache_kv.shape[:2]
    pool5 = cache_kv.reshape(NP, PS // S_CMP, S_CMP, -1, HEAD_B)
    past = (pool5, page_table, cache_win_kv.reshape(db, cache_win_kv.shape[1], -1))
    y_s, wkv_s, shift_s, kv_s, win_s = _trunk(x_sample, p_sample, n_pages * PS, state_wkv, state_shift, past, W, slopes)
    return (y_p, y_s, wkv_p, shift_p, kv_p, win_p, wkv_s, shift_s, kv_s, win_s)
```

```python
import functools
import math

import jax
import jax.numpy as jnp
from jax import lax
from jax.experimental import pallas as pl
from jax.experimental.pallas import tpu as pltpu

F32 = jnp.float32
BF16 = jnp.bfloat16

HEAD_A = 64
GN_EPS = 64e-5
HEAD_B = 128
G_KV = 4
L_CMP = 32
S_CMP = 16
L_SEL = 64
TOPK_SEL = 16
WINDOW = 512
RMS_EPS = 1e-6
NEG = -1e30
FORCE_BONUS = 1e4

LANES = 128
SUBLANES = 8
VMEM_LIMIT = 56 * 1024 * 1024

SCAN_NH = 2
SCAN_C = 64
SCAN_GP = 16

NT_DIMS = (((1,), (1,)), ((), ()))
TN_DIMS = (((0,), (0,)), ((), ()))


def _cparams(sem):
    return pltpu.CompilerParams(dimension_semantics=sem, vmem_limit_bytes=VMEM_LIMIT)


def _bdot(a, b):
    return jnp.dot(a.astype(BF16), b.astype(BF16), preferred_element_type=F32)


def _bdot_nt(a, b):
    return lax.dot_general(a.astype(BF16), b.astype(BF16), NT_DIMS, preferred_element_type=F32)


def _bdot_tn(a, b):
    return lax.dot_general(a.astype(BF16), b.astype(BF16), TN_DIMS, preferred_element_type=F32)


def _rms_kernel(x_ref, g_ref, o_ref):
    x = x_ref[...]
    ms = jnp.mean(x * x, axis=-1, keepdims=True)
    o_ref[...] = x * lax.rsqrt(ms + RMS_EPS) * g_ref[...]


def rmsnorm(x, g):
    M, D = x.shape
    tm = min(M, 256)
    return pl.pallas_call(
        _rms_kernel,
        grid=(pl.cdiv(M, tm),),
        in_specs=[pl.BlockSpec((tm, D), lambda i: (i, 0)),
                  pl.BlockSpec((1, D), lambda i: (0, 0))],
        out_specs=pl.BlockSpec((tm, D), lambda i: (i, 0)),
        out_shape=jax.ShapeDtypeStruct((M, D), F32),
        compiler_params=_cparams(("parallel",)),
        name="rmsnorm",
    )(x, g.reshape(1, D))


def _mm_kernel(x_ref, w_ref, o_ref):
    o_ref[...] = jnp.dot(x_ref[...].astype(BF16), w_ref[...], preferred_element_type=F32)


def mm(x, w, widx=()):
    M, K = x.shape
    N = w.shape[-1]
    assert w.shape[-2] == K and len(widx) == w.ndim - 2
    tm = min(M, 1024 if K <= 2048 else 512)
    tn = N if N <= 512 else 512
    nlead = len(widx)
    w_spec = pl.BlockSpec((None,) * nlead + (K, tn), lambda i, j: tuple(widx) + (0, j))
    return pl.pallas_call(
        _mm_kernel,
        grid=(pl.cdiv(M, tm), pl.cdiv(N, tn)),
        in_specs=[pl.BlockSpec((tm, K), lambda i, j: (i, 0)), w_spec],
        out_specs=pl.BlockSpec((tm, tn), lambda i, j: (i, j)),
        out_shape=jax.ShapeDtypeStruct((M, N), F32),
        compiler_params=_cparams(("parallel", "parallel")),
        name="mm",
    )(x, w)


def _col_tile(n_rows, n_cols, tn):
    return tn if n_rows > 64 else min(n_cols, 4096)


def _norm_rows(x, g):
    return x * lax.rsqrt(jnp.mean(x * x, axis=-1, keepdims=True) + RMS_EPS) * g


def _sigmoid(x):
    return 0.5 + 0.5 * jnp.tanh(0.5 * x)


def _norm_and_shift(h_ref, hprev_ref, xprev_ref, g_ref, i, tm, T):
    g = g_ref[...]
    hn = _norm_rows(h_ref[...], g)
    prev_row = _norm_rows(hprev_ref[SUBLANES - 1:SUBLANES, :], g)
    row = lax.broadcasted_iota(jnp.int32, (tm, 1), 0)
    xs = jnp.where(row == 0, prev_row, pltpu.roll(hn, 1, axis=0))
    if T >= tm:
        assert T % tm == 0
        start = (i * tm) % T == 0
        xs = jnp.where((row == 0) & start, xprev_ref[pl.ds((i * tm) // T, 1), :], xs)
    else:
        assert tm % T == 0
        for bb in range(tm // T):
            xs = jnp.where(row == bb * T, xprev_ref[pl.ds(i * (tm // T) + bb, 1), :], xs)
    return hn, xs


def _rwkv_in_kernel(h_ref, hprev_ref, xprev_ref, g_ref, mu_ref, w_ref, o_ref, hn_scr, xs_scr, *, tm, T):
    i, j, n = pl.program_id(0), pl.program_id(1), pl.program_id(2)
    rc = min(tm, 256)
    chunks = [slice(c * rc, (c + 1) * rc) for c in range(tm // rc)]
    P = SUBLANES

    @pl.when((j == 0) & (n == 0))
    def _():
        g = g_ref[...]
        prev = _norm_rows(hprev_ref[P - 1:P, :], g)
        if T >= tm:
            assert T % tm == 0
            prev = jnp.where((i * tm) % T == 0, xprev_ref[pl.ds((i * tm) // T, 1), :], prev)
        hn_scr[0:P, :] = jnp.broadcast_to(prev, (P, prev.shape[1]))
        for ch in chunks:
            hn_scr[P + ch.start:P + ch.stop, :] = _norm_rows(h_ref[ch, :], g)

    @pl.when(n == 0)
    def _():
        mu = mu_ref[pl.ds(j, 1), :]
        for ch in chunks:
            hn = hn_scr[P + ch.start:P + ch.stop, :]
            xs = hn_scr[P - 1 + ch.start:P - 1 + ch.stop, :]
            if T < tm:
                assert tm % T == 0 and len(chunks) == 1
                row = lax.broadcasted_iota(jnp.int32, (tm, 1), 0)
                for bb in range(tm // T):
                    xs = jnp.where(row == bb * T, xprev_ref[pl.ds(i * (tm // T) + bb, 1), :], xs)
            xs_scr[ch, :] = (hn + (xs - hn) * mu).astype(BF16)

    o_ref[...] = jnp.dot(xs_scr[...], w_ref[...], preferred_element_type=F32)


def _shift_specs(tm, D, nb, ngrid):
    z = (0,) * (ngrid - 1)
    wrap = lambda f: (lambda i, *_: f(i))
    return [pl.BlockSpec((tm, D), wrap(lambda i: (i, 0))),
            pl.BlockSpec((SUBLANES, D), wrap(lambda i: (jnp.maximum(i * (tm // SUBLANES) - 1, 0), 0))),
            pl.BlockSpec((nb, D), wrap(lambda i: (0, 0))),
            pl.BlockSpec((1, D), wrap(lambda i: (0, 0))),
            pl.BlockSpec((6, D), wrap(lambda i: (0, 0)))]


def rwkv_in(h, xprev, g, mu, w, layer, T):
    N, D = h.shape
    C = w.shape[-1]
    tm = min(N, 1024)
    tn = _col_tile(N, C, 1024)
    return pl.pallas_call(
        functools.partial(_rwkv_in_kernel, tm=tm, T=T),
        grid=(N // tm, 4, C // tn),
        in_specs=_shift_specs(tm, D, xprev.shape[0], 3) + [
            pl.BlockSpec((None, None, D, tn), lambda i, j, n: (layer, j, 0, n))],
        out_specs=pl.BlockSpec((None, tm, tn), lambda i, j, n: (j, i, n)),
        out_shape=jax.ShapeDtypeStruct((4, N, C), F32),
        scratch_shapes=[pltpu.VMEM((tm + SUBLANES, D), F32), pltpu.VMEM((tm, D), BF16)],
        compiler_params=_cparams(("parallel", "arbitrary", "arbitrary")),
        name="rwkv_in",
    )(h, h, xprev, g.reshape(1, D), mu, w)


def _rwkv_lora_kernel(h_ref, hprev_ref, xprev_ref, g_ref, mu_ref, lw1_ref, lw2_ref, la1_ref, la2_ref,
                      w0_ref, a0_ref, lw_ref, a_ref, *, tm, T):
    hn, xs = _norm_and_shift(h_ref, hprev_ref, xprev_ref, g_ref, pl.program_id(0), tm, T)
    dx = xs - hn
    x4 = (hn + dx * mu_ref[4:5, :]).astype(BF16)
    x5 = (hn + dx * mu_ref[5:6, :]).astype(BF16)
    t4 = jnp.tanh(jnp.dot(x4, lw1_ref[...], preferred_element_type=F32)).astype(BF16)
    x = w0_ref[...] + jnp.dot(t4, lw2_ref[...], preferred_element_type=F32)
    lw_ref[...] = -math.exp(-0.5) * _sigmoid(x)
    t5 = jnp.dot(x5, la1_ref[...], preferred_element_type=F32).astype(BF16)
    a_ref[...] = _sigmoid(a0_ref[...] + jnp.dot(t5, la2_ref[...], preferred_element_type=F32))


def rwkv_lora(h, xprev, g, mu, lw1, lw2, la1, la2, w0, a0, T):
    N, D = h.shape
    R, C = lw2.shape
    tm = min(N, 256)
    full = lambda shape: pl.BlockSpec(shape, lambda i: (0,) * len(shape))
    o_spec = pl.BlockSpec((tm, C), lambda i: (i, 0))
    return pl.pallas_call(
        functools.partial(_rwkv_lora_kernel, tm=tm, T=T),
        grid=(N // tm,),
        in_specs=_shift_specs(tm, D, xprev.shape[0], 1) + [
            full((D, R)), full((R, C)), full((D, R)), full((R, C)), full((1, C)), full((1, C))],
        out_specs=[o_spec, o_spec],
        out_shape=[jax.ShapeDtypeStruct((N, C), F32)] * 2,
        compiler_params=_cparams(("parallel",)),
        name="rwkv_lora",
    )(h, h, xprev, g.reshape(1, D), mu, lw1, lw2, la1, la2, w0.reshape(1, C), a0.reshape(1, C))


def _norm_mm_kernel(h_ref, g_ref, w_ref, o_ref, xs_scr):
    @pl.when(pl.program_id(1) == 0)
    def _():
        rc = min(h_ref.shape[0], 256)

        def norm_chunk(c, carry):
            rows = pl.ds(pl.multiple_of(c * rc, rc), rc)
            xs_scr[rows, :] = _norm_rows(h_ref[rows, :], g_ref[...]).astype(BF16)
            return carry

        lax.fori_loop(0, h_ref.shape[0] // rc, norm_chunk, 0)

    o_ref[...] = jnp.dot(xs_scr[...], w_ref[...], preferred_element_type=F32)


def norm_mm(h, g, w, widx=()):
    N, D = h.shape
    NO = w.shape[-1]
    tm = min(N, 1024)
    tn = _col_tile(N, NO, 1024)
    nlead = len(widx)
    return pl.pallas_call(
        _norm_mm_kernel,
        grid=(N // tm, pl.cdiv(NO, tn)),
        in_specs=[pl.BlockSpec((tm, D), lambda i, n: (i, 0)),
                  pl.BlockSpec((1, D), lambda i, n: (0, 0)),
                  pl.BlockSpec((None,) * nlead + (D, tn), lambda i, n: tuple(widx) + (0, n))],
        out_specs=pl.BlockSpec((tm, tn), lambda i, n: (i, n)),
        out_shape=jax.ShapeDtypeStruct((N, NO), F32),
        scratch_shapes=[pltpu.VMEM((tm, D), BF16)],
        compiler_params=_cparams(("parallel", "arbitrary")),
        name="norm_mm",
    )(h, g.reshape(1, D), w)


def _mm_res_kernel(x_ref, h_ref, w_ref, o_ref, ob_ref):
    h1 = h_ref[...] + jnp.dot(x_ref[...].astype(BF16), w_ref[...], preferred_element_type=F32)
    o_ref[...] = h1
    ob_ref[...] = h1.astype(BF16)


def _ple_gate_kernel(h1b_ref, h1_ref, p_ref, wp_ref, wg_ref, o_ref):
    gate = jnp.dot(h1b_ref[...], wg_ref[...], preferred_element_type=F32)
    ple = jnp.dot(p_ref[...].astype(BF16), wp_ref[...], preferred_element_type=F32)
    o_ref[...] = h1_ref[...] + ple * _sigmoid(gate)


def out_ple(x, h, p, w_out, oidx, w_ple, w_gate, layer):
    N, C = x.shape
    D = h.shape[1]
    DP = p.shape[1]
    tm = min(N, 1024)
    tn = _col_tile(N, D, 512)
    tile = pl.BlockSpec((tm, tn), lambda i, n: (i, n))
    h1, h1b = pl.pallas_call(
        _mm_res_kernel,
        grid=(N // tm, D // tn),
        in_specs=[pl.BlockSpec((tm, C), lambda i, n: (i, 0)), tile,
                  pl.BlockSpec((None, C, tn), lambda i, n: (oidx, 0, n))],
        out_specs=[tile, tile],
        out_shape=[jax.ShapeDtypeStruct((N, D), F32), jax.ShapeDtypeStruct((N, D), BF16)],
        compiler_params=_cparams(("parallel", "parallel")),
        name="mm_res",
    )(x, h, w_out)
    tg = _col_tile(N, D, 1024)
    gtile = pl.BlockSpec((tm, tg), lambda i, n: (i, n))
    return pl.pallas_call(
        _ple_gate_kernel,
        grid=(N // tm, D // tg),
        in_specs=[pl.BlockSpec((tm, D), lambda i, n: (i, 0)), gtile,
                  pl.BlockSpec((tm, DP), lambda i, n: (i, 0)),
                  pl.BlockSpec((None, DP, tg), lambda i, n: (layer, 0, n)),
                  pl.BlockSpec((None, D, tg), lambda i, n: (layer, 0, n))],
        out_specs=gtile,
        out_shape=jax.ShapeDtypeStruct((N, D), F32),
        compiler_params=_cparams(("parallel", "parallel")),
        name="ple_gate",
    )(h1b, h1, p, w_ple, w_gate)


def _scan_kernel(r_ref, k_ref, v_ref, zg_ref, lw_ref, a_ref, pv_ref, s0_ref, o_ref, sfin_ref, s_scr, *, C, NH, GP):
    L = NH * HEAD_A
    NC = NH * C
    ci = pl.program_id(2)

    @pl.when(ci == 0)
    def _():
        for gp in range(GP):
            rows = []
            for hh in range(NH):
                pieces = [s0_ref[0, gp * NH + hh] if h2 == hh else jnp.zeros((HEAD_A, HEAD_A), F32)
                          for h2 in range(NH)]
                rows.append(jnp.concatenate(pieces, axis=1))
            s_scr[gp] = jnp.concatenate(rows, axis=0)

    row_c = lax.broadcasted_iota(jnp.int32, (C, NC), 0)
    col_s = lax.broadcasted_iota(jnp.int32, (C, NC), 1) % C
    tri_strict = col_s < row_c
    tri_incl = (lax.broadcasted_iota(jnp.int32, (C, 2 * NC), 1) % C
                <= lax.broadcasted_iota(jnp.int32, (C, 2 * NC), 0))
    st_mask = (lax.broadcasted_iota(jnp.int32, (NC, L), 0) // C
               == lax.broadcasted_iota(jnp.int32, (NC, L), 1) // HEAD_A)
    bd_mask = (lax.broadcasted_iota(jnp.int32, (NC, NC), 0) // C
               == lax.broadcasted_iota(jnp.int32, (NC, NC), 1) // C)
    head_mask = (lax.broadcasted_iota(jnp.int32, (L, L), 0) // HEAD_A
                 == lax.broadcasted_iota(jnp.int32, (L, L), 1) // HEAD_A)

    def st(x):
        return jnp.where(st_mask, jnp.concatenate([x] * NH, axis=0), 0.0)

    def bd(w):
        return jnp.where(bd_mask, jnp.concatenate([w] * NH, axis=0), 0.0)

    n_double = int(math.log2(C))
    each = lambda f, *cols: [f(*xs) for xs in zip(*cols)]
    sls = [slice(gp * L, (gp + 1) * L) for gp in range(GP)]
    head_of_lane = lax.broadcasted_iota(jnp.int32, (1, L), 1) // HEAD_A

    def hsum(x):
        out = None
        for hh in range(NH):
            sh = jnp.sum(jnp.where(head_of_lane == hh, x, 0.0), axis=-1, keepdims=True)
            out = sh if out is None else jnp.where(head_of_lane == hh, sh, out)
        return out

    k_k, k_a, r_k, ln_w, ln_b = ([pv_ref[n:n + 1, sl] for sl in sls] for n in range(5))
    lw = [lw_ref[0, :, sl] for sl in sls]
    a_sig = [a_ref[0, :, sl] for sl in sls]
    r = [r_ref[0, :, sl] for sl in sls]
    k_raw = [k_ref[0, :, sl] for sl in sls]
    v = [v_ref[0, :, sl] for sl in sls]
    kk = each(lambda x, w: x * w, k_raw, k_k)
    kk = each(lambda x: x * lax.rsqrt(jnp.maximum(hsum(x * x), 1e-24)), kk)
    k = each(lambda x, a, w: x * (1.0 + (a - 1.0) * w), k_raw, a_sig, k_a)
    lw_hi = each(lambda z: z.astype(BF16), lw)
    lw_lo = each(lambda z, hi: (z - hi.astype(F32)).astype(BF16), lw, lw_hi)
    cum_mat = (lax.broadcasted_iota(jnp.int32, (C, C), 1)
               <= lax.broadcasted_iota(jnp.int32, (C, C), 0)).astype(BF16)
    cum2 = each(lambda hi, lo: jnp.dot(cum_mat, jnp.concatenate([hi, lo], axis=1), preferred_element_type=F32),
                lw_hi, lw_lo)
    cum = [c2[:, :L] + c2[:, L:] for c2 in cum2]
    p_incl = each(jnp.exp, cum)
    p_inv = each(lambda z: jnp.exp(-z), cum)
    at = each(lambda x, c, w: -x * jnp.exp(c - w), kk, cum, lw)
    rt = each(lambda x, p: x * p, r, p_incl)
    bt = each(lambda x, a, p: x * a * p, kk, a_sig, p_inv)
    kt = each(lambda x, p: x * p, k, p_inv)
    S = [s_scr[gp] for gp in range(GP)]
    ar = each(lambda x, y: jnp.concatenate([x, y], axis=0), at, rt)
    bk_st = each(lambda x, y: jnp.concatenate([st(x), st(y)], axis=0), bt, kt)
    Gm = each(_bdot_nt, ar, bk_st)
    w_ab = [jnp.where(tri_strict, g[:C, :NC], 0.0) for g in Gm]
    tm = w_ab
    pw = each(lambda w: _bdot(w, bd(w)), w_ab)
    LH = each(_bdot_nt, ar, S)
    v_st = each(st, v)
    x = [lh[:C] + _bdot(jnp.where(tri_strict, g[:C, NC:], 0.0), vs) for lh, g, vs in zip(LH, Gm, v_st)]
    for it in range(1, n_double):
        if it < n_double - 1:
            both = each(lambda t, p: _bdot(p, jnp.concatenate([bd(t), bd(p)], axis=1)), tm, pw)
            tm = each(lambda t, p, b2: t + p + b2[:, :NC], tm, pw, both)
            pw = [b2[:, NC:] for b2 in both]
        else:
            tm = each(lambda t, p: t + p + _bdot(p, bd(t)), tm, pw)
    u = each(lambda xx, t: xx + _bdot(t, st(xx)), x, tm)
    o = [lh[C:] + _bdot(jnp.where(tri_incl, g[C:], 0.0), jnp.concatenate([st(uu), vs], axis=0))
         for lh, g, uu, vs in zip(LH, Gm, u, v_st)]
    inv_n = 1.0 / HEAD_A
    dev = each(lambda x: x - hsum(x) * inv_n, o)
    gn = each(lambda d, w, b_: d * lax.rsqrt(hsum(d * d) * inv_n + GN_EPS) * w + b_, dev, ln_w, ln_b)
    bonus = each(lambda rr, kx, w, vv: hsum(rr * kx * w) * vv, r, k, r_k, v)
    for sl, y, bo in zip(sls, gn, bonus):
        zg = zg_ref[0, :, sl]
        o_ref[0, :, sl] = ((y + bo) * (zg * _sigmoid(zg))).astype(o_ref.dtype)
    ds = [_bdot_tn(jnp.concatenate([uu, vv], axis=0), jnp.concatenate([b_, k_], axis=0))
          for uu, vv, b_, k_ in zip(u, v, bt, kt)]
    for gp in range(GP):
        s_scr[gp] = (S[gp] + jnp.where(head_mask, ds[gp], 0.0)) * p_incl[gp][C - 1:C, :]

    @pl.when(ci == pl.num_programs(2) - 1)
    def _():
        for gp in range(GP):
            s_all = s_scr[gp]
            for hh in range(NH):
                blk = slice(hh * HEAD_A, (hh + 1) * HEAD_A)
                sfin_ref[0, gp * NH + hh] = s_all[blk, blk]


def rwkv_scan(rkvg, lw, a, pvec, s0):
    _, B, T, CA = rkvg.shape
    H = CA // HEAD_A
    NH, C, GP = SCAN_NH, SCAN_C, SCAN_GP
    assert NH * C == LANES and H % (NH * GP) == 0
    L = NH * HEAD_A
    NG = H // NH
    Tp = -(-T // C) * C
    if Tp != T:
        rkvg = jnp.pad(rkvg, ((0, 0), (0, 0), (0, Tp - T), (0, 0)))
        lw, a = (jnp.pad(z, ((0, 0), (0, Tp - T), (0, 0))) for z in (lw, a))
    seq_spec = pl.BlockSpec((1, C, GP * L), lambda bi, gi, ci: (bi, ci, gi))
    proj_spec = lambda j: pl.BlockSpec((None, 1, C, GP * L), lambda bi, gi, ci: (j, bi, ci, gi))
    st_spec = pl.BlockSpec((1, GP * NH, HEAD_A, HEAD_A), lambda bi, gi, ci: (bi, gi, 0, 0))
    o, s_fin = pl.pallas_call(
        functools.partial(_scan_kernel, C=C, NH=NH, GP=GP),
        grid=(B, NG // GP, Tp // C),
        in_specs=[proj_spec(j) for j in range(4)] + [seq_spec, seq_spec,
                  pl.BlockSpec((5, GP * L), lambda bi, gi, ci: (0, gi)), st_spec],
        out_specs=[seq_spec, st_spec],
        out_shape=[jax.ShapeDtypeStruct((B, Tp, CA), BF16), jax.ShapeDtypeStruct((B, H, HEAD_A, HEAD_A), F32)],
        scratch_shapes=[pltpu.VMEM((GP, L, L), F32)],
        compiler_params=_cparams(("parallel", "parallel", "arbitrary")),
        name="rwkv_scan",
    )(rkvg, rkvg, rkvg, rkvg, lw, a, pvec, s0)
    return o[:, :T], s_fin


def _gelu_tanh(x):
    c = math.sqrt(2.0 / math.pi)
    return 0.5 * x * (1.0 + jnp.tanh(c * (x + 0.044715 * (x * x * x))))


def _compress_kernel(pt_ref, *refs, PGS):
    del pt_ref
    page_refs = refs[:PGS]
    next_ref, pe_ref, w1_ref, w2_ref, out_ref = refs[PGS:]
    CPP = page_refs[0].shape[1]
    NCH = PGS * CPP
    CG = 2 * G_KV
    M = (NCH + 1) * CG

    def rows_of(l, hf):
        pe = pe_ref[hf, l]
        parts = [(page_refs[i][0, :, l] + pe[None]).reshape(CPP * CG, HEAD_B) for i in range(PGS)]
        parts.append(next_ref[0, 0, l] + pe)
        return jnp.concatenate(parts, axis=0)

    top = jnp.zeros((M, 2 * HEAD_B), F32)
    bot = jnp.zeros((M, 2 * HEAD_B), F32)
    for l in range(0, S_CMP, 2):
        wrows = pl.ds(l * HEAD_B, 2 * HEAD_B)
        xt = jnp.concatenate([rows_of(l, 0), rows_of(l + 1, 0)], axis=1).astype(BF16)
        top = top + jnp.dot(xt, w1_ref[0, wrows, :], preferred_element_type=F32)
        xb = jnp.concatenate([rows_of(l, 1), rows_of(l + 1, 1)], axis=1).astype(BF16)
        bot = bot + jnp.dot(xb, w1_ref[1, wrows, :], preferred_element_type=F32)
    is_k = (lax.broadcasted_iota(jnp.int32, (M, 1), 0) % CG) < G_KV
    pick = lambda z, n: jnp.where(is_k[:n], z[:n, :HEAD_B], z[:n, HEAD_B:])
    hcur = pick(top, NCH * CG) + pick(bot, M)[CG:]
    o2 = jnp.dot(_gelu_tanh(hcur).astype(BF16), w2_ref[...], preferred_element_type=F32)
    out_ref[0] = pick(o2, NCH * CG).reshape(NCH, CG, HEAD_B)


def compress_kv(pool5, table, pe_cmp, w1, w2):
    NP, CPP = pool5.shape[:2]
    B, n_pages = table.shape
    PGS = max(d for d in (16, 8, 4, 2, 1) if n_pages % d == 0)
    NCH = PGS * CPP
    CG = 2 * G_KV
    half = S_CMP * HEAD_B
    pe_r = jnp.repeat(pe_cmp.reshape(2, 2, S_CMP, HEAD_B).transpose(1, 2, 0, 3), G_KV, axis=2)
    w1_r = w1.reshape(2, 2, half, HEAD_B).transpose(1, 2, 0, 3).reshape(2, half, 2 * HEAD_B)
    w2_r = jnp.concatenate([w2[0], w2[1]], axis=1)

    def page_map(i):
        return lambda b, s, pt: (pt[b, s * PGS + i], 0, 0, 0, 0)

    def next_map(b, s, pt):
        return (pt[b, jnp.minimum((s + 1) * PGS, n_pages - 1)], 0, 0, 0, 0)

    const = lambda n: (lambda b, s, pt: (0,) * n)
    grid_spec = pltpu.PrefetchScalarGridSpec(
        num_scalar_prefetch=1,
        grid=(B, n_pages // PGS),
        in_specs=[pl.BlockSpec((1, CPP, S_CMP, CG, HEAD_B), page_map(i)) for i in range(PGS)] + [
            pl.BlockSpec((1, 1, S_CMP, CG, HEAD_B), next_map),
            pl.BlockSpec((2, S_CMP, CG, HEAD_B), const(4)),
            pl.BlockSpec((2, half, 2 * HEAD_B), const(3)),
            pl.BlockSpec((HEAD_B, 2 * HEAD_B), const(2)),
        ],
        out_specs=pl.BlockSpec((1, NCH, CG, HEAD_B), lambda b, s, pt: (b, s, 0, 0)),
    )
    return pl.pallas_call(
        functools.partial(_compress_kernel, PGS=PGS),
        grid_spec=grid_spec,
        out_shape=jax.ShapeDtypeStruct((B, n_pages * CPP, CG, HEAD_B), F32),
        compiler_params=_cparams(("parallel", "arbitrary")),
        name="compress_kv",
    )(table, *([pool5] * PGS), pool5, pe_r, w1_r, w2_r)


def _stack_heads(q, HG):
    return jnp.concatenate([q[:, h * HEAD_B:(h + 1) * HEAD_B] for h in range(HG)], axis=0)


def _masked_softmax_rows(s, mask):
    s = jnp.where(mask, s, NEG)
    m = jnp.max(s, axis=-1, keepdims=True)
    e = jnp.where(mask, jnp.exp(s - m), 0.0)
    l = jnp.sum(e, axis=-1, keepdims=True)
    return e / jnp.where(l > 0.0, l, 1.0)


def _attend_stacked(s, dist, mask, v, slopes_ref, g, HG, tq):
    ps = []
    psum = jnp.zeros(dist.shape, F32)
    for h in range(HG):
        p = _masked_softmax_rows(s[h * tq:(h + 1) * tq] - slopes_ref[g * HG + h] * dist, mask)
        psum = psum + p
        ps.append(p.astype(BF16))
    o = jnp.dot(jnp.concatenate(ps, axis=0), v.astype(BF16), preferred_element_type=F32)
    return o, psum


def _unstack_store(o_ref, o, HG, tq):
    for h in range(HG):
        o_ref[0, :, h * HEAD_B:(h + 1) * HEAD_B] = o[h * tq:(h + 1) * tq]


def _nsa_cmp_kernel(slopes_ref, q_ref, kc_ref, vc_ref, o_ref, selm_ref, *, tq, HG, nc, nsb, pos0):
    g = pl.program_id(1)
    qt = pl.program_id(2)
    NCp = kc_ref.shape[1]
    NSBp = selm_ref.shape[3]
    q_st = _stack_heads(q_ref[0] * (HEAD_B ** -0.5), HG)
    s = _bdot_nt(q_st, kc_ref[0])
    qpos = pos0 + qt * tq + lax.broadcasted_iota(jnp.int32, (tq, 1), 0)
    cidx = lax.broadcasted_iota(jnp.int32, (1, NCp), 1)
    cend = S_CMP * cidx + (L_CMP - 1)
    mask = (cend <= qpos) & (cidx < nc)
    dist = (qpos - cend).astype(F32)
    o, imp_c = _attend_stacked(s, dist, mask, vc_ref[0], slopes_ref, g, HG, tq)
    _unstack_store(o_ref, o, HG, tq)

    crow = lax.broadcasted_iota(jnp.int32, (NCp, NSBp), 0)
    jcol = lax.broadcasted_iota(jnp.int32, (NCp, NSBp), 1)
    overlap = ((S_CMP * crow < L_SEL * (jcol + 1)) & (S_CMP * crow + L_CMP > L_SEL * jcol)
               & (crow < nc)).astype(F32)
    imp = jnp.dot(imp_c, overlap, precision=lax.Precision.HIGHEST, preferred_element_type=F32)
    lane = lax.broadcasted_iota(jnp.int32, (tq, NSBp), 1)
    cur = jnp.right_shift(qpos, int(math.log2(L_SEL)))
    forced = ((lane == 0) | (lane == cur) | (lane == cur - 1)).astype(F32)
    score = jnp.where(lane <= cur, imp + FORCE_BONUS * forced, NEG)
    score = jnp.where(lane < nsb, score, -3e38)

    if tq % LANES == 0 and NSBp == LANES:
        nr = -(-nsb // SUBLANES) * SUBLANES
        st = score.T[:nr]
        sub = lax.broadcasted_iota(jnp.int32, (nr, tq), 0)
        cnt = jnp.zeros((nr, tq), F32)
        for i in range(nsb):
            row = st[i:i + 1, :]
            beats = (row > st) | ((row == st) & (sub > i))
            cnt = cnt + jnp.where(beats, 1.0, 0.0)
        sel_t = jnp.where((cnt < TOPK_SEL) & (st > 0.5 * NEG), 1.0, 0.0)
        selm_ref[0, 0] = jnp.concatenate([sel_t, jnp.zeros((NSBp - nr, tq), F32)], axis=0).T
    else:
        cnt = jnp.zeros((tq, NSBp), F32)
        for i in range(nsb):
            col = score[:, i:i + 1]
            beats = (col > score) | ((col == score) & (lane > i))
            cnt = cnt + jnp.where(beats, 1.0, 0.0)
        sel = (cnt < TOPK_SEL) & (score > 0.5 * NEG)
        selm_ref[0, 0] = sel.astype(F32)


def nsa_cmp(proj, kvc, slopes, *, tq, nc, nsb, pos0):
    B, T, _ = proj.shape
    HG = slopes.shape[0] // G_KV
    NCp = kvc.shape[1]
    NSBp = -(-nsb // LANES) * LANES
    gw = HG * HEAD_B
    return pl.pallas_call(
        functools.partial(_nsa_cmp_kernel, tq=tq, HG=HG, nc=nc, nsb=nsb, pos0=pos0),
        grid=(B, G_KV, T // tq),
        in_specs=[pl.BlockSpec(memory_space=pltpu.SMEM),
                  pl.BlockSpec((1, tq, gw), lambda b, g, t: (b, t, g)),
                  pl.BlockSpec((1, NCp, HEAD_B), lambda b, g, t: (b, 0, g)),
                  pl.BlockSpec((1, NCp, HEAD_B), lambda b, g, t: (b, 0, G_KV + g))],
        out_specs=[pl.BlockSpec((1, tq, gw), lambda b, g, t: (b, t, g)),
                   pl.BlockSpec((1, 1, tq, NSBp), lambda b, g, t: (b, g, t, 0))],
        out_shape=[jax.ShapeDtypeStruct((B, T, G_KV * gw), F32),
                   jax.ShapeDtypeStruct((B, G_KV, T, NSBp), F32)],
        compiler_params=_cparams(("parallel", "parallel", "parallel")),
        name="nsa_cmp",
    )(slopes, proj, kvc.reshape(B, NCp, -1), kvc.reshape(B, NCp, -1))


LOG2E = 1.4426950408889634


def _bf16_part(x):
    return x.astype(BF16).astype(F32)


def _alibi_lhs(q, slope_col):
    c = slope_col * LOG2E
    c1 = _bf16_part(c)
    c2 = _bf16_part(c - c1)
    c3 = _bf16_part(c - c1 - c2)
    lane = lax.broadcasted_iota(jnp.int32, q.shape, 1)
    extra = jnp.where((lane == 0) | (lane == 3), c1,
                      jnp.where((lane == 1) | (lane == 4), c2, jnp.where((lane == 2) | (lane == 5), c3, 0.0)))
    return jnp.concatenate([q, extra], axis=1).astype(BF16)


def _alibi_rhs(k, k0):
    pos = k0 + lax.broadcasted_iota(jnp.int32, k.shape, 0)
    lane = lax.broadcasted_iota(jnp.int32, k.shape, 1)
    hi = jnp.bitwise_and(pos, -L_SEL)
    extra = jnp.where(lane < 3, hi, jnp.where(lane < 6, pos - hi, 0)).astype(F32)
    return jnp.concatenate([k, extra], axis=1).astype(BF16)


def _with_ones(v):
    lane = lax.broadcasted_iota(jnp.int32, v.shape, 1)
    return jnp.concatenate([v, jnp.where(lane == 0, 1.0, 0.0)], axis=1).astype(BF16)


def _silu(x):
    return x * _sigmoid(x)


def _nsa_selwin_prompt_kernel(slopes_ref, q_ref, selm_ref, ks_ref, vs_ref, kw_ref, vw_ref,
                              ocmp_ref, zc_ref, zs_ref, zw_ref, gate_ref, o_ref, s_scr, *, tq, HG, T, WS, SEG):
    g = pl.program_id(1)
    qt = pl.program_id(2)
    NSBp = selm_ref.shape[3]
    R = HG * tq
    qpos = qt * tq + lax.broadcasted_iota(jnp.int32, (tq, 1), 0)
    slope_col = jnp.concatenate([jnp.full((tq, 1), slopes_ref[g * HG + h], F32) for h in range(HG)], axis=0)
    tile_heads = lambda x: jnp.concatenate([x] * HG, axis=0)
    q2 = _alibi_lhs(_stack_heads(q_ref[0] * (HEAD_B ** -0.5 * LOG2E), HG), slope_col)

    selm_b = selm_ref[0, 0].astype(BF16)
    nseg = (qt * tq + tq + SEG - 1) // SEG

    def seg_scores(si, m):
        k0 = pl.multiple_of(si * SEG, SEG)
        kcol = k0 + lax.broadcasted_iota(jnp.int32, (NSBp, SEG), 1)
        expand = (jnp.right_shift(kcol, int(math.log2(L_SEL)))
                  == lax.broadcasted_iota(jnp.int32, (NSBp, SEG), 0)).astype(BF16)
        in_blk = jnp.dot(selm_b, expand, preferred_element_type=F32) > 0.5
        kpos = k0 + lax.broadcasted_iota(jnp.int32, (1, SEG), 1)
        mask = tile_heads(in_blk & (kpos <= qpos))
        s = lax.dot_general(q2, _alibi_rhs(ks_ref[0, pl.ds(k0, SEG), :], k0), NT_DIMS, preferred_element_type=F32)
        s = jnp.where(mask, s, NEG)
        s_scr[si] = s
        return jnp.maximum(m, jnp.max(s, axis=-1, keepdims=True))

    m = lax.fori_loop(0, nseg, seg_scores, jnp.full((R, 1), NEG, F32))

    def seg_pv(si, acc):
        k0 = pl.multiple_of(si * SEG, SEG)
        e = jnp.exp2(s_scr[si] - m).astype(BF16)
        return acc + jnp.dot(e, _with_ones(vs_ref[0, pl.ds(k0, SEG), :]), preferred_element_type=F32)

    acc = lax.fori_loop(0, nseg, seg_pv, jnp.zeros((R, 2 * HEAD_B), F32))
    o = acc[:, :HEAD_B] / acc[:, HEAD_B:HEAD_B + 1]

    start = pl.multiple_of(jnp.clip(qt * tq - WINDOW, 0, T - WS), SUBLANES)
    distw = qpos - (start + lax.broadcasted_iota(jnp.int32, (1, WS), 1))
    maskw = tile_heads((distw >= 0) & (distw < WINDOW))
    sw = lax.dot_general(q2, _alibi_rhs(kw_ref[0, pl.ds(start, WS), :], start), NT_DIMS, preferred_element_type=F32)
    sw = jnp.where(maskw, sw, NEG)
    ew = jnp.exp2(sw - jnp.max(sw, axis=-1, keepdims=True)).astype(BF16)
    accw = jnp.dot(ew, _with_ones(vw_ref[0, pl.ds(start, WS), :]), preferred_element_type=F32)
    ow = accw[:, :HEAD_B] / accw[:, HEAD_B:HEAD_B + 1]

    HB = G_KV * HG
    gates = _sigmoid(gate_ref[0])
    lane = lax.broadcasted_iota(jnp.int32, gates.shape, 1)
    gate_col = lambda idx: jnp.sum(jnp.where(lane == idx, gates, 0.0), axis=1, keepdims=True)
    for h in range(HG):
        hs = slice(h * HEAD_B, (h + 1) * HEAD_B)
        rs = slice(h * tq, (h + 1) * tq)
        hd = g * HG + h
        y = (gate_col(hd) * ocmp_ref[0, :, hs] * _silu(zc_ref[0, :, hs])
             + gate_col(HB + hd) * o[rs] * _silu(zs_ref[0, :, hs])
             + gate_col(2 * HB + hd) * ow[rs] * _silu(zw_ref[0, :, hs]))
        o_ref[0, :, hs] = y.astype(o_ref.dtype)


def nsa_selwin_prompt(proj, rows, selm, o_cmp, slopes, *, tq):
    B, T, _ = proj.shape
    HG = slopes.shape[0] // G_KV
    NSBp = selm.shape[3]
    gw = HG * HEAD_B
    CB = G_KV * gw
    WS = min(T, WINDOW + tq)
    SEG = min(T, 512)
    assert T % SEG == 0
    kv_spec = lambda c: pl.BlockSpec((1, T, HEAD_B), lambda b, g, t: (b, 0, c * G_KV + g))
    head_spec = lambda blk: pl.BlockSpec((1, tq, gw), lambda b, g, t: (b, t, blk * G_KV + g))
    return pl.pallas_call(
        functools.partial(_nsa_selwin_prompt_kernel, tq=tq, HG=HG, T=T, WS=WS, SEG=SEG),
        grid=(B, G_KV, T // tq),
        in_specs=[pl.BlockSpec(memory_space=pltpu.SMEM),
                  head_spec(0),
                  pl.BlockSpec((1, 1, tq, NSBp), lambda b, g, t: (b, g, t, 0)),
                  kv_spec(2), kv_spec(3), kv_spec(4), kv_spec(5),
                  head_spec(0), head_spec(1), head_spec(2), head_spec(3),
                  pl.BlockSpec((1, tq, LANES), lambda b, g, t: (b, t, 4 * CB // LANES))],
        out_specs=head_spec(0),
        out_shape=jax.ShapeDtypeStruct((B, T, CB), BF16),
        scratch_shapes=[pltpu.VMEM((T // SEG, HG * tq, SEG), F32)],
        compiler_params=_cparams(("parallel", "parallel", "parallel")),
        name="nsa_selwin_prompt",
    )(slopes, proj, selm, rows, rows, rows, rows, o_cmp, proj, proj, proj, proj)


def _nsa_selwin_sample_kernel(pt_ref, slopes_ref, q_ref, selm_ref, selst_ref, *refs, PGS, PS, HG, TQ, pos0, n_new, n_win):
    del pt_ref
    page_refs = refs[:PGS]
    (new_ref, cwin_ref, ocmp_ref, zc_ref, zs_ref, zw_ref, gate_ref, o_ref,
     m_scr, l_scr, acc_scr) = refs[PGS:]
    st = pl.program_id(1)
    NSBp = selm_ref.shape[3]
    GW = G_KV * HEAD_B
    sel_shift = int(math.log2(L_SEL))

    @pl.when(st == 0)
    def _():
        m_scr[...] = jnp.full(m_scr.shape, NEG, F32)
        l_scr[...] = jnp.zeros(l_scr.shape, F32)
        acc_scr[...] = jnp.zeros(acc_scr.shape, F32)

    qpos = pos0 + lax.broadcasted_iota(jnp.int32, (TQ, 1), 0)
    lane_j = lax.broadcasted_iota(jnp.int32, (TQ, NSBp), 1)
    tile_heads = lambda x: jnp.concatenate([x] * HG, axis=0)

    def sel_col(selm_g, j):
        return jnp.sum(jnp.where(lane_j == j, selm_g, 0.0), axis=1, keepdims=True)

    def online_update(s, mask, v):
        gs = range(G_KV)
        m_old = [m_scr[g] for g in gs]
        m_new = [jnp.maximum(m_old[g], jnp.max(jnp.where(mask[g], s[g], NEG), axis=-1, keepdims=True)) for g in gs]
        e = [jnp.where(mask[g], jnp.exp(s[g] - m_new[g]), 0.0) for g in gs]
        alpha = [jnp.exp(m_old[g] - m_new[g]) for g in gs]
        pv = [_bdot(e[g], v[g]) for g in gs]
        for g in gs:
            l_scr[g] = alpha[g] * l_scr[g] + jnp.sum(e[g], axis=-1, keepdims=True)
            acc_scr[g] = alpha[g] * acc_scr[g] + pv[g]
            m_scr[g] = m_new[g]

    NK = PGS * PS
    kpos = st * NK + lax.broadcasted_iota(jnp.int32, (1, NK), 1)
    expand = (jnp.right_shift(lax.broadcasted_iota(jnp.int32, (LANES, NK), 1), sel_shift)
              == lax.broadcasted_iota(jnp.int32, (LANES, NK), 0)).astype(BF16)
    in_blk_all = jnp.dot(selst_ref[0, 0].astype(BF16), expand, preferred_element_type=F32)
    dist = qpos - kpos
    distf = tile_heads(dist.astype(F32))
    gs = range(G_KV)
    q_st = [_stack_heads(q_ref[0, :, g * HG * HEAD_B:(g + 1) * HG * HEAD_B] * (HEAD_B ** -0.5), HG).astype(BF16)
            for g in gs]
    slope_col = [jnp.concatenate([jnp.full((TQ, 1), slopes_ref[g * HG + h], F32) for h in range(HG)], axis=0)
                 for g in gs]
    slabs = [jnp.swapaxes(page_refs[i][0].reshape(PS, 2 * G_KV, HEAD_B), 0, 1) for i in range(PGS)]
    k = [jnp.concatenate([slabs[i][g] for i in range(PGS)], axis=0) for g in gs]
    v = [jnp.concatenate([slabs[i][G_KV + g] for i in range(PGS)], axis=0) for g in gs]
    mask = [tile_heads((in_blk_all[g * TQ:(g + 1) * TQ] > 0.5) & (dist >= 0)) for g in gs]
    s = [_bdot_nt(q_st[g], k[g]) - slope_col[g] * distf for g in gs]
    online_update(s, mask, v)

    @pl.when(st == pl.num_programs(1) - 1)
    def _():
        NN = new_ref.shape[1]
        rnew = lax.broadcasted_iota(jnp.int32, (1, NN), 1)
        kpos_n = pos0 + rnew
        dist_n = qpos - kpos_n
        ok_n = (rnew < n_new) & (dist_n >= 0)
        jn = pos0 >> sel_shift
        kpos_w = pos0 - n_win + lax.broadcasted_iota(jnp.int32, (1, n_win), 1)
        dist_w = qpos - kpos_w
        dist_wall = jnp.concatenate([dist_w, dist_n], axis=1)
        mask_wall = jnp.concatenate([(dist_w >= 0) & (dist_w < WINDOW), ok_n & (dist_n < WINDOW)], axis=1)
        kn = [new_ref[0, :, 2 * GW + g * HEAD_B:2 * GW + (g + 1) * HEAD_B] for g in gs]
        vn = [new_ref[0, :, 3 * GW + g * HEAD_B:3 * GW + (g + 1) * HEAD_B] for g in gs]
        mask_n = [tile_heads((sel_col(selm_ref[0, g], jn) > 0.5) & ok_n) for g in gs]
        dist_nf = tile_heads(dist_n.astype(F32))
        s_n = [_bdot_nt(q_st[g], kn[g]) - slope_col[g] * dist_nf for g in gs]
        online_update(s_n, mask_n, vn)
        kw = [jnp.concatenate([cwin_ref[0, :, g * HEAD_B:(g + 1) * HEAD_B],
                               new_ref[0, :, 4 * GW + g * HEAD_B:4 * GW + (g + 1) * HEAD_B]], axis=0) for g in gs]
        vw = [jnp.concatenate([cwin_ref[0, :, GW + g * HEAD_B:GW + (g + 1) * HEAD_B],
                               new_ref[0, :, 5 * GW + g * HEAD_B:5 * GW + (g + 1) * HEAD_B]], axis=0) for g in gs]
        dist_wf = tile_heads(dist_wall.astype(F32))
        mask_w = tile_heads(mask_wall)
        sw = [_bdot_nt(q_st[g], kw[g]) - slope_col[g] * dist_wf for g in gs]
        pw = [_masked_softmax_rows(sw[g], mask_w) for g in gs]
        ow = [_bdot(pw[g], vw[g]) for g in gs]
        HB = G_KV * HG
        gates = _sigmoid(gate_ref[0])
        for g in gs:
            l = l_scr[g]
            o = acc_scr[g] / jnp.where(l > 0.0, l, 1.0)
            for h in range(HG):
                hd = g * HG + h
                hs = slice(hd * HEAD_B, (hd + 1) * HEAD_B)
                rs = slice(h * TQ, (h + 1) * TQ)
                y = (gates[:, hd:hd + 1] * ocmp_ref[0, :, hs] * _silu(zc_ref[0, :, hs])
                     + gates[:, HB + hd:HB + hd + 1] * o[rs] * _silu(zs_ref[0, :, hs])
                     + gates[:, 2 * HB + hd:2 * HB + hd + 1] * ow[g][rs] * _silu(zw_ref[0, :, hs]))
                o_ref[0, :, hs] = y.astype(o_ref.dtype)


def nsa_selwin_sample(proj, selm, o_cmp, pool5, table, new_rows, cwin, slopes, *, pos0, n_new):
    B, TQ, _ = proj.shape
    HG = slopes.shape[0] // G_KV
    NP, CPP = pool5.shape[:2]
    PS = CPP * S_CMP
    n_pages = table.shape[1]
    NSBp = selm.shape[3]
    PGS = max(d for d in (8, 4, 2, 1) if n_pages % d == 0)
    CB = G_KV * HG * HEAD_B
    GW = G_KV * HEAD_B
    NN = new_rows.shape[1]
    n_win = cwin.shape[1]
    assert pos0 % L_SEL == 0 and n_new <= L_SEL and pos0 == n_pages * PS

    def page_map(i):
        return lambda b, s, pt: (pt[b, s * PGS + i], 0, 0, 1, 0)

    n_steps = n_pages // PGS
    bps = PGS * PS // L_SEL
    assert bps <= LANES
    selst = selm[:, :, :, :n_steps * bps].reshape(B, G_KV, TQ, n_steps, bps).transpose(0, 3, 1, 2, 4)
    selst = jnp.pad(selst.reshape(B, n_steps, G_KV * TQ, bps), ((0, 0), (0, 0), (0, 0), (0, LANES - bps)))

    const = lambda b, s, pt: (b, 0, 0)
    wide = lambda blk: pl.BlockSpec((1, TQ, CB), lambda b, s, pt: (b, 0, blk))
    grid_spec = pltpu.PrefetchScalarGridSpec(
        num_scalar_prefetch=1,
        grid=(B, n_steps),
        in_specs=[pl.BlockSpec(memory_space=pltpu.SMEM),
                  pl.BlockSpec((1, TQ, CB), const),
                  pl.BlockSpec((1, G_KV, TQ, NSBp), lambda b, s, pt: (b, 0, 0, 0)),
                  pl.BlockSpec((1, 1, G_KV * TQ, LANES), lambda b, s, pt: (b, s, 0, 0))]
                 + [pl.BlockSpec((1, CPP, S_CMP, 2 * G_KV, HEAD_B), page_map(i)) for i in range(PGS)]
                 + [pl.BlockSpec((1, NN, 6 * GW), const),
                    pl.BlockSpec((1, n_win, 2 * GW), const),
                    wide(0), wide(1), wide(2), wide(3),
                    pl.BlockSpec((1, TQ, LANES), lambda b, s, pt: (b, 0, 4 * CB // LANES))],
        out_specs=wide(0),
        scratch_shapes=[pltpu.VMEM((G_KV, HG * TQ, 1), F32),
                        pltpu.VMEM((G_KV, HG * TQ, 1), F32),
                        pltpu.VMEM((G_KV, HG * TQ, HEAD_B), F32)],
    )
    return pl.pallas_call(
        functools.partial(_nsa_selwin_sample_kernel, PGS=PGS, PS=PS, HG=HG, TQ=TQ, pos0=pos0,
                          n_new=n_new, n_win=n_win),
        grid_spec=grid_spec,
        out_shape=jax.ShapeDtypeStruct((B, TQ, CB), F32),
        compiler_params=_cparams(("parallel", "arbitrary")),
        name="nsa_selwin_sample",
    )(table, slopes, proj, selm, selst, *([pool5] * PGS), new_rows, cwin, o_cmp, proj, proj, proj, proj)


def _rwkv_layer(h, x_prev, s0, i, W, B, T):
    N, D = h.shape
    g = W["norm_g"][i]
    rkvg = rwkv_in(h, x_prev, g, W["mu_a"][i], W["w_in_a"], i, T)
    CA = rkvg.shape[-1]
    lw, a = rwkv_lora(h, x_prev, g, W["mu_a"][i], W["w_lora_w1"][i], W["w_lora_w2"][i], W["a_lora1"][i],
                      W["a_lora2"][i], W["w0_a"][i], W["a0_a"][i], T)
    pvec = jnp.stack([W["k_k"][i], W["k_a"][i], W["r_k"][i].reshape(CA), W["ln_x_w"][i], W["ln_x_b"][i]])
    o, s_fin = rwkv_scan(rkvg.reshape(4, B, T, CA), lw.reshape(B, T, CA), a.reshape(B, T, CA), pvec, s0)
    last = rmsnorm(h.reshape(B, T, D)[:, -1], g)
    return o.reshape(N, CA), s_fin, last


def _nsa_layer(h, jb, shared, W, slopes, B, T, norm_g):
    N, D = h.shape
    CB = W["w_out_b"].shape[1]
    proj3 = norm_mm(h, norm_g, W["w_in_b"], (jb,)).reshape(B, T, -1)
    if shared["past"] is None:
        o_cmp, selm = nsa_cmp(proj3, shared["kvc"], slopes, tq=min(T, 256), nc=shared["nc"],
                              nsb=shared["nsb"], pos0=0)
        o = nsa_selwin_prompt(proj3, shared["rows"], selm, o_cmp, slopes, tq=min(T, 256))
    else:
        TQ = SUBLANES
        projp = jnp.pad(proj3, ((0, 0), (0, TQ - T), (0, 0)))
        o_cmp, selm = nsa_cmp(projp, shared["kvc"], slopes, tq=TQ, nc=shared["nc"],
                              nsb=shared["nsb"], pos0=shared["pos0"])
        pool, table, cwin = shared["past"]
        o = nsa_selwin_sample(projp, selm, o_cmp, pool, table, shared["new_rows"], cwin, slopes,
                              pos0=shared["pos0"], n_new=T)[:, :T]
    return o.reshape(N, CB)


def _trunk(x, p, pos0, wkv0, shift0, past, W, slopes):
    B, T, D = x.shape
    N = B * T
    depth = p.shape[0]
    n_a = W["w_in_a"].shape[0]
    GW = G_KV * HEAD_B
    h = x.reshape(N, D)
    wkv_new, shift_new = [], []
    shared, kv_rows, win_state = None, None, None
    for i in range(depth):
        if i < n_a:
            o, s_fin, last = _rwkv_layer(h, shift0[i], wkv0[i], i, W, B, T)
            wkv_new.append(s_fin)
            shift_new.append(last)
            h = out_ple(o, h, p[i].reshape(N, -1), W["w_out_a"], i, W["w_ple"], W["w_ple_gate"], i)
        else:
            o = _nsa_layer(h, i - n_a, shared, W, slopes, B, T, W["norm_g"][i])
            h = out_ple(o, h, p[i].reshape(N, -1), W["w_out_b"], i - n_a, W["w_ple"], W["w_ple_gate"], i)
        if i == n_a - 1:
            rows = norm_mm(h, W["kv_norm_g"], W["w_kv"]).reshape(B, T, 6 * GW)
            kv_rows = rows[:, :, :4 * GW].reshape(B, T, 4, G_KV, HEAD_B)
            win_new = rows[:, :, 4 * GW:].reshape(B, T, 2, G_KV, HEAD_B)
            if past is None:
                PS = 128
                pool = rows.reshape(B * T // PS, PS // S_CMP, S_CMP, 6 * G_KV, HEAD_B)
                table = jnp.arange(B * T // PS, dtype=jnp.int32).reshape(B, T // PS)
                t_all = T
                win_all = win_new
                shared = {"past": None, "rows": rows}
            else:
                pool, table, cwin = past
                PS = pool.shape[1] * S_CMP
                t_all = pos0 + T
                win_all = jnp.concatenate([cwin.reshape(B, -1, 2, G_KV, HEAD_B), win_new], axis=1)
                NN = LANES
                shared = {"past": past, "new_rows": jnp.pad(rows, ((0, 0), (0, NN - T), (0, 0)))}
            win_state = win_all[:, win_all.shape[1] - min(WINDOW, pos0 + T):]
            nc = (t_all - L_CMP) // S_CMP + 1
            assert nc < table.shape[1] * PS // S_CMP
            kvc = compress_kv(pool, table, W["pe_cmp"], W["w_cmp1"], W["w_cmp2"])
            shared.update(kvc=kvc, nc=nc, nsb=max(-(-t_all // L_SEL), TOPK_SEL), pos0=pos0)
    y = rmsnorm(h, W["final_norm_g"]).reshape(B, T, D)
    return y, jnp.stack(wkv_new), jnp.stack(shift_new), kv_rows, win_state


def kernel(x_prompt, x_sample, state_wkv, state_shift, cache_kv, cache_win_kv, page_table, p_prompt, p_sample, norm_g, mu_a, w_in_a, w_lora_w1, w_lora_w2, w0_a, a_lora1, a_lora2, a0_a, k_k, k_a, r_k, ln_x_w, ln_x_b, w_out_a, w_in_b, w_out_b, kv_norm_g, w_kv, pe_cmp, w_cmp1, w_cmp2, w_ple, w_ple_gate, final_norm_g):
    bf = lambda w: w.astype(BF16)
    CA = w_out_a.shape[1]
    W = dict(norm_g=norm_g, mu_a=mu_a, w_in_a=bf(w_in_a), w_lora_w1=bf(w_lora_w1), w_lora_w2=bf(w_lora_w2),
             w0_a=w0_a, a_lora1=bf(a_lora1), a_lora2=bf(a_lora2), a0_a=a0_a, k_k=k_k, k_a=k_a,
             r_k=r_k, ln_x_w=ln_x_w, ln_x_b=ln_x_b,
             w_out_a=bf(w_out_a), w_in_b=bf(w_in_b), w_out_b=bf(w_out_b), kv_norm_g=kv_norm_g, w_kv=bf(w_kv),
             pe_cmp=pe_cmp, w_cmp1=bf(w_cmp1), w_cmp2=bf(w_cmp2), w_ple=bf(w_ple), w_ple_gate=bf(w_ple_gate),
             final_norm_g=final_norm_g)
    HB = w_out_b.shape[1] // HEAD_B
    slopes = 2.0 ** (-8.0 * jnp.arange(1, HB + 1, dtype=F32) / HB)
    bp = x_prompt.shape[0]
    n_a = w_in_a.shape[0]
    D = x_prompt.shape[-1]
    wkv0 = jnp.zeros((n_a, bp, CA // HEAD_A, HEAD_A, HEAD_A), F32)
    shift0 = jnp.zeros((n_a, bp, D), F32)
    y_p, wkv_p, shift_p, kv_p, win_p = _trunk(x_prompt, p_prompt, 0, wkv0, shift0, None, W, slopes)
    db, n_pages = page_table.shape
    NP, PS = cache_kv.shape[:2]
    pool5 = cache_kv.reshape(NP, PS // S_CMP, S_CMP, -1, HEAD_B)
    past = (pool5, page_table, cache_win_kv.reshape(db, cache_win_kv.shape[1], -1))
    y_s, wkv_s, shift_s, kv_s, win_s = _trunk(x_sample, p_sample, n_pages * PS, state_wkv, state_shift, past, W, slopes)
    return (y_p, y_s, wkv_p, shift_p, kv_p, win_p, wkv_s, shift_s, kv_s, win_s)
```

```python
import functools
import math

import jax
import jax.numpy as jnp
from jax import lax
from jax.experimental import pallas as pl
from jax.experimental.pallas import tpu as pltpu

F32 = jnp.float32
BF16 = jnp.bfloat16

HEAD_A = 64
GN_EPS = 64e-5
HEAD_B = 128
G_KV = 4
L_CMP = 32
S_CMP = 16
L_SEL = 64
TOPK_SEL = 16
WINDOW = 512
RMS_EPS = 1e-6
NEG = -1e30
FORCE_BONUS = 1e4

LANES = 128
SUBLANES = 8
VMEM_LIMIT = 56 * 1024 * 1024

SCAN_NH = 2
SCAN_C = 64
SCAN_GP = 16
SCAN_CPS = 2

ROW_TILE = 1024
COL_TILE = 1024
COL_TILE_K4 = 512
FEW_ROWS = 64
WIDE_COLS = 4096
ROW_CHUNK = 256
ATTN_TQ = 256
SEL_SEG = 512

NT_DIMS = (((1,), (1,)), ((), ()))
TN_DIMS = (((0,), (0,)), ((), ()))


def _cparams(sem):
    return pltpu.CompilerParams(dimension_semantics=sem, vmem_limit_bytes=VMEM_LIMIT)


def _bdot(a, b):
    return jnp.dot(a.astype(BF16), b.astype(BF16), preferred_element_type=F32)


def _bdot_nt(a, b):
    return lax.dot_general(a.astype(BF16), b.astype(BF16), NT_DIMS, preferred_element_type=F32)


def _bdot_tn(a, b):
    return lax.dot_general(a.astype(BF16), b.astype(BF16), TN_DIMS, preferred_element_type=F32)


def _rms_kernel(x_ref, g_ref, o_ref):
    x = x_ref[...]
    ms = jnp.mean(x * x, axis=-1, keepdims=True)
    o_ref[...] = x * lax.rsqrt(ms + RMS_EPS) * g_ref[...]


def rmsnorm(x, g):
    M, D = x.shape
    tm = min(M, ROW_CHUNK)
    return pl.pallas_call(
        _rms_kernel,
        grid=(pl.cdiv(M, tm),),
        in_specs=[pl.BlockSpec((tm, D), lambda i: (i, 0)),
                  pl.BlockSpec((1, D), lambda i: (0, 0))],
        out_specs=pl.BlockSpec((tm, D), lambda i: (i, 0)),
        out_shape=jax.ShapeDtypeStruct((M, D), F32),
        compiler_params=_cparams(("parallel",)),
        name="rmsnorm",
    )(x, g.reshape(1, D))


def _col_tile(n_rows, n_cols, tn):
    return tn if n_rows > FEW_ROWS else min(n_cols, WIDE_COLS)


def _norm_rows(x, g):
    return x * lax.rsqrt(jnp.mean(x * x, axis=-1, keepdims=True) + RMS_EPS) * g


def _sigmoid(x):
    return 0.5 + 0.5 * jnp.tanh(0.5 * x)


def _norm_and_shift(h_ref, hprev_ref, xprev_ref, g_ref, i, tm, T):
    g = g_ref[...]
    hn = _norm_rows(h_ref[...], g)
    prev_row = _norm_rows(hprev_ref[SUBLANES - 1:SUBLANES, :], g)
    row = lax.broadcasted_iota(jnp.int32, (tm, 1), 0)
    xs = jnp.where(row == 0, prev_row, pltpu.roll(hn, 1, axis=0))
    if T >= tm:
        assert T % tm == 0
        start = (i * tm) % T == 0
        xs = jnp.where((row == 0) & start, xprev_ref[pl.ds((i * tm) // T, 1), :], xs)
    else:
        assert tm % T == 0
        for bb in range(tm // T):
            xs = jnp.where(row == bb * T, xprev_ref[pl.ds(i * (tm // T) + bb, 1), :], xs)
    return hn, xs


def _rwkv_in_kernel(h_ref, hprev_ref, xprev_ref, g_ref, mu_ref, w_ref, o_ref, hn_scr, xs_scr, *, tm, T):
    i, j, n = pl.program_id(0), pl.program_id(1), pl.program_id(2)
    rc = min(tm, ROW_CHUNK)
    chunks = [slice(c * rc, (c + 1) * rc) for c in range(tm // rc)]
    P = SUBLANES

    @pl.when((j == 0) & (n == 0))
    def _():
        g = g_ref[...]
        prev = _norm_rows(hprev_ref[P - 1:P, :], g)
        if T >= tm:
            assert T % tm == 0
            prev = jnp.where((i * tm) % T == 0, xprev_ref[pl.ds((i * tm) // T, 1), :], prev)
        hn_scr[0:P, :] = jnp.broadcast_to(prev, (P, prev.shape[1]))
        for ch in chunks:
            hn_scr[P + ch.start:P + ch.stop, :] = _norm_rows(h_ref[ch, :], g)

    @pl.when(n == 0)
    def _():
        mu = mu_ref[pl.ds(j, 1), :]
        for ch in chunks:
            hn = hn_scr[P + ch.start:P + ch.stop, :]
            xs = hn_scr[P - 1 + ch.start:P - 1 + ch.stop, :]
            if T < tm:
                assert tm % T == 0 and len(chunks) == 1
                row = lax.broadcasted_iota(jnp.int32, (tm, 1), 0)
                for bb in range(tm // T):
                    xs = jnp.where(row == bb * T, xprev_ref[pl.ds(i * (tm // T) + bb, 1), :], xs)
            xs_scr[ch, :] = (hn + (xs - hn) * mu).astype(BF16)

    o_ref[...] = jnp.dot(xs_scr[...], w_ref[...], preferred_element_type=F32)


def _shift_specs(tm, D, nb, ngrid):
    z = (0,) * (ngrid - 1)
    wrap = lambda f: (lambda i, *_: f(i))
    return [pl.BlockSpec((tm, D), wrap(lambda i: (i, 0))),
            pl.BlockSpec((SUBLANES, D), wrap(lambda i: (jnp.maximum(i * (tm // SUBLANES) - 1, 0), 0))),
            pl.BlockSpec((nb, D), wrap(lambda i: (0, 0))),
            pl.BlockSpec((1, D), wrap(lambda i: (0, 0))),
            pl.BlockSpec((6, D), wrap(lambda i: (0, 0)))]


def rwkv_in(h, xprev, g, mu, w, layer, T):
    N, D = h.shape
    C = w.shape[-1]
    tm = min(N, ROW_TILE)
    tn = _col_tile(N, C, COL_TILE)
    return pl.pallas_call(
        functools.partial(_rwkv_in_kernel, tm=tm, T=T),
        grid=(N // tm, 4, C // tn),
        in_specs=_shift_specs(tm, D, xprev.shape[0], 3) + [
            pl.BlockSpec((None, None, D, tn), lambda i, j, n: (layer, j, 0, n))],
        out_specs=pl.BlockSpec((None, tm, tn), lambda i, j, n: (j, i, n)),
        out_shape=jax.ShapeDtypeStruct((4, N, C), F32),
        scratch_shapes=[pltpu.VMEM((tm + SUBLANES, D), F32), pltpu.VMEM((tm, D), BF16)],
        compiler_params=_cparams(("parallel", "arbitrary", "arbitrary")),
        name="rwkv_in",
    )(h, h, xprev, g.reshape(1, D), mu, w)


def _rwkv_lora_kernel(h_ref, hprev_ref, xprev_ref, g_ref, mu_ref, lw1_ref, lw2_ref, la1_ref, la2_ref,
                      w0_ref, a0_ref, lw_ref, a_ref, *, tm, T):
    hn, xs = _norm_and_shift(h_ref, hprev_ref, xprev_ref, g_ref, pl.program_id(0), tm, T)
    dx = xs - hn
    x4 = (hn + dx * mu_ref[4:5, :]).astype(BF16)
    x5 = (hn + dx * mu_ref[5:6, :]).astype(BF16)
    t4 = jnp.tanh(jnp.dot(x4, lw1_ref[...], preferred_element_type=F32)).astype(BF16)
    x = w0_ref[...] + jnp.dot(t4, lw2_ref[...], preferred_element_type=F32)
    lw_ref[...] = -math.exp(-0.5) * _sigmoid(x)
    t5 = jnp.dot(x5, la1_ref[...], preferred_element_type=F32).astype(BF16)
    a_ref[...] = _sigmoid(a0_ref[...] + jnp.dot(t5, la2_ref[...], preferred_element_type=F32))


def rwkv_lora(h, xprev, g, mu, lw1, lw2, la1, la2, w0, a0, T):
    N, D = h.shape
    R, C = lw2.shape
    tm = min(N, ROW_CHUNK)
    full = lambda shape: pl.BlockSpec(shape, lambda i: (0,) * len(shape))
    o_spec = pl.BlockSpec((tm, C), lambda i: (i, 0))
    return pl.pallas_call(
        functools.partial(_rwkv_lora_kernel, tm=tm, T=T),
        grid=(N // tm,),
        in_specs=_shift_specs(tm, D, xprev.shape[0], 1) + [
            full((D, R)), full((R, C)), full((D, R)), full((R, C)), full((1, C)), full((1, C))],
        out_specs=[o_spec, o_spec],
        out_shape=[jax.ShapeDtypeStruct((N, C), F32)] * 2,
        compiler_params=_cparams(("parallel",)),
        name="rwkv_lora",
    )(h, h, xprev, g.reshape(1, D), mu, lw1, lw2, la1, la2, w0.reshape(1, C), a0.reshape(1, C))


def _norm_mm_kernel(h_ref, g_ref, w_ref, o_ref, xs_scr):
    @pl.when(pl.program_id(1) == 0)
    def _():
        rc = min(h_ref.shape[0], ROW_CHUNK)

        def norm_chunk(c, carry):
            rows = pl.ds(pl.multiple_of(c * rc, rc), rc)
            xs_scr[rows, :] = _norm_rows(h_ref[rows, :], g_ref[...]).astype(BF16)
            return carry

        lax.fori_loop(0, h_ref.shape[0] // rc, norm_chunk, 0)

    o_ref[...] = jnp.dot(xs_scr[...], w_ref[...], preferred_element_type=F32)


def norm_mm(h, g, w, widx=()):
    N, D = h.shape
    NO = w.shape[-1]
    tm = min(N, ROW_TILE)
    tn = _col_tile(N, NO, COL_TILE)
    nlead = len(widx)
    return pl.pallas_call(
        _norm_mm_kernel,
        grid=(N // tm, pl.cdiv(NO, tn)),
        in_specs=[pl.BlockSpec((tm, D), lambda i, n: (i, 0)),
                  pl.BlockSpec((1, D), lambda i, n: (0, 0)),
                  pl.BlockSpec((None,) * nlead + (D, tn), lambda i, n: tuple(widx) + (0, n))],
        out_specs=pl.BlockSpec((tm, tn), lambda i, n: (i, n)),
        out_shape=jax.ShapeDtypeStruct((N, NO), F32),
        scratch_shapes=[pltpu.VMEM((tm, D), BF16)],
        compiler_params=_cparams(("parallel", "arbitrary")),
        name="norm_mm",
    )(h, g.reshape(1, D), w)


def _mm_res_kernel(x_ref, h_ref, w_ref, o_ref, ob_ref):
    h1 = h_ref[...] + jnp.dot(x_ref[...].astype(BF16), w_ref[...], preferred_element_type=F32)
    o_ref[...] = h1
    ob_ref[...] = h1.astype(BF16)


def _ple_gate_kernel(h1b_ref, h1_ref, p_ref, wp_ref, wg_ref, o_ref):
    gate = jnp.dot(h1b_ref[...], wg_ref[...], preferred_element_type=F32)
    ple = jnp.dot(p_ref[...].astype(BF16), wp_ref[...], preferred_element_type=F32)
    o_ref[...] = h1_ref[...] + ple * _sigmoid(gate)


def out_ple(x, h, p, w_out, oidx, w_ple, w_gate, layer):
    N, C = x.shape
    D = h.shape[1]
    DP = p.shape[1]
    tm = min(N, ROW_TILE)
    tn = _col_tile(N, D, COL_TILE_K4)
    tile = pl.BlockSpec((tm, tn), lambda i, n: (i, n))
    h1, h1b = pl.pallas_call(
        _mm_res_kernel,
        grid=(N // tm, D // tn),
        in_specs=[pl.BlockSpec((tm, C), lambda i, n: (i, 0)), tile,
                  pl.BlockSpec((None, C, tn), lambda i, n: (oidx, 0, n))],
        out_specs=[tile, tile],
        out_shape=[jax.ShapeDtypeStruct((N, D), F32), jax.ShapeDtypeStruct((N, D), BF16)],
        compiler_params=_cparams(("parallel", "parallel")),
        name="mm_res",
    )(x, h, w_out)
    tg = _col_tile(N, D, COL_TILE)
    gtile = pl.BlockSpec((tm, tg), lambda i, n: (i, n))
    return pl.pallas_call(
        _ple_gate_kernel,
        grid=(N // tm, D // tg),
        in_specs=[pl.BlockSpec((tm, D), lambda i, n: (i, 0)), gtile,
                  pl.BlockSpec((tm, DP), lambda i, n: (i, 0)),
                  pl.BlockSpec((None, DP, tg), lambda i, n: (layer, 0, n)),
                  pl.BlockSpec((None, D, tg), lambda i, n: (layer, 0, n))],
        out_specs=gtile,
        out_shape=jax.ShapeDtypeStruct((N, D), F32),
        compiler_params=_cparams(("parallel", "parallel")),
        name="ple_gate",
    )(h1b, h1, p, w_ple, w_gate)


def _scan_kernel(r_ref, k_ref, v_ref, zg_ref, lw_ref, a_ref, pv_ref, s0_ref, o_ref, sfin_ref, s_scr, *, C, NH, GP, CPS):
    L = NH * HEAD_A
    NC = NH * C
    ci = pl.program_id(2)

    @pl.when(ci == 0)
    def _():
        for gp in range(GP):
            rows = []
            for hh in range(NH):
                pieces = [s0_ref[0, gp * NH + hh] if h2 == hh else jnp.zeros((HEAD_A, HEAD_A), F32)
                          for h2 in range(NH)]
                rows.append(jnp.concatenate(pieces, axis=1))
            s_scr[gp] = jnp.concatenate(rows, axis=0)

    row_c = lax.broadcasted_iota(jnp.int32, (C, NC), 0)
    col_s = lax.broadcasted_iota(jnp.int32, (C, NC), 1) % C
    tri_strict = col_s < row_c
    tri_incl = (lax.broadcasted_iota(jnp.int32, (C, 2 * NC), 1) % C
                <= lax.broadcasted_iota(jnp.int32, (C, 2 * NC), 0))
    st_mask = (lax.broadcasted_iota(jnp.int32, (NC, L), 0) // C
               == lax.broadcasted_iota(jnp.int32, (NC, L), 1) // HEAD_A)
    bd_mask = (lax.broadcasted_iota(jnp.int32, (NC, NC), 0) // C
               == lax.broadcasted_iota(jnp.int32, (NC, NC), 1) // C)
    head_mask = (lax.broadcasted_iota(jnp.int32, (L, L), 0) // HEAD_A
                 == lax.broadcasted_iota(jnp.int32, (L, L), 1) // HEAD_A)

    def st(x):
        return jnp.where(st_mask, jnp.concatenate([x] * NH, axis=0), 0.0)

    def bd(w):
        return jnp.where(bd_mask, jnp.concatenate([w] * NH, axis=0), 0.0)

    n_double = int(math.log2(C))
    each = lambda f, *cols: [f(*xs) for xs in zip(*cols)]
    sls = [slice(gp * L, (gp + 1) * L) for gp in range(GP)]
    head_of_lane = lax.broadcasted_iota(jnp.int32, (1, L), 1) // HEAD_A

    def hsum(x):
        out = None
        for hh in range(NH):
            sh = jnp.sum(jnp.where(head_of_lane == hh, x, 0.0), axis=-1, keepdims=True)
            out = sh if out is None else jnp.where(head_of_lane == hh, sh, out)
        return out

    k_k, k_a, r_k, ln_w, ln_b = ([pv_ref[n:n + 1, sl] for sl in sls] for n in range(5))
    cum_mat = (lax.broadcasted_iota(jnp.int32, (C, C), 1)
               <= lax.broadcasted_iota(jnp.int32, (C, C), 0)).astype(BF16)
    for cc in range(CPS):
        rs = slice(cc * C, (cc + 1) * C)
        lw = [lw_ref[0, rs, sl] for sl in sls]
        a_sig = [a_ref[0, rs, sl] for sl in sls]
        r = [r_ref[0, rs, sl] for sl in sls]
        k_raw = [k_ref[0, rs, sl] for sl in sls]
        v = [v_ref[0, rs, sl] for sl in sls]
        kk = each(lambda x, w: x * w, k_raw, k_k)
        kk = each(lambda x: x * lax.rsqrt(jnp.maximum(hsum(x * x), 1e-24)), kk)
        k = each(lambda x, a, w: x * (1.0 + (a - 1.0) * w), k_raw, a_sig, k_a)
        lw_hi = each(lambda z: z.astype(BF16), lw)
        lw_lo = each(lambda z, hi: (z - hi.astype(F32)).astype(BF16), lw, lw_hi)
        cum2 = each(lambda hi, lo: jnp.dot(cum_mat, jnp.concatenate([hi, lo], axis=1), preferred_element_type=F32),
                    lw_hi, lw_lo)
        cum = [c2[:, :L] + c2[:, L:] for c2 in cum2]
        p_incl = each(jnp.exp, cum)
        p_inv = each(lambda z: jnp.exp(-z), cum)
        at = each(lambda x, c, w: -x * jnp.exp(c - w), kk, cum, lw)
        rt = each(lambda x, p: x * p, r, p_incl)
        bt = each(lambda x, a, p: x * a * p, kk, a_sig, p_inv)
        kt = each(lambda x, p: x * p, k, p_inv)
        S = [s_scr[gp] for gp in range(GP)]
        ar = each(lambda x, y: jnp.concatenate([x, y], axis=0), at, rt)
        bk_st = each(lambda x, y: jnp.concatenate([st(x), st(y)], axis=0), bt, kt)
        Gm = each(_bdot_nt, ar, bk_st)
        w_ab = [jnp.where(tri_strict, g[:C, :NC], 0.0) for g in Gm]
        tm = w_ab
        pw = each(lambda w: _bdot(w, bd(w)), w_ab)
        LH = each(_bdot_nt, ar, S)
        v_st = each(st, v)
        x = [lh[:C] + _bdot(jnp.where(tri_strict, g[:C, NC:], 0.0), vs) for lh, g, vs in zip(LH, Gm, v_st)]
        for it in range(1, n_double):
            if it < n_double - 1:
                both = each(lambda t, p: _bdot(p, jnp.concatenate([bd(t), bd(p)], axis=1)), tm, pw)
                tm = each(lambda t, p, b2: t + p + b2[:, :NC], tm, pw, both)
                pw = [b2[:, NC:] for b2 in both]
            else:
                tm = each(lambda t, p: t + p + _bdot(p, bd(t)), tm, pw)
        u = each(lambda xx, t: xx + _bdot(t, st(xx)), x, tm)
        o = [lh[C:] + _bdot(jnp.where(tri_incl, g[C:], 0.0), jnp.concatenate([st(uu), vs], axis=0))
             for lh, g, uu, vs in zip(LH, Gm, u, v_st)]
        inv_n = 1.0 / HEAD_A
        dev = each(lambda x: x - hsum(x) * inv_n, o)
        gn = each(lambda d, w, b_: d * lax.rsqrt(hsum(d * d) * inv_n + GN_EPS) * w + b_, dev, ln_w, ln_b)
        bonus = each(lambda rr, kx, w, vv: hsum(rr * kx * w) * vv, r, k, r_k, v)
        for sl, y, bo in zip(sls, gn, bonus):
            zg = zg_ref[0, rs, sl]
            o_ref[0, rs, sl] = ((y + bo) * (zg * _sigmoid(zg))).astype(o_ref.dtype)
        ds = [_bdot_tn(jnp.concatenate([uu, vv], axis=0), jnp.concatenate([b_, k_], axis=0))
              for uu, vv, b_, k_ in zip(u, v, bt, kt)]
        for gp in range(GP):
            s_scr[gp] = (S[gp] + jnp.where(head_mask, ds[gp], 0.0)) * p_incl[gp][C - 1:C, :]

    @pl.when(ci == pl.num_programs(2) - 1)
    def _():
        for gp in range(GP):
            s_all = s_scr[gp]
            for hh in range(NH):
                blk = slice(hh * HEAD_A, (hh + 1) * HEAD_A)
                sfin_ref[0, gp * NH + hh] = s_all[blk, blk]


def rwkv_scan(rkvg, lw, a, pvec, s0):
    _, B, T, CA = rkvg.shape
    H = CA // HEAD_A
    NH, C, GP = SCAN_NH, SCAN_C, SCAN_GP
    assert NH * C == LANES and H % (NH * GP) == 0
    L = NH * HEAD_A
    NG = H // NH
    CPS = SCAN_CPS if T >= SCAN_CPS * C else 1
    TS = CPS * C
    Tp = -(-T // TS) * TS
    if Tp != T:
        rkvg = jnp.pad(rkvg, ((0, 0), (0, 0), (0, Tp - T), (0, 0)))
        lw, a = (jnp.pad(z, ((0, 0), (0, Tp - T), (0, 0))) for z in (lw, a))
    seq_spec = pl.BlockSpec((1, TS, GP * L), lambda bi, gi, ci: (bi, ci, gi))
    proj_spec = lambda j: pl.BlockSpec((None, 1, TS, GP * L), lambda bi, gi, ci: (j, bi, ci, gi))
    st_spec = pl.BlockSpec((1, GP * NH, HEAD_A, HEAD_A), lambda bi, gi, ci: (bi, gi, 0, 0))
    o, s_fin = pl.pallas_call(
        functools.partial(_scan_kernel, C=C, NH=NH, GP=GP, CPS=CPS),
        grid=(B, NG // GP, Tp // TS),
        in_specs=[proj_spec(j) for j in range(4)] + [seq_spec, seq_spec,
                  pl.BlockSpec((5, GP * L), lambda bi, gi, ci: (0, gi)), st_spec],
        out_specs=[seq_spec, st_spec],
        out_shape=[jax.ShapeDtypeStruct((B, Tp, CA), BF16), jax.ShapeDtypeStruct((B, H, HEAD_A, HEAD_A), F32)],
        scratch_shapes=[pltpu.VMEM((GP, L, L), F32)],
        compiler_params=_cparams(("parallel", "parallel", "arbitrary")),
        name="rwkv_scan",
    )(rkvg, rkvg, rkvg, rkvg, lw, a, pvec, s0)
    return o[:, :T], s_fin


def _gelu_tanh(x):
    c = math.sqrt(2.0 / math.pi)
    return 0.5 * x * (1.0 + jnp.tanh(c * (x + 0.044715 * (x * x * x))))


def _compress_kernel(pt_ref, *refs, PGS):
    del pt_ref
    page_refs = refs[:PGS]
    next_ref, pe_ref, w1_ref, w2_ref, out_ref = refs[PGS:]
    CPP = page_refs[0].shape[1]
    NCH = PGS * CPP
    CG = 2 * G_KV
    M = (NCH + 1) * CG

    def rows_of(l, hf):
        pe = pe_ref[hf, l]
        parts = [(page_refs[i][0, :, l] + pe[None]).reshape(CPP * CG, HEAD_B) for i in range(PGS)]
        parts.append(next_ref[0, 0, l] + pe)
        return jnp.concatenate(parts, axis=0)

    top = jnp.zeros((M, 2 * HEAD_B), F32)
    bot = jnp.zeros((M, 2 * HEAD_B), F32)
    for l in range(0, S_CMP, 2):
        wrows = pl.ds(l * HEAD_B, 2 * HEAD_B)
        xt = jnp.concatenate([rows_of(l, 0), rows_of(l + 1, 0)], axis=1).astype(BF16)
        top = top + jnp.dot(xt, w1_ref[0, wrows, :], preferred_element_type=F32)
        xb = jnp.concatenate([rows_of(l, 1), rows_of(l + 1, 1)], axis=1).astype(BF16)
        bot = bot + jnp.dot(xb, w1_ref[1, wrows, :], preferred_element_type=F32)
    is_k = (lax.broadcasted_iota(jnp.int32, (M, 1), 0) % CG) < G_KV
    pick = lambda z, n: jnp.where(is_k[:n], z[:n, :HEAD_B], z[:n, HEAD_B:])
    hcur = pick(top, NCH * CG) + pick(bot, M)[CG:]
    o2 = jnp.dot(_gelu_tanh(hcur).astype(BF16), w2_ref[...], preferred_element_type=F32)
    out_ref[0] = pick(o2, NCH * CG).reshape(NCH, CG, HEAD_B)


def compress_kv(pool5, table, pe_cmp, w1, w2):
    NP, CPP = pool5.shape[:2]
    B, n_pages = table.shape
    PGS = max(d for d in (16, 8, 4, 2, 1) if n_pages % d == 0)
    NCH = PGS * CPP
    CG = 2 * G_KV
    half = S_CMP * HEAD_B
    pe_r = jnp.repeat(pe_cmp.reshape(2, 2, S_CMP, HEAD_B).transpose(1, 2, 0, 3), G_KV, axis=2)
    w1_r = w1.reshape(2, 2, half, HEAD_B).transpose(1, 2, 0, 3).reshape(2, half, 2 * HEAD_B)
    w2_r = jnp.concatenate([w2[0], w2[1]], axis=1)

    def page_map(i):
        return lambda b, s, pt: (pt[b, s * PGS + i], 0, 0, 0, 0)

    def next_map(b, s, pt):
        return (pt[b, jnp.minimum((s + 1) * PGS, n_pages - 1)], 0, 0, 0, 0)

    const = lambda n: (lambda b, s, pt: (0,) * n)
    grid_spec = pltpu.PrefetchScalarGridSpec(
        num_scalar_prefetch=1,
        grid=(B, n_pages // PGS),
        in_specs=[pl.BlockSpec((1, CPP, S_CMP, CG, HEAD_B), page_map(i)) for i in range(PGS)] + [
            pl.BlockSpec((1, 1, S_CMP, CG, HEAD_B), next_map),
            pl.BlockSpec((2, S_CMP, CG, HEAD_B), const(4)),
            pl.BlockSpec((2, half, 2 * HEAD_B), const(3)),
            pl.BlockSpec((HEAD_B, 2 * HEAD_B), const(2)),
        ],
        out_specs=pl.BlockSpec((1, NCH, CG, HEAD_B), lambda b, s, pt: (b, s, 0, 0)),
    )
    return pl.pallas_call(
        functools.partial(_compress_kernel, PGS=PGS),
        grid_spec=grid_spec,
        out_shape=jax.ShapeDtypeStruct((B, n_pages * CPP, CG, HEAD_B), F32),
        compiler_params=_cparams(("parallel", "arbitrary")),
        name="compress_kv",
    )(table, *([pool5] * PGS), pool5, pe_r, w1_r, w2_r)


def _stack_heads(q, HG):
    return jnp.concatenate([q[:, h * HEAD_B:(h + 1) * HEAD_B] for h in range(HG)], axis=0)


def _masked_softmax_rows(s, mask):
    s = jnp.where(mask, s, NEG)
    m = jnp.max(s, axis=-1, keepdims=True)
    e = jnp.where(mask, jnp.exp(s - m), 0.0)
    l = jnp.sum(e, axis=-1, keepdims=True)
    return e / jnp.where(l > 0.0, l, 1.0)


def _attend_stacked(s, dist, mask, v, slopes_ref, g, HG, tq):
    ps = []
    psum = jnp.zeros(dist.shape, F32)
    for h in range(HG):
        p = _masked_softmax_rows(s[h * tq:(h + 1) * tq] - slopes_ref[g * HG + h] * dist, mask)
        psum = psum + p
        ps.append(p.astype(BF16))
    o = jnp.dot(jnp.concatenate(ps, axis=0), v.astype(BF16), preferred_element_type=F32)
    return o, psum


def _unstack_store(o_ref, o, HG, tq):
    for h in range(HG):
        o_ref[0, :, h * HEAD_B:(h + 1) * HEAD_B] = o[h * tq:(h + 1) * tq]


def _nsa_cmp_kernel(slopes_ref, q_ref, kc_ref, vc_ref, o_ref, selm_ref, *, tq, HG, nc, nsb, pos0):
    g = pl.program_id(1)
    qt = pl.program_id(2)
    NCp = kc_ref.shape[1]
    NSBp = selm_ref.shape[3]
    q_st = _stack_heads(q_ref[0] * (HEAD_B ** -0.5), HG)
    s = _bdot_nt(q_st, kc_ref[0])
    qpos = pos0 + qt * tq + lax.broadcasted_iota(jnp.int32, (tq, 1), 0)
    cidx = lax.broadcasted_iota(jnp.int32, (1, NCp), 1)
    cend = S_CMP * cidx + (L_CMP - 1)
    mask = (cend <= qpos) & (cidx < nc)
    dist = (qpos - cend).astype(F32)
    o, imp_c = _attend_stacked(s, dist, mask, vc_ref[0], slopes_ref, g, HG, tq)
    _unstack_store(o_ref, o, HG, tq)

    crow = lax.broadcasted_iota(jnp.int32, (NCp, NSBp), 0)
    jcol = lax.broadcasted_iota(jnp.int32, (NCp, NSBp), 1)
    overlap = ((S_CMP * crow < L_SEL * (jcol + 1)) & (S_CMP * crow + L_CMP > L_SEL * jcol)
               & (crow < nc)).astype(F32)
    imp = jnp.dot(imp_c, overlap, precision=lax.Precision.HIGHEST, preferred_element_type=F32)
    lane = lax.broadcasted_iota(jnp.int32, (tq, NSBp), 1)
    cur = jnp.right_shift(qpos, int(math.log2(L_SEL)))
    forced = ((lane == 0) | (lane == cur) | (lane == cur - 1)).astype(F32)
    score = jnp.where(lane <= cur, imp + FORCE_BONUS * forced, NEG)
    score = jnp.where(lane < nsb, score, -3e38)

    if tq % LANES == 0 and NSBp == LANES:
        nr = -(-nsb // SUBLANES) * SUBLANES
        st = score.T[:nr]
        sub = lax.broadcasted_iota(jnp.int32, (nr, tq), 0)
        cnt = jnp.zeros((nr, tq), F32)
        for i in range(nsb):
            row = st[i:i + 1, :]
            beats = (row > st) | ((row == st) & (sub > i))
            cnt = cnt + jnp.where(beats, 1.0, 0.0)
        sel_t = jnp.where((cnt < TOPK_SEL) & (st > 0.5 * NEG), 1.0, 0.0)
        selm_ref[0, 0] = jnp.concatenate([sel_t, jnp.zeros((NSBp - nr, tq), F32)], axis=0).T
    else:
        cnt = jnp.zeros((tq, NSBp), F32)
        for i in range(nsb):
            col = score[:, i:i + 1]
            beats = (col > score) | ((col == score) & (lane > i))
            cnt = cnt + jnp.where(beats, 1.0, 0.0)
        sel = (cnt < TOPK_SEL) & (score > 0.5 * NEG)
        selm_ref[0, 0] = sel.astype(F32)


def nsa_cmp(proj, kvc, slopes, *, tq, nc, nsb, pos0):
    B, T, _ = proj.shape
    HG = slopes.shape[0] // G_KV
    NCp = kvc.shape[1]
    NSBp = -(-nsb // LANES) * LANES
    gw = HG * HEAD_B
    return pl.pallas_call(
        functools.partial(_nsa_cmp_kernel, tq=tq, HG=HG, nc=nc, nsb=nsb, pos0=pos0),
        grid=(B, G_KV, T // tq),
        in_specs=[pl.BlockSpec(memory_space=pltpu.SMEM),
                  pl.BlockSpec((1, tq, gw), lambda b, g, t: (b, t, g)),
                  pl.BlockSpec((1, NCp, HEAD_B), lambda b, g, t: (b, 0, g)),
                  pl.BlockSpec((1, NCp, HEAD_B), lambda b, g, t: (b, 0, G_KV + g))],
        out_specs=[pl.BlockSpec((1, tq, gw), lambda b, g, t: (b, t, g)),
                   pl.BlockSpec((1, 1, tq, NSBp), lambda b, g, t: (b, g, t, 0))],
        out_shape=[jax.ShapeDtypeStruct((B, T, G_KV * gw), F32),
                   jax.ShapeDtypeStruct((B, G_KV, T, NSBp), F32)],
        compiler_params=_cparams(("parallel", "parallel", "parallel")),
        name="nsa_cmp",
    )(slopes, proj, kvc.reshape(B, NCp, -1), kvc.reshape(B, NCp, -1))


LOG2E = 1.4426950408889634


def _bf16_part(x):
    return x.astype(BF16).astype(F32)


def _alibi_lhs(q, slope_col):
    c = slope_col * LOG2E
    c1 = _bf16_part(c)
    c2 = _bf16_part(c - c1)
    c3 = _bf16_part(c - c1 - c2)
    lane = lax.broadcasted_iota(jnp.int32, q.shape, 1)
    extra = jnp.where((lane == 0) | (lane == 3), c1,
                      jnp.where((lane == 1) | (lane == 4), c2, jnp.where((lane == 2) | (lane == 5), c3, 0.0)))
    return jnp.concatenate([q, extra], axis=1).astype(BF16)


def _alibi_rhs(k, k0):
    pos = k0 + lax.broadcasted_iota(jnp.int32, k.shape, 0)
    lane = lax.broadcasted_iota(jnp.int32, k.shape, 1)
    hi = jnp.bitwise_and(pos, -L_SEL)
    extra = jnp.where(lane < 3, hi, jnp.where(lane < 6, pos - hi, 0)).astype(F32)
    return jnp.concatenate([k, extra], axis=1).astype(BF16)


def _with_ones(v):
    lane = lax.broadcasted_iota(jnp.int32, v.shape, 1)
    return jnp.concatenate([v, jnp.where(lane == 0, 1.0, 0.0)], axis=1).astype(BF16)


def _silu(x):
    return x * _sigmoid(x)


def _nsa_selwin_prompt_kernel(slopes_ref, q_ref, selm_ref, ks_ref, vs_ref, kw_ref, vw_ref,
                              ocmp_ref, zc_ref, zs_ref, zw_ref, gate_ref, o_ref, s_scr, *, tq, HG, T, WS, SEG):
    g = pl.program_id(1)
    qt = pl.program_id(2)
    NSBp = selm_ref.shape[3]
    R = HG * tq
    qpos = qt * tq + lax.broadcasted_iota(jnp.int32, (tq, 1), 0)
    slope_col = jnp.concatenate([jnp.full((tq, 1), slopes_ref[g * HG + h], F32) for h in range(HG)], axis=0)
    tile_heads = lambda x: jnp.concatenate([x] * HG, axis=0)
    q2 = _alibi_lhs(_stack_heads(q_ref[0] * (HEAD_B ** -0.5 * LOG2E), HG), slope_col)

    selm_b = selm_ref[0, 0].astype(BF16)
    nseg = (qt * tq + tq + SEG - 1) // SEG

    def seg_scores(si, m):
        k0 = pl.multiple_of(si * SEG, SEG)
        kcol = k0 + lax.broadcasted_iota(jnp.int32, (NSBp, SEG), 1)
        expand = (jnp.right_shift(kcol, int(math.log2(L_SEL)))
                  == lax.broadcasted_iota(jnp.int32, (NSBp, SEG), 0)).astype(BF16)
        in_blk = jnp.dot(selm_b, expand, preferred_element_type=F32) > 0.5
        kpos = k0 + lax.broadcasted_iota(jnp.int32, (1, SEG), 1)
        mask = tile_heads(in_blk & (kpos <= qpos))
        s = lax.dot_general(q2, _alibi_rhs(ks_ref[0, pl.ds(k0, SEG), :], k0), NT_DIMS, preferred_element_type=F32)
        s = jnp.where(mask, s, NEG)
        s_scr[si] = s
        return jnp.maximum(m, jnp.max(s, axis=-1, keepdims=True))

    m = lax.fori_loop(0, nseg, seg_scores, jnp.full((R, 1), NEG, F32))

    def seg_pv(si, acc):
        k0 = pl.multiple_of(si * SEG, SEG)
        e = jnp.exp2(s_scr[si] - m).astype(BF16)
        return acc + jnp.dot(e, _with_ones(vs_ref[0, pl.ds(k0, SEG), :]), preferred_element_type=F32)

    acc = lax.fori_loop(0, nseg, seg_pv, jnp.zeros((R, 2 * HEAD_B), F32))
    o = acc[:, :HEAD_B] / acc[:, HEAD_B:HEAD_B + 1]

    start = pl.multiple_of(jnp.clip(qt * tq - WINDOW, 0, T - WS), SUBLANES)
    distw = qpos - (start + lax.broadcasted_iota(jnp.int32, (1, WS), 1))
    maskw = tile_heads((distw >= 0) & (distw < WINDOW))
    sw = lax.dot_general(q2, _alibi_rhs(kw_ref[0, pl.ds(start, WS), :], start), NT_DIMS, preferred_element_type=F32)
    sw = jnp.where(maskw, sw, NEG)
    ew = jnp.exp2(sw - jnp.max(sw, axis=-1, keepdims=True)).astype(BF16)
    accw = jnp.dot(ew, _with_ones(vw_ref[0, pl.ds(start, WS), :]), preferred_element_type=F32)
    ow = accw[:, :HEAD_B] / accw[:, HEAD_B:HEAD_B + 1]

    HB = G_KV * HG
    gates = _sigmoid(gate_ref[0])
    lane = lax.broadcasted_iota(jnp.int32, gates.shape, 1)
    gate_col = lambda idx: jnp.sum(jnp.where(lane == idx, gates, 0.0), axis=1, keepdims=True)
    for h in range(HG):
        hs = slice(h * HEAD_B, (h + 1) * HEAD_B)
        rs = slice(h * tq, (h + 1) * tq)
        hd = g * HG + h
        y = (gate_col(hd) * ocmp_ref[0, :, hs] * _silu(zc_ref[0, :, hs])
             + gate_col(HB + hd) * o[rs] * _silu(zs_ref[0, :, hs])
             + gate_col(2 * HB + hd) * ow[rs] * _silu(zw_ref[0, :, hs]))
        o_ref[0, :, hs] = y.astype(o_ref.dtype)


def nsa_selwin_prompt(proj, rows, selm, o_cmp, slopes, *, tq):
    B, T, _ = proj.shape
    HG = slopes.shape[0] // G_KV
    NSBp = selm.shape[3]
    gw = HG * HEAD_B
    CB = G_KV * gw
    WS = min(T, WINDOW + tq)
    SEG = min(T, SEL_SEG)
    assert T % SEG == 0
    kv_spec = lambda c: pl.BlockSpec((1, T, HEAD_B), lambda b, g, t: (b, 0, c * G_KV + g))
    head_spec = lambda blk: pl.BlockSpec((1, tq, gw), lambda b, g, t: (b, t, blk * G_KV + g))
    return pl.pallas_call(
        functools.partial(_nsa_selwin_prompt_kernel, tq=tq, HG=HG, T=T, WS=WS, SEG=SEG),
        grid=(B, G_KV, T // tq),
        in_specs=[pl.BlockSpec(memory_space=pltpu.SMEM),
                  head_spec(0),
                  pl.BlockSpec((1, 1, tq, NSBp), lambda b, g, t: (b, g, t, 0)),
                  kv_spec(2), kv_spec(3), kv_spec(4), kv_spec(5),
                  head_spec(0), head_spec(1), head_spec(2), head_spec(3),
                  pl.BlockSpec((1, tq, LANES), lambda b, g, t: (b, t, 4 * CB // LANES))],
        out_specs=head_spec(0),
        out_shape=jax.ShapeDtypeStruct((B, T, CB), BF16),
        scratch_shapes=[pltpu.VMEM((T // SEG, HG * tq, SEG), F32)],
        compiler_params=_cparams(("parallel", "parallel", "parallel")),
        name="nsa_selwin_prompt",
    )(slopes, proj, selm, rows, rows, rows, rows, o_cmp, proj, proj, proj, proj)


def _nsa_selwin_sample_kernel(pt_ref, slopes_ref, q_ref, selm_ref, selst_ref, *refs, PGS, PS, HG, TQ, pos0, n_new, n_win):
    del pt_ref
    page_refs = refs[:PGS]
    (new_ref, cwin_ref, ocmp_ref, zc_ref, zs_ref, zw_ref, gate_ref, o_ref,
     m_scr, l_scr, acc_scr) = refs[PGS:]
    st = pl.program_id(1)
    NSBp = selm_ref.shape[3]
    GW = G_KV * HEAD_B
    sel_shift = int(math.log2(L_SEL))

    @pl.when(st == 0)
    def _():
        m_scr[...] = jnp.full(m_scr.shape, NEG, F32)
        l_scr[...] = jnp.zeros(l_scr.shape, F32)
        acc_scr[...] = jnp.zeros(acc_scr.shape, F32)

    qpos = pos0 + lax.broadcasted_iota(jnp.int32, (TQ, 1), 0)
    lane_j = lax.broadcasted_iota(jnp.int32, (TQ, NSBp), 1)
    tile_heads = lambda x: jnp.concatenate([x] * HG, axis=0)

    def sel_col(selm_g, j):
        return jnp.sum(jnp.where(lane_j == j, selm_g, 0.0), axis=1, keepdims=True)

    def online_update(s, mask, v):
        gs = range(G_KV)
        m_old = [m_scr[g] for g in gs]
        m_new = [jnp.maximum(m_old[g], jnp.max(jnp.where(mask[g], s[g], NEG), axis=-1, keepdims=True)) for g in gs]
        e = [jnp.where(mask[g], jnp.exp(s[g] - m_new[g]), 0.0) for g in gs]
        alpha = [jnp.exp(m_old[g] - m_new[g]) for g in gs]
        pv = [_bdot(e[g], v[g]) for g in gs]
        for g in gs:
            l_scr[g] = alpha[g] * l_scr[g] + jnp.sum(e[g], axis=-1, keepdims=True)
            acc_scr[g] = alpha[g] * acc_scr[g] + pv[g]
            m_scr[g] = m_new[g]

    NK = PGS * PS
    kpos = st * NK + lax.broadcasted_iota(jnp.int32, (1, NK), 1)
    expand = (jnp.right_shift(lax.broadcasted_iota(jnp.int32, (LANES, NK), 1), sel_shift)
              == lax.broadcasted_iota(jnp.int32, (LANES, NK), 0)).astype(BF16)
    in_blk_all = jnp.dot(selst_ref[0, 0].astype(BF16), expand, preferred_element_type=F32)
    dist = qpos - kpos
    distf = tile_heads(dist.astype(F32))
    gs = range(G_KV)
    q_st = [_stack_heads(q_ref[0, :, g * HG * HEAD_B:(g + 1) * HG * HEAD_B] * (HEAD_B ** -0.5), HG).astype(BF16)
            for g in gs]
    slope_col = [jnp.concatenate([jnp.full((TQ, 1), slopes_ref[g * HG + h], F32) for h in range(HG)], axis=0)
                 for g in gs]
    slabs = [jnp.swapaxes(page_refs[i][0].reshape(PS, 2 * G_KV, HEAD_B), 0, 1) for i in range(PGS)]
    k = [jnp.concatenate([slabs[i][g] for i in range(PGS)], axis=0) for g in gs]
    v = [jnp.concatenate([slabs[i][G_KV + g] for i in range(PGS)], axis=0) for g in gs]
    mask = [tile_heads((in_blk_all[g * TQ:(g + 1) * TQ] > 0.5) & (dist >= 0)) for g in gs]
    s = [_bdot_nt(q_st[g], k[g]) - slope_col[g] * distf for g in gs]
    online_update(s, mask, v)

    @pl.when(st == pl.num_programs(1) - 1)
    def _():
        NN = new_ref.shape[1]
        rnew = lax.broadcasted_iota(jnp.int32, (1, NN), 1)
        kpos_n = pos0 + rnew
        dist_n = qpos - kpos_n
        ok_n = (rnew < n_new) & (dist_n >= 0)
        jn = pos0 >> sel_shift
        kpos_w = pos0 - n_win + lax.broadcasted_iota(jnp.int32, (1, n_win), 1)
        dist_w = qpos - kpos_w
        dist_wall = jnp.concatenate([dist_w, dist_n], axis=1)
        mask_wall = jnp.concatenate([(dist_w >= 0) & (dist_w < WINDOW), ok_n & (dist_n < WINDOW)], axis=1)
        kn = [new_ref[0, :, 2 * GW + g * HEAD_B:2 * GW + (g + 1) * HEAD_B] for g in gs]
        vn = [new_ref[0, :, 3 * GW + g * HEAD_B:3 * GW + (g + 1) * HEAD_B] for g in gs]
        mask_n = [tile_heads((sel_col(selm_ref[0, g], jn) > 0.5) & ok_n) for g in gs]
        dist_nf = tile_heads(dist_n.astype(F32))
        s_n = [_bdot_nt(q_st[g], kn[g]) - slope_col[g] * dist_nf for g in gs]
        online_update(s_n, mask_n, vn)
        kw = [jnp.concatenate([cwin_ref[0, :, g * HEAD_B:(g + 1) * HEAD_B],
                               new_ref[0, :, 4 * GW + g * HEAD_B:4 * GW + (g + 1) * HEAD_B]], axis=0) for g in gs]
        vw = [jnp.concatenate([cwin_ref[0, :, GW + g * HEAD_B:GW + (g + 1) * HEAD_B],
                               new_ref[0, :, 5 * GW + g * HEAD_B:5 * GW + (g + 1) * HEAD_B]], axis=0) for g in gs]
        dist_wf = tile_heads(dist_wall.astype(F32))
        mask_w = tile_heads(mask_wall)
        sw = [_bdot_nt(q_st[g], kw[g]) - slope_col[g] * dist_wf for g in gs]
        pw = [_masked_softmax_rows(sw[g], mask_w) for g in gs]
        ow = [_bdot(pw[g], vw[g]) for g in gs]
        HB = G_KV * HG
        gates = _sigmoid(gate_ref[0])
        for g in gs:
            l = l_scr[g]
            o = acc_scr[g] / jnp.where(l > 0.0, l, 1.0)
            for h in range(HG):
                hd = g * HG + h
                hs = slice(hd * HEAD_B, (hd + 1) * HEAD_B)
                rs = slice(h * TQ, (h + 1) * TQ)
                y = (gates[:, hd:hd + 1] * ocmp_ref[0, :, hs] * _silu(zc_ref[0, :, hs])
                     + gates[:, HB + hd:HB + hd + 1] * o[rs] * _silu(zs_ref[0, :, hs])
                     + gates[:, 2 * HB + hd:2 * HB + hd + 1] * ow[g][rs] * _silu(zw_ref[0, :, hs]))
                o_ref[0, :, hs] = y.astype(o_ref.dtype)


def nsa_selwin_sample(proj, selm, o_cmp, pool5, table, new_rows, cwin, slopes, *, pos0, n_new):
    B, TQ, _ = proj.shape
    HG = slopes.shape[0] // G_KV
    NP, CPP = pool5.shape[:2]
    PS = CPP * S_CMP
    n_pages = table.shape[1]
    NSBp = selm.shape[3]
    PGS = max(d for d in (8, 4, 2, 1) if n_pages % d == 0)
    CB = G_KV * HG * HEAD_B
    GW = G_KV * HEAD_B
    NN = new_rows.shape[1]
    n_win = cwin.shape[1]
    assert pos0 % L_SEL == 0 and n_new <= L_SEL and pos0 == n_pages * PS

    def page_map(i):
        return lambda b, s, pt: (pt[b, s * PGS + i], 0, 0, 1, 0)

    n_steps = n_pages // PGS
    bps = PGS * PS // L_SEL
    assert bps <= LANES
    selst = selm[:, :, :, :n_steps * bps].reshape(B, G_KV, TQ, n_steps, bps).transpose(0, 3, 1, 2, 4)
    selst = jnp.pad(selst.reshape(B, n_steps, G_KV * TQ, bps), ((0, 0), (0, 0), (0, 0), (0, LANES - bps)))

    const = lambda b, s, pt: (b, 0, 0)
    wide = lambda blk: pl.BlockSpec((1, TQ, CB), lambda b, s, pt: (b, 0, blk))
    grid_spec = pltpu.PrefetchScalarGridSpec(
        num_scalar_prefetch=1,
        grid=(B, n_steps),
        in_specs=[pl.BlockSpec(memory_space=pltpu.SMEM),
                  pl.BlockSpec((1, TQ, CB), const),
                  pl.BlockSpec((1, G_KV, TQ, NSBp), lambda b, s, pt: (b, 0, 0, 0)),
                  pl.BlockSpec((1, 1, G_KV * TQ, LANES), lambda b, s, pt: (b, s, 0, 0))]
                 + [pl.BlockSpec((1, CPP, S_CMP, 2 * G_KV, HEAD_B), page_map(i)) for i in range(PGS)]
                 + [pl.BlockSpec((1, NN, 6 * GW), const),
                    pl.BlockSpec((1, n_win, 2 * GW), const),
                    wide(0), wide(1), wide(2), wide(3),
                    pl.BlockSpec((1, TQ, LANES), lambda b, s, pt: (b, 0, 4 * CB // LANES))],
        out_specs=wide(0),
        scratch_shapes=[pltpu.VMEM((G_KV, HG * TQ, 1), F32),
                        pltpu.VMEM((G_KV, HG * TQ, 1), F32),
                        pltpu.VMEM((G_KV, HG * TQ, HEAD_B), F32)],
    )
    return pl.pallas_call(
        functools.partial(_nsa_selwin_sample_kernel, PGS=PGS, PS=PS, HG=HG, TQ=TQ, pos0=pos0,
                          n_new=n_new, n_win=n_win),
        grid_spec=grid_spec,
        out_shape=jax.ShapeDtypeStruct((B, TQ, CB), F32),
        compiler_params=_cparams(("parallel", "arbitrary")),
        name="nsa_selwin_sample",
    )(table, slopes, proj, selm, selst, *([pool5] * PGS), new_rows, cwin, o_cmp, proj, proj, proj, proj)


def _rwkv_layer(h, x_prev, s0, i, W, B, T):
    N, D = h.shape
    g = W["norm_g"][i]
    rkvg = rwkv_in(h, x_prev, g, W["mu_a"][i], W["w_in_a"], i, T)
    CA = rkvg.shape[-1]
    lw, a = rwkv_lora(h, x_prev, g, W["mu_a"][i], W["w_lora_w1"][i], W["w_lora_w2"][i], W["a_lora1"][i],
                      W["a_lora2"][i], W["w0_a"][i], W["a0_a"][i], T)
    pvec = jnp.stack([W["k_k"][i], W["k_a"][i], W["r_k"][i].reshape(CA), W["ln_x_w"][i], W["ln_x_b"][i]])
    o, s_fin = rwkv_scan(rkvg.reshape(4, B, T, CA), lw.reshape(B, T, CA), a.reshape(B, T, CA), pvec, s0)
    last = rmsnorm(h.reshape(B, T, D)[:, -1], g)
    return o.reshape(N, CA), s_fin, last


def _nsa_layer(h, jb, shared, W, slopes, B, T, norm_g):
    N, D = h.shape
    CB = W["w_out_b"].shape[1]
    proj3 = norm_mm(h, norm_g, W["w_in_b"], (jb,)).reshape(B, T, -1)
    if shared["past"] is None:
        o_cmp, selm = nsa_cmp(proj3, shared["kvc"], slopes, tq=min(T, ATTN_TQ), nc=shared["nc"],
                              nsb=shared["nsb"], pos0=0)
        o = nsa_selwin_prompt(proj3, shared["rows"], selm, o_cmp, slopes, tq=min(T, ATTN_TQ))
    else:
        TQ = SUBLANES
        projp = jnp.pad(proj3, ((0, 0), (0, TQ - T), (0, 0)))
        o_cmp, selm = nsa_cmp(projp, shared["kvc"], slopes, tq=TQ, nc=shared["nc"],
                              nsb=shared["nsb"], pos0=shared["pos0"])
        pool, table, cwin = shared["past"]
        o = nsa_selwin_sample(projp, selm, o_cmp, pool, table, shared["new_rows"], cwin, slopes,
                              pos0=shared["pos0"], n_new=T)[:, :T]
    return o.reshape(N, CB)


def _trunk(x, p, pos0, wkv0, shift0, past, W, slopes):
    B, T, D = x.shape
    N = B * T
    depth = p.shape[0]
    n_a = W["w_in_a"].shape[0]
    GW = G_KV * HEAD_B
    h = x.reshape(N, D)
    wkv_new, shift_new = [], []
    shared, kv_rows, win_state = None, None, None
    for i in range(depth):
        if i < n_a:
            o, s_fin, last = _rwkv_layer(h, shift0[i], wkv0[i], i, W, B, T)
            wkv_new.append(s_fin)
            shift_new.append(last)
            h = out_ple(o, h, p[i].reshape(N, -1), W["w_out_a"], i, W["w_ple"], W["w_ple_gate"], i)
        else:
            o = _nsa_layer(h, i - n_a, shared, W, slopes, B, T, W["norm_g"][i])
            h = out_ple(o, h, p[i].reshape(N, -1), W["w_out_b"], i - n_a, W["w_ple"], W["w_ple_gate"], i)
        if i == n_a - 1:
            rows = norm_mm(h, W["kv_norm_g"], W["w_kv"]).reshape(B, T, 6 * GW)
            kv_rows = rows[:, :, :4 * GW].reshape(B, T, 4, G_KV, HEAD_B)
            win_new = rows[:, :, 4 * GW:].reshape(B, T, 2, G_KV, HEAD_B)
            if past is None:
                PS = 128
                pool = rows.reshape(B * T // PS, PS // S_CMP, S_CMP, 6 * G_KV, HEAD_B)
                table = jnp.arange(B * T // PS, dtype=jnp.int32).reshape(B, T // PS)
                t_all = T
                win_all = win_new
                shared = {"past": None, "rows": rows}
            else:
                pool, table, cwin = past
                PS = pool.shape[1] * S_CMP
                t_all = pos0 + T
                win_all = jnp.concatenate([cwin.reshape(B, -1, 2, G_KV, HEAD_B), win_new], axis=1)
                NN = LANES
                shared = {"past": past, "new_rows": jnp.pad(rows, ((0, 0), (0, NN - T), (0, 0)))}
            win_state = win_all[:, win_all.shape[1] - min(WINDOW, pos0 + T):]
            nc = (t_all - L_CMP) // S_CMP + 1
            assert nc < table.shape[1] * PS // S_CMP
            kvc = compress_kv(pool, table, W["pe_cmp"], W["w_cmp1"], W["w_cmp2"])
            shared.update(kvc=kvc, nc=nc, nsb=max(-(-t_all // L_SEL), TOPK_SEL), pos0=pos0)
    y = rmsnorm(h, W["final_norm_g"]).reshape(B, T, D)
    return y, jnp.stack(wkv_new), jnp.stack(shift_new), kv_rows, win_state


def kernel(x_prompt, x_sample, state_wkv, state_shift, cache_kv, cache_win_kv, page_table, p_prompt, p_sample, norm_g, mu_a, w_in_a, w_lora_w1, w_lora_w2, w0_a, a_lora1, a_lora2, a0_a, k_k, k_a, r_k, ln_x_w, ln_x_b, w_out_a, w_in_b, w_out_b, kv_norm_g, w_kv, pe_cmp, w_cmp1, w_cmp2, w_ple, w_ple_gate, final_norm_g):
    bf = lambda w: w.astype(BF16)
    CA = w_out_a.shape[1]
    W = dict(norm_g=norm_g, mu_a=mu_a, w_in_a=bf(w_in_a), w_lora_w1=bf(w_lora_w1), w_lora_w2=bf(w_lora_w2),
             w0_a=w0_a, a_lora1=bf(a_lora1), a_lora2=bf(a_lora2), a0_a=a0_a, k_k=k_k, k_a=k_a,
             r_k=r_k, ln_x_w=ln_x_w, ln_x_b=ln_x_b,
             w_out_a=bf(w_out_a), w_in_b=bf(w_in_b), w_out_b=bf(w_out_b), kv_norm_g=kv_norm_g, w_kv=bf(w_kv),
             pe_cmp=pe_cmp, w_cmp1=bf(w_cmp1), w_cmp2=bf(w_cmp2), w_ple=bf(w_ple), w_ple_gate=bf(w_ple_gate),
             final_norm_g=final_norm_g)
    HB = w_out_b.shape[1] // HEAD_B
    slopes = 2.0 ** (-8.0 * jnp.arange(1, HB + 1, dtype=F32) / HB)
    bp = x_prompt.shape[0]
    n_a = w_in_a.shape[0]
    D = x_prompt.shape[-1]
    wkv0 = jnp.zeros((n_a, bp, CA // HEAD_A, HEAD_A, HEAD_A), F32)
    shift0 = jnp.zeros((n_a, bp, D), F32)
    y_p, wkv_p, shift_p, kv_p, win_p = _trunk(x_prompt, p_prompt, 0, wkv0, shift0, None, W, slopes)
    db, n_pages = page_table.shape
    NP, PS = cache_kv.shape[:2]
    pool5 = cache_kv.reshape(NP, PS // S_CMP, S_CMP, -1, HEAD_B)
    past = (pool5, page_table, cache_win_kv.reshape(db, cache_win_kv.shape[1], -1))
    y_s, wkv_s, shift_s, kv_s, win_s = _trunk(x_sample, p_sample, n_pages * PS, state_wkv, state_shift, past, W, slopes)
    return (y_p, y_s, wkv_p, shift_p, kv_p, win_p, wkv_s, shift_s, kv_s, win_s)
```

```python
import functools
import math

import jax
import jax.numpy as jnp
from jax import lax
from jax.experimental import pallas as pl
from jax.experimental.pallas import tpu as pltpu

F32 = jnp.float32
BF16 = jnp.bfloat16

HEAD_A = 64
GN_EPS = 64e-5
HEAD_B = 128
G_KV = 4
L_CMP = 32
S_CMP = 16
L_SEL = 64
TOPK_SEL = 16
WINDOW = 512
RMS_EPS = 1e-6
NEG = -1e30
FORCE_BONUS = 1e4

LANES = 128
SUBLANES = 8
VMEM_LIMIT = 56 * 1024 * 1024

SCAN_NH = 2
SCAN_C = 64
SCAN_GP = 16
SCAN_CPS = 2

ROW_TILE = 1024
COL_TILE = 1024
COL_TILE_K4 = 512
FEW_ROWS = 64
WIDE_COLS = 4096
ROW_CHUNK = 256
ATTN_TQ = 256
SEL_SEG = 512

NT_DIMS = (((1,), (1,)), ((), ()))
TN_DIMS = (((0,), (0,)), ((), ()))


def _cparams(sem):
    return pltpu.CompilerParams(dimension_semantics=sem, vmem_limit_bytes=VMEM_LIMIT)


def _bdot(a, b):
    return jnp.dot(a.astype(BF16), b.astype(BF16), preferred_element_type=F32)


def _bdot_nt(a, b):
    return lax.dot_general(a.astype(BF16), b.astype(BF16), NT_DIMS, preferred_element_type=F32)


def _bdot_tn(a, b):
    return lax.dot_general(a.astype(BF16), b.astype(BF16), TN_DIMS, preferred_element_type=F32)


def _rms_kernel(x_ref, g_ref, o_ref):
    x = x_ref[...]
    ms = jnp.mean(x * x, axis=-1, keepdims=True)
    o_ref[...] = x * lax.rsqrt(ms + RMS_EPS) * g_ref[...]


def rmsnorm(x, g):
    M, D = x.shape
    tm = min(M, ROW_CHUNK)
    return pl.pallas_call(
        _rms_kernel,
        grid=(pl.cdiv(M, tm),),
        in_specs=[pl.BlockSpec((tm, D), lambda i: (i, 0)),
                  pl.BlockSpec((1, D), lambda i: (0, 0))],
        out_specs=pl.BlockSpec((tm, D), lambda i: (i, 0)),
        out_shape=jax.ShapeDtypeStruct((M, D), F32),
        compiler_params=_cparams(("parallel",)),
        name="rmsnorm",
    )(x, g.reshape(1, D))


def _col_tile(n_rows, n_cols, tn):
    return tn if n_rows > FEW_ROWS else min(n_cols, WIDE_COLS)


def _norm_rows(x, g):
    return x * lax.rsqrt(jnp.mean(x * x, axis=-1, keepdims=True) + RMS_EPS) * g


def _sigmoid(x):
    return 0.5 + 0.5 * jnp.tanh(0.5 * x)


def _norm_and_shift(h_ref, hprev_ref, xprev_ref, g_ref, i, tm, T):
    g = g_ref[...]
    hn = _norm_rows(h_ref[...], g)
    prev_row = _norm_rows(hprev_ref[SUBLANES - 1:SUBLANES, :], g)
    row = lax.broadcasted_iota(jnp.int32, (tm, 1), 0)
    xs = jnp.where(row == 0, prev_row, pltpu.roll(hn, 1, axis=0))
    if T >= tm:
        assert T % tm == 0
        start = (i * tm) % T == 0
        xs = jnp.where((row == 0) & start, xprev_ref[pl.ds((i * tm) // T, 1), :], xs)
    else:
        assert tm % T == 0
        for bb in range(tm // T):
            xs = jnp.where(row == bb * T, xprev_ref[pl.ds(i * (tm // T) + bb, 1), :], xs)
    return hn, xs


def _rwkv_in_kernel(h_ref, hprev_ref, xprev_ref, g_ref, mu_ref, w_ref, o_ref, hn_scr, xs_scr, *, tm, T):
    i, j, n = pl.program_id(0), pl.program_id(1), pl.program_id(2)
    rc = min(tm, ROW_CHUNK)
    chunks = [slice(c * rc, (c + 1) * rc) for c in range(tm // rc)]
    P = SUBLANES

    @pl.when((j == 0) & (n == 0))
    def _():
        g = g_ref[...]
        prev = _norm_rows(hprev_ref[P - 1:P, :], g)
        if T >= tm:
            assert T % tm == 0
            prev = jnp.where((i * tm) % T == 0, xprev_ref[pl.ds((i * tm) // T, 1), :], prev)
        hn_scr[0:P, :] = jnp.broadcast_to(prev, (P, prev.shape[1]))
        for ch in chunks:
            hn_scr[P + ch.start:P + ch.stop, :] = _norm_rows(h_ref[ch, :], g)

    @pl.when(n == 0)
    def _():
        mu = mu_ref[pl.ds(j, 1), :]
        for ch in chunks:
            hn = hn_scr[P + ch.start:P + ch.stop, :]
            xs = hn_scr[P - 1 + ch.start:P - 1 + ch.stop, :]
            if T < tm:
                assert tm % T == 0 and len(chunks) == 1
                row = lax.broadcasted_iota(jnp.int32, (tm, 1), 0)
                for bb in range(tm // T):
                    xs = jnp.where(row == bb * T, xprev_ref[pl.ds(i * (tm // T) + bb, 1), :], xs)
            xs_scr[ch, :] = (hn + (xs - hn) * mu).astype(BF16)

    o_ref[...] = jnp.dot(xs_scr[...], w_ref[...], preferred_element_type=F32)


def _shift_specs(tm, D, nb, ngrid):
    z = (0,) * (ngrid - 1)
    wrap = lambda f: (lambda i, *_: f(i))
    return [pl.BlockSpec((tm, D), wrap(lambda i: (i, 0))),
            pl.BlockSpec((SUBLANES, D), wrap(lambda i: (jnp.maximum(i * (tm // SUBLANES) - 1, 0), 0))),
            pl.BlockSpec((nb, D), wrap(lambda i: (0, 0))),
            pl.BlockSpec((1, D), wrap(lambda i: (0, 0))),
            pl.BlockSpec((6, D), wrap(lambda i: (0, 0)))]


def rwkv_in(h, xprev, g, mu, w, layer, T):
    N, D = h.shape
    C = w.shape[-1]
    tm = min(N, ROW_TILE)
    tn = _col_tile(N, C, COL_TILE)
    return pl.pallas_call(
        functools.partial(_rwkv_in_kernel, tm=tm, T=T),
        grid=(N // tm, 4, C // tn),
        in_specs=_shift_specs(tm, D, xprev.shape[0], 3) + [
            pl.BlockSpec((None, None, D, tn), lambda i, j, n: (layer, j, 0, n))],
        out_specs=pl.BlockSpec((None, tm, tn), lambda i, j, n: (j, i, n)),
        out_shape=jax.ShapeDtypeStruct((4, N, C), F32),
        scratch_shapes=[pltpu.VMEM((tm + SUBLANES, D), F32), pltpu.VMEM((tm, D), BF16)],
        compiler_params=_cparams(("parallel", "arbitrary", "arbitrary")),
        name="rwkv_in",
    )(h, h, xprev, g.reshape(1, D), mu, w)


def _rwkv_lora_kernel(h_ref, hprev_ref, xprev_ref, g_ref, mu_ref, lw1_ref, lw2_ref, la1_ref, la2_ref,
                      w0_ref, a0_ref, lw_ref, a_ref, *, tm, T):
    hn, xs = _norm_and_shift(h_ref, hprev_ref, xprev_ref, g_ref, pl.program_id(0), tm, T)
    dx = xs - hn
    x4 = (hn + dx * mu_ref[4:5, :]).astype(BF16)
    x5 = (hn + dx * mu_ref[5:6, :]).astype(BF16)
    t4 = jnp.tanh(jnp.dot(x4, lw1_ref[...], preferred_element_type=F32)).astype(BF16)
    x = w0_ref[...] + jnp.dot(t4, lw2_ref[...], preferred_element_type=F32)
    lw_ref[...] = -math.exp(-0.5) * _sigmoid(x)
    t5 = jnp.dot(x5, la1_ref[...], preferred_element_type=F32).astype(BF16)
    a_ref[...] = _sigmoid(a0_ref[...] + jnp.dot(t5, la2_ref[...], preferred_element_type=F32))


def rwkv_lora(h, xprev, g, mu, lw1, lw2, la1, la2, w0, a0, T):
    N, D = h.shape
    R, C = lw2.shape
    tm = min(N, ROW_CHUNK)
    full = lambda shape: pl.BlockSpec(shape, lambda i: (0,) * len(shape))
    o_spec = pl.BlockSpec((tm, C), lambda i: (i, 0))
    return pl.pallas_call(
        functools.partial(_rwkv_lora_kernel, tm=tm, T=T),
        grid=(N // tm,),
        in_specs=_shift_specs(tm, D, xprev.shape[0], 1) + [
            full((D, R)), full((R, C)), full((D, R)), full((R, C)), full((1, C)), full((1, C))],
        out_specs=[o_spec, o_spec],
        out_shape=[jax.ShapeDtypeStruct((N, C), F32)] * 2,
        compiler_params=_cparams(("parallel",)),
        name="rwkv_lora",
    )(h, h, xprev, g.reshape(1, D), mu, lw1, lw2, la1, la2, w0.reshape(1, C), a0.reshape(1, C))


def _norm_mm_kernel(h_ref, g_ref, w_ref, o_ref, xs_scr):
    @pl.when(pl.program_id(1) == 0)
    def _():
        rc = min(h_ref.shape[0], ROW_CHUNK)

        def norm_chunk(c, carry):
            rows = pl.ds(pl.multiple_of(c * rc, rc), rc)
            xs_scr[rows, :] = _norm_rows(h_ref[rows, :], g_ref[...]).astype(BF16)
            return carry

        lax.fori_loop(0, h_ref.shape[0] // rc, norm_chunk, 0)

    o_ref[...] = jnp.dot(xs_scr[...], w_ref[...], preferred_element_type=F32)


def norm_mm(h, g, w, widx=()):
    N, D = h.shape
    NO = w.shape[-1]
    tm = min(N, ROW_TILE)
    tn = _col_tile(N, NO, COL_TILE)
    nlead = len(widx)
    return pl.pallas_call(
        _norm_mm_kernel,
        grid=(N // tm, pl.cdiv(NO, tn)),
        in_specs=[pl.BlockSpec((tm, D), lambda i, n: (i, 0)),
                  pl.BlockSpec((1, D), lambda i, n: (0, 0)),
                  pl.BlockSpec((None,) * nlead + (D, tn), lambda i, n: tuple(widx) + (0, n))],
        out_specs=pl.BlockSpec((tm, tn), lambda i, n: (i, n)),
        out_shape=jax.ShapeDtypeStruct((N, NO), F32),
        scratch_shapes=[pltpu.VMEM((tm, D), BF16)],
        compiler_params=_cparams(("parallel", "arbitrary")),
        name="norm_mm",
    )(h, g.reshape(1, D), w)


def _mm_res_kernel(x_ref, h_ref, w_ref, o_ref, ob_ref):
    h1 = h_ref[...] + jnp.dot(x_ref[...].astype(BF16), w_ref[...], preferred_element_type=F32)
    o_ref[...] = h1
    ob_ref[...] = h1.astype(BF16)


def _ple_gate_kernel(h1b_ref, h1_ref, p_ref, wp_ref, wg_ref, o_ref):
    gate = jnp.dot(h1b_ref[...], wg_ref[...], preferred_element_type=F32)
    ple = jnp.dot(p_ref[...].astype(BF16), wp_ref[...], preferred_element_type=F32)
    o_ref[...] = h1_ref[...] + ple * _sigmoid(gate)


def out_ple(x, h, p, w_out, oidx, w_ple, w_gate, layer):
    N, C = x.shape
    D = h.shape[1]
    DP = p.shape[1]
    tm = min(N, ROW_TILE)
    tn = _col_tile(N, D, COL_TILE_K4)
    tile = pl.BlockSpec((tm, tn), lambda i, n: (i, n))
    h1, h1b = pl.pallas_call(
        _mm_res_kernel,
        grid=(N // tm, D // tn),
        in_specs=[pl.BlockSpec((tm, C), lambda i, n: (i, 0)), tile,
                  pl.BlockSpec((None, C, tn), lambda i, n: (oidx, 0, n))],
        out_specs=[tile, tile],
        out_shape=[jax.ShapeDtypeStruct((N, D), F32), jax.ShapeDtypeStruct((N, D), BF16)],
        compiler_params=_cparams(("parallel", "parallel")),
        name="mm_res",
    )(x, h, w_out)
    tg = _col_tile(N, D, COL_TILE)
    gtile = pl.BlockSpec((tm, tg), lambda i, n: (i, n))
    return pl.pallas_call(
        _ple_gate_kernel,
        grid=(N // tm, D // tg),
        in_specs=[pl.BlockSpec((tm, D), lambda i, n: (i, 0)), gtile,
                  pl.BlockSpec((tm, DP), lambda i, n: (i, 0)),
                  pl.BlockSpec((None, DP, tg), lambda i, n: (layer, 0, n)),
                  pl.BlockSpec((None, D, tg), lambda i, n: (layer, 0, n))],
        out_specs=gtile,
        out_shape=jax.ShapeDtypeStruct((N, D), F32),
        compiler_params=_cparams(("parallel", "parallel")),
        name="ple_gate",
    )(h1b, h1, p, w_ple, w_gate)


def _scan_kernel(r_ref, k_ref, v_ref, zg_ref, lw_ref, a_ref, pv_ref, s0_ref, o_ref, sfin_ref, s_scr, *, C, NH, GP, CPS):
    L = NH * HEAD_A
    NC = NH * C
    ci = pl.program_id(2)

    @pl.when(ci == 0)
    def _():
        for gp in range(GP):
            rows = []
            for hh in range(NH):
                pieces = [s0_ref[0, gp * NH + hh] if h2 == hh else jnp.zeros((HEAD_A, HEAD_A), F32)
                          for h2 in range(NH)]
                rows.append(jnp.concatenate(pieces, axis=1))
            s_scr[gp] = jnp.concatenate(rows, axis=0)

    row_c = lax.broadcasted_iota(jnp.int32, (C, NC), 0)
    col_s = lax.broadcasted_iota(jnp.int32, (C, NC), 1) % C
    tri_strict = col_s < row_c
    tri_incl = (lax.broadcasted_iota(jnp.int32, (C, 2 * NC), 1) % C
                <= lax.broadcasted_iota(jnp.int32, (C, 2 * NC), 0))
    st_mask = (lax.broadcasted_iota(jnp.int32, (NC, L), 0) // C
               == lax.broadcasted_iota(jnp.int32, (NC, L), 1) // HEAD_A)
    bd_mask = (lax.broadcasted_iota(jnp.int32, (NC, NC), 0) // C
               == lax.broadcasted_iota(jnp.int32, (NC, NC), 1) // C)
    head_mask = (lax.broadcasted_iota(jnp.int32, (L, L), 0) // HEAD_A
                 == lax.broadcasted_iota(jnp.int32, (L, L), 1) // HEAD_A)

    def st(x):
        return jnp.where(st_mask, jnp.concatenate([x] * NH, axis=0), 0.0)

    def bd(w):
        return jnp.where(bd_mask, jnp.concatenate([w] * NH, axis=0), 0.0)

    n_double = int(math.log2(C))
    each = lambda f, *cols: [f(*xs) for xs in zip(*cols)]
    sls = [slice(gp * L, (gp + 1) * L) for gp in range(GP)]
    head_of_lane = lax.broadcasted_iota(jnp.int32, (1, L), 1) // HEAD_A

    def hsum(x):
        out = None
        for hh in range(NH):
            sh = jnp.sum(jnp.where(head_of_lane == hh, x, 0.0), axis=-1, keepdims=True)
            out = sh if out is None else jnp.where(head_of_lane == hh, sh, out)
        return out

    k_k, k_a, r_k, ln_w, ln_b = ([pv_ref[n:n + 1, sl] for sl in sls] for n in range(5))
    cum_mat = (lax.broadcasted_iota(jnp.int32, (C, C), 1)
               <= lax.broadcasted_iota(jnp.int32, (C, C), 0)).astype(BF16)
    for cc in range(CPS):
        rs = slice(cc * C, (cc + 1) * C)
        lw = [lw_ref[0, rs, sl] for sl in sls]
        a_sig = [a_ref[0, rs, sl] for sl in sls]
        r = [r_ref[0, rs, sl] for sl in sls]
        k_raw = [k_ref[0, rs, sl] for sl in sls]
        v = [v_ref[0, rs, sl] for sl in sls]
        kk = each(lambda x, w: x * w, k_raw, k_k)
        kk = each(lambda x: x * lax.rsqrt(jnp.maximum(hsum(x * x), 1e-24)), kk)
        k = each(lambda x, a, w: x * (1.0 + (a - 1.0) * w), k_raw, a_sig, k_a)
        lw_hi = each(lambda z: z.astype(BF16), lw)
        lw_lo = each(lambda z, hi: (z - hi.astype(F32)).astype(BF16), lw, lw_hi)
        cum2 = each(lambda hi, lo: jnp.dot(cum_mat, jnp.concatenate([hi, lo], axis=1), preferred_element_type=F32),
                    lw_hi, lw_lo)
        cum = [c2[:, :L] + c2[:, L:] for c2 in cum2]
        p_incl = each(jnp.exp, cum)
        p_inv = each(lambda z: jnp.exp(-z), cum)
        at = each(lambda x, c, w: -x * jnp.exp(c - w), kk, cum, lw)
        rt = each(lambda x, p: x * p, r, p_incl)
        bt = each(lambda x, a, p: x * a * p, kk, a_sig, p_inv)
        kt = each(lambda x, p: x * p, k, p_inv)
        S = [s_scr[gp] for gp in range(GP)]
        ar = each(lambda x, y: jnp.concatenate([x, y], axis=0), at, rt)
        bk_st = each(lambda x, y: jnp.concatenate([st(x), st(y)], axis=0), bt, kt)
        Gm = each(_bdot_nt, ar, bk_st)
        w_ab = [jnp.where(tri_strict, g[:C, :NC], 0.0) for g in Gm]
        tm = w_ab
        pw = each(lambda w: _bdot(w, bd(w)), w_ab)
        LH = each(_bdot_nt, ar, S)
        v_st = each(st, v)
        x = [lh[:C] + _bdot(jnp.where(tri_strict, g[:C, NC:], 0.0), vs) for lh, g, vs in zip(LH, Gm, v_st)]
        for it in range(1, n_double):
            if it < n_double - 1:
                both = each(lambda t, p: _bdot(p, jnp.concatenate([bd(t), bd(p)], axis=1)), tm, pw)
                tm = each(lambda t, p, b2: t + p + b2[:, :NC], tm, pw, both)
                pw = [b2[:, NC:] for b2 in both]
            else:
                tm = each(lambda t, p: t + p + _bdot(p, bd(t)), tm, pw)
        u = each(lambda xx, t: xx + _bdot(t, st(xx)), x, tm)
        o = [lh[C:] + _bdot(jnp.where(tri_incl, g[C:], 0.0), jnp.concatenate([st(uu), vs], axis=0))
             for lh, g, uu, vs in zip(LH, Gm, u, v_st)]
        inv_n = 1.0 / HEAD_A
        dev = each(lambda x: x - hsum(x) * inv_n, o)
        gn = each(lambda d, w, b_: d * lax.rsqrt(hsum(d * d) * inv_n + GN_EPS) * w + b_, dev, ln_w, ln_b)
        bonus = each(lambda rr, kx, w, vv: hsum(rr * kx * w) * vv, r, k, r_k, v)
        for sl, y, bo in zip(sls, gn, bonus):
            zg = zg_ref[0, rs, sl]
            o_ref[0, rs, sl] = ((y + bo) * (zg * _sigmoid(zg))).astype(o_ref.dtype)
        ds = [_bdot_tn(jnp.concatenate([uu, vv], axis=0), jnp.concatenate([b_, k_], axis=0))
              for uu, vv, b_, k_ in zip(u, v, bt, kt)]
        for gp in range(GP):
            s_scr[gp] = (S[gp] + jnp.where(head_mask, ds[gp], 0.0)) * p_incl[gp][C - 1:C, :]

    @pl.when(ci == pl.num_programs(2) - 1)
    def _():
        for gp in range(GP):
            s_all = s_scr[gp]
            for hh in range(NH):
                blk = slice(hh * HEAD_A, (hh + 1) * HEAD_A)
                sfin_ref[0, gp * NH + hh] = s_all[blk, blk]


def rwkv_scan(rkvg, lw, a, pvec, s0):
    _, B, T, CA = rkvg.shape
    H = CA // HEAD_A
    NH, C, GP = SCAN_NH, SCAN_C, SCAN_GP
    assert NH * C == LANES and H % (NH * GP) == 0
    L = NH * HEAD_A
    NG = H // NH
    CPS = SCAN_CPS if T >= SCAN_CPS * C else 1
    TS = CPS * C
    Tp = -(-T // TS) * TS
    if Tp != T:
        rkvg = jnp.pad(rkvg, ((0, 0), (0, 0), (0, Tp - T), (0, 0)))
        lw, a = (jnp.pad(z, ((0, 0), (0, Tp - T), (0, 0))) for z in (lw, a))
    seq_spec = pl.BlockSpec((1, TS, GP * L), lambda bi, gi, ci: (bi, ci, gi))
    proj_spec = lambda j: pl.BlockSpec((None, 1, TS, GP * L), lambda bi, gi, ci: (j, bi, ci, gi))
    st_spec = pl.BlockSpec((1, GP * NH, HEAD_A, HEAD_A), lambda bi, gi, ci: (bi, gi, 0, 0))
    o, s_fin = pl.pallas_call(
        functools.partial(_scan_kernel, C=C, NH=NH, GP=GP, CPS=CPS),
        grid=(B, NG // GP, Tp // TS),
        in_specs=[proj_spec(j) for j in range(4)] + [seq_spec, seq_spec,
                  pl.BlockSpec((5, GP * L), lambda bi, gi, ci: (0, gi)), st_spec],
        out_specs=[seq_spec, st_spec],
        out_shape=[jax.ShapeDtypeStruct((B, Tp, CA), BF16), jax.ShapeDtypeStruct((B, H, HEAD_A, HEAD_A), F32)],
        scratch_shapes=[pltpu.VMEM((GP, L, L), F32)],
        compiler_params=_cparams(("parallel", "parallel", "arbitrary")),
        name="rwkv_scan",
    )(rkvg, rkvg, rkvg, rkvg, lw, a, pvec, s0)
    return o[:, :T], s_fin


def _gelu_tanh(x):
    c = math.sqrt(2.0 / math.pi)
    return 0.5 * x * (1.0 + jnp.tanh(c * (x + 0.044715 * (x * x * x))))


def _compress_kernel(pt_ref, *refs, PGS):
    del pt_ref
    page_refs = refs[:PGS]
    next_ref, pe_ref, w1_ref, w2_ref, out_ref = refs[PGS:]
    CPP = page_refs[0].shape[1]
    NCH = PGS * CPP
    CG = 2 * G_KV
    M = (NCH + 1) * CG

    def rows_of(l, hf):
        pe = pe_ref[hf, l]
        parts = [(page_refs[i][0, :, l] + pe[None]).reshape(CPP * CG, HEAD_B) for i in range(PGS)]
        parts.append(next_ref[0, 0, l] + pe)
        return jnp.concatenate(parts, axis=0)

    top = jnp.zeros((M, 2 * HEAD_B), F32)
    bot = jnp.zeros((M, 2 * HEAD_B), F32)
    for l in range(0, S_CMP, 2):
        wrows = pl.ds(l * HEAD_B, 2 * HEAD_B)
        xt = jnp.concatenate([rows_of(l, 0), rows_of(l + 1, 0)], axis=1).astype(BF16)
        top = top + jnp.dot(xt, w1_ref[0, wrows, :], preferred_element_type=F32)
        xb = jnp.concatenate([rows_of(l, 1), rows_of(l + 1, 1)], axis=1).astype(BF16)
        bot = bot + jnp.dot(xb, w1_ref[1, wrows, :], preferred_element_type=F32)
    is_k = (lax.broadcasted_iota(jnp.int32, (M, 1), 0) % CG) < G_KV
    pick = lambda z, n: jnp.where(is_k[:n], z[:n, :HEAD_B], z[:n, HEAD_B:])
    hcur = pick(top, NCH * CG) + pick(bot, M)[CG:]
    o2 = jnp.dot(_gelu_tanh(hcur).astype(BF16), w2_ref[...], preferred_element_type=F32)
    out_ref[0] = pick(o2, NCH * CG).reshape(NCH, CG, HEAD_B)


def compress_kv(pool5, table, pe_cmp, w1, w2):
    NP, CPP = pool5.shape[:2]
    B, n_pages = table.shape
    PGS = max(d for d in (16, 8, 4, 2, 1) if n_pages % d == 0)
    NCH = PGS * CPP
    CG = 2 * G_KV
    half = S_CMP * HEAD_B
    pe_r = jnp.repeat(pe_cmp.reshape(2, 2, S_CMP, HEAD_B).transpose(1, 2, 0, 3), G_KV, axis=2)
    w1_r = w1.reshape(2, 2, half, HEAD_B).transpose(1, 2, 0, 3).reshape(2, half, 2 * HEAD_B)
    w2_r = jnp.concatenate([w2[0], w2[1]], axis=1)

    def page_map(i):
        return lambda b, s, pt: (pt[b, s * PGS + i], 0, 0, 0, 0)

    def next_map(b, s, pt):
        return (pt[b, jnp.minimum((s + 1) * PGS, n_pages - 1)], 0, 0, 0, 0)

    const = lambda n: (lambda b, s, pt: (0,) * n)
    grid_spec = pltpu.PrefetchScalarGridSpec(
        num_scalar_prefetch=1,
        grid=(B, n_pages // PGS),
        in_specs=[pl.BlockSpec((1, CPP, S_CMP, CG, HEAD_B), page_map(i)) for i in range(PGS)] + [
            pl.BlockSpec((1, 1, S_CMP, CG, HEAD_B), next_map),
            pl.BlockSpec((2, S_CMP, CG, HEAD_B), const(4)),
            pl.BlockSpec((2, half, 2 * HEAD_B), const(3)),
            pl.BlockSpec((HEAD_B, 2 * HEAD_B), const(2)),
        ],
        out_specs=pl.BlockSpec((1, NCH, CG, HEAD_B), lambda b, s, pt: (b, s, 0, 0)),
    )
    return pl.pallas_call(
        functools.partial(_compress_kernel, PGS=PGS),
        grid_spec=grid_spec,
        out_shape=jax.ShapeDtypeStruct((B, n_pages * CPP, CG, HEAD_B), F32),
        compiler_params=_cparams(("parallel", "arbitrary")),
        name="compress_kv",
    )(table, *([pool5] * PGS), pool5, pe_r, w1_r, w2_r)


def _stack_heads(q, HG):
    return jnp.concatenate([q[:, h * HEAD_B:(h + 1) * HEAD_B] for h in range(HG)], axis=0)


def _masked_softmax_rows(s, mask):
    s = jnp.where(mask, s, NEG)
    m = jnp.max(s, axis=-1, keepdims=True)
    e = jnp.where(mask, jnp.exp(s - m), 0.0)
    l = jnp.sum(e, axis=-1, keepdims=True)
    return e / jnp.where(l > 0.0, l, 1.0)


def _attend_stacked(s, dist, mask, v, slopes_ref, g, HG, tq):
    ps = []
    psum = jnp.zeros(dist.shape, F32)
    for h in range(HG):
        p = _masked_softmax_rows(s[h * tq:(h + 1) * tq] - slopes_ref[g * HG + h] * dist, mask)
        psum = psum + p
        ps.append(p.astype(BF16))
    o = jnp.dot(jnp.concatenate(ps, axis=0), v.astype(BF16), preferred_element_type=F32)
    return o, psum


def _unstack_store(o_ref, o, HG, tq):
    for h in range(HG):
        o_ref[0, :, h * HEAD_B:(h + 1) * HEAD_B] = o[h * tq:(h + 1) * tq]


def _cmp_branch(q, kc, vc, slopes_ref, g, qpos, *, tq, HG, nc, nsb, NSBp):
    NCp = kc.shape[0]
    q_st = _stack_heads(q * (HEAD_B ** -0.5), HG)
    s = _bdot_nt(q_st, kc)
    cidx = lax.broadcasted_iota(jnp.int32, (1, NCp), 1)
    cend = S_CMP * cidx + (L_CMP - 1)
    mask = (cend <= qpos) & (cidx < nc)
    dist = (qpos - cend).astype(F32)
    o, imp_c = _attend_stacked(s, dist, mask, vc, slopes_ref, g, HG, tq)

    crow = lax.broadcasted_iota(jnp.int32, (NCp, NSBp), 0)
    jcol = lax.broadcasted_iota(jnp.int32, (NCp, NSBp), 1)
    overlap = ((S_CMP * crow < L_SEL * (jcol + 1)) & (S_CMP * crow + L_CMP > L_SEL * jcol)
               & (crow < nc)).astype(F32)
    imp = jnp.dot(imp_c, overlap, precision=lax.Precision.HIGHEST, preferred_element_type=F32)
    lane = lax.broadcasted_iota(jnp.int32, (tq, NSBp), 1)
    cur = jnp.right_shift(qpos, int(math.log2(L_SEL)))
    forced = ((lane == 0) | (lane == cur) | (lane == cur - 1)).astype(F32)
    score = jnp.where(lane <= cur, imp + FORCE_BONUS * forced, NEG)
    score = jnp.where(lane < nsb, score, -3e38)

    if tq % LANES == 0 and NSBp == LANES:
        nr = -(-nsb // SUBLANES) * SUBLANES
        st = score.T[:nr]
        sub = lax.broadcasted_iota(jnp.int32, (nr, tq), 0)
        cnt = jnp.zeros((nr, tq), F32)
        for i in range(nsb):
            row = st[i:i + 1, :]
            beats = (row > st) | ((row == st) & (sub > i))
            cnt = cnt + jnp.where(beats, 1.0, 0.0)
        sel_t = jnp.where((cnt < TOPK_SEL) & (st > 0.5 * NEG), 1.0, 0.0)
        return o, jnp.concatenate([sel_t, jnp.zeros((NSBp - nr, tq), F32)], axis=0).T
    else:
        cnt = jnp.zeros((tq, NSBp), F32)
        for i in range(nsb):
            col = score[:, i:i + 1]
            beats = (col > score) | ((col == score) & (lane > i))
            cnt = cnt + jnp.where(beats, 1.0, 0.0)
        sel = (cnt < TOPK_SEL) & (score > 0.5 * NEG)
        return o, sel.astype(F32)


def _nsa_cmp_kernel(slopes_ref, q_ref, kc_ref, vc_ref, o_ref, selm_ref, *, tq, HG, nc, nsb, pos0):
    g = pl.program_id(1)
    qpos = pos0 + pl.program_id(2) * tq + lax.broadcasted_iota(jnp.int32, (tq, 1), 0)
    o, sel = _cmp_branch(q_ref[0], kc_ref[0], vc_ref[0], slopes_ref, g, qpos,
                         tq=tq, HG=HG, nc=nc, nsb=nsb, NSBp=selm_ref.shape[3])
    _unstack_store(o_ref, o, HG, tq)
    selm_ref[0, 0] = sel


def nsa_cmp(proj, kvc, slopes, *, tq, nc, nsb, pos0):
    B, T, _ = proj.shape
    HG = slopes.shape[0] // G_KV
    NCp = kvc.shape[1]
    NSBp = -(-nsb // LANES) * LANES
    gw = HG * HEAD_B
    return pl.pallas_call(
        functools.partial(_nsa_cmp_kernel, tq=tq, HG=HG, nc=nc, nsb=nsb, pos0=pos0),
        grid=(B, G_KV, T // tq),
        in_specs=[pl.BlockSpec(memory_space=pltpu.SMEM),
                  pl.BlockSpec((1, tq, gw), lambda b, g, t: (b, t, g)),
                  pl.BlockSpec((1, NCp, HEAD_B), lambda b, g, t: (b, 0, g)),
                  pl.BlockSpec((1, NCp, HEAD_B), lambda b, g, t: (b, 0, G_KV + g))],
        out_specs=[pl.BlockSpec((1, tq, gw), lambda b, g, t: (b, t, g)),
                   pl.BlockSpec((1, 1, tq, NSBp), lambda b, g, t: (b, g, t, 0))],
        out_shape=[jax.ShapeDtypeStruct((B, T, G_KV * gw), F32),
                   jax.ShapeDtypeStruct((B, G_KV, T, NSBp), F32)],
        compiler_params=_cparams(("parallel", "parallel", "parallel")),
        name="nsa_cmp",
    )(slopes, proj, kvc.reshape(B, NCp, -1), kvc.reshape(B, NCp, -1))


LOG2E = 1.4426950408889634


def _bf16_part(x):
    return x.astype(BF16).astype(F32)


def _alibi_lhs(q, slope_col):
    c = slope_col * LOG2E
    c1 = _bf16_part(c)
    c2 = _bf16_part(c - c1)
    c3 = _bf16_part(c - c1 - c2)
    lane = lax.broadcasted_iota(jnp.int32, q.shape, 1)
    extra = jnp.where((lane == 0) | (lane == 3), c1,
                      jnp.where((lane == 1) | (lane == 4), c2, jnp.where((lane == 2) | (lane == 5), c3, 0.0)))
    return jnp.concatenate([q, extra], axis=1).astype(BF16)


def _alibi_rhs(k, k0):
    pos = k0 + lax.broadcasted_iota(jnp.int32, k.shape, 0)
    lane = lax.broadcasted_iota(jnp.int32, k.shape, 1)
    hi = jnp.bitwise_and(pos, -L_SEL)
    extra = jnp.where(lane < 3, hi, jnp.where(lane < 6, pos - hi, 0)).astype(F32)
    return jnp.concatenate([k, extra], axis=1).astype(BF16)


def _with_ones(v):
    lane = lax.broadcasted_iota(jnp.int32, v.shape, 1)
    return jnp.concatenate([v, jnp.where(lane == 0, 1.0, 0.0)], axis=1).astype(BF16)


def _silu(x):
    return x * _sigmoid(x)


def _nsa_selwin_prompt_kernel(slopes_ref, q_ref, kc_ref, vc_ref, ks_ref, vs_ref, kw_ref, vw_ref,
                              zc_ref, zs_ref, zw_ref, gate_ref, o_ref, s_scr, *, tq, HG, T, WS, SEG, nc, nsb):
    g = pl.program_id(1)
    qt = pl.program_id(2)
    NSBp = -(-nsb // LANES) * LANES
    R = HG * tq
    qpos = qt * tq + lax.broadcasted_iota(jnp.int32, (tq, 1), 0)
    o_cmp, selm = _cmp_branch(q_ref[0], kc_ref[0], vc_ref[0], slopes_ref, g, qpos,
                              tq=tq, HG=HG, nc=nc, nsb=nsb, NSBp=NSBp)
    slope_col = jnp.concatenate([jnp.full((tq, 1), slopes_ref[g * HG + h], F32) for h in range(HG)], axis=0)
    tile_heads = lambda x: jnp.concatenate([x] * HG, axis=0)
    q2 = _alibi_lhs(_stack_heads(q_ref[0] * (HEAD_B ** -0.5 * LOG2E), HG), slope_col)

    selm_b = selm.astype(BF16)
    nseg = (qt * tq + tq + SEG - 1) // SEG

    def seg_scores(si, m):
        k0 = pl.multiple_of(si * SEG, SEG)
        kcol = k0 + lax.broadcasted_iota(jnp.int32, (NSBp, SEG), 1)
        expand = (jnp.right_shift(kcol, int(math.log2(L_SEL)))
                  == lax.broadcasted_iota(jnp.int32, (NSBp, SEG), 0)).astype(BF16)
        in_blk = jnp.dot(selm_b, expand, preferred_element_type=F32) > 0.5
        kpos = k0 + lax.broadcasted_iota(jnp.int32, (1, SEG), 1)
        mask = tile_heads(in_blk & (kpos <= qpos))
        s = lax.dot_general(q2, _alibi_rhs(ks_ref[0, pl.ds(k0, SEG), :], k0), NT_DIMS, preferred_element_type=F32)
        s = jnp.where(mask, s, NEG)
        s_scr[si] = s
        return jnp.maximum(m, jnp.max(s, axis=-1, keepdims=True))

    m = lax.fori_loop(0, nseg, seg_scores, jnp.full((R, 1), NEG, F32))

    def seg_pv(si, acc):
        k0 = pl.multiple_of(si * SEG, SEG)
        e = jnp.exp2(s_scr[si] - m).astype(BF16)
        return acc + jnp.dot(e, _with_ones(vs_ref[0, pl.ds(k0, SEG), :]), preferred_element_type=F32)

    acc = lax.fori_loop(0, nseg, seg_pv, jnp.zeros((R, 2 * HEAD_B), F32))
    o = acc[:, :HEAD_B] / acc[:, HEAD_B:HEAD_B + 1]

    start = pl.multiple_of(jnp.clip(qt * tq - WINDOW, 0, T - WS), SUBLANES)
    distw = qpos - (start + lax.broadcasted_iota(jnp.int32, (1, WS), 1))
    maskw = tile_heads((distw >= 0) & (distw < WINDOW))
    sw = lax.dot_general(q2, _alibi_rhs(kw_ref[0, pl.ds(start, WS), :], start), NT_DIMS, preferred_element_type=F32)
    sw = jnp.where(maskw, sw, NEG)
    ew = jnp.exp2(sw - jnp.max(sw, axis=-1, keepdims=True)).astype(BF16)
    accw = jnp.dot(ew, _with_ones(vw_ref[0, pl.ds(start, WS), :]), preferred_element_type=F32)
    ow = accw[:, :HEAD_B] / accw[:, HEAD_B:HEAD_B + 1]

    HB = G_KV * HG
    gates = _sigmoid(gate_ref[0])
    lane = lax.broadcasted_iota(jnp.int32, gates.shape, 1)
    gate_col = lambda idx: jnp.sum(jnp.where(lane == idx, gates, 0.0), axis=1, keepdims=True)
    for h in range(HG):
        hs = slice(h * HEAD_B, (h + 1) * HEAD_B)
        rs = slice(h * tq, (h + 1) * tq)
        hd = g * HG + h
        y = (gate_col(hd) * o_cmp[rs] * _silu(zc_ref[0, :, hs])
             + gate_col(HB + hd) * o[rs] * _silu(zs_ref[0, :, hs])
             + gate_col(2 * HB + hd) * ow[rs] * _silu(zw_ref[0, :, hs]))
        o_ref[0, :, hs] = y.astype(o_ref.dtype)


def nsa_prompt(proj, rows, kvc, slopes, *, tq, nc, nsb):
    B, T, _ = proj.shape
    HG = slopes.shape[0] // G_KV
    NCp = kvc.shape[1]
    gw = HG * HEAD_B
    CB = G_KV * gw
    WS = min(T, WINDOW + tq)
    SEG = min(T, SEL_SEG)
    assert T % SEG == 0
    kvc2 = kvc.reshape(B, NCp, -1)
    kv_spec = lambda c: pl.BlockSpec((1, T, HEAD_B), lambda b, g, t: (b, 0, c * G_KV + g))
    cmp_spec = lambda c: pl.BlockSpec((1, NCp, HEAD_B), lambda b, g, t: (b, 0, c * G_KV + g))
    head_spec = lambda blk: pl.BlockSpec((1, tq, gw), lambda b, g, t: (b, t, blk * G_KV + g))
    return pl.pallas_call(
        functools.partial(_nsa_selwin_prompt_kernel, tq=tq, HG=HG, T=T, WS=WS, SEG=SEG, nc=nc, nsb=nsb),
        grid=(B, G_KV, T // tq),
        in_specs=[pl.BlockSpec(memory_space=pltpu.SMEM),
                  head_spec(0), cmp_spec(0), cmp_spec(1),
                  kv_spec(2), kv_spec(3), kv_spec(4), kv_spec(5),
                  head_spec(1), head_spec(2), head_spec(3),
                  pl.BlockSpec((1, tq, LANES), lambda b, g, t: (b, t, 4 * CB // LANES))],
        out_specs=head_spec(0),
        out_shape=jax.ShapeDtypeStruct((B, T, CB), BF16),
        scratch_shapes=[pltpu.VMEM((T // SEG, HG * tq, SEG), F32)],
        compiler_params=_cparams(("parallel", "parallel", "parallel")),
        name="nsa_prompt",
    )(slopes, proj, kvc2, kvc2, rows, rows, rows, rows, proj, proj, proj, proj)


def _nsa_selwin_sample_kernel(pt_ref, slopes_ref, q_ref, selm_ref, selst_ref, *refs, PGS, PS, HG, TQ, pos0, n_new, n_win):
    del pt_ref
    page_refs = refs[:PGS]
    (new_ref, cwin_ref, ocmp_ref, zc_ref, zs_ref, zw_ref, gate_ref, o_ref,
     m_scr, l_scr, acc_scr) = refs[PGS:]
    st = pl.program_id(1)
    NSBp = selm_ref.shape[3]
    GW = G_KV * HEAD_B
    sel_shift = int(math.log2(L_SEL))

    @pl.when(st == 0)
    def _():
        m_scr[...] = jnp.full(m_scr.shape, NEG, F32)
        l_scr[...] = jnp.zeros(l_scr.shape, F32)
        acc_scr[...] = jnp.zeros(acc_scr.shape, F32)

    qpos = pos0 + lax.broadcasted_iota(jnp.int32, (TQ, 1), 0)
    lane_j = lax.broadcasted_iota(jnp.int32, (TQ, NSBp), 1)
    tile_heads = lambda x: jnp.concatenate([x] * HG, axis=0)

    def sel_col(selm_g, j):
        return jnp.sum(jnp.where(lane_j == j, selm_g, 0.0), axis=1, keepdims=True)

    def online_update(s, mask, v):
        gs = range(G_KV)
        m_old = [m_scr[g] for g in gs]
        m_new = [jnp.maximum(m_old[g], jnp.max(jnp.where(mask[g], s[g], NEG), axis=-1, keepdims=True)) for g in gs]
        e = [jnp.where(mask[g], jnp.exp(s[g] - m_new[g]), 0.0) for g in gs]
        alpha = [jnp.exp(m_old[g] - m_new[g]) for g in gs]
        pv = [_bdot(e[g], v[g]) for g in gs]
        for g in gs:
            l_scr[g] = alpha[g] * l_scr[g] + jnp.sum(e[g], axis=-1, keepdims=True)
            acc_scr[g] = alpha[g] * acc_scr[g] + pv[g]
            m_scr[g] = m_new[g]

    NK = PGS * PS
    kpos = st * NK + lax.broadcasted_iota(jnp.int32, (1, NK), 1)
    expand = (jnp.right_shift(lax.broadcasted_iota(jnp.int32, (LANES, NK), 1), sel_shift)
              == lax.broadcasted_iota(jnp.int32, (LANES, NK), 0)).astype(BF16)
    in_blk_all = jnp.dot(selst_ref[0, 0].astype(BF16), expand, preferred_element_type=F32)
    dist = qpos - kpos
    distf = tile_heads(dist.astype(F32))
    gs = range(G_KV)
    q_st = [_stack_heads(q_ref[0, :, g * HG * HEAD_B:(g + 1) * HG * HEAD_B] * (HEAD_B ** -0.5), HG).astype(BF16)
            for g in gs]
    slope_col = [jnp.concatenate([jnp.full((TQ, 1), slopes_ref[g * HG + h], F32) for h in range(HG)], axis=0)
                 for g in gs]
    slabs = [jnp.swapaxes(page_refs[i][0].reshape(PS, 2 * G_KV, HEAD_B), 0, 1) for i in range(PGS)]
    k = [jnp.concatenate([slabs[i][g] for i in range(PGS)], axis=0) for g in gs]
    v = [jnp.concatenate([slabs[i][G_KV + g] for i in range(PGS)], axis=0) for g in gs]
    mask = [tile_heads((in_blk_all[g * TQ:(g + 1) * TQ] > 0.5) & (dist >= 0)) for g in gs]
    s = [_bdot_nt(q_st[g], k[g]) - slope_col[g] * distf for g in gs]
    online_update(s, mask, v)

    @pl.when(st == pl.num_programs(1) - 1)
    def _():
        NN = new_ref.shape[1]
        rnew = lax.broadcasted_iota(jnp.int32, (1, NN), 1)
        kpos_n = pos0 + rnew
        dist_n = qpos - kpos_n
        ok_n = (rnew < n_new) & (dist_n >= 0)
        jn = pos0 >> sel_shift
        kpos_w = pos0 - n_win + lax.broadcasted_iota(jnp.int32, (1, n_win), 1)
        dist_w = qpos - kpos_w
        dist_wall = jnp.concatenate([dist_w, dist_n], axis=1)
        mask_wall = jnp.concatenate([(dist_w >= 0) & (dist_w < WINDOW), ok_n & (dist_n < WINDOW)], axis=1)
        kn = [new_ref[0, :, 2 * GW + g * HEAD_B:2 * GW + (g + 1) * HEAD_B] for g in gs]
        vn = [new_ref[0, :, 3 * GW + g * HEAD_B:3 * GW + (g + 1) * HEAD_B] for g in gs]
        mask_n = [tile_heads((sel_col(selm_ref[0, g], jn) > 0.5) & ok_n) for g in gs]
        dist_nf = tile_heads(dist_n.astype(F32))
        s_n = [_bdot_nt(q_st[g], kn[g]) - slope_col[g] * dist_nf for g in gs]
        online_update(s_n, mask_n, vn)
        kw = [jnp.concatenate([cwin_ref[0, :, g * HEAD_B:(g + 1) * HEAD_B],
                               new_ref[0, :, 4 * GW + g * HEAD_B:4 * GW + (g + 1) * HEAD_B]], axis=0) for g in gs]
        vw = [jnp.concatenate([cwin_ref[0, :, GW + g * HEAD_B:GW + (g + 1) * HEAD_B],
                               new_ref[0, :, 5 * GW + g * HEAD_B:5 * GW + (g + 1) * HEAD_B]], axis=0) for g in gs]
        dist_wf = tile_heads(dist_wall.astype(F32))
        mask_w = tile_heads(mask_wall)
        sw = [_bdot_nt(q_st[g], kw[g]) - slope_col[g] * dist_wf for g in gs]
        pw = [_masked_softmax_rows(sw[g], mask_w) for g in gs]
        ow = [_bdot(pw[g], vw[g]) for g in gs]
        HB = G_KV * HG
        gates = _sigmoid(gate_ref[0])
        for g in gs:
            l = l_scr[g]
            o = acc_scr[g] / jnp.where(l > 0.0, l, 1.0)
            for h in range(HG):
                hd = g * HG + h
                hs = slice(hd * HEAD_B, (hd + 1) * HEAD_B)
                rs = slice(h * TQ, (h + 1) * TQ)
                y = (gates[:, hd:hd + 1] * ocmp_ref[0, :, hs] * _silu(zc_ref[0, :, hs])
                     + gates[:, HB + hd:HB + hd + 1] * o[rs] * _silu(zs_ref[0, :, hs])
                     + gates[:, 2 * HB + hd:2 * HB + hd + 1] * ow[g][rs] * _silu(zw_ref[0, :, hs]))
                o_ref[0, :, hs] = y.astype(o_ref.dtype)


def nsa_selwin_sample(proj, selm, o_cmp, pool5, table, new_rows, cwin, slopes, *, pos0, n_new):
    B, TQ, _ = proj.shape
    HG = slopes.shape[0] // G_KV
    NP, CPP = pool5.shape[:2]
    PS = CPP * S_CMP
    n_pages = table.shape[1]
    NSBp = selm.shape[3]
    PGS = max(d for d in (8, 4, 2, 1) if n_pages % d == 0)
    CB = G_KV * HG * HEAD_B
    GW = G_KV * HEAD_B
    NN = new_rows.shape[1]
    n_win = cwin.shape[1]
    assert pos0 % L_SEL == 0 and n_new <= L_SEL and pos0 == n_pages * PS

    def page_map(i):
        return lambda b, s, pt: (pt[b, s * PGS + i], 0, 0, 1, 0)

    n_steps = n_pages // PGS
    bps = PGS * PS // L_SEL
    assert bps <= LANES
    selst = selm[:, :, :, :n_steps * bps].reshape(B, G_KV, TQ, n_steps, bps).transpose(0, 3, 1, 2, 4)
    selst = jnp.pad(selst.reshape(B, n_steps, G_KV * TQ, bps), ((0, 0), (0, 0), (0, 0), (0, LANES - bps)))

    const = lambda b, s, pt: (b, 0, 0)
    wide = lambda blk: pl.BlockSpec((1, TQ, CB), lambda b, s, pt: (b, 0, blk))
    grid_spec = pltpu.PrefetchScalarGridSpec(
        num_scalar_prefetch=1,
        grid=(B, n_steps),
        in_specs=[pl.BlockSpec(memory_space=pltpu.SMEM),
                  pl.BlockSpec((1, TQ, CB), const),
                  pl.BlockSpec((1, G_KV, TQ, NSBp), lambda b, s, pt: (b, 0, 0, 0)),
                  pl.BlockSpec((1, 1, G_KV * TQ, LANES), lambda b, s, pt: (b, s, 0, 0))]
                 + [pl.BlockSpec((1, CPP, S_CMP, 2 * G_KV, HEAD_B), page_map(i)) for i in range(PGS)]
                 + [pl.BlockSpec((1, NN, 6 * GW), const),
                    pl.BlockSpec((1, n_win, 2 * GW), const),
                    wide(0), wide(1), wide(2), wide(3),
                    pl.BlockSpec((1, TQ, LANES), lambda b, s, pt: (b, 0, 4 * CB // LANES))],
        out_specs=wide(0),
        scratch_shapes=[pltpu.VMEM((G_KV, HG * TQ, 1), F32),
                        pltpu.VMEM((G_KV, HG * TQ, 1), F32),
                        pltpu.VMEM((G_KV, HG * TQ, HEAD_B), F32)],
    )
    return pl.pallas_call(
        functools.partial(_nsa_selwin_sample_kernel, PGS=PGS, PS=PS, HG=HG, TQ=TQ, pos0=pos0,
                          n_new=n_new, n_win=n_win),
        grid_spec=grid_spec,
        out_shape=jax.ShapeDtypeStruct((B, TQ, CB), F32),
        compiler_params=_cparams(("parallel", "arbitrary")),
        name="nsa_selwin_sample",
    )(table, slopes, proj, selm, selst, *([pool5] * PGS), new_rows, cwin, o_cmp, proj, proj, proj, proj)


def _rwkv_layer(h, x_prev, s0, i, W, B, T):
    N, D = h.shape
    g = W["norm_g"][i]
    rkvg = rwkv_in(h, x_prev, g, W["mu_a"][i], W["w_in_a"], i, T)
    CA = rkvg.shape[-1]
    lw, a = rwkv_lora(h, x_prev, g, W["mu_a"][i], W["w_lora_w1"][i], W["w_lora_w2"][i], W["a_lora1"][i],
                      W["a_lora2"][i], W["w0_a"][i], W["a0_a"][i], T)
    pvec = jnp.stack([W["k_k"][i], W["k_a"][i], W["r_k"][i].reshape(CA), W["ln_x_w"][i], W["ln_x_b"][i]])
    o, s_fin = rwkv_scan(rkvg.reshape(4, B, T, CA), lw.reshape(B, T, CA), a.reshape(B, T, CA), pvec, s0)
    last = rmsnorm(h.reshape(B, T, D)[:, -1], g)
    return o.reshape(N, CA), s_fin, last


def _nsa_layer(h, jb, shared, W, slopes, B, T, norm_g):
    N, D = h.shape
    CB = W["w_out_b"].shape[1]
    proj3 = norm_mm(h, norm_g, W["w_in_b"], (jb,)).reshape(B, T, -1)
    if shared["past"] is None:
        o = nsa_prompt(proj3, shared["rows"], shared["kvc"], slopes, tq=min(T, ATTN_TQ),
                       nc=shared["nc"], nsb=shared["nsb"])
    else:
        TQ = SUBLANES
        projp = jnp.pad(proj3, ((0, 0), (0, TQ - T), (0, 0)))
        o_cmp, selm = nsa_cmp(projp, shared["kvc"], slopes, tq=TQ, nc=shared["nc"],
                              nsb=shared["nsb"], pos0=shared["pos0"])
        pool, table, cwin = shared["past"]
        o = nsa_selwin_sample(projp, selm, o_cmp, pool, table, shared["new_rows"], cwin, slopes,
                              pos0=shared["pos0"], n_new=T)[:, :T]
    return o.reshape(N, CB)


def _trunk(x, p, pos0, wkv0, shift0, past, W, slopes):
    B, T, D = x.shape
    N = B * T
    depth = p.shape[0]
    n_a = W["w_in_a"].shape[0]
    GW = G_KV * HEAD_B
    h = x.reshape(N, D)
    wkv_new, shift_new = [], []
    shared, kv_rows, win_state = None, None, None
    for i in range(depth):
        if i < n_a:
            o, s_fin, last = _rwkv_layer(h, shift0[i], wkv0[i], i, W, B, T)
            wkv_new.append(s_fin)
            shift_new.append(last)
            h = out_ple(o, h, p[i].reshape(N, -1), W["w_out_a"], i, W["w_ple"], W["w_ple_gate"], i)
        else:
            o = _nsa_layer(h, i - n_a, shared, W, slopes, B, T, W["norm_g"][i])
            h = out_ple(o, h, p[i].reshape(N, -1), W["w_out_b"], i - n_a, W["w_ple"], W["w_ple_gate"], i)
        if i == n_a - 1:
            rows = norm_mm(h, W["kv_norm_g"], W["w_kv"]).reshape(B, T, 6 * GW)
            kv_rows = rows[:, :, :4 * GW].reshape(B, T, 4, G_KV, HEAD_B)
            win_new = rows[:, :, 4 * GW:].reshape(B, T, 2, G_KV, HEAD_B)
            if past is None:
                PS = 128
                pool = rows.reshape(B * T // PS, PS // S_CMP, S_CMP, 6 * G_KV, HEAD_B)
                table = jnp.arange(B * T // PS, dtype=jnp.int32).reshape(B, T // PS)
                t_all = T
                win_all = win_new
                shared = {"past": None, "rows": rows}
            else:
                pool, table, cwin = past
                PS = pool.shape[1] * S_CMP
                t_all = pos0 + T
                win_all = jnp.concatenate([cwin.reshape(B, -1, 2, G_KV, HEAD_B), win_new], axis=1)
                NN = LANES
                shared = {"past": past, "new_rows": jnp.pad(rows, ((0, 0), (0, NN - T), (0, 0)))}
            win_state = win_all[:, win_all.shape[1] - min(WINDOW, pos0 + T):]
            nc = (t_all - L_CMP) // S_CMP + 1
            assert nc < table.shape[1] * PS // S_CMP
            kvc = compress_kv(pool, table, W["pe_cmp"], W["w_cmp1"], W["w_cmp2"])
            shared.update(kvc=kvc, nc=nc, nsb=max(-(-t_all // L_SEL), TOPK_SEL), pos0=pos0)
    y = rmsnorm(h, W["final_norm_g"]).reshape(B, T, D)
    return y, jnp.stack(wkv_new), jnp.stack(shift_new), kv_rows, win_state


def kernel(x_prompt, x_sample, state_wkv, state_shift, cache_kv, cache_win_kv, page_table, p_prompt, p_sample, norm_g, mu_a, w_in_a, w_lora_w1, w_lora_w2, w0_a, a_lora1, a_lora2, a0_a, k_k, k_a, r_k, ln_x_w, ln_x_b, w_out_a, w_in_b, w_out_b, kv_norm_g, w_kv, pe_cmp, w_cmp1, w_cmp2, w_ple, w_ple_gate, final_norm_g):
    bf = lambda w: w.astype(BF16)
    CA = w_out_a.shape[1]
    W = dict(norm_g=norm_g, mu_a=mu_a, w_in_a=bf(w_in_a), w_lora_w1=bf(w_lora_w1), w_lora_w2=bf(w_lora_w2),
             w0_a=w0_a, a_lora1=bf(a_lora1), a_lora2=bf(a_lora2), a0_a=a0_a, k_k=k_k, k_a=k_a,
             r_k=r_k, ln_x_w=ln_x_w, ln_x_b=ln_x_b,
             w_out_a=bf(w_out_a), w_in_b=bf(w_in_b), w_out_b=bf(w_out_b), kv_norm_g=kv_norm_g, w_kv=bf(w_kv),
             pe_cmp=pe_cmp, w_cmp1=bf(w_cmp1), w_cmp2=bf(w_cmp2), w_ple=bf(w_ple), w_ple_gate=bf(w_ple_gate),
             final_norm_g=final_norm_g)
    HB = w_out_b.shape[1] // HEAD_B
    slopes = 2.0 ** (-8.0 * jnp.arange(1, HB + 1, dtype=F32) / HB)
    bp = x_prompt.shape[0]
    n_a = w_in_a.shape[0]
    D = x_prompt.shape[-1]
    wkv0 = jnp.zeros((n_a, bp, CA // HEAD_A, HEAD_A, HEAD_A), F32)
    shift0 = jnp.zeros((n_a, bp, D), F32)
    y_p, wkv_p, shift_p, kv_p, win_p = _trunk(x_prompt, p_prompt, 0, wkv0, shift0, None, W, slopes)
    db, n_pages = page_table.shape
    NP, PS = cache_kv.shape[:2]
    pool5 = cache_kv.reshape(NP, PS // S_CMP, S_CMP, -1, HEAD_B)
    past = (pool5, page_table, cache_win_kv.reshape(db, cache_win_kv.shape[1], -1))
    y_s, wkv_s, shift_s, kv_s, win_s = _trunk(x_sample, p_sample, n_pages * PS, state_wkv, state_shift, past, W, slopes)
    return (y_p, y_s, wkv_p, shift_p, kv_p, win_p, wkv_s, shift_s, kv_s, win_s)
```

```python
import functools
import math

import jax
import jax.numpy as jnp
from jax import lax
from jax.experimental import pallas as pl
from jax.experimental.pallas import tpu as pltpu

F32 = jnp.float32
BF16 = jnp.bfloat16

HEAD_A = 64
GN_EPS = 64e-5
HEAD_B = 128
G_KV = 4
L_CMP = 32
S_CMP = 16
L_SEL = 64
TOPK_SEL = 16
WINDOW = 512
RMS_EPS = 1e-6
NEG = -1e30
FORCE_BONUS = 1e4

LANES = 128
SUBLANES = 8
VMEM_LIMIT = 56 * 1024 * 1024

SCAN_NH = 2
SCAN_C = 64
SCAN_GP = 16
SCAN_CPS = 2

ROW_TILE = 1024
COL_TILE = 1024
COL_TILE_K4 = 512
FEW_ROWS = 64
WIDE_COLS = 4096
ROW_CHUNK = 256
ATTN_TQ = 256
SEL_SEG = 512

NT_DIMS = (((1,), (1,)), ((), ()))
TN_DIMS = (((0,), (0,)), ((), ()))


def _cparams(sem):
    return pltpu.CompilerParams(dimension_semantics=sem, vmem_limit_bytes=VMEM_LIMIT)


def _bdot(a, b):
    return jnp.dot(a.astype(BF16), b.astype(BF16), preferred_element_type=F32)


def _bdot_nt(a, b):
    return lax.dot_general(a.astype(BF16), b.astype(BF16), NT_DIMS, preferred_element_type=F32)


def _bdot_tn(a, b):
    return lax.dot_general(a.astype(BF16), b.astype(BF16), TN_DIMS, preferred_element_type=F32)


def _rms_kernel(x_ref, g_ref, o_ref):
    x = x_ref[...]
    ms = jnp.mean(x * x, axis=-1, keepdims=True)
    o_ref[...] = x * lax.rsqrt(ms + RMS_EPS) * g_ref[...]


def rmsnorm(x, g):
    M, D = x.shape
    tm = min(M, ROW_CHUNK)
    return pl.pallas_call(
        _rms_kernel,
        grid=(pl.cdiv(M, tm),),
        in_specs=[pl.BlockSpec((tm, D), lambda i: (i, 0)),
                  pl.BlockSpec((1, D), lambda i: (0, 0))],
        out_specs=pl.BlockSpec((tm, D), lambda i: (i, 0)),
        out_shape=jax.ShapeDtypeStruct((M, D), F32),
        compiler_params=_cparams(("parallel",)),
        name="rmsnorm",
    )(x, g.reshape(1, D))


def _col_tile(n_rows, n_cols, tn):
    return tn if n_rows > FEW_ROWS else min(n_cols, WIDE_COLS)


def _norm_rows(x, g):
    return x * lax.rsqrt(jnp.mean(x * x, axis=-1, keepdims=True) + RMS_EPS) * g


def _sigmoid(x):
    return 0.5 + 0.5 * jnp.tanh(0.5 * x)


def _norm_and_shift(h_ref, hprev_ref, xprev_ref, g_ref, i, tm, T):
    g = g_ref[...]
    hn = _norm_rows(h_ref[...], g)
    prev_row = _norm_rows(hprev_ref[SUBLANES - 1:SUBLANES, :], g)
    row = lax.broadcasted_iota(jnp.int32, (tm, 1), 0)
    xs = jnp.where(row == 0, prev_row, pltpu.roll(hn, 1, axis=0))
    if T >= tm:
        assert T % tm == 0
        start = (i * tm) % T == 0
        xs = jnp.where((row == 0) & start, xprev_ref[pl.ds((i * tm) // T, 1), :], xs)
    else:
        assert tm % T == 0
        for bb in range(tm // T):
            xs = jnp.where(row == bb * T, xprev_ref[pl.ds(i * (tm // T) + bb, 1), :], xs)
    return hn, xs


def _rwkv_in_kernel(h_ref, hprev_ref, xprev_ref, g_ref, mu_ref, w_ref, o_ref, hn_scr, xs_scr, *, tm, T):
    i, j, n = pl.program_id(0), pl.program_id(1), pl.program_id(2)
    rc = min(tm, ROW_CHUNK)
    chunks = [slice(c * rc, (c + 1) * rc) for c in range(tm // rc)]
    P = SUBLANES

    @pl.when((j == 0) & (n == 0))
    def _():
        g = g_ref[...]
        prev = _norm_rows(hprev_ref[P - 1:P, :], g)
        if T >= tm:
            assert T % tm == 0
            prev = jnp.where((i * tm) % T == 0, xprev_ref[pl.ds((i * tm) // T, 1), :], prev)
        hn_scr[0:P, :] = jnp.broadcast_to(prev, (P, prev.shape[1]))
        for ch in chunks:
            hn_scr[P + ch.start:P + ch.stop, :] = _norm_rows(h_ref[ch, :], g)

    @pl.when(n == 0)
    def _():
        mu = mu_ref[pl.ds(j, 1), :]
        for ch in chunks:
            hn = hn_scr[P + ch.start:P + ch.stop, :]
            xs = hn_scr[P - 1 + ch.start:P - 1 + ch.stop, :]
            if T < tm:
                assert tm % T == 0 and len(chunks) == 1
                row = lax.broadcasted_iota(jnp.int32, (tm, 1), 0)
                for bb in range(tm // T):
                    xs = jnp.where(row == bb * T, xprev_ref[pl.ds(i * (tm // T) + bb, 1), :], xs)
            xs_scr[ch, :] = (hn + (xs - hn) * mu).astype(BF16)

    o_ref[...] = jnp.dot(xs_scr[...], w_ref[...], preferred_element_type=F32)


def _shift_specs(tm, D, nb, ngrid):
    z = (0,) * (ngrid - 1)
    wrap = lambda f: (lambda i, *_: f(i))
    return [pl.BlockSpec((tm, D), wrap(lambda i: (i, 0))),
            pl.BlockSpec((SUBLANES, D), wrap(lambda i: (jnp.maximum(i * (tm // SUBLANES) - 1, 0), 0))),
            pl.BlockSpec((nb, D), wrap(lambda i: (0, 0))),
            pl.BlockSpec((1, D), wrap(lambda i: (0, 0))),
            pl.BlockSpec((6, D), wrap(lambda i: (0, 0)))]


def rwkv_in(h, xprev, g, mu, w, layer, T):
    N, D = h.shape
    C = w.shape[-1]
    tm = min(N, ROW_TILE)
    tn = _col_tile(N, C, COL_TILE)
    return pl.pallas_call(
        functools.partial(_rwkv_in_kernel, tm=tm, T=T),
        grid=(N // tm, 4, C // tn),
        in_specs=_shift_specs(tm, D, xprev.shape[0], 3) + [
            pl.BlockSpec((None, None, D, tn), lambda i, j, n: (layer, j, 0, n))],
        out_specs=pl.BlockSpec((None, tm, tn), lambda i, j, n: (j, i, n)),
        out_shape=jax.ShapeDtypeStruct((4, N, C), F32),
        scratch_shapes=[pltpu.VMEM((tm + SUBLANES, D), F32), pltpu.VMEM((tm, D), BF16)],
        compiler_params=_cparams(("parallel", "arbitrary", "arbitrary")),
        name="rwkv_in",
    )(h, h, xprev, g.reshape(1, D), mu, w)


def _rwkv_lora_kernel(h_ref, hprev_ref, xprev_ref, g_ref, mu_ref, lw1_ref, lw2_ref, la1_ref, la2_ref,
                      w0_ref, a0_ref, lw_ref, a_ref, *, tm, T):
    hn, xs = _norm_and_shift(h_ref, hprev_ref, xprev_ref, g_ref, pl.program_id(0), tm, T)
    dx = xs - hn
    x4 = (hn + dx * mu_ref[4:5, :]).astype(BF16)
    x5 = (hn + dx * mu_ref[5:6, :]).astype(BF16)
    t4 = jnp.tanh(jnp.dot(x4, lw1_ref[...], preferred_element_type=F32)).astype(BF16)
    x = w0_ref[...] + jnp.dot(t4, lw2_ref[...], preferred_element_type=F32)
    lw_ref[...] = -math.exp(-0.5) * _sigmoid(x)
    t5 = jnp.dot(x5, la1_ref[...], preferred_element_type=F32).astype(BF16)
    a_ref[...] = _sigmoid(a0_ref[...] + jnp.dot(t5, la2_ref[...], preferred_element_type=F32))


def rwkv_lora(h, xprev, g, mu, lw1, lw2, la1, la2, w0, a0, T):
    N, D = h.shape
    R, C = lw2.shape
    tm = min(N, ROW_CHUNK)
    full = lambda shape: pl.BlockSpec(shape, lambda i: (0,) * len(shape))
    o_spec = pl.BlockSpec((tm, C), lambda i: (i, 0))
    return pl.pallas_call(
        functools.partial(_rwkv_lora_kernel, tm=tm, T=T),
        grid=(N // tm,),
        in_specs=_shift_specs(tm, D, xprev.shape[0], 1) + [
            full((D, R)), full((R, C)), full((D, R)), full((R, C)), full((1, C)), full((1, C))],
        out_specs=[o_spec, o_spec],
        out_shape=[jax.ShapeDtypeStruct((N, C), F32)] * 2,
        compiler_params=_cparams(("parallel",)),
        name="rwkv_lora",
    )(h, h, xprev, g.reshape(1, D), mu, lw1, lw2, la1, la2, w0.reshape(1, C), a0.reshape(1, C))


def _norm_mm_kernel(h_ref, g_ref, w_ref, o_ref, xs_scr):
    @pl.when(pl.program_id(1) == 0)
    def _():
        rc = min(h_ref.shape[0], ROW_CHUNK)

        def norm_chunk(c, carry):
            rows = pl.ds(pl.multiple_of(c * rc, rc), rc)
            xs_scr[rows, :] = _norm_rows(h_ref[rows, :], g_ref[...]).astype(BF16)
            return carry

        lax.fori_loop(0, h_ref.shape[0] // rc, norm_chunk, 0)

    o_ref[...] = jnp.dot(xs_scr[...], w_ref[...], preferred_element_type=F32)


def norm_mm(h, g, w, widx=()):
    N, D = h.shape
    NO = w.shape[-1]
    tm = min(N, ROW_TILE)
    tn = _col_tile(N, NO, COL_TILE)
    if N > FEW_ROWS:
        tn = min(range(COL_TILE, COL_TILE + 3 * LANES, LANES), key=lambda t: -(-NO // t) * t)
    nlead = len(widx)
    return pl.pallas_call(
        _norm_mm_kernel,
        grid=(N // tm, pl.cdiv(NO, tn)),
        in_specs=[pl.BlockSpec((tm, D), lambda i, n: (i, 0)),
                  pl.BlockSpec((1, D), lambda i, n: (0, 0)),
                  pl.BlockSpec((None,) * nlead + (D, tn), lambda i, n: tuple(widx) + (0, n))],
        out_specs=pl.BlockSpec((tm, tn), lambda i, n: (i, n)),
        out_shape=jax.ShapeDtypeStruct((N, NO), F32),
        scratch_shapes=[pltpu.VMEM((tm, D), BF16)],
        compiler_params=_cparams(("parallel", "arbitrary")),
        name="norm_mm",
    )(h, g.reshape(1, D), w)


def _mm_res_kernel(x_ref, h_ref, w_ref, o_ref, ob_ref):
    h1 = h_ref[...] + jnp.dot(x_ref[...].astype(BF16), w_ref[...], preferred_element_type=F32)
    o_ref[...] = h1
    ob_ref[...] = h1.astype(BF16)


def _ple_gate_kernel(h1b_ref, h1_ref, p_ref, wp_ref, wg_ref, o_ref):
    gate = jnp.dot(h1b_ref[...], wg_ref[...], preferred_element_type=F32)
    ple = jnp.dot(p_ref[...].astype(BF16), wp_ref[...], preferred_element_type=F32)
    o_ref[...] = h1_ref[...] + ple * _sigmoid(gate)


def out_ple(x, h, p, w_out, oidx, w_ple, w_gate, layer):
    N, C = x.shape
    D = h.shape[1]
    DP = p.shape[1]
    tm = min(N, ROW_TILE)
    tn = _col_tile(N, D, COL_TILE_K4)
    tile = pl.BlockSpec((tm, tn), lambda i, n: (i, n))
    h1, h1b = pl.pallas_call(
        _mm_res_kernel,
        grid=(N // tm, D // tn),
        in_specs=[pl.BlockSpec((tm, C), lambda i, n: (i, 0)), tile,
                  pl.BlockSpec((None, C, tn), lambda i, n: (oidx, 0, n))],
        out_specs=[tile, tile],
        out_shape=[jax.ShapeDtypeStruct((N, D), F32), jax.ShapeDtypeStruct((N, D), BF16)],
        compiler_params=_cparams(("parallel", "parallel")),
        name="mm_res",
    )(x, h, w_out)
    tg = _col_tile(N, D, COL_TILE)
    gtile = pl.BlockSpec((tm, tg), lambda i, n: (i, n))
    return pl.pallas_call(
        _ple_gate_kernel,
        grid=(N // tm, D // tg),
        in_specs=[pl.BlockSpec((tm, D), lambda i, n: (i, 0)), gtile,
                  pl.BlockSpec((tm, DP), lambda i, n: (i, 0)),
                  pl.BlockSpec((None, DP, tg), lambda i, n: (layer, 0, n)),
                  pl.BlockSpec((None, D, tg), lambda i, n: (layer, 0, n))],
        out_specs=gtile,
        out_shape=jax.ShapeDtypeStruct((N, D), F32),
        compiler_params=_cparams(("parallel", "parallel")),
        name="ple_gate",
    )(h1b, h1, p, w_ple, w_gate)


def _scan_kernel(r_ref, k_ref, v_ref, zg_ref, lw_ref, a_ref, pv_ref, s0_ref, o_ref, sfin_ref, s_scr, *, C, NH, GP, CPS):
    L = NH * HEAD_A
    NC = NH * C
    ci = pl.program_id(2)

    @pl.when(ci == 0)
    def _():
        for gp in range(GP):
            rows = []
            for hh in range(NH):
                pieces = [s0_ref[0, gp * NH + hh] if h2 == hh else jnp.zeros((HEAD_A, HEAD_A), F32)
                          for h2 in range(NH)]
                rows.append(jnp.concatenate(pieces, axis=1))
            s_scr[gp] = jnp.concatenate(rows, axis=0)

    row_c = lax.broadcasted_iota(jnp.int32, (C, NC), 0)
    col_s = lax.broadcasted_iota(jnp.int32, (C, NC), 1) % C
    tri_strict = col_s < row_c
    tri_incl = (lax.broadcasted_iota(jnp.int32, (C, 2 * NC), 1) % C
                <= lax.broadcasted_iota(jnp.int32, (C, 2 * NC), 0))
    st_mask = (lax.broadcasted_iota(jnp.int32, (NC, L), 0) // C
               == lax.broadcasted_iota(jnp.int32, (NC, L), 1) // HEAD_A)
    bd_mask = (lax.broadcasted_iota(jnp.int32, (NC, NC), 0) // C
               == lax.broadcasted_iota(jnp.int32, (NC, NC), 1) // C)
    head_mask = (lax.broadcasted_iota(jnp.int32, (L, L), 0) // HEAD_A
                 == lax.broadcasted_iota(jnp.int32, (L, L), 1) // HEAD_A)

    def st(x):
        return jnp.where(st_mask, jnp.concatenate([x] * NH, axis=0), 0.0)

    def bd(w):
        return jnp.where(bd_mask, jnp.concatenate([w] * NH, axis=0), 0.0)

    n_double = int(math.log2(C))
    each = lambda f, *cols: [f(*xs) for xs in zip(*cols)]
    sls = [slice(gp * L, (gp + 1) * L) for gp in range(GP)]
    head_of_lane = lax.broadcasted_iota(jnp.int32, (1, L), 1) // HEAD_A

    def hsum(x):
        out = None
        for hh in range(NH):
            sh = jnp.sum(jnp.where(head_of_lane == hh, x, 0.0), axis=-1, keepdims=True)
            out = sh if out is None else jnp.where(head_of_lane == hh, sh, out)
        return out

    k_k, k_a, r_k, ln_w, ln_b = ([pv_ref[n:n + 1, sl] for sl in sls] for n in range(5))
    cum_mat = (lax.broadcasted_iota(jnp.int32, (C, C), 1)
               <= lax.broadcasted_iota(jnp.int32, (C, C), 0)).astype(BF16)
    for cc in range(CPS):
        rs = slice(cc * C, (cc + 1) * C)
        lw = [lw_ref[0, rs, sl] for sl in sls]
        a_sig = [a_ref[0, rs, sl] for sl in sls]
        r = [r_ref[0, rs, sl] for sl in sls]
        k_raw = [k_ref[0, rs, sl] for sl in sls]
        v = [v_ref[0, rs, sl] for sl in sls]
        kk = each(lambda x, w: x * w, k_raw, k_k)
        kk = each(lambda x: x * lax.rsqrt(jnp.maximum(hsum(x * x), 1e-24)), kk)
        k = each(lambda x, a, w: x * (1.0 + (a - 1.0) * w), k_raw, a_sig, k_a)
        lw_hi = each(lambda z: z.astype(BF16), lw)
        lw_lo = each(lambda z, hi: (z - hi.astype(F32)).astype(BF16), lw, lw_hi)
        cum2 = each(lambda hi, lo: jnp.dot(cum_mat, jnp.concatenate([hi, lo], axis=1), preferred_element_type=F32),
                    lw_hi, lw_lo)
        cum = [c2[:, :L] + c2[:, L:] for c2 in cum2]
        p_incl = each(jnp.exp, cum)
        p_inv = each(lambda z: jnp.exp(-z), cum)
        at = each(lambda x, c, w: -x * jnp.exp(c - w), kk, cum, lw)
        rt = each(lambda x, p: x * p, r, p_incl)
        bt = each(lambda x, a, p: x * a * p, kk, a_sig, p_inv)
        kt = each(lambda x, p: x * p, k, p_inv)
        S = [s_scr[gp] for gp in range(GP)]
        ar = each(lambda x, y: jnp.concatenate([x, y], axis=0), at, rt)
        bk_st = each(lambda x, y: jnp.concatenate([st(x), st(y)], axis=0), bt, kt)
        Gm = each(_bdot_nt, ar, bk_st)
        w_ab = [jnp.where(tri_strict, g[:C, :NC], 0.0) for g in Gm]
        tm = w_ab
        pw = each(lambda w: _bdot(w, bd(w)), w_ab)
        LH = each(_bdot_nt, ar, S)
        v_st = each(st, v)
        x = [lh[:C] + _bdot(jnp.where(tri_strict, g[:C, NC:], 0.0), vs) for lh, g, vs in zip(LH, Gm, v_st)]
        for it in range(1, n_double):
            if it < n_double - 1:
                both = each(lambda t, p: _bdot(p, jnp.concatenate([bd(t), bd(p)], axis=1)), tm, pw)
                tm = each(lambda t, p, b2: t + p + b2[:, :NC], tm, pw, both)
                pw = [b2[:, NC:] for b2 in both]
            else:
                tm = each(lambda t, p: t + p + _bdot(p, bd(t)), tm, pw)
        u = each(lambda xx, t: xx + _bdot(t, st(xx)), x, tm)
        o = [lh[C:] + _bdot(jnp.where(tri_incl, g[C:], 0.0), jnp.concatenate([st(uu), vs], axis=0))
             for lh, g, uu, vs in zip(LH, Gm, u, v_st)]
        inv_n = 1.0 / HEAD_A
        dev = each(lambda x: x - hsum(x) * inv_n, o)
        gn = each(lambda d, w, b_: d * lax.rsqrt(hsum(d * d) * inv_n + GN_EPS) * w + b_, dev, ln_w, ln_b)
        bonus = each(lambda rr, kx, w, vv: hsum(rr * kx * w) * vv, r, k, r_k, v)
        for sl, y, bo in zip(sls, gn, bonus):
            zg = zg_ref[0, rs, sl]
            o_ref[0, rs, sl] = ((y + bo) * (zg * _sigmoid(zg))).astype(o_ref.dtype)
        ds = [_bdot_tn(jnp.concatenate([uu, vv], axis=0), jnp.concatenate([b_, k_], axis=0))
              for uu, vv, b_, k_ in zip(u, v, bt, kt)]
        for gp in range(GP):
            s_scr[gp] = (S[gp] + jnp.where(head_mask, ds[gp], 0.0)) * p_incl[gp][C - 1:C, :]

    @pl.when(ci == pl.num_programs(2) - 1)
    def _():
        for gp in range(GP):
            s_all = s_scr[gp]
            for hh in range(NH):
                blk = slice(hh * HEAD_A, (hh + 1) * HEAD_A)
                sfin_ref[0, gp * NH + hh] = s_all[blk, blk]


def rwkv_scan(rkvg, lw, a, pvec, s0):
    _, B, T, CA = rkvg.shape
    H = CA // HEAD_A
    NH, C, GP = SCAN_NH, SCAN_C, SCAN_GP
    assert NH * C == LANES and H % (NH * GP) == 0
    L = NH * HEAD_A
    NG = H // NH
    CPS = SCAN_CPS if T >= SCAN_CPS * C else 1
    TS = CPS * C
    Tp = -(-T // TS) * TS
    if Tp != T:
        rkvg = jnp.pad(rkvg, ((0, 0), (0, 0), (0, Tp - T), (0, 0)))
        lw, a = (jnp.pad(z, ((0, 0), (0, Tp - T), (0, 0))) for z in (lw, a))
    seq_spec = pl.BlockSpec((1, TS, GP * L), lambda bi, gi, ci: (bi, ci, gi))
    proj_spec = lambda j: pl.BlockSpec((None, 1, TS, GP * L), lambda bi, gi, ci: (j, bi, ci, gi))
    st_spec = pl.BlockSpec((1, GP * NH, HEAD_A, HEAD_A), lambda bi, gi, ci: (bi, gi, 0, 0))
    o, s_fin = pl.pallas_call(
        functools.partial(_scan_kernel, C=C, NH=NH, GP=GP, CPS=CPS),
        grid=(B, NG // GP, Tp // TS),
        in_specs=[proj_spec(j) for j in range(4)] + [seq_spec, seq_spec,
                  pl.BlockSpec((5, GP * L), lambda bi, gi, ci: (0, gi)), st_spec],
        out_specs=[seq_spec, st_spec],
        out_shape=[jax.ShapeDtypeStruct((B, Tp, CA), BF16), jax.ShapeDtypeStruct((B, H, HEAD_A, HEAD_A), F32)],
        scratch_shapes=[pltpu.VMEM((GP, L, L), F32)],
        compiler_params=_cparams(("parallel", "parallel", "arbitrary")),
        name="rwkv_scan",
    )(rkvg, rkvg, rkvg, rkvg, lw, a, pvec, s0)
    return o[:, :T], s_fin


def _gelu_tanh(x):
    c = math.sqrt(2.0 / math.pi)
    return 0.5 * x * (1.0 + jnp.tanh(c * (x + 0.044715 * (x * x * x))))


def _compress_kernel(pt_ref, *refs, PGS):
    del pt_ref
    page_refs = refs[:PGS]
    next_ref, pe_ref, w1_ref, w2_ref, out_ref = refs[PGS:]
    CPP = page_refs[0].shape[1]
    NCH = PGS * CPP
    CG = 2 * G_KV
    M = (NCH + 1) * CG

    def rows_of(l, hf):
        pe = pe_ref[hf, l]
        parts = [(page_refs[i][0, :, l] + pe[None]).reshape(CPP * CG, HEAD_B) for i in range(PGS)]
        parts.append(next_ref[0, 0, l] + pe)
        return jnp.concatenate(parts, axis=0)

    top = jnp.zeros((M, 2 * HEAD_B), F32)
    bot = jnp.zeros((M, 2 * HEAD_B), F32)
    for l in range(0, S_CMP, 2):
        wrows = pl.ds(l * HEAD_B, 2 * HEAD_B)
        xt = jnp.concatenate([rows_of(l, 0), rows_of(l + 1, 0)], axis=1).astype(BF16)
        top = top + jnp.dot(xt, w1_ref[0, wrows, :], preferred_element_type=F32)
        xb = jnp.concatenate([rows_of(l, 1), rows_of(l + 1, 1)], axis=1).astype(BF16)
        bot = bot + jnp.dot(xb, w1_ref[1, wrows, :], preferred_element_type=F32)
    is_k = (lax.broadcasted_iota(jnp.int32, (M, 1), 0) % CG) < G_KV
    pick = lambda z, n: jnp.where(is_k[:n], z[:n, :HEAD_B], z[:n, HEAD_B:])
    hcur = pick(top, NCH * CG) + pick(bot, M)[CG:]
    o2 = jnp.dot(_gelu_tanh(hcur).astype(BF16), w2_ref[...], preferred_element_type=F32)
    out_ref[0] = pick(o2, NCH * CG).reshape(NCH, CG, HEAD_B)


def compress_kv(pool5, table, pe_cmp, w1, w2):
    NP, CPP = pool5.shape[:2]
    B, n_pages = table.shape
    PGS = max(d for d in (16, 8, 4, 2, 1) if n_pages % d == 0)
    NCH = PGS * CPP
    CG = 2 * G_KV
    half = S_CMP * HEAD_B
    pe_r = jnp.repeat(pe_cmp.reshape(2, 2, S_CMP, HEAD_B).transpose(1, 2, 0, 3), G_KV, axis=2)
    w1_r = w1.reshape(2, 2, half, HEAD_B).transpose(1, 2, 0, 3).reshape(2, half, 2 * HEAD_B)
    w2_r = jnp.concatenate([w2[0], w2[1]], axis=1)

    def page_map(i):
        return lambda b, s, pt: (pt[b, s * PGS + i], 0, 0, 0, 0)

    def next_map(b, s, pt):
        return (pt[b, jnp.minimum((s + 1) * PGS, n_pages - 1)], 0, 0, 0, 0)

    const = lambda n: (lambda b, s, pt: (0,) * n)
    grid_spec = pltpu.PrefetchScalarGridSpec(
        num_scalar_prefetch=1,
        grid=(B, n_pages // PGS),
        in_specs=[pl.BlockSpec((1, CPP, S_CMP, CG, HEAD_B), page_map(i)) for i in range(PGS)] + [
            pl.BlockSpec((1, 1, S_CMP, CG, HEAD_B), next_map),
            pl.BlockSpec((2, S_CMP, CG, HEAD_B), const(4)),
            pl.BlockSpec((2, half, 2 * HEAD_B), const(3)),
            pl.BlockSpec((HEAD_B, 2 * HEAD_B), const(2)),
        ],
        out_specs=pl.BlockSpec((1, NCH, CG, HEAD_B), lambda b, s, pt: (b, s, 0, 0)),
    )
    return pl.pallas_call(
        functools.partial(_compress_kernel, PGS=PGS),
        grid_spec=grid_spec,
        out_shape=jax.ShapeDtypeStruct((B, n_pages * CPP, CG, HEAD_B), F32),
        compiler_params=_cparams(("parallel", "arbitrary")),
        name="compress_kv",
    )(table, *([pool5] * PGS), pool5, pe_r, w1_r, w2_r)


def _stack_heads(q, HG):
    return jnp.concatenate([q[:, h * HEAD_B:(h + 1) * HEAD_B] for h in range(HG)], axis=0)


def _masked_softmax_rows(s, mask):
    s = jnp.where(mask, s, NEG)
    m = jnp.max(s, axis=-1, keepdims=True)
    e = jnp.where(mask, jnp.exp(s - m), 0.0)
    l = jnp.sum(e, axis=-1, keepdims=True)
    return e / jnp.where(l > 0.0, l, 1.0)


def _attend_stacked(s, dist, mask, v, slopes_ref, g, HG, tq):
    ps = []
    psum = jnp.zeros(dist.shape, F32)
    for h in range(HG):
        p = _masked_softmax_rows(s[h * tq:(h + 1) * tq] - slopes_ref[g * HG + h] * dist, mask)
        psum = psum + p
        ps.append(p.astype(BF16))
    o = jnp.dot(jnp.concatenate(ps, axis=0), v.astype(BF16), preferred_element_type=F32)
    return o, psum


def _unstack_store(o_ref, o, HG, tq):
    for h in range(HG):
        o_ref[0, :, h * HEAD_B:(h + 1) * HEAD_B] = o[h * tq:(h + 1) * tq]


def _cmp_branch(q, kc, vc, slopes_ref, g, qpos, *, tq, HG, nc, nsb, NSBp):
    NCp = kc.shape[0]
    q_st = _stack_heads(q * (HEAD_B ** -0.5), HG)
    s = _bdot_nt(q_st, kc)
    cidx = lax.broadcasted_iota(jnp.int32, (1, NCp), 1)
    cend = S_CMP * cidx + (L_CMP - 1)
    mask = (cend <= qpos) & (cidx < nc)
    dist = (qpos - cend).astype(F32)
    o, imp_c = _attend_stacked(s, dist, mask, vc, slopes_ref, g, HG, tq)

    crow = lax.broadcasted_iota(jnp.int32, (NCp, NSBp), 0)
    jcol = lax.broadcasted_iota(jnp.int32, (NCp, NSBp), 1)
    overlap = ((S_CMP * crow < L_SEL * (jcol + 1)) & (S_CMP * crow + L_CMP > L_SEL * jcol)
               & (crow < nc)).astype(F32)
    imp = jnp.dot(imp_c, overlap, precision=lax.Precision.HIGHEST, preferred_element_type=F32)
    lane = lax.broadcasted_iota(jnp.int32, (tq, NSBp), 1)
    cur = jnp.right_shift(qpos, int(math.log2(L_SEL)))
    forced = ((lane == 0) | (lane == cur) | (lane == cur - 1)).astype(F32)
    score = jnp.where(lane <= cur, imp + FORCE_BONUS * forced, NEG)
    score = jnp.where(lane < nsb, score, -3e38)

    if tq % LANES == 0 and NSBp == LANES:
        nr = -(-nsb // SUBLANES) * SUBLANES
        st = score.T[:nr]
        sub = lax.broadcasted_iota(jnp.int32, (nr, tq), 0)
        cnt = jnp.zeros((nr, tq), F32)
        for i in range(nsb):
            row = st[i:i + 1, :]
            beats = (row > st) | ((row == st) & (sub > i))
            cnt = cnt + jnp.where(beats, 1.0, 0.0)
        sel_t = jnp.where((cnt < TOPK_SEL) & (st > 0.5 * NEG), 1.0, 0.0)
        return o, jnp.concatenate([sel_t, jnp.zeros((NSBp - nr, tq), F32)], axis=0).T
    else:
        cnt = jnp.zeros((tq, NSBp), F32)
        for i in range(nsb):
            col = score[:, i:i + 1]
            beats = (col > score) | ((col == score) & (lane > i))
            cnt = cnt + jnp.where(beats, 1.0, 0.0)
        sel = (cnt < TOPK_SEL) & (score > 0.5 * NEG)
        return o, sel.astype(F32)


def _nsa_cmp_kernel(slopes_ref, q_ref, kc_ref, vc_ref, o_ref, selm_ref, *, tq, HG, nc, nsb, pos0):
    g = pl.program_id(1)
    qpos = pos0 + pl.program_id(2) * tq + lax.broadcasted_iota(jnp.int32, (tq, 1), 0)
    o, sel = _cmp_branch(q_ref[0], kc_ref[0], vc_ref[0], slopes_ref, g, qpos,
                         tq=tq, HG=HG, nc=nc, nsb=nsb, NSBp=selm_ref.shape[3])
    _unstack_store(o_ref, o, HG, tq)
    selm_ref[0, 0] = sel


def nsa_cmp(proj, kvc, slopes, *, tq, nc, nsb, pos0):
    B, T, _ = proj.shape
    HG = slopes.shape[0] // G_KV
    NCp = kvc.shape[1]
    NSBp = -(-nsb // LANES) * LANES
    gw = HG * HEAD_B
    return pl.pallas_call(
        functools.partial(_nsa_cmp_kernel, tq=tq, HG=HG, nc=nc, nsb=nsb, pos0=pos0),
        grid=(B, G_KV, T // tq),
        in_specs=[pl.BlockSpec(memory_space=pltpu.SMEM),
                  pl.BlockSpec((1, tq, gw), lambda b, g, t: (b, t, g)),
                  pl.BlockSpec((1, NCp, HEAD_B), lambda b, g, t: (b, 0, g)),
                  pl.BlockSpec((1, NCp, HEAD_B), lambda b, g, t: (b, 0, G_KV + g))],
        out_specs=[pl.BlockSpec((1, tq, gw), lambda b, g, t: (b, t, g)),
                   pl.BlockSpec((1, 1, tq, NSBp), lambda b, g, t: (b, g, t, 0))],
        out_shape=[jax.ShapeDtypeStruct((B, T, G_KV * gw), F32),
                   jax.ShapeDtypeStruct((B, G_KV, T, NSBp), F32)],
        compiler_params=_cparams(("parallel", "parallel", "parallel")),
        name="nsa_cmp",
    )(slopes, proj, kvc.reshape(B, NCp, -1), kvc.reshape(B, NCp, -1))


LOG2E = 1.4426950408889634


def _bf16_part(x):
    return x.astype(BF16).astype(F32)


def _alibi_lhs(q, slope_col):
    c = slope_col * LOG2E
    c1 = _bf16_part(c)
    c2 = _bf16_part(c - c1)
    c3 = _bf16_part(c - c1 - c2)
    lane = lax.broadcasted_iota(jnp.int32, q.shape, 1)
    extra = jnp.where((lane == 0) | (lane == 3), c1,
                      jnp.where((lane == 1) | (lane == 4), c2, jnp.where((lane == 2) | (lane == 5), c3, 0.0)))
    return jnp.concatenate([q, extra], axis=1).astype(BF16)


def _alibi_rhs(k, k0):
    pos = k0 + lax.broadcasted_iota(jnp.int32, k.shape, 0)
    lane = lax.broadcasted_iota(jnp.int32, k.shape, 1)
    hi = jnp.bitwise_and(pos, -L_SEL)
    extra = jnp.where(lane < 3, hi, jnp.where(lane < 6, pos - hi, 0)).astype(F32)
    return jnp.concatenate([k, extra], axis=1).astype(BF16)


def _with_ones(v):
    lane = lax.broadcasted_iota(jnp.int32, v.shape, 1)
    return jnp.concatenate([v, jnp.where(lane == 0, 1.0, 0.0)], axis=1).astype(BF16)


def _silu(x):
    return x * _sigmoid(x)


def _nsa_selwin_prompt_kernel(slopes_ref, q_ref, kc_ref, vc_ref, ks_ref, vs_ref, kw_ref, vw_ref,
                              zc_ref, zs_ref, zw_ref, gate_ref, o_ref, s_scr, *, tq, HG, T, WS, SEG, nc, nsb):
    g = pl.program_id(1)
    qt = pl.program_id(2)
    NSBp = -(-nsb // LANES) * LANES
    R = HG * tq
    qpos = qt * tq + lax.broadcasted_iota(jnp.int32, (tq, 1), 0)
    o_cmp, selm = _cmp_branch(q_ref[0], kc_ref[0], vc_ref[0], slopes_ref, g, qpos,
                              tq=tq, HG=HG, nc=nc, nsb=nsb, NSBp=NSBp)
    slope_col = jnp.concatenate([jnp.full((tq, 1), slopes_ref[g * HG + h], F32) for h in range(HG)], axis=0)
    tile_heads = lambda x: jnp.concatenate([x] * HG, axis=0)
    q2 = _alibi_lhs(_stack_heads(q_ref[0] * (HEAD_B ** -0.5 * LOG2E), HG), slope_col)

    selm_b = selm.astype(BF16)
    nseg = (qt * tq + tq + SEG - 1) // SEG

    def seg_scores(si, m):
        k0 = pl.multiple_of(si * SEG, SEG)
        kcol = k0 + lax.broadcasted_iota(jnp.int32, (NSBp, SEG), 1)
        expand = (jnp.right_shift(kcol, int(math.log2(L_SEL)))
                  == lax.broadcasted_iota(jnp.int32, (NSBp, SEG), 0)).astype(BF16)
        in_blk = jnp.dot(selm_b, expand, preferred_element_type=F32) > 0.5
        kpos = k0 + lax.broadcasted_iota(jnp.int32, (1, SEG), 1)
        mask = tile_heads(in_blk & (kpos <= qpos))
        s = lax.dot_general(q2, _alibi_rhs(ks_ref[0, pl.ds(k0, SEG), :], k0), NT_DIMS, preferred_element_type=F32)
        s = jnp.where(mask, s, NEG)
        s_scr[si] = s
        return jnp.maximum(m, jnp.max(s, axis=-1, keepdims=True))

    m = lax.fori_loop(0, nseg, seg_scores, jnp.full((R, 1), NEG, F32))

    def seg_pv(si, acc):
        k0 = pl.multiple_of(si * SEG, SEG)
        e = jnp.exp2(s_scr[si] - m).astype(BF16)
        return acc + jnp.dot(e, _with_ones(vs_ref[0, pl.ds(k0, SEG), :]), preferred_element_type=F32)

    acc = lax.fori_loop(0, nseg, seg_pv, jnp.zeros((R, 2 * HEAD_B), F32))
    o = acc[:, :HEAD_B] / acc[:, HEAD_B:HEAD_B + 1]

    start = pl.multiple_of(jnp.clip(qt * tq - WINDOW, 0, T - WS), SUBLANES)
    distw = qpos - (start + lax.broadcasted_iota(jnp.int32, (1, WS), 1))
    maskw = tile_heads((distw >= 0) & (distw < WINDOW))
    sw = lax.dot_general(q2, _alibi_rhs(kw_ref[0, pl.ds(start, WS), :], start), NT_DIMS, preferred_element_type=F32)
    sw = jnp.where(maskw, sw, NEG)
    ew = jnp.exp2(sw - jnp.max(sw, axis=-1, keepdims=True)).astype(BF16)
    accw = jnp.dot(ew, _with_ones(vw_ref[0, pl.ds(start, WS), :]), preferred_element_type=F32)
    ow = accw[:, :HEAD_B] / accw[:, HEAD_B:HEAD_B + 1]

    HB = G_KV * HG
    gates = _sigmoid(gate_ref[0])
    lane = lax.broadcasted_iota(jnp.int32, gates.shape, 1)
    gate_col = lambda idx: jnp.sum(jnp.where(lane == idx, gates, 0.0), axis=1, keepdims=True)
    for h in range(HG):
        hs = slice(h * HEAD_B, (h + 1) * HEAD_B)
        rs = slice(h * tq, (h + 1) * tq)
        hd = g * HG + h
        y = (gate_col(hd) * o_cmp[rs] * _silu(zc_ref[0, :, hs])
             + gate_col(HB + hd) * o[rs] * _silu(zs_ref[0, :, hs])
             + gate_col(2 * HB + hd) * ow[rs] * _silu(zw_ref[0, :, hs]))
        o_ref[0, :, hs] = y.astype(o_ref.dtype)


def nsa_prompt(proj, rows, kvc, slopes, *, tq, nc, nsb):
    B, T, _ = proj.shape
    HG = slopes.shape[0] // G_KV
    NCp = kvc.shape[1]
    gw = HG * HEAD_B
    CB = G_KV * gw
    WS = min(T, WINDOW + tq)
    SEG = min(T, SEL_SEG)
    assert T % SEG == 0
    kvc2 = kvc.reshape(B, NCp, -1)
    kv_spec = lambda c: pl.BlockSpec((1, T, HEAD_B), lambda b, g, t: (b, 0, c * G_KV + g))
    cmp_spec = lambda c: pl.BlockSpec((1, NCp, HEAD_B), lambda b, g, t: (b, 0, c * G_KV + g))
    head_spec = lambda blk: pl.BlockSpec((1, tq, gw), lambda b, g, t: (b, t, blk * G_KV + g))
    return pl.pallas_call(
        functools.partial(_nsa_selwin_prompt_kernel, tq=tq, HG=HG, T=T, WS=WS, SEG=SEG, nc=nc, nsb=nsb),
        grid=(B, G_KV, T // tq),
        in_specs=[pl.BlockSpec(memory_space=pltpu.SMEM),
                  head_spec(0), cmp_spec(0), cmp_spec(1),
                  kv_spec(2), kv_spec(3), kv_spec(4), kv_spec(5),
                  head_spec(1), head_spec(2), head_spec(3),
                  pl.BlockSpec((1, tq, LANES), lambda b, g, t: (b, t, 4 * CB // LANES))],
        out_specs=head_spec(0),
        out_shape=jax.ShapeDtypeStruct((B, T, CB), BF16),
        scratch_shapes=[pltpu.VMEM((T // SEG, HG * tq, SEG), F32)],
        compiler_params=_cparams(("parallel", "parallel", "parallel")),
        name="nsa_prompt",
    )(slopes, proj, kvc2, kvc2, rows, rows, rows, rows, proj, proj, proj, proj)


def _nsa_selwin_sample_kernel(pt_ref, slopes_ref, q_ref, selm_ref, selst_ref, *refs, PGS, PS, HG, TQ, pos0, n_new, n_win):
    del pt_ref
    page_refs = refs[:PGS]
    (new_ref, cwin_ref, ocmp_ref, zc_ref, zs_ref, zw_ref, gate_ref, o_ref,
     m_scr, l_scr, acc_scr) = refs[PGS:]
    st = pl.program_id(1)
    NSBp = selm_ref.shape[3]
    GW = G_KV * HEAD_B
    sel_shift = int(math.log2(L_SEL))

    @pl.when(st == 0)
    def _():
        m_scr[...] = jnp.full(m_scr.shape, NEG, F32)
        l_scr[...] = jnp.zeros(l_scr.shape, F32)
        acc_scr[...] = jnp.zeros(acc_scr.shape, F32)

    qpos = pos0 + lax.broadcasted_iota(jnp.int32, (TQ, 1), 0)
    lane_j = lax.broadcasted_iota(jnp.int32, (TQ, NSBp), 1)
    tile_heads = lambda x: jnp.concatenate([x] * HG, axis=0)

    def sel_col(selm_g, j):
        return jnp.sum(jnp.where(lane_j == j, selm_g, 0.0), axis=1, keepdims=True)

    def online_update(s, mask, v):
        gs = range(G_KV)
        m_old = [m_scr[g] for g in gs]
        m_new = [jnp.maximum(m_old[g], jnp.max(jnp.where(mask[g], s[g], NEG), axis=-1, keepdims=True)) for g in gs]
        e = [jnp.where(mask[g], jnp.exp(s[g] - m_new[g]), 0.0) for g in gs]
        alpha = [jnp.exp(m_old[g] - m_new[g]) for g in gs]
        pv = [_bdot(e[g], v[g]) for g in gs]
        for g in gs:
            l_scr[g] = alpha[g] * l_scr[g] + jnp.sum(e[g], axis=-1, keepdims=True)
            acc_scr[g] = alpha[g] * acc_scr[g] + pv[g]
            m_scr[g] = m_new[g]

    NK = PGS * PS
    kpos = st * NK + lax.broadcasted_iota(jnp.int32, (1, NK), 1)
    expand = (jnp.right_shift(lax.broadcasted_iota(jnp.int32, (LANES, NK), 1), sel_shift)
              == lax.broadcasted_iota(jnp.int32, (LANES, NK), 0)).astype(BF16)
    in_blk_all = jnp.dot(selst_ref[0, 0].astype(BF16), expand, preferred_element_type=F32)
    dist = qpos - kpos
    distf = tile_heads(dist.astype(F32))
    gs = range(G_KV)
    q_st = [_stack_heads(q_ref[0, :, g * HG * HEAD_B:(g + 1) * HG * HEAD_B] * (HEAD_B ** -0.5), HG).astype(BF16)
            for g in gs]
    slope_col = [jnp.concatenate([jnp.full((TQ, 1), slopes_ref[g * HG + h], F32) for h in range(HG)], axis=0)
                 for g in gs]
    slabs = [jnp.swapaxes(page_refs[i][0].reshape(PS, 2 * G_KV, HEAD_B), 0, 1) for i in range(PGS)]
    k = [jnp.concatenate([slabs[i][g] for i in range(PGS)], axis=0) for g in gs]
    v = [jnp.concatenate([slabs[i][G_KV + g] for i in range(PGS)], axis=0) for g in gs]
    mask = [tile_heads((in_blk_all[g * TQ:(g + 1) * TQ] > 0.5) & (dist >= 0)) for g in gs]
    s = [_bdot_nt(q_st[g], k[g]) - slope_col[g] * distf for g in gs]
    online_update(s, mask, v)

    @pl.when(st == pl.num_programs(1) - 1)
    def _():
        NN = new_ref.shape[1]
        rnew = lax.broadcasted_iota(jnp.int32, (1, NN), 1)
        kpos_n = pos0 + rnew
        dist_n = qpos - kpos_n
        ok_n = (rnew < n_new) & (dist_n >= 0)
        jn = pos0 >> sel_shift
        kpos_w = pos0 - n_win + lax.broadcasted_iota(jnp.int32, (1, n_win), 1)
        dist_w = qpos - kpos_w
        dist_wall = jnp.concatenate([dist_w, dist_n], axis=1)
        mask_wall = jnp.concatenate([(dist_w >= 0) & (dist_w < WINDOW), ok_n & (dist_n < WINDOW)], axis=1)
        kn = [new_ref[0, :, 2 * GW + g * HEAD_B:2 * GW + (g + 1) * HEAD_B] for g in gs]
        vn = [new_ref[0, :, 3 * GW + g * HEAD_B:3 * GW + (g + 1) * HEAD_B] for g in gs]
        mask_n = [tile_heads((sel_col(selm_ref[0, g], jn) > 0.5) & ok_n) for g in gs]
        dist_nf = tile_heads(dist_n.astype(F32))
        s_n = [_bdot_nt(q_st[g], kn[g]) - slope_col[g] * dist_nf for g in gs]
        online_update(s_n, mask_n, vn)
        kw = [jnp.concatenate([cwin_ref[0, :, g * HEAD_B:(g + 1) * HEAD_B],
                               new_ref[0, :, 4 * GW + g * HEAD_B:4 * GW + (g + 1) * HEAD_B]], axis=0) for g in gs]
        vw = [jnp.concatenate([cwin_ref[0, :, GW + g * HEAD_B:GW + (g + 1) * HEAD_B],
                               new_ref[0, :, 5 * GW + g * HEAD_B:5 * GW + (g + 1) * HEAD_B]], axis=0) for g in gs]
        dist_wf = tile_heads(dist_wall.astype(F32))
        mask_w = tile_heads(mask_wall)
        sw = [_bdot_nt(q_st[g], kw[g]) - slope_col[g] * dist_wf for g in gs]
        pw = [_masked_softmax_rows(sw[g], mask_w) for g in gs]
        ow = [_bdot(pw[g], vw[g]) for g in gs]
        HB = G_KV * HG
        gates = _sigmoid(gate_ref[0])
        for g in gs:
            l = l_scr[g]
            o = acc_scr[g] / jnp.where(l > 0.0, l, 1.0)
            for h in range(HG):
                hd = g * HG + h
                hs = slice(hd * HEAD_B, (hd + 1) * HEAD_B)
                rs = slice(h * TQ, (h + 1) * TQ)
                y = (gates[:, hd:hd + 1] * ocmp_ref[0, :, hs] * _silu(zc_ref[0, :, hs])
                     + gates[:, HB + hd:HB + hd + 1] * o[rs] * _silu(zs_ref[0, :, hs])
                     + gates[:, 2 * HB + hd:2 * HB + hd + 1] * ow[g][rs] * _silu(zw_ref[0, :, hs]))
                o_ref[0, :, hs] = y.astype(o_ref.dtype)


def nsa_selwin_sample(proj, selm, o_cmp, pool5, table, new_rows, cwin, slopes, *, pos0, n_new):
    B, TQ, _ = proj.shape
    HG = slopes.shape[0] // G_KV
    NP, CPP = pool5.shape[:2]
    PS = CPP * S_CMP
    n_pages = table.shape[1]
    NSBp = selm.shape[3]
    PGS = max(d for d in (8, 4, 2, 1) if n_pages % d == 0)
    CB = G_KV * HG * HEAD_B
    GW = G_KV * HEAD_B
    NN = new_rows.shape[1]
    n_win = cwin.shape[1]
    assert pos0 % L_SEL == 0 and n_new <= L_SEL and pos0 == n_pages * PS

    def page_map(i):
        return lambda b, s, pt: (pt[b, s * PGS + i], 0, 0, 1, 0)

    n_steps = n_pages // PGS
    bps = PGS * PS // L_SEL
    assert bps <= LANES
    selst = selm[:, :, :, :n_steps * bps].reshape(B, G_KV, TQ, n_steps, bps).transpose(0, 3, 1, 2, 4)
    selst = jnp.pad(selst.reshape(B, n_steps, G_KV * TQ, bps), ((0, 0), (0, 0), (0, 0), (0, LANES - bps)))

    const = lambda b, s, pt: (b, 0, 0)
    wide = lambda blk: pl.BlockSpec((1, TQ, CB), lambda b, s, pt: (b, 0, blk))
    grid_spec = pltpu.PrefetchScalarGridSpec(
        num_scalar_prefetch=1,
        grid=(B, n_steps),
        in_specs=[pl.BlockSpec(memory_space=pltpu.SMEM),
                  pl.BlockSpec((1, TQ, CB), const),
                  pl.BlockSpec((1, G_KV, TQ, NSBp), lambda b, s, pt: (b, 0, 0, 0)),
                  pl.BlockSpec((1, 1, G_KV * TQ, LANES), lambda b, s, pt: (b, s, 0, 0))]
                 + [pl.BlockSpec((1, CPP, S_CMP, 2 * G_KV, HEAD_B), page_map(i)) for i in range(PGS)]
                 + [pl.BlockSpec((1, NN, 6 * GW), const),
                    pl.BlockSpec((1, n_win, 2 * GW), const),
                    wide(0), wide(1), wide(2), wide(3),
                    pl.BlockSpec((1, TQ, LANES), lambda b, s, pt: (b, 0, 4 * CB // LANES))],
        out_specs=wide(0),
        scratch_shapes=[pltpu.VMEM((G_KV, HG * TQ, 1), F32),
                        pltpu.VMEM((G_KV, HG * TQ, 1), F32),
                        pltpu.VMEM((G_KV, HG * TQ, HEAD_B), F32)],
    )
    return pl.pallas_call(
        functools.partial(_nsa_selwin_sample_kernel, PGS=PGS, PS=PS, HG=HG, TQ=TQ, pos0=pos0,
                          n_new=n_new, n_win=n_win),
        grid_spec=grid_spec,
        out_shape=jax.ShapeDtypeStruct((B, TQ, CB), F32),
        compiler_params=_cparams(("parallel", "arbitrary")),
        name="nsa_selwin_sample",
    )(table, slopes, proj, selm, selst, *([pool5] * PGS), new_rows, cwin, o_cmp, proj, proj, proj, proj)


def _rwkv_layer(h, x_prev, s0, i, W, B, T):
    N, D = h.shape
    g = W["norm_g"][i]
    rkvg = rwkv_in(h, x_prev, g, W["mu_a"][i], W["w_in_a"], i, T)
    CA = rkvg.shape[-1]
    lw, a = rwkv_lora(h, x_prev, g, W["mu_a"][i], W["w_lora_w1"][i], W["w_lora_w2"][i], W["a_lora1"][i],
                      W["a_lora2"][i], W["w0_a"][i], W["a0_a"][i], T)
    pvec = jnp.stack([W["k_k"][i], W["k_a"][i], W["r_k"][i].reshape(CA), W["ln_x_w"][i], W["ln_x_b"][i]])
    o, s_fin = rwkv_scan(rkvg.reshape(4, B, T, CA), lw.reshape(B, T, CA), a.reshape(B, T, CA), pvec, s0)
    last = rmsnorm(h.reshape(B, T, D)[:, -1], g)
    return o.reshape(N, CA), s_fin, last


def _nsa_layer(h, jb, shared, W, slopes, B, T, norm_g):
    N, D = h.shape
    CB = W["w_out_b"].shape[1]
    proj3 = norm_mm(h, norm_g, W["w_in_b"], (jb,)).reshape(B, T, -1)
    if shared["past"] is None:
        o = nsa_prompt(proj3, shared["rows"], shared["kvc"], slopes, tq=min(T, ATTN_TQ),
                       nc=shared["nc"], nsb=shared["nsb"])
    else:
        TQ = SUBLANES
        projp = jnp.pad(proj3, ((0, 0), (0, TQ - T), (0, 0)))
        o_cmp, selm = nsa_cmp(projp, shared["kvc"], slopes, tq=TQ, nc=shared["nc"],
                              nsb=shared["nsb"], pos0=shared["pos0"])
        pool, table, cwin = shared["past"]
        o = nsa_selwin_sample(projp, selm, o_cmp, pool, table, shared["new_rows"], cwin, slopes,
                              pos0=shared["pos0"], n_new=T)[:, :T]
    return o.reshape(N, CB)


def _trunk(x, p, pos0, wkv0, shift0, past, W, slopes):
    B, T, D = x.shape
    N = B * T
    depth = p.shape[0]
    n_a = W["w_in_a"].shape[0]
    GW = G_KV * HEAD_B
    h = x.reshape(N, D)
    wkv_new, shift_new = [], []
    shared, kv_rows, win_state = None, None, None
    for i in range(depth):
        if i < n_a:
            o, s_fin, last = _rwkv_layer(h, shift0[i], wkv0[i], i, W, B, T)
            wkv_new.append(s_fin)
            shift_new.append(last)
            h = out_ple(o, h, p[i].reshape(N, -1), W["w_out_a"], i, W["w_ple"], W["w_ple_gate"], i)
        else:
            o = _nsa_layer(h, i - n_a, shared, W, slopes, B, T, W["norm_g"][i])
            h = out_ple(o, h, p[i].reshape(N, -1), W["w_out_b"], i - n_a, W["w_ple"], W["w_ple_gate"], i)
        if i == n_a - 1:
            rows = norm_mm(h, W["kv_norm_g"], W["w_kv"]).reshape(B, T, 6 * GW)
            kv_rows = rows[:, :, :4 * GW].reshape(B, T, 4, G_KV, HEAD_B)
            win_new = rows[:, :, 4 * GW:].reshape(B, T, 2, G_KV, HEAD_B)
            if past is None:
                PS = 128
                pool = rows.reshape(B * T // PS, PS // S_CMP, S_CMP, 6 * G_KV, HEAD_B)
                table = jnp.arange(B * T // PS, dtype=jnp.int32).reshape(B, T // PS)
                t_all = T
                win_all = win_new
                shared = {"past": None, "rows": rows}
            else:
                pool, table, cwin = past
                PS = pool.shape[1] * S_CMP
                t_all = pos0 + T
                win_all = jnp.concatenate([cwin.reshape(B, -1, 2, G_KV, HEAD_B), win_new], axis=1)
                NN = LANES
                shared = {"past": past, "new_rows": jnp.pad(rows, ((0, 0), (0, NN - T), (0, 0)))}
            win_state = win_all[:, win_all.shape[1] - min(WINDOW, pos0 + T):]
            nc = (t_all - L_CMP) // S_CMP + 1
            assert nc < table.shape[1] * PS // S_CMP
            kvc = compress_kv(pool, table, W["pe_cmp"], W["w_cmp1"], W["w_cmp2"])
            shared.update(kvc=kvc, nc=nc, nsb=max(-(-t_all // L_SEL), TOPK_SEL), pos0=pos0)
    y = rmsnorm(h, W["final_norm_g"]).reshape(B, T, D)
    return y, jnp.stack(wkv_new), jnp.stack(shift_new), kv_rows, win_state


def kernel(x_prompt, x_sample, state_wkv, state_shift, cache_kv, cache_win_kv, page_table, p_prompt, p_sample, norm_g, mu_a, w_in_a, w_lora_w1, w_lora_w2, w0_a, a_lora1, a_lora2, a0_a, k_k, k_a, r_k, ln_x_w, ln_x_b, w_out_a, w_in_b, w_out_b, kv_norm_g, w_kv, pe_cmp, w_cmp1, w_cmp2, w_ple, w_ple_gate, final_norm_g):
    bf = lambda w: w.astype(BF16)
    CA = w_out_a.shape[1]
    W = dict(norm_g=norm_g, mu_a=mu_a, w_in_a=bf(w_in_a), w_lora_w1=bf(w_lora_w1), w_lora_w2=bf(w_lora_w2),
             w0_a=w0_a, a_lora1=bf(a_lora1), a_lora2=bf(a_lora2), a0_a=a0_a, k_k=k_k, k_a=k_a,
             r_k=r_k, ln_x_w=ln_x_w, ln_x_b=ln_x_b,
             w_out_a=bf(w_out_a), w_in_b=bf(w_in_b), w_out_b=bf(w_out_b), kv_norm_g=kv_norm_g, w_kv=bf(w_kv),
             pe_cmp=pe_cmp, w_cmp1=bf(w_cmp1), w_cmp2=bf(w_cmp2), w_ple=bf(w_ple), w_ple_gate=bf(w_ple_gate),
             final_norm_g=final_norm_g)
    HB = w_out_b.shape[1] // HEAD_B
    slopes = 2.0 ** (-8.0 * jnp.arange(1, HB + 1, dtype=F32) / HB)
    bp = x_prompt.shape[0]
    n_a = w_in_a.shape[0]
    D = x_prompt.shape[-1]
    wkv0 = jnp.zeros((n_a, bp, CA // HEAD_A, HEAD_A, HEAD_A), F32)
    shift0 = jnp.zeros((n_a, bp, D), F32)
    y_p, wkv_p, shift_p, kv_p, win_p = _trunk(x_prompt, p_prompt, 0, wkv0, shift0, None, W, slopes)
    db, n_pages = page_table.shape
    NP, PS = cache_kv.shape[:2]
    pool5 = cache_kv.reshape(NP, PS // S_CMP, S_CMP, -1, HEAD_B)
    past = (pool5, page_table, cache_win_kv.reshape(db, cache_win_kv.shape[1], -1))
    y_s, wkv_s, shift_s, kv_s, win_s = _trunk(x_sample, p_sample, n_pages * PS, state_wkv, state_shift, past, W, slopes)
    return (y_p, y_s, wkv_p, shift_p, kv_p, win_p, wkv_s, shift_s, kv_s, win_s)
```
